```python
import math
import jax
import jax.numpy as jnp
from jax import lax
import numpy as np

D_MODEL = 1024
BATCH = 8
SEQ = 8192
DEPTH = 4

N_MIXERS = 4
N_MEM = 256
EPS = 1e-6
ROPE_THETA = 10000.0
POS_OFFSET_MAX = 4096
CHUNK = 64
D_MIX = 768
XA_HEADS = 4
XA_HEAD_DIM = 64
D_XA = XA_HEADS * XA_HEAD_DIM
GLA_HEADS = 4
GLA_DK = D_MIX // 2 // GLA_HEADS
GLA_DV = D_MIX // GLA_HEADS
GLA_RANK = 16
GLA_GATE_NORM = 16.0
DIL_GROUPS = ((128, 1), (512, 4), (2048, 16))
DIL_HEADS = 4
DIL_HEAD_DIM = 128
DIL_BLOCK = 128
D_DIL = DIL_HEADS * DIL_HEAD_DIM
SSM_HEAD_DIM = 64
SSM_HEADS = D_MIX // SSM_HEAD_DIM
SSM_GROUPS = 2
SSM_STATE = 128
SSM_CONV = 4
HGRN_EXPAND = 128
HGRN_HEADS = D_MIX // HGRN_EXPAND
HGRN_DK = HGRN_EXPAND
HGRN_DV = D_MIX // HGRN_HEADS
D_FF = 2816
FFN_CONV = 3

kernel_name = "hybrid_interleaved_gla_dilated_ssd_hgrn2_block"

F32 = jnp.float32


def rms_norm(x, g):
    xf = x.astype(F32)
    y = xf * lax.rsqrt(jnp.mean(xf * xf, axis=-1, keepdims=True) + EPS)
    return (y * g.astype(F32)).astype(x.dtype)


def split_cols(y, sizes):
    return jnp.split(y, [int(c) for c in np.cumsum(sizes)[:-1]], axis=-1)


def causal_dwconv(x, w, b):
    K, S = w.shape[0], x.shape[1]
    xp = jnp.pad(x, ((0, 0), (K - 1, 0), (0, 0)))
    y = b
    for j in range(K):
        y = y + xp[:, j:j + S] * w[j]
    return y


def rope_tables(positions, dim):
    half = dim // 2
    inv_freq = ROPE_THETA ** (-jnp.arange(half, dtype=F32) / half)
    ang = positions.astype(F32)[..., None] * inv_freq
    return jnp.cos(ang)[:, :, None], jnp.sin(ang)[:, :, None]


def apply_rope(x, cos, sin):
    half = x.shape[-1] // 2
    x1, x2 = x[..., :half].astype(F32), x[..., half:].astype(F32)
    return jnp.concatenate([x1 * cos - x2 * sin, x2 * cos + x1 * sin], axis=-1).astype(x.dtype)


def chunked_gla(q, k, v, log_a):
    Bsz, S, H, K = q.shape
    V = v.shape[-1]
    C = CHUNK
    n = S // C
    q, k, v, log_a = (t.astype(F32).reshape(Bsz, n, C, H, -1) for t in (q, k, v, log_a))
    b = jnp.cumsum(log_a, axis=2)
    b_last = b[:, :, -1:]
    b_ref = b[:, :, C // 2 - 1:C // 2]
    att = jnp.einsum('bnihk,bnjhk->bnhij', q * jnp.exp(b - b_ref), k * jnp.exp(b_ref - b))
    att = jnp.where(jnp.tril(jnp.ones((C, C), bool)), att, 0.0)
    o_intra = jnp.einsum('bnhij,bnjhv->bnihv', att, v)
    q_inter = q * jnp.exp(b)
    k_state = k * jnp.exp(b_last - b)
    decay = jnp.exp(b_last[:, :, 0])

    def step(state, xs):
        qc, kc, vc, dc = xs
        o = jnp.einsum('bchk,bhkv->bchv', qc, state)
        state = dc[..., None] * state + jnp.einsum('bchk,bchv->bhkv', kc, vc)
        return state, o

    tm = lambda t: jnp.moveaxis(t, 1, 0)
    _, o_inter = lax.scan(step, jnp.zeros((Bsz, H, K, V), F32),
                          (tm(q_inter), tm(k_state), tm(v), tm(decay)))
    return (o_intra + jnp.moveaxis(o_inter, 0, 1)).reshape(Bsz, S, H, V)


def ssd_chunked(x, dt, A, Bm, Cm):
    Bsz, S, G, Hg, P = x.shape
    N = Bm.shape[-1]
    C = CHUNK
    n = S // C
    a = (dt * A).reshape(Bsz, n, C, G, Hg)
    xdt = (x * dt[..., None]).reshape(Bsz, n, C, G, Hg, P)
    Bc = Bm.astype(F32).reshape(Bsz, n, C, G, N)
    Cc = Cm.astype(F32).reshape(Bsz, n, C, G, N)
    acs = jnp.cumsum(a, axis=2)
    seg = acs[:, :, :, None] - acs[:, :, None]
    causal = jnp.tril(jnp.ones((C, C), bool))[:, :, None, None]
    Lmat = jnp.exp(jnp.where(causal, seg, -jnp.inf))
    cb = jnp.einsum('bnlgk,bnsgk->bnlsg', Cc, Bc)
    y_diag = jnp.einsum('bnlsgh,bnsghp->bnlghp', cb[..., None] * Lmat, xdt)
    x_end = xdt * jnp.exp(acs[:, :, -1:] - acs)[..., None]
    in_decay = jnp.exp(acs)
    chunk_decay = jnp.exp(acs[:, :, -1])

    def step(hst, xs):
        c_c, b_c, xe_c, ind_c, cd_c = xs
        y_off = jnp.einsum('blgk,bghpk->blghp', c_c, hst) * ind_c[..., None]
        hst = cd_c[..., None, None] * hst + jnp.einsum('bsgk,bsghp->bghpk', b_c, xe_c)
        return hst, y_off

    tm = lambda t: jnp.moveaxis(t, 1, 0)
    _, y_off = lax.scan(step, jnp.zeros((Bsz, G, Hg, P, N), F32),
                        (tm(Cc), tm(Bc), tm(x_end), tm(in_decay), tm(chunk_decay)))
    return (y_diag + jnp.moveaxis(y_off, 0, 1)).reshape(Bsz, S, G, Hg, P)


def dilated_window_attention(q, k, v, window, dilation):
    Bsz, S, H, hd = q.shape
    r = dilation
    W = window // dilation
    Q = DIL_BLOCK
    L = S // r
    nb = -(-L // Q)
    Lp = nb * Q

    def to_blocks(t):
        t = t.reshape(Bsz, L, r, H, hd).transpose(0, 2, 1, 3, 4)
        t = jnp.pad(t, ((0, 0), (0, 0), (0, Lp - L), (0, 0), (0, 0)))
        return t.reshape(Bsz, r, nb, Q, H, hd)

    def with_prev(t):
        prev = jnp.pad(t, ((0, 0), (0, 0), (1, 0), (0, 0), (0, 0), (0, 0)))[:, :, :-1]
        return jnp.concatenate([prev, t], axis=3)

    qb = to_blocks(q)
    kk = with_prev(to_blocks(k))
    vv = with_prev(to_blocks(v))
    i = jnp.arange(Q)[:, None]
    j = jnp.arange(2 * Q)[None, :]
    dist = Q + i - j
    band = (dist >= 0) & (dist <= W)
    valid = (jnp.arange(nb)[:, None, None] > 0) | (j >= Q)[None]
    mask = band[None] & valid
    s = jnp.einsum('brnqhd,brnkhd->brnhqk', qb, kk).astype(F32) * (hd ** -0.5)
    s = jnp.where(mask[None, None, :, None], s, -jnp.inf)
    m = jnp.max(s, axis=-1, keepdims=True)
    p = jnp.exp(s - m)
    l = jnp.sum(p, axis=-1, keepdims=True)
    o = jnp.einsum('brnhqk,brnkhd->brnqhd', p / l, vv.astype(F32))
    lse = (m + jnp.log(l))[..., 0].transpose(0, 1, 2, 4, 3)
    o = o.reshape(Bsz, r, Lp, H, hd)[:, :, :L].transpose(0, 2, 1, 3, 4).reshape(Bsz, S, H, hd)
    lse = lse.reshape(Bsz, r, Lp, H)[:, :, :L].transpose(0, 2, 1, 3).reshape(Bsz, S, H)
    return o, lse


def gla_mixer(h, w_in, w_gate2, b_gate, o_norm):
    Bsz, S, _ = h.shape
    q, k, v, glr, og, xq = split_cols(
        h @ w_in, [GLA_HEADS * GLA_DK, GLA_HEADS * GLA_DK, D_MIX, GLA_RANK, D_MIX, D_XA])
    log_a = jax.nn.log_sigmoid((glr @ w_gate2 + b_gate).astype(F32)) / GLA_GATE_NORM
    hs = lambda t: t.reshape(Bsz, S, GLA_HEADS, -1)
    o = chunked_gla(hs(q) * (GLA_DK ** -0.5), hs(k), hs(v), hs(log_a))
    o = rms_norm(o, o_norm).reshape(Bsz, S, D_MIX) * jax.nn.silu(og.astype(F32))
    return o.astype(h.dtype), xq


def dilated_mixer(h, w_in, q_norm, k_norm, positions):
    Bsz, S, _ = h.shape
    nh = len(DIL_GROUPS) * DIL_HEADS
    q, k, v, xq = split_cols(h @ w_in, [nh * DIL_HEAD_DIM] * 3 + [D_XA])
    hs = lambda t: t.reshape(Bsz, S, nh, DIL_HEAD_DIM)
    cos, sin = rope_tables(positions, DIL_HEAD_DIM)
    q = apply_rope(rms_norm(hs(q), q_norm), cos, sin)
    k = apply_rope(rms_norm(hs(k), k_norm), cos, sin)
    v = hs(v)
    outs, lses = [], []
    for g, (window, dilation) in enumerate(DIL_GROUPS):
        sl = slice(g * DIL_HEADS, (g + 1) * DIL_HEADS)
        o, lse = dilated_window_attention(q[:, :, sl], k[:, :, sl], v[:, :, sl], window, dilation)
        outs.append(o)
        lses.append(lse)
    wts = jax.nn.softmax(jnp.stack(lses), axis=0)
    o = jnp.sum(wts[..., None] * jnp.stack(outs), axis=0)
    return o.reshape(Bsz, S, D_DIL).astype(h.dtype), xq


def mamba2_mixer(h, w_in, conv_w, conv_b, dt_bias, a_log, d_skip, norm_g):
    Bsz, S, _ = h.shape
    GN = SSM_GROUPS * SSM_STATE
    hg = SSM_HEADS // SSM_GROUPS
    z, xbc, dt, xq = split_cols(h @ w_in, [D_MIX, D_MIX + 2 * GN, SSM_HEADS, D_XA])
    xbc = jax.nn.silu(causal_dwconv(xbc, conv_w, conv_b))
    xs, bm, cm = split_cols(xbc, [D_MIX, GN, GN])
    dt = jax.nn.softplus(dt.astype(F32) + dt_bias.astype(F32)).reshape(Bsz, S, SSM_GROUPS, hg)
    A = -jnp.exp(a_log.astype(F32)).reshape(SSM_GROUPS, hg)
    xs = xs.astype(F32).reshape(Bsz, S, SSM_GROUPS, hg, SSM_HEAD_DIM)
    y = ssd_chunked(xs, dt, A, bm.reshape(Bsz, S, SSM_GROUPS, SSM_STATE),
                    cm.reshape(Bsz, S, SSM_GROUPS, SSM_STATE))
    y = y + d_skip.astype(F32).reshape(SSM_GROUPS, hg)[..., None] * xs
    y = y.reshape(Bsz, S, D_MIX) * jax.nn.silu(z.astype(F32))
    y = rms_norm(y.reshape(Bsz, S, SSM_GROUPS, -1), norm_g.reshape(SSM_GROUPS, -1))
    return y.reshape(Bsz, S, D_MIX).astype(h.dtype), xq


def hgrn2_mixer(h, w_in, lower_bounds, o_norm, layer):
    Bsz, S, _ = h.shape
    q, f, i, og, xq = split_cols(
        h @ w_in, [HGRN_HEADS * HGRN_DK, HGRN_HEADS * HGRN_DK, D_MIX, D_MIX, D_XA])
    lbs = jnp.cumsum(jax.nn.softmax(lower_bounds.astype(F32), axis=0), axis=0)
    lb = lbs[layer] - lbs[0]
    fg = lb + (1.0 - lb) * jax.nn.sigmoid(f.astype(F32))
    hk = lambda t: t.reshape(Bsz, S, HGRN_HEADS, HGRN_DK)
    o = chunked_gla(hk(jax.nn.silu(q.astype(F32))), hk(1.0 - fg),
                    i.reshape(Bsz, S, HGRN_HEADS, HGRN_DV), hk(jnp.log(fg)))
    o = rms_norm(o, o_norm).reshape(Bsz, S, D_MIX) * jax.nn.sigmoid(og.astype(F32))
    return o.astype(h.dtype), xq


def memory_cross_attention(xq, mem_n, w_kv, q_norm, k_norm):
    Bsz, S, _ = xq.shape
    M = mem_n.shape[1]
    k, v = jnp.split(mem_n @ w_kv, 2, axis=-1)
    q = rms_norm(xq.reshape(Bsz, S, XA_HEADS, XA_HEAD_DIM), q_norm).astype(F32)
    k = rms_norm(k.reshape(Bsz, M, XA_HEADS, XA_HEAD_DIM), k_norm).astype(F32)
    v = v.reshape(Bsz, M, XA_HEADS, XA_HEAD_DIM).astype(F32)
    p = jax.nn.softmax(jnp.einsum('bshd,bmhd->bhsm', q, k) * (XA_HEAD_DIM ** -0.5), axis=-1)
    o = jnp.einsum('bhsm,bmhd->bshd', p, v)
    return o.reshape(Bsz, S, D_XA).astype(xq.dtype)


def conv_ffn(h, w_up, conv_w, conv_b, w_down):
    u = causal_dwconv(h @ w_up, conv_w, conv_b)
    gate, val = jnp.split(u, 2, axis=-1)
    return (jax.nn.silu(gate) * val) @ w_down


def _fwd_setup_inputs(seed: int = 0) -> dict:
    key = jax.random.key(seed)
    ks = iter(jax.random.split(key, 40))

    def nrm(shape, scale=1.0):
        return jax.random.normal(next(ks), shape, F32) * scale

    def gain(shape):
        return 1.0 + nrm(shape, 0.02)

    D, F = D_MODEL, D_FF
    GN = SSM_GROUPS * SSM_STATE
    a_cols = 2 * GLA_HEADS * GLA_DK + 2 * D_MIX + GLA_RANK + D_XA
    b_cols = 3 * len(DIL_GROUPS) * D_DIL + D_XA
    c_cols = 2 * D_MIX + 2 * GN + SSM_HEADS + D_XA
    d_cols = 2 * HGRN_HEADS * HGRN_DK + 2 * D_MIX + D_XA
    a_out_in, b_out_in = D_MIX + D_XA, D_DIL + D_XA
    c_out_in, d_out_in = D_MIX + D_XA, D_MIX + D_XA
    inp = {}
    inp['x'] = nrm((BATCH, SEQ, D))
    inp['mem'] = nrm((BATCH, N_MEM, D))
    inp['positions'] = (jnp.arange(SEQ, dtype=jnp.int32)[None, :]
                        + jax.random.randint(next(ks), (BATCH, 1), 0, POS_OFFSET_MAX, dtype=jnp.int32))
    inp['mem_norm'] = gain((D,))
    inp['mix_norm'] = gain((DEPTH, D))
    inp['xa_w_kv'] = nrm((DEPTH, D, 2 * D_XA), D ** -0.5)
    inp['xa_q_norm'] = gain((DEPTH, XA_HEAD_DIM))
    inp['xa_k_norm'] = gain((DEPTH, XA_HEAD_DIM))
    inp['ffn_norm'] = gain((DEPTH, D))
    inp['ffn_w_up'] = nrm((DEPTH, D, 2 * F), D ** -0.5)
    inp['ffn_conv_w'] = nrm((DEPTH, FFN_CONV, 2 * F), FFN_CONV ** -0.5)
    inp['ffn_conv_b'] = nrm((DEPTH, 2 * F), 0.02)
    inp['ffn_w_down'] = nrm((DEPTH, F, D), 0.5 * F ** -0.5)
    inp['a_w_in'] = nrm((D, a_cols), D ** -0.5)
    inp['a_w_gate2'] = nrm((GLA_RANK, GLA_HEADS * GLA_DK), GLA_RANK ** -0.5)
    inp['a_b_gate'] = nrm((GLA_HEADS * GLA_DK,), 0.1)
    inp['a_o_norm'] = gain((GLA_DV,))
    inp['a_w_out'] = nrm((a_out_in, D), 0.5 * a_out_in ** -0.5)
    inp['b_w_in'] = nrm((D, b_cols), D ** -0.5)
    inp['b_q_norm'] = gain((DIL_HEAD_DIM,))
    inp['b_k_norm'] = gain((DIL_HEAD_DIM,))
    inp['b_w_out'] = nrm((b_out_in, D), 0.5 * b_out_in ** -0.5)
    inp['c_w_in'] = nrm((D, c_cols), D ** -0.5)
    inp['c_conv_w'] = nrm((SSM_CONV, D_MIX + 2 * GN), 0.5)
    inp['c_conv_b'] = nrm((D_MIX + 2 * GN,), 0.02)
    dt0 = jnp.exp(jax.random.uniform(next(ks), (SSM_HEADS,), F32, math.log(1e-3), math.log(1e-1)))
    inp['c_dt_bias'] = dt0 + jnp.log(-jnp.expm1(-dt0))
    inp['c_a_log'] = jnp.log(jax.random.uniform(next(ks), (SSM_HEADS,), F32, 1.0, 16.0))
    inp['c_d'] = gain((SSM_HEADS,))
    inp['c_norm'] = gain((D_MIX,))
    inp['c_w_out'] = nrm((c_out_in, D), 0.5 * c_out_in ** -0.5)
    inp['d_w_in'] = nrm((D, d_cols), D ** -0.5)
    inp['d_lower_bounds'] = nrm((DEPTH, HGRN_HEADS * HGRN_DK), 0.02)
    inp['d_o_norm'] = gain((HGRN_DV,))
    inp['d_w_out'] = nrm((d_out_in, D), 0.5 * d_out_in ** -0.5)
    return inp


def _fwd_reference(x, mem, positions, mem_norm, mix_norm, xa_w_kv, xa_q_norm, xa_k_norm,
              ffn_norm, ffn_w_up, ffn_conv_w, ffn_conv_b, ffn_w_down,
              a_w_in, a_w_gate2, a_b_gate, a_o_norm, a_w_out,
              b_w_in, b_q_norm, b_k_norm, b_w_out,
              c_w_in, c_conv_w, c_conv_b, c_dt_bias, c_a_log, c_d, c_norm, c_w_out,
              d_w_in, d_lower_bounds, d_o_norm, d_w_out):
    mem_n = rms_norm(mem, mem_norm)
    for i in range(DEPTH):
        h = rms_norm(x, mix_norm[i])
        kind = i % N_MIXERS
        if kind == 0:
            tok, xq = gla_mixer(h, a_w_in, a_w_gate2, a_b_gate, a_o_norm)
            w_out = a_w_out
        elif kind == 1:
            tok, xq = dilated_mixer(h, b_w_in, b_q_norm, b_k_norm, positions)
            w_out = b_w_out
        elif kind == 2:
            tok, xq = mamba2_mixer(h, c_w_in, c_conv_w, c_conv_b, c_dt_bias, c_a_log, c_d, c_norm)
            w_out = c_w_out
        else:
            tok, xq = hgrn2_mixer(h, d_w_in, d_lower_bounds, d_o_norm, i)
            w_out = d_w_out
        xa = memory_cross_attention(xq, mem_n, xa_w_kv[i], xa_q_norm[i], xa_k_norm[i])
        x = x + jnp.concatenate([tok, xa], axis=-1) @ w_out
        x = x + conv_ffn(rms_norm(x, ffn_norm[i]), ffn_w_up[i], ffn_conv_w[i], ffn_conv_b[i],
                         ffn_w_down[i])
    return x


import jax as _jax
import jax.numpy as _jnp

TWIN_FORMAT = 'train_step'
FWD_PARAMS = ['x', 'mem', 'positions', 'mem_norm', 'mix_norm', 'xa_w_kv', 'xa_q_norm', 'xa_k_norm', 'ffn_norm', 'ffn_w_up', 'ffn_conv_w', 'ffn_conv_b', 'ffn_w_down', 'a_w_in', 'a_w_gate2', 'a_b_gate', 'a_o_norm', 'a_w_out', 'b_w_in', 'b_q_norm', 'b_k_norm', 'b_w_out', 'c_w_in', 'c_conv_w', 'c_conv_b', 'c_dt_bias', 'c_a_log', 'c_d', 'c_norm', 'c_w_out', 'd_w_in', 'd_lower_bounds', 'd_o_norm', 'd_w_out']
TWIN_WEIGHTS = ['mem_norm', 'mix_norm', 'xa_w_kv', 'xa_q_norm', 'xa_k_norm', 'ffn_norm', 'ffn_w_up', 'ffn_conv_w', 'ffn_conv_b', 'ffn_w_down', 'a_w_in', 'a_w_gate2', 'a_b_gate', 'a_o_norm', 'a_w_out', 'b_w_in', 'b_q_norm', 'b_k_norm', 'b_w_out', 'c_w_in', 'c_conv_w', 'c_conv_b', 'c_dt_bias', 'c_a_log', 'c_d', 'c_norm', 'c_w_out', 'd_w_in', 'd_lower_bounds', 'd_o_norm', 'd_w_out']
TWIN_DIFF_INPUT = 'x'
TWIN_INPUTS = ['x', 'mem', 'positions', 'mem_norm', 'mix_norm', 'xa_w_kv', 'xa_q_norm', 'xa_k_norm', 'ffn_norm', 'ffn_w_up', 'ffn_conv_w', 'ffn_conv_b', 'ffn_w_down', 'a_w_in', 'a_w_gate2', 'a_b_gate', 'a_o_norm', 'a_w_out', 'b_w_in', 'b_q_norm', 'b_k_norm', 'b_w_out', 'c_w_in', 'c_conv_w', 'c_conv_b', 'c_dt_bias', 'c_a_log', 'c_d', 'c_norm', 'c_w_out', 'd_w_in', 'd_lower_bounds', 'd_o_norm', 'd_w_out', 'loss_target', 'm_mem_norm', 'm_mix_norm', 'm_xa_w_kv', 'm_xa_q_norm', 'm_xa_k_norm', 'm_ffn_norm', 'm_ffn_w_up', 'm_ffn_conv_w', 'm_ffn_conv_b', 'm_ffn_w_down', 'm_a_w_in', 'm_a_w_gate2', 'm_a_b_gate', 'm_a_o_norm', 'm_a_w_out', 'm_b_w_in', 'm_b_q_norm', 'm_b_k_norm', 'm_b_w_out', 'm_c_w_in', 'm_c_conv_w', 'm_c_conv_b', 'm_c_dt_bias', 'm_c_a_log', 'm_c_d', 'm_c_norm', 'm_c_w_out', 'm_d_w_in', 'm_d_lower_bounds', 'm_d_o_norm', 'm_d_w_out', 'v_mem_norm', 'v_mix_norm', 'v_xa_w_kv', 'v_xa_q_norm', 'v_xa_k_norm', 'v_ffn_norm', 'v_ffn_w_up', 'v_ffn_conv_w', 'v_ffn_conv_b', 'v_ffn_w_down', 'v_a_w_in', 'v_a_w_gate2', 'v_a_b_gate', 'v_a_o_norm', 'v_a_w_out', 'v_b_w_in', 'v_b_q_norm', 'v_b_k_norm', 'v_b_w_out', 'v_c_w_in', 'v_c_conv_w', 'v_c_conv_b', 'v_c_dt_bias', 'v_c_a_log', 'v_c_d', 'v_c_norm', 'v_c_w_out', 'v_d_w_in', 'v_d_lower_bounds', 'v_d_o_norm', 'v_d_w_out']
TWIN_OUTPUTS = ['loss', 'grad_x', 'grad_mem_norm', 'grad_mix_norm', 'grad_xa_w_kv', 'grad_xa_q_norm', 'grad_xa_k_norm', 'grad_ffn_norm', 'grad_ffn_w_up', 'grad_ffn_conv_w', 'grad_ffn_conv_b', 'grad_ffn_w_down', 'grad_a_w_in', 'grad_a_w_gate2', 'grad_a_b_gate', 'grad_a_o_norm', 'grad_a_w_out', 'grad_b_w_in', 'grad_b_q_norm', 'grad_b_k_norm', 'grad_b_w_out', 'grad_c_w_in', 'grad_c_conv_w', 'grad_c_conv_b', 'grad_c_dt_bias', 'grad_c_a_log', 'grad_c_d', 'grad_c_norm', 'grad_c_w_out', 'grad_d_w_in', 'grad_d_lower_bounds', 'grad_d_o_norm', 'grad_d_w_out', 'delta_mem_norm', 'delta_mix_norm', 'delta_xa_w_kv', 'delta_xa_q_norm', 'delta_xa_k_norm', 'delta_ffn_norm', 'delta_ffn_w_up', 'delta_ffn_conv_w', 'delta_ffn_conv_b', 'delta_ffn_w_down', 'delta_a_w_in', 'delta_a_w_gate2', 'delta_a_b_gate', 'delta_a_o_norm', 'delta_a_w_out', 'delta_b_w_in', 'delta_b_q_norm', 'delta_b_k_norm', 'delta_b_w_out', 'delta_c_w_in', 'delta_c_conv_w', 'delta_c_conv_b', 'delta_c_dt_bias', 'delta_c_a_log', 'delta_c_d', 'delta_c_norm', 'delta_c_w_out', 'delta_d_w_in', 'delta_d_lower_bounds', 'delta_d_o_norm', 'delta_d_w_out', 'new_m_mem_norm', 'new_m_mix_norm', 'new_m_xa_w_kv', 'new_m_xa_q_norm', 'new_m_xa_k_norm', 'new_m_ffn_norm', 'new_m_ffn_w_up', 'new_m_ffn_conv_w', 'new_m_ffn_conv_b', 'new_m_ffn_w_down', 'new_m_a_w_in', 'new_m_a_w_gate2', 'new_m_a_b_gate', 'new_m_a_o_norm', 'new_m_a_w_out', 'new_m_b_w_in', 'new_m_b_q_norm', 'new_m_b_k_norm', 'new_m_b_w_out', 'new_m_c_w_in', 'new_m_c_conv_w', 'new_m_c_conv_b', 'new_m_c_dt_bias', 'new_m_c_a_log', 'new_m_c_d', 'new_m_c_norm', 'new_m_c_w_out', 'new_m_d_w_in', 'new_m_d_lower_bounds', 'new_m_d_o_norm', 'new_m_d_w_out', 'new_v_mem_norm', 'new_v_mix_norm', 'new_v_xa_w_kv', 'new_v_xa_q_norm', 'new_v_xa_k_norm', 'new_v_ffn_norm', 'new_v_ffn_w_up', 'new_v_ffn_conv_w', 'new_v_ffn_conv_b', 'new_v_ffn_w_down', 'new_v_a_w_in', 'new_v_a_w_gate2', 'new_v_a_b_gate', 'new_v_a_o_norm', 'new_v_a_w_out', 'new_v_b_w_in', 'new_v_b_q_norm', 'new_v_b_k_norm', 'new_v_b_w_out', 'new_v_c_w_in', 'new_v_c_conv_w', 'new_v_c_conv_b', 'new_v_c_dt_bias', 'new_v_c_a_log', 'new_v_c_d', 'new_v_c_norm', 'new_v_c_w_out', 'new_v_d_w_in', 'new_v_d_lower_bounds', 'new_v_d_o_norm', 'new_v_d_w_out']
TWIN_LEAF_KINDS = {'loss': 'loss', 'grad_x': 'grad_x', 'grad_mem_norm': 'grad_w', 'grad_mix_norm': 'grad_w', 'grad_xa_w_kv': 'grad_w', 'grad_xa_q_norm': 'grad_w', 'grad_xa_k_norm': 'grad_w', 'grad_ffn_norm': 'grad_w', 'grad_ffn_w_up': 'grad_w', 'grad_ffn_conv_w': 'grad_w', 'grad_ffn_conv_b': 'grad_w', 'grad_ffn_w_down': 'grad_w', 'grad_a_w_in': 'grad_w', 'grad_a_w_gate2': 'grad_w', 'grad_a_b_gate': 'grad_w', 'grad_a_o_norm': 'grad_w', 'grad_a_w_out': 'grad_w', 'grad_b_w_in': 'grad_w', 'grad_b_q_norm': 'grad_w', 'grad_b_k_norm': 'grad_w', 'grad_b_w_out': 'grad_w', 'grad_c_w_in': 'grad_w', 'grad_c_conv_w': 'grad_w', 'grad_c_conv_b': 'grad_w', 'grad_c_dt_bias': 'grad_w', 'grad_c_a_log': 'grad_w', 'grad_c_d': 'grad_w', 'grad_c_norm': 'grad_w', 'grad_c_w_out': 'grad_w', 'grad_d_w_in': 'grad_w', 'grad_d_lower_bounds': 'grad_w', 'grad_d_o_norm': 'grad_w', 'grad_d_w_out': 'grad_w', 'delta_mem_norm': 'delta_w', 'delta_mix_norm': 'delta_w', 'delta_xa_w_kv': 'delta_w', 'delta_xa_q_norm': 'delta_w', 'delta_xa_k_norm': 'delta_w', 'delta_ffn_norm': 'delta_w', 'delta_ffn_w_up': 'delta_w', 'delta_ffn_conv_w': 'delta_w', 'delta_ffn_conv_b': 'delta_w', 'delta_ffn_w_down': 'delta_w', 'delta_a_w_in': 'delta_w', 'delta_a_w_gate2': 'delta_w', 'delta_a_b_gate': 'delta_w', 'delta_a_o_norm': 'delta_w', 'delta_a_w_out': 'delta_w', 'delta_b_w_in': 'delta_w', 'delta_b_q_norm': 'delta_w', 'delta_b_k_norm': 'delta_w', 'delta_b_w_out': 'delta_w', 'delta_c_w_in': 'delta_w', 'delta_c_conv_w': 'delta_w', 'delta_c_conv_b': 'delta_w', 'delta_c_dt_bias': 'delta_w', 'delta_c_a_log': 'delta_w', 'delta_c_d': 'delta_w', 'delta_c_norm': 'delta_w', 'delta_c_w_out': 'delta_w', 'delta_d_w_in': 'delta_w', 'delta_d_lower_bounds': 'delta_w', 'delta_d_o_norm': 'delta_w', 'delta_d_w_out': 'delta_w', 'new_m_mem_norm': 'new_m', 'new_m_mix_norm': 'new_m', 'new_m_xa_w_kv': 'new_m', 'new_m_xa_q_norm': 'new_m', 'new_m_xa_k_norm': 'new_m', 'new_m_ffn_norm': 'new_m', 'new_m_ffn_w_up': 'new_m', 'new_m_ffn_conv_w': 'new_m', 'new_m_ffn_conv_b': 'new_m', 'new_m_ffn_w_down': 'new_m', 'new_m_a_w_in': 'new_m', 'new_m_a_w_gate2': 'new_m', 'new_m_a_b_gate': 'new_m', 'new_m_a_o_norm': 'new_m', 'new_m_a_w_out': 'new_m', 'new_m_b_w_in': 'new_m', 'new_m_b_q_norm': 'new_m', 'new_m_b_k_norm': 'new_m', 'new_m_b_w_out': 'new_m', 'new_m_c_w_in': 'new_m', 'new_m_c_conv_w': 'new_m', 'new_m_c_conv_b': 'new_m', 'new_m_c_dt_bias': 'new_m', 'new_m_c_a_log': 'new_m', 'new_m_c_d': 'new_m', 'new_m_c_norm': 'new_m', 'new_m_c_w_out': 'new_m', 'new_m_d_w_in': 'new_m', 'new_m_d_lower_bounds': 'new_m', 'new_m_d_o_norm': 'new_m', 'new_m_d_w_out': 'new_m', 'new_v_mem_norm': 'new_v', 'new_v_mix_norm': 'new_v', 'new_v_xa_w_kv': 'new_v', 'new_v_xa_q_norm': 'new_v', 'new_v_xa_k_norm': 'new_v', 'new_v_ffn_norm': 'new_v', 'new_v_ffn_w_up': 'new_v', 'new_v_ffn_conv_w': 'new_v', 'new_v_ffn_conv_b': 'new_v', 'new_v_ffn_w_down': 'new_v', 'new_v_a_w_in': 'new_v', 'new_v_a_w_gate2': 'new_v', 'new_v_a_b_gate': 'new_v', 'new_v_a_o_norm': 'new_v', 'new_v_a_w_out': 'new_v', 'new_v_b_w_in': 'new_v', 'new_v_b_q_norm': 'new_v', 'new_v_b_k_norm': 'new_v', 'new_v_b_w_out': 'new_v', 'new_v_c_w_in': 'new_v', 'new_v_c_conv_w': 'new_v', 'new_v_c_conv_b': 'new_v', 'new_v_c_dt_bias': 'new_v', 'new_v_c_a_log': 'new_v', 'new_v_c_d': 'new_v', 'new_v_c_norm': 'new_v', 'new_v_c_w_out': 'new_v', 'new_v_d_w_in': 'new_v', 'new_v_d_lower_bounds': 'new_v', 'new_v_d_o_norm': 'new_v', 'new_v_d_w_out': 'new_v'}


def _forward(args):
    return _fwd_reference(*[args[k] for k in FWD_PARAMS])


def _output_shape():
    out = _jax.eval_shape(lambda: _forward(_fwd_setup_inputs(0)))
    return out.shape, out.dtype

N_MICROBATCH = 1
ADAM_LR = 0.001
ADAM_B1 = 0.9
ADAM_B2 = 0.999
ADAM_EPS = 1e-08
ADAM_WD = 0.01
ADAM_STEP = 10
PER_EXAMPLE_BATCH_AXIS = {'x': 0, 'mem': 0, 'positions': 0, 'loss_target': 0}
SHARED_INPUTS = []
_WEIGHT_DTYPES = {'mem_norm': _jnp.float32, 'mix_norm': _jnp.float32, 'xa_w_kv': _jnp.float32, 'xa_q_norm': _jnp.float32, 'xa_k_norm': _jnp.float32, 'ffn_norm': _jnp.float32, 'ffn_w_up': _jnp.float32, 'ffn_conv_w': _jnp.float32, 'ffn_conv_b': _jnp.float32, 'ffn_w_down': _jnp.float32, 'a_w_in': _jnp.float32, 'a_w_gate2': _jnp.float32, 'a_b_gate': _jnp.float32, 'a_o_norm': _jnp.float32, 'a_w_out': _jnp.float32, 'b_w_in': _jnp.float32, 'b_q_norm': _jnp.float32, 'b_k_norm': _jnp.float32, 'b_w_out': _jnp.float32, 'c_w_in': _jnp.float32, 'c_conv_w': _jnp.float32, 'c_conv_b': _jnp.float32, 'c_dt_bias': _jnp.float32, 'c_a_log': _jnp.float32, 'c_d': _jnp.float32, 'c_norm': _jnp.float32, 'c_w_out': _jnp.float32, 'd_w_in': _jnp.float32, 'd_lower_bounds': _jnp.float32, 'd_o_norm': _jnp.float32, 'd_w_out': _jnp.float32}
MOMENT_SCALE = {'mem_norm': 1.877667e-01, 'mix_norm': 1.877173e+00, 'xa_w_kv': 8.022241e-02, 'xa_q_norm': 4.917234e-01, 'xa_k_norm': 4.951179e-01, 'ffn_norm': 1.305277e+01, 'ffn_w_up': 1.387727e-01, 'ffn_conv_w': 1.674595e+00, 'ffn_conv_b': 1.555156e+00, 'ffn_w_down': 4.021251e-01, 'a_w_in': 2.423982e-01, 'a_w_gate2': 3.936938e-02, 'a_b_gate': 1.553224e-01, 'a_o_norm': 2.268742e+01, 'a_w_out': 3.895224e-01, 'b_w_in': 3.824977e-02, 'b_q_norm': 2.796247e-01, 'b_k_norm': 2.746880e-01, 'b_w_out': 1.189206e-01, 'c_w_in': 2.504802e-01, 'c_conv_w': 4.236293e-01, 'c_conv_b': 1.481060e+00, 'c_dt_bias': 7.138776e-01, 'c_a_log': 2.903133e+00, 'c_d': 1.888503e+00, 'c_norm': 2.034517e+01, 'c_w_out': 1.385385e+00, 'd_w_in': 1.683563e-01, 'd_lower_bounds': 1.447050e-02, 'd_o_norm': 2.741337e+01, 'd_w_out': 6.221289e-01}


def _to_microbatches(a, axis):
    t = _jnp.moveaxis(a, axis, 0)
    t = t.reshape((N_MICROBATCH, t.shape[0] // N_MICROBATCH) + t.shape[1:])
    return _jnp.moveaxis(t, 1, axis + 1)


def setup_inputs(seed: int = 0) -> dict:
    inp = _fwd_setup_inputs(seed)
    key = _jax.random.fold_in(_jax.random.key(seed), 7919)
    shape, _ = _output_shape()
    out = dict(inp)
    out["loss_target"] = _jax.random.normal(_jax.random.fold_in(key, 0), shape, _jnp.float32)
    for i, name in enumerate(TWIN_WEIGHTS):
        w = inp[name].astype(_jnp.float32)
        if MOMENT_SCALE is None:
            s = _jnp.sqrt(_jnp.mean(_jnp.square(w)) + 1e-30)
        else:
            s = MOMENT_SCALE[name]
        km, kv = _jax.random.split(_jax.random.fold_in(key, i + 1))
        out[name] = w
        out["m_" + name] = s * _jax.random.normal(km, w.shape, _jnp.float32)
        out["v_" + name] = (s * s) * _jax.random.uniform(kv, w.shape, _jnp.float32, 0.5, 1.5)
    if N_MICROBATCH > 1:
        for name, axis in PER_EXAMPLE_BATCH_AXIS.items():
            out[name] = _to_microbatches(out[name], axis)
    return {'x': out['x'], 'mem': out['mem'], 'positions': out['positions'], 'mem_norm': out['mem_norm'], 'mix_norm': out['mix_norm'], 'xa_w_kv': out['xa_w_kv'], 'xa_q_norm': out['xa_q_norm'], 'xa_k_norm': out['xa_k_norm'], 'ffn_norm': out['ffn_norm'], 'ffn_w_up': out['ffn_w_up'], 'ffn_conv_w': out['ffn_conv_w'], 'ffn_conv_b': out['ffn_conv_b'], 'ffn_w_down': out['ffn_w_down'], 'a_w_in': out['a_w_in'], 'a_w_gate2': out['a_w_gate2'], 'a_b_gate': out['a_b_gate'], 'a_o_norm': out['a_o_norm'], 'a_w_out': out['a_w_out'], 'b_w_in': out['b_w_in'], 'b_q_norm': out['b_q_norm'], 'b_k_norm': out['b_k_norm'], 'b_w_out': out['b_w_out'], 'c_w_in': out['c_w_in'], 'c_conv_w': out['c_conv_w'], 'c_conv_b': out['c_conv_b'], 'c_dt_bias': out['c_dt_bias'], 'c_a_log': out['c_a_log'], 'c_d': out['c_d'], 'c_norm': out['c_norm'], 'c_w_out': out['c_w_out'], 'd_w_in': out['d_w_in'], 'd_lower_bounds': out['d_lower_bounds'], 'd_o_norm': out['d_o_norm'], 'd_w_out': out['d_w_out'], 'loss_target': out['loss_target'], 'm_mem_norm': out['m_mem_norm'], 'm_mix_norm': out['m_mix_norm'], 'm_xa_w_kv': out['m_xa_w_kv'], 'm_xa_q_norm': out['m_xa_q_norm'], 'm_xa_k_norm': out['m_xa_k_norm'], 'm_ffn_norm': out['m_ffn_norm'], 'm_ffn_w_up': out['m_ffn_w_up'], 'm_ffn_conv_w': out['m_ffn_conv_w'], 'm_ffn_conv_b': out['m_ffn_conv_b'], 'm_ffn_w_down': out['m_ffn_w_down'], 'm_a_w_in': out['m_a_w_in'], 'm_a_w_gate2': out['m_a_w_gate2'], 'm_a_b_gate': out['m_a_b_gate'], 'm_a_o_norm': out['m_a_o_norm'], 'm_a_w_out': out['m_a_w_out'], 'm_b_w_in': out['m_b_w_in'], 'm_b_q_norm': out['m_b_q_norm'], 'm_b_k_norm': out['m_b_k_norm'], 'm_b_w_out': out['m_b_w_out'], 'm_c_w_in': out['m_c_w_in'], 'm_c_conv_w': out['m_c_conv_w'], 'm_c_conv_b': out['m_c_conv_b'], 'm_c_dt_bias': out['m_c_dt_bias'], 'm_c_a_log': out['m_c_a_log'], 'm_c_d': out['m_c_d'], 'm_c_norm': out['m_c_norm'], 'm_c_w_out': out['m_c_w_out'], 'm_d_w_in': out['m_d_w_in'], 'm_d_lower_bounds': out['m_d_lower_bounds'], 'm_d_o_norm': out['m_d_o_norm'], 'm_d_w_out': out['m_d_w_out'], 'v_mem_norm': out['v_mem_norm'], 'v_mix_norm': out['v_mix_norm'], 'v_xa_w_kv': out['v_xa_w_kv'], 'v_xa_q_norm': out['v_xa_q_norm'], 'v_xa_k_norm': out['v_xa_k_norm'], 'v_ffn_norm': out['v_ffn_norm'], 'v_ffn_w_up': out['v_ffn_w_up'], 'v_ffn_conv_w': out['v_ffn_conv_w'], 'v_ffn_conv_b': out['v_ffn_conv_b'], 'v_ffn_w_down': out['v_ffn_w_down'], 'v_a_w_in': out['v_a_w_in'], 'v_a_w_gate2': out['v_a_w_gate2'], 'v_a_b_gate': out['v_a_b_gate'], 'v_a_o_norm': out['v_a_o_norm'], 'v_a_w_out': out['v_a_w_out'], 'v_b_w_in': out['v_b_w_in'], 'v_b_q_norm': out['v_b_q_norm'], 'v_b_k_norm': out['v_b_k_norm'], 'v_b_w_out': out['v_b_w_out'], 'v_c_w_in': out['v_c_w_in'], 'v_c_conv_w': out['v_c_conv_w'], 'v_c_conv_b': out['v_c_conv_b'], 'v_c_dt_bias': out['v_c_dt_bias'], 'v_c_a_log': out['v_c_a_log'], 'v_c_d': out['v_c_d'], 'v_c_norm': out['v_c_norm'], 'v_c_w_out': out['v_c_w_out'], 'v_d_w_in': out['v_d_w_in'], 'v_d_lower_bounds': out['v_d_lower_bounds'], 'v_d_o_norm': out['v_d_o_norm'], 'v_d_w_out': out['v_d_w_out']}


def _loss(weights, diff, rest, loss_target):
    with _jax.named_scope("forward"):
        args = {**rest, TWIN_DIFF_INPUT: diff, **{k: w.astype(_WEIGHT_DTYPES[k]) for k, w in weights.items()}}
        y = _forward(args)
    with _jax.named_scope("loss_head"):
        err = _jnp.square(y.astype(_jnp.float32) - loss_target)
        return 0.5 * _jnp.sum(_jnp.mean(err, axis=-1)) if err.ndim else 0.5 * err


def _adamw(w, g, m, v):
    m = ADAM_B1 * m + (1.0 - ADAM_B1) * g
    v = ADAM_B2 * v + (1.0 - ADAM_B2) * _jnp.square(g)
    m_hat = m / (1.0 - ADAM_B1 ** ADAM_STEP)
    v_hat = v / (1.0 - ADAM_B2 ** ADAM_STEP)
    delta = -ADAM_LR * (m_hat / (_jnp.sqrt(v_hat) + ADAM_EPS) + ADAM_WD * w)
    return delta, m, v


def reference(x, mem, positions, mem_norm, mix_norm, xa_w_kv, xa_q_norm, xa_k_norm, ffn_norm, ffn_w_up, ffn_conv_w, ffn_conv_b, ffn_w_down, a_w_in, a_w_gate2, a_b_gate, a_o_norm, a_w_out, b_w_in, b_q_norm, b_k_norm, b_w_out, c_w_in, c_conv_w, c_conv_b, c_dt_bias, c_a_log, c_d, c_norm, c_w_out, d_w_in, d_lower_bounds, d_o_norm, d_w_out, loss_target, m_mem_norm, m_mix_norm, m_xa_w_kv, m_xa_q_norm, m_xa_k_norm, m_ffn_norm, m_ffn_w_up, m_ffn_conv_w, m_ffn_conv_b, m_ffn_w_down, m_a_w_in, m_a_w_gate2, m_a_b_gate, m_a_o_norm, m_a_w_out, m_b_w_in, m_b_q_norm, m_b_k_norm, m_b_w_out, m_c_w_in, m_c_conv_w, m_c_conv_b, m_c_dt_bias, m_c_a_log, m_c_d, m_c_norm, m_c_w_out, m_d_w_in, m_d_lower_bounds, m_d_o_norm, m_d_w_out, v_mem_norm, v_mix_norm, v_xa_w_kv, v_xa_q_norm, v_xa_k_norm, v_ffn_norm, v_ffn_w_up, v_ffn_conv_w, v_ffn_conv_b, v_ffn_w_down, v_a_w_in, v_a_w_gate2, v_a_b_gate, v_a_o_norm, v_a_w_out, v_b_w_in, v_b_q_norm, v_b_k_norm, v_b_w_out, v_c_w_in, v_c_conv_w, v_c_conv_b, v_c_dt_bias, v_c_a_log, v_c_d, v_c_norm, v_c_w_out, v_d_w_in, v_d_lower_bounds, v_d_o_norm, v_d_w_out):
    given = dict(x=x, mem=mem, positions=positions, mem_norm=mem_norm, mix_norm=mix_norm, xa_w_kv=xa_w_kv, xa_q_norm=xa_q_norm, xa_k_norm=xa_k_norm, ffn_norm=ffn_norm, ffn_w_up=ffn_w_up, ffn_conv_w=ffn_conv_w, ffn_conv_b=ffn_conv_b, ffn_w_down=ffn_w_down, a_w_in=a_w_in, a_w_gate2=a_w_gate2, a_b_gate=a_b_gate, a_o_norm=a_o_norm, a_w_out=a_w_out, b_w_in=b_w_in, b_q_norm=b_q_norm, b_k_norm=b_k_norm, b_w_out=b_w_out, c_w_in=c_w_in, c_conv_w=c_conv_w, c_conv_b=c_conv_b, c_dt_bias=c_dt_bias, c_a_log=c_a_log, c_d=c_d, c_norm=c_norm, c_w_out=c_w_out, d_w_in=d_w_in, d_lower_bounds=d_lower_bounds, d_o_norm=d_o_norm, d_w_out=d_w_out, loss_target=loss_target, m_mem_norm=m_mem_norm, m_mix_norm=m_mix_norm, m_xa_w_kv=m_xa_w_kv, m_xa_q_norm=m_xa_q_norm, m_xa_k_norm=m_xa_k_norm, m_ffn_norm=m_ffn_norm, m_ffn_w_up=m_ffn_w_up, m_ffn_conv_w=m_ffn_conv_w, m_ffn_conv_b=m_ffn_conv_b, m_ffn_w_down=m_ffn_w_down, m_a_w_in=m_a_w_in, m_a_w_gate2=m_a_w_gate2, m_a_b_gate=m_a_b_gate, m_a_o_norm=m_a_o_norm, m_a_w_out=m_a_w_out, m_b_w_in=m_b_w_in, m_b_q_norm=m_b_q_norm, m_b_k_norm=m_b_k_norm, m_b_w_out=m_b_w_out, m_c_w_in=m_c_w_in, m_c_conv_w=m_c_conv_w, m_c_conv_b=m_c_conv_b, m_c_dt_bias=m_c_dt_bias, m_c_a_log=m_c_a_log, m_c_d=m_c_d, m_c_norm=m_c_norm, m_c_w_out=m_c_w_out, m_d_w_in=m_d_w_in, m_d_lower_bounds=m_d_lower_bounds, m_d_o_norm=m_d_o_norm, m_d_w_out=m_d_w_out, v_mem_norm=v_mem_norm, v_mix_norm=v_mix_norm, v_xa_w_kv=v_xa_w_kv, v_xa_q_norm=v_xa_q_norm, v_xa_k_norm=v_xa_k_norm, v_ffn_norm=v_ffn_norm, v_ffn_w_up=v_ffn_w_up, v_ffn_conv_w=v_ffn_conv_w, v_ffn_conv_b=v_ffn_conv_b, v_ffn_w_down=v_ffn_w_down, v_a_w_in=v_a_w_in, v_a_w_gate2=v_a_w_gate2, v_a_b_gate=v_a_b_gate, v_a_o_norm=v_a_o_norm, v_a_w_out=v_a_w_out, v_b_w_in=v_b_w_in, v_b_q_norm=v_b_q_norm, v_b_k_norm=v_b_k_norm, v_b_w_out=v_b_w_out, v_c_w_in=v_c_w_in, v_c_conv_w=v_c_conv_w, v_c_conv_b=v_c_conv_b, v_c_dt_bias=v_c_dt_bias, v_c_a_log=v_c_a_log, v_c_d=v_c_d, v_c_norm=v_c_norm, v_c_w_out=v_c_w_out, v_d_w_in=v_d_w_in, v_d_lower_bounds=v_d_lower_bounds, v_d_o_norm=v_d_o_norm, v_d_w_out=v_d_w_out)
    weights = {n: given[n] for n in TWIN_WEIGHTS}
    shared = {n: given[n] for n in SHARED_INPUTS}
    per_example = {n: given[n] for n in ['x', 'mem', 'positions']}
    grad_fn = _jax.value_and_grad(_loss, argnums=(0, 1))

    def one_microbatch(ex, loss_target):
        ex = dict(ex)
        diff = ex.pop(TWIN_DIFF_INPUT)
        return grad_fn(weights, diff, {**shared, **ex}, loss_target)

    if N_MICROBATCH == 1:
        loss, (grad_w, grad_x) = one_microbatch(per_example, given["loss_target"])
    else:
        def body(carry, xs):
            loss_sum, grad_sum = carry
            l_k, (gw_k, gx_k) = one_microbatch(xs[0], xs[1])
            with _jax.named_scope("update"):
                return (loss_sum + l_k, _jax.tree.map(_jnp.add, grad_sum, gw_k)), gx_k

        init = (_jnp.zeros((), _jnp.float32), _jax.tree.map(_jnp.zeros_like, weights))
        (loss, grad_w), grad_x = _jax.lax.scan(body, init, (per_example, given["loss_target"]))
    with _jax.named_scope("update"):
        delta_w, new_m, new_v = {}, {}, {}
        for n in TWIN_WEIGHTS:
            delta_w[n], new_m[n], new_v[n] = _adamw(weights[n], grad_w[n], given["m_" + n], given["v_" + n])
    return (loss, grad_x, *[grad_w[n] for n in TWIN_WEIGHTS], *[delta_w[n] for n in TWIN_WEIGHTS],
            *[new_m[n] for n in TWIN_WEIGHTS], *[new_v[n] for n in TWIN_WEIGHTS])
```

```python
import functools
import math

import jax
import jax.numpy as jnp
from jax import lax
from jax.experimental import pallas as pl
from jax.experimental.pallas import tpu as pltpu

F32 = jnp.float32
BF = jnp.bfloat16
_MM_DTYPE = BF

N_DEV = 8
EPS = 1e-6
ROPE_THETA = 10000.0
CHUNK = 64
D_MIX = 768
XA_HEADS, XA_HD, D_XA = 4, 64, 256
GLA_HEADS, GLA_DK, GLA_DV, GLA_RANK, GLA_GATE_NORM = 4, 96, 192, 16, 16.0
DIL_GROUPS = ((128, 1), (512, 4), (2048, 16))
DIL_HEADS, DIL_HD, DIL_BLOCK, D_DIL = 4, 128, 128, 512
SSM_HD, SSM_HEADS, SSM_GROUPS, SSM_STATE, SSM_CONV = 64, 12, 2, 128, 4
HGRN_HEADS, HGRN_DK, HGRN_DV = 6, 128, 128
D_FF = 2816
FFN_CONV = 3
DEPTH = 4
ADAM_LR, ADAM_B1, ADAM_B2, ADAM_EPS, ADAM_WD, ADAM_STEP = 0.001, 0.9, 0.999, 1e-08, 0.01, 10
NEG = -1e30
VMEM_LIMIT = 56 << 20

WEIGHTS = ['mem_norm', 'mix_norm', 'xa_w_kv', 'xa_q_norm', 'xa_k_norm', 'ffn_norm', 'ffn_w_up', 'ffn_conv_w',
           'ffn_conv_b', 'ffn_w_down', 'a_w_in', 'a_w_gate2', 'a_b_gate', 'a_o_norm', 'a_w_out', 'b_w_in', 'b_q_norm',
           'b_k_norm', 'b_w_out', 'c_w_in', 'c_conv_w', 'c_conv_b', 'c_dt_bias', 'c_a_log', 'c_d', 'c_norm', 'c_w_out',
           'd_w_in', 'd_lower_bounds', 'd_o_norm', 'd_w_out']
SHARD_AXIS = {'xa_w_kv': 1, 'ffn_w_up': 2, 'ffn_conv_w': 2, 'ffn_w_down': 1, 'a_w_in': 1, 'a_w_gate2': 1, 'a_w_out': 0,
              'b_w_in': 1, 'b_w_out': 1, 'c_w_in': 0, 'c_conv_w': 1, 'c_w_out': 0, 'd_w_in': 1, 'd_w_out': 0}
SHARDED = [n for n in WEIGHTS if n in SHARD_AXIS]
REPLICATED = [n for n in WEIGHTS if n not in SHARD_AXIS]


def _dot(a, b, ca, cb):
    return lax.dot_general(a.astype(_MM_DTYPE), b.astype(_MM_DTYPE), (((ca,), (cb,)), ((), ())),
                           preferred_element_type=F32)


@jax.custom_vjp
def mm_nn(a, b):
    return _dot(a, b, 1, 0)


mm_nn.defvjp(lambda a, b: (_dot(a, b, 1, 0), (a, b)),
             lambda r, g: (_dot(g, r[1], 1, 1), _dot(r[0], g, 0, 0)))


@jax.custom_vjp
def mm_nt(a, b):
    return _dot(a, b, 1, 1)


mm_nt.defvjp(lambda a, b: (_dot(a, b, 1, 1), (a, b)),
             lambda r, g: (_dot(g, r[1], 1, 0), _dot(g, r[0], 0, 0)))


@jax.custom_vjp
def mm_tn(a, b):
    return _dot(a, b, 0, 0)


mm_tn.defvjp(lambda a, b: (_dot(a, b, 0, 0), (a, b)),
             lambda r, g: (_dot(r[1], g, 1, 1), _dot(r[0], g, 1, 0)))


def _dot_hi(a, b, ca, cb):
    return lax.dot_general(a, b, (((ca,), (cb,)), ((), ())), precision=lax.Precision.HIGHEST,
                           preferred_element_type=F32)


def _tril(c):
    return lax.broadcasted_iota(jnp.int32, (c, c), 0) >= lax.broadcasted_iota(jnp.int32, (c, c), 1)


@jax.custom_vjp
def cumsum_rows(x):
    return _dot_hi(_tril(x.shape[0]).astype(F32), x, 1, 0)


cumsum_rows.defvjp(lambda x: (cumsum_rows(x), None),
                   lambda r, g: (_dot_hi(_tril(g.shape[0]).astype(F32), g, 0, 0),))


@jax.custom_vjp
def cumsum_rows_t(x):
    return _dot_hi(x, _tril(x.shape[0]).astype(F32), 0, 1)


cumsum_rows_t.defvjp(lambda x: (cumsum_rows_t(x), None),
                     lambda r, g: (_dot_hi(_tril(g.shape[1]).astype(F32), g, 0, 1),))


def _split(x, sizes):
    sizes = tuple(int(s) for s in sizes)
    assert sum(sizes) == x.shape[-1], (sizes, x.shape)

    @jax.custom_vjp
    def sp(x):
        out, o = [], 0
        for s in sizes:
            out.append(x[:, o:o + s])
            o += s
        return tuple(out)

    sp.defvjp(lambda x: (sp(x), None), lambda r, g: (jnp.concatenate(list(g), axis=1),))
    return sp(x)


def _row(x, r):
    m = lax.broadcasted_iota(jnp.int32, x.shape, 0) == r
    return jnp.sum(jnp.where(m, x, 0.0), axis=0, keepdims=True)


@jax.custom_vjp
def _roll_half(x):
    return pltpu.roll(x, 64, 1)


_roll_half.defvjp(lambda x: (pltpu.roll(x, 64, 1), None), lambda r, g: (pltpu.roll(g, 64, 1),))


def _shift(xp, x, d):
    if d == 0:
        return x
    n = x.shape[0]

    @jax.custom_vjp
    def sh(xp, x):
        row = lax.broadcasted_iota(jnp.int32, x.shape, 0)
        return jnp.where(row < d, pltpu.roll(xp, d, 0), pltpu.roll(x, d, 0))

    def bwd(_, g):
        row = lax.broadcasted_iota(jnp.int32, g.shape, 0)
        r = pltpu.roll(g, n - d, 0)
        return jnp.where(row >= n - d, r, 0.0), jnp.where(row < n - d, r, 0.0)

    sh.defvjp(lambda xp, x: (sh(xp, x), None), bwd)
    return sh(xp, x)


def _rms(x, g):
    return x * lax.rsqrt(jnp.mean(x * x, axis=-1, keepdims=True) + EPS) * g


def _lane_pair(a, b, width=128):
    shape = a.shape[:-1] + (width,)
    lane = lax.broadcasted_iota(jnp.int32, shape, len(shape) - 1)
    return jnp.where(lane < width // 2, a, b)


def _row_spec(a, w=None, c=None, prev=False, diff=True, dn=None):
    return dict(a=a, w=a.shape[1] if w is None else w, c=(lambda jc: 0) if c is None else c, prev=prev, diff=diff,
                dn=a.shape[1] if dn is None else dn)


def _par_spec(a, bs=None, idx=None, diff=True):
    nd = a.ndim
    return dict(a=a, bs=tuple(a.shape) if bs is None else tuple(bs),
                idx=(lambda jc: (0,) * nd) if idx is None else idx, diff=diff)


def _out_spec(n, w=None, c=None, dt=F32):
    return dict(n=n, w=n if w is None else w, c=(lambda jc: 0) if c is None else c, dt=dt)


def _cparams():
    return pltpu.CompilerParams(dimension_semantics=("arbitrary", "arbitrary"), vmem_limit_bytes=VMEM_LIMIT)


def _seq_fwd(name, f, R, rows, params, state_shapes, outs, *, ncol=1, period=None, save_states=False):
    nrows = rows[0]['a'].shape[0]
    nb = nrows // R
    assert nb * R == nrows
    period = nb if period is None else period
    prev_ids = [k for k, r in enumerate(rows) if r['prev']]
    n_rows, n_prev, n_par, n_out, n_st = len(rows), len(prev_ids), len(params), len(outs), len(state_shapes)

    def body(*refs):
        o = 0
        cur = refs[o:o + n_rows]; o += n_rows
        prv = refs[o:o + n_prev]; o += n_prev
        par = refs[o:o + n_par]; o += n_par
        out = refs[o:o + n_out]; o += n_out
        sav = refs[o:o + (n_st if save_states else 0)]; o += len(sav)
        st = refs[o:o + n_st]
        i = pl.program_id(1)
        first = (i % period) == 0

        @pl.when(i == 0)
        def _():
            for s in st:
                s[...] = jnp.zeros_like(s)

        xs = [r[...].astype(F32) for r in cur]
        xp = [r[...].astype(F32) for r in prv]
        ps = [r[...] for r in par]
        sts = [s[...] for s in st]
        for sv, s in zip(sav, sts):
            sv[0] = s
        ov, ns = f(first, xp, xs, ps, sts)
        for r, v in zip(out, ov):
            r[...] = v.astype(r.dtype)
        for s, v in zip(st, ns):
            s[...] = v

    in_specs = [pl.BlockSpec((R, r['w']), functools.partial(lambda jc, i, c: (i, c(jc)), c=r['c'])) for r in rows]
    in_specs += [pl.BlockSpec((R, rows[k]['w']),
                              functools.partial(lambda jc, i, c: (jnp.maximum(i - 1, 0), c(jc)), c=rows[k]['c']))
                 for k in prev_ids]
    in_specs += [pl.BlockSpec(p['bs'], functools.partial(lambda jc, i, idx: idx(jc), idx=p['idx'])) for p in params]
    out_specs = [pl.BlockSpec((R, o_['w']), functools.partial(lambda jc, i, c: (i, c(jc)), c=o_['c'])) for o_ in outs]
    out_shape = [jax.ShapeDtypeStruct((nrows, o_['n']), o_['dt']) for o_ in outs]
    if save_states:
        for s in state_shapes:
            out_specs.append(pl.BlockSpec((1,) + tuple(s), lambda jc, i, nd=len(s): (i,) + (0,) * nd))
            out_shape.append(jax.ShapeDtypeStruct((nb,) + tuple(s), F32))
    args = [r['a'] for r in rows] + [rows[k]['a'] for k in prev_ids] + [p['a'] for p in params]
    res = pl.pallas_call(
        body, name=name, grid=(ncol, nb), in_specs=in_specs, out_specs=out_specs, out_shape=out_shape,
        scratch_shapes=[pltpu.VMEM(tuple(s), F32) for s in state_shapes], compiler_params=_cparams())(*args)
    return list(res[:n_out]), list(res[n_out:])


def _seq_bwd(name, f, R, rows, params, state_shapes, douts, saved, *, ncol=1, period=None, dx_dt=F32):
    nrows = rows[0]['a'].shape[0]
    nb = nrows // R
    period = nb if period is None else period
    prev_ids = [k for k, r in enumerate(rows) if r['prev']]
    drow_ids = [k for k, r in enumerate(rows) if r['diff']]
    dpar_ids = [k for k, p in enumerate(params) if p['diff']]
    for k in prev_ids:
        assert rows[k]['diff']
    n_rows, n_prev, n_par, n_do, n_st = len(rows), len(prev_ids), len(params), len(douts), len(state_shapes)
    n_dx, n_dp = len(drow_ids), len(dpar_ids)

    def body(*refs):
        o = 0
        cur = refs[o:o + n_rows]; o += n_rows
        prv = refs[o:o + n_prev]; o += n_prev
        par = refs[o:o + n_par]; o += n_par
        sav = refs[o:o + n_st]; o += n_st
        dou = refs[o:o + n_do]; o += n_do
        dxr = refs[o:o + n_dx]; o += n_dx
        dpr = refs[o:o + n_dp]; o += n_dp
        dst = refs[o:o + n_st]; o += n_st
        car = refs[o:o + n_prev]
        j = pl.program_id(1)
        i = nb - 1 - j
        first = (i % period) == 0

        @pl.when(j == 0)
        def _():
            for s in tuple(dst) + tuple(car) + tuple(dpr):
                s[...] = jnp.zeros_like(s)

        xs = [r[...].astype(F32) for r in cur]
        xp = [r[...].astype(F32) for r in prv]
        ps = [r[...] for r in par]
        sts = [s[0] for s in sav]

        def g(dxs, dxp, dps, dsts):
            xs_, ps_ = list(xs), list(ps)
            for k, v in zip(drow_ids, dxs):
                xs_[k] = v
            for k, v in zip(dpar_ids, dps):
                ps_[k] = v
            ov, ns = f(first, list(dxp), xs_, ps_, list(dsts))
            return tuple(ov), tuple(ns)

        _, vjp = jax.vjp(g, tuple(xs[k] for k in drow_ids), tuple(xp), tuple(ps[k] for k in dpar_ids), tuple(sts))
        dxs, dxp, dps, dsts = vjp((tuple(r[...].astype(F32) for r in dou), tuple(s[...] for s in dst)))
        dxs = list(dxs)
        for n_, k in enumerate(prev_ids):
            pos = drow_ids.index(k)
            dxs[pos] = dxs[pos] + car[n_][...]
            car[n_][...] = dxp[n_]
        for r, v in zip(dxr, dxs):
            r[...] = v.astype(r.dtype)
        for r, v in zip(dpr, dps):
            r[...] += v
        for s, v in zip(dst, dsts):
            s[...] = v

    def rmap(jc, j, c):
        return (nb - 1 - j, c(jc))

    def pmap(jc, j, c):
        return (jnp.maximum(nb - 2 - j, 0), c(jc))

    in_specs = [pl.BlockSpec((R, r['w']), functools.partial(rmap, c=r['c'])) for r in rows]
    in_specs += [pl.BlockSpec((R, rows[k]['w']), functools.partial(pmap, c=rows[k]['c'])) for k in prev_ids]
    in_specs += [pl.BlockSpec(p['bs'], functools.partial(lambda jc, j, idx: idx(jc), idx=p['idx'])) for p in params]
    in_specs += [pl.BlockSpec((1,) + tuple(s), lambda jc, j, nd=len(s): (nb - 1 - j,) + (0,) * nd) for s in state_shapes]
    in_specs += [pl.BlockSpec((R, d['w']), functools.partial(rmap, c=d['c'])) for d in douts]
    out_specs = [pl.BlockSpec((R, rows[k]['w']), functools.partial(rmap, c=rows[k]['c'])) for k in drow_ids]
    out_shape = [jax.ShapeDtypeStruct((nrows, rows[k]['dn']), dx_dt) for k in drow_ids]
    for k in dpar_ids:
        p = params[k]
        out_specs.append(pl.BlockSpec(p['bs'], functools.partial(lambda jc, j, idx: idx(jc), idx=p['idx'])))
        out_shape.append(jax.ShapeDtypeStruct(p['a'].shape, F32))
    scratch = [pltpu.VMEM(tuple(s), F32) for s in state_shapes]
    scratch += [pltpu.VMEM((R, rows[k]['w']), F32) for k in prev_ids]
    args = ([r['a'] for r in rows] + [rows[k]['a'] for k in prev_ids] + [p['a'] for p in params] + list(saved)
            + [d['a'] for d in douts])
    res = pl.pallas_call(
        body, name=name, grid=(ncol, nb), in_specs=in_specs, out_specs=out_specs, out_shape=out_shape,
        scratch_shapes=scratch, compiler_params=_cparams())(*args)
    return list(res[:n_dx]), list(res[n_dx:])


def _tile(n, cands):
    for c in cands:
        if n % c == 0:
            return c
    return n


def _matmul(a, b, mode="nn", add=None, out_dtype=F32, name="matmul"):
    if mode == "nn":
        (M, K), N = a.shape, b.shape[1]
    elif mode == "nt":
        (M, K), N = a.shape, b.shape[0]
    else:
        (K, M), N = a.shape, b.shape[1]
    tm = _tile(M, (1024, 512, 256, 128, 64, 32, 16, 8))
    tn = _tile(N, (512, 256, 128))
    tk = K if K <= 2816 else _tile(K, (2048, 1024, 512, 256, 128))
    nk = K // tk
    ca, cb = {"nn": (1, 0), "nt": (1, 1), "tn": (0, 0)}[mode]
    has_add = add is not None

    def body(*refs):
        a_ref, b_ref = refs[0], refs[1]
        add_ref = refs[2] if has_add else None
        o_ref, acc = refs[-2], refs[-1]
        k = pl.program_id(2)
        part = _dot(a_ref[...], b_ref[...], ca, cb)

        @pl.when(k == 0)
        def _():
            acc[...] = part

        @pl.when(k > 0)
        def _():
            acc[...] += part

        @pl.when(k == nk - 1)
        def _():
            r = acc[...]
            if has_add:
                r = r + add_ref[...].astype(F32)
            o_ref[...] = r.astype(o_ref.dtype)

    if mode == "tn":
        a_spec = pl.BlockSpec((tk, tm), lambda i, j, k: (k, i))
    else:
        a_spec = pl.BlockSpec((tm, tk), lambda i, j, k: (i, k))
    if mode == "nt":
        b_spec = pl.BlockSpec((tn, tk), lambda i, j, k: (j, k))
    else:
        b_spec = pl.BlockSpec((tk, tn), lambda i, j, k: (k, j))
    in_specs = [a_spec, b_spec]
    args = [a, b]
    if has_add:
        in_specs.append(pl.BlockSpec((tm, tn), lambda i, j, k: (i, j)))
        args.append(add)
    return pl.pallas_call(
        body, name=name, grid=(M // tm, N // tn, nk), in_specs=in_specs,
        out_specs=pl.BlockSpec((tm, tn), lambda i, j, k: (i, j)),
        out_shape=jax.ShapeDtypeStruct((M, N), out_dtype),
        scratch_shapes=[pltpu.VMEM((tm, tn), F32)],
        compiler_params=pltpu.CompilerParams(dimension_semantics=("parallel", "parallel", "arbitrary"),
                                             vmem_limit_bytes=VMEM_LIMIT))(*args)


def f_rmsnorm(first, xp, xs, ps, sts):
    return (_rms(xs[0], ps[0]),), ()


def f_xattn(first, xp, xs, ps, sts):
    (xq,), (kv, qn, kn) = xs, ps
    qs = _split(xq, [XA_HD] * XA_HEADS)
    kvs = _split(kv, [XA_HD] * (2 * XA_HEADS))
    outs = []
    for h in range(XA_HEADS):
        q = _rms(qs[h], qn)
        k = _rms(kvs[h], kn)
        v = kvs[XA_HEADS + h]
        s = mm_nt(q, k) * (XA_HD ** -0.5)
        m = lax.stop_gradient(jnp.max(s, axis=-1, keepdims=True))
        p = jnp.exp(s - m)
        outs.append(mm_nn(p / jnp.sum(p, axis=-1, keepdims=True), v))
    return (jnp.concatenate(outs, axis=1),), ()


def _conv(xp, x, w, b, first, taps):
    xp = jnp.where(first, 0.0, xp)
    y = b + w[taps - 1:taps] * x
    for d in range(1, taps):
        y = y + w[taps - 1 - d:taps - d] * _shift(xp, x, d)
    return y


def f_ffn_act(first, xp, xs, ps, sts):
    (ugp, uvp), (ug, uv), (wg, wv, bg, bv) = xp, xs, ps
    gate = _conv(ugp, ug, wg, bg, first, FFN_CONV)
    val = _conv(uvp, uv, wv, bv, first, FFN_CONV)
    return (jax.nn.silu(gate) * val,), ()


def _gla_chunk(q, k, v, la, st):
    c = q.shape[0]
    b = cumsum_rows(la)
    b_last = _row(b, c - 1)
    b_ref = _row(b, c // 2 - 1)
    att = mm_nt(q * jnp.exp(b - b_ref), k * jnp.exp(b_ref - b))
    att = jnp.where(_tril(c), att, 0.0)
    o = mm_nn(att, v) + mm_nt(q * jnp.exp(b), st)
    st_new = st * jnp.exp(b_last) + mm_tn(v, k * jnp.exp(b_last - b))
    return o, st_new


def _a_cols(ntot):
    used = D_XA + 2 * GLA_HEADS * GLA_DK + D_MIX + GLA_RANK + D_MIX
    return [D_XA, GLA_HEADS * GLA_DK, GLA_HEADS * GLA_DK, D_MIX, GLA_RANK, D_MIX] + ([ntot - used] if ntot > used else [])


def f_gla(first, xp, xs, ps, sts):
    (p,), (wg2, bg, on) = xs, ps
    parts = _split(p, _a_cols(p.shape[1]))
    q, k, v, glr, og = parts[1:6]
    la = jax.nn.log_sigmoid(mm_nn(glr, wg2) + bg) / GLA_GATE_NORM
    qs = _split(q * (GLA_DK ** -0.5), [GLA_DK] * GLA_HEADS)
    ks = _split(k, [GLA_DK] * GLA_HEADS)
    vs = _split(v, [GLA_DV] * GLA_HEADS)
    las = _split(la, [GLA_DK] * GLA_HEADS)
    outs, new = [], []
    for h in range(GLA_HEADS):
        o, s = _gla_chunk(qs[h], ks[h], vs[h], las[h], sts[h])
        outs.append(_rms(o, on))
        new.append(s)
    return (jnp.concatenate(outs, axis=1) * jax.nn.silu(og),), tuple(new)


def f_hgrn(first, xp, xs, ps, sts):
    (p,), (lbp, on) = xs, ps
    _, q, fgate, iv, og = _split(p, [D_XA, D_MIX, D_MIX, D_MIX, D_MIX])
    e = jnp.exp(lbp - jnp.max(lbp, axis=0, keepdims=True))
    row = lax.broadcasted_iota(jnp.int32, e.shape, 0)
    lb = jnp.sum(jnp.where(row >= 1, e, 0.0), axis=0, keepdims=True) / jnp.sum(e, axis=0, keepdims=True)
    fg = lb + (1.0 - lb) * jax.nn.sigmoid(fgate)
    qs = _split(jax.nn.silu(q), [HGRN_DK] * HGRN_HEADS)
    ks = _split(1.0 - fg, [HGRN_DK] * HGRN_HEADS)
    vs = _split(iv, [HGRN_DV] * HGRN_HEADS)
    las = _split(jnp.log(fg), [HGRN_DK] * HGRN_HEADS)
    outs, new = [], []
    for h in range(HGRN_HEADS):
        o, s = _gla_chunk(qs[h], ks[h], vs[h], las[h], sts[h])
        outs.append(_rms(o, on))
        new.append(s)
    return (jnp.concatenate(outs, axis=1) * jax.nn.sigmoid(og),), tuple(new)


def _c_cols(ntot):
    gn = SSM_GROUPS * SSM_STATE
    used = D_XA + D_MIX + D_MIX + 2 * gn + SSM_HEADS
    return [D_XA, D_MIX, D_MIX + 2 * gn, SSM_HEADS] + ([ntot - used] if ntot > used else [])


def f_ssd(first, xp, xs, ps, sts):
    (pp,), (p,), (cw, cb, dtb, alog, dsk, ng) = xp, xs, ps
    c = p.shape[0]
    gn = SSM_GROUPS * SSM_STATE
    hg = SSM_HEADS // SSM_GROUPS
    _, z, xbc, dtr = _split(p, _c_cols(p.shape[1]))[:4]
    xbc_p = _split(pp, _c_cols(p.shape[1]))[2]
    xbc = jax.nn.silu(_conv(xbc_p, xbc, cw, cb, first, SSM_CONV))
    xs_, bm, cm = _split(xbc, [D_MIX, gn, gn])
    dt = jax.nn.softplus(dtr + dtb)
    a = dt * (-jnp.exp(alog))
    acs = cumsum_rows(a)
    acs_t = cumsum_rows_t(a)
    acs_last = _row(acs, c - 1)
    dt_h = _split(dt, [1] * SSM_HEADS)
    acs_h = _split(acs, [1] * SSM_HEADS)
    al_h = _split(acs_last, [1] * SSM_HEADS)
    d_h = _split(dsk, [1] * SSM_HEADS)
    x2s = _split(xs_, [2 * SSM_HD] * (SSM_HEADS // 2))
    bms = _split(bm, [SSM_STATE] * SSM_GROUPS)
    cms = _split(cm, [SSM_STATE] * SSM_GROUPS)
    tril = _tril(c)
    cbs = [mm_nt(cms[g], bms[g]) for g in range(SSM_GROUPS)]
    ys, new = [], []
    for j in range(SSM_HEADS // 2):
        g = (2 * j) // hg
        h0, h1 = 2 * j, 2 * j + 1
        xdt = x2s[j] * _lane_pair(dt_h[h0], dt_h[h1])
        acs2 = _lane_pair(acs_h[h0], acs_h[h1])
        al2 = _lane_pair(al_h[h0], al_h[h1])
        yd = []
        for h in (h0, h1):
            seg = acs_h[h] - _row(acs_t, h)
            lm = jnp.exp(jnp.where(tril, seg, NEG))
            yd.append(mm_nn(cbs[g] * lm, xdt))
        lane = lax.broadcasted_iota(jnp.int32, xdt.shape, 1)
        y_diag = jnp.where(lane < SSM_HD, yd[0], yd[1])
        y_off = mm_nn(cms[g], sts[j]) * jnp.exp(acs2)
        x_end = xdt * jnp.exp(al2 - acs2)
        new.append(sts[j] * jnp.exp(al2) + mm_tn(bms[g], x_end))
        ys.append(y_diag + y_off + _lane_pair(d_h[h0], d_h[h1]) * x2s[j])
    y = jnp.concatenate(ys, axis=1) * jax.nn.silu(z)
    gw = D_MIX // SSM_GROUPS
    yg = _split(y, [gw] * SSM_GROUPS)
    ngs = _split(ng, [gw] * SSM_GROUPS)
    y = jnp.concatenate([_rms(yg[g], ngs[g]) for g in range(SSM_GROUPS)], axis=1)
    return (y,), tuple(new)


def f_dil_prep(first, xp, xs, ps, sts):
    (p, pos), (qn, kn, invf, sign) = xs, ps
    nh = len(DIL_GROUPS) * DIL_HEADS
    _, q, k, _v = _split(p, [D_XA] + [nh * DIL_HD] * 3)
    ang = pos * invf
    cos, sin = jnp.cos(ang), jnp.sin(ang) * sign

    def rope(t, g):
        hs = _split(t, [DIL_HD] * nh)
        out = []
        for h in hs:
            n = _rms(h, g)
            out.append(n * cos + _roll_half(n) * sin)
        return jnp.concatenate(out, axis=1)

    return (rope(q, qn), rope(k, kn)), ()


def f_dil_attn(first, xp, xs, ps, sts):
    (kp, vp), (q, k, v) = xp, xs
    Q = DIL_BLOCK
    qs, ks, vs = (_split(t, [DIL_HD] * DIL_HEADS) for t in (q, k, v))
    kps, vps = (_split(t, [DIL_HD] * DIL_HEADS) for t in (kp, vp))
    i = lax.broadcasted_iota(jnp.int32, (Q, 2 * Q), 0)
    j = lax.broadcasted_iota(jnp.int32, (Q, 2 * Q), 1)
    dist = Q + i - j
    mask = (dist >= 0) & (dist <= Q) & (jnp.logical_not(first) | (j >= Q))
    outs, lses = [], []
    for h in range(DIL_HEADS):
        k2 = jnp.concatenate([kps[h], ks[h]], axis=0)
        v2 = jnp.concatenate([vps[h], vs[h]], axis=0)
        s = jnp.where(mask, mm_nt(qs[h], k2) * (DIL_HD ** -0.5), NEG)
        m = lax.stop_gradient(jnp.max(s, axis=-1, keepdims=True))
        p = jnp.exp(s - m)
        l = jnp.sum(p, axis=-1, keepdims=True)
        outs.append(mm_nn(p / l, v2))
        lses.append(jnp.broadcast_to(m + jnp.log(l), (Q, DIL_HD)))
    return (jnp.concatenate(outs, axis=1), jnp.concatenate(lses, axis=1)), ()


def f_dil_merge(first, xp, xs, ps, sts):
    o0, o1, o2, l0, l1, l2 = xs
    m = jnp.maximum(jnp.maximum(l0, l1), l2)
    e0, e1, e2 = jnp.exp(l0 - m), jnp.exp(l1 - m), jnp.exp(l2 - m)
    den = e0 + e1 + e2
    return ((e0 * o0 + e1 * o1 + e2 * o2) / den,), ()


def _loss_head(y, target):
    S, D = y.shape
    R = _tile(S, (512, 256, 128, 64, 32, 16, 8))

    def body(y_ref, t_ref, dy_ref, l_ref):
        e = y_ref[...] - t_ref[...]
        dy_ref[...] = e * (1.0 / D)

        @pl.when(pl.program_id(0) == 0)
        def _():
            l_ref[...] = jnp.zeros_like(l_ref)

        l_ref[...] += jnp.broadcast_to(0.5 * jnp.sum(jnp.mean(e * e, axis=-1, keepdims=True), axis=0, keepdims=True),
                                       l_ref.shape)

    dy, l = pl.pallas_call(
        body, name="loss_head", grid=(S // R,),
        in_specs=[pl.BlockSpec((R, D), lambda i: (i, 0))] * 2,
        out_specs=[pl.BlockSpec((R, D), lambda i: (i, 0)), pl.BlockSpec((8, 128), lambda i: (0, 0))],
        out_shape=[jax.ShapeDtypeStruct((S, D), F32), jax.ShapeDtypeStruct((8, 128), F32)],
        compiler_params=pltpu.CompilerParams(dimension_semantics=("arbitrary",)))(y, target)
    return dy, l[0, 0]


def _adamw(parts, w, m, v, name):
    _, n, width = parts.shape
    tr = _tile(n, (256, 128, 64, 32, 16, 8))

    def body(p_ref, w_ref, m_ref, v_ref, g_ref, d_ref, nm_ref, nv_ref):
        g = p_ref[0].astype(F32)
        for s in range(1, N_DEV):
            g = g + p_ref[s].astype(F32)
        nm = ADAM_B1 * m_ref[...] + (1.0 - ADAM_B1) * g
        nv = ADAM_B2 * v_ref[...] + (1.0 - ADAM_B2) * (g * g)
        m_hat = nm / (1.0 - ADAM_B1 ** ADAM_STEP)
        v_hat = nv / (1.0 - ADAM_B2 ** ADAM_STEP)
        g_ref[...] = g
        d_ref[...] = -ADAM_LR * (m_hat / (jnp.sqrt(v_hat) + ADAM_EPS) + ADAM_WD * w_ref[...])
        nm_ref[...] = nm
        nv_ref[...] = nv

    blk = pl.BlockSpec((tr, width), lambda i: (i, 0))
    return pl.pallas_call(
        body, name=name, grid=(n // tr,),
        in_specs=[pl.BlockSpec((N_DEV, tr, width), lambda i: (0, i, 0)), blk, blk, blk],
        out_specs=[blk] * 4, out_shape=[jax.ShapeDtypeStruct((n, width), F32)] * 4,
        compiler_params=pltpu.CompilerParams(dimension_semantics=("arbitrary",), vmem_limit_bytes=VMEM_LIMIT))(
            parts, w, m, v)


def _peer(k):
    x, y, c = lax.axis_index("x"), lax.axis_index("y"), lax.axis_index("c")
    px = 1 - x if k & 4 else x
    py = 1 - y if k & 2 else y
    pc = 1 - c if k & 1 else c
    return (px, py, pc), 4 * px + 2 * py + pc


def _my_id():
    return 4 * lax.axis_index("x") + 2 * lax.axis_index("y") + lax.axis_index("c")


def _all_gather(x, name):
    def body(x_ref, out_ref, send, recv, loc):
        me = _my_id()
        mine = pltpu.make_async_copy(x_ref, out_ref.at[me], loc)
        mine.start()
        cps = []
        for k in range(1, N_DEV):
            peer, _ = _peer(k)
            cp = pltpu.make_async_remote_copy(src_ref=x_ref, dst_ref=out_ref.at[me], send_sem=send.at[k - 1],
                                              recv_sem=recv.at[k - 1], device_id=peer,
                                              device_id_type=pl.DeviceIdType.MESH)
            cp.start()
            cps.append(cp)
        for k in range(1, N_DEV):
            peer, pid = _peer(k)
            pltpu.make_async_remote_copy(src_ref=x_ref, dst_ref=out_ref.at[pid], send_sem=send.at[k - 1],
                                         recv_sem=recv.at[k - 1], device_id=peer,
                                         device_id_type=pl.DeviceIdType.MESH).wait_recv()
        for cp in cps:
            cp.wait_send()
        mine.wait()

    return pl.pallas_call(
        body, name=name, out_shape=jax.ShapeDtypeStruct((N_DEV,) + x.shape, x.dtype),
        in_specs=[pl.BlockSpec(memory_space=pl.ANY)], out_specs=pl.BlockSpec(memory_space=pl.ANY),
        scratch_shapes=[pltpu.SemaphoreType.DMA((N_DEV - 1,)), pltpu.SemaphoreType.DMA((N_DEV - 1,)),
                        pltpu.SemaphoreType.DMA],
        compiler_params=pltpu.CompilerParams(has_side_effects=True))(x)


def _exchange(g, name):
    def body(g_ref, out_ref, send, recv, loc):
        me = _my_id()
        mine = pltpu.make_async_copy(g_ref.at[me], out_ref.at[me], loc)
        mine.start()
        cps = []
        for k in range(1, N_DEV):
            peer, pid = _peer(k)
            cp = pltpu.make_async_remote_copy(src_ref=g_ref.at[pid], dst_ref=out_ref.at[me], send_sem=send.at[k - 1],
                                              recv_sem=recv.at[k - 1], device_id=peer,
                                              device_id_type=pl.DeviceIdType.MESH)
            cp.start()
            cps.append(cp)
        for k in range(1, N_DEV):
            peer, pid = _peer(k)
            pltpu.make_async_remote_copy(src_ref=g_ref.at[me], dst_ref=out_ref.at[pid], send_sem=send.at[k - 1],
                                         recv_sem=recv.at[k - 1], device_id=peer,
                                         device_id_type=pl.DeviceIdType.MESH).wait_recv()
        for cp in cps:
            cp.wait_send()
        mine.wait()

    return pl.pallas_call(
        body, name=name, out_shape=jax.ShapeDtypeStruct(g.shape, g.dtype),
        in_specs=[pl.BlockSpec(memory_space=pl.ANY)], out_specs=pl.BlockSpec(memory_space=pl.ANY),
        scratch_shapes=[pltpu.SemaphoreType.DMA((N_DEV - 1,)), pltpu.SemaphoreType.DMA((N_DEV - 1,)),
                        pltpu.SemaphoreType.DMA],
        compiler_params=pltpu.CompilerParams(has_side_effects=True))(g)


PACK_W = 1024


def _granule(n):
    return (256 if n >= 256 * PACK_W else 8) * PACK_W


def _pack(arrs, dtype):
    flat = jnp.concatenate([a.reshape(-1).astype(dtype) for a in arrs])
    n = flat.shape[0]
    pad = (-n) % _granule(n)
    if pad:
        flat = jnp.concatenate([flat, jnp.zeros((pad,), dtype)])
    return flat.reshape(-1, PACK_W)


def _unpack(packed, shapes):
    flat = packed.reshape(-1)
    out, o = [], 0
    for s in shapes:
        n = math.prod(s)
        out.append(flat[o:o + n].reshape(s))
        o += n
    return out


def _pack_lead(arrs, dtype):
    flat = jnp.concatenate([a.reshape(N_DEV, -1).astype(dtype) for a in arrs], axis=1)
    n = flat.shape[1]
    pad = (-n) % _granule(n)
    if pad:
        flat = jnp.concatenate([flat, jnp.zeros((N_DEV, pad), dtype)], axis=1)
    return flat.reshape(N_DEV, -1, PACK_W)


def _unpack_lead(packed, shapes):
    flat = packed.reshape(N_DEV, -1)
    out, o = [], 0
    for s in shapes:
        n = math.prod(s)
        out.append(flat[:, o:o + n].reshape((N_DEV,) + tuple(s)))
        o += n
    return out


def _to_full(stacked, axis):
    t = jnp.moveaxis(stacked, 0, axis)
    s = list(t.shape)
    return t.reshape(s[:axis] + [s[axis] * s[axis + 1]] + s[axis + 2:])


def _to_chunks(full, axis):
    s = list(full.shape)
    t = full.reshape(s[:axis] + [N_DEV, s[axis] // N_DEV] + s[axis + 1:])
    return jnp.moveaxis(t, axis, 0)


def _pad_cols(w, n):
    return w if w.shape[1] == n else jnp.concatenate([w, jnp.zeros((w.shape[0], n - w.shape[1]), w.dtype)], axis=1)


def _w_cat(w_in, n_mix):
    w = jnp.concatenate([w_in[:, n_mix:], w_in[:, :n_mix]], axis=1)
    return _pad_cols(w, -(-w.shape[1] // 256) * 256)


def _dw_uncat(dw, n_mix):
    return jnp.concatenate([dw[:, D_XA:D_XA + n_mix], dw[:, :D_XA]], axis=1)


def _rows_of(S):
    return _tile(S, (512, 256, 128, 64))


def _norm_fwd(x, g, dt=BF):
    (h,), _ = _seq_fwd("rmsnorm_fwd", f_rmsnorm, _rows_of(x.shape[0]), [_row_spec(x)], [_par_spec(g)], [],
                       [_out_spec(x.shape[1], dt=dt)])
    return h


def _norm_bwd(x, g, dh):
    (dx,), (dg,) = _seq_bwd("rmsnorm_bwd", f_rmsnorm, _rows_of(x.shape[0]), [_row_spec(x)], [_par_spec(g)], [],
                            [_row_spec(dh)], [])
    return dx, dg


def _mixer_specs(kind, S, p, w):
    if kind == 0:
        return (f_gla, CHUNK, [_row_spec(p)],
                [_par_spec(w['a_w_gate2']), _par_spec(w['a_b_gate'].reshape(1, -1)), _par_spec(w['a_o_norm'].reshape(1, -1))],
                [(GLA_DV, GLA_DK)] * GLA_HEADS, D_MIX)
    if kind == 2:
        return (f_ssd, CHUNK, [_row_spec(p, prev=True)],
                [_par_spec(w['c_conv_w']), _par_spec(w['c_conv_b'].reshape(1, -1)), _par_spec(w['c_dt_bias'].reshape(1, -1)),
                 _par_spec(w['c_a_log'].reshape(1, -1)), _par_spec(w['c_d'].reshape(1, -1)),
                 _par_spec(w['c_norm'].reshape(1, -1))],
                [(SSM_STATE, 2 * SSM_HD)] * (SSM_HEADS // 2), D_MIX)
    return (f_hgrn, CHUNK, [_row_spec(p)],
            [_par_spec(w['d_lower_bounds']), _par_spec(w['d_o_norm'].reshape(1, -1))],
            [(HGRN_DV, HGRN_DK)] * HGRN_HEADS, D_MIX)


def _perm(t, r):
    if r == 1:
        return t
    S, n = t.shape
    return t.reshape(S // r, r, n).transpose(1, 0, 2).reshape(S, n)


def _unperm(t, r):
    if r == 1:
        return t
    S, n = t.shape
    return t.reshape(r, S // r, n).transpose(1, 0, 2).reshape(S, n)


def _rope_consts():
    half = DIL_HD // 2
    inv = ROPE_THETA ** (-jnp.arange(half, dtype=F32) / half)
    invf = jnp.concatenate([inv, inv]).reshape(1, DIL_HD)
    sign = jnp.concatenate([-jnp.ones((half,), F32), jnp.ones((half,), F32)]).reshape(1, DIL_HD)
    return invf, sign


def _dil_fwd(p, pos, w):
    S = p.shape[0]
    invf, sign = _rope_consts()
    prep_rows = [_row_spec(p), _row_spec(pos, diff=False)]
    prep_pars = [_par_spec(w['b_q_norm'].reshape(1, -1)), _par_spec(w['b_k_norm'].reshape(1, -1)),
                 _par_spec(invf, diff=False), _par_spec(sign, diff=False)]
    nqk = len(DIL_GROUPS) * D_DIL
    (qr, kr), _ = _seq_fwd("dil_prep_fwd", f_dil_prep, _tile(S, (256, 128)), prep_rows, prep_pars, [],
                           [_out_spec(nqk), _out_spec(nqk)])
    v = p[:, D_XA + 2 * nqk:D_XA + 3 * nqk]
    res = dict(qr=qr, kr=kr, perm=[], o=[], lse=[])
    for g, (window, r) in enumerate(DIL_GROUPS):
        sl = slice(g * D_DIL, (g + 1) * D_DIL)
        qp, kp, vp = _perm(qr[:, sl], r), _perm(kr[:, sl], r), _perm(v[:, sl], r)
        rows = [_row_spec(qp), _row_spec(kp, prev=True), _row_spec(vp, prev=True)]
        (o, lse), _ = _seq_fwd("dil_attn_fwd", f_dil_attn, DIL_BLOCK, rows, [], [], [_out_spec(D_DIL), _out_spec(D_DIL)],
                               period=S // r // DIL_BLOCK)
        res['perm'].append((qp, kp, vp))
        res['o'].append(_unperm(o, r))
        res['lse'].append(_unperm(lse, r))
    mrows = [_row_spec(t) for t in res['o'] + res['lse']]
    (tok,), _ = _seq_fwd("dil_merge_fwd", f_dil_merge, _rows_of(S), mrows, [], [], [_out_spec(D_DIL)])
    res['prep'] = (prep_rows, prep_pars)
    return tok, res


def _dil_bwd(dtok, res, p):
    S = p.shape[0]
    mrows = [_row_spec(t) for t in res['o'] + res['lse']]
    dm, _ = _seq_bwd("dil_merge_bwd", f_dil_merge, _rows_of(S), mrows, [], [], [_row_spec(dtok)], [])
    dq, dk, dv = [], [], []
    for g, (window, r) in enumerate(DIL_GROUPS):
        qp, kp, vp = res['perm'][g]
        rows = [_row_spec(qp), _row_spec(kp, prev=True), _row_spec(vp, prev=True)]
        douts = [_row_spec(_perm(dm[g], r)), _row_spec(_perm(dm[3 + g], r))]
        (a, b, c), _ = _seq_bwd("dil_attn_bwd", f_dil_attn, DIL_BLOCK, rows, [], [], douts, [],
                                period=S // r // DIL_BLOCK)
        dq.append(_unperm(a, r)); dk.append(_unperm(b, r)); dv.append(_unperm(c, r))
    dqr, dkr, dv = (jnp.concatenate(t, axis=1) for t in (dq, dk, dv))
    prep_rows, prep_pars = res['prep']
    (dp,), (dqn, dkn) = _seq_bwd("dil_prep_bwd", f_dil_prep, _tile(S, (256, 128)), prep_rows, prep_pars, [],
                                 [_row_spec(dqr), _row_spec(dkr)], [])
    nqk = len(DIL_GROUPS) * D_DIL
    dp = jnp.concatenate([dp[:, :D_XA + 2 * nqk], dv], axis=1)
    return dp, dict(b_q_norm=dqn.reshape(-1), b_k_norm=dkn.reshape(-1))


def _ffn_specs(ug, uv, cw, cb):
    nt = D_FF // 256
    rows = [_row_spec(ug, w=256, c=lambda jc: jc, prev=True), _row_spec(uv, w=256, c=lambda jc: jc, prev=True)]
    pars = [_par_spec(cw[:, :D_FF], bs=(FFN_CONV, 256), idx=lambda jc: (0, jc)),
            _par_spec(cw[:, D_FF:], bs=(FFN_CONV, 256), idx=lambda jc: (0, jc)),
            _par_spec(cb[:, :D_FF], bs=(1, 256), idx=lambda jc: (0, jc)),
            _par_spec(cb[:, D_FF:], bs=(1, 256), idx=lambda jc: (0, jc))]
    return nt, rows, pars


def _device_step(x, mem, pos, w, target):
    S = x.shape[0]
    wb = {k: v.astype(BF) for k, v in w.items() if v.ndim >= 2 and k in SHARD_AXIS}
    posf = pos.reshape(S, 1).astype(F32)
    n_mix = {0: 2 * GLA_HEADS * GLA_DK + 2 * D_MIX + GLA_RANK, 1: 3 * len(DIL_GROUPS) * D_DIL,
             2: 2 * D_MIX + 2 * SSM_GROUPS * SSM_STATE + SSM_HEADS, 3: 2 * HGRN_HEADS * HGRN_DK + 2 * D_MIX}
    w_in_name = {0: 'a_w_in', 1: 'b_w_in', 2: 'c_w_in', 3: 'd_w_in'}
    w_out_name = {0: 'a_w_out', 1: 'b_w_out', 2: 'c_w_out', 3: 'd_w_out'}
    mem_g = w['mem_norm'].reshape(1, -1)
    mem_n = _norm_fwd(mem, mem_g)
    R = _rows_of(S)

    saved = []
    for i in range(DEPTH):
        kind = i % 4
        L = dict(x0=x)
        g1 = w['mix_norm'][i].reshape(1, -1)
        h = _norm_fwd(x, g1)
        wcat = _w_cat(wb[w_in_name[kind]], n_mix[kind])
        p = _matmul(h, wcat, name="matmul_in")
        if kind == 1:
            tok, L['dil'] = _dil_fwd(p, posf, w)
        else:
            f, Rm, rows, pars, sshapes, _ = _mixer_specs(kind, S, p, w)
            (tok,), L['states'] = _seq_fwd("mixer%d_fwd" % kind, f, Rm, rows, pars, sshapes, [_out_spec(D_MIX)],
                                           save_states=True)
        kv = _matmul(mem_n, wb['xa_w_kv'][i], name="matmul_kv")
        xa_rows = [_row_spec(p, w=D_XA, dn=D_XA)]
        xa_pars = [_par_spec(kv), _par_spec(w['xa_q_norm'][i].reshape(1, -1)), _par_spec(w['xa_k_norm'][i].reshape(1, -1))]
        (xa,), _ = _seq_fwd("xattn_fwd", f_xattn, R, xa_rows, xa_pars, [], [_out_spec(D_XA)])
        cat = jnp.concatenate([tok, xa], axis=1).astype(BF)
        x1 = _matmul(cat, wb[w_out_name[kind]], add=x, name="matmul_out")
        g2 = w['ffn_norm'][i].reshape(1, -1)
        h2 = _norm_fwd(x1, g2)
        wup = wb['ffn_w_up'][i]
        ug = _matmul(h2, wup[:, :D_FF], name="matmul_up")
        uv = _matmul(h2, wup[:, D_FF:], name="matmul_up")
        cw, cb = w['ffn_conv_w'][i], w['ffn_conv_b'][i].reshape(1, -1)
        nt, frows, fpars = _ffn_specs(ug, uv, cw, cb)
        (act,), _ = _seq_fwd("ffn_act_fwd", f_ffn_act, R, frows, fpars, [], [_out_spec(D_FF, w=256, c=lambda jc: jc, dt=BF)],
                             ncol=nt)
        x = _matmul(act, wb['ffn_w_down'][i], add=x1, name="matmul_down")
        L.update(h=h, p=p, wcat=wcat, kv=kv, cat=cat, x1=x1, h2=h2, ug=ug, uv=uv, act=act, g1=g1, g2=g2)
        saved.append(L)

    dx, loss = _loss_head(x, target)

    G = {}
    d_mem_n = None
    dkv_all, dqn_all, dkn_all = [], [], []
    acc = {k: [None] * DEPTH for k in ('mix_norm', 'ffn_norm', 'ffn_w_up', 'ffn_conv_w', 'ffn_conv_b', 'ffn_w_down',
                                        'xa_w_kv', 'xa_q_norm', 'xa_k_norm')}
    for i in reversed(range(DEPTH)):
        kind = i % 4
        L = saved[i]
        dxb = dx.astype(BF)
        acc['ffn_w_down'][i] = _matmul(L['act'], dxb, mode="tn", name="matmul_dw_down")
        dact = _matmul(dxb, wb['ffn_w_down'][i], mode="nt", name="matmul_dact")
        cw, cb = w['ffn_conv_w'][i], w['ffn_conv_b'][i].reshape(1, -1)
        nt, frows, fpars = _ffn_specs(L['ug'], L['uv'], cw, cb)
        (dug, duv), (dwg, dwv, dbg, dbv) = _seq_bwd(
            "ffn_act_bwd", f_ffn_act, R, frows, fpars, [], [_row_spec(dact, w=256, c=lambda jc: jc)], [], ncol=nt, dx_dt=BF)
        acc['ffn_conv_w'][i] = jnp.concatenate([dwg, dwv], axis=1)
        acc['ffn_conv_b'][i] = jnp.concatenate([dbg, dbv], axis=1).reshape(-1)
        wup = wb['ffn_w_up'][i]
        acc['ffn_w_up'][i] = jnp.concatenate([_matmul(L['h2'], dug, mode="tn", name="matmul_dw_up"),
                                              _matmul(L['h2'], duv, mode="tn", name="matmul_dw_up")], axis=1)
        dh2 = _matmul(dug, wup[:, :D_FF], mode="nt", name="matmul_dh2")
        dh2 = _matmul(duv, wup[:, D_FF:], mode="nt", add=dh2, name="matmul_dh2b")
        dx1n, dg2 = _norm_bwd(L['x1'], L['g2'], dh2)
        acc['ffn_norm'][i] = dg2.reshape(-1)
        dx1 = _add(dx, dx1n)
        dx1b = dx1.astype(BF)
        G_out = _matmul(L['cat'], dx1b, mode="tn", name="matmul_dw_out")
        dcat = _matmul(dx1b, wb[w_out_name[kind]], mode="nt", name="matmul_dcat")
        ntok = D_DIL if kind == 1 else D_MIX
        dtok, dxa = dcat[:, :ntok], dcat[:, ntok:]
        p = L['p']
        xa_rows = [_row_spec(p, w=D_XA, dn=D_XA)]
        xa_pars = [_par_spec(L['kv']), _par_spec(w['xa_q_norm'][i].reshape(1, -1)), _par_spec(w['xa_k_norm'][i].reshape(1, -1))]
        (dxq,), (dkv, dqn, dkn) = _seq_bwd("xattn_bwd", f_xattn, R, xa_rows, xa_pars, [], [_row_spec(dxa)], [])
        acc['xa_q_norm'][i], acc['xa_k_norm'][i] = dqn.reshape(-1), dkn.reshape(-1)
        acc['xa_w_kv'][i] = _matmul(mem_n, dkv, mode="tn", name="matmul_dw_kv")
        d_mem_n = _matmul(dkv, wb['xa_w_kv'][i], mode="nt", add=d_mem_n, name="matmul_dmem" + ("" if d_mem_n is None else "_acc"))
        if kind == 1:
            dp, gm = _dil_bwd(dtok, L['dil'], p)
            G.update(gm)
        else:
            f, Rm, rows, pars, sshapes, _ = _mixer_specs(kind, S, p, w)
            (dp,), dps = _seq_bwd("mixer%d_bwd" % kind, f, Rm, rows, pars, sshapes, [_row_spec(dtok)], L['states'])
            if kind == 0:
                G['a_w_gate2'], G['a_b_gate'], G['a_o_norm'] = dps[0], dps[1].reshape(-1), dps[2].reshape(-1)
            elif kind == 2:
                G['c_conv_w'] = dps[0]
                for nme, v in zip(('c_conv_b', 'c_dt_bias', 'c_a_log', 'c_d', 'c_norm'), dps[1:]):
                    G[nme] = v.reshape(-1)
            else:
                G['d_lower_bounds'], G['d_o_norm'] = dps[0], dps[1].reshape(-1)
        dp = jnp.concatenate([dxq, dp[:, D_XA:]], axis=1).astype(BF)
        dwcat = _matmul(L['h'], dp, mode="tn", name="matmul_dw_in")
        G[w_in_name[kind]] = _dw_uncat(dwcat, n_mix[kind])
        G[w_out_name[kind]] = G_out
        dh = _matmul(dp, L['wcat'], mode="nt", name="matmul_dh")
        dxn, dg1 = _norm_bwd(L['x0'], L['g1'], dh)
        acc['mix_norm'][i] = dg1.reshape(-1)
        dx = _add(dx1, dxn)

    _, dmg = _norm_bwd(mem, mem_g, d_mem_n)
    G['mem_norm'] = dmg.reshape(-1)
    for k, v in acc.items():
        G[k] = jnp.stack(v)
    return loss, dx, G


def _add(a, b):
    S, D = a.shape
    R = _rows_of(S)

    def body(a_ref, b_ref, o_ref):
        o_ref[...] = a_ref[...] + b_ref[...]

    blk = pl.BlockSpec((R, D), lambda i: (i, 0))
    return pl.pallas_call(body, name="residual_add", grid=(S // R,), in_specs=[blk, blk], out_specs=blk,
                          out_shape=jax.ShapeDtypeStruct((S, D), F32),
                          compiler_params=pltpu.CompilerParams(dimension_semantics=("arbitrary",)))(a, b)


def kernel(x, mem, positions, mem_norm, mix_norm, xa_w_kv, xa_q_norm, xa_k_norm, ffn_norm, ffn_w_up, ffn_conv_w, ffn_conv_b, ffn_w_down, a_w_in, a_w_gate2, a_b_gate, a_o_norm, a_w_out, b_w_in, b_q_norm, b_k_norm, b_w_out, c_w_in, c_conv_w, c_conv_b, c_dt_bias, c_a_log, c_d, c_norm, c_w_out, d_w_in, d_lower_bounds, d_o_norm, d_w_out, loss_target, m_mem_norm, m_mix_norm, m_xa_w_kv, m_xa_q_norm, m_xa_k_norm, m_ffn_norm, m_ffn_w_up, m_ffn_conv_w, m_ffn_conv_b, m_ffn_w_down, m_a_w_in, m_a_w_gate2, m_a_b_gate, m_a_o_norm, m_a_w_out, m_b_w_in, m_b_q_norm, m_b_k_norm, m_b_w_out, m_c_w_in, m_c_conv_w, m_c_conv_b, m_c_dt_bias, m_c_a_log, m_c_d, m_c_norm, m_c_w_out, m_d_w_in, m_d_lower_bounds, m_d_o_norm, m_d_w_out, v_mem_norm, v_mix_norm, v_xa_w_kv, v_xa_q_norm, v_xa_k_norm, v_ffn_norm, v_ffn_w_up, v_ffn_conv_w, v_ffn_conv_b, v_ffn_w_down, v_a_w_in, v_a_w_gate2, v_a_b_gate, v_a_o_norm, v_a_w_out, v_b_w_in, v_b_q_norm, v_b_k_norm, v_b_w_out, v_c_w_in, v_c_conv_w, v_c_conv_b, v_c_dt_bias, v_c_a_log, v_c_d, v_c_norm, v_c_w_out, v_d_w_in, v_d_lower_bounds, v_d_o_norm, v_d_w_out):
    args = locals()
    w = {n: args[n] for n in WEIGHTS}
    m = {n: args['m_' + n] for n in WEIGHTS}
    v = {n: args['v_' + n] for n in WEIGHTS}

    big = [n for n in SHARDED if w[n].ndim >= 2 and w[n].size >= 65536]
    small = [n for n in SHARDED if n not in big]
    shard_shapes = {n: tuple(w[n].shape) for n in SHARDED}
    g_big = _all_gather(_pack([w[n] for n in big], BF), "gather_weights")
    g_small = _all_gather(_pack([w[n] for n in small], F32), "gather_small_weights")
    full = {n: w[n] for n in REPLICATED}
    for n, t in zip(big, _unpack_lead(g_big, [shard_shapes[n] for n in big])):
        full[n] = _to_full(t, SHARD_AXIS[n]).astype(F32)
    for n, t in zip(small, _unpack_lead(g_small, [shard_shapes[n] for n in small])):
        full[n] = _to_full(t, SHARD_AXIS[n])

    loss, grad_x, G = _device_step(x[0], mem[0], positions[0], full, loss_target[0])
    loss = lax.psum(loss, ("x", "y", "c"))

    sent = _pack_lead([_to_chunks(G[n], SHARD_AXIS[n]) for n in SHARDED], BF)
    parts = _exchange(sent, "exchange_grads")
    rep = _all_gather(_pack([G[n] for n in REPLICATED], F32), "gather_replicated_grads")

    out = {}
    for names, prt, tag in ((SHARDED, parts, "adamw_sharded"), (REPLICATED, rep, "adamw_replicated")):
        shapes = [tuple(w[n].shape) for n in names]
        res = _adamw(prt, _pack([w[n] for n in names], F32), _pack([m[n] for n in names], F32),
                     _pack([v[n] for n in names], F32), tag)
        for kind, r in zip(("grad", "delta", "new_m", "new_v"), res):
            for n, t in zip(names, _unpack(r, shapes)):
                out[kind + "_" + n] = t
    return (loss, grad_x[None], *[out["grad_" + n] for n in WEIGHTS], *[out["delta_" + n] for n in WEIGHTS],
            *[out["new_m_" + n] for n in WEIGHTS], *[out["new_v_" + n] for n in WEIGHTS])
```

```python
import functools
import math

import jax
import jax.numpy as jnp
from jax import lax
from jax.experimental import pallas as pl
from jax.experimental.pallas import tpu as pltpu

F32 = jnp.float32
BF = jnp.bfloat16
_MM_DTYPE = BF

N_DEV = 8
EPS = 1e-6
ROPE_THETA = 10000.0
CHUNK = 64
D_MIX = 768
XA_HEADS, XA_HD, D_XA = 4, 64, 256
GLA_HEADS, GLA_DK, GLA_DV, GLA_RANK, GLA_GATE_NORM = 4, 96, 192, 16, 16.0
DIL_GROUPS = ((128, 1), (512, 4), (2048, 16))
DIL_HEADS, DIL_HD, DIL_BLOCK, D_DIL = 4, 128, 128, 512
SSM_HD, SSM_HEADS, SSM_GROUPS, SSM_STATE, SSM_CONV = 64, 12, 2, 128, 4
HGRN_HEADS, HGRN_DK, HGRN_DV = 6, 128, 128
D_FF = 2816
FFN_CONV = 3
DEPTH = 4
ADAM_LR, ADAM_B1, ADAM_B2, ADAM_EPS, ADAM_WD, ADAM_STEP = 0.001, 0.9, 0.999, 1e-08, 0.01, 10
NEG = -1e30
VMEM_LIMIT = 56 << 20
ADAM_BLOCK = 1 << 18

WEIGHTS = ['mem_norm', 'mix_norm', 'xa_w_kv', 'xa_q_norm', 'xa_k_norm', 'ffn_norm', 'ffn_w_up', 'ffn_conv_w',
           'ffn_conv_b', 'ffn_w_down', 'a_w_in', 'a_w_gate2', 'a_b_gate', 'a_o_norm', 'a_w_out', 'b_w_in', 'b_q_norm',
           'b_k_norm', 'b_w_out', 'c_w_in', 'c_conv_w', 'c_conv_b', 'c_dt_bias', 'c_a_log', 'c_d', 'c_norm', 'c_w_out',
           'd_w_in', 'd_lower_bounds', 'd_o_norm', 'd_w_out']
SHARD_AXIS = {'xa_w_kv': 1, 'ffn_w_up': 2, 'ffn_conv_w': 2, 'ffn_w_down': 1, 'a_w_in': 1, 'a_w_gate2': 1, 'a_w_out': 0,
              'b_w_in': 1, 'b_w_out': 1, 'c_w_in': 0, 'c_conv_w': 1, 'c_w_out': 0, 'd_w_in': 1, 'd_w_out': 0}
SHARDED = [n for n in WEIGHTS if n in SHARD_AXIS]
REPLICATED = [n for n in WEIGHTS if n not in SHARD_AXIS]


def _dot(a, b, ca, cb):
    return lax.dot_general(a.astype(_MM_DTYPE), b.astype(_MM_DTYPE), (((ca,), (cb,)), ((), ())),
                           preferred_element_type=F32)


@jax.custom_vjp
def mm_nn(a, b):
    return _dot(a, b, 1, 0)


mm_nn.defvjp(lambda a, b: (_dot(a, b, 1, 0), (a, b)),
             lambda r, g: (_dot(g, r[1], 1, 1), _dot(r[0], g, 0, 0)))


@jax.custom_vjp
def mm_nt(a, b):
    return _dot(a, b, 1, 1)


mm_nt.defvjp(lambda a, b: (_dot(a, b, 1, 1), (a, b)),
             lambda r, g: (_dot(g, r[1], 1, 0), _dot(g, r[0], 0, 0)))


@jax.custom_vjp
def mm_tn(a, b):
    return _dot(a, b, 0, 0)


mm_tn.defvjp(lambda a, b: (_dot(a, b, 0, 0), (a, b)),
             lambda r, g: (_dot(r[1], g, 1, 1), _dot(r[0], g, 1, 0)))


def _dot_hi(a, b, ca, cb):
    return lax.dot_general(a, b, (((ca,), (cb,)), ((), ())), precision=lax.Precision.HIGHEST,
                           preferred_element_type=F32)


def _tril(c):
    return lax.broadcasted_iota(jnp.int32, (c, c), 0) >= lax.broadcasted_iota(jnp.int32, (c, c), 1)


@jax.custom_vjp
def cumsum_rows(x):
    return _dot_hi(_tril(x.shape[0]).astype(F32), x, 1, 0)


cumsum_rows.defvjp(lambda x: (cumsum_rows(x), None),
                   lambda r, g: (_dot_hi(_tril(g.shape[0]).astype(F32), g, 0, 0),))


@jax.custom_vjp
def cumsum_rows_t(x):
    return _dot_hi(x, _tril(x.shape[0]).astype(F32), 0, 1)


cumsum_rows_t.defvjp(lambda x: (cumsum_rows_t(x), None),
                     lambda r, g: (_dot_hi(_tril(g.shape[1]).astype(F32), g, 0, 1),))


def _split(x, sizes):
    sizes = tuple(int(s) for s in sizes)
    assert sum(sizes) == x.shape[-1], (sizes, x.shape)

    @jax.custom_vjp
    def sp(x):
        out, o = [], 0
        for s in sizes:
            out.append(x[:, o:o + s])
            o += s
        return tuple(out)

    sp.defvjp(lambda x: (sp(x), None), lambda r, g: (jnp.concatenate(list(g), axis=1),))
    return sp(x)


def _row(x, r):
    m = lax.broadcasted_iota(jnp.int32, x.shape, 0) == r
    return jnp.sum(jnp.where(m, x, 0.0), axis=0, keepdims=True)


@jax.custom_vjp
def _roll_half(x):
    return pltpu.roll(x, 64, 1)


_roll_half.defvjp(lambda x: (pltpu.roll(x, 64, 1), None), lambda r, g: (pltpu.roll(g, 64, 1),))


def _shift(xp, x, d):
    if d == 0:
        return x
    n = x.shape[0]

    @jax.custom_vjp
    def sh(xp, x):
        row = lax.broadcasted_iota(jnp.int32, x.shape, 0)
        return jnp.where(row < d, pltpu.roll(xp, d, 0), pltpu.roll(x, d, 0))

    def bwd(_, g):
        row = lax.broadcasted_iota(jnp.int32, g.shape, 0)
        r = pltpu.roll(g, n - d, 0)
        return jnp.where(row >= n - d, r, 0.0), jnp.where(row < n - d, r, 0.0)

    sh.defvjp(lambda xp, x: (sh(xp, x), None), bwd)
    return sh(xp, x)


def _rms(x, g):
    return x * lax.rsqrt(jnp.mean(x * x, axis=-1, keepdims=True) + EPS) * g


def _lane_pair(a, b, width=128):
    shape = a.shape[:-1] + (width,)
    lane = lax.broadcasted_iota(jnp.int32, shape, len(shape) - 1)
    return jnp.where(lane < width // 2, a, b)


def _row_spec(a, w=None, c=None, prev=False, diff=True, dn=None, lb=(), li=None):
    return dict(a=a, w=a.shape[-1] if w is None else w, c=(lambda jc: 0) if c is None else c, prev=prev, diff=diff,
                dn=a.shape[-1] if dn is None else dn, lb=tuple(lb), li=(lambda jc: ()) if li is None else li)


def _par_spec(a, bs=None, idx=None, diff=True):
    nd = a.ndim
    return dict(a=a, bs=tuple(a.shape) if bs is None else tuple(bs),
                idx=(lambda jc: (0,) * nd) if idx is None else idx, diff=diff)


def _out_spec(n, w=None, c=None, dt=F32, ls=(), lb=(), li=None):
    return dict(n=n, w=n if w is None else w, c=(lambda jc: 0) if c is None else c, dt=dt, ls=tuple(ls), lb=tuple(lb),
                li=(lambda jc: ()) if li is None else li)


def _cparams():
    return pltpu.CompilerParams(dimension_semantics=("arbitrary", "arbitrary"), vmem_limit_bytes=VMEM_LIMIT)


def _bspec(s, R, rowfn):
    return pl.BlockSpec(s['lb'] + (R, s['w']),
                        functools.partial(lambda jc, i, s: tuple(s['li'](jc)) + (rowfn(i), s['c'](jc)), s=s))


def _seq_fwd(name, f, R, rows, params, state_shapes, outs, *, ncol=1, period=None, save_states=False):
    nrows = rows[0]['a'].shape[-2]
    nb = nrows // R
    assert nb * R == nrows
    period = nb if period is None else period
    prev_ids = [k for k, r in enumerate(rows) if r['prev']]
    n_rows, n_prev, n_par, n_out, n_st = len(rows), len(prev_ids), len(params), len(outs), len(state_shapes)

    def body(*refs):
        o = 0
        cur = refs[o:o + n_rows]; o += n_rows
        prv = refs[o:o + n_prev]; o += n_prev
        par = refs[o:o + n_par]; o += n_par
        out = refs[o:o + n_out]; o += n_out
        sav = refs[o:o + (n_st if save_states else 0)]; o += len(sav)
        st = refs[o:o + n_st]
        i = pl.program_id(1)
        first = (i % period) == 0

        @pl.when(i == 0)
        def _():
            for s in st:
                s[...] = jnp.zeros_like(s)

        xs = [r[...].astype(F32) for r in cur]
        xp = [r[...].astype(F32) for r in prv]
        ps = [r[...] for r in par]
        sts = [s[...] for s in st]
        for sv, s in zip(sav, sts):
            sv[0] = s
        ov, ns = f(first, xp, xs, ps, sts)
        for r, v in zip(out, ov):
            r[...] = v.astype(r.dtype)
        for s, v in zip(st, ns):
            s[...] = v

    in_specs = [_bspec(r, R, lambda i: i) for r in rows]
    in_specs += [_bspec(rows[k], R, lambda i: jnp.maximum(i - 1, 0)) for k in prev_ids]
    in_specs += [pl.BlockSpec(p['bs'], functools.partial(lambda jc, i, idx: idx(jc), idx=p['idx'])) for p in params]
    out_specs = [_bspec(o_, R, lambda i: i) for o_ in outs]
    out_shape = [jax.ShapeDtypeStruct(o_['ls'] + (nrows, o_['n']), o_['dt']) for o_ in outs]
    if save_states:
        for s in state_shapes:
            out_specs.append(pl.BlockSpec((1,) + tuple(s), lambda jc, i, nd=len(s): (i,) + (0,) * nd))
            out_shape.append(jax.ShapeDtypeStruct((nb,) + tuple(s), F32))
    args = [r['a'] for r in rows] + [rows[k]['a'] for k in prev_ids] + [p['a'] for p in params]
    res = pl.pallas_call(
        body, name=name, grid=(ncol, nb), in_specs=in_specs, out_specs=out_specs, out_shape=out_shape,
        scratch_shapes=[pltpu.VMEM(tuple(s), F32) for s in state_shapes], compiler_params=_cparams())(*args)
    return list(res[:n_out]), list(res[n_out:])


def _seq_bwd(name, f, R, rows, params, state_shapes, douts, saved, *, ncol=1, period=None, dx_dt=F32):
    nrows = rows[0]['a'].shape[-2]
    nb = nrows // R
    period = nb if period is None else period
    prev_ids = [k for k, r in enumerate(rows) if r['prev']]
    drow_ids = [k for k, r in enumerate(rows) if r['diff']]
    dpar_ids = [k for k, p in enumerate(params) if p['diff']]
    for k in prev_ids:
        assert rows[k]['diff']
    n_rows, n_prev, n_par, n_do, n_st = len(rows), len(prev_ids), len(params), len(douts), len(state_shapes)
    n_dx, n_dp = len(drow_ids), len(dpar_ids)

    def body(*refs):
        o = 0
        cur = refs[o:o + n_rows]; o += n_rows
        prv = refs[o:o + n_prev]; o += n_prev
        par = refs[o:o + n_par]; o += n_par
        sav = refs[o:o + n_st]; o += n_st
        dou = refs[o:o + n_do]; o += n_do
        dxr = refs[o:o + n_dx]; o += n_dx
        dpr = refs[o:o + n_dp]; o += n_dp
        dst = refs[o:o + n_st]; o += n_st
        car = refs[o:o + n_prev]
        j = pl.program_id(1)
        i = nb - 1 - j
        first = (i % period) == 0

        @pl.when(j == 0)
        def _():
            for s in tuple(dst) + tuple(car) + tuple(dpr):
                s[...] = jnp.zeros_like(s)

        xs = [r[...].astype(F32) for r in cur]
        xp = [r[...].astype(F32) for r in prv]
        ps = [r[...] for r in par]
        sts = [s[0] for s in sav]

        def g(dxs, dxp, dps, dsts):
            xs_, ps_ = list(xs), list(ps)
            for k, v in zip(drow_ids, dxs):
                xs_[k] = v
            for k, v in zip(dpar_ids, dps):
                ps_[k] = v
            ov, ns = f(first, list(dxp), xs_, ps_, list(dsts))
            return tuple(ov), tuple(ns)

        _, vjp = jax.vjp(g, tuple(xs[k] for k in drow_ids), tuple(xp), tuple(ps[k] for k in dpar_ids), tuple(sts))
        dxs, dxp, dps, dsts = vjp((tuple(r[...].astype(F32) for r in dou), tuple(s[...] for s in dst)))
        dxs = list(dxs)
        for n_, k in enumerate(prev_ids):
            pos = drow_ids.index(k)
            dxs[pos] = dxs[pos] + car[n_][...]
            car[n_][...] = dxp[n_]
        for r, v in zip(dxr, dxs):
            r[...] = v.astype(r.dtype)
        for r, v in zip(dpr, dps):
            r[...] += v
        for s, v in zip(dst, dsts):
            s[...] = v

    def rev(j):
        return nb - 1 - j

    def rev_prev(j):
        return jnp.maximum(nb - 2 - j, 0)

    in_specs = [_bspec(r, R, rev) for r in rows]
    in_specs += [_bspec(rows[k], R, rev_prev) for k in prev_ids]
    in_specs += [pl.BlockSpec(p['bs'], functools.partial(lambda jc, j, idx: idx(jc), idx=p['idx'])) for p in params]
    in_specs += [pl.BlockSpec((1,) + tuple(s), lambda jc, j, nd=len(s): (nb - 1 - j,) + (0,) * nd) for s in state_shapes]
    in_specs += [_bspec(d, R, rev) for d in douts]
    out_specs = [_bspec(rows[k], R, rev) for k in drow_ids]
    out_shape = [jax.ShapeDtypeStruct(tuple(rows[k]['a'].shape[:-1]) + (rows[k]['dn'],), dx_dt) for k in drow_ids]
    for k in dpar_ids:
        p = params[k]
        out_specs.append(pl.BlockSpec(p['bs'], functools.partial(lambda jc, j, idx: idx(jc), idx=p['idx'])))
        out_shape.append(jax.ShapeDtypeStruct(p['a'].shape, F32))
    scratch = [pltpu.VMEM(tuple(s), F32) for s in state_shapes]
    scratch += [pltpu.VMEM(tuple(d for d in rows[k]['lb'] if d is not None) + (R, rows[k]['w']), F32) for k in prev_ids]
    args = ([r['a'] for r in rows] + [rows[k]['a'] for k in prev_ids] + [p['a'] for p in params] + list(saved)
            + [d['a'] for d in douts])
    res = pl.pallas_call(
        body, name=name, grid=(ncol, nb), in_specs=in_specs, out_specs=out_specs, out_shape=out_shape,
        scratch_shapes=scratch, compiler_params=_cparams())(*args)
    return list(res[:n_dx]), list(res[n_dx:])


def _tile(n, cands):
    for c in cands:
        if n % c == 0:
            return c
    return n


def _mm_call(name, grid, a, a_spec, b, b_spec, contract, out_shape, out_spec, acc_shape, add=None, add_spec=None):
    nk = grid[2]
    ca, cb = contract
    has_add = add is not None

    def body(*refs):
        a_ref, b_ref = refs[0], refs[1]
        add_ref = refs[2] if has_add else None
        o_ref, acc = refs[-2], refs[-1]
        k = pl.program_id(2)
        part = _dot(a_ref[...], b_ref[...], ca, cb)

        @pl.when(k == 0)
        def _():
            acc[...] = part

        @pl.when(k > 0)
        def _():
            acc[...] += part

        @pl.when(k == nk - 1)
        def _():
            r = acc[...]
            if has_add:
                r = r + add_ref[...].astype(F32)
            o_ref[...] = r.astype(o_ref.dtype)

    in_specs, args = [a_spec, b_spec], [a, b]
    if has_add:
        in_specs.append(add_spec)
        args.append(add)
    return pl.pallas_call(
        body, name=name, grid=grid, in_specs=in_specs, out_specs=out_spec, out_shape=out_shape,
        scratch_shapes=[pltpu.VMEM(acc_shape, F32)],
        compiler_params=pltpu.CompilerParams(dimension_semantics=("parallel", "parallel", "arbitrary"),
                                             vmem_limit_bytes=VMEM_LIMIT))(*args)


def _matmul(a, b, mode="nn", add=None, out_dtype=F32, name="matmul"):
    if mode == "nn":
        (M, K), N = a.shape, b.shape[1]
    elif mode == "nt":
        (M, K), N = a.shape, b.shape[0]
    else:
        (K, M), N = a.shape, b.shape[1]
    tm = _tile(M, (1024, 512, 256, 128, 64, 32, 16, 8))
    tn = _tile(N, (512, 256, 128))
    tk = K if K <= 2816 else _tile(K, (2048, 1024, 512, 256, 128))
    if mode == "tn":
        a_spec = pl.BlockSpec((tk, tm), lambda i, j, k: (k, i))
    else:
        a_spec = pl.BlockSpec((tm, tk), lambda i, j, k: (i, k))
    if mode == "nt":
        b_spec = pl.BlockSpec((tn, tk), lambda i, j, k: (j, k))
    else:
        b_spec = pl.BlockSpec((tk, tn), lambda i, j, k: (k, j))
    blk = pl.BlockSpec((tm, tn), lambda i, j, k: (i, j))
    return _mm_call(name, (M // tm, N // tn, K // tk), a, a_spec, b, b_spec,
                    {"nn": (1, 0), "nt": (1, 1), "tn": (0, 0)}[mode], jax.ShapeDtypeStruct((M, N), out_dtype), blk,
                    (tm, tn), add, blk)


FF_SH = 2 * D_FF // N_DEV


def _ffn_up(h2, wup, i):
    S, D = h2.shape
    tm = _tile(S, (1024, 512, 256, 128))
    return _mm_call("matmul_up", (S // tm, N_DEV, 1), h2, pl.BlockSpec((tm, D), lambda m, j, k: (m, 0)),
                    wup, pl.BlockSpec((None, None, D, FF_SH), lambda m, j, k: (j, i, 0, 0)), (1, 0),
                    jax.ShapeDtypeStruct((2, N_DEV // 2, S, FF_SH), F32),
                    pl.BlockSpec((None, None, tm, FF_SH), lambda m, j, k: (j // 4, j % 4, m, 0)), (tm, FF_SH))


def _ffn_down(act, wd, x1):
    _, S, _ = act.shape
    D = wd.shape[1]
    tm, tn = _tile(S, (1024, 512, 256, 128)), _tile(D, (512, 256, 128))
    blk = pl.BlockSpec((tm, tn), lambda m, n, k: (m, n))
    return _mm_call("matmul_down", (S // tm, D // tn, N_DEV // 2), act,
                    pl.BlockSpec((None, tm, FF_SH), lambda m, n, k: (k, m, 0)), wd,
                    pl.BlockSpec((FF_SH, tn), lambda m, n, k: (k, n)), (1, 0), jax.ShapeDtypeStruct((S, D), F32), blk,
                    (tm, tn), x1, blk)


def _ffn_dact(dxb, wd):
    S, D = dxb.shape
    tm = _tile(S, (1024, 512, 256, 128))
    return _mm_call("matmul_dact", (S // tm, N_DEV // 2, 1), dxb, pl.BlockSpec((tm, D), lambda m, j, k: (m, 0)), wd,
                    pl.BlockSpec((FF_SH, D), lambda m, j, k: (j, 0)), (1, 1),
                    jax.ShapeDtypeStruct((N_DEV // 2, S, FF_SH), F32),
                    pl.BlockSpec((None, tm, FF_SH), lambda m, j, k: (j, m, 0)), (tm, FF_SH))


def _ffn_dw_down(act, dxb):
    _, S, _ = act.shape
    D = dxb.shape[1]
    tk, tn = _tile(S, (2048, 1024, 512, 256, 128)), _tile(D, (512, 256, 128))
    return _mm_call("matmul_dw_down", (N_DEV // 2, D // tn, S // tk), act,
                    pl.BlockSpec((None, tk, FF_SH), lambda j, n, k: (j, k, 0)), dxb,
                    pl.BlockSpec((tk, tn), lambda j, n, k: (k, n)), (0, 0), jax.ShapeDtypeStruct((D_FF, D), BF),
                    pl.BlockSpec((FF_SH, tn), lambda j, n, k: (j, n)), (FF_SH, tn))


def _ffn_dw_up(h2, du):
    S, D = h2.shape
    tk = _tile(S, (1024, 512, 256, 128))
    return _mm_call("matmul_dw_up", (N_DEV, 1, S // tk), h2, pl.BlockSpec((tk, D), lambda j, n, k: (k, 0)), du,
                    pl.BlockSpec((None, None, tk, FF_SH), lambda j, n, k: (j // 4, j % 4, k, 0)), (0, 0),
                    jax.ShapeDtypeStruct((N_DEV, D, FF_SH), BF),
                    pl.BlockSpec((None, D, FF_SH), lambda j, n, k: (j, 0, 0)), (D, FF_SH))


def _ffn_dh2(du, wup, i):
    S = du.shape[2]
    D = wup.shape[2]
    tm = _tile(S, (1024, 512, 256, 128))
    return _mm_call("matmul_dh2", (S // tm, 1, N_DEV), du,
                    pl.BlockSpec((None, None, tm, FF_SH), lambda m, n, k: (k // 4, k % 4, m, 0)), wup,
                    pl.BlockSpec((None, None, D, FF_SH), lambda m, n, k: (k, i, 0, 0)), (1, 1),
                    jax.ShapeDtypeStruct((S, D), F32), pl.BlockSpec((tm, D), lambda m, n, k: (m, 0)), (tm, D))


def f_rmsnorm(first, xp, xs, ps, sts):
    return (_rms(xs[0], ps[0]),), ()


def f_xattn(first, xp, xs, ps, sts):
    (xq,), (kv, qn, kn) = xs, ps
    qs = _split(xq, [XA_HD] * XA_HEADS)
    kvs = _split(kv, [XA_HD] * (2 * XA_HEADS))
    outs = []
    for h in range(XA_HEADS):
        q = _rms(qs[h], qn)
        k = _rms(kvs[h], kn)
        v = kvs[XA_HEADS + h]
        s = mm_nt(q, k) * (XA_HD ** -0.5)
        m = lax.stop_gradient(jnp.max(s, axis=-1, keepdims=True))
        p = jnp.exp(s - m)
        outs.append(mm_nn(p / jnp.sum(p, axis=-1, keepdims=True), v))
    return (jnp.concatenate(outs, axis=1),), ()


def _conv(xp, x, w, b, first, taps):
    xp = jnp.where(first, 0.0, xp)
    y = b + w[taps - 1:taps] * x
    for d in range(1, taps):
        y = y + w[taps - 1 - d:taps - d] * _shift(xp, x, d)
    return y


def _unstack2(x):
    @jax.custom_vjp
    def us(x):
        return x[0], x[1]

    us.defvjp(lambda x: (us(x), None), lambda r, g: (jnp.stack(g),))
    return us(x)


def f_ffn_act(first, xp, xs, ps, sts):
    (up,), (u,), (wg, wv, bg, bv) = xp, xs, ps
    (ugp, uvp), (ug, uv) = _unstack2(up), _unstack2(u)
    gate = _conv(ugp, ug, wg, bg, first, FFN_CONV)
    val = _conv(uvp, uv, wv, bv, first, FFN_CONV)
    return (jax.nn.silu(gate) * val,), ()


def _gla_chunk(q, k, v, la, st):
    c = q.shape[0]
    b = cumsum_rows(la)
    b_last = _row(b, c - 1)
    b_ref = _row(b, c // 2 - 1)
    att = mm_nt(q * jnp.exp(b - b_ref), k * jnp.exp(b_ref - b))
    att = jnp.where(_tril(c), att, 0.0)
    o = mm_nn(att, v) + mm_nt(q * jnp.exp(b), st)
    st_new = st * jnp.exp(b_last) + mm_tn(v, k * jnp.exp(b_last - b))
    return o, st_new


def _a_cols(ntot):
    used = D_XA + 2 * GLA_HEADS * GLA_DK + D_MIX + GLA_RANK + D_MIX
    return [D_XA, GLA_HEADS * GLA_DK, GLA_HEADS * GLA_DK, D_MIX, GLA_RANK, D_MIX] + ([ntot - used] if ntot > used else [])


def f_gla(first, xp, xs, ps, sts):
    (p,), (wg2, bg, on) = xs, ps
    parts = _split(p, _a_cols(p.shape[1]))
    q, k, v, glr, og = parts[1:6]
    la = jax.nn.log_sigmoid(mm_nn(glr, wg2) + bg) / GLA_GATE_NORM
    qs = _split(q * (GLA_DK ** -0.5), [GLA_DK] * GLA_HEADS)
    ks = _split(k, [GLA_DK] * GLA_HEADS)
    vs = _split(v, [GLA_DV] * GLA_HEADS)
    las = _split(la, [GLA_DK] * GLA_HEADS)
    outs, new = [], []
    for h in range(GLA_HEADS):
        o, s = _gla_chunk(qs[h], ks[h], vs[h], las[h], sts[h])
        outs.append(_rms(o, on))
        new.append(s)
    return (jnp.concatenate(outs, axis=1) * jax.nn.silu(og),), tuple(new)


def f_hgrn(first, xp, xs, ps, sts):
    (p,), (lbp, on) = xs, ps
    _, q, fgate, iv, og = _split(p, [D_XA, D_MIX, D_MIX, D_MIX, D_MIX])
    e = jnp.exp(lbp - jnp.max(lbp, axis=0, keepdims=True))
    row = lax.broadcasted_iota(jnp.int32, e.shape, 0)
    lb = jnp.sum(jnp.where(row >= 1, e, 0.0), axis=0, keepdims=True) / jnp.sum(e, axis=0, keepdims=True)
    fg = lb + (1.0 - lb) * jax.nn.sigmoid(fgate)
    qs = _split(jax.nn.silu(q), [HGRN_DK] * HGRN_HEADS)
    ks = _split(1.0 - fg, [HGRN_DK] * HGRN_HEADS)
    vs = _split(iv, [HGRN_DV] * HGRN_HEADS)
    las = _split(jnp.log(fg), [HGRN_DK] * HGRN_HEADS)
    outs, new = [], []
    for h in range(HGRN_HEADS):
        o, s = _gla_chunk(qs[h], ks[h], vs[h], las[h], sts[h])
        outs.append(_rms(o, on))
        new.append(s)
    return (jnp.concatenate(outs, axis=1) * jax.nn.sigmoid(og),), tuple(new)


def _c_cols(ntot):
    gn = SSM_GROUPS * SSM_STATE
    used = D_XA + D_MIX + D_MIX + 2 * gn + SSM_HEADS
    return [D_XA, D_MIX, D_MIX + 2 * gn, SSM_HEADS] + ([ntot - used] if ntot > used else [])


def f_ssd(first, xp, xs, ps, sts):
    (pp,), (p,), (cw, cb, dtb, alog, dsk, ng) = xp, xs, ps
    c = p.shape[0]
    gn = SSM_GROUPS * SSM_STATE
    hg = SSM_HEADS // SSM_GROUPS
    _, z, xbc, dtr = _split(p, _c_cols(p.shape[1]))[:4]
    xbc_p = _split(pp, _c_cols(p.shape[1]))[2]
    xbc = jax.nn.silu(_conv(xbc_p, xbc, cw, cb, first, SSM_CONV))
    xs_, bm, cm = _split(xbc, [D_MIX, gn, gn])
    dt = jax.nn.softplus(dtr + dtb)
    a = dt * (-jnp.exp(alog))
    acs = cumsum_rows(a)
    acs_t = cumsum_rows_t(a)
    acs_last = _row(acs, c - 1)
    dt_h = _split(dt, [1] * SSM_HEADS)
    acs_h = _split(acs, [1] * SSM_HEADS)
    al_h = _split(acs_last, [1] * SSM_HEADS)
    d_h = _split(dsk, [1] * SSM_HEADS)
    x2s = _split(xs_, [2 * SSM_HD] * (SSM_HEADS // 2))
    bms = _split(bm, [SSM_STATE] * SSM_GROUPS)
    cms = _split(cm, [SSM_STATE] * SSM_GROUPS)
    tril = _tril(c)
    cbs = [mm_nt(cms[g], bms[g]) for g in range(SSM_GROUPS)]
    ys, new = [], []
    for j in range(SSM_HEADS // 2):
        g = (2 * j) // hg
        h0, h1 = 2 * j, 2 * j + 1
        xdt = x2s[j] * _lane_pair(dt_h[h0], dt_h[h1])
        acs2 = _lane_pair(acs_h[h0], acs_h[h1])
        al2 = _lane_pair(al_h[h0], al_h[h1])
        yd = []
        for h in (h0, h1):
            seg = acs_h[h] - _row(acs_t, h)
            lm = jnp.exp(jnp.where(tril, seg, NEG))
            yd.append(mm_nn(cbs[g] * lm, xdt))
        lane = lax.broadcasted_iota(jnp.int32, xdt.shape, 1)
        y_diag = jnp.where(lane < SSM_HD, yd[0], yd[1])
        y_off = mm_nn(cms[g], sts[j]) * jnp.exp(acs2)
        x_end = xdt * jnp.exp(al2 - acs2)
        new.append(sts[j] * jnp.exp(al2) + mm_tn(bms[g], x_end))
        ys.append(y_diag + y_off + _lane_pair(d_h[h0], d_h[h1]) * x2s[j])
    y = jnp.concatenate(ys, axis=1) * jax.nn.silu(z)
    gw = D_MIX // SSM_GROUPS
    yg = _split(y, [gw] * SSM_GROUPS)
    ngs = _split(ng, [gw] * SSM_GROUPS)
    y = jnp.concatenate([_rms(yg[g], ngs[g]) for g in range(SSM_GROUPS)], axis=1)
    return (y,), tuple(new)


def f_dil_prep(first, xp, xs, ps, sts):
    (p, pos), (qn, kn, invf, sign) = xs, ps
    nh = len(DIL_GROUPS) * DIL_HEADS
    _, q, k, _v = _split(p, [D_XA] + [nh * DIL_HD] * 3)
    ang = pos * invf
    cos, sin = jnp.cos(ang), jnp.sin(ang) * sign

    def rope(t, g):
        hs = _split(t, [DIL_HD] * nh)
        out = []
        for h in hs:
            n = _rms(h, g)
            out.append(n * cos + _roll_half(n) * sin)
        return jnp.concatenate(out, axis=1)

    return (rope(q, qn), rope(k, kn)), ()


def f_dil_attn(first, xp, xs, ps, sts):
    (kp, vp), (q, k, v) = xp, xs
    Q = DIL_BLOCK
    qs, ks, vs = (_split(t, [DIL_HD] * DIL_HEADS) for t in (q, k, v))
    kps, vps = (_split(t, [DIL_HD] * DIL_HEADS) for t in (kp, vp))
    i = lax.broadcasted_iota(jnp.int32, (Q, 2 * Q), 0)
    j = lax.broadcasted_iota(jnp.int32, (Q, 2 * Q), 1)
    dist = Q + i - j
    mask = (dist >= 0) & (dist <= Q) & (jnp.logical_not(first) | (j >= Q))
    outs, lses = [], []
    for h in range(DIL_HEADS):
        k2 = jnp.concatenate([kps[h], ks[h]], axis=0)
        v2 = jnp.concatenate([vps[h], vs[h]], axis=0)
        s = jnp.where(mask, mm_nt(qs[h], k2) * (DIL_HD ** -0.5), NEG)
        m = lax.stop_gradient(jnp.max(s, axis=-1, keepdims=True))
        p = jnp.exp(s - m)
        l = jnp.sum(p, axis=-1, keepdims=True)
        outs.append(mm_nn(p / l, v2))
        lses.append(jnp.broadcast_to(m + jnp.log(l), (Q, DIL_HD)))
    return (jnp.concatenate(outs, axis=1), jnp.concatenate(lses, axis=1)), ()


def f_dil_merge(first, xp, xs, ps, sts):
    o0, o1, o2, l0, l1, l2 = xs
    m = jnp.maximum(jnp.maximum(l0, l1), l2)
    e0, e1, e2 = jnp.exp(l0 - m), jnp.exp(l1 - m), jnp.exp(l2 - m)
    den = e0 + e1 + e2
    return ((e0 * o0 + e1 * o1 + e2 * o2) / den,), ()


def _loss_head(y, target):
    S, D = y.shape
    R = _tile(S, (512, 256, 128, 64, 32, 16, 8))

    def body(y_ref, t_ref, dy_ref, l_ref):
        e = y_ref[...] - t_ref[...]
        dy_ref[...] = e * (1.0 / D)

        @pl.when(pl.program_id(0) == 0)
        def _():
            l_ref[...] = jnp.zeros_like(l_ref)

        l_ref[...] += jnp.broadcast_to(0.5 * jnp.sum(jnp.mean(e * e, axis=-1, keepdims=True), axis=0, keepdims=True),
                                       l_ref.shape)

    dy, l = pl.pallas_call(
        body, name="loss_head", grid=(S // R,),
        in_specs=[pl.BlockSpec((R, D), lambda i: (i, 0))] * 2,
        out_specs=[pl.BlockSpec((R, D), lambda i: (i, 0)), pl.BlockSpec((8, 128), lambda i: (0, 0))],
        out_shape=[jax.ShapeDtypeStruct((S, D), F32), jax.ShapeDtypeStruct((8, 128), F32)],
        compiler_params=pltpu.CompilerParams(dimension_semantics=("arbitrary",)))(y, target)
    return dy, l[0, 0]


def _adamw(parts, w, m, v, name):
    _, n, width = parts.shape
    tr = _tile(n, [t for t in (512, 256, 128, 64, 32, 16, 8) if t * width <= ADAM_BLOCK])

    def body(p_ref, w_ref, m_ref, v_ref, g_ref, d_ref, nm_ref, nv_ref):
        g = p_ref[0].astype(F32)
        for s in range(1, N_DEV):
            g = g + p_ref[s].astype(F32)
        nm = ADAM_B1 * m_ref[...] + (1.0 - ADAM_B1) * g
        nv = ADAM_B2 * v_ref[...] + (1.0 - ADAM_B2) * (g * g)
        m_hat = nm / (1.0 - ADAM_B1 ** ADAM_STEP)
        v_hat = nv / (1.0 - ADAM_B2 ** ADAM_STEP)
        g_ref[...] = g
        d_ref[...] = -ADAM_LR * (m_hat / (jnp.sqrt(v_hat) + ADAM_EPS) + ADAM_WD * w_ref[...])
        nm_ref[...] = nm
        nv_ref[...] = nv

    blk = pl.BlockSpec((tr, width), lambda i: (i, 0))
    return pl.pallas_call(
        body, name=name, grid=(n // tr,),
        in_specs=[pl.BlockSpec((N_DEV, tr, width), lambda i: (0, i, 0)), blk, blk, blk],
        out_specs=[blk] * 4, out_shape=[jax.ShapeDtypeStruct((n, width), F32)] * 4,
        compiler_params=pltpu.CompilerParams(dimension_semantics=("arbitrary",), vmem_limit_bytes=VMEM_LIMIT))(
            parts, w, m, v)


def _peer(k):
    x, y, c = lax.axis_index("x"), lax.axis_index("y"), lax.axis_index("c")
    px = 1 - x if k & 4 else x
    py = 1 - y if k & 2 else y
    pc = 1 - c if k & 1 else c
    return (px, py, pc), 4 * px + 2 * py + pc


def _my_id():
    return 4 * lax.axis_index("x") + 2 * lax.axis_index("y") + lax.axis_index("c")


def _all_gather(x, name):
    def body(x_ref, out_ref, send, recv, loc):
        me = _my_id()
        mine = pltpu.make_async_copy(x_ref, out_ref.at[me], loc)
        mine.start()
        cps = []
        for k in range(1, N_DEV):
            peer, _ = _peer(k)
            cp = pltpu.make_async_remote_copy(src_ref=x_ref, dst_ref=out_ref.at[me], send_sem=send.at[k - 1],
                                              recv_sem=recv.at[k - 1], device_id=peer,
                                              device_id_type=pl.DeviceIdType.MESH)
            cp.start()
            cps.append(cp)
        for k in range(1, N_DEV):
            peer, pid = _peer(k)
            pltpu.make_async_remote_copy(src_ref=x_ref, dst_ref=out_ref.at[pid], send_sem=send.at[k - 1],
                                         recv_sem=recv.at[k - 1], device_id=peer,
                                         device_id_type=pl.DeviceIdType.MESH).wait_recv()
        for cp in cps:
            cp.wait_send()
        mine.wait()

    return pl.pallas_call(
        body, name=name, out_shape=jax.ShapeDtypeStruct((N_DEV,) + x.shape, x.dtype),
        in_specs=[pl.BlockSpec(memory_space=pl.ANY)], out_specs=pl.BlockSpec(memory_space=pl.ANY),
        scratch_shapes=[pltpu.SemaphoreType.DMA((N_DEV - 1,)), pltpu.SemaphoreType.DMA((N_DEV - 1,)),
                        pltpu.SemaphoreType.DMA],
        compiler_params=pltpu.CompilerParams(has_side_effects=True))(x)


def _exchange_many(gs, name):
    n = len(gs)

    def body(*refs):
        g_refs, out_refs, (send, recv, loc) = refs[:n], refs[n:2 * n], refs[2 * n:]
        me = _my_id()
        mine = [pltpu.make_async_copy(g.at[me], o.at[me], loc.at[w]) for w, (g, o) in enumerate(zip(g_refs, out_refs))]
        for cp in mine:
            cp.start()
        cps = []
        for k in range(1, N_DEV):
            peer, pid = _peer(k)
            for w, (g, o) in enumerate(zip(g_refs, out_refs)):
                s = w * (N_DEV - 1) + k - 1
                cp = pltpu.make_async_remote_copy(src_ref=g.at[pid], dst_ref=o.at[me], send_sem=send.at[s],
                                                  recv_sem=recv.at[s], device_id=peer,
                                                  device_id_type=pl.DeviceIdType.MESH)
                cp.start()
                cps.append(cp)
        for k in range(1, N_DEV):
            peer, pid = _peer(k)
            for w, (g, o) in enumerate(zip(g_refs, out_refs)):
                s = w * (N_DEV - 1) + k - 1
                pltpu.make_async_remote_copy(src_ref=g.at[me], dst_ref=o.at[pid], send_sem=send.at[s],
                                             recv_sem=recv.at[s], device_id=peer,
                                             device_id_type=pl.DeviceIdType.MESH).wait_recv()
        for cp in cps:
            cp.wait_send()
        for cp in mine:
            cp.wait()

    return pl.pallas_call(
        body, name=name, out_shape=[jax.ShapeDtypeStruct(g.shape, g.dtype) for g in gs],
        in_specs=[pl.BlockSpec(memory_space=pl.ANY)] * n, out_specs=[pl.BlockSpec(memory_space=pl.ANY)] * n,
        scratch_shapes=[pltpu.SemaphoreType.DMA((n * (N_DEV - 1),)), pltpu.SemaphoreType.DMA((n * (N_DEV - 1),)),
                        pltpu.SemaphoreType.DMA((n,))],
        compiler_params=pltpu.CompilerParams(has_side_effects=True))(*gs)


def _gather_many(xs, name):
    n = len(xs)

    def body(*refs):
        x_refs, out_refs, (send, recv, loc) = refs[:n], refs[n:2 * n], refs[2 * n:]
        x, y, c = lax.axis_index("x"), lax.axis_index("y"), lax.axis_index("c")
        me, sibling = (x, y, c), (x, y, 1 - c)
        chips = [(1 - x, y), (x, 1 - y), (1 - x, 1 - y)]

        def slot(p):
            return 4 * p[0] + 2 * p[1] + p[2]

        def copy(w, k, block, to, src=None):
            dst = out_refs[w].at[slot(block)]
            return pltpu.make_async_remote_copy(src_ref=dst if src is None else src, dst_ref=dst,
                                                send_sem=send.at[w * (N_DEV - 1) + k], recv_sem=recv.at[w * (N_DEV - 1) + k],
                                                device_id=to, device_id_type=pl.DeviceIdType.MESH)

        mine = [pltpu.make_async_copy(x_refs[w], out_refs[w].at[slot(me)], loc.at[w]) for w in range(n)]
        for cp in mine:
            cp.start()
        first = []
        for j, chip in enumerate(chips):
            first += [copy(w, 1 + j, me, (*chip, c), src=x_refs[w]) for w in range(n)]
        first += [copy(w, 0, me, sibling, src=x_refs[w]) for w in range(n)]
        for cp in first:
            cp.start()
        passed = []
        for j, chip in enumerate(chips):
            for w in range(n):
                copy(w, 1 + j, (*chip, c), me).wait_recv()
                cp = copy(w, 4 + j, (*chip, c), sibling)
                cp.start()
                passed.append(cp)
        for w in range(n):
            copy(w, 0, sibling, me).wait_recv()
            for j, chip in enumerate(chips):
                copy(w, 4 + j, (*chip, 1 - c), me).wait_recv()
        for cp in first + passed:
            cp.wait_send()
        for cp in mine:
            cp.wait()

    return pl.pallas_call(
        body, name=name, out_shape=[jax.ShapeDtypeStruct((N_DEV,) + x.shape, x.dtype) for x in xs],
        in_specs=[pl.BlockSpec(memory_space=pl.ANY)] * n, out_specs=[pl.BlockSpec(memory_space=pl.ANY)] * n,
        scratch_shapes=[pltpu.SemaphoreType.DMA((n * (N_DEV - 1),)), pltpu.SemaphoreType.DMA((n * (N_DEV - 1),)),
                        pltpu.SemaphoreType.DMA((n,))],
        compiler_params=pltpu.CompilerParams(has_side_effects=True))(*xs)


def _cat_segs(G, ws, n_mix):
    segs = []
    for g in range(G):
        lo, hi = g * ws, (g + 1) * ws
        if lo < n_mix:
            segs.append((g, 0, min(hi, n_mix) - lo, D_XA + lo))
        if hi > n_mix:
            s = max(lo, n_mix)
            segs.append((g, s - lo, hi - s, s - n_mix))
    return segs


def _cat_cols(src, n_mix, ntot):
    G, R, ws = src.shape
    segs = _cat_segs(G, ws, n_mix)
    tr = _tile(R, (256, 128, 64, 32, 16, 8))

    def body(i_ref, o_ref):
        if ntot > G * ws:
            o_ref[...] = jnp.zeros_like(o_ref)
        for g, s, n, d in segs:
            o_ref[:, d:d + n] = i_ref[g][:, s:s + n]

    return pl.pallas_call(
        body, name="cat_cols", grid=(R // tr,), in_specs=[pl.BlockSpec((G, tr, ws), lambda i: (0, i, 0))],
        out_specs=pl.BlockSpec((tr, ntot), lambda i: (i, 0)), out_shape=jax.ShapeDtypeStruct((R, ntot), src.dtype),
        compiler_params=pltpu.CompilerParams(dimension_semantics=("arbitrary",)))(src)


def _uncat_cols(dw, G, ws, n_mix):
    R, ntot = dw.shape
    segs = _cat_segs(G, ws, n_mix)
    tr = _tile(R, (256, 128, 64, 32, 16, 8))

    def body(i_ref, o_ref):
        v = i_ref[...]
        for g, s, n, d in segs:
            o_ref[g, :, s:s + n] = v[:, d:d + n]

    return pl.pallas_call(
        body, name="uncat_cols", grid=(R // tr,), in_specs=[pl.BlockSpec((tr, ntot), lambda i: (i, 0))],
        out_specs=pl.BlockSpec((G, tr, ws), lambda i: (0, i, 0)), out_shape=jax.ShapeDtypeStruct((G, R, ws), dw.dtype),
        compiler_params=pltpu.CompilerParams(dimension_semantics=("arbitrary",)))(dw)


PACK_W = 1024


def _granule(n):
    return (256 if n >= 256 * PACK_W else 8) * PACK_W


def _pack(arrs, dtype):
    flat = jnp.concatenate([a.reshape(-1).astype(dtype) for a in arrs])
    n = flat.shape[0]
    pad = (-n) % _granule(n)
    if pad:
        flat = jnp.concatenate([flat, jnp.zeros((pad,), dtype)])
    return flat.reshape(-1, PACK_W)


def _unpack(packed, shapes):
    flat = packed.reshape(-1)
    out, o = [], 0
    for s in shapes:
        n = math.prod(s)
        out.append(flat[o:o + n].reshape(s))
        o += n
    return out


def _pack_lead(arrs, dtype):
    flat = jnp.concatenate([a.reshape(N_DEV, -1).astype(dtype) for a in arrs], axis=1)
    n = flat.shape[1]
    pad = (-n) % _granule(n)
    if pad:
        flat = jnp.concatenate([flat, jnp.zeros((N_DEV, pad), dtype)], axis=1)
    return flat.reshape(N_DEV, -1, PACK_W)


def _to_full(stacked, axis):
    t = jnp.moveaxis(stacked, 0, axis)
    s = list(t.shape)
    return t.reshape(s[:axis] + [s[axis] * s[axis + 1]] + s[axis + 2:])


def _to_chunks(full, axis):
    s = list(full.shape)
    t = full.reshape(s[:axis] + [N_DEV, s[axis] // N_DEV] + s[axis + 1:])
    return jnp.moveaxis(t, axis, 0)


def _rows_of(S):
    return _tile(S, (512, 256, 128, 64))


def _norm_fwd(x, g, dt=BF):
    (h,), _ = _seq_fwd("rmsnorm_fwd", f_rmsnorm, _rows_of(x.shape[0]), [_row_spec(x)], [_par_spec(g)], [],
                       [_out_spec(x.shape[1], dt=dt)])
    return h


def _norm_bwd(x, g, dh):
    (dx,), (dg,) = _seq_bwd("rmsnorm_bwd", f_rmsnorm, _rows_of(x.shape[0]), [_row_spec(x)], [_par_spec(g)], [],
                            [_row_spec(dh)], [])
    return dx, dg


def _mixer_specs(kind, S, p, w):
    if kind == 0:
        return (f_gla, CHUNK, [_row_spec(p)],
                [_par_spec(w['a_w_gate2']), _par_spec(w['a_b_gate'].reshape(1, -1)), _par_spec(w['a_o_norm'].reshape(1, -1))],
                [(GLA_DV, GLA_DK)] * GLA_HEADS, D_MIX)
    if kind == 2:
        return (f_ssd, CHUNK, [_row_spec(p, prev=True)],
                [_par_spec(w['c_conv_w']), _par_spec(w['c_conv_b'].reshape(1, -1)), _par_spec(w['c_dt_bias'].reshape(1, -1)),
                 _par_spec(w['c_a_log'].reshape(1, -1)), _par_spec(w['c_d'].reshape(1, -1)),
                 _par_spec(w['c_norm'].reshape(1, -1))],
                [(SSM_STATE, 2 * SSM_HD)] * (SSM_HEADS // 2), D_MIX)
    return (f_hgrn, CHUNK, [_row_spec(p)],
            [_par_spec(w['d_lower_bounds']), _par_spec(w['d_o_norm'].reshape(1, -1))],
            [(HGRN_DV, HGRN_DK)] * HGRN_HEADS, D_MIX)


def _perm(t, r):
    if r == 1:
        return t
    S, n = t.shape
    return t.reshape(S // r, r, n).transpose(1, 0, 2).reshape(S, n)


def _unperm(t, r):
    if r == 1:
        return t
    S, n = t.shape
    return t.reshape(r, S // r, n).transpose(1, 0, 2).reshape(S, n)


def _rope_consts():
    half = DIL_HD // 2
    inv = ROPE_THETA ** (-jnp.arange(half, dtype=F32) / half)
    invf = jnp.concatenate([inv, inv]).reshape(1, DIL_HD)
    sign = jnp.concatenate([-jnp.ones((half,), F32), jnp.ones((half,), F32)]).reshape(1, DIL_HD)
    return invf, sign


def _dil_fwd(p, pos, w):
    S = p.shape[0]
    invf, sign = _rope_consts()
    prep_rows = [_row_spec(p), _row_spec(pos, diff=False)]
    prep_pars = [_par_spec(w['b_q_norm'].reshape(1, -1)), _par_spec(w['b_k_norm'].reshape(1, -1)),
                 _par_spec(invf, diff=False), _par_spec(sign, diff=False)]
    nqk = len(DIL_GROUPS) * D_DIL
    (qr, kr), _ = _seq_fwd("dil_prep_fwd", f_dil_prep, _tile(S, (256, 128)), prep_rows, prep_pars, [],
                           [_out_spec(nqk), _out_spec(nqk)])
    v = p[:, D_XA + 2 * nqk:D_XA + 3 * nqk]
    res = dict(qr=qr, kr=kr, perm=[], o=[], lse=[])
    for g, (window, r) in enumerate(DIL_GROUPS):
        sl = slice(g * D_DIL, (g + 1) * D_DIL)
        qp, kp, vp = _perm(qr[:, sl], r), _perm(kr[:, sl], r), _perm(v[:, sl], r)
        rows = [_row_spec(qp), _row_spec(kp, prev=True), _row_spec(vp, prev=True)]
        (o, lse), _ = _seq_fwd("dil_attn_fwd", f_dil_attn, DIL_BLOCK, rows, [], [], [_out_spec(D_DIL), _out_spec(D_DIL)],
                               period=S // r // DIL_BLOCK)
        res['perm'].append((qp, kp, vp))
        res['o'].append(_unperm(o, r))
        res['lse'].append(_unperm(lse, r))
    mrows = [_row_spec(t) for t in res['o'] + res['lse']]
    (tok,), _ = _seq_fwd("dil_merge_fwd", f_dil_merge, _rows_of(S), mrows, [], [], [_out_spec(D_DIL)])
    res['prep'] = (prep_rows, prep_pars)
    return tok, res


def _dil_bwd(dtok, res, p):
    S = p.shape[0]
    mrows = [_row_spec(t) for t in res['o'] + res['lse']]
    dm, _ = _seq_bwd("dil_merge_bwd", f_dil_merge, _rows_of(S), mrows, [], [], [_row_spec(dtok)], [])
    dq, dk, dv = [], [], []
    for g, (window, r) in enumerate(DIL_GROUPS):
        qp, kp, vp = res['perm'][g]
        rows = [_row_spec(qp), _row_spec(kp, prev=True), _row_spec(vp, prev=True)]
        douts = [_row_spec(_perm(dm[g], r)), _row_spec(_perm(dm[3 + g], r))]
        (a, b, c), _ = _seq_bwd("dil_attn_bwd", f_dil_attn, DIL_BLOCK, rows, [], [], douts, [],
                                period=S // r // DIL_BLOCK)
        dq.append(_unperm(a, r)); dk.append(_unperm(b, r)); dv.append(_unperm(c, r))
    dqr, dkr, dv = (jnp.concatenate(t, axis=1) for t in (dq, dk, dv))
    prep_rows, prep_pars = res['prep']
    (dp,), (dqn, dkn) = _seq_bwd("dil_prep_bwd", f_dil_prep, _tile(S, (256, 128)), prep_rows, prep_pars, [],
                                 [_row_spec(dqr), _row_spec(dkr)], [])
    nqk = len(DIL_GROUPS) * D_DIL
    dp = jnp.concatenate([dp[:, :D_XA + 2 * nqk], dv], axis=1)
    return dp, dict(b_q_norm=dqn.reshape(-1), b_k_norm=dkn.reshape(-1))


def _ffn_specs(u, cw, cb):
    half = N_DEV // 2
    rows = [_row_spec(u, prev=True, lb=(2, None), li=lambda jc: (0, jc))]
    pars = [_par_spec(cw, bs=(None, FFN_CONV, FF_SH), idx=lambda jc: (jc, 0, 0)),
            _par_spec(cw, bs=(None, FFN_CONV, FF_SH), idx=lambda jc: (jc + half, 0, 0)),
            _par_spec(cb, bs=(None, 1, FF_SH), idx=lambda jc: (jc, 0, 0)),
            _par_spec(cb, bs=(None, 1, FF_SH), idx=lambda jc: (jc + half, 0, 0))]
    return half, rows, pars


N_MIX = {0: 2 * GLA_HEADS * GLA_DK + 2 * D_MIX + GLA_RANK, 1: 3 * len(DIL_GROUPS) * D_DIL,
         2: 2 * D_MIX + 2 * SSM_GROUPS * SSM_STATE + SSM_HEADS, 3: 2 * HGRN_HEADS * HGRN_DK + 2 * D_MIX}
W_IN = {0: 'a_w_in', 1: 'b_w_in', 2: 'c_w_in', 3: 'd_w_in'}
W_OUT = {0: 'a_w_out', 1: 'b_w_out', 2: 'c_w_out', 3: 'd_w_out'}


def _in_blocks(name, t):
    return t if SHARD_AXIS[name] == 1 else t.reshape(1, N_DEV * t.shape[1], t.shape[2])


def _device_step(x, mem, pos, gw, rep, target):
    S, D = x.shape
    w = dict(rep)
    w['a_w_gate2'] = _to_full(gw['a_w_gate2'], 1)
    w['c_conv_w'] = _to_full(gw['c_conv_w'], 1)
    posf = pos.reshape(S, 1).astype(F32)
    n_mix, w_in_name, w_out_name = N_MIX, W_IN, W_OUT
    ntot = {k: -(-(n_mix[k] + D_XA) // 256) * 256 for k in n_mix}
    in_blocks = {k: _in_blocks(w_in_name[k], gw[w_in_name[k]]) for k in n_mix}
    w_out = {k: (_to_full(gw[w_out_name[k]], 1) if SHARD_AXIS[w_out_name[k]] == 1
                 else gw[w_out_name[k]].reshape(-1, D)) for k in n_mix}
    mem_g = w['mem_norm'].reshape(1, -1)
    mem_n = _norm_fwd(mem, mem_g)
    R = _rows_of(S)
    wup = gw['ffn_w_up']

    saved = []
    for i in range(DEPTH):
        kind = i % 4
        L = dict(x0=x)
        g1 = w['mix_norm'][i].reshape(1, -1)
        h = _norm_fwd(x, g1)
        wcat = _cat_cols(in_blocks[kind], n_mix[kind], ntot[kind])
        p = _matmul(h, wcat, name="matmul_in")
        if kind == 1:
            tok, L['dil'] = _dil_fwd(p, posf, w)
        else:
            f, Rm, rows, pars, sshapes, _ = _mixer_specs(kind, S, p, w)
            (tok,), L['states'] = _seq_fwd("mixer%d_fwd" % kind, f, Rm, rows, pars, sshapes, [_out_spec(D_MIX)],
                                           save_states=True)
        wkv = gw['xa_w_kv'][:, i].reshape(D, 2 * D_XA)
        kv = _matmul(mem_n, wkv, name="matmul_kv")
        xa_rows = [_row_spec(p, w=D_XA, dn=D_XA)]
        xa_pars = [_par_spec(kv), _par_spec(w['xa_q_norm'][i].reshape(1, -1)), _par_spec(w['xa_k_norm'][i].reshape(1, -1))]
        (xa,), _ = _seq_fwd("xattn_fwd", f_xattn, R, xa_rows, xa_pars, [], [_out_spec(D_XA)])
        cat = jnp.concatenate([tok, xa], axis=1).astype(BF)
        x1 = _matmul(cat, w_out[kind], add=x, name="matmul_out")
        g2 = w['ffn_norm'][i].reshape(1, -1)
        h2 = _norm_fwd(x1, g2)
        u = _ffn_up(h2, wup, i)
        cw, cb = gw['ffn_conv_w'][:, i], w['ffn_conv_b'][i].reshape(N_DEV, 1, FF_SH)
        nt, frows, fpars = _ffn_specs(u, cw, cb)
        (act,), _ = _seq_fwd("ffn_act_fwd", f_ffn_act, R, frows, fpars, [],
                             [_out_spec(FF_SH, dt=BF, ls=(nt,), lb=(None,), li=lambda jc: (jc,))], ncol=nt)
        wd = gw['ffn_w_down'][:, i].reshape(D_FF, D)
        x = _ffn_down(act, wd, x1)
        L.update(h=h, p=p, wcat=wcat, kv=kv, wkv=wkv, cat=cat, x1=x1, h2=h2, u=u, act=act, wd=wd, g1=g1, g2=g2)
        saved.append(L)

    dx, loss = _loss_head(x, target)

    G, Gc = {}, {}
    d_mem_n = None
    acc = {k: [None] * DEPTH for k in ('mix_norm', 'ffn_norm', 'ffn_conv_b', 'xa_q_norm', 'xa_k_norm')}
    accc = {k: [None] * DEPTH for k in ('ffn_w_up', 'ffn_conv_w', 'ffn_w_down', 'xa_w_kv')}
    half = N_DEV // 2
    for i in reversed(range(DEPTH)):
        kind = i % 4
        L = saved[i]
        dxb = dx.astype(BF)
        accc['ffn_w_down'][i] = _ffn_dw_down(L['act'], dxb).reshape(N_DEV, D_FF // N_DEV, D)
        dact = _ffn_dact(dxb, L['wd'])
        cw, cb = gw['ffn_conv_w'][:, i], w['ffn_conv_b'][i].reshape(N_DEV, 1, FF_SH)
        nt, frows, fpars = _ffn_specs(L['u'], cw, cb)
        (du,), (dwg, dwv, dbg, dbv) = _seq_bwd(
            "ffn_act_bwd", f_ffn_act, R, frows, fpars, [], [_row_spec(dact, lb=(None,), li=lambda jc: (jc,))], [],
            ncol=nt, dx_dt=BF)
        accc['ffn_conv_w'][i] = jnp.concatenate([dwg[:half], dwv[half:]], axis=0)
        acc['ffn_conv_b'][i] = jnp.concatenate([dbg[:half], dbv[half:]], axis=0).reshape(-1)
        accc['ffn_w_up'][i] = _ffn_dw_up(L['h2'], du)
        dh2 = _ffn_dh2(du, wup, i)
        dx1n, dg2 = _norm_bwd(L['x1'], L['g2'], dh2)
        acc['ffn_norm'][i] = dg2.reshape(-1)
        dx1 = _add(dx, dx1n)
        dx1b = dx1.astype(BF)
        G_out = _matmul(L['cat'], dx1b, mode="tn", out_dtype=BF, name="matmul_dw_out")
        dcat = _matmul(dx1b, w_out[kind], mode="nt", name="matmul_dcat")
        ntok = D_DIL if kind == 1 else D_MIX
        dtok, dxa = dcat[:, :ntok], dcat[:, ntok:]
        p = L['p']
        xa_rows = [_row_spec(p, w=D_XA, dn=D_XA)]
        xa_pars = [_par_spec(L['kv']), _par_spec(w['xa_q_norm'][i].reshape(1, -1)), _par_spec(w['xa_k_norm'][i].reshape(1, -1))]
        (dxq,), (dkv, dqn, dkn) = _seq_bwd("xattn_bwd", f_xattn, R, xa_rows, xa_pars, [], [_row_spec(dxa)], [])
        acc['xa_q_norm'][i], acc['xa_k_norm'][i] = dqn.reshape(-1), dkn.reshape(-1)
        accc['xa_w_kv'][i] = _matmul(mem_n, dkv, mode="tn", out_dtype=BF, name="matmul_dw_kv").reshape(
            N_DEV, D // N_DEV, 2 * D_XA)
        d_mem_n = _matmul(dkv, L['wkv'], mode="nt", add=d_mem_n, name="matmul_dmem" + ("" if d_mem_n is None else "_acc"))
        if kind == 1:
            dp, gm = _dil_bwd(dtok, L['dil'], p)
            G.update(gm)
        else:
            f, Rm, rows, pars, sshapes, _ = _mixer_specs(kind, S, p, w)
            (dp,), dps = _seq_bwd("mixer%d_bwd" % kind, f, Rm, rows, pars, sshapes, [_row_spec(dtok)], L['states'])
            if kind == 0:
                Gc['a_w_gate2'], G['a_b_gate'], G['a_o_norm'] = _to_chunks(dps[0], 1), dps[1].reshape(-1), dps[2].reshape(-1)
            elif kind == 2:
                Gc['c_conv_w'] = _to_chunks(dps[0], 1)
                for nme, v in zip(('c_conv_b', 'c_dt_bias', 'c_a_log', 'c_d', 'c_norm'), dps[1:]):
                    G[nme] = v.reshape(-1)
            else:
                G['d_lower_bounds'], G['d_o_norm'] = dps[0], dps[1].reshape(-1)
        dp = jnp.concatenate([dxq, dp[:, D_XA:]], axis=1).astype(BF)
        dwcat = _matmul(L['h'], dp, mode="tn", out_dtype=BF, name="matmul_dw_in")
        blocks = in_blocks[kind]
        Gc[w_in_name[kind]] = _uncat_cols(dwcat, blocks.shape[0], blocks.shape[2], n_mix[kind]).reshape(
            gw[w_in_name[kind]].shape)
        Gc[w_out_name[kind]] = (_to_chunks(G_out, 1) if SHARD_AXIS[w_out_name[kind]] == 1
                                else G_out.reshape(gw[w_out_name[kind]].shape))
        dh = _matmul(dp, L['wcat'], mode="nt", name="matmul_dh")
        dxn, dg1 = _norm_bwd(L['x0'], L['g1'], dh)
        acc['mix_norm'][i] = dg1.reshape(-1)
        dx = _add(dx1, dxn)

    _, dmg = _norm_bwd(mem, mem_g, d_mem_n)
    G['mem_norm'] = dmg.reshape(-1)
    for k, v in acc.items():
        G[k] = jnp.stack(v)
    for k, v in accc.items():
        Gc[k] = jnp.stack(v, axis=1)
    return loss, dx, Gc, G


def _add(a, b):
    S, D = a.shape
    R = _rows_of(S)

    def body(a_ref, b_ref, o_ref):
        o_ref[...] = a_ref[...] + b_ref[...]

    blk = pl.BlockSpec((R, D), lambda i: (i, 0))
    return pl.pallas_call(body, name="residual_add", grid=(S // R,), in_specs=[blk, blk], out_specs=blk,
                          out_shape=jax.ShapeDtypeStruct((S, D), F32),
                          compiler_params=pltpu.CompilerParams(dimension_semantics=("arbitrary",)))(a, b)


def kernel(x, mem, positions, mem_norm, mix_norm, xa_w_kv, xa_q_norm, xa_k_norm, ffn_norm, ffn_w_up, ffn_conv_w, ffn_conv_b, ffn_w_down, a_w_in, a_w_gate2, a_b_gate, a_o_norm, a_w_out, b_w_in, b_q_norm, b_k_norm, b_w_out, c_w_in, c_conv_w, c_conv_b, c_dt_bias, c_a_log, c_d, c_norm, c_w_out, d_w_in, d_lower_bounds, d_o_norm, d_w_out, loss_target, m_mem_norm, m_mix_norm, m_xa_w_kv, m_xa_q_norm, m_xa_k_norm, m_ffn_norm, m_ffn_w_up, m_ffn_conv_w, m_ffn_conv_b, m_ffn_w_down, m_a_w_in, m_a_w_gate2, m_a_b_gate, m_a_o_norm, m_a_w_out, m_b_w_in, m_b_q_norm, m_b_k_norm, m_b_w_out, m_c_w_in, m_c_conv_w, m_c_conv_b, m_c_dt_bias, m_c_a_log, m_c_d, m_c_norm, m_c_w_out, m_d_w_in, m_d_lower_bounds, m_d_o_norm, m_d_w_out, v_mem_norm, v_mix_norm, v_xa_w_kv, v_xa_q_norm, v_xa_k_norm, v_ffn_norm, v_ffn_w_up, v_ffn_conv_w, v_ffn_conv_b, v_ffn_w_down, v_a_w_in, v_a_w_gate2, v_a_b_gate, v_a_o_norm, v_a_w_out, v_b_w_in, v_b_q_norm, v_b_k_norm, v_b_w_out, v_c_w_in, v_c_conv_w, v_c_conv_b, v_c_dt_bias, v_c_a_log, v_c_d, v_c_norm, v_c_w_out, v_d_w_in, v_d_lower_bounds, v_d_o_norm, v_d_w_out):
    args = locals()
    w = {n: args[n] for n in WEIGHTS}
    m = {n: args['m_' + n] for n in WEIGHTS}
    v = {n: args['v_' + n] for n in WEIGHTS}

    big = [n for n in SHARDED if w[n].size >= 65536]
    small = [n for n in SHARDED if n not in big]
    gathered = _gather_many([w[n].astype(BF) for n in big] + [w[n] for n in small], "gather_weights")
    gw = dict(zip(big + small, gathered))

    loss, grad_x, Gc, G = _device_step(x[0], mem[0], positions[0], gw, {n: w[n] for n in REPLICATED}, loss_target[0])
    loss = lax.psum(loss, ("x", "y", "c"))

    parts = dict(zip(big + small, _exchange_many([Gc[n] for n in big + small], "exchange_grads")))
    rep_parts = _all_gather(_pack([G[n] for n in REPLICATED], F32), "gather_replicated_grads")

    out = {}

    def put(names, res, shapes):
        for kind, r in zip(("grad", "delta", "new_m", "new_v"), res):
            for n, t in zip(names, _unpack(r, shapes)):
                out[kind + "_" + n] = t

    for n in big:
        shp = tuple(w[n].shape)
        two_d = (math.prod(shp[:-1]), shp[-1])
        res = _adamw(parts[n].reshape((N_DEV,) + two_d), w[n].reshape(two_d), m[n].reshape(two_d), v[n].reshape(two_d),
                     "adamw")
        for kind, r in zip(("grad", "delta", "new_m", "new_v"), res):
            out[kind + "_" + n] = r.reshape(shp)
    for names, prt, tag in ((small, _pack_lead([parts[n] for n in small], F32), "adamw_small"),
                            (REPLICATED, rep_parts, "adamw_replicated")):
        res = _adamw(prt, _pack([w[n] for n in names], F32), _pack([m[n] for n in names], F32),
                     _pack([v[n] for n in names], F32), tag)
        put(names, res, [tuple(w[n].shape) for n in names])
    return (loss, grad_x[None], *[out["grad_" + n] for n in WEIGHTS], *[out["delta_" + n] for n in WEIGHTS],
            *[out["new_m_" + n] for n in WEIGHTS], *[out["new_v_" + n] for n in WEIGHTS])
```

```python
import functools
import math

import jax
import jax.numpy as jnp
from jax import lax
from jax.experimental import pallas as pl
from jax.experimental.pallas import tpu as pltpu

F32 = jnp.float32
BF = jnp.bfloat16
_MM_DTYPE = BF

N_DEV = 8
EPS = 1e-6
ROPE_THETA = 10000.0
CHUNK = 64
D_MIX = 768
XA_HEADS, XA_HD, D_XA = 4, 64, 256
GLA_HEADS, GLA_DK, GLA_DV, GLA_RANK, GLA_GATE_NORM = 4, 96, 192, 16, 16.0
DIL_GROUPS = ((128, 1), (512, 4), (2048, 16))
DIL_HEADS, DIL_HD, DIL_BLOCK, D_DIL = 4, 128, 128, 512
SSM_HD, SSM_HEADS, SSM_GROUPS, SSM_STATE, SSM_CONV = 64, 12, 2, 128, 4
HGRN_HEADS, HGRN_DK, HGRN_DV = 6, 128, 128
D_FF = 2816
FFN_CONV = 3
DEPTH = 4
ADAM_LR, ADAM_B1, ADAM_B2, ADAM_EPS, ADAM_WD, ADAM_STEP = 0.001, 0.9, 0.999, 1e-08, 0.01, 10
NEG = -1e30
HALO = 8
VMEM_LIMIT = 56 << 20
ADAM_BLOCK = 1 << 18

WEIGHTS = ['mem_norm', 'mix_norm', 'xa_w_kv', 'xa_q_norm', 'xa_k_norm', 'ffn_norm', 'ffn_w_up', 'ffn_conv_w',
           'ffn_conv_b', 'ffn_w_down', 'a_w_in', 'a_w_gate2', 'a_b_gate', 'a_o_norm', 'a_w_out', 'b_w_in', 'b_q_norm',
           'b_k_norm', 'b_w_out', 'c_w_in', 'c_conv_w', 'c_conv_b', 'c_dt_bias', 'c_a_log', 'c_d', 'c_norm', 'c_w_out',
           'd_w_in', 'd_lower_bounds', 'd_o_norm', 'd_w_out']
SHARD_AXIS = {'xa_w_kv': 1, 'ffn_w_up': 2, 'ffn_conv_w': 2, 'ffn_w_down': 1, 'a_w_in': 1, 'a_w_gate2': 1, 'a_w_out': 0,
              'b_w_in': 1, 'b_w_out': 1, 'c_w_in': 0, 'c_conv_w': 1, 'c_w_out': 0, 'd_w_in': 1, 'd_w_out': 0}
SHARDED = [n for n in WEIGHTS if n in SHARD_AXIS]
REPLICATED = [n for n in WEIGHTS if n not in SHARD_AXIS]


def _dot(a, b, ca, cb):
    return lax.dot_general(a.astype(_MM_DTYPE), b.astype(_MM_DTYPE), (((ca,), (cb,)), ((), ())),
                           preferred_element_type=F32)


@jax.custom_vjp
def mm_nn(a, b):
    return _dot(a, b, 1, 0)


mm_nn.defvjp(lambda a, b: (_dot(a, b, 1, 0), (a, b)),
             lambda r, g: (_dot(g, r[1], 1, 1), _dot(r[0], g, 0, 0)))


@jax.custom_vjp
def mm_nt(a, b):
    return _dot(a, b, 1, 1)


mm_nt.defvjp(lambda a, b: (_dot(a, b, 1, 1), (a, b)),
             lambda r, g: (_dot(g, r[1], 1, 0), _dot(g, r[0], 0, 0)))


@jax.custom_vjp
def mm_tn(a, b):
    return _dot(a, b, 0, 0)


mm_tn.defvjp(lambda a, b: (_dot(a, b, 0, 0), (a, b)),
             lambda r, g: (_dot(r[1], g, 1, 1), _dot(r[0], g, 1, 0)))


def _dot_hi(a, b, ca, cb):
    return lax.dot_general(a, b, (((ca,), (cb,)), ((), ())), precision=lax.Precision.HIGHEST,
                           preferred_element_type=F32)


def _tril(c):
    return lax.broadcasted_iota(jnp.int32, (c, c), 0) >= lax.broadcasted_iota(jnp.int32, (c, c), 1)


@jax.custom_vjp
def cumsum_rows(x):
    return _dot_hi(_tril(x.shape[0]).astype(F32), x, 1, 0)


cumsum_rows.defvjp(lambda x: (cumsum_rows(x), None),
                   lambda r, g: (_dot_hi(_tril(g.shape[0]).astype(F32), g, 0, 0),))


@jax.custom_vjp
def cumsum_rows_t(x):
    return _dot_hi(x, _tril(x.shape[0]).astype(F32), 0, 1)


cumsum_rows_t.defvjp(lambda x: (cumsum_rows_t(x), None),
                     lambda r, g: (_dot_hi(_tril(g.shape[1]).astype(F32), g, 0, 1),))


def _split(x, sizes):
    sizes = tuple(int(s) for s in sizes)
    assert sum(sizes) == x.shape[-1], (sizes, x.shape)

    @jax.custom_vjp
    def sp(x):
        out, o = [], 0
        for s in sizes:
            out.append(x[:, o:o + s])
            o += s
        return tuple(out)

    sp.defvjp(lambda x: (sp(x), None), lambda r, g: (jnp.concatenate(list(g), axis=1),))
    return sp(x)


def _row(x, r):
    m = lax.broadcasted_iota(jnp.int32, x.shape, 0) == r
    return jnp.sum(jnp.where(m, x, 0.0), axis=0, keepdims=True)


@jax.custom_vjp
def _roll_half(x):
    return pltpu.roll(x, 64, 1)


_roll_half.defvjp(lambda x: (pltpu.roll(x, 64, 1), None), lambda r, g: (pltpu.roll(g, 64, 1),))


def _shift(xp, x, d):
    if d == 0:
        return x
    n, m = x.shape[0], xp.shape[0]
    assert d <= m == HALO

    @jax.custom_vjp
    def sh(xp, x):
        r = pltpu.roll(x, d, 0)
        row = lax.broadcasted_iota(jnp.int32, xp.shape, 0)
        head = jnp.where(row < d, pltpu.roll(xp, d, 0), r[:m])
        return jnp.concatenate([head, r[m:]], axis=0)

    def bwd(_, g):
        row = lax.broadcasted_iota(jnp.int32, g.shape, 0)
        rowp = lax.broadcasted_iota(jnp.int32, (m,) + g.shape[1:], 0)
        dxp = jnp.where(rowp >= m - d, pltpu.roll(g[:m], m - d, 0), 0.0)
        return dxp, jnp.where(row < n - d, pltpu.roll(g, n - d, 0), 0.0)

    sh.defvjp(lambda xp, x: (sh(xp, x), None), bwd)
    return sh(xp, x)


def _rms(x, g):
    return x * lax.rsqrt(jnp.mean(x * x, axis=-1, keepdims=True) + EPS) * g


def _lane_pair(a, b, width=128):
    shape = a.shape[:-1] + (width,)
    lane = lax.broadcasted_iota(jnp.int32, shape, len(shape) - 1)
    return jnp.where(lane < width // 2, a, b)


def _row_spec(a, w=None, c=None, prev=False, diff=True, dn=None, lb=(), li=None):
    return dict(a=a, w=a.shape[-1] if w is None else w, c=(lambda jc: 0) if c is None else c, prev=prev, diff=diff,
                dn=a.shape[-1] if dn is None else dn, lb=tuple(lb), li=(lambda jc: ()) if li is None else li)


def _par_spec(a, bs=None, idx=None, diff=True):
    nd = a.ndim
    return dict(a=a, bs=tuple(a.shape) if bs is None else tuple(bs),
                idx=(lambda jc: (0,) * nd) if idx is None else idx, diff=diff)


def _out_spec(n, w=None, c=None, dt=F32, ls=(), lb=(), li=None):
    return dict(n=n, w=n if w is None else w, c=(lambda jc: 0) if c is None else c, dt=dt, ls=tuple(ls), lb=tuple(lb),
                li=(lambda jc: ()) if li is None else li)


def _cparams():
    return pltpu.CompilerParams(dimension_semantics=("arbitrary", "arbitrary"), vmem_limit_bytes=VMEM_LIMIT)


def _bspec(s, R, rowfn):
    return pl.BlockSpec(s['lb'] + (R, s['w']),
                        functools.partial(lambda jc, i, s: tuple(s['li'](jc)) + (rowfn(i), s['c'](jc)), s=s))


def _prev_rows(s, R):
    return R if s['prev'] == 'block' else HALO


def _pspec(s, R, blockfn):
    pr = _prev_rows(s, R)
    return pl.BlockSpec(s['lb'] + (pr, s['w']), functools.partial(
        lambda jc, i, s: tuple(s['li'](jc)) + (jnp.maximum(blockfn(i) * (R // pr) - 1, 0), s['c'](jc)), s=s))


def _seq_fwd(name, f, R, rows, params, state_shapes, outs, *, ncol=1, period=None, save_states=False):
    nrows = rows[0]['a'].shape[-2]
    nb = nrows // R
    assert nb * R == nrows
    period = nb if period is None else period
    prev_ids = [k for k, r in enumerate(rows) if r['prev']]
    n_rows, n_prev, n_par, n_out, n_st = len(rows), len(prev_ids), len(params), len(outs), len(state_shapes)

    def body(*refs):
        o = 0
        cur = refs[o:o + n_rows]; o += n_rows
        prv = refs[o:o + n_prev]; o += n_prev
        par = refs[o:o + n_par]; o += n_par
        out = refs[o:o + n_out]; o += n_out
        sav = refs[o:o + (n_st if save_states else 0)]; o += len(sav)
        st = refs[o:o + n_st]
        i = pl.program_id(1)
        first = (i % period) == 0

        @pl.when(i == 0)
        def _():
            for s in st:
                s[...] = jnp.zeros_like(s)

        xs = [r[...].astype(F32) for r in cur]
        xp = [r[...].astype(F32) for r in prv]
        ps = [r[...] for r in par]
        sts = [s[...] for s in st]
        for sv, s in zip(sav, sts):
            sv[0] = s
        ov, ns = f(first, xp, xs, ps, sts)
        for r, v in zip(out, ov):
            r[...] = v.astype(r.dtype)
        for s, v in zip(st, ns):
            s[...] = v

    in_specs = [_bspec(r, R, lambda i: i) for r in rows]
    in_specs += [_pspec(rows[k], R, lambda i: i) for k in prev_ids]
    in_specs += [pl.BlockSpec(p['bs'], functools.partial(lambda jc, i, idx: idx(jc), idx=p['idx'])) for p in params]
    out_specs = [_bspec(o_, R, lambda i: i) for o_ in outs]
    out_shape = [jax.ShapeDtypeStruct(o_['ls'] + (nrows, o_['n']), o_['dt']) for o_ in outs]
    if save_states:
        for s in state_shapes:
            out_specs.append(pl.BlockSpec((1,) + tuple(s), lambda jc, i, nd=len(s): (i,) + (0,) * nd))
            out_shape.append(jax.ShapeDtypeStruct((nb,) + tuple(s), F32))
    args = [r['a'] for r in rows] + [rows[k]['a'] for k in prev_ids] + [p['a'] for p in params]
    res = pl.pallas_call(
        body, name=name, grid=(ncol, nb), in_specs=in_specs, out_specs=out_specs, out_shape=out_shape,
        scratch_shapes=[pltpu.VMEM(tuple(s), F32) for s in state_shapes], compiler_params=_cparams())(*args)
    return list(res[:n_out]), list(res[n_out:])


def _seq_bwd(name, f, R, rows, params, state_shapes, douts, saved, *, ncol=1, period=None, dx_dt=F32, dx_add=None,
             dx_bf=False, dx_alias=None):
    nrows = rows[0]['a'].shape[-2]
    nb = nrows // R
    period = nb if period is None else period
    prev_ids = [k for k, r in enumerate(rows) if r['prev']]
    drow_ids = [k for k, r in enumerate(rows) if r['diff']]
    dpar_ids = [k for k, p in enumerate(params) if p['diff']]
    for k in prev_ids:
        assert rows[k]['diff']
    dx_add, dx_alias = dict(dx_add or {}), dict(dx_alias or {})
    add_ids, alias_ids = sorted(dx_add), sorted(dx_alias)
    n_rows, n_prev, n_par, n_do, n_st = len(rows), len(prev_ids), len(params), len(douts), len(state_shapes)
    n_dx, n_dp, n_add, n_al = len(drow_ids), len(dpar_ids), len(add_ids), len(alias_ids)

    def body(*refs):
        o = 0
        cur = refs[o:o + n_rows]; o += n_rows
        prv = refs[o:o + n_prev]; o += n_prev
        par = refs[o:o + n_par]; o += n_par
        sav = refs[o:o + n_st]; o += n_st
        dou = refs[o:o + n_do]; o += n_do
        adr = refs[o:o + n_add]; o += n_add
        o += n_al
        dxr = refs[o:o + n_dx]; o += n_dx
        dpr = refs[o:o + n_dp]; o += n_dp
        dxb = refs[o:o + (n_dx if dx_bf else 0)]; o += len(dxb)
        dst = refs[o:o + n_st]; o += n_st
        car = refs[o:o + n_prev]
        j = pl.program_id(1)
        i = nb - 1 - j
        first = (i % period) == 0

        @pl.when(j == 0)
        def _():
            for s in tuple(dst) + tuple(car) + tuple(dpr):
                s[...] = jnp.zeros_like(s)

        xs = [r[...].astype(F32) for r in cur]
        xp = [r[...].astype(F32) for r in prv]
        ps = [r[...] for r in par]
        sts = [s[0] for s in sav]

        def g(dxs, dxp, dps, dsts):
            xs_, ps_ = list(xs), list(ps)
            for k, v in zip(drow_ids, dxs):
                xs_[k] = v
            for k, v in zip(dpar_ids, dps):
                ps_[k] = v
            ov, ns = f(first, list(dxp), xs_, ps_, list(dsts))
            return tuple(ov), tuple(ns)

        _, vjp = jax.vjp(g, tuple(xs[k] for k in drow_ids), tuple(xp), tuple(ps[k] for k in dpar_ids), tuple(sts))
        dxs, dxp, dps, dsts = vjp((tuple(r[...].astype(F32) for r in dou), tuple(s[...] for s in dst)))
        dxs = list(dxs)
        for n_, pos in enumerate(add_ids):
            dxs[pos] = dxs[pos] + adr[n_][...].astype(F32)
        tails = {}
        for n_, k in enumerate(prev_ids):
            pos = drow_ids.index(k)
            if rows[k]['prev'] == 'block':
                dxs[pos] = dxs[pos] + car[n_][...]
            else:
                tails[pos] = car[n_][...]
            car[n_][...] = dxp[n_]
        for pos, v in enumerate(dxs):
            outs_ = [dxr[pos]] + ([dxb[pos]] if dx_bf else [])
            if pos in tails:
                v = jnp.concatenate([v[..., :R - HALO, :], v[..., R - HALO:, :] + tails[pos]], axis=-2)
            for r in outs_:
                r[...] = v.astype(r.dtype)
        for r, v in zip(dpr, dps):
            r[...] += v
        for s, v in zip(dst, dsts):
            s[...] = v

    def rev(j):
        return nb - 1 - j

    def dspec(k):
        return _bspec(rows[k], R, rev)

    in_specs = [_bspec(r, R, rev) for r in rows]
    in_specs += [_pspec(rows[k], R, rev) for k in prev_ids]
    in_specs += [pl.BlockSpec(p['bs'], functools.partial(lambda jc, j, idx: idx(jc), idx=p['idx'])) for p in params]
    in_specs += [pl.BlockSpec((1,) + tuple(s), lambda jc, j, nd=len(s): (nb - 1 - j,) + (0,) * nd) for s in state_shapes]
    in_specs += [_bspec(d, R, rev) for d in douts]
    in_specs += [dspec(drow_ids[pos]) for pos in add_ids]
    in_specs += [pl.BlockSpec(memory_space=pl.ANY) for _ in alias_ids]
    out_specs = [dspec(k) for k in drow_ids]
    out_shape = [jax.ShapeDtypeStruct(tuple(rows[k]['a'].shape[:-1]) + (rows[k]['dn'],), dx_dt) for k in drow_ids]
    for k in dpar_ids:
        p = params[k]
        out_specs.append(pl.BlockSpec(p['bs'], functools.partial(lambda jc, j, idx: idx(jc), idx=p['idx'])))
        out_shape.append(jax.ShapeDtypeStruct(p['a'].shape, F32))
    if dx_bf:
        out_specs += [dspec(k) for k in drow_ids]
        out_shape += [jax.ShapeDtypeStruct(tuple(rows[k]['a'].shape[:-1]) + (rows[k]['dn'],), BF) for k in drow_ids]
    scratch = [pltpu.VMEM(tuple(s), F32) for s in state_shapes]
    scratch += [pltpu.VMEM(tuple(d for d in rows[k]['lb'] if d is not None) + (_prev_rows(rows[k], R), rows[k]['w']), F32)
                for k in prev_ids]
    args = ([r['a'] for r in rows] + [rows[k]['a'] for k in prev_ids] + [p['a'] for p in params] + list(saved)
            + [d['a'] for d in douts] + [dx_add[pos] for pos in add_ids] + [dx_alias[pos] for pos in alias_ids])
    n_in = len(args)
    aliases = {n_in - n_al + n_: pos for n_, pos in enumerate(alias_ids)}
    for pos in alias_ids:
        assert dx_alias[pos].shape == out_shape[pos].shape and dx_alias[pos].dtype == out_shape[pos].dtype
    res = pl.pallas_call(
        body, name=name, grid=(ncol, nb), in_specs=in_specs, out_specs=out_specs, out_shape=out_shape,
        scratch_shapes=scratch, input_output_aliases=aliases, compiler_params=_cparams())(*args)
    if dx_bf:
        return list(res[:n_dx]), list(res[n_dx:n_dx + n_dp]), list(res[n_dx + n_dp:])
    return list(res[:n_dx]), list(res[n_dx:])


def _tile(n, cands):
    for c in cands:
        if n % c == 0:
            return c
    return n


def _mm_call(name, grid, a, a_spec, b, b_spec, contract, out_shape, out_spec, acc_shape, add=None, add_spec=None):
    nk = grid[2]
    ca, cb = contract
    has_add = add is not None

    def body_one(*refs):
        r = _dot(refs[0][...], refs[1][...], ca, cb)
        if has_add:
            r = r + refs[2][...].astype(F32)
        refs[-1][...] = r.astype(refs[-1].dtype)

    def body(*refs):
        a_ref, b_ref = refs[0], refs[1]
        add_ref = refs[2] if has_add else None
        o_ref, acc = refs[-2], refs[-1]
        k = pl.program_id(2)
        part = _dot(a_ref[...], b_ref[...], ca, cb)

        @pl.when(k == 0)
        def _():
            acc[...] = part

        @pl.when(k > 0)
        def _():
            acc[...] += part

        @pl.when(k == nk - 1)
        def _():
            r = acc[...]
            if has_add:
                r = r + add_ref[...].astype(F32)
            o_ref[...] = r.astype(o_ref.dtype)

    in_specs, args = [a_spec, b_spec], [a, b]
    if has_add:
        in_specs.append(add_spec)
        args.append(add)
    return pl.pallas_call(
        body_one if nk == 1 else body, name=name, grid=grid, in_specs=in_specs, out_specs=out_spec, out_shape=out_shape,
        scratch_shapes=[] if nk == 1 else [pltpu.VMEM(acc_shape, F32)],
        compiler_params=pltpu.CompilerParams(dimension_semantics=("parallel", "parallel", "arbitrary"),
                                             vmem_limit_bytes=VMEM_LIMIT))(*args)


def _matmul(a, b, mode="nn", add=None, out_dtype=F32, name="matmul"):
    if mode == "nn":
        (M, K), N = a.shape, b.shape[1]
    elif mode == "nt":
        (M, K), N = a.shape, b.shape[0]
    else:
        (K, M), N = a.shape, b.shape[1]
    tm = _tile(M, (1024, 512, 256, 128, 64, 32, 16, 8))
    tn = _tile(N, (512, 256, 128))
    tk = K if K <= 5120 else _tile(K, (2048, 1024, 512, 256, 128))
    if mode == "tn":
        a_spec = pl.BlockSpec((tk, tm), lambda i, j, k: (k, i))
    else:
        a_spec = pl.BlockSpec((tm, tk), lambda i, j, k: (i, k))
    if mode == "nt":
        b_spec = pl.BlockSpec((tn, tk), lambda i, j, k: (j, k))
    else:
        b_spec = pl.BlockSpec((tk, tn), lambda i, j, k: (k, j))
    blk = pl.BlockSpec((tm, tn), lambda i, j, k: (i, j))
    return _mm_call(name, (M // tm, N // tn, K // tk), a, a_spec, b, b_spec,
                    {"nn": (1, 0), "nt": (1, 1), "tn": (0, 0)}[mode], jax.ShapeDtypeStruct((M, N), out_dtype), blk,
                    (tm, tn), add, blk)


FF_SH = 2 * D_FF // N_DEV


def _ffn_up(h2, wup, i):
    S, D = h2.shape
    tm = _tile(S, (1024, 512, 256, 128))
    return _mm_call("matmul_up", (S // tm, N_DEV, 1), h2, pl.BlockSpec((tm, D), lambda m, j, k: (m, 0)),
                    wup, pl.BlockSpec((None, None, D, FF_SH), lambda m, j, k: (j, i, 0, 0)), (1, 0),
                    jax.ShapeDtypeStruct((2, N_DEV // 2, S, FF_SH), F32),
                    pl.BlockSpec((None, None, tm, FF_SH), lambda m, j, k: (j // 4, j % 4, m, 0)), (tm, FF_SH))


def _ffn_down(act, wd, x1):
    _, S, _ = act.shape
    D = wd.shape[1]
    tm, tn = _tile(S, (1024, 512, 256, 128)), _tile(D, (512, 256, 128))
    blk = pl.BlockSpec((tm, tn), lambda m, n, k: (m, n))
    return _mm_call("matmul_down", (S // tm, D // tn, N_DEV // 2), act,
                    pl.BlockSpec((None, tm, FF_SH), lambda m, n, k: (k, m, 0)), wd,
                    pl.BlockSpec((FF_SH, tn), lambda m, n, k: (k, n)), (1, 0), jax.ShapeDtypeStruct((S, D), F32), blk,
                    (tm, tn), x1, blk)


def _ffn_dact(dxb, wd):
    S, D = dxb.shape
    tm = _tile(S, (1024, 512, 256, 128))
    return _mm_call("matmul_dact", (S // tm, N_DEV // 2, 1), dxb, pl.BlockSpec((tm, D), lambda m, j, k: (m, 0)), wd,
                    pl.BlockSpec((FF_SH, D), lambda m, j, k: (j, 0)), (1, 1),
                    jax.ShapeDtypeStruct((N_DEV // 2, S, FF_SH), F32),
                    pl.BlockSpec((None, tm, FF_SH), lambda m, j, k: (j, m, 0)), (tm, FF_SH))


def _ffn_dw_down(act, dxb):
    _, S, _ = act.shape
    D = dxb.shape[1]
    tk, tn = _tile(S, (2048, 1024, 512, 256, 128)), _tile(D, (512, 256, 128))
    return _mm_call("matmul_dw_down", (N_DEV // 2, D // tn, S // tk), act,
                    pl.BlockSpec((None, tk, FF_SH), lambda j, n, k: (j, k, 0)), dxb,
                    pl.BlockSpec((tk, tn), lambda j, n, k: (k, n)), (0, 0), jax.ShapeDtypeStruct((D_FF, D), BF),
                    pl.BlockSpec((FF_SH, tn), lambda j, n, k: (j, n)), (FF_SH, tn))


def _ffn_dw_up(h2, du):
    S, D = h2.shape
    tk = _tile(S, (1024, 512, 256, 128))
    return _mm_call("matmul_dw_up", (N_DEV, 1, S // tk), h2, pl.BlockSpec((tk, D), lambda j, n, k: (k, 0)), du,
                    pl.BlockSpec((None, None, tk, FF_SH), lambda j, n, k: (j // 4, j % 4, k, 0)), (0, 0),
                    jax.ShapeDtypeStruct((N_DEV, D, FF_SH), BF),
                    pl.BlockSpec((None, D, FF_SH), lambda j, n, k: (j, 0, 0)), (D, FF_SH))


def _ffn_dh2(du, wup, i):
    S = du.shape[2]
    D = wup.shape[2]
    tm = _tile(S, (1024, 512, 256, 128))
    return _mm_call("matmul_dh2", (S // tm, 1, N_DEV), du,
                    pl.BlockSpec((None, None, tm, FF_SH), lambda m, n, k: (k // 4, k % 4, m, 0)), wup,
                    pl.BlockSpec((None, None, D, FF_SH), lambda m, n, k: (k, i, 0, 0)), (1, 1),
                    jax.ShapeDtypeStruct((S, D), F32), pl.BlockSpec((tm, D), lambda m, n, k: (m, 0)), (tm, D))


def f_rmsnorm(first, xp, xs, ps, sts):
    return (_rms(xs[0], ps[0]),), ()


def f_xattn(first, xp, xs, ps, sts):
    (xq,), (kv, qn, kn) = xs, ps
    qs = _split(xq, [XA_HD] * XA_HEADS)
    kvs = _split(kv, [XA_HD] * (2 * XA_HEADS))
    outs = []
    for h in range(XA_HEADS):
        q = _rms(qs[h], qn)
        k = _rms(kvs[h], kn)
        v = kvs[XA_HEADS + h]
        s = mm_nt(q, k) * (XA_HD ** -0.5)
        m = lax.stop_gradient(jnp.max(s, axis=-1, keepdims=True))
        p = jnp.exp(s - m)
        outs.append(mm_nn(p / jnp.sum(p, axis=-1, keepdims=True), v))
    return (jnp.concatenate(outs, axis=1),), ()


def _conv(xp, x, w, b, first, taps):
    xp = jnp.where(first, 0.0, xp)
    y = b + w[taps - 1:taps] * x
    for d in range(1, taps):
        y = y + w[taps - 1 - d:taps - d] * _shift(xp, x, d)
    return y


def _unstack2(x):
    @jax.custom_vjp
    def us(x):
        return x[0], x[1]

    us.defvjp(lambda x: (us(x), None), lambda r, g: (jnp.stack(g),))
    return us(x)


def f_ffn_act(first, xp, xs, ps, sts):
    (up,), (u,), (wg, wv, bg, bv) = xp, xs, ps
    (ugp, uvp), (ug, uv) = _unstack2(up), _unstack2(u)
    gate = _conv(ugp, ug, wg, bg, first, FFN_CONV)
    val = _conv(uvp, uv, wv, bv, first, FFN_CONV)
    return (jax.nn.silu(gate) * val,), ()


def _gla_chunk(q, k, v, la, st):
    c = q.shape[0]
    b = cumsum_rows(la)
    b_last = _row(b, c - 1)
    b_ref = _row(b, c // 2 - 1)
    att = mm_nt(q * jnp.exp(b - b_ref), k * jnp.exp(b_ref - b))
    att = jnp.where(_tril(c), att, 0.0)
    o = mm_nn(att, v) + mm_nt(q * jnp.exp(b), st)
    st_new = st * jnp.exp(b_last) + mm_tn(v, k * jnp.exp(b_last - b))
    return o, st_new


def _a_cols(ntot):
    used = D_XA + 2 * GLA_HEADS * GLA_DK + D_MIX + GLA_RANK + D_MIX
    return [D_XA, GLA_HEADS * GLA_DK, GLA_HEADS * GLA_DK, D_MIX, GLA_RANK, D_MIX] + ([ntot - used] if ntot > used else [])


def f_gla(first, xp, xs, ps, sts):
    (p,), (wg2, bg, on) = xs, ps
    parts = _split(p, _a_cols(p.shape[1]))
    q, k, v, glr, og = parts[1:6]
    la = jax.nn.log_sigmoid(mm_nn(glr, wg2) + bg) / GLA_GATE_NORM
    qs = _split(q * (GLA_DK ** -0.5), [GLA_DK] * GLA_HEADS)
    ks = _split(k, [GLA_DK] * GLA_HEADS)
    vs = _split(v, [GLA_DV] * GLA_HEADS)
    las = _split(la, [GLA_DK] * GLA_HEADS)
    outs, new = [], []
    for h in range(GLA_HEADS):
        o, s = _gla_chunk(qs[h], ks[h], vs[h], las[h], sts[h])
        outs.append(_rms(o, on))
        new.append(s)
    return (jnp.concatenate(outs, axis=1) * jax.nn.silu(og),), tuple(new)


def f_hgrn(first, xp, xs, ps, sts):
    (p,), (lbp, on) = xs, ps
    _, q, fgate, iv, og = _split(p, [D_XA, D_MIX, D_MIX, D_MIX, D_MIX])
    e = jnp.exp(lbp - jnp.max(lbp, axis=0, keepdims=True))
    row = lax.broadcasted_iota(jnp.int32, e.shape, 0)
    lb = jnp.sum(jnp.where(row >= 1, e, 0.0), axis=0, keepdims=True) / jnp.sum(e, axis=0, keepdims=True)
    fg = lb + (1.0 - lb) * jax.nn.sigmoid(fgate)
    qs = _split(jax.nn.silu(q), [HGRN_DK] * HGRN_HEADS)
    ks = _split(1.0 - fg, [HGRN_DK] * HGRN_HEADS)
    vs = _split(iv, [HGRN_DV] * HGRN_HEADS)
    las = _split(jnp.log(fg), [HGRN_DK] * HGRN_HEADS)
    outs, new = [], []
    for h in range(HGRN_HEADS):
        o, s = _gla_chunk(qs[h], ks[h], vs[h], las[h], sts[h])
        outs.append(_rms(o, on))
        new.append(s)
    return (jnp.concatenate(outs, axis=1) * jax.nn.sigmoid(og),), tuple(new)


def _c_cols(ntot):
    gn = SSM_GROUPS * SSM_STATE
    used = D_XA + D_MIX + D_MIX + 2 * gn + SSM_HEADS
    return [D_XA, D_MIX, D_MIX + 2 * gn, SSM_HEADS] + ([ntot - used] if ntot > used else [])


def f_ssd(first, xp, xs, ps, sts):
    (pp,), (p,), (cw, cb, dtb, alog, dsk, ng) = xp, xs, ps
    c = p.shape[0]
    gn = SSM_GROUPS * SSM_STATE
    hg = SSM_HEADS // SSM_GROUPS
    _, z, xbc, dtr = _split(p, _c_cols(p.shape[1]))[:4]
    xbc_p = _split(pp, _c_cols(p.shape[1]))[2]
    xbc = jax.nn.silu(_conv(xbc_p, xbc, cw, cb, first, SSM_CONV))
    xs_, bm, cm = _split(xbc, [D_MIX, gn, gn])
    dt = jax.nn.softplus(dtr + dtb)
    a = dt * (-jnp.exp(alog))
    acs = cumsum_rows(a)
    acs_t = cumsum_rows_t(a)
    acs_last = _row(acs, c - 1)
    dt_h = _split(dt, [1] * SSM_HEADS)
    acs_h = _split(acs, [1] * SSM_HEADS)
    al_h = _split(acs_last, [1] * SSM_HEADS)
    d_h = _split(dsk, [1] * SSM_HEADS)
    x2s = _split(xs_, [2 * SSM_HD] * (SSM_HEADS // 2))
    bms = _split(bm, [SSM_STATE] * SSM_GROUPS)
    cms = _split(cm, [SSM_STATE] * SSM_GROUPS)
    tril = _tril(c)
    cbs = [mm_nt(cms[g], bms[g]) for g in range(SSM_GROUPS)]
    ys, new = [], []
    for j in range(SSM_HEADS // 2):
        g = (2 * j) // hg
        h0, h1 = 2 * j, 2 * j + 1
        xdt = x2s[j] * _lane_pair(dt_h[h0], dt_h[h1])
        acs2 = _lane_pair(acs_h[h0], acs_h[h1])
        al2 = _lane_pair(al_h[h0], al_h[h1])
        yd = []
        for h in (h0, h1):
            seg = acs_h[h] - _row(acs_t, h)
            lm = jnp.exp(jnp.where(tril, seg, NEG))
            yd.append(mm_nn(cbs[g] * lm, xdt))
        lane = lax.broadcasted_iota(jnp.int32, xdt.shape, 1)
        y_diag = jnp.where(lane < SSM_HD, yd[0], yd[1])
        y_off = mm_nn(cms[g], sts[j]) * jnp.exp(acs2)
        x_end = xdt * jnp.exp(al2 - acs2)
        new.append(sts[j] * jnp.exp(al2) + mm_tn(bms[g], x_end))
        ys.append(y_diag + y_off + _lane_pair(d_h[h0], d_h[h1]) * x2s[j])
    y = jnp.concatenate(ys, axis=1) * jax.nn.silu(z)
    gw = D_MIX // SSM_GROUPS
    yg = _split(y, [gw] * SSM_GROUPS)
    ngs = _split(ng, [gw] * SSM_GROUPS)
    y = jnp.concatenate([_rms(yg[g], ngs[g]) for g in range(SSM_GROUPS)], axis=1)
    return (y,), tuple(new)


def f_dil_prep(first, xp, xs, ps, sts):
    (p, pos), (qn, kn, invf, sign) = xs, ps
    nh = len(DIL_GROUPS) * DIL_HEADS
    _, q, k, v = _split(p, [D_XA] + [nh * DIL_HD] * 3)
    ang = pos * invf
    cos, sin = jnp.cos(ang), jnp.sin(ang) * sign

    def rope(t, g):
        hs = _split(t, [DIL_HD] * nh)
        out = []
        for h in hs:
            n = _rms(h, g)
            out.append(n * cos + _roll_half(n) * sin)
        return jnp.concatenate(out, axis=1)

    return (rope(q, qn), rope(k, kn), v), ()


def f_dil_attn(first, xp, xs, ps, sts):
    (kp, vp), (q, k, v) = xp, xs
    Q = DIL_BLOCK
    qs, ks, vs = (_split(t, [DIL_HD] * DIL_HEADS) for t in (q, k, v))
    kps, vps = (_split(t, [DIL_HD] * DIL_HEADS) for t in (kp, vp))
    i = lax.broadcasted_iota(jnp.int32, (Q, 2 * Q), 0)
    j = lax.broadcasted_iota(jnp.int32, (Q, 2 * Q), 1)
    dist = Q + i - j
    mask = (dist >= 0) & (dist <= Q) & (jnp.logical_not(first) | (j >= Q))
    outs, lses = [], []
    for h in range(DIL_HEADS):
        k2 = jnp.concatenate([kps[h], ks[h]], axis=0)
        v2 = jnp.concatenate([vps[h], vs[h]], axis=0)
        s = jnp.where(mask, mm_nt(qs[h], k2) * (DIL_HD ** -0.5), NEG)
        m = lax.stop_gradient(jnp.max(s, axis=-1, keepdims=True))
        p = jnp.exp(s - m)
        l = jnp.sum(p, axis=-1, keepdims=True)
        outs.append(mm_nn(p / l, v2))
        lses.append(jnp.broadcast_to(m + jnp.log(l), (Q, DIL_HD)))
    return (jnp.concatenate(outs, axis=1), jnp.concatenate(lses, axis=1)), ()


def f_dil_merge(first, xp, xs, ps, sts):
    o0, o1, o2, l0, l1, l2 = xs
    m = jnp.maximum(jnp.maximum(l0, l1), l2)
    e0, e1, e2 = jnp.exp(l0 - m), jnp.exp(l1 - m), jnp.exp(l2 - m)
    den = e0 + e1 + e2
    return ((e0 * o0 + e1 * o1 + e2 * o2) / den,), ()


def _loss_head(y, target):
    S, D = y.shape
    R = _tile(S, (512, 256, 128, 64, 32, 16, 8))

    def body(y_ref, t_ref, dy_ref, dyb_ref, l_ref):
        e = y_ref[...] - t_ref[...]
        dy_ref[...] = e * (1.0 / D)
        dyb_ref[...] = (e * (1.0 / D)).astype(BF)

        @pl.when(pl.program_id(0) == 0)
        def _():
            l_ref[...] = jnp.zeros_like(l_ref)

        l_ref[...] += jnp.broadcast_to(0.5 * jnp.sum(jnp.mean(e * e, axis=-1, keepdims=True), axis=0, keepdims=True),
                                       l_ref.shape)

    blk = pl.BlockSpec((R, D), lambda i: (i, 0))
    dy, dyb, l = pl.pallas_call(
        body, name="loss_head", grid=(S // R,), in_specs=[blk, blk],
        out_specs=[blk, blk, pl.BlockSpec((8, 128), lambda i: (0, 0))],
        out_shape=[jax.ShapeDtypeStruct((S, D), F32), jax.ShapeDtypeStruct((S, D), BF),
                   jax.ShapeDtypeStruct((8, 128), F32)],
        compiler_params=pltpu.CompilerParams(dimension_semantics=("arbitrary",)))(y, target)
    return dy, dyb, l[0, 0]


def _adamw(parts, w, m, v, name):
    _, n, width = parts.shape
    tr = _tile(n, [t for t in (512, 256, 128, 64, 32, 16, 8) if t * width <= ADAM_BLOCK])

    def body(p_ref, w_ref, m_ref, v_ref, g_ref, d_ref, nm_ref, nv_ref):
        g = p_ref[0].astype(F32)
        for s in range(1, N_DEV):
            g = g + p_ref[s].astype(F32)
        nm = ADAM_B1 * m_ref[...] + (1.0 - ADAM_B1) * g
        nv = ADAM_B2 * v_ref[...] + (1.0 - ADAM_B2) * (g * g)
        m_hat = nm / (1.0 - ADAM_B1 ** ADAM_STEP)
        v_hat = nv / (1.0 - ADAM_B2 ** ADAM_STEP)
        g_ref[...] = g
        d_ref[...] = -ADAM_LR * (m_hat / (jnp.sqrt(v_hat) + ADAM_EPS) + ADAM_WD * w_ref[...])
        nm_ref[...] = nm
        nv_ref[...] = nv

    blk = pl.BlockSpec((tr, width), lambda i: (i, 0))
    return pl.pallas_call(
        body, name=name, grid=(n // tr,),
        in_specs=[pl.BlockSpec((N_DEV, tr, width), lambda i: (0, i, 0)), blk, blk, blk],
        out_specs=[blk] * 4, out_shape=[jax.ShapeDtypeStruct((n, width), F32)] * 4,
        compiler_params=pltpu.CompilerParams(dimension_semantics=("arbitrary",), vmem_limit_bytes=VMEM_LIMIT))(
            parts, w, m, v)


def _peer(k):
    x, y, c = lax.axis_index("x"), lax.axis_index("y"), lax.axis_index("c")
    px = 1 - x if k & 4 else x
    py = 1 - y if k & 2 else y
    pc = 1 - c if k & 1 else c
    return (px, py, pc), 4 * px + 2 * py + pc


def _my_id():
    return 4 * lax.axis_index("x") + 2 * lax.axis_index("y") + lax.axis_index("c")


def _all_gather(x, name):
    def body(x_ref, out_ref, send, recv, loc):
        me = _my_id()
        mine = pltpu.make_async_copy(x_ref, out_ref.at[me], loc)
        mine.start()
        cps = []
        for k in range(1, N_DEV):
            peer, _ = _peer(k)
            cp = pltpu.make_async_remote_copy(src_ref=x_ref, dst_ref=out_ref.at[me], send_sem=send.at[k - 1],
                                              recv_sem=recv.at[k - 1], device_id=peer,
                                              device_id_type=pl.DeviceIdType.MESH)
            cp.start()
            cps.append(cp)
        for k in range(1, N_DEV):
            peer, pid = _peer(k)
            pltpu.make_async_remote_copy(src_ref=x_ref, dst_ref=out_ref.at[pid], send_sem=send.at[k - 1],
                                         recv_sem=recv.at[k - 1], device_id=peer,
                                         device_id_type=pl.DeviceIdType.MESH).wait_recv()
        for cp in cps:
            cp.wait_send()
        mine.wait()

    return pl.pallas_call(
        body, name=name, out_shape=jax.ShapeDtypeStruct((N_DEV,) + x.shape, x.dtype),
        in_specs=[pl.BlockSpec(memory_space=pl.ANY)], out_specs=pl.BlockSpec(memory_space=pl.ANY),
        scratch_shapes=[pltpu.SemaphoreType.DMA((N_DEV - 1,)), pltpu.SemaphoreType.DMA((N_DEV - 1,)),
                        pltpu.SemaphoreType.DMA],
        compiler_params=pltpu.CompilerParams(has_side_effects=True))(x)


def _exchange_many(gs, name):
    n = len(gs)

    def body(*refs):
        g_refs, out_refs, (send, recv, loc) = refs[:n], refs[n:2 * n], refs[2 * n:]
        me = _my_id()
        mine = [pltpu.make_async_copy(g.at[me], o.at[me], loc.at[w]) for w, (g, o) in enumerate(zip(g_refs, out_refs))]
        for cp in mine:
            cp.start()
        cps = []
        for k in range(1, N_DEV):
            peer, pid = _peer(k)
            for w, (g, o) in enumerate(zip(g_refs, out_refs)):
                s = w * (N_DEV - 1) + k - 1
                cp = pltpu.make_async_remote_copy(src_ref=g.at[pid], dst_ref=o.at[me], send_sem=send.at[s],
                                                  recv_sem=recv.at[s], device_id=peer,
                                                  device_id_type=pl.DeviceIdType.MESH)
                cp.start()
                cps.append(cp)
        for k in range(1, N_DEV):
            peer, pid = _peer(k)
            for w, (g, o) in enumerate(zip(g_refs, out_refs)):
                s = w * (N_DEV - 1) + k - 1
                pltpu.make_async_remote_copy(src_ref=g.at[me], dst_ref=o.at[pid], send_sem=send.at[s],
                                             recv_sem=recv.at[s], device_id=peer,
                                             device_id_type=pl.DeviceIdType.MESH).wait_recv()
        for cp in cps:
            cp.wait_send()
        for cp in mine:
            cp.wait()

    return pl.pallas_call(
        body, name=name, out_shape=[jax.ShapeDtypeStruct(g.shape, g.dtype) for g in gs],
        in_specs=[pl.BlockSpec(memory_space=pl.ANY)] * n, out_specs=[pl.BlockSpec(memory_space=pl.ANY)] * n,
        scratch_shapes=[pltpu.SemaphoreType.DMA((n * (N_DEV - 1),)), pltpu.SemaphoreType.DMA((n * (N_DEV - 1),)),
                        pltpu.SemaphoreType.DMA((n,))],
        compiler_params=pltpu.CompilerParams(has_side_effects=True))(*gs)


def _gather_many(xs, name):
    n = len(xs)

    def body(*refs):
        x_refs, out_refs, (send, recv, loc) = refs[:n], refs[n:2 * n], refs[2 * n:]
        x, y, c = lax.axis_index("x"), lax.axis_index("y"), lax.axis_index("c")
        me, sibling = (x, y, c), (x, y, 1 - c)
        chips = [(1 - x, y), (x, 1 - y), (1 - x, 1 - y)]

        def slot(p):
            return 4 * p[0] + 2 * p[1] + p[2]

        def copy(w, k, block, to, src=None):
            dst = out_refs[w].at[slot(block)]
            return pltpu.make_async_remote_copy(src_ref=dst if src is None else src, dst_ref=dst,
                                                send_sem=send.at[w * (N_DEV - 1) + k], recv_sem=recv.at[w * (N_DEV - 1) + k],
                                                device_id=to, device_id_type=pl.DeviceIdType.MESH)

        mine = [pltpu.make_async_copy(x_refs[w], out_refs[w].at[slot(me)], loc.at[w]) for w in range(n)]
        for cp in mine:
            cp.start()
        first = []
        for j, chip in enumerate(chips):
            first += [copy(w, 1 + j, me, (*chip, c), src=x_refs[w]) for w in range(n)]
        first += [copy(w, 0, me, sibling, src=x_refs[w]) for w in range(n)]
        for cp in first:
            cp.start()
        passed = []
        for j, chip in enumerate(chips):
            for w in range(n):
                copy(w, 1 + j, (*chip, c), me).wait_recv()
                cp = copy(w, 4 + j, (*chip, c), sibling)
                cp.start()
                passed.append(cp)
        for w in range(n):
            copy(w, 0, sibling, me).wait_recv()
            for j, chip in enumerate(chips):
                copy(w, 4 + j, (*chip, 1 - c), me).wait_recv()
        for cp in first + passed:
            cp.wait_send()
        for cp in mine:
            cp.wait()

    return pl.pallas_call(
        body, name=name, out_shape=[jax.ShapeDtypeStruct((N_DEV,) + x.shape, x.dtype) for x in xs],
        in_specs=[pl.BlockSpec(memory_space=pl.ANY)] * n, out_specs=[pl.BlockSpec(memory_space=pl.ANY)] * n,
        scratch_shapes=[pltpu.SemaphoreType.DMA((n * (N_DEV - 1),)), pltpu.SemaphoreType.DMA((n * (N_DEV - 1),)),
                        pltpu.SemaphoreType.DMA((n,))],
        compiler_params=pltpu.CompilerParams(has_side_effects=True))(*xs)


def _cat_segs(G, ws, n_mix):
    segs = []
    for g in range(G):
        lo, hi = g * ws, (g + 1) * ws
        if lo < n_mix:
            segs.append((g, 0, min(hi, n_mix) - lo, D_XA + lo))
        if hi > n_mix:
            s = max(lo, n_mix)
            segs.append((g, s - lo, hi - s, s - n_mix))
    return segs


def _cat_cols(src, n_mix, ntot):
    G, R, ws = src.shape
    segs = _cat_segs(G, ws, n_mix)
    tr = _tile(R, (256, 128, 64, 32, 16, 8))

    def body(i_ref, o_ref):
        if ntot > G * ws:
            o_ref[...] = jnp.zeros_like(o_ref)
        for g, s, n, d in segs:
            o_ref[:, d:d + n] = i_ref[g][:, s:s + n]

    return pl.pallas_call(
        body, name="cat_cols", grid=(R // tr,), in_specs=[pl.BlockSpec((G, tr, ws), lambda i: (0, i, 0))],
        out_specs=pl.BlockSpec((tr, ntot), lambda i: (i, 0)), out_shape=jax.ShapeDtypeStruct((R, ntot), src.dtype),
        compiler_params=pltpu.CompilerParams(dimension_semantics=("arbitrary",)))(src)


def _uncat_cols(dw, G, ws, n_mix):
    R, ntot = dw.shape
    segs = _cat_segs(G, ws, n_mix)
    tr = _tile(R, (256, 128, 64, 32, 16, 8))

    def body(i_ref, o_ref):
        v = i_ref[...]
        for g, s, n, d in segs:
            o_ref[g, :, s:s + n] = v[:, d:d + n]

    return pl.pallas_call(
        body, name="uncat_cols", grid=(R // tr,), in_specs=[pl.BlockSpec((tr, ntot), lambda i: (i, 0))],
        out_specs=pl.BlockSpec((G, tr, ws), lambda i: (0, i, 0)), out_shape=jax.ShapeDtypeStruct((G, R, ws), dw.dtype),
        compiler_params=pltpu.CompilerParams(dimension_semantics=("arbitrary",)))(dw)


PACK_W = 1024


def _granule(n):
    return (256 if n >= 256 * PACK_W else 8) * PACK_W


def _pack(arrs, dtype):
    flat = jnp.concatenate([a.reshape(-1).astype(dtype) for a in arrs])
    n = flat.shape[0]
    pad = (-n) % _granule(n)
    if pad:
        flat = jnp.concatenate([flat, jnp.zeros((pad,), dtype)])
    return flat.reshape(-1, PACK_W)


def _unpack(packed, shapes):
    flat = packed.reshape(-1)
    out, o = [], 0
    for s in shapes:
        n = math.prod(s)
        out.append(flat[o:o + n].reshape(s))
        o += n
    return out


def _pack_lead(arrs, dtype):
    flat = jnp.concatenate([a.reshape(N_DEV, -1).astype(dtype) for a in arrs], axis=1)
    n = flat.shape[1]
    pad = (-n) % _granule(n)
    if pad:
        flat = jnp.concatenate([flat, jnp.zeros((N_DEV, pad), dtype)], axis=1)
    return flat.reshape(N_DEV, -1, PACK_W)


def _to_full(stacked, axis):
    t = jnp.moveaxis(stacked, 0, axis)
    s = list(t.shape)
    return t.reshape(s[:axis] + [s[axis] * s[axis + 1]] + s[axis + 2:])


def _to_chunks(full, axis):
    s = list(full.shape)
    t = full.reshape(s[:axis] + [N_DEV, s[axis] // N_DEV] + s[axis + 1:])
    return jnp.moveaxis(t, axis, 0)


def _rows_of(S):
    return _tile(S, (512, 256, 128, 64))


def _norm_fwd(x, g, dt=BF):
    (h,), _ = _seq_fwd("rmsnorm_fwd", f_rmsnorm, _rows_of(x.shape[0]), [_row_spec(x)], [_par_spec(g)], [],
                       [_out_spec(x.shape[1], dt=dt)])
    return h


def _norm_bwd(x, g, dh, res=None):
    if res is None:
        (dx,), (dg,) = _seq_bwd("rmsnorm_bwd", f_rmsnorm, _rows_of(x.shape[0]), [_row_spec(x)], [_par_spec(g)], [],
                                [_row_spec(dh)], [])
        return dx, None, dg
    (dx,), (dg,), (dxb,) = _seq_bwd("rmsnorm_res_bwd", f_rmsnorm, _rows_of(x.shape[0]), [_row_spec(x)], [_par_spec(g)], [],
                                    [_row_spec(dh)], [], dx_add={0: res}, dx_bf=True)
    return dx, dxb, dg


def _mixer_specs(kind, S, p, w):
    if kind == 0:
        return (f_gla, CHUNK, [_row_spec(p)],
                [_par_spec(w['a_w_gate2']), _par_spec(w['a_b_gate'].reshape(1, -1)), _par_spec(w['a_o_norm'].reshape(1, -1))],
                [(GLA_DV, GLA_DK)] * GLA_HEADS, D_MIX)
    if kind == 2:
        return (f_ssd, CHUNK, [_row_spec(p, prev='halo')],
                [_par_spec(w['c_conv_w']), _par_spec(w['c_conv_b'].reshape(1, -1)), _par_spec(w['c_dt_bias'].reshape(1, -1)),
                 _par_spec(w['c_a_log'].reshape(1, -1)), _par_spec(w['c_d'].reshape(1, -1)),
                 _par_spec(w['c_norm'].reshape(1, -1))],
                [(SSM_STATE, 2 * SSM_HD)] * (SSM_HEADS // 2), D_MIX)
    return (f_hgrn, CHUNK, [_row_spec(p)],
            [_par_spec(w['d_lower_bounds']), _par_spec(w['d_o_norm'].reshape(1, -1))],
            [(HGRN_DV, HGRN_DK)] * HGRN_HEADS, D_MIX)


def _perm(t, r):
    if r == 1:
        return t
    S, n = t.shape
    return t.reshape(S // r, r, n).transpose(1, 0, 2).reshape(S, n)


def _unperm(t, r):
    if r == 1:
        return t
    S, n = t.shape
    return t.reshape(r, S // r, n).transpose(1, 0, 2).reshape(S, n)


def _rope_consts():
    half = DIL_HD // 2
    inv = ROPE_THETA ** (-jnp.arange(half, dtype=F32) / half)
    invf = jnp.concatenate([inv, inv]).reshape(1, DIL_HD)
    sign = jnp.concatenate([-jnp.ones((half,), F32), jnp.ones((half,), F32)]).reshape(1, DIL_HD)
    return invf, sign


def _dil_fwd(p, pos, w):
    S = p.shape[0]
    invf, sign = _rope_consts()
    prep_rows = [_row_spec(p), _row_spec(pos, diff=False)]
    prep_pars = [_par_spec(w['b_q_norm'].reshape(1, -1)), _par_spec(w['b_k_norm'].reshape(1, -1)),
                 _par_spec(invf, diff=False), _par_spec(sign, diff=False)]
    nqk = len(DIL_GROUPS) * D_DIL
    (qr, kr, v), _ = _seq_fwd("dil_prep_fwd", f_dil_prep, _tile(S, (256, 128)), prep_rows, prep_pars, [],
                              [_out_spec(nqk), _out_spec(nqk), _out_spec(nqk)])
    res = dict(perm=[], o=[], lse=[])
    for g, (window, r) in enumerate(DIL_GROUPS):
        sl = slice(g * D_DIL, (g + 1) * D_DIL)
        qp, kp, vp = _perm(qr[:, sl], r), _perm(kr[:, sl], r), _perm(v[:, sl], r)
        rows = [_row_spec(qp), _row_spec(kp, prev='block'), _row_spec(vp, prev='block')]
        (o, lse), _ = _seq_fwd("dil_attn_fwd", f_dil_attn, DIL_BLOCK, rows, [], [], [_out_spec(D_DIL), _out_spec(D_DIL)],
                               period=S // r // DIL_BLOCK)
        res['perm'].append((qp, kp, vp))
        res['o'].append(_unperm(o, r))
        res['lse'].append(_unperm(lse, r))
    mrows = [_row_spec(t) for t in res['o'] + res['lse']]
    (tok,), _ = _seq_fwd("dil_merge_fwd", f_dil_merge, _rows_of(S), mrows, [], [], [_out_spec(D_DIL)])
    res['prep'] = (prep_rows, prep_pars)
    return tok, res


def _dil_bwd(dtok, res, p):
    S = p.shape[0]
    mrows = [_row_spec(t) for t in res['o'] + res['lse']]
    dm, _ = _seq_bwd("dil_merge_bwd", f_dil_merge, _rows_of(S), mrows, [], [], [dtok], [])
    dq, dk, dv = [], [], []
    for g, (window, r) in enumerate(DIL_GROUPS):
        qp, kp, vp = res['perm'][g]
        rows = [_row_spec(qp), _row_spec(kp, prev='block'), _row_spec(vp, prev='block')]
        douts = [_row_spec(_perm(dm[g], r)), _row_spec(_perm(dm[3 + g], r))]
        (a, b, c), _ = _seq_bwd("dil_attn_bwd", f_dil_attn, DIL_BLOCK, rows, [], [], douts, [],
                                period=S // r // DIL_BLOCK)
        dq.append(_unperm(a, r)); dk.append(_unperm(b, r)); dv.append(_unperm(c, r))
    dqr, dkr, dv = (jnp.concatenate(t, axis=1) for t in (dq, dk, dv))
    prep_rows, prep_pars = res['prep']
    (dp,), (dqn, dkn) = _seq_bwd("dil_prep_bwd", f_dil_prep, _tile(S, (256, 128)), prep_rows, prep_pars, [],
                                 [_row_spec(dqr), _row_spec(dkr), _row_spec(dv)], [], dx_dt=BF)
    return dp, dict(b_q_norm=dqn.reshape(-1), b_k_norm=dkn.reshape(-1))


def _ffn_specs(u, cw, cb):
    half = N_DEV // 2
    rows = [_row_spec(u, prev='halo', lb=(2, None), li=lambda jc: (0, jc))]
    pars = [_par_spec(cw, bs=(None, FFN_CONV, FF_SH), idx=lambda jc: (jc, 0, 0)),
            _par_spec(cw, bs=(None, FFN_CONV, FF_SH), idx=lambda jc: (jc + half, 0, 0)),
            _par_spec(cb, bs=(None, 1, FF_SH), idx=lambda jc: (jc, 0, 0)),
            _par_spec(cb, bs=(None, 1, FF_SH), idx=lambda jc: (jc + half, 0, 0))]
    return half, rows, pars


N_MIX = {0: 2 * GLA_HEADS * GLA_DK + 2 * D_MIX + GLA_RANK, 1: 3 * len(DIL_GROUPS) * D_DIL,
         2: 2 * D_MIX + 2 * SSM_GROUPS * SSM_STATE + SSM_HEADS, 3: 2 * HGRN_HEADS * HGRN_DK + 2 * D_MIX}
W_IN = {0: 'a_w_in', 1: 'b_w_in', 2: 'c_w_in', 3: 'd_w_in'}
W_OUT = {0: 'a_w_out', 1: 'b_w_out', 2: 'c_w_out', 3: 'd_w_out'}


def _in_blocks(name, t):
    return t if SHARD_AXIS[name] == 1 else t.reshape(1, N_DEV * t.shape[1], t.shape[2])


def _device_step(x, mem, pos, gw, rep, target):
    S, D = x.shape
    w = dict(rep)
    w['a_w_gate2'] = _to_full(gw['a_w_gate2'], 1)
    w['c_conv_w'] = _to_full(gw['c_conv_w'], 1)
    posf = pos.reshape(S, 1).astype(F32)
    n_mix, w_in_name, w_out_name = N_MIX, W_IN, W_OUT
    ntot = {k: -(-(n_mix[k] + D_XA) // 256) * 256 for k in n_mix}
    in_blocks = {k: _in_blocks(w_in_name[k], gw[w_in_name[k]]) for k in n_mix}
    w_out = {k: (_to_full(gw[w_out_name[k]], 1) if SHARD_AXIS[w_out_name[k]] == 1
                 else gw[w_out_name[k]].reshape(-1, D)) for k in n_mix}
    mem_g = w['mem_norm'].reshape(1, -1)
    mem_n = _norm_fwd(mem, mem_g)
    R = _rows_of(S)
    wup = gw['ffn_w_up']

    saved = []
    for i in range(DEPTH):
        kind = i % 4
        L = dict(x0=x)
        g1 = w['mix_norm'][i].reshape(1, -1)
        h = _norm_fwd(x, g1)
        wcat = _cat_cols(in_blocks[kind], n_mix[kind], ntot[kind])
        p = _matmul(h, wcat, name="matmul_in")
        if kind == 1:
            tok, L['dil'] = _dil_fwd(p, posf, w)
        else:
            f, Rm, rows, pars, sshapes, _ = _mixer_specs(kind, S, p, w)
            (tok,), L['states'] = _seq_fwd("mixer%d_fwd" % kind, f, Rm, rows, pars, sshapes, [_out_spec(D_MIX)],
                                           save_states=True)
        wkv = gw['xa_w_kv'][:, i].reshape(D, 2 * D_XA)
        kv = _matmul(mem_n, wkv, name="matmul_kv")
        xa_rows = [_row_spec(p, w=D_XA, dn=D_XA)]
        xa_pars = [_par_spec(kv), _par_spec(w['xa_q_norm'][i].reshape(1, -1)), _par_spec(w['xa_k_norm'][i].reshape(1, -1))]
        (xa,), _ = _seq_fwd("xattn_fwd", f_xattn, R, xa_rows, xa_pars, [], [_out_spec(D_XA)])
        cat = jnp.concatenate([tok, xa], axis=1).astype(BF)
        x1 = _matmul(cat, w_out[kind], add=x, name="matmul_out")
        g2 = w['ffn_norm'][i].reshape(1, -1)
        h2 = _norm_fwd(x1, g2)
        u = _ffn_up(h2, wup, i)
        cw, cb = gw['ffn_conv_w'][:, i], w['ffn_conv_b'][i].reshape(N_DEV, 1, FF_SH)
        nt, frows, fpars = _ffn_specs(u, cw, cb)
        (act,), _ = _seq_fwd("ffn_act_fwd", f_ffn_act, R, frows, fpars, [],
                             [_out_spec(FF_SH, dt=BF, ls=(nt,), lb=(None,), li=lambda jc: (jc,))], ncol=nt)
        wd = gw['ffn_w_down'][:, i].reshape(D_FF, D)
        x = _ffn_down(act, wd, x1)
        L.update(h=h, p=p, wcat=wcat, kv=kv, wkv=wkv, cat=cat, x1=x1, h2=h2, u=u, act=act, wd=wd, g1=g1, g2=g2)
        saved.append(L)

    dx, dxb, loss = _loss_head(x, target)

    G, Gc = {}, {}
    d_mem_n = None
    acc = {k: [None] * DEPTH for k in ('mix_norm', 'ffn_norm', 'ffn_conv_b', 'xa_q_norm', 'xa_k_norm')}
    accc = {k: [None] * DEPTH for k in ('ffn_w_up', 'ffn_conv_w', 'ffn_w_down', 'xa_w_kv')}
    half = N_DEV // 2
    for i in reversed(range(DEPTH)):
        kind = i % 4
        L = saved[i]
        accc['ffn_w_down'][i] = _ffn_dw_down(L['act'], dxb).reshape(N_DEV, D_FF // N_DEV, D)
        dact = _ffn_dact(dxb, L['wd'])
        cw, cb = gw['ffn_conv_w'][:, i], w['ffn_conv_b'][i].reshape(N_DEV, 1, FF_SH)
        nt, frows, fpars = _ffn_specs(L['u'], cw, cb)
        (du,), (dwg, dwv, dbg, dbv) = _seq_bwd(
            "ffn_act_bwd", f_ffn_act, R, frows, fpars, [], [_row_spec(dact, lb=(None,), li=lambda jc: (jc,))], [],
            ncol=nt, dx_dt=BF)
        accc['ffn_conv_w'][i] = jnp.concatenate([dwg[:half], dwv[half:]], axis=0)
        acc['ffn_conv_b'][i] = jnp.concatenate([dbg[:half], dbv[half:]], axis=0).reshape(-1)
        accc['ffn_w_up'][i] = _ffn_dw_up(L['h2'], du)
        dh2 = _ffn_dh2(du, wup, i)
        dx1, dx1b, dg2 = _norm_bwd(L['x1'], L['g2'], dh2, res=dx)
        acc['ffn_norm'][i] = dg2.reshape(-1)
        G_out = _matmul(L['cat'], dx1b, mode="tn", out_dtype=BF, name="matmul_dw_out")
        dcat = _matmul(dx1b, w_out[kind], mode="nt", name="matmul_dcat")
        ntok = D_DIL if kind == 1 else D_MIX
        dtok = _row_spec(dcat, w=ntok)
        dxa = _row_spec(dcat, w=D_XA, c=lambda jc: ntok // D_XA)
        p = L['p']
        if kind == 1:
            dp, gm = _dil_bwd(dtok, L['dil'], p)
            G.update(gm)
        else:
            f, Rm, rows, pars, sshapes, _ = _mixer_specs(kind, S, p, w)
            (dp,), dps = _seq_bwd("mixer%d_bwd" % kind, f, Rm, rows, pars, sshapes, [dtok], L['states'], dx_dt=BF)
            if kind == 0:
                Gc['a_w_gate2'], G['a_b_gate'], G['a_o_norm'] = _to_chunks(dps[0], 1), dps[1].reshape(-1), dps[2].reshape(-1)
            elif kind == 2:
                Gc['c_conv_w'] = _to_chunks(dps[0], 1)
                for nme, v in zip(('c_conv_b', 'c_dt_bias', 'c_a_log', 'c_d', 'c_norm'), dps[1:]):
                    G[nme] = v.reshape(-1)
            else:
                G['d_lower_bounds'], G['d_o_norm'] = dps[0], dps[1].reshape(-1)
        xa_rows = [_row_spec(p, w=D_XA)]
        xa_pars = [_par_spec(L['kv']), _par_spec(w['xa_q_norm'][i].reshape(1, -1)), _par_spec(w['xa_k_norm'][i].reshape(1, -1))]
        (dp,), (dkv, dqn, dkn) = _seq_bwd("xattn_bwd", f_xattn, R, xa_rows, xa_pars, [], [dxa], [], dx_dt=BF,
                                          dx_alias={0: dp})
        acc['xa_q_norm'][i], acc['xa_k_norm'][i] = dqn.reshape(-1), dkn.reshape(-1)
        accc['xa_w_kv'][i] = _matmul(mem_n, dkv, mode="tn", out_dtype=BF, name="matmul_dw_kv").reshape(
            N_DEV, D // N_DEV, 2 * D_XA)
        d_mem_n = _matmul(dkv, L['wkv'], mode="nt", add=d_mem_n, name="matmul_dmem" + ("" if d_mem_n is None else "_acc"))
        dwcat = _matmul(L['h'], dp, mode="tn", out_dtype=BF, name="matmul_dw_in")
        blocks = in_blocks[kind]
        Gc[w_in_name[kind]] = _uncat_cols(dwcat, blocks.shape[0], blocks.shape[2], n_mix[kind]).reshape(
            gw[w_in_name[kind]].shape)
        Gc[w_out_name[kind]] = (_to_chunks(G_out, 1) if SHARD_AXIS[w_out_name[kind]] == 1
                                else G_out.reshape(gw[w_out_name[kind]].shape))
        dh = _matmul(dp, L['wcat'], mode="nt", name="matmul_dh")
        dx, dxb, dg1 = _norm_bwd(L['x0'], L['g1'], dh, res=dx1)
        acc['mix_norm'][i] = dg1.reshape(-1)

    _, _, dmg = _norm_bwd(mem, mem_g, d_mem_n)
    G['mem_norm'] = dmg.reshape(-1)
    for k, v in acc.items():
        G[k] = jnp.stack(v)
    for k, v in accc.items():
        Gc[k] = jnp.stack(v, axis=1)
    return loss, dx, Gc, G


def kernel(x, mem, positions, mem_norm, mix_norm, xa_w_kv, xa_q_norm, xa_k_norm, ffn_norm, ffn_w_up, ffn_conv_w, ffn_conv_b, ffn_w_down, a_w_in, a_w_gate2, a_b_gate, a_o_norm, a_w_out, b_w_in, b_q_norm, b_k_norm, b_w_out, c_w_in, c_conv_w, c_conv_b, c_dt_bias, c_a_log, c_d, c_norm, c_w_out, d_w_in, d_lower_bounds, d_o_norm, d_w_out, loss_target, m_mem_norm, m_mix_norm, m_xa_w_kv, m_xa_q_norm, m_xa_k_norm, m_ffn_norm, m_ffn_w_up, m_ffn_conv_w, m_ffn_conv_b, m_ffn_w_down, m_a_w_in, m_a_w_gate2, m_a_b_gate, m_a_o_norm, m_a_w_out, m_b_w_in, m_b_q_norm, m_b_k_norm, m_b_w_out, m_c_w_in, m_c_conv_w, m_c_conv_b, m_c_dt_bias, m_c_a_log, m_c_d, m_c_norm, m_c_w_out, m_d_w_in, m_d_lower_bounds, m_d_o_norm, m_d_w_out, v_mem_norm, v_mix_norm, v_xa_w_kv, v_xa_q_norm, v_xa_k_norm, v_ffn_norm, v_ffn_w_up, v_ffn_conv_w, v_ffn_conv_b, v_ffn_w_down, v_a_w_in, v_a_w_gate2, v_a_b_gate, v_a_o_norm, v_a_w_out, v_b_w_in, v_b_q_norm, v_b_k_norm, v_b_w_out, v_c_w_in, v_c_conv_w, v_c_conv_b, v_c_dt_bias, v_c_a_log, v_c_d, v_c_norm, v_c_w_out, v_d_w_in, v_d_lower_bounds, v_d_o_norm, v_d_w_out):
    args = locals()
    w = {n: args[n] for n in WEIGHTS}
    m = {n: args['m_' + n] for n in WEIGHTS}
    v = {n: args['v_' + n] for n in WEIGHTS}

    big = [n for n in SHARDED if w[n].size >= 65536]
    small = [n for n in SHARDED if n not in big]
    gathered = _gather_many([w[n].astype(BF) for n in big] + [w[n] for n in small], "gather_weights")
    gw = dict(zip(big + small, gathered))

    loss, grad_x, Gc, G = _device_step(x[0], mem[0], positions[0], gw, {n: w[n] for n in REPLICATED}, loss_target[0])
    loss = lax.psum(loss, ("x", "y", "c"))

    parts = dict(zip(big + small, _exchange_many([Gc[n] for n in big + small], "exchange_grads")))
    rep_parts = _all_gather(_pack([G[n] for n in REPLICATED], F32), "gather_replicated_grads")

    out = {}

    def put(names, res, shapes):
        for kind, r in zip(("grad", "delta", "new_m", "new_v"), res):
            for n, t in zip(names, _unpack(r, shapes)):
                out[kind + "_" + n] = t

    for n in big:
        shp = tuple(w[n].shape)
        two_d = (math.prod(shp[:-1]), shp[-1])
        res = _adamw(parts[n].reshape((N_DEV,) + two_d), w[n].reshape(two_d), m[n].reshape(two_d), v[n].reshape(two_d),
                     "adamw")
        for kind, r in zip(("grad", "delta", "new_m", "new_v"), res):
            out[kind + "_" + n] = r.reshape(shp)
    for names, prt, tag in ((small, _pack_lead([parts[n] for n in small], F32), "adamw_small"),
                            (REPLICATED, rep_parts, "adamw_replicated")):
        res = _adamw(prt, _pack([w[n] for n in names], F32), _pack([m[n] for n in names], F32),
                     _pack([v[n] for n in names], F32), tag)
        put(names, res, [tuple(w[n].shape) for n in names])
    return (loss, grad_x[None], *[out["grad_" + n] for n in WEIGHTS], *[out["delta_" + n] for n in WEIGHTS],
            *[out["new_m_" + n] for n in WEIGHTS], *[out["new_v_" + n] for n in WEIGHTS])
```

```python
import functools
import math

import jax
import jax.numpy as jnp
from jax import lax
from jax.experimental import pallas as pl
from jax.experimental.pallas import tpu as pltpu

F32 = jnp.float32
BF = jnp.bfloat16
_MM_DTYPE = BF

N_DEV = 8
EPS = 1e-6
ROPE_THETA = 10000.0
CHUNK = 64
D_MIX = 768
XA_HEADS, XA_HD, D_XA = 4, 64, 256
GLA_HEADS, GLA_DK, GLA_DV, GLA_RANK, GLA_GATE_NORM = 4, 96, 192, 16, 16.0
DIL_GROUPS = ((128, 1), (512, 4), (2048, 16))
DIL_HEADS, DIL_HD, DIL_BLOCK, D_DIL = 4, 128, 128, 512
SSM_HD, SSM_HEADS, SSM_GROUPS, SSM_STATE, SSM_CONV = 64, 12, 2, 128, 4
HGRN_HEADS, HGRN_DK, HGRN_DV = 6, 128, 128
D_FF = 2816
FFN_CONV = 3
DEPTH = 4
ADAM_LR, ADAM_B1, ADAM_B2, ADAM_EPS, ADAM_WD, ADAM_STEP = 0.001, 0.9, 0.999, 1e-08, 0.01, 10
NEG = -1e30
HALO = 8
VMEM_LIMIT = 56 << 20
ADAM_BLOCK = 1 << 18

WEIGHTS = ['mem_norm', 'mix_norm', 'xa_w_kv', 'xa_q_norm', 'xa_k_norm', 'ffn_norm', 'ffn_w_up', 'ffn_conv_w',
           'ffn_conv_b', 'ffn_w_down', 'a_w_in', 'a_w_gate2', 'a_b_gate', 'a_o_norm', 'a_w_out', 'b_w_in', 'b_q_norm',
           'b_k_norm', 'b_w_out', 'c_w_in', 'c_conv_w', 'c_conv_b', 'c_dt_bias', 'c_a_log', 'c_d', 'c_norm', 'c_w_out',
           'd_w_in', 'd_lower_bounds', 'd_o_norm', 'd_w_out']
SHARD_AXIS = {'xa_w_kv': 1, 'ffn_w_up': 2, 'ffn_conv_w': 2, 'ffn_w_down': 1, 'a_w_in': 1, 'a_w_gate2': 1, 'a_w_out': 0,
              'b_w_in': 1, 'b_w_out': 1, 'c_w_in': 0, 'c_conv_w': 1, 'c_w_out': 0, 'd_w_in': 1, 'd_w_out': 0}
SHARDED = [n for n in WEIGHTS if n in SHARD_AXIS]
REPLICATED = [n for n in WEIGHTS if n not in SHARD_AXIS]


def _dot(a, b, ca, cb):
    return lax.dot_general(a.astype(_MM_DTYPE), b.astype(_MM_DTYPE), (((ca,), (cb,)), ((), ())),
                           preferred_element_type=F32)


@jax.custom_vjp
def mm_nn(a, b):
    return _dot(a, b, 1, 0)


mm_nn.defvjp(lambda a, b: (_dot(a, b, 1, 0), (a, b)),
             lambda r, g: (_dot(g, r[1], 1, 1), _dot(r[0], g, 0, 0)))


@jax.custom_vjp
def mm_nt(a, b):
    return _dot(a, b, 1, 1)


mm_nt.defvjp(lambda a, b: (_dot(a, b, 1, 1), (a, b)),
             lambda r, g: (_dot(g, r[1], 1, 0), _dot(g, r[0], 0, 0)))


@jax.custom_vjp
def mm_tn(a, b):
    return _dot(a, b, 0, 0)


mm_tn.defvjp(lambda a, b: (_dot(a, b, 0, 0), (a, b)),
             lambda r, g: (_dot(r[1], g, 1, 1), _dot(r[0], g, 1, 0)))


def _dot_hi(a, b, ca, cb):
    return lax.dot_general(a, b, (((ca,), (cb,)), ((), ())), precision=lax.Precision.HIGHEST,
                           preferred_element_type=F32)


def _tril(c):
    return lax.broadcasted_iota(jnp.int32, (c, c), 0) >= lax.broadcasted_iota(jnp.int32, (c, c), 1)


@jax.custom_vjp
def cumsum_rows(x):
    return _dot_hi(_tril(x.shape[0]).astype(F32), x, 1, 0)


cumsum_rows.defvjp(lambda x: (cumsum_rows(x), None),
                   lambda r, g: (_dot_hi(_tril(g.shape[0]).astype(F32), g, 0, 0),))


@jax.custom_vjp
def cumsum_rows_t(x):
    return _dot_hi(x, _tril(x.shape[0]).astype(F32), 0, 1)


cumsum_rows_t.defvjp(lambda x: (cumsum_rows_t(x), None),
                     lambda r, g: (_dot_hi(_tril(g.shape[1]).astype(F32), g, 0, 1),))


def _split(x, sizes):
    sizes = tuple(int(s) for s in sizes)
    assert sum(sizes) == x.shape[-1], (sizes, x.shape)

    @jax.custom_vjp
    def sp(x):
        out, o = [], 0
        for s in sizes:
            out.append(x[:, o:o + s])
            o += s
        return tuple(out)

    sp.defvjp(lambda x: (sp(x), None), lambda r, g: (jnp.concatenate(list(g), axis=1),))
    return sp(x)


def _row(x, r):
    m = lax.broadcasted_iota(jnp.int32, x.shape, 0) == r
    return jnp.sum(jnp.where(m, x, 0.0), axis=0, keepdims=True)


@jax.custom_vjp
def _roll_half(x):
    return pltpu.roll(x, 64, 1)


_roll_half.defvjp(lambda x: (pltpu.roll(x, 64, 1), None), lambda r, g: (pltpu.roll(g, 64, 1),))


def _shift(xp, x, d):
    if d == 0:
        return x
    n, m = x.shape[0], xp.shape[0]
    assert d <= m == HALO

    @jax.custom_vjp
    def sh(xp, x):
        r = pltpu.roll(x, d, 0)
        row = lax.broadcasted_iota(jnp.int32, xp.shape, 0)
        head = jnp.where(row < d, pltpu.roll(xp, d, 0), r[:m])
        return jnp.concatenate([head, r[m:]], axis=0)

    def bwd(_, g):
        row = lax.broadcasted_iota(jnp.int32, g.shape, 0)
        rowp = lax.broadcasted_iota(jnp.int32, (m,) + g.shape[1:], 0)
        dxp = jnp.where(rowp >= m - d, pltpu.roll(g[:m], m - d, 0), 0.0)
        return dxp, jnp.where(row < n - d, pltpu.roll(g, n - d, 0), 0.0)

    sh.defvjp(lambda xp, x: (sh(xp, x), None), bwd)
    return sh(xp, x)


def _rms(x, g):
    return x * lax.rsqrt(jnp.mean(x * x, axis=-1, keepdims=True) + EPS) * g


def _lane_pair(a, b, width=128):
    shape = a.shape[:-1] + (width,)
    lane = lax.broadcasted_iota(jnp.int32, shape, len(shape) - 1)
    return jnp.where(lane < width // 2, a, b)


def _row_spec(a, w=None, c=None, prev=False, diff=True, dn=None, lb=(), li=None):
    return dict(a=a, w=a.shape[-1] if w is None else w, c=(lambda jc: 0) if c is None else c, prev=prev, diff=diff,
                dn=a.shape[-1] if dn is None else dn, lb=tuple(lb), li=(lambda jc: ()) if li is None else li)


def _par_spec(a, bs=None, idx=None, diff=True):
    nd = a.ndim
    return dict(a=a, bs=tuple(a.shape) if bs is None else tuple(bs),
                idx=(lambda jc: (0,) * nd) if idx is None else idx, diff=diff)


def _out_spec(n, w=None, c=None, dt=F32, ls=(), lb=(), li=None):
    return dict(n=n, w=n if w is None else w, c=(lambda jc: 0) if c is None else c, dt=dt, ls=tuple(ls), lb=tuple(lb),
                li=(lambda jc: ()) if li is None else li)


def _cparams():
    return pltpu.CompilerParams(dimension_semantics=("arbitrary", "arbitrary"), vmem_limit_bytes=VMEM_LIMIT)


def _bspec(s, R, rowfn):
    return pl.BlockSpec(s['lb'] + (R, s['w']),
                        functools.partial(lambda jc, i, s: tuple(s['li'](jc)) + (rowfn(i), s['c'](jc)), s=s))


def _prev_rows(s, R):
    return R if s['prev'] == 'block' else HALO


def _pspec(s, R, blockfn):
    pr = _prev_rows(s, R)
    return pl.BlockSpec(s['lb'] + (pr, s['w']), functools.partial(
        lambda jc, i, s: tuple(s['li'](jc)) + (jnp.maximum(blockfn(i) * (R // pr) - 1, 0), s['c'](jc)), s=s))


def _seq_fwd(name, f, R, rows, params, state_shapes, outs, *, ncol=1, period=None, save_states=False):
    nrows = rows[0]['a'].shape[-2]
    nb = nrows // R
    assert nb * R == nrows
    period = nb if period is None else period
    prev_ids = [k for k, r in enumerate(rows) if r['prev']]
    n_rows, n_prev, n_par, n_out, n_st = len(rows), len(prev_ids), len(params), len(outs), len(state_shapes)

    def body(*refs):
        o = 0
        cur = refs[o:o + n_rows]; o += n_rows
        prv = refs[o:o + n_prev]; o += n_prev
        par = refs[o:o + n_par]; o += n_par
        out = refs[o:o + n_out]; o += n_out
        sav = refs[o:o + (n_st if save_states else 0)]; o += len(sav)
        st = refs[o:o + n_st]
        i = pl.program_id(1)
        first = (i % period) == 0

        @pl.when(i == 0)
        def _():
            for s in st:
                s[...] = jnp.zeros_like(s)

        xs = [r[...].astype(F32) for r in cur]
        xp = [r[...].astype(F32) for r in prv]
        ps = [r[...] for r in par]
        sts = [s[...] for s in st]
        for sv, s in zip(sav, sts):
            sv[0] = s
        ov, ns = f(first, xp, xs, ps, sts)
        for r, v in zip(out, ov):
            r[...] = v.astype(r.dtype)
        for s, v in zip(st, ns):
            s[...] = v

    in_specs = [_bspec(r, R, lambda i: i) for r in rows]
    in_specs += [_pspec(rows[k], R, lambda i: i) for k in prev_ids]
    in_specs += [pl.BlockSpec(p['bs'], functools.partial(lambda jc, i, idx: idx(jc), idx=p['idx'])) for p in params]
    out_specs = [_bspec(o_, R, lambda i: i) for o_ in outs]
    out_shape = [jax.ShapeDtypeStruct(o_['ls'] + (nrows, o_['n']), o_['dt']) for o_ in outs]
    if save_states:
        for s in state_shapes:
            out_specs.append(pl.BlockSpec((1,) + tuple(s), lambda jc, i, nd=len(s): (i,) + (0,) * nd))
            out_shape.append(jax.ShapeDtypeStruct((nb,) + tuple(s), F32))
    args = [r['a'] for r in rows] + [rows[k]['a'] for k in prev_ids] + [p['a'] for p in params]
    res = pl.pallas_call(
        body, name=name, grid=(ncol, nb), in_specs=in_specs, out_specs=out_specs, out_shape=out_shape,
        scratch_shapes=[pltpu.VMEM(tuple(s), F32) for s in state_shapes], compiler_params=_cparams())(*args)
    return list(res[:n_out]), list(res[n_out:])


def _seq_bwd(name, f, R, rows, params, state_shapes, douts, saved, *, ncol=1, period=None, dx_dt=F32, dx_add=None,
             dx_bf=False, dx_alias=None):
    nrows = rows[0]['a'].shape[-2]
    nb = nrows // R
    period = nb if period is None else period
    prev_ids = [k for k, r in enumerate(rows) if r['prev']]
    drow_ids = [k for k, r in enumerate(rows) if r['diff']]
    dpar_ids = [k for k, p in enumerate(params) if p['diff']]
    for k in prev_ids:
        assert rows[k]['diff']
    dx_add, dx_alias = dict(dx_add or {}), dict(dx_alias or {})
    add_ids, alias_ids = sorted(dx_add), sorted(dx_alias)
    n_rows, n_prev, n_par, n_do, n_st = len(rows), len(prev_ids), len(params), len(douts), len(state_shapes)
    n_dx, n_dp, n_add, n_al = len(drow_ids), len(dpar_ids), len(add_ids), len(alias_ids)

    def body(*refs):
        o = 0
        cur = refs[o:o + n_rows]; o += n_rows
        prv = refs[o:o + n_prev]; o += n_prev
        par = refs[o:o + n_par]; o += n_par
        sav = refs[o:o + n_st]; o += n_st
        dou = refs[o:o + n_do]; o += n_do
        adr = refs[o:o + n_add]; o += n_add
        o += n_al
        dxr = refs[o:o + n_dx]; o += n_dx
        dpr = refs[o:o + n_dp]; o += n_dp
        dxb = refs[o:o + (n_dx if dx_bf else 0)]; o += len(dxb)
        dst = refs[o:o + n_st]; o += n_st
        car = refs[o:o + n_prev]
        j = pl.program_id(1)
        i = nb - 1 - j
        first = (i % period) == 0

        @pl.when(j == 0)
        def _():
            for s in tuple(dst) + tuple(car) + tuple(dpr):
                s[...] = jnp.zeros_like(s)

        xs = [r[...].astype(F32) for r in cur]
        xp = [r[...].astype(F32) for r in prv]
        ps = [r[...] for r in par]
        sts = [s[0] for s in sav]

        def g(dxs, dxp, dps, dsts):
            xs_, ps_ = list(xs), list(ps)
            for k, v in zip(drow_ids, dxs):
                xs_[k] = v
            for k, v in zip(dpar_ids, dps):
                ps_[k] = v
            ov, ns = f(first, list(dxp), xs_, ps_, list(dsts))
            return tuple(ov), tuple(ns)

        _, vjp = jax.vjp(g, tuple(xs[k] for k in drow_ids), tuple(xp), tuple(ps[k] for k in dpar_ids), tuple(sts))
        dxs, dxp, dps, dsts = vjp((tuple(r[...].astype(F32) for r in dou), tuple(s[...] for s in dst)))
        dxs = list(dxs)
        for n_, pos in enumerate(add_ids):
            dxs[pos] = dxs[pos] + adr[n_][...].astype(F32)
        tails = {}
        for n_, k in enumerate(prev_ids):
            pos = drow_ids.index(k)
            if rows[k]['prev'] == 'block':
                dxs[pos] = dxs[pos] + car[n_][...]
            else:
                tails[pos] = car[n_][...]
            car[n_][...] = dxp[n_]
        for pos, v in enumerate(dxs):
            outs_ = [dxr[pos]] + ([dxb[pos]] if dx_bf else [])
            if pos in tails:
                v = jnp.concatenate([v[..., :R - HALO, :], v[..., R - HALO:, :] + tails[pos]], axis=-2)
            for r in outs_:
                r[...] = v.astype(r.dtype)
        for r, v in zip(dpr, dps):
            r[...] += v
        for s, v in zip(dst, dsts):
            s[...] = v

    def rev(j):
        return nb - 1 - j

    def dspec(k):
        return _bspec(rows[k], R, rev)

    in_specs = [_bspec(r, R, rev) for r in rows]
    in_specs += [_pspec(rows[k], R, rev) for k in prev_ids]
    in_specs += [pl.BlockSpec(p['bs'], functools.partial(lambda jc, j, idx: idx(jc), idx=p['idx'])) for p in params]
    in_specs += [pl.BlockSpec((1,) + tuple(s), lambda jc, j, nd=len(s): (nb - 1 - j,) + (0,) * nd) for s in state_shapes]
    in_specs += [_bspec(d, R, rev) for d in douts]
    in_specs += [dspec(drow_ids[pos]) for pos in add_ids]
    in_specs += [pl.BlockSpec(memory_space=pl.ANY) for _ in alias_ids]
    out_specs = [dspec(k) for k in drow_ids]
    out_shape = [jax.ShapeDtypeStruct(tuple(rows[k]['a'].shape[:-1]) + (rows[k]['dn'],), dx_dt) for k in drow_ids]
    for k in dpar_ids:
        p = params[k]
        out_specs.append(pl.BlockSpec(p['bs'], functools.partial(lambda jc, j, idx: idx(jc), idx=p['idx'])))
        out_shape.append(jax.ShapeDtypeStruct(p['a'].shape, F32))
    if dx_bf:
        out_specs += [dspec(k) for k in drow_ids]
        out_shape += [jax.ShapeDtypeStruct(tuple(rows[k]['a'].shape[:-1]) + (rows[k]['dn'],), BF) for k in drow_ids]
    scratch = [pltpu.VMEM(tuple(s), F32) for s in state_shapes]
    scratch += [pltpu.VMEM(tuple(d for d in rows[k]['lb'] if d is not None) + (_prev_rows(rows[k], R), rows[k]['w']), F32)
                for k in prev_ids]
    args = ([r['a'] for r in rows] + [rows[k]['a'] for k in prev_ids] + [p['a'] for p in params] + list(saved)
            + [d['a'] for d in douts] + [dx_add[pos] for pos in add_ids] + [dx_alias[pos] for pos in alias_ids])
    n_in = len(args)
    aliases = {n_in - n_al + n_: pos for n_, pos in enumerate(alias_ids)}
    for pos in alias_ids:
        assert dx_alias[pos].shape == out_shape[pos].shape and dx_alias[pos].dtype == out_shape[pos].dtype
    res = pl.pallas_call(
        body, name=name, grid=(ncol, nb), in_specs=in_specs, out_specs=out_specs, out_shape=out_shape,
        scratch_shapes=scratch, input_output_aliases=aliases, compiler_params=_cparams())(*args)
    if dx_bf:
        return list(res[:n_dx]), list(res[n_dx:n_dx + n_dp]), list(res[n_dx + n_dp:])
    return list(res[:n_dx]), list(res[n_dx:])


def _tile(n, cands):
    for c in cands:
        if n % c == 0:
            return c
    return n


def _mm_call(name, grid, a, a_spec, b, b_spec, contract, out_shape, out_spec, acc_shape, add=None, add_spec=None,
             exchange=None):
    nk = grid[2]
    ca, cb = contract
    has_add = add is not None
    ex = list(exchange or [])
    n_ex = len(ex)
    n_in = 2 + has_add

    def body(*refs):
        a_ref, b_ref = refs[0], refs[1]
        add_ref = refs[2] if has_add else None
        o_ref = refs[n_in + n_ex]
        scr = refs[n_in + 2 * n_ex + 1:]
        step = [pl.program_id(d) for d in range(3)]
        if n_ex:
            g_refs, r_refs, sems = refs[n_in:n_in + n_ex], refs[n_in + n_ex + 1:n_in + 2 * n_ex + 1], scr[-3:]

            @pl.when((step[0] == 0) & (step[1] == 0) & (step[2] == 0))
            def _():
                _exchange_start(g_refs, r_refs, *sems)

        part = _dot(a_ref[...], b_ref[...], ca, cb)

        def finish(r):
            if has_add:
                r = r + add_ref[...].astype(F32)
            o_ref[...] = r.astype(o_ref.dtype)

        if nk == 1:
            finish(part)
        else:
            acc = scr[0]

            @pl.when(step[2] == 0)
            def _():
                acc[...] = part

            @pl.when(step[2] > 0)
            def _():
                acc[...] += part

            @pl.when(step[2] == nk - 1)
            def _():
                finish(acc[...])

        if n_ex:
            @pl.when((step[0] == grid[0] - 1) & (step[1] == grid[1] - 1) & (step[2] == grid[2] - 1))
            def _():
                _exchange_wait(g_refs, r_refs, *sems)

    in_specs, args = [a_spec, b_spec], [a, b]
    if has_add:
        in_specs.append(add_spec)
        args.append(add)
    any_spec = pl.BlockSpec(memory_space=pl.ANY)
    scratch = [] if nk == 1 else [pltpu.VMEM(acc_shape, F32)]
    if n_ex:
        scratch += _exchange_sems(n_ex)
    res = pl.pallas_call(
        body, name=name, grid=grid, in_specs=in_specs + [any_spec] * n_ex, out_specs=[out_spec] + [any_spec] * n_ex,
        out_shape=[out_shape] + [jax.ShapeDtypeStruct(g.shape, g.dtype) for g in ex], scratch_shapes=scratch,
        compiler_params=pltpu.CompilerParams(
            dimension_semantics=("arbitrary",) * 3 if n_ex else ("parallel", "parallel", "arbitrary"),
            vmem_limit_bytes=VMEM_LIMIT, has_side_effects=bool(n_ex)))(*args, *ex)
    return (res[0], list(res[1:])) if n_ex else res[0]


def _matmul(a, b, mode="nn", add=None, out_dtype=F32, name="matmul"):
    if mode == "nn":
        (M, K), N = a.shape, b.shape[1]
    elif mode == "nt":
        (M, K), N = a.shape, b.shape[0]
    else:
        (K, M), N = a.shape, b.shape[1]
    tm = _tile(M, (1024, 512, 256, 128, 64, 32, 16, 8))
    tn = _tile(N, (512, 256, 128))
    tk = K if K <= 5120 else _tile(K, (2048, 1024, 512, 256, 128))
    if mode == "tn":
        a_spec = pl.BlockSpec((tk, tm), lambda i, j, k: (k, i))
    else:
        a_spec = pl.BlockSpec((tm, tk), lambda i, j, k: (i, k))
    if mode == "nt":
        b_spec = pl.BlockSpec((tn, tk), lambda i, j, k: (j, k))
    else:
        b_spec = pl.BlockSpec((tk, tn), lambda i, j, k: (k, j))
    blk = pl.BlockSpec((tm, tn), lambda i, j, k: (i, j))
    return _mm_call(name, (M // tm, N // tn, K // tk), a, a_spec, b, b_spec,
                    {"nn": (1, 0), "nt": (1, 1), "tn": (0, 0)}[mode], jax.ShapeDtypeStruct((M, N), out_dtype), blk,
                    (tm, tn), add, blk)


FF_SH = 2 * D_FF // N_DEV


def _ffn_up(h2, wup):
    S, D = h2.shape
    tm = _tile(S, (1024, 512, 256, 128))
    return _mm_call("matmul_up", (S // tm, N_DEV, 1), h2, pl.BlockSpec((tm, D), lambda m, j, k: (m, 0)),
                    wup, pl.BlockSpec((None, D, FF_SH), lambda m, j, k: (j, 0, 0)), (1, 0),
                    jax.ShapeDtypeStruct((2, N_DEV // 2, S, FF_SH), F32),
                    pl.BlockSpec((None, None, tm, FF_SH), lambda m, j, k: (j // 4, j % 4, m, 0)), (tm, FF_SH))


def _ffn_down(act, wd, x1):
    _, S, _ = act.shape
    D = wd.shape[1]
    tm, tn = _tile(S, (1024, 512, 256, 128)), _tile(D, (512, 256, 128))
    blk = pl.BlockSpec((tm, tn), lambda m, n, k: (m, n))
    return _mm_call("matmul_down", (S // tm, D // tn, N_DEV // 2), act,
                    pl.BlockSpec((None, tm, FF_SH), lambda m, n, k: (k, m, 0)), wd,
                    pl.BlockSpec((FF_SH, tn), lambda m, n, k: (k, n)), (1, 0), jax.ShapeDtypeStruct((S, D), F32), blk,
                    (tm, tn), x1, blk)


def _ffn_dact(dxb, wd):
    S, D = dxb.shape
    tm = _tile(S, (1024, 512, 256, 128))
    return _mm_call("matmul_dact", (S // tm, N_DEV // 2, 1), dxb, pl.BlockSpec((tm, D), lambda m, j, k: (m, 0)), wd,
                    pl.BlockSpec((FF_SH, D), lambda m, j, k: (j, 0)), (1, 1),
                    jax.ShapeDtypeStruct((N_DEV // 2, S, FF_SH), F32),
                    pl.BlockSpec((None, tm, FF_SH), lambda m, j, k: (j, m, 0)), (tm, FF_SH))


def _ffn_dw_down(act, dxb):
    _, S, _ = act.shape
    D = dxb.shape[1]
    tk, tn = _tile(S, (2048, 1024, 512, 256, 128)), _tile(D, (512, 256, 128))
    return _mm_call("matmul_dw_down", (N_DEV // 2, D // tn, S // tk), act,
                    pl.BlockSpec((None, tk, FF_SH), lambda j, n, k: (j, k, 0)), dxb,
                    pl.BlockSpec((tk, tn), lambda j, n, k: (k, n)), (0, 0), jax.ShapeDtypeStruct((D_FF, D), BF),
                    pl.BlockSpec((FF_SH, tn), lambda j, n, k: (j, n)), (FF_SH, tn))


def _ffn_dw_up(h2, du, exchange=None):
    S, D = h2.shape
    tk = _tile(S, (1024, 512, 256, 128))
    return _mm_call("matmul_dw_up", (N_DEV, 1, S // tk), h2, pl.BlockSpec((tk, D), lambda j, n, k: (k, 0)), du,
                    pl.BlockSpec((None, None, tk, FF_SH), lambda j, n, k: (j // 4, j % 4, k, 0)), (0, 0),
                    jax.ShapeDtypeStruct((N_DEV, D, FF_SH), BF),
                    pl.BlockSpec((None, D, FF_SH), lambda j, n, k: (j, 0, 0)), (D, FF_SH), exchange=exchange)


def _ffn_dh2(du, wup, exchange=None):
    S = du.shape[2]
    D = wup.shape[1]
    tm = _tile(S, (1024, 512, 256, 128))
    return _mm_call("matmul_dh2", (S // tm, 1, N_DEV), du,
                    pl.BlockSpec((None, None, tm, FF_SH), lambda m, n, k: (k // 4, k % 4, m, 0)), wup,
                    pl.BlockSpec((None, D, FF_SH), lambda m, n, k: (k, 0, 0)), (1, 1),
                    jax.ShapeDtypeStruct((S, D), F32), pl.BlockSpec((tm, D), lambda m, n, k: (m, 0)), (tm, D),
                    exchange=exchange)


def f_rmsnorm(first, xp, xs, ps, sts):
    return (_rms(xs[0], ps[0]),), ()


def f_xattn(first, xp, xs, ps, sts):
    (xq,), (kv, qn, kn) = xs, ps
    qs = _split(xq, [XA_HD] * XA_HEADS)
    kvs = _split(kv, [XA_HD] * (2 * XA_HEADS))
    outs = []
    for h in range(XA_HEADS):
        q = _rms(qs[h], qn)
        k = _rms(kvs[h], kn)
        v = kvs[XA_HEADS + h]
        s = mm_nt(q, k) * (XA_HD ** -0.5)
        m = lax.stop_gradient(jnp.max(s, axis=-1, keepdims=True))
        p = jnp.exp(s - m)
        outs.append(mm_nn(p / jnp.sum(p, axis=-1, keepdims=True), v))
    return (jnp.concatenate(outs, axis=1),), ()


def _conv(xp, x, w, b, first, taps):
    xp = jnp.where(first, 0.0, xp)
    y = b + w[taps - 1:taps] * x
    for d in range(1, taps):
        y = y + w[taps - 1 - d:taps - d] * _shift(xp, x, d)
    return y


def _unstack2(x):
    @jax.custom_vjp
    def us(x):
        return x[0], x[1]

    us.defvjp(lambda x: (us(x), None), lambda r, g: (jnp.stack(g),))
    return us(x)


def f_ffn_act(first, xp, xs, ps, sts):
    (up,), (u,), (wg, wv, bg, bv) = xp, xs, ps
    (ugp, uvp), (ug, uv) = _unstack2(up), _unstack2(u)
    gate = _conv(ugp, ug, wg, bg, first, FFN_CONV)
    val = _conv(uvp, uv, wv, bv, first, FFN_CONV)
    return (jax.nn.silu(gate) * val,), ()


def _gla_chunk(q, k, v, la, st):
    c = q.shape[0]
    b = cumsum_rows(la)
    b_last = _row(b, c - 1)
    b_ref = _row(b, c // 2 - 1)
    att = mm_nt(q * jnp.exp(b - b_ref), k * jnp.exp(b_ref - b))
    att = jnp.where(_tril(c), att, 0.0)
    o = mm_nn(att, v) + mm_nt(q * jnp.exp(b), st)
    st_new = st * jnp.exp(b_last) + mm_tn(v, k * jnp.exp(b_last - b))
    return o, st_new


def _a_cols(ntot):
    used = D_XA + 2 * GLA_HEADS * GLA_DK + D_MIX + GLA_RANK + D_MIX
    return [D_XA, GLA_HEADS * GLA_DK, GLA_HEADS * GLA_DK, D_MIX, GLA_RANK, D_MIX] + ([ntot - used] if ntot > used else [])


def f_gla(first, xp, xs, ps, sts):
    (p,), (wg2, bg, on) = xs, ps
    parts = _split(p, _a_cols(p.shape[1]))
    q, k, v, glr, og = parts[1:6]
    la = jax.nn.log_sigmoid(mm_nn(glr, wg2) + bg) / GLA_GATE_NORM
    qs = _split(q * (GLA_DK ** -0.5), [GLA_DK] * GLA_HEADS)
    ks = _split(k, [GLA_DK] * GLA_HEADS)
    vs = _split(v, [GLA_DV] * GLA_HEADS)
    las = _split(la, [GLA_DK] * GLA_HEADS)
    outs, new = [], []
    for h in range(GLA_HEADS):
        o, s = _gla_chunk(qs[h], ks[h], vs[h], las[h], sts[h])
        outs.append(_rms(o, on))
        new.append(s)
    return (jnp.concatenate(outs, axis=1) * jax.nn.silu(og),), tuple(new)


def f_hgrn(first, xp, xs, ps, sts):
    (p,), (lbp, on) = xs, ps
    _, q, fgate, iv, og = _split(p, [D_XA, D_MIX, D_MIX, D_MIX, D_MIX])
    e = jnp.exp(lbp - jnp.max(lbp, axis=0, keepdims=True))
    row = lax.broadcasted_iota(jnp.int32, e.shape, 0)
    lb = jnp.sum(jnp.where(row >= 1, e, 0.0), axis=0, keepdims=True) / jnp.sum(e, axis=0, keepdims=True)
    fg = lb + (1.0 - lb) * jax.nn.sigmoid(fgate)
    qs = _split(jax.nn.silu(q), [HGRN_DK] * HGRN_HEADS)
    ks = _split(1.0 - fg, [HGRN_DK] * HGRN_HEADS)
    vs = _split(iv, [HGRN_DV] * HGRN_HEADS)
    las = _split(jnp.log(fg), [HGRN_DK] * HGRN_HEADS)
    outs, new = [], []
    for h in range(HGRN_HEADS):
        o, s = _gla_chunk(qs[h], ks[h], vs[h], las[h], sts[h])
        outs.append(_rms(o, on))
        new.append(s)
    return (jnp.concatenate(outs, axis=1) * jax.nn.sigmoid(og),), tuple(new)


def _c_cols(ntot):
    gn = SSM_GROUPS * SSM_STATE
    used = D_XA + D_MIX + D_MIX + 2 * gn + SSM_HEADS
    return [D_XA, D_MIX, D_MIX + 2 * gn, SSM_HEADS] + ([ntot - used] if ntot > used else [])


def f_ssd(first, xp, xs, ps, sts):
    (pp,), (p,), (cw, cb, dtb, alog, dsk, ng) = xp, xs, ps
    c = p.shape[0]
    gn = SSM_GROUPS * SSM_STATE
    hg = SSM_HEADS // SSM_GROUPS
    _, z, xbc, dtr = _split(p, _c_cols(p.shape[1]))[:4]
    xbc_p = _split(pp, _c_cols(p.shape[1]))[2]
    xbc = jax.nn.silu(_conv(xbc_p, xbc, cw, cb, first, SSM_CONV))
    xs_, bm, cm = _split(xbc, [D_MIX, gn, gn])
    dt = jax.nn.softplus(dtr + dtb)
    a = dt * (-jnp.exp(alog))
    acs = cumsum_rows(a)
    acs_t = cumsum_rows_t(a)
    acs_last = _row(acs, c - 1)
    dt_h = _split(dt, [1] * SSM_HEADS)
    acs_h = _split(acs, [1] * SSM_HEADS)
    al_h = _split(acs_last, [1] * SSM_HEADS)
    d_h = _split(dsk, [1] * SSM_HEADS)
    x2s = _split(xs_, [2 * SSM_HD] * (SSM_HEADS // 2))
    bms = _split(bm, [SSM_STATE] * SSM_GROUPS)
    cms = _split(cm, [SSM_STATE] * SSM_GROUPS)
    tril = _tril(c)
    cbs = [mm_nt(cms[g], bms[g]) for g in range(SSM_GROUPS)]
    ys, new = [], []
    for j in range(SSM_HEADS // 2):
        g = (2 * j) // hg
        h0, h1 = 2 * j, 2 * j + 1
        xdt = x2s[j] * _lane_pair(dt_h[h0], dt_h[h1])
        acs2 = _lane_pair(acs_h[h0], acs_h[h1])
        al2 = _lane_pair(al_h[h0], al_h[h1])
        yd = []
        for h in (h0, h1):
            seg = acs_h[h] - _row(acs_t, h)
            lm = jnp.exp(jnp.where(tril, seg, NEG))
            yd.append(mm_nn(cbs[g] * lm, xdt))
        lane = lax.broadcasted_iota(jnp.int32, xdt.shape, 1)
        y_diag = jnp.where(lane < SSM_HD, yd[0], yd[1])
        y_off = mm_nn(cms[g], sts[j]) * jnp.exp(acs2)
        x_end = xdt * jnp.exp(al2 - acs2)
        new.append(sts[j] * jnp.exp(al2) + mm_tn(bms[g], x_end))
        ys.append(y_diag + y_off + _lane_pair(d_h[h0], d_h[h1]) * x2s[j])
    y = jnp.concatenate(ys, axis=1) * jax.nn.silu(z)
    gw = D_MIX // SSM_GROUPS
    yg = _split(y, [gw] * SSM_GROUPS)
    ngs = _split(ng, [gw] * SSM_GROUPS)
    y = jnp.concatenate([_rms(yg[g], ngs[g]) for g in range(SSM_GROUPS)], axis=1)
    return (y,), tuple(new)


def f_dil_prep(first, xp, xs, ps, sts):
    (p, pos), (qn, kn, invf, sign) = xs, ps
    nh = len(DIL_GROUPS) * DIL_HEADS
    _, q, k, v = _split(p, [D_XA] + [nh * DIL_HD] * 3)
    ang = pos * invf
    cos, sin = jnp.cos(ang), jnp.sin(ang) * sign

    def rope(t, g):
        hs = _split(t, [DIL_HD] * nh)
        out = []
        for h in hs:
            n = _rms(h, g)
            out.append(n * cos + _roll_half(n) * sin)
        return jnp.concatenate(out, axis=1)

    return (rope(q, qn), rope(k, kn), v), ()


def f_dil_attn(first, xp, xs, ps, sts):
    (kp, vp), (q, k, v) = xp, xs
    Q = DIL_BLOCK
    qs, ks, vs = (_split(t, [DIL_HD] * DIL_HEADS) for t in (q, k, v))
    kps, vps = (_split(t, [DIL_HD] * DIL_HEADS) for t in (kp, vp))
    i = lax.broadcasted_iota(jnp.int32, (Q, 2 * Q), 0)
    j = lax.broadcasted_iota(jnp.int32, (Q, 2 * Q), 1)
    dist = Q + i - j
    mask = (dist >= 0) & (dist <= Q) & (jnp.logical_not(first) | (j >= Q))
    outs, lses = [], []
    for h in range(DIL_HEADS):
        k2 = jnp.concatenate([kps[h], ks[h]], axis=0)
        v2 = jnp.concatenate([vps[h], vs[h]], axis=0)
        s = jnp.where(mask, mm_nt(qs[h], k2) * (DIL_HD ** -0.5), NEG)
        m = lax.stop_gradient(jnp.max(s, axis=-1, keepdims=True))
        p = jnp.exp(s - m)
        l = jnp.sum(p, axis=-1, keepdims=True)
        outs.append(mm_nn(p / l, v2))
        lses.append(jnp.broadcast_to(m + jnp.log(l), (Q, DIL_HD)))
    return (jnp.concatenate(outs, axis=1), jnp.concatenate(lses, axis=1)), ()


def f_dil_merge(first, xp, xs, ps, sts):
    o0, o1, o2, l0, l1, l2 = xs
    m = jnp.maximum(jnp.maximum(l0, l1), l2)
    e0, e1, e2 = jnp.exp(l0 - m), jnp.exp(l1 - m), jnp.exp(l2 - m)
    den = e0 + e1 + e2
    return ((e0 * o0 + e1 * o1 + e2 * o2) / den,), ()


def _loss_head(y, target):
    S, D = y.shape
    R = _tile(S, (512, 256, 128, 64, 32, 16, 8))

    def body(y_ref, t_ref, dy_ref, dyb_ref, l_ref):
        e = y_ref[...] - t_ref[...]
        dy_ref[...] = e * (1.0 / D)
        dyb_ref[...] = (e * (1.0 / D)).astype(BF)

        @pl.when(pl.program_id(0) == 0)
        def _():
            l_ref[...] = jnp.zeros_like(l_ref)

        l_ref[...] += jnp.broadcast_to(0.5 * jnp.sum(jnp.mean(e * e, axis=-1, keepdims=True), axis=0, keepdims=True),
                                       l_ref.shape)

    blk = pl.BlockSpec((R, D), lambda i: (i, 0))
    dy, dyb, l = pl.pallas_call(
        body, name="loss_head", grid=(S // R,), in_specs=[blk, blk],
        out_specs=[blk, blk, pl.BlockSpec((8, 128), lambda i: (0, 0))],
        out_shape=[jax.ShapeDtypeStruct((S, D), F32), jax.ShapeDtypeStruct((S, D), BF),
                   jax.ShapeDtypeStruct((8, 128), F32)],
        compiler_params=pltpu.CompilerParams(dimension_semantics=("arbitrary",)))(y, target)
    return dy, dyb, l[0, 0]


def _adamw(parts, w, m, v, name):
    _, n, width = parts.shape
    tr = _tile(n, [t for t in (512, 256, 128, 64, 32, 16, 8) if t * width <= ADAM_BLOCK])

    def body(p_ref, w_ref, m_ref, v_ref, g_ref, d_ref, nm_ref, nv_ref):
        g = p_ref[0].astype(F32)
        for s in range(1, N_DEV):
            g = g + p_ref[s].astype(F32)
        nm = ADAM_B1 * m_ref[...] + (1.0 - ADAM_B1) * g
        nv = ADAM_B2 * v_ref[...] + (1.0 - ADAM_B2) * (g * g)
        m_hat = nm / (1.0 - ADAM_B1 ** ADAM_STEP)
        v_hat = nv / (1.0 - ADAM_B2 ** ADAM_STEP)
        g_ref[...] = g
        d_ref[...] = -ADAM_LR * (m_hat / (jnp.sqrt(v_hat) + ADAM_EPS) + ADAM_WD * w_ref[...])
        nm_ref[...] = nm
        nv_ref[...] = nv

    blk = pl.BlockSpec((tr, width), lambda i: (i, 0))
    return pl.pallas_call(
        body, name=name, grid=(n // tr,),
        in_specs=[pl.BlockSpec((N_DEV, tr, width), lambda i: (0, i, 0)), blk, blk, blk],
        out_specs=[blk] * 4, out_shape=[jax.ShapeDtypeStruct((n, width), F32)] * 4,
        compiler_params=pltpu.CompilerParams(dimension_semantics=("arbitrary",), vmem_limit_bytes=VMEM_LIMIT))(
            parts, w, m, v)


def _peer(k):
    x, y, c = lax.axis_index("x"), lax.axis_index("y"), lax.axis_index("c")
    px = 1 - x if k & 4 else x
    py = 1 - y if k & 2 else y
    pc = 1 - c if k & 1 else c
    return (px, py, pc), 4 * px + 2 * py + pc


def _my_id():
    return 4 * lax.axis_index("x") + 2 * lax.axis_index("y") + lax.axis_index("c")


def _all_gather(x, name):
    def body(x_ref, out_ref, send, recv, loc):
        me = _my_id()
        mine = pltpu.make_async_copy(x_ref, out_ref.at[me], loc)
        mine.start()
        cps = []
        for k in range(1, N_DEV):
            peer, _ = _peer(k)
            cp = pltpu.make_async_remote_copy(src_ref=x_ref, dst_ref=out_ref.at[me], send_sem=send.at[k - 1],
                                              recv_sem=recv.at[k - 1], device_id=peer,
                                              device_id_type=pl.DeviceIdType.MESH)
            cp.start()
            cps.append(cp)
        for k in range(1, N_DEV):
            peer, pid = _peer(k)
            pltpu.make_async_remote_copy(src_ref=x_ref, dst_ref=out_ref.at[pid], send_sem=send.at[k - 1],
                                         recv_sem=recv.at[k - 1], device_id=peer,
                                         device_id_type=pl.DeviceIdType.MESH).wait_recv()
        for cp in cps:
            cp.wait_send()
        mine.wait()

    return pl.pallas_call(
        body, name=name, out_shape=jax.ShapeDtypeStruct((N_DEV,) + x.shape, x.dtype),
        in_specs=[pl.BlockSpec(memory_space=pl.ANY)], out_specs=pl.BlockSpec(memory_space=pl.ANY),
        scratch_shapes=[pltpu.SemaphoreType.DMA((N_DEV - 1,)), pltpu.SemaphoreType.DMA((N_DEV - 1,)),
                        pltpu.SemaphoreType.DMA],
        compiler_params=pltpu.CompilerParams(has_side_effects=True))(x)


def _exchange_sems(n):
    return [pltpu.SemaphoreType.DMA((n * (N_DEV - 1),)), pltpu.SemaphoreType.DMA((n * (N_DEV - 1),)),
            pltpu.SemaphoreType.DMA((n,))]


def _exchange_copies(g_refs, out_refs, send, recv, loc, with_arrivals):
    me = _my_id()
    local = [pltpu.make_async_copy(g.at[me], o.at[me], loc.at[w]) for w, (g, o) in enumerate(zip(g_refs, out_refs))]
    pushes, arrivals = [], []
    for k in range(1, N_DEV):
        peer, pid = _peer(k)
        for w, (g, o) in enumerate(zip(g_refs, out_refs)):
            s = w * (N_DEV - 1) + k - 1
            ends = [(g.at[pid], o.at[me], pushes)] + ([(g.at[me], o.at[pid], arrivals)] if with_arrivals else [])
            for src, dst, into in ends:
                into.append(pltpu.make_async_remote_copy(src_ref=src, dst_ref=dst, send_sem=send.at[s],
                                                         recv_sem=recv.at[s], device_id=peer,
                                                         device_id_type=pl.DeviceIdType.MESH))
    return local, pushes, arrivals


def _exchange_start(g_refs, out_refs, send, recv, loc):
    local, pushes, _ = _exchange_copies(g_refs, out_refs, send, recv, loc, False)
    for cp in local + pushes:
        cp.start()


def _exchange_wait(g_refs, out_refs, send, recv, loc):
    local, pushes, arrivals = _exchange_copies(g_refs, out_refs, send, recv, loc, True)
    for cp in arrivals:
        cp.wait_recv()
    for cp in pushes:
        cp.wait_send()
    for cp in local:
        cp.wait()


def _exchange_many(gs, name):
    n = len(gs)

    def body(*refs):
        g_refs, out_refs, sems = refs[:n], refs[n:2 * n], refs[2 * n:]
        _exchange_start(g_refs, out_refs, *sems)
        _exchange_wait(g_refs, out_refs, *sems)

    return pl.pallas_call(
        body, name=name, out_shape=[jax.ShapeDtypeStruct(g.shape, g.dtype) for g in gs],
        in_specs=[pl.BlockSpec(memory_space=pl.ANY)] * n, out_specs=[pl.BlockSpec(memory_space=pl.ANY)] * n,
        scratch_shapes=_exchange_sems(n), compiler_params=pltpu.CompilerParams(has_side_effects=True))(*gs)


def _gather_many(xs, name):
    n = len(xs)

    def body(*refs):
        x_refs, out_refs, (send, recv, loc) = refs[:n], refs[n:2 * n], refs[2 * n:]
        x, y, c = lax.axis_index("x"), lax.axis_index("y"), lax.axis_index("c")
        me, sibling = (x, y, c), (x, y, 1 - c)
        chips = [(1 - x, y), (x, 1 - y), (1 - x, 1 - y)]

        def slot(p):
            return 4 * p[0] + 2 * p[1] + p[2]

        def copy(w, k, block, to, src=None):
            dst = out_refs[w].at[slot(block)]
            return pltpu.make_async_remote_copy(src_ref=dst if src is None else src, dst_ref=dst,
                                                send_sem=send.at[w * (N_DEV - 1) + k], recv_sem=recv.at[w * (N_DEV - 1) + k],
                                                device_id=to, device_id_type=pl.DeviceIdType.MESH)

        mine = [pltpu.make_async_copy(x_refs[w], out_refs[w].at[slot(me)], loc.at[w]) for w in range(n)]
        for cp in mine:
            cp.start()
        first = []
        for j, chip in enumerate(chips):
            first += [copy(w, 1 + j, me, (*chip, c), src=x_refs[w]) for w in range(n)]
        first += [copy(w, 0, me, sibling, src=x_refs[w]) for w in range(n)]
        for cp in first:
            cp.start()
        passed = []
        for j, chip in enumerate(chips):
            for w in range(n):
                copy(w, 1 + j, (*chip, c), me).wait_recv()
                cp = copy(w, 4 + j, (*chip, c), sibling)
                cp.start()
                passed.append(cp)
        for w in range(n):
            copy(w, 0, sibling, me).wait_recv()
            for j, chip in enumerate(chips):
                copy(w, 4 + j, (*chip, 1 - c), me).wait_recv()
        for cp in first + passed:
            cp.wait_send()
        for cp in mine:
            cp.wait()

    return pl.pallas_call(
        body, name=name, out_shape=[jax.ShapeDtypeStruct((N_DEV,) + x.shape, x.dtype) for x in xs],
        in_specs=[pl.BlockSpec(memory_space=pl.ANY)] * n, out_specs=[pl.BlockSpec(memory_space=pl.ANY)] * n,
        scratch_shapes=[pltpu.SemaphoreType.DMA((n * (N_DEV - 1),)), pltpu.SemaphoreType.DMA((n * (N_DEV - 1),)),
                        pltpu.SemaphoreType.DMA((n,))],
        compiler_params=pltpu.CompilerParams(has_side_effects=True))(*xs)


def _cat_segs(G, ws, n_mix):
    segs = []
    for g in range(G):
        lo, hi = g * ws, (g + 1) * ws
        if lo < n_mix:
            segs.append((g, 0, min(hi, n_mix) - lo, D_XA + lo))
        if hi > n_mix:
            s = max(lo, n_mix)
            segs.append((g, s - lo, hi - s, s - n_mix))
    return segs


def _cat_cols(src, n_mix, ntot):
    G, R, ws = src.shape
    segs = _cat_segs(G, ws, n_mix)
    tr = _tile(R, (256, 128, 64, 32, 16, 8))

    def body(i_ref, o_ref):
        if ntot > G * ws:
            o_ref[...] = jnp.zeros_like(o_ref)
        for g, s, n, d in segs:
            o_ref[:, d:d + n] = i_ref[g][:, s:s + n]

    return pl.pallas_call(
        body, name="cat_cols", grid=(R // tr,), in_specs=[pl.BlockSpec((G, tr, ws), lambda i: (0, i, 0))],
        out_specs=pl.BlockSpec((tr, ntot), lambda i: (i, 0)), out_shape=jax.ShapeDtypeStruct((R, ntot), src.dtype),
        compiler_params=pltpu.CompilerParams(dimension_semantics=("arbitrary",)))(src)


def _uncat_cols(dw, G, ws, n_mix):
    R, ntot = dw.shape
    segs = _cat_segs(G, ws, n_mix)
    tr = _tile(R, (256, 128, 64, 32, 16, 8))

    def body(i_ref, o_ref):
        v = i_ref[...]
        for g, s, n, d in segs:
            o_ref[g, :, s:s + n] = v[:, d:d + n]

    return pl.pallas_call(
        body, name="uncat_cols", grid=(R // tr,), in_specs=[pl.BlockSpec((tr, ntot), lambda i: (i, 0))],
        out_specs=pl.BlockSpec((G, tr, ws), lambda i: (0, i, 0)), out_shape=jax.ShapeDtypeStruct((G, R, ws), dw.dtype),
        compiler_params=pltpu.CompilerParams(dimension_semantics=("arbitrary",)))(dw)


PACK_W = 1024


def _granule(n):
    return (256 if n >= 256 * PACK_W else 8) * PACK_W


def _pack(arrs, dtype):
    flat = jnp.concatenate([a.reshape(-1).astype(dtype) for a in arrs])
    n = flat.shape[0]
    pad = (-n) % _granule(n)
    if pad:
        flat = jnp.concatenate([flat, jnp.zeros((pad,), dtype)])
    return flat.reshape(-1, PACK_W)


def _unpack(packed, shapes):
    flat = packed.reshape(-1)
    out, o = [], 0
    for s in shapes:
        n = math.prod(s)
        out.append(flat[o:o + n].reshape(s))
        o += n
    return out


def _pack_lead(arrs, dtype):
    flat = jnp.concatenate([a.reshape(N_DEV, -1).astype(dtype) for a in arrs], axis=1)
    n = flat.shape[1]
    pad = (-n) % _granule(n)
    if pad:
        flat = jnp.concatenate([flat, jnp.zeros((N_DEV, pad), dtype)], axis=1)
    return flat.reshape(N_DEV, -1, PACK_W)


def _to_full(stacked, axis):
    t = jnp.moveaxis(stacked, 0, axis)
    s = list(t.shape)
    return t.reshape(s[:axis] + [s[axis] * s[axis + 1]] + s[axis + 2:])


def _to_chunks(full, axis):
    s = list(full.shape)
    t = full.reshape(s[:axis] + [N_DEV, s[axis] // N_DEV] + s[axis + 1:])
    return jnp.moveaxis(t, axis, 0)


def _rows_of(S):
    return _tile(S, (512, 256, 128, 64))


def _norm_fwd(x, g, dt=BF):
    (h,), _ = _seq_fwd("rmsnorm_fwd", f_rmsnorm, _rows_of(x.shape[0]), [_row_spec(x)], [_par_spec(g)], [],
                       [_out_spec(x.shape[1], dt=dt)])
    return h


def _norm_bwd(x, g, dh, res=None):
    if res is None:
        (dx,), (dg,) = _seq_bwd("rmsnorm_bwd", f_rmsnorm, _rows_of(x.shape[0]), [_row_spec(x)], [_par_spec(g)], [],
                                [_row_spec(dh)], [])
        return dx, None, dg
    (dx,), (dg,), (dxb,) = _seq_bwd("rmsnorm_res_bwd", f_rmsnorm, _rows_of(x.shape[0]), [_row_spec(x)], [_par_spec(g)], [],
                                    [_row_spec(dh)], [], dx_add={0: res}, dx_bf=True)
    return dx, dxb, dg


def _mixer_specs(kind, S, p, w):
    if kind == 0:
        return (f_gla, CHUNK, [_row_spec(p)],
                [_par_spec(w['a_w_gate2']), _par_spec(w['a_b_gate'].reshape(1, -1)), _par_spec(w['a_o_norm'].reshape(1, -1))],
                [(GLA_DV, GLA_DK)] * GLA_HEADS, D_MIX)
    if kind == 2:
        return (f_ssd, CHUNK, [_row_spec(p, prev='halo')],
                [_par_spec(w['c_conv_w']), _par_spec(w['c_conv_b'].reshape(1, -1)), _par_spec(w['c_dt_bias'].reshape(1, -1)),
                 _par_spec(w['c_a_log'].reshape(1, -1)), _par_spec(w['c_d'].reshape(1, -1)),
                 _par_spec(w['c_norm'].reshape(1, -1))],
                [(SSM_STATE, 2 * SSM_HD)] * (SSM_HEADS // 2), D_MIX)
    return (f_hgrn, CHUNK, [_row_spec(p)],
            [_par_spec(w['d_lower_bounds']), _par_spec(w['d_o_norm'].reshape(1, -1))],
            [(HGRN_DV, HGRN_DK)] * HGRN_HEADS, D_MIX)


def _perm(t, r):
    if r == 1:
        return t
    S, n = t.shape
    return t.reshape(S // r, r, n).transpose(1, 0, 2).reshape(S, n)


def _unperm(t, r):
    if r == 1:
        return t
    S, n = t.shape
    return t.reshape(r, S // r, n).transpose(1, 0, 2).reshape(S, n)


def _rope_consts():
    half = DIL_HD // 2
    inv = ROPE_THETA ** (-jnp.arange(half, dtype=F32) / half)
    invf = jnp.concatenate([inv, inv]).reshape(1, DIL_HD)
    sign = jnp.concatenate([-jnp.ones((half,), F32), jnp.ones((half,), F32)]).reshape(1, DIL_HD)
    return invf, sign


def _dil_fwd(p, pos, w):
    S = p.shape[0]
    invf, sign = _rope_consts()
    prep_rows = [_row_spec(p), _row_spec(pos, diff=False)]
    prep_pars = [_par_spec(w['b_q_norm'].reshape(1, -1)), _par_spec(w['b_k_norm'].reshape(1, -1)),
                 _par_spec(invf, diff=False), _par_spec(sign, diff=False)]
    nqk = len(DIL_GROUPS) * D_DIL
    (qr, kr, v), _ = _seq_fwd("dil_prep_fwd", f_dil_prep, _tile(S, (256, 128)), prep_rows, prep_pars, [],
                              [_out_spec(nqk), _out_spec(nqk), _out_spec(nqk)])
    res = dict(perm=[], o=[], lse=[])
    for g, (window, r) in enumerate(DIL_GROUPS):
        sl = slice(g * D_DIL, (g + 1) * D_DIL)
        qp, kp, vp = _perm(qr[:, sl], r), _perm(kr[:, sl], r), _perm(v[:, sl], r)
        rows = [_row_spec(qp), _row_spec(kp, prev='block'), _row_spec(vp, prev='block')]
        (o, lse), _ = _seq_fwd("dil_attn_fwd", f_dil_attn, DIL_BLOCK, rows, [], [], [_out_spec(D_DIL), _out_spec(D_DIL)],
                               period=S // r // DIL_BLOCK)
        res['perm'].append((qp, kp, vp))
        res['o'].append(_unperm(o, r))
        res['lse'].append(_unperm(lse, r))
    mrows = [_row_spec(t) for t in res['o'] + res['lse']]
    (tok,), _ = _seq_fwd("dil_merge_fwd", f_dil_merge, _rows_of(S), mrows, [], [], [_out_spec(D_DIL)])
    res['prep'] = (prep_rows, prep_pars)
    return tok, res


def _dil_bwd(dtok, res, p):
    S = p.shape[0]
    mrows = [_row_spec(t) for t in res['o'] + res['lse']]
    dm, _ = _seq_bwd("dil_merge_bwd", f_dil_merge, _rows_of(S), mrows, [], [], [dtok], [])
    dq, dk, dv = [], [], []
    for g, (window, r) in enumerate(DIL_GROUPS):
        qp, kp, vp = res['perm'][g]
        rows = [_row_spec(qp), _row_spec(kp, prev='block'), _row_spec(vp, prev='block')]
        douts = [_row_spec(_perm(dm[g], r)), _row_spec(_perm(dm[3 + g], r))]
        (a, b, c), _ = _seq_bwd("dil_attn_bwd", f_dil_attn, DIL_BLOCK, rows, [], [], douts, [],
                                period=S // r // DIL_BLOCK)
        dq.append(_unperm(a, r)); dk.append(_unperm(b, r)); dv.append(_unperm(c, r))
    dqr, dkr, dv = (jnp.concatenate(t, axis=1) for t in (dq, dk, dv))
    prep_rows, prep_pars = res['prep']
    (dp,), (dqn, dkn) = _seq_bwd("dil_prep_bwd", f_dil_prep, _tile(S, (256, 128)), prep_rows, prep_pars, [],
                                 [_row_spec(dqr), _row_spec(dkr), _row_spec(dv)], [], dx_dt=BF)
    return dp, dict(b_q_norm=dqn.reshape(-1), b_k_norm=dkn.reshape(-1))


def _ffn_specs(u, cw, cb):
    half = N_DEV // 2
    rows = [_row_spec(u, prev='halo', lb=(2, None), li=lambda jc: (0, jc))]
    pars = [_par_spec(cw, bs=(None, FFN_CONV, FF_SH), idx=lambda jc: (jc, 0, 0)),
            _par_spec(cw, bs=(None, FFN_CONV, FF_SH), idx=lambda jc: (jc + half, 0, 0)),
            _par_spec(cb, bs=(None, 1, FF_SH), idx=lambda jc: (jc, 0, 0)),
            _par_spec(cb, bs=(None, 1, FF_SH), idx=lambda jc: (jc + half, 0, 0))]
    return half, rows, pars


N_MIX = {0: 2 * GLA_HEADS * GLA_DK + 2 * D_MIX + GLA_RANK, 1: 3 * len(DIL_GROUPS) * D_DIL,
         2: 2 * D_MIX + 2 * SSM_GROUPS * SSM_STATE + SSM_HEADS, 3: 2 * HGRN_HEADS * HGRN_DK + 2 * D_MIX}
W_IN = {0: 'a_w_in', 1: 'b_w_in', 2: 'c_w_in', 3: 'd_w_in'}
W_OUT = {0: 'a_w_out', 1: 'b_w_out', 2: 'c_w_out', 3: 'd_w_out'}


def _in_blocks(name, t):
    return t if SHARD_AXIS[name] == 1 else t.reshape(1, N_DEV * t.shape[1], t.shape[2])


LAYER_STACKED = ('ffn_w_up', 'ffn_conv_w', 'ffn_w_down', 'xa_w_kv')


def _device_step(x, mem, pos, gw, rep, target, distributed=True):
    S, D = x.shape
    w = dict(rep)
    w['a_w_gate2'] = _to_full(gw['a_w_gate2'], 1)
    w['c_conv_w'] = _to_full(gw['c_conv_w'], 1)
    posf = pos.reshape(S, 1).astype(F32)
    n_mix, w_in_name, w_out_name = N_MIX, W_IN, W_OUT
    ntot = {k: -(-(n_mix[k] + D_XA) // 256) * 256 for k in n_mix}
    in_blocks = {k: _in_blocks(w_in_name[k], gw[w_in_name[k]]) for k in n_mix}
    w_out = {k: (_to_full(gw[w_out_name[k]], 1) if SHARD_AXIS[w_out_name[k]] == 1
                 else gw[w_out_name[k]].reshape(-1, D)) for k in n_mix}
    mem_g = w['mem_norm'].reshape(1, -1)
    mem_n = _norm_fwd(mem, mem_g)
    R = _rows_of(S)

    saved = []
    for i in range(DEPTH):
        kind = i % 4
        L = dict(x0=x)
        g1 = w['mix_norm'][i].reshape(1, -1)
        h = _norm_fwd(x, g1)
        wcat = _cat_cols(in_blocks[kind], n_mix[kind], ntot[kind])
        p = _matmul(h, wcat, name="matmul_in")
        if kind == 1:
            tok, L['dil'] = _dil_fwd(p, posf, w)
        else:
            f, Rm, rows, pars, sshapes, _ = _mixer_specs(kind, S, p, w)
            (tok,), L['states'] = _seq_fwd("mixer%d_fwd" % kind, f, Rm, rows, pars, sshapes, [_out_spec(D_MIX)],
                                           save_states=True)
        wkv = gw['xa_w_kv'][:, i].reshape(D, 2 * D_XA)
        kv = _matmul(mem_n, wkv, name="matmul_kv")
        xa_rows = [_row_spec(p, w=D_XA, dn=D_XA)]
        xa_pars = [_par_spec(kv), _par_spec(w['xa_q_norm'][i].reshape(1, -1)), _par_spec(w['xa_k_norm'][i].reshape(1, -1))]
        (xa,), _ = _seq_fwd("xattn_fwd", f_xattn, R, xa_rows, xa_pars, [], [_out_spec(D_XA)])
        cat = jnp.concatenate([tok, xa], axis=1).astype(BF)
        x1 = _matmul(cat, w_out[kind], add=x, name="matmul_out")
        g2 = w['ffn_norm'][i].reshape(1, -1)
        h2 = _norm_fwd(x1, g2)
        wup = gw['ffn_w_up'][:, i]
        u = _ffn_up(h2, wup)
        cw, cb = gw['ffn_conv_w'][:, i], w['ffn_conv_b'][i].reshape(N_DEV, 1, FF_SH)
        nt, frows, fpars = _ffn_specs(u, cw, cb)
        (act,), _ = _seq_fwd("ffn_act_fwd", f_ffn_act, R, frows, fpars, [],
                             [_out_spec(FF_SH, dt=BF, ls=(nt,), lb=(None,), li=lambda jc: (jc,))], ncol=nt)
        wd = gw['ffn_w_down'][:, i].reshape(D_FF, D)
        x = _ffn_down(act, wd, x1)
        L.update(h=h, p=p, wcat=wcat, kv=kv, wkv=wkv, cat=cat, x1=x1, h2=h2, u=u, act=act, wd=wd, wup=wup, g1=g1, g2=g2)
        saved.append(L)

    dx, dxb, loss = _loss_head(x, target)

    G = {}
    d_mem_n = None
    acc = {k: [None] * DEPTH for k in ('mix_norm', 'ffn_norm', 'ffn_conv_b', 'xa_q_norm', 'xa_k_norm')}
    parts = [None] * DEPTH
    pending = None
    half = N_DEV // 2
    for i in reversed(range(DEPTH)):
        kind = i % 4
        L = saved[i]
        Gc = {}
        Gc['ffn_w_down'] = _ffn_dw_down(L['act'], dxb).reshape(N_DEV, D_FF // N_DEV, D)
        dact = _ffn_dact(dxb, L['wd'])
        cw, cb = gw['ffn_conv_w'][:, i], w['ffn_conv_b'][i].reshape(N_DEV, 1, FF_SH)
        nt, frows, fpars = _ffn_specs(L['u'], cw, cb)
        (du,), (dwg, dwv, dbg, dbv) = _seq_bwd(
            "ffn_act_bwd", f_ffn_act, R, frows, fpars, [], [_row_spec(dact, lb=(None,), li=lambda jc: (jc,))], [],
            ncol=nt, dx_dt=BF)
        Gc['ffn_conv_w'] = jnp.concatenate([dwg[:half], dwv[half:]], axis=0)
        acc['ffn_conv_b'][i] = jnp.concatenate([dbg[:half], dbv[half:]], axis=0).reshape(-1)
        if pending is not None and distributed:
            first = ['ffn_w_up']
            rest = [n for n in pending if n not in first]
            Gc['ffn_w_up'], got_a = _ffn_dw_up(L['h2'], du, exchange=[pending[n] for n in first])
            dh2, got_b = _ffn_dh2(du, L['wup'], exchange=[pending[n] for n in rest])
            parts[i + 1] = dict(zip(first + rest, got_a + got_b))
        else:
            if pending is not None:
                parts[i + 1] = pending
            Gc['ffn_w_up'] = _ffn_dw_up(L['h2'], du)
            dh2 = _ffn_dh2(du, L['wup'])
        dx1, dx1b, dg2 = _norm_bwd(L['x1'], L['g2'], dh2, res=dx)
        acc['ffn_norm'][i] = dg2.reshape(-1)
        G_out = _matmul(L['cat'], dx1b, mode="tn", out_dtype=BF, name="matmul_dw_out")
        dcat = _matmul(dx1b, w_out[kind], mode="nt", name="matmul_dcat")
        ntok = D_DIL if kind == 1 else D_MIX
        dtok = _row_spec(dcat, w=ntok)
        dxa = _row_spec(dcat, w=D_XA, c=lambda jc: ntok // D_XA)
        p = L['p']
        if kind == 1:
            dp, gm = _dil_bwd(dtok, L['dil'], p)
            G.update(gm)
        else:
            f, Rm, rows, pars, sshapes, _ = _mixer_specs(kind, S, p, w)
            (dp,), dps = _seq_bwd("mixer%d_bwd" % kind, f, Rm, rows, pars, sshapes, [dtok], L['states'], dx_dt=BF)
            if kind == 0:
                Gc['a_w_gate2'], G['a_b_gate'], G['a_o_norm'] = _to_chunks(dps[0], 1), dps[1].reshape(-1), dps[2].reshape(-1)
            elif kind == 2:
                Gc['c_conv_w'] = _to_chunks(dps[0], 1)
                for nme, v in zip(('c_conv_b', 'c_dt_bias', 'c_a_log', 'c_d', 'c_norm'), dps[1:]):
                    G[nme] = v.reshape(-1)
            else:
                G['d_lower_bounds'], G['d_o_norm'] = dps[0], dps[1].reshape(-1)
        xa_rows = [_row_spec(p, w=D_XA)]
        xa_pars = [_par_spec(L['kv']), _par_spec(w['xa_q_norm'][i].reshape(1, -1)), _par_spec(w['xa_k_norm'][i].reshape(1, -1))]
        (dp,), (dkv, dqn, dkn) = _seq_bwd("xattn_bwd", f_xattn, R, xa_rows, xa_pars, [], [dxa], [], dx_dt=BF,
                                          dx_alias={0: dp})
        acc['xa_q_norm'][i], acc['xa_k_norm'][i] = dqn.reshape(-1), dkn.reshape(-1)
        Gc['xa_w_kv'] = _matmul(mem_n, dkv, mode="tn", out_dtype=BF, name="matmul_dw_kv").reshape(
            N_DEV, D // N_DEV, 2 * D_XA)
        d_mem_n = _matmul(dkv, L['wkv'], mode="nt", add=d_mem_n, name="matmul_dmem" + ("" if d_mem_n is None else "_acc"))
        dwcat = _matmul(L['h'], dp, mode="tn", out_dtype=BF, name="matmul_dw_in")
        blocks = in_blocks[kind]
        Gc[w_in_name[kind]] = _uncat_cols(dwcat, blocks.shape[0], blocks.shape[2], n_mix[kind]).reshape(
            gw[w_in_name[kind]].shape)
        Gc[w_out_name[kind]] = (_to_chunks(G_out, 1) if SHARD_AXIS[w_out_name[kind]] == 1
                                else G_out.reshape(gw[w_out_name[kind]].shape))
        dh = _matmul(dp, L['wcat'], mode="nt", name="matmul_dh")
        dx, dxb, dg1 = _norm_bwd(L['x0'], L['g1'], dh, res=dx1)
        acc['mix_norm'][i] = dg1.reshape(-1)
        pending = Gc

    names = list(pending)
    parts[0] = dict(zip(names, _exchange_many([pending[n] for n in names], "exchange_grads"))) if distributed else pending
    _, _, dmg = _norm_bwd(mem, mem_g, d_mem_n)
    G['mem_norm'] = dmg.reshape(-1)
    for k, v in acc.items():
        G[k] = jnp.stack(v)
    got = {}
    for i in range(DEPTH):
        for n, t in parts[i].items():
            if n not in LAYER_STACKED:
                got[n] = t
    for n in LAYER_STACKED:
        got[n] = jnp.stack([parts[i][n] for i in range(DEPTH)], axis=1)
    return loss, dx, got, G


def kernel(x, mem, positions, mem_norm, mix_norm, xa_w_kv, xa_q_norm, xa_k_norm, ffn_norm, ffn_w_up, ffn_conv_w, ffn_conv_b, ffn_w_down, a_w_in, a_w_gate2, a_b_gate, a_o_norm, a_w_out, b_w_in, b_q_norm, b_k_norm, b_w_out, c_w_in, c_conv_w, c_conv_b, c_dt_bias, c_a_log, c_d, c_norm, c_w_out, d_w_in, d_lower_bounds, d_o_norm, d_w_out, loss_target, m_mem_norm, m_mix_norm, m_xa_w_kv, m_xa_q_norm, m_xa_k_norm, m_ffn_norm, m_ffn_w_up, m_ffn_conv_w, m_ffn_conv_b, m_ffn_w_down, m_a_w_in, m_a_w_gate2, m_a_b_gate, m_a_o_norm, m_a_w_out, m_b_w_in, m_b_q_norm, m_b_k_norm, m_b_w_out, m_c_w_in, m_c_conv_w, m_c_conv_b, m_c_dt_bias, m_c_a_log, m_c_d, m_c_norm, m_c_w_out, m_d_w_in, m_d_lower_bounds, m_d_o_norm, m_d_w_out, v_mem_norm, v_mix_norm, v_xa_w_kv, v_xa_q_norm, v_xa_k_norm, v_ffn_norm, v_ffn_w_up, v_ffn_conv_w, v_ffn_conv_b, v_ffn_w_down, v_a_w_in, v_a_w_gate2, v_a_b_gate, v_a_o_norm, v_a_w_out, v_b_w_in, v_b_q_norm, v_b_k_norm, v_b_w_out, v_c_w_in, v_c_conv_w, v_c_conv_b, v_c_dt_bias, v_c_a_log, v_c_d, v_c_norm, v_c_w_out, v_d_w_in, v_d_lower_bounds, v_d_o_norm, v_d_w_out):
    args = locals()
    w = {n: args[n] for n in WEIGHTS}
    m = {n: args['m_' + n] for n in WEIGHTS}
    v = {n: args['v_' + n] for n in WEIGHTS}

    big = [n for n in SHARDED if w[n].size >= 65536]
    small = [n for n in SHARDED if n not in big]
    gathered = _gather_many([w[n].astype(BF) for n in big] + [w[n] for n in small], "gather_weights")
    gw = dict(zip(big + small, gathered))

    loss, grad_x, parts, G = _device_step(x[0], mem[0], positions[0], gw, {n: w[n] for n in REPLICATED}, loss_target[0])
    loss = lax.psum(loss, ("x", "y", "c"))
    rep_parts = _all_gather(_pack([G[n] for n in REPLICATED], F32), "gather_replicated_grads")

    out = {}

    def put(names, res, shapes):
        for kind, r in zip(("grad", "delta", "new_m", "new_v"), res):
            for n, t in zip(names, _unpack(r, shapes)):
                out[kind + "_" + n] = t

    for n in big:
        shp = tuple(w[n].shape)
        two_d = (math.prod(shp[:-1]), shp[-1])
        res = _adamw(parts[n].reshape((N_DEV,) + two_d), w[n].reshape(two_d), m[n].reshape(two_d), v[n].reshape(two_d),
                     "adamw")
        for kind, r in zip(("grad", "delta", "new_m", "new_v"), res):
            out[kind + "_" + n] = r.reshape(shp)
    for names, prt, tag in ((small, _pack_lead([parts[n] for n in small], F32), "adamw_small"),
                            (REPLICATED, rep_parts, "adamw_replicated")):
        res = _adamw(prt, _pack([w[n] for n in names], F32), _pack([m[n] for n in names], F32),
                     _pack([v[n] for n in names], F32), tag)
        put(names, res, [tuple(w[n].shape) for n in names])
    return (loss, grad_x[None], *[out["grad_" + n] for n in WEIGHTS], *[out["delta_" + n] for n in WEIGHTS],
            *[out["new_m_" + n] for n in WEIGHTS], *[out["new_v_" + n] for n in WEIGHTS])
```

```python
import functools
import math

import jax
import jax.numpy as jnp
from jax import lax
from jax.experimental import pallas as pl
from jax.experimental.pallas import tpu as pltpu

F32 = jnp.float32
BF = jnp.bfloat16
_MM_DTYPE = BF

N_DEV = 8
EPS = 1e-6
ROPE_THETA = 10000.0
CHUNK = 64
D_MIX = 768
XA_HEADS, XA_HD, D_XA = 4, 64, 256
GLA_HEADS, GLA_DK, GLA_DV, GLA_RANK, GLA_GATE_NORM = 4, 96, 192, 16, 16.0
DIL_GROUPS = ((128, 1), (512, 4), (2048, 16))
DIL_HEADS, DIL_HD, DIL_BLOCK, D_DIL = 4, 128, 128, 512
SSM_HD, SSM_HEADS, SSM_GROUPS, SSM_STATE, SSM_CONV = 64, 12, 2, 128, 4
HGRN_HEADS, HGRN_DK, HGRN_DV = 6, 128, 128
D_FF = 2816
FFN_CONV = 3
DEPTH = 4
ADAM_LR, ADAM_B1, ADAM_B2, ADAM_EPS, ADAM_WD, ADAM_STEP = 0.001, 0.9, 0.999, 1e-08, 0.01, 10
NEG = -1e30
HALO = 8
VMEM_LIMIT = 56 << 20
ADAM_BLOCK = 1 << 18

WEIGHTS = ['mem_norm', 'mix_norm', 'xa_w_kv', 'xa_q_norm', 'xa_k_norm', 'ffn_norm', 'ffn_w_up', 'ffn_conv_w',
           'ffn_conv_b', 'ffn_w_down', 'a_w_in', 'a_w_gate2', 'a_b_gate', 'a_o_norm', 'a_w_out', 'b_w_in', 'b_q_norm',
           'b_k_norm', 'b_w_out', 'c_w_in', 'c_conv_w', 'c_conv_b', 'c_dt_bias', 'c_a_log', 'c_d', 'c_norm', 'c_w_out',
           'd_w_in', 'd_lower_bounds', 'd_o_norm', 'd_w_out']
SHARD_AXIS = {'xa_w_kv': 1, 'ffn_w_up': 2, 'ffn_conv_w': 2, 'ffn_w_down': 1, 'a_w_in': 1, 'a_w_gate2': 1, 'a_w_out': 0,
              'b_w_in': 1, 'b_w_out': 1, 'c_w_in': 0, 'c_conv_w': 1, 'c_w_out': 0, 'd_w_in': 1, 'd_w_out': 0}
SHARDED = [n for n in WEIGHTS if n in SHARD_AXIS]
REPLICATED = [n for n in WEIGHTS if n not in SHARD_AXIS]


def _dot(a, b, ca, cb):
    return lax.dot_general(a.astype(_MM_DTYPE), b.astype(_MM_DTYPE), (((ca,), (cb,)), ((), ())),
                           preferred_element_type=F32)


@jax.custom_vjp
def mm_nn(a, b):
    return _dot(a, b, 1, 0)


mm_nn.defvjp(lambda a, b: (_dot(a, b, 1, 0), (a, b)),
             lambda r, g: (_dot(g, r[1], 1, 1), _dot(r[0], g, 0, 0)))


@jax.custom_vjp
def mm_nt(a, b):
    return _dot(a, b, 1, 1)


mm_nt.defvjp(lambda a, b: (_dot(a, b, 1, 1), (a, b)),
             lambda r, g: (_dot(g, r[1], 1, 0), _dot(g, r[0], 0, 0)))


@jax.custom_vjp
def mm_tn(a, b):
    return _dot(a, b, 0, 0)


mm_tn.defvjp(lambda a, b: (_dot(a, b, 0, 0), (a, b)),
             lambda r, g: (_dot(r[1], g, 1, 1), _dot(r[0], g, 1, 0)))


def _dot_hi(a, b, ca, cb):
    return lax.dot_general(a, b, (((ca,), (cb,)), ((), ())), precision=lax.Precision.HIGHEST,
                           preferred_element_type=F32)


def _tril(c):
    return lax.broadcasted_iota(jnp.int32, (c, c), 0) >= lax.broadcasted_iota(jnp.int32, (c, c), 1)


@jax.custom_vjp
def cumsum_rows(x):
    return _dot_hi(_tril(x.shape[0]).astype(F32), x, 1, 0)


cumsum_rows.defvjp(lambda x: (cumsum_rows(x), None),
                   lambda r, g: (_dot_hi(_tril(g.shape[0]).astype(F32), g, 0, 0),))


@jax.custom_vjp
def cumsum_rows_t(x):
    return _dot_hi(x, _tril(x.shape[0]).astype(F32), 0, 1)


cumsum_rows_t.defvjp(lambda x: (cumsum_rows_t(x), None),
                     lambda r, g: (_dot_hi(_tril(g.shape[1]).astype(F32), g, 0, 1),))


def _split(x, sizes):
    sizes = tuple(int(s) for s in sizes)
    assert sum(sizes) == x.shape[-1], (sizes, x.shape)

    @jax.custom_vjp
    def sp(x):
        out, o = [], 0
        for s in sizes:
            out.append(x[:, o:o + s])
            o += s
        return tuple(out)

    sp.defvjp(lambda x: (sp(x), None), lambda r, g: (jnp.concatenate(list(g), axis=1),))
    return sp(x)


def _row(x, r):
    m = lax.broadcasted_iota(jnp.int32, x.shape, 0) == r
    return jnp.sum(jnp.where(m, x, 0.0), axis=0, keepdims=True)


@jax.custom_vjp
def _roll_half(x):
    return pltpu.roll(x, 64, 1)


_roll_half.defvjp(lambda x: (pltpu.roll(x, 64, 1), None), lambda r, g: (pltpu.roll(g, 64, 1),))


def _shift(xp, x, d):
    if d == 0:
        return x
    n, m = x.shape[0], xp.shape[0]
    assert d <= m == HALO

    @jax.custom_vjp
    def sh(xp, x):
        r = pltpu.roll(x, d, 0)
        row = lax.broadcasted_iota(jnp.int32, xp.shape, 0)
        head = jnp.where(row < d, pltpu.roll(xp, d, 0), r[:m])
        return jnp.concatenate([head, r[m:]], axis=0)

    def bwd(_, g):
        row = lax.broadcasted_iota(jnp.int32, g.shape, 0)
        rowp = lax.broadcasted_iota(jnp.int32, (m,) + g.shape[1:], 0)
        dxp = jnp.where(rowp >= m - d, pltpu.roll(g[:m], m - d, 0), 0.0)
        return dxp, jnp.where(row < n - d, pltpu.roll(g, n - d, 0), 0.0)

    sh.defvjp(lambda xp, x: (sh(xp, x), None), bwd)
    return sh(xp, x)


def _rms(x, g):
    return x * lax.rsqrt(jnp.mean(x * x, axis=-1, keepdims=True) + EPS) * g


def _lane_pair(a, b, width=128):
    shape = a.shape[:-1] + (width,)
    lane = lax.broadcasted_iota(jnp.int32, shape, len(shape) - 1)
    return jnp.where(lane < width // 2, a, b)


def _row_spec(a, w=None, c=None, prev=False, diff=True, dn=None, lb=(), li=None):
    return dict(a=a, w=a.shape[-1] if w is None else w, c=(lambda jc: 0) if c is None else c, prev=prev, diff=diff,
                dn=a.shape[-1] if dn is None else dn, lb=tuple(lb), li=(lambda jc: ()) if li is None else li)


def _par_spec(a, bs=None, idx=None, diff=True):
    nd = a.ndim
    return dict(a=a, bs=tuple(a.shape) if bs is None else tuple(bs),
                idx=(lambda jc: (0,) * nd) if idx is None else idx, diff=diff)


def _out_spec(n, w=None, c=None, dt=F32, ls=(), lb=(), li=None):
    return dict(n=n, w=n if w is None else w, c=(lambda jc: 0) if c is None else c, dt=dt, ls=tuple(ls), lb=tuple(lb),
                li=(lambda jc: ()) if li is None else li)


def _cparams():
    return pltpu.CompilerParams(dimension_semantics=("arbitrary", "arbitrary"), vmem_limit_bytes=VMEM_LIMIT)


def _bspec(s, R, rowfn):
    return pl.BlockSpec(s['lb'] + (R, s['w']),
                        functools.partial(lambda jc, i, s: tuple(s['li'](jc)) + (rowfn(i), s['c'](jc)), s=s))


def _prev_rows(s, R):
    return R if s['prev'] == 'block' else HALO


def _pspec(s, R, blockfn):
    pr = _prev_rows(s, R)
    return pl.BlockSpec(s['lb'] + (pr, s['w']), functools.partial(
        lambda jc, i, s: tuple(s['li'](jc)) + (jnp.maximum(blockfn(i) * (R // pr) - 1, 0), s['c'](jc)), s=s))


def _seq_fwd(name, f, R, rows, params, state_shapes, outs, *, ncol=1, period=None, save_states=False):
    nrows = rows[0]['a'].shape[-2]
    nb = nrows // R
    assert nb * R == nrows
    period = nb if period is None else period
    prev_ids = [k for k, r in enumerate(rows) if r['prev']]
    n_rows, n_prev, n_par, n_out, n_st = len(rows), len(prev_ids), len(params), len(outs), len(state_shapes)

    def body(*refs):
        o = 0
        cur = refs[o:o + n_rows]; o += n_rows
        prv = refs[o:o + n_prev]; o += n_prev
        par = refs[o:o + n_par]; o += n_par
        out = refs[o:o + n_out]; o += n_out
        sav = refs[o:o + (n_st if save_states else 0)]; o += len(sav)
        st = refs[o:o + n_st]
        i = pl.program_id(1)
        first = (i % period) == 0

        @pl.when(i == 0)
        def _():
            for s in st:
                s[...] = jnp.zeros_like(s)

        xs = [r[...].astype(F32) for r in cur]
        xp = [r[...].astype(F32) for r in prv]
        ps = [r[...] for r in par]
        sts = [s[...] for s in st]
        for sv, s in zip(sav, sts):
            sv[0] = s
        ov, ns = f(first, xp, xs, ps, sts)
        for r, v in zip(out, ov):
            r[...] = v.astype(r.dtype)
        for s, v in zip(st, ns):
            s[...] = v

    in_specs = [_bspec(r, R, lambda i: i) for r in rows]
    in_specs += [_pspec(rows[k], R, lambda i: i) for k in prev_ids]
    in_specs += [pl.BlockSpec(p['bs'], functools.partial(lambda jc, i, idx: idx(jc), idx=p['idx'])) for p in params]
    out_specs = [_bspec(o_, R, lambda i: i) for o_ in outs]
    out_shape = [jax.ShapeDtypeStruct(o_['ls'] + (nrows, o_['n']), o_['dt']) for o_ in outs]
    if save_states:
        for s in state_shapes:
            out_specs.append(pl.BlockSpec((1,) + tuple(s), lambda jc, i, nd=len(s): (i,) + (0,) * nd))
            out_shape.append(jax.ShapeDtypeStruct((nb,) + tuple(s), F32))
    args = [r['a'] for r in rows] + [rows[k]['a'] for k in prev_ids] + [p['a'] for p in params]
    res = pl.pallas_call(
        body, name=name, grid=(ncol, nb), in_specs=in_specs, out_specs=out_specs, out_shape=out_shape,
        scratch_shapes=[pltpu.VMEM(tuple(s), F32) for s in state_shapes], compiler_params=_cparams())(*args)
    return list(res[:n_out]), list(res[n_out:])


def _seq_bwd(name, f, R, rows, params, state_shapes, douts, saved, *, ncol=1, period=None, dx_dt=F32, dx_add=None,
             dx_bf=False, dx_alias=None):
    nrows = rows[0]['a'].shape[-2]
    nb = nrows // R
    period = nb if period is None else period
    prev_ids = [k for k, r in enumerate(rows) if r['prev']]
    drow_ids = [k for k, r in enumerate(rows) if r['diff']]
    dpar_ids = [k for k, p in enumerate(params) if p['diff']]
    for k in prev_ids:
        assert rows[k]['diff']
    dx_add, dx_alias = dict(dx_add or {}), dict(dx_alias or {})
    add_ids, alias_ids = sorted(dx_add), sorted(dx_alias)
    n_rows, n_prev, n_par, n_do, n_st = len(rows), len(prev_ids), len(params), len(douts), len(state_shapes)
    n_dx, n_dp, n_add, n_al = len(drow_ids), len(dpar_ids), len(add_ids), len(alias_ids)

    def body(*refs):
        o = 0
        cur = refs[o:o + n_rows]; o += n_rows
        prv = refs[o:o + n_prev]; o += n_prev
        par = refs[o:o + n_par]; o += n_par
        sav = refs[o:o + n_st]; o += n_st
        dou = refs[o:o + n_do]; o += n_do
        adr = refs[o:o + n_add]; o += n_add
        o += n_al
        dxr = refs[o:o + n_dx]; o += n_dx
        dpr = refs[o:o + n_dp]; o += n_dp
        dxb = refs[o:o + (n_dx if dx_bf else 0)]; o += len(dxb)
        dst = refs[o:o + n_st]; o += n_st
        car = refs[o:o + n_prev]
        j = pl.program_id(1)
        i = nb - 1 - j
        first = (i % period) == 0

        @pl.when(j == 0)
        def _():
            for s in tuple(dst) + tuple(car) + tuple(dpr):
                s[...] = jnp.zeros_like(s)

        xs = [r[...].astype(F32) for r in cur]
        xp = [r[...].astype(F32) for r in prv]
        ps = [r[...] for r in par]
        sts = [s[0] for s in sav]

        def g(dxs, dxp, dps, dsts):
            xs_, ps_ = list(xs), list(ps)
            for k, v in zip(drow_ids, dxs):
                xs_[k] = v
            for k, v in zip(dpar_ids, dps):
                ps_[k] = v
            ov, ns = f(first, list(dxp), xs_, ps_, list(dsts))
            return tuple(ov), tuple(ns)

        _, vjp = jax.vjp(g, tuple(xs[k] for k in drow_ids), tuple(xp), tuple(ps[k] for k in dpar_ids), tuple(sts))
        dxs, dxp, dps, dsts = vjp((tuple(r[...].astype(F32) for r in dou), tuple(s[...] for s in dst)))
        dxs = list(dxs)
        for n_, pos in enumerate(add_ids):
            dxs[pos] = dxs[pos] + adr[n_][...].astype(F32)
        tails = {}
        for n_, k in enumerate(prev_ids):
            pos = drow_ids.index(k)
            if rows[k]['prev'] == 'block':
                dxs[pos] = dxs[pos] + car[n_][...]
            else:
                tails[pos] = car[n_][...]
            car[n_][...] = dxp[n_]
        for pos, v in enumerate(dxs):
            outs_ = [dxr[pos]] + ([dxb[pos]] if dx_bf else [])
            if pos in tails:
                v = jnp.concatenate([v[..., :R - HALO, :], v[..., R - HALO:, :] + tails[pos]], axis=-2)
            for r in outs_:
                r[...] = v.astype(r.dtype)
        for r, v in zip(dpr, dps):
            r[...] += v
        for s, v in zip(dst, dsts):
            s[...] = v

    def rev(j):
        return nb - 1 - j

    def dspec(k):
        return _bspec(rows[k], R, rev)

    in_specs = [_bspec(r, R, rev) for r in rows]
    in_specs += [_pspec(rows[k], R, rev) for k in prev_ids]
    in_specs += [pl.BlockSpec(p['bs'], functools.partial(lambda jc, j, idx: idx(jc), idx=p['idx'])) for p in params]
    in_specs += [pl.BlockSpec((1,) + tuple(s), lambda jc, j, nd=len(s): (nb - 1 - j,) + (0,) * nd) for s in state_shapes]
    in_specs += [_bspec(d, R, rev) for d in douts]
    in_specs += [dspec(drow_ids[pos]) for pos in add_ids]
    in_specs += [pl.BlockSpec(memory_space=pl.ANY) for _ in alias_ids]
    out_specs = [dspec(k) for k in drow_ids]
    out_shape = [jax.ShapeDtypeStruct(tuple(rows[k]['a'].shape[:-1]) + (rows[k]['dn'],), dx_dt) for k in drow_ids]
    for k in dpar_ids:
        p = params[k]
        out_specs.append(pl.BlockSpec(p['bs'], functools.partial(lambda jc, j, idx: idx(jc), idx=p['idx'])))
        out_shape.append(jax.ShapeDtypeStruct(p['a'].shape, F32))
    if dx_bf:
        out_specs += [dspec(k) for k in drow_ids]
        out_shape += [jax.ShapeDtypeStruct(tuple(rows[k]['a'].shape[:-1]) + (rows[k]['dn'],), BF) for k in drow_ids]
    scratch = [pltpu.VMEM(tuple(s), F32) for s in state_shapes]
    scratch += [pltpu.VMEM(tuple(d for d in rows[k]['lb'] if d is not None) + (_prev_rows(rows[k], R), rows[k]['w']), F32)
                for k in prev_ids]
    args = ([r['a'] for r in rows] + [rows[k]['a'] for k in prev_ids] + [p['a'] for p in params] + list(saved)
            + [d['a'] for d in douts] + [dx_add[pos] for pos in add_ids] + [dx_alias[pos] for pos in alias_ids])
    n_in = len(args)
    aliases = {n_in - n_al + n_: pos for n_, pos in enumerate(alias_ids)}
    for pos in alias_ids:
        assert dx_alias[pos].shape == out_shape[pos].shape and dx_alias[pos].dtype == out_shape[pos].dtype
    res = pl.pallas_call(
        body, name=name, grid=(ncol, nb), in_specs=in_specs, out_specs=out_specs, out_shape=out_shape,
        scratch_shapes=scratch, input_output_aliases=aliases, compiler_params=_cparams())(*args)
    if dx_bf:
        return list(res[:n_dx]), list(res[n_dx:n_dx + n_dp]), list(res[n_dx + n_dp:])
    return list(res[:n_dx]), list(res[n_dx:])


def _tile(n, cands):
    for c in cands:
        if n % c == 0:
            return c
    return n


def _mm_call(name, grid, a, a_spec, b, b_spec, contract, out_shape, out_spec, acc_shape, add=None, add_spec=None,
             exchange=None, gather=None):
    nk = grid[2]
    ca, cb = contract
    has_add = add is not None
    ex = list(exchange or []) + list(gather or [])
    ex_shapes = [g.shape for g in exchange or []] + [(N_DEV,) + tuple(g.shape) for g in gather or []]
    n_ex = len(ex)
    n_in = 2 + has_add

    def body(*refs):
        a_ref, b_ref = refs[0], refs[1]
        add_ref = refs[2] if has_add else None
        o_ref = refs[n_in + n_ex]
        scr = refs[n_in + 2 * n_ex + 1:]
        step = [pl.program_id(d) for d in range(3)]
        if n_ex:
            g_refs, r_refs, sems = refs[n_in:n_in + n_ex], refs[n_in + n_ex + 1:n_in + 2 * n_ex + 1], scr[-3:]

            @pl.when((step[0] == 0) & (step[1] == 0) & (step[2] == 0))
            def _():
                _exchange_start(g_refs, r_refs, *sems)

        part = _dot(a_ref[...], b_ref[...], ca, cb)

        def finish(r):
            if has_add:
                r = r + add_ref[...].astype(F32)
            o_ref[...] = r.astype(o_ref.dtype)

        if nk == 1:
            finish(part)
        else:
            acc = scr[0]

            @pl.when(step[2] == 0)
            def _():
                acc[...] = part

            @pl.when(step[2] > 0)
            def _():
                acc[...] += part

            @pl.when(step[2] == nk - 1)
            def _():
                finish(acc[...])

        if n_ex:
            @pl.when((step[0] == grid[0] - 1) & (step[1] == grid[1] - 1) & (step[2] == grid[2] - 1))
            def _():
                _exchange_wait(g_refs, r_refs, *sems)

    in_specs, args = [a_spec, b_spec], [a, b]
    if has_add:
        in_specs.append(add_spec)
        args.append(add)
    any_spec = pl.BlockSpec(memory_space=pl.ANY)
    scratch = [] if nk == 1 else [pltpu.VMEM(acc_shape, F32)]
    if n_ex:
        scratch += _exchange_sems(n_ex)
    res = pl.pallas_call(
        body, name=name, grid=grid, in_specs=in_specs + [any_spec] * n_ex, out_specs=[out_spec] + [any_spec] * n_ex,
        out_shape=[out_shape] + [jax.ShapeDtypeStruct(s, g.dtype) for s, g in zip(ex_shapes, ex)], scratch_shapes=scratch,
        compiler_params=pltpu.CompilerParams(
            dimension_semantics=("arbitrary",) * 3 if n_ex else ("parallel", "parallel", "arbitrary"),
            vmem_limit_bytes=VMEM_LIMIT, has_side_effects=bool(n_ex)))(*args, *ex)
    return (res[0], list(res[1:])) if n_ex else res[0]


def _matmul(a, b, mode="nn", add=None, out_dtype=F32, name="matmul", **pushed):
    if mode == "nn":
        (M, K), N = a.shape, b.shape[1]
    elif mode == "nt":
        (M, K), N = a.shape, b.shape[0]
    else:
        (K, M), N = a.shape, b.shape[1]
    tm = _tile(M, (1024, 512, 256, 128, 64, 32, 16, 8))
    tn = _tile(N, (512, 256, 128))
    tk = K if K <= 5120 else _tile(K, (2048, 1024, 512, 256, 128))
    if mode == "tn":
        a_spec = pl.BlockSpec((tk, tm), lambda i, j, k: (k, i))
    else:
        a_spec = pl.BlockSpec((tm, tk), lambda i, j, k: (i, k))
    if mode == "nt":
        b_spec = pl.BlockSpec((tn, tk), lambda i, j, k: (j, k))
    else:
        b_spec = pl.BlockSpec((tk, tn), lambda i, j, k: (k, j))
    blk = pl.BlockSpec((tm, tn), lambda i, j, k: (i, j))
    return _mm_call(name, (M // tm, N // tn, K // tk), a, a_spec, b, b_spec,
                    {"nn": (1, 0), "nt": (1, 1), "tn": (0, 0)}[mode], jax.ShapeDtypeStruct((M, N), out_dtype), blk,
                    (tm, tn), add, blk, **pushed)


FF_SH = 2 * D_FF // N_DEV


def _ffn_up(h2, wup, **pushed):
    S, D = h2.shape
    tm = _tile(S, (1024, 512, 256, 128))
    return _mm_call("matmul_up", (S // tm, N_DEV, 1), h2, pl.BlockSpec((tm, D), lambda m, j, k: (m, 0)),
                    wup, pl.BlockSpec((None, D, FF_SH), lambda m, j, k: (j, 0, 0)), (1, 0),
                    jax.ShapeDtypeStruct((2, N_DEV // 2, S, FF_SH), F32),
                    pl.BlockSpec((None, None, tm, FF_SH), lambda m, j, k: (j // 4, j % 4, m, 0)), (tm, FF_SH), **pushed)


def _ffn_down(act, wd, x1, **pushed):
    _, S, _ = act.shape
    D = wd.shape[1]
    tm, tn = _tile(S, (1024, 512, 256, 128)), _tile(D, (512, 256, 128))
    blk = pl.BlockSpec((tm, tn), lambda m, n, k: (m, n))
    return _mm_call("matmul_down", (S // tm, D // tn, N_DEV // 2), act,
                    pl.BlockSpec((None, tm, FF_SH), lambda m, n, k: (k, m, 0)), wd,
                    pl.BlockSpec((FF_SH, tn), lambda m, n, k: (k, n)), (1, 0), jax.ShapeDtypeStruct((S, D), F32), blk,
                    (tm, tn), x1, blk, **pushed)


def _ffn_dact(dxb, wd):
    S, D = dxb.shape
    tm = _tile(S, (1024, 512, 256, 128))
    return _mm_call("matmul_dact", (S // tm, N_DEV // 2, 1), dxb, pl.BlockSpec((tm, D), lambda m, j, k: (m, 0)), wd,
                    pl.BlockSpec((FF_SH, D), lambda m, j, k: (j, 0)), (1, 1),
                    jax.ShapeDtypeStruct((N_DEV // 2, S, FF_SH), F32),
                    pl.BlockSpec((None, tm, FF_SH), lambda m, j, k: (j, m, 0)), (tm, FF_SH))


def _ffn_dw_down(act, dxb):
    _, S, _ = act.shape
    D = dxb.shape[1]
    tk, tn = _tile(S, (2048, 1024, 512, 256, 128)), _tile(D, (512, 256, 128))
    return _mm_call("matmul_dw_down", (N_DEV // 2, D // tn, S // tk), act,
                    pl.BlockSpec((None, tk, FF_SH), lambda j, n, k: (j, k, 0)), dxb,
                    pl.BlockSpec((tk, tn), lambda j, n, k: (k, n)), (0, 0), jax.ShapeDtypeStruct((D_FF, D), BF),
                    pl.BlockSpec((FF_SH, tn), lambda j, n, k: (j, n)), (FF_SH, tn))


def _ffn_dw_up(h2, du, exchange=None):
    S, D = h2.shape
    tk = _tile(S, (1024, 512, 256, 128))
    return _mm_call("matmul_dw_up", (N_DEV, 1, S // tk), h2, pl.BlockSpec((tk, D), lambda j, n, k: (k, 0)), du,
                    pl.BlockSpec((None, None, tk, FF_SH), lambda j, n, k: (j // 4, j % 4, k, 0)), (0, 0),
                    jax.ShapeDtypeStruct((N_DEV, D, FF_SH), BF),
                    pl.BlockSpec((None, D, FF_SH), lambda j, n, k: (j, 0, 0)), (D, FF_SH), exchange=exchange)


def _ffn_dh2(du, wup, exchange=None):
    S = du.shape[2]
    D = wup.shape[1]
    tm = _tile(S, (1024, 512, 256, 128))
    return _mm_call("matmul_dh2", (S // tm, 1, N_DEV), du,
                    pl.BlockSpec((None, None, tm, FF_SH), lambda m, n, k: (k // 4, k % 4, m, 0)), wup,
                    pl.BlockSpec((None, D, FF_SH), lambda m, n, k: (k, 0, 0)), (1, 1),
                    jax.ShapeDtypeStruct((S, D), F32), pl.BlockSpec((tm, D), lambda m, n, k: (m, 0)), (tm, D),
                    exchange=exchange)


def f_rmsnorm(first, xp, xs, ps, sts):
    return (_rms(xs[0], ps[0]),), ()


def f_xattn(first, xp, xs, ps, sts):
    (xq,), (kv, qn, kn) = xs, ps
    qs = _split(xq, [XA_HD] * XA_HEADS)
    kvs = _split(kv, [XA_HD] * (2 * XA_HEADS))
    outs = []
    for h in range(XA_HEADS):
        q = _rms(qs[h], qn)
        k = _rms(kvs[h], kn)
        v = kvs[XA_HEADS + h]
        s = mm_nt(q, k) * (XA_HD ** -0.5)
        m = lax.stop_gradient(jnp.max(s, axis=-1, keepdims=True))
        p = jnp.exp(s - m)
        outs.append(mm_nn(p / jnp.sum(p, axis=-1, keepdims=True), v))
    return (jnp.concatenate(outs, axis=1),), ()


def _conv(xp, x, w, b, first, taps):
    xp = jnp.where(first, 0.0, xp)
    y = b + w[taps - 1:taps] * x
    for d in range(1, taps):
        y = y + w[taps - 1 - d:taps - d] * _shift(xp, x, d)
    return y


def _unstack2(x):
    @jax.custom_vjp
    def us(x):
        return x[0], x[1]

    us.defvjp(lambda x: (us(x), None), lambda r, g: (jnp.stack(g),))
    return us(x)


def f_ffn_act(first, xp, xs, ps, sts):
    (up,), (u,), (wg, wv, bg, bv) = xp, xs, ps
    (ugp, uvp), (ug, uv) = _unstack2(up), _unstack2(u)
    gate = _conv(ugp, ug, wg, bg, first, FFN_CONV)
    val = _conv(uvp, uv, wv, bv, first, FFN_CONV)
    return (jax.nn.silu(gate) * val,), ()


def _gla_chunk(q, k, v, la, st):
    c = q.shape[0]
    b = cumsum_rows(la)
    b_last = _row(b, c - 1)
    b_ref = _row(b, c // 2 - 1)
    att = mm_nt(q * jnp.exp(b - b_ref), k * jnp.exp(b_ref - b))
    att = jnp.where(_tril(c), att, 0.0)
    o = mm_nn(att, v) + mm_nt(q * jnp.exp(b), st)
    st_new = st * jnp.exp(b_last) + mm_tn(v, k * jnp.exp(b_last - b))
    return o, st_new


def _a_cols(ntot):
    used = D_XA + 2 * GLA_HEADS * GLA_DK + D_MIX + GLA_RANK + D_MIX
    return [D_XA, GLA_HEADS * GLA_DK, GLA_HEADS * GLA_DK, D_MIX, GLA_RANK, D_MIX] + ([ntot - used] if ntot > used else [])


def f_gla(first, xp, xs, ps, sts):
    (p,), (wg2, bg, on) = xs, ps
    parts = _split(p, _a_cols(p.shape[1]))
    q, k, v, glr, og = parts[1:6]
    la = jax.nn.log_sigmoid(mm_nn(glr, wg2) + bg) / GLA_GATE_NORM
    qs = _split(q * (GLA_DK ** -0.5), [GLA_DK] * GLA_HEADS)
    ks = _split(k, [GLA_DK] * GLA_HEADS)
    vs = _split(v, [GLA_DV] * GLA_HEADS)
    las = _split(la, [GLA_DK] * GLA_HEADS)
    outs, new = [], []
    for h in range(GLA_HEADS):
        o, s = _gla_chunk(qs[h], ks[h], vs[h], las[h], sts[h])
        outs.append(_rms(o, on))
        new.append(s)
    return (jnp.concatenate(outs, axis=1) * jax.nn.silu(og),), tuple(new)


def f_hgrn(first, xp, xs, ps, sts):
    (p,), (lbp, on) = xs, ps
    _, q, fgate, iv, og = _split(p, [D_XA, D_MIX, D_MIX, D_MIX, D_MIX])
    e = jnp.exp(lbp - jnp.max(lbp, axis=0, keepdims=True))
    row = lax.broadcasted_iota(jnp.int32, e.shape, 0)
    lb = jnp.sum(jnp.where(row >= 1, e, 0.0), axis=0, keepdims=True) / jnp.sum(e, axis=0, keepdims=True)
    fg = lb + (1.0 - lb) * jax.nn.sigmoid(fgate)
    qs = _split(jax.nn.silu(q), [HGRN_DK] * HGRN_HEADS)
    ks = _split(1.0 - fg, [HGRN_DK] * HGRN_HEADS)
    vs = _split(iv, [HGRN_DV] * HGRN_HEADS)
    las = _split(jnp.log(fg), [HGRN_DK] * HGRN_HEADS)
    outs, new = [], []
    for h in range(HGRN_HEADS):
        o, s = _gla_chunk(qs[h], ks[h], vs[h], las[h], sts[h])
        outs.append(_rms(o, on))
        new.append(s)
    return (jnp.concatenate(outs, axis=1) * jax.nn.sigmoid(og),), tuple(new)


def _c_cols(ntot):
    gn = SSM_GROUPS * SSM_STATE
    used = D_XA + D_MIX + D_MIX + 2 * gn + SSM_HEADS
    return [D_XA, D_MIX, D_MIX + 2 * gn, SSM_HEADS] + ([ntot - used] if ntot > used else [])


def f_ssd(first, xp, xs, ps, sts):
    (pp,), (p,), (cw, cb, dtb, alog, dsk, ng) = xp, xs, ps
    c = p.shape[0]
    gn = SSM_GROUPS * SSM_STATE
    hg = SSM_HEADS // SSM_GROUPS
    _, z, xbc, dtr = _split(p, _c_cols(p.shape[1]))[:4]
    xbc_p = _split(pp, _c_cols(p.shape[1]))[2]
    xbc = jax.nn.silu(_conv(xbc_p, xbc, cw, cb, first, SSM_CONV))
    xs_, bm, cm = _split(xbc, [D_MIX, gn, gn])
    dt = jax.nn.softplus(dtr + dtb)
    a = dt * (-jnp.exp(alog))
    acs = cumsum_rows(a)
    acs_t = cumsum_rows_t(a)
    acs_last = _row(acs, c - 1)
    dt_h = _split(dt, [1] * SSM_HEADS)
    acs_h = _split(acs, [1] * SSM_HEADS)
    al_h = _split(acs_last, [1] * SSM_HEADS)
    d_h = _split(dsk, [1] * SSM_HEADS)
    x2s = _split(xs_, [2 * SSM_HD] * (SSM_HEADS // 2))
    bms = _split(bm, [SSM_STATE] * SSM_GROUPS)
    cms = _split(cm, [SSM_STATE] * SSM_GROUPS)
    tril = _tril(c)
    cbs = [mm_nt(cms[g], bms[g]) for g in range(SSM_GROUPS)]
    ys, new = [], []
    for j in range(SSM_HEADS // 2):
        g = (2 * j) // hg
        h0, h1 = 2 * j, 2 * j + 1
        xdt = x2s[j] * _lane_pair(dt_h[h0], dt_h[h1])
        acs2 = _lane_pair(acs_h[h0], acs_h[h1])
        al2 = _lane_pair(al_h[h0], al_h[h1])
        yd = []
        for h in (h0, h1):
            seg = acs_h[h] - _row(acs_t, h)
            lm = jnp.exp(jnp.where(tril, seg, NEG))
            yd.append(mm_nn(cbs[g] * lm, xdt))
        lane = lax.broadcasted_iota(jnp.int32, xdt.shape, 1)
        y_diag = jnp.where(lane < SSM_HD, yd[0], yd[1])
        y_off = mm_nn(cms[g], sts[j]) * jnp.exp(acs2)
        x_end = xdt * jnp.exp(al2 - acs2)
        new.append(sts[j] * jnp.exp(al2) + mm_tn(bms[g], x_end))
        ys.append(y_diag + y_off + _lane_pair(d_h[h0], d_h[h1]) * x2s[j])
    y = jnp.concatenate(ys, axis=1) * jax.nn.silu(z)
    gw = D_MIX // SSM_GROUPS
    yg = _split(y, [gw] * SSM_GROUPS)
    ngs = _split(ng, [gw] * SSM_GROUPS)
    y = jnp.concatenate([_rms(yg[g], ngs[g]) for g in range(SSM_GROUPS)], axis=1)
    return (y,), tuple(new)


def f_dil_prep(first, xp, xs, ps, sts):
    (p, pos), (qn, kn, invf, sign) = xs, ps
    nh = len(DIL_GROUPS) * DIL_HEADS
    _, q, k, v = _split(p, [D_XA] + [nh * DIL_HD] * 3)
    ang = pos * invf
    cos, sin = jnp.cos(ang), jnp.sin(ang) * sign

    def rope(t, g):
        hs = _split(t, [DIL_HD] * nh)
        out = []
        for h in hs:
            n = _rms(h, g)
            out.append(n * cos + _roll_half(n) * sin)
        return jnp.concatenate(out, axis=1)

    return (rope(q, qn), rope(k, kn), v), ()


def f_dil_attn(first, xp, xs, ps, sts):
    (kp, vp), (q, k, v) = xp, xs
    Q = DIL_BLOCK
    qs, ks, vs = (_split(t, [DIL_HD] * DIL_HEADS) for t in (q, k, v))
    kps, vps = (_split(t, [DIL_HD] * DIL_HEADS) for t in (kp, vp))
    i = lax.broadcasted_iota(jnp.int32, (Q, 2 * Q), 0)
    j = lax.broadcasted_iota(jnp.int32, (Q, 2 * Q), 1)
    dist = Q + i - j
    mask = (dist >= 0) & (dist <= Q) & (jnp.logical_not(first) | (j >= Q))
    outs, lses = [], []
    for h in range(DIL_HEADS):
        k2 = jnp.concatenate([kps[h], ks[h]], axis=0)
        v2 = jnp.concatenate([vps[h], vs[h]], axis=0)
        s = jnp.where(mask, mm_nt(qs[h], k2) * (DIL_HD ** -0.5), NEG)
        m = lax.stop_gradient(jnp.max(s, axis=-1, keepdims=True))
        p = jnp.exp(s - m)
        l = jnp.sum(p, axis=-1, keepdims=True)
        outs.append(mm_nn(p / l, v2))
        lses.append(jnp.broadcast_to(m + jnp.log(l), (Q, DIL_HD)))
    return (jnp.concatenate(outs, axis=1), jnp.concatenate(lses, axis=1)), ()


def f_dil_merge(first, xp, xs, ps, sts):
    o0, o1, o2, l0, l1, l2 = xs
    m = jnp.maximum(jnp.maximum(l0, l1), l2)
    e0, e1, e2 = jnp.exp(l0 - m), jnp.exp(l1 - m), jnp.exp(l2 - m)
    den = e0 + e1 + e2
    return ((e0 * o0 + e1 * o1 + e2 * o2) / den,), ()


def _loss_head(y, target):
    S, D = y.shape
    R = _tile(S, (512, 256, 128, 64, 32, 16, 8))

    def body(y_ref, t_ref, dy_ref, dyb_ref, l_ref):
        e = y_ref[...] - t_ref[...]
        dy_ref[...] = e * (1.0 / D)
        dyb_ref[...] = (e * (1.0 / D)).astype(BF)

        @pl.when(pl.program_id(0) == 0)
        def _():
            l_ref[...] = jnp.zeros_like(l_ref)

        l_ref[...] += jnp.broadcast_to(0.5 * jnp.sum(jnp.mean(e * e, axis=-1, keepdims=True), axis=0, keepdims=True),
                                       l_ref.shape)

    blk = pl.BlockSpec((R, D), lambda i: (i, 0))
    dy, dyb, l = pl.pallas_call(
        body, name="loss_head", grid=(S // R,), in_specs=[blk, blk],
        out_specs=[blk, blk, pl.BlockSpec((8, 128), lambda i: (0, 0))],
        out_shape=[jax.ShapeDtypeStruct((S, D), F32), jax.ShapeDtypeStruct((S, D), BF),
                   jax.ShapeDtypeStruct((8, 128), F32)],
        compiler_params=pltpu.CompilerParams(dimension_semantics=("arbitrary",)))(y, target)
    return dy, dyb, l[0, 0]


def _adamw(parts, w, m, v, name):
    _, n, width = parts.shape
    tr = _tile(n, [t for t in (512, 256, 128, 64, 32, 16, 8) if t * width <= ADAM_BLOCK])

    def body(p_ref, w_ref, m_ref, v_ref, g_ref, d_ref, nm_ref, nv_ref):
        g = p_ref[0].astype(F32)
        for s in range(1, N_DEV):
            g = g + p_ref[s].astype(F32)
        nm = ADAM_B1 * m_ref[...] + (1.0 - ADAM_B1) * g
        nv = ADAM_B2 * v_ref[...] + (1.0 - ADAM_B2) * (g * g)
        m_hat = nm / (1.0 - ADAM_B1 ** ADAM_STEP)
        v_hat = nv / (1.0 - ADAM_B2 ** ADAM_STEP)
        g_ref[...] = g
        d_ref[...] = -ADAM_LR * (m_hat / (jnp.sqrt(v_hat) + ADAM_EPS) + ADAM_WD * w_ref[...])
        nm_ref[...] = nm
        nv_ref[...] = nv

    blk = pl.BlockSpec((tr, width), lambda i: (i, 0))
    return pl.pallas_call(
        body, name=name, grid=(n // tr,),
        in_specs=[pl.BlockSpec((N_DEV, tr, width), lambda i: (0, i, 0)), blk, blk, blk],
        out_specs=[blk] * 4, out_shape=[jax.ShapeDtypeStruct((n, width), F32)] * 4,
        compiler_params=pltpu.CompilerParams(dimension_semantics=("arbitrary",), vmem_limit_bytes=VMEM_LIMIT))(
            parts, w, m, v)


def _peer(k):
    x, y, c = lax.axis_index("x"), lax.axis_index("y"), lax.axis_index("c")
    px = 1 - x if k & 4 else x
    py = 1 - y if k & 2 else y
    pc = 1 - c if k & 1 else c
    return (px, py, pc), 4 * px + 2 * py + pc


def _my_id():
    return 4 * lax.axis_index("x") + 2 * lax.axis_index("y") + lax.axis_index("c")


def _all_gather(x, name):
    def body(x_ref, out_ref, send, recv, loc):
        me = _my_id()
        mine = pltpu.make_async_copy(x_ref, out_ref.at[me], loc)
        mine.start()
        cps = []
        for k in range(1, N_DEV):
            peer, _ = _peer(k)
            cp = pltpu.make_async_remote_copy(src_ref=x_ref, dst_ref=out_ref.at[me], send_sem=send.at[k - 1],
                                              recv_sem=recv.at[k - 1], device_id=peer,
                                              device_id_type=pl.DeviceIdType.MESH)
            cp.start()
            cps.append(cp)
        for k in range(1, N_DEV):
            peer, pid = _peer(k)
            pltpu.make_async_remote_copy(src_ref=x_ref, dst_ref=out_ref.at[pid], send_sem=send.at[k - 1],
                                         recv_sem=recv.at[k - 1], device_id=peer,
                                         device_id_type=pl.DeviceIdType.MESH).wait_recv()
        for cp in cps:
            cp.wait_send()
        mine.wait()

    return pl.pallas_call(
        body, name=name, out_shape=jax.ShapeDtypeStruct((N_DEV,) + x.shape, x.dtype),
        in_specs=[pl.BlockSpec(memory_space=pl.ANY)], out_specs=pl.BlockSpec(memory_space=pl.ANY),
        scratch_shapes=[pltpu.SemaphoreType.DMA((N_DEV - 1,)), pltpu.SemaphoreType.DMA((N_DEV - 1,)),
                        pltpu.SemaphoreType.DMA],
        compiler_params=pltpu.CompilerParams(has_side_effects=True))(x)


def _exchange_sems(n):
    return [pltpu.SemaphoreType.DMA((n * (N_DEV - 1),)), pltpu.SemaphoreType.DMA((n * (N_DEV - 1),)),
            pltpu.SemaphoreType.DMA((n,))]


def _exchange_copies(g_refs, out_refs, send, recv, loc, with_arrivals):
    me = _my_id()

    def mine(g, o, d):
        return g.at[d] if len(g.shape) == len(o.shape) else g

    local = [pltpu.make_async_copy(mine(g, o, me), o.at[me], loc.at[w]) for w, (g, o) in enumerate(zip(g_refs, out_refs))]
    pushes, arrivals = [], []
    for k in range(1, N_DEV):
        peer, pid = _peer(k)
        for w, (g, o) in enumerate(zip(g_refs, out_refs)):
            s = w * (N_DEV - 1) + k - 1
            ends = [(mine(g, o, pid), o.at[me], pushes)] + ([(mine(g, o, me), o.at[pid], arrivals)] if with_arrivals else [])
            for src, dst, into in ends:
                into.append(pltpu.make_async_remote_copy(src_ref=src, dst_ref=dst, send_sem=send.at[s],
                                                         recv_sem=recv.at[s], device_id=peer,
                                                         device_id_type=pl.DeviceIdType.MESH))
    return local, pushes, arrivals


def _exchange_start(g_refs, out_refs, send, recv, loc):
    local, pushes, _ = _exchange_copies(g_refs, out_refs, send, recv, loc, False)
    for cp in local + pushes:
        cp.start()


def _exchange_wait(g_refs, out_refs, send, recv, loc):
    local, pushes, arrivals = _exchange_copies(g_refs, out_refs, send, recv, loc, True)
    for cp in arrivals:
        cp.wait_recv()
    for cp in pushes:
        cp.wait_send()
    for cp in local:
        cp.wait()


def _exchange_many(gs, name):
    n = len(gs)

    def body(*refs):
        g_refs, out_refs, sems = refs[:n], refs[n:2 * n], refs[2 * n:]
        _exchange_start(g_refs, out_refs, *sems)
        _exchange_wait(g_refs, out_refs, *sems)

    return pl.pallas_call(
        body, name=name, out_shape=[jax.ShapeDtypeStruct(g.shape, g.dtype) for g in gs],
        in_specs=[pl.BlockSpec(memory_space=pl.ANY)] * n, out_specs=[pl.BlockSpec(memory_space=pl.ANY)] * n,
        scratch_shapes=_exchange_sems(n), compiler_params=pltpu.CompilerParams(has_side_effects=True))(*gs)


def _gather_many(xs, name):
    n = len(xs)

    def body(*refs):
        x_refs, out_refs, (send, recv, loc) = refs[:n], refs[n:2 * n], refs[2 * n:]
        x, y, c = lax.axis_index("x"), lax.axis_index("y"), lax.axis_index("c")
        me, sibling = (x, y, c), (x, y, 1 - c)
        chips = [(1 - x, y), (x, 1 - y), (1 - x, 1 - y)]

        def slot(p):
            return 4 * p[0] + 2 * p[1] + p[2]

        def copy(w, k, block, to, src=None):
            dst = out_refs[w].at[slot(block)]
            return pltpu.make_async_remote_copy(src_ref=dst if src is None else src, dst_ref=dst,
                                                send_sem=send.at[w * (N_DEV - 1) + k], recv_sem=recv.at[w * (N_DEV - 1) + k],
                                                device_id=to, device_id_type=pl.DeviceIdType.MESH)

        mine = [pltpu.make_async_copy(x_refs[w], out_refs[w].at[slot(me)], loc.at[w]) for w in range(n)]
        for cp in mine:
            cp.start()
        first = []
        for j, chip in enumerate(chips):
            first += [copy(w, 1 + j, me, (*chip, c), src=x_refs[w]) for w in range(n)]
        first += [copy(w, 0, me, sibling, src=x_refs[w]) for w in range(n)]
        for cp in first:
            cp.start()
        passed = []
        for j, chip in enumerate(chips):
            for w in range(n):
                copy(w, 1 + j, (*chip, c), me).wait_recv()
                cp = copy(w, 4 + j, (*chip, c), sibling)
                cp.start()
                passed.append(cp)
        for w in range(n):
            copy(w, 0, sibling, me).wait_recv()
            for j, chip in enumerate(chips):
                copy(w, 4 + j, (*chip, 1 - c), me).wait_recv()
        for cp in first + passed:
            cp.wait_send()
        for cp in mine:
            cp.wait()

    return pl.pallas_call(
        body, name=name, out_shape=[jax.ShapeDtypeStruct((N_DEV,) + x.shape, x.dtype) for x in xs],
        in_specs=[pl.BlockSpec(memory_space=pl.ANY)] * n, out_specs=[pl.BlockSpec(memory_space=pl.ANY)] * n,
        scratch_shapes=[pltpu.SemaphoreType.DMA((n * (N_DEV - 1),)), pltpu.SemaphoreType.DMA((n * (N_DEV - 1),)),
                        pltpu.SemaphoreType.DMA((n,))],
        compiler_params=pltpu.CompilerParams(has_side_effects=True))(*xs)


def _cat_segs(G, ws, n_mix):
    segs = []
    for g in range(G):
        lo, hi = g * ws, (g + 1) * ws
        if lo < n_mix:
            segs.append((g, 0, min(hi, n_mix) - lo, D_XA + lo))
        if hi > n_mix:
            s = max(lo, n_mix)
            segs.append((g, s - lo, hi - s, s - n_mix))
    return segs


def _cat_cols(src, n_mix, ntot):
    G, R, ws = src.shape
    segs = _cat_segs(G, ws, n_mix)
    tr = _tile(R, (256, 128, 64, 32, 16, 8))

    def body(i_ref, o_ref):
        if ntot > G * ws:
            o_ref[...] = jnp.zeros_like(o_ref)
        for g, s, n, d in segs:
            o_ref[:, d:d + n] = i_ref[g][:, s:s + n]

    return pl.pallas_call(
        body, name="cat_cols", grid=(R // tr,), in_specs=[pl.BlockSpec((G, tr, ws), lambda i: (0, i, 0))],
        out_specs=pl.BlockSpec((tr, ntot), lambda i: (i, 0)), out_shape=jax.ShapeDtypeStruct((R, ntot), src.dtype),
        compiler_params=pltpu.CompilerParams(dimension_semantics=("arbitrary",)))(src)


def _uncat_cols(dw, G, ws, n_mix):
    R, ntot = dw.shape
    segs = _cat_segs(G, ws, n_mix)
    tr = _tile(R, (256, 128, 64, 32, 16, 8))

    def body(i_ref, o_ref):
        v = i_ref[...]
        for g, s, n, d in segs:
            o_ref[g, :, s:s + n] = v[:, d:d + n]

    return pl.pallas_call(
        body, name="uncat_cols", grid=(R // tr,), in_specs=[pl.BlockSpec((tr, ntot), lambda i: (i, 0))],
        out_specs=pl.BlockSpec((G, tr, ws), lambda i: (0, i, 0)), out_shape=jax.ShapeDtypeStruct((G, R, ws), dw.dtype),
        compiler_params=pltpu.CompilerParams(dimension_semantics=("arbitrary",)))(dw)


PACK_W = 1024


def _granule(n):
    return (256 if n >= 256 * PACK_W else 8) * PACK_W


def _pack(arrs, dtype):
    flat = jnp.concatenate([a.reshape(-1).astype(dtype) for a in arrs])
    n = flat.shape[0]
    pad = (-n) % _granule(n)
    if pad:
        flat = jnp.concatenate([flat, jnp.zeros((pad,), dtype)])
    return flat.reshape(-1, PACK_W)


def _unpack(packed, shapes):
    flat = packed.reshape(-1)
    out, o = [], 0
    for s in shapes:
        n = math.prod(s)
        out.append(flat[o:o + n].reshape(s))
        o += n
    return out


def _pack_lead(arrs, dtype):
    flat = jnp.concatenate([a.reshape(N_DEV, -1).astype(dtype) for a in arrs], axis=1)
    n = flat.shape[1]
    pad = (-n) % _granule(n)
    if pad:
        flat = jnp.concatenate([flat, jnp.zeros((N_DEV, pad), dtype)], axis=1)
    return flat.reshape(N_DEV, -1, PACK_W)


def _to_full(stacked, axis):
    t = jnp.moveaxis(stacked, 0, axis)
    s = list(t.shape)
    return t.reshape(s[:axis] + [s[axis] * s[axis + 1]] + s[axis + 2:])


def _to_chunks(full, axis):
    s = list(full.shape)
    t = full.reshape(s[:axis] + [N_DEV, s[axis] // N_DEV] + s[axis + 1:])
    return jnp.moveaxis(t, axis, 0)


def _rows_of(S):
    return _tile(S, (512, 256, 128, 64))


def _norm_fwd(x, g, dt=BF):
    (h,), _ = _seq_fwd("rmsnorm_fwd", f_rmsnorm, _rows_of(x.shape[0]), [_row_spec(x)], [_par_spec(g)], [],
                       [_out_spec(x.shape[1], dt=dt)])
    return h


def _norm_bwd(x, g, dh, res=None):
    if res is None:
        (dx,), (dg,) = _seq_bwd("rmsnorm_bwd", f_rmsnorm, _rows_of(x.shape[0]), [_row_spec(x)], [_par_spec(g)], [],
                                [_row_spec(dh)], [])
        return dx, None, dg
    (dx,), (dg,), (dxb,) = _seq_bwd("rmsnorm_res_bwd", f_rmsnorm, _rows_of(x.shape[0]), [_row_spec(x)], [_par_spec(g)], [],
                                    [_row_spec(dh)], [], dx_add={0: res}, dx_bf=True)
    return dx, dxb, dg


def _mixer_specs(kind, S, p, w):
    if kind == 0:
        return (f_gla, CHUNK, [_row_spec(p)],
                [_par_spec(w['a_w_gate2']), _par_spec(w['a_b_gate'].reshape(1, -1)), _par_spec(w['a_o_norm'].reshape(1, -1))],
                [(GLA_DV, GLA_DK)] * GLA_HEADS, D_MIX)
    if kind == 2:
        return (f_ssd, CHUNK, [_row_spec(p, prev='halo')],
                [_par_spec(w['c_conv_w']), _par_spec(w['c_conv_b'].reshape(1, -1)), _par_spec(w['c_dt_bias'].reshape(1, -1)),
                 _par_spec(w['c_a_log'].reshape(1, -1)), _par_spec(w['c_d'].reshape(1, -1)),
                 _par_spec(w['c_norm'].reshape(1, -1))],
                [(SSM_STATE, 2 * SSM_HD)] * (SSM_HEADS // 2), D_MIX)
    return (f_hgrn, CHUNK, [_row_spec(p)],
            [_par_spec(w['d_lower_bounds']), _par_spec(w['d_o_norm'].reshape(1, -1))],
            [(HGRN_DV, HGRN_DK)] * HGRN_HEADS, D_MIX)


def _perm(t, r):
    if r == 1:
        return t
    S, n = t.shape
    return t.reshape(S // r, r, n).transpose(1, 0, 2).reshape(S, n)


def _unperm(t, r):
    if r == 1:
        return t
    S, n = t.shape
    return t.reshape(r, S // r, n).transpose(1, 0, 2).reshape(S, n)


def _rope_consts():
    half = DIL_HD // 2
    inv = ROPE_THETA ** (-jnp.arange(half, dtype=F32) / half)
    invf = jnp.concatenate([inv, inv]).reshape(1, DIL_HD)
    sign = jnp.concatenate([-jnp.ones((half,), F32), jnp.ones((half,), F32)]).reshape(1, DIL_HD)
    return invf, sign


def _dil_fwd(p, pos, w):
    S = p.shape[0]
    invf, sign = _rope_consts()
    prep_rows = [_row_spec(p), _row_spec(pos, diff=False)]
    prep_pars = [_par_spec(w['b_q_norm'].reshape(1, -1)), _par_spec(w['b_k_norm'].reshape(1, -1)),
                 _par_spec(invf, diff=False), _par_spec(sign, diff=False)]
    nqk = len(DIL_GROUPS) * D_DIL
    (qr, kr, v), _ = _seq_fwd("dil_prep_fwd", f_dil_prep, _tile(S, (256, 128)), prep_rows, prep_pars, [],
                              [_out_spec(nqk), _out_spec(nqk), _out_spec(nqk)])
    res = dict(perm=[], o=[], lse=[])
    for g, (window, r) in enumerate(DIL_GROUPS):
        sl = slice(g * D_DIL, (g + 1) * D_DIL)
        qp, kp, vp = _perm(qr[:, sl], r), _perm(kr[:, sl], r), _perm(v[:, sl], r)
        rows = [_row_spec(qp), _row_spec(kp, prev='block'), _row_spec(vp, prev='block')]
        (o, lse), _ = _seq_fwd("dil_attn_fwd", f_dil_attn, DIL_BLOCK, rows, [], [], [_out_spec(D_DIL), _out_spec(D_DIL)],
                               period=S // r // DIL_BLOCK)
        res['perm'].append((qp, kp, vp))
        res['o'].append(_unperm(o, r))
        res['lse'].append(_unperm(lse, r))
    mrows = [_row_spec(t) for t in res['o'] + res['lse']]
    (tok,), _ = _seq_fwd("dil_merge_fwd", f_dil_merge, _rows_of(S), mrows, [], [], [_out_spec(D_DIL)])
    res['prep'] = (prep_rows, prep_pars)
    return tok, res


def _dil_bwd(dtok, res, p):
    S = p.shape[0]
    mrows = [_row_spec(t) for t in res['o'] + res['lse']]
    dm, _ = _seq_bwd("dil_merge_bwd", f_dil_merge, _rows_of(S), mrows, [], [], [dtok], [])
    dq, dk, dv = [], [], []
    for g, (window, r) in enumerate(DIL_GROUPS):
        qp, kp, vp = res['perm'][g]
        rows = [_row_spec(qp), _row_spec(kp, prev='block'), _row_spec(vp, prev='block')]
        douts = [_row_spec(_perm(dm[g], r)), _row_spec(_perm(dm[3 + g], r))]
        (a, b, c), _ = _seq_bwd("dil_attn_bwd", f_dil_attn, DIL_BLOCK, rows, [], [], douts, [],
                                period=S // r // DIL_BLOCK)
        dq.append(_unperm(a, r)); dk.append(_unperm(b, r)); dv.append(_unperm(c, r))
    dqr, dkr, dv = (jnp.concatenate(t, axis=1) for t in (dq, dk, dv))
    prep_rows, prep_pars = res['prep']
    (dp,), (dqn, dkn) = _seq_bwd("dil_prep_bwd", f_dil_prep, _tile(S, (256, 128)), prep_rows, prep_pars, [],
                                 [_row_spec(dqr), _row_spec(dkr), _row_spec(dv)], [], dx_dt=BF)
    return dp, dict(b_q_norm=dqn.reshape(-1), b_k_norm=dkn.reshape(-1))


def _ffn_specs(u, cw, cb):
    half = N_DEV // 2
    rows = [_row_spec(u, prev='halo', lb=(2, None), li=lambda jc: (0, jc))]
    pars = [_par_spec(cw, bs=(None, FFN_CONV, FF_SH), idx=lambda jc: (jc, 0, 0)),
            _par_spec(cw, bs=(None, FFN_CONV, FF_SH), idx=lambda jc: (jc + half, 0, 0)),
            _par_spec(cb, bs=(None, 1, FF_SH), idx=lambda jc: (jc, 0, 0)),
            _par_spec(cb, bs=(None, 1, FF_SH), idx=lambda jc: (jc + half, 0, 0))]
    return half, rows, pars


N_MIX = {0: 2 * GLA_HEADS * GLA_DK + 2 * D_MIX + GLA_RANK, 1: 3 * len(DIL_GROUPS) * D_DIL,
         2: 2 * D_MIX + 2 * SSM_GROUPS * SSM_STATE + SSM_HEADS, 3: 2 * HGRN_HEADS * HGRN_DK + 2 * D_MIX}
W_IN = {0: 'a_w_in', 1: 'b_w_in', 2: 'c_w_in', 3: 'd_w_in'}
W_OUT = {0: 'a_w_out', 1: 'b_w_out', 2: 'c_w_out', 3: 'd_w_out'}


def _in_blocks(name, t):
    return t if SHARD_AXIS[name] == 1 else t.reshape(1, N_DEV * t.shape[1], t.shape[2])


LAYER_STACKED = ('ffn_w_up', 'ffn_conv_w', 'ffn_w_down', 'xa_w_kv')


SMALL_OF_KIND = {0: ['a_w_gate2'], 2: ['c_conv_w']}


def _layer_names(i):
    return list(LAYER_STACKED) + [W_IN[i % 4], W_OUT[i % 4]] + SMALL_OF_KIND.get(i % 4, [])


def _device_step(x, mem, pos, sh, rep, target, distributed=True):
    S, D = x.shape
    w = dict(rep)
    posf = pos.reshape(S, 1).astype(F32)
    n_mix, w_in_name, w_out_name = N_MIX, W_IN, W_OUT
    ntot = {k: -(-(n_mix[k] + D_XA) // 256) * 256 for k in n_mix}
    mem_g = w['mem_norm'].reshape(1, -1)
    mem_n = _norm_fwd(mem, mem_g)
    R = _rows_of(S)

    def mine(i):
        return {n: (sh[n][i] if n in LAYER_STACKED else sh[n]) for n in _layer_names(i)}

    if distributed:
        gl = dict(zip(_layer_names(0), _gather_many(list(mine(0).values()), "gather_weights")))
    else:
        gl = {n: (sh[n][:, 0] if n in LAYER_STACKED else sh[n]) for n in _layer_names(0)}

    saved = []
    for i in range(DEPTH):
        kind = i % 4
        L = dict(x0=x)
        for n in SMALL_OF_KIND.get(kind, []):
            w[n] = _to_full(gl[n], 1)
        in_blocks = _in_blocks(w_in_name[kind], gl[w_in_name[kind]])
        w_out = (_to_full(gl[w_out_name[kind]], 1) if SHARD_AXIS[w_out_name[kind]] == 1
                 else gl[w_out_name[kind]].reshape(-1, D))
        nxt, push = {}, [[], [], []]
        if i + 1 < DEPTH:
            if distributed:
                nxt = mine(i + 1)
                push = [[n for n in nxt if n not in ('ffn_w_up', w_in_name[(i + 1) % 4], w_out_name[(i + 1) % 4])],
                        ['ffn_w_up'], [w_in_name[(i + 1) % 4], w_out_name[(i + 1) % 4]]]
            else:
                nxt = {n: (sh[n][:, i + 1] if n in LAYER_STACKED else sh[n]) for n in _layer_names(i + 1)}
        got = dict(nxt) if not distributed else {}

        def hosted(call, names):
            if not names:
                return call()
            res, arrived = call(gather=[nxt[n] for n in names])
            got.update(zip(names, arrived))
            return res

        g1 = w['mix_norm'][i].reshape(1, -1)
        h = _norm_fwd(x, g1)
        wcat = _cat_cols(in_blocks, n_mix[kind], ntot[kind])
        p = hosted(functools.partial(_matmul, h, wcat, name="matmul_in"), push[0])
        if kind == 1:
            tok, L['dil'] = _dil_fwd(p, posf, w)
        else:
            f, Rm, rows, pars, sshapes, _ = _mixer_specs(kind, S, p, w)
            (tok,), L['states'] = _seq_fwd("mixer%d_fwd" % kind, f, Rm, rows, pars, sshapes, [_out_spec(D_MIX)],
                                           save_states=True)
        wkv = gl['xa_w_kv'].reshape(D, 2 * D_XA)
        kv = _matmul(mem_n, wkv, name="matmul_kv")
        xa_rows = [_row_spec(p, w=D_XA, dn=D_XA)]
        xa_pars = [_par_spec(kv), _par_spec(w['xa_q_norm'][i].reshape(1, -1)), _par_spec(w['xa_k_norm'][i].reshape(1, -1))]
        (xa,), _ = _seq_fwd("xattn_fwd", f_xattn, R, xa_rows, xa_pars, [], [_out_spec(D_XA)])
        cat = jnp.concatenate([tok, xa], axis=1).astype(BF)
        x1 = _matmul(cat, w_out, add=x, name="matmul_out")
        g2 = w['ffn_norm'][i].reshape(1, -1)
        h2 = _norm_fwd(x1, g2)
        wup = gl['ffn_w_up']
        u = hosted(functools.partial(_ffn_up, h2, wup), push[1])
        cw, cb = gl['ffn_conv_w'], w['ffn_conv_b'][i].reshape(N_DEV, 1, FF_SH)
        nt, frows, fpars = _ffn_specs(u, cw, cb)
        (act,), _ = _seq_fwd("ffn_act_fwd", f_ffn_act, R, frows, fpars, [],
                             [_out_spec(FF_SH, dt=BF, ls=(nt,), lb=(None,), li=lambda jc: (jc,))], ncol=nt)
        wd = gl['ffn_w_down'].reshape(D_FF, D)
        x = hosted(functools.partial(_ffn_down, act, wd, x1), push[2])
        L.update(h=h, p=p, wcat=wcat, kv=kv, wkv=wkv, cat=cat, x1=x1, h2=h2, u=u, act=act, wd=wd, wup=wup, cw=cw, g1=g1,
                 g2=g2, in_blocks=in_blocks, w_out=w_out, shapes={n: t.shape for n, t in gl.items()})
        saved.append(L)
        gl = got

    dx, dxb, loss = _loss_head(x, target)

    G = {}
    d_mem_n = None
    acc = {k: [None] * DEPTH for k in ('mix_norm', 'ffn_norm', 'ffn_conv_b', 'xa_q_norm', 'xa_k_norm')}
    parts = [{} for _ in range(DEPTH)]
    pending = {}
    half = N_DEV // 2

    def sent(call, blocks, layer):
        if not blocks:
            return call()
        if not distributed:
            parts[layer].update(blocks)
            return call()
        res, arrived = call(exchange=list(blocks.values()))
        parts[layer].update(zip(blocks, arrived))
        return res

    for i in reversed(range(DEPTH)):
        kind = i % 4
        L = saved[i]
        Gc = {}
        Gc['ffn_w_down'] = _ffn_dw_down(L['act'], dxb).reshape(N_DEV, D_FF // N_DEV, D)
        dact = _ffn_dact(dxb, L['wd'])
        cw, cb = L['cw'], w['ffn_conv_b'][i].reshape(N_DEV, 1, FF_SH)
        nt, frows, fpars = _ffn_specs(L['u'], cw, cb)
        (du,), (dwg, dwv, dbg, dbv) = _seq_bwd(
            "ffn_act_bwd", f_ffn_act, R, frows, fpars, [], [_row_spec(dact, lb=(None,), li=lambda jc: (jc,))], [],
            ncol=nt, dx_dt=BF)
        Gc['ffn_conv_w'] = jnp.concatenate([dwg[:half], dwv[half:]], axis=0)
        acc['ffn_conv_b'][i] = jnp.concatenate([dbg[:half], dbv[half:]], axis=0).reshape(-1)
        Gc['ffn_w_up'] = _ffn_dw_up(L['h2'], du)
        dh2 = sent(functools.partial(_ffn_dh2, du, L['wup']), pending, i + 1)
        dx1, dx1b, dg2 = _norm_bwd(L['x1'], L['g2'], dh2, res=dx)
        acc['ffn_norm'][i] = dg2.reshape(-1)
        G_out = _matmul(L['cat'], dx1b, mode="tn", out_dtype=BF, name="matmul_dw_out")
        dcat = _matmul(dx1b, L['w_out'], mode="nt", name="matmul_dcat")
        ntok = D_DIL if kind == 1 else D_MIX
        dtok = _row_spec(dcat, w=ntok)
        dxa = _row_spec(dcat, w=D_XA, c=lambda jc: ntok // D_XA)
        p = L['p']
        if kind == 1:
            dp, gm = _dil_bwd(dtok, L['dil'], p)
            G.update(gm)
        else:
            f, Rm, rows, pars, sshapes, _ = _mixer_specs(kind, S, p, w)
            (dp,), dps = _seq_bwd("mixer%d_bwd" % kind, f, Rm, rows, pars, sshapes, [dtok], L['states'], dx_dt=BF)
            if kind == 0:
                Gc['a_w_gate2'], G['a_b_gate'], G['a_o_norm'] = _to_chunks(dps[0], 1), dps[1].reshape(-1), dps[2].reshape(-1)
            elif kind == 2:
                Gc['c_conv_w'] = _to_chunks(dps[0], 1)
                for nme, v in zip(('c_conv_b', 'c_dt_bias', 'c_a_log', 'c_d', 'c_norm'), dps[1:]):
                    G[nme] = v.reshape(-1)
            else:
                G['d_lower_bounds'], G['d_o_norm'] = dps[0], dps[1].reshape(-1)
        xa_rows = [_row_spec(p, w=D_XA)]
        xa_pars = [_par_spec(L['kv']), _par_spec(w['xa_q_norm'][i].reshape(1, -1)), _par_spec(w['xa_k_norm'][i].reshape(1, -1))]
        (dp,), (dkv, dqn, dkn) = _seq_bwd("xattn_bwd", f_xattn, R, xa_rows, xa_pars, [], [dxa], [], dx_dt=BF,
                                          dx_alias={0: dp})
        acc['xa_q_norm'][i], acc['xa_k_norm'][i] = dqn.reshape(-1), dkn.reshape(-1)
        Gc['xa_w_kv'] = _matmul(mem_n, dkv, mode="tn", out_dtype=BF, name="matmul_dw_kv").reshape(
            N_DEV, D // N_DEV, 2 * D_XA)
        d_mem_n = _matmul(dkv, L['wkv'], mode="nt", add=d_mem_n, name="matmul_dmem" + ("" if d_mem_n is None else "_acc"))
        dwcat = sent(functools.partial(_matmul, L['h'], dp, mode="tn", out_dtype=BF, name="matmul_dw_in"),
                     {'ffn_w_up': Gc.pop('ffn_w_up')}, i)
        blocks = L['in_blocks']
        Gc[w_in_name[kind]] = _uncat_cols(dwcat, blocks.shape[0], blocks.shape[2], n_mix[kind]).reshape(
            L['shapes'][w_in_name[kind]])
        Gc[w_out_name[kind]] = (_to_chunks(G_out, 1) if SHARD_AXIS[w_out_name[kind]] == 1
                                else G_out.reshape(L['shapes'][w_out_name[kind]]))
        dh = sent(functools.partial(_matmul, dp, L['wcat'], mode="nt", name="matmul_dh"),
                  {n: Gc.pop(n) for n in ('ffn_w_down', 'ffn_conv_w')}, i)
        dx, dxb, dg1 = _norm_bwd(L['x0'], L['g1'], dh, res=dx1)
        acc['mix_norm'][i] = dg1.reshape(-1)
        pending = Gc

    if distributed:
        names = list(pending)
        parts[0].update(zip(names, _exchange_many([pending[n] for n in names], "exchange_grads")))
    else:
        parts[0].update(pending)
    _, _, dmg = _norm_bwd(mem, mem_g, d_mem_n)
    G['mem_norm'] = dmg.reshape(-1)
    for k, v in acc.items():
        G[k] = jnp.stack(v)
    got = {}
    for i in range(DEPTH):
        for n, t in parts[i].items():
            if n not in LAYER_STACKED:
                got[n] = t
    for n in LAYER_STACKED:
        got[n] = jnp.stack([parts[i][n] for i in range(DEPTH)], axis=1)
    return loss, dx, got, G


def kernel(x, mem, positions, mem_norm, mix_norm, xa_w_kv, xa_q_norm, xa_k_norm, ffn_norm, ffn_w_up, ffn_conv_w, ffn_conv_b, ffn_w_down, a_w_in, a_w_gate2, a_b_gate, a_o_norm, a_w_out, b_w_in, b_q_norm, b_k_norm, b_w_out, c_w_in, c_conv_w, c_conv_b, c_dt_bias, c_a_log, c_d, c_norm, c_w_out, d_w_in, d_lower_bounds, d_o_norm, d_w_out, loss_target, m_mem_norm, m_mix_norm, m_xa_w_kv, m_xa_q_norm, m_xa_k_norm, m_ffn_norm, m_ffn_w_up, m_ffn_conv_w, m_ffn_conv_b, m_ffn_w_down, m_a_w_in, m_a_w_gate2, m_a_b_gate, m_a_o_norm, m_a_w_out, m_b_w_in, m_b_q_norm, m_b_k_norm, m_b_w_out, m_c_w_in, m_c_conv_w, m_c_conv_b, m_c_dt_bias, m_c_a_log, m_c_d, m_c_norm, m_c_w_out, m_d_w_in, m_d_lower_bounds, m_d_o_norm, m_d_w_out, v_mem_norm, v_mix_norm, v_xa_w_kv, v_xa_q_norm, v_xa_k_norm, v_ffn_norm, v_ffn_w_up, v_ffn_conv_w, v_ffn_conv_b, v_ffn_w_down, v_a_w_in, v_a_w_gate2, v_a_b_gate, v_a_o_norm, v_a_w_out, v_b_w_in, v_b_q_norm, v_b_k_norm, v_b_w_out, v_c_w_in, v_c_conv_w, v_c_conv_b, v_c_dt_bias, v_c_a_log, v_c_d, v_c_norm, v_c_w_out, v_d_w_in, v_d_lower_bounds, v_d_o_norm, v_d_w_out):
    args = locals()
    w = {n: args[n] for n in WEIGHTS}
    m = {n: args['m_' + n] for n in WEIGHTS}
    v = {n: args['v_' + n] for n in WEIGHTS}

    big = [n for n in SHARDED if w[n].size >= 65536]
    small = [n for n in SHARDED if n not in big]
    sh = {n: (w[n].astype(BF) if n in big else w[n]) for n in SHARDED}
    loss, grad_x, parts, G = _device_step(x[0], mem[0], positions[0], sh, {n: w[n] for n in REPLICATED}, loss_target[0])
    loss = lax.psum(loss, ("x", "y", "c"))
    rep_parts = _all_gather(_pack([G[n] for n in REPLICATED], F32), "gather_replicated_grads")

    out = {}

    def put(names, res, shapes):
        for kind, r in zip(("grad", "delta", "new_m", "new_v"), res):
            for n, t in zip(names, _unpack(r, shapes)):
                out[kind + "_" + n] = t

    for n in big:
        shp = tuple(w[n].shape)
        two_d = (math.prod(shp[:-1]), shp[-1])
        res = _adamw(parts[n].reshape((N_DEV,) + two_d), w[n].reshape(two_d), m[n].reshape(two_d), v[n].reshape(two_d),
                     "adamw")
        for kind, r in zip(("grad", "delta", "new_m", "new_v"), res):
            out[kind + "_" + n] = r.reshape(shp)
    for names, prt, tag in ((small, _pack_lead([parts[n] for n in small], F32), "adamw_small"),
                            (REPLICATED, rep_parts, "adamw_replicated")):
        res = _adamw(prt, _pack([w[n] for n in names], F32), _pack([m[n] for n in names], F32),
                     _pack([v[n] for n in names], F32), tag)
        put(names, res, [tuple(w[n].shape) for n in names])
    return (loss, grad_x[None], *[out["grad_" + n] for n in WEIGHTS], *[out["delta_" + n] for n in WEIGHTS],
            *[out["new_m_" + n] for n in WEIGHTS], *[out["new_v_" + n] for n in WEIGHTS])
```

```python
import functools
import math

import jax
import jax.numpy as jnp
from jax import lax
from jax.experimental import pallas as pl
from jax.experimental.pallas import tpu as pltpu

F32 = jnp.float32
BF = jnp.bfloat16
_MM_DTYPE = BF

N_DEV = 8
EPS = 1e-6
ROPE_THETA = 10000.0
CHUNK = 64
MIX_ROWS = 256
D_MIX = 768
XA_HEADS, XA_HD, D_XA = 4, 64, 256
GLA_HEADS, GLA_DK, GLA_DV, GLA_RANK, GLA_GATE_NORM = 4, 96, 192, 16, 16.0
DIL_GROUPS = ((128, 1), (512, 4), (2048, 16))
DIL_HEADS, DIL_HD, DIL_BLOCK, D_DIL = 4, 128, 128, 512
SSM_HD, SSM_HEADS, SSM_GROUPS, SSM_STATE, SSM_CONV = 64, 12, 2, 128, 4
HGRN_HEADS, HGRN_DK, HGRN_DV = 6, 128, 128
D_FF = 2816
FFN_CONV = 3
DEPTH = 4
ADAM_LR, ADAM_B1, ADAM_B2, ADAM_EPS, ADAM_WD, ADAM_STEP = 0.001, 0.9, 0.999, 1e-08, 0.01, 10
NEG = -1e30
HALO = 8
VMEM_LIMIT = 56 << 20
ADAM_BLOCK = 1 << 18

WEIGHTS = ['mem_norm', 'mix_norm', 'xa_w_kv', 'xa_q_norm', 'xa_k_norm', 'ffn_norm', 'ffn_w_up', 'ffn_conv_w',
           'ffn_conv_b', 'ffn_w_down', 'a_w_in', 'a_w_gate2', 'a_b_gate', 'a_o_norm', 'a_w_out', 'b_w_in', 'b_q_norm',
           'b_k_norm', 'b_w_out', 'c_w_in', 'c_conv_w', 'c_conv_b', 'c_dt_bias', 'c_a_log', 'c_d', 'c_norm', 'c_w_out',
           'd_w_in', 'd_lower_bounds', 'd_o_norm', 'd_w_out']
SHARD_AXIS = {'xa_w_kv': 1, 'ffn_w_up': 2, 'ffn_conv_w': 2, 'ffn_w_down': 1, 'a_w_in': 1, 'a_w_gate2': 1, 'a_w_out': 0,
              'b_w_in': 1, 'b_w_out': 1, 'c_w_in': 0, 'c_conv_w': 1, 'c_w_out': 0, 'd_w_in': 1, 'd_w_out': 0}
SHARDED = [n for n in WEIGHTS if n in SHARD_AXIS]
REPLICATED = [n for n in WEIGHTS if n not in SHARD_AXIS]


def _dot(a, b, ca, cb):
    return lax.dot_general(a.astype(_MM_DTYPE), b.astype(_MM_DTYPE), (((ca,), (cb,)), ((), ())),
                           preferred_element_type=F32)


@jax.custom_vjp
def mm_nn(a, b):
    return _dot(a, b, 1, 0)


mm_nn.defvjp(lambda a, b: (_dot(a, b, 1, 0), (a, b)),
             lambda r, g: (_dot(g, r[1], 1, 1), _dot(r[0], g, 0, 0)))


@jax.custom_vjp
def mm_nt(a, b):
    return _dot(a, b, 1, 1)


mm_nt.defvjp(lambda a, b: (_dot(a, b, 1, 1), (a, b)),
             lambda r, g: (_dot(g, r[1], 1, 0), _dot(g, r[0], 0, 0)))


@jax.custom_vjp
def mm_tn(a, b):
    return _dot(a, b, 0, 0)


mm_tn.defvjp(lambda a, b: (_dot(a, b, 0, 0), (a, b)),
             lambda r, g: (_dot(r[1], g, 1, 1), _dot(r[0], g, 1, 0)))


def _dot_hi(a, b, ca, cb):
    return lax.dot_general(a, b, (((ca,), (cb,)), ((), ())), precision=lax.Precision.HIGHEST,
                           preferred_element_type=F32)


def _tril(c):
    return lax.broadcasted_iota(jnp.int32, (c, c), 0) >= lax.broadcasted_iota(jnp.int32, (c, c), 1)


@jax.custom_vjp
def cumsum_rows(x):
    return _dot_hi(_tril(x.shape[0]).astype(F32), x, 1, 0)


cumsum_rows.defvjp(lambda x: (cumsum_rows(x), None),
                   lambda r, g: (_dot_hi(_tril(g.shape[0]).astype(F32), g, 0, 0),))


@jax.custom_vjp
def cumsum_rows_t(x):
    return _dot_hi(x, _tril(x.shape[0]).astype(F32), 0, 1)


cumsum_rows_t.defvjp(lambda x: (cumsum_rows_t(x), None),
                     lambda r, g: (_dot_hi(_tril(g.shape[1]).astype(F32), g, 0, 1),))


def _split(x, sizes):
    sizes = tuple(int(s) for s in sizes)
    assert sum(sizes) == x.shape[-1], (sizes, x.shape)

    @jax.custom_vjp
    def sp(x):
        out, o = [], 0
        for s in sizes:
            out.append(x[:, o:o + s])
            o += s
        return tuple(out)

    sp.defvjp(lambda x: (sp(x), None), lambda r, g: (jnp.concatenate(list(g), axis=1),))
    return sp(x)


def _row(x, r):
    m = lax.broadcasted_iota(jnp.int32, x.shape, 0) == r
    return jnp.sum(jnp.where(m, x, 0.0), axis=0, keepdims=True)


@jax.custom_vjp
def _roll_half(x):
    return pltpu.roll(x, 64, 1)


_roll_half.defvjp(lambda x: (pltpu.roll(x, 64, 1), None), lambda r, g: (pltpu.roll(g, 64, 1),))


def _shift(xp, x, d):
    if d == 0:
        return x
    n, m = x.shape[0], xp.shape[0]
    assert d <= m == HALO

    @jax.custom_vjp
    def sh(xp, x):
        r = pltpu.roll(x, d, 0)
        row = lax.broadcasted_iota(jnp.int32, xp.shape, 0)
        head = jnp.where(row < d, pltpu.roll(xp, d, 0), r[:m])
        return jnp.concatenate([head, r[m:]], axis=0)

    def bwd(_, g):
        row = lax.broadcasted_iota(jnp.int32, g.shape, 0)
        rowp = lax.broadcasted_iota(jnp.int32, (m,) + g.shape[1:], 0)
        dxp = jnp.where(rowp >= m - d, pltpu.roll(g[:m], m - d, 0), 0.0)
        return dxp, jnp.where(row < n - d, pltpu.roll(g, n - d, 0), 0.0)

    sh.defvjp(lambda xp, x: (sh(xp, x), None), bwd)
    return sh(xp, x)


def _rms(x, g):
    return x * lax.rsqrt(jnp.mean(x * x, axis=-1, keepdims=True) + EPS) * g


def _lane_pair(a, b, width=128):
    shape = a.shape[:-1] + (width,)
    lane = lax.broadcasted_iota(jnp.int32, shape, len(shape) - 1)
    return jnp.where(lane < width // 2, a, b)


def _row_spec(a, w=None, c=None, prev=False, diff=True, dn=None, lb=(), li=None):
    return dict(a=a, w=a.shape[-1] if w is None else w, c=(lambda jc: 0) if c is None else c, prev=prev, diff=diff,
                dn=a.shape[-1] if dn is None else dn, lb=tuple(lb), li=(lambda jc: ()) if li is None else li)


def _par_spec(a, bs=None, idx=None, diff=True):
    nd = a.ndim
    return dict(a=a, bs=tuple(a.shape) if bs is None else tuple(bs),
                idx=(lambda jc: (0,) * nd) if idx is None else idx, diff=diff)


def _out_spec(n, w=None, c=None, dt=F32, ls=(), lb=(), li=None):
    return dict(n=n, w=n if w is None else w, c=(lambda jc: 0) if c is None else c, dt=dt, ls=tuple(ls), lb=tuple(lb),
                li=(lambda jc: ()) if li is None else li)


def _cparams():
    return pltpu.CompilerParams(dimension_semantics=("arbitrary", "arbitrary"), vmem_limit_bytes=VMEM_LIMIT)


def _bspec(s, R, rowfn):
    return pl.BlockSpec(s['lb'] + (R, s['w']),
                        functools.partial(lambda jc, i, s: tuple(s['li'](jc)) + (rowfn(i), s['c'](jc)), s=s))


def _prev_rows(s, R):
    return R if s['prev'] == 'block' else HALO


def _pspec(s, R, blockfn):
    pr = _prev_rows(s, R)
    return pl.BlockSpec(s['lb'] + (pr, s['w']), functools.partial(
        lambda jc, i, s: tuple(s['li'](jc)) + (jnp.maximum(blockfn(i) * (R // pr) - 1, 0), s['c'](jc)), s=s))


def _seq_fwd(name, f, R, rows, params, state_shapes, outs, *, ncol=1, period=None, save_states=False, out_alias=None):
    nrows = rows[0]['a'].shape[-2]
    nb = nrows // R
    assert nb * R == nrows
    period = nb if period is None else period
    prev_ids = [k for k, r in enumerate(rows) if r['prev']]
    n_rows, n_prev, n_par, n_out, n_st = len(rows), len(prev_ids), len(params), len(outs), len(state_shapes)

    def body(*refs):
        o = 0
        cur = refs[o:o + n_rows]; o += n_rows
        prv = refs[o:o + n_prev]; o += n_prev
        par = refs[o:o + n_par]; o += n_par + len(out_alias or {})
        out = refs[o:o + n_out]; o += n_out
        sav = refs[o:o + (n_st if save_states else 0)]; o += len(sav)
        st = refs[o:o + n_st]
        i = pl.program_id(1)
        first = (i % period) == 0

        @pl.when(i == 0)
        def _():
            for s in st:
                s[...] = jnp.zeros_like(s)

        xs = [r[...].astype(F32) for r in cur]
        xp = [r[...].astype(F32) for r in prv]
        ps = [r[...] for r in par]
        sts = [s[...] for s in st]
        for sv, s in zip(sav, sts):
            sv[0] = s
        ov, ns = f(first, xp, xs, ps, sts)
        for r, v in zip(out, ov):
            r[...] = v.astype(r.dtype)
        for s, v in zip(st, ns):
            s[...] = v

    in_specs = [_bspec(r, R, lambda i: i) for r in rows]
    in_specs += [_pspec(rows[k], R, lambda i: i) for k in prev_ids]
    in_specs += [pl.BlockSpec(p['bs'], functools.partial(lambda jc, i, idx: idx(jc), idx=p['idx'])) for p in params]
    out_specs = [_bspec(o_, R, lambda i: i) for o_ in outs]
    out_shape = [jax.ShapeDtypeStruct(o_['ls'] + (nrows, o_['n']), o_['dt']) for o_ in outs]
    if save_states:
        for s in state_shapes:
            out_specs.append(pl.BlockSpec((1,) + tuple(s), lambda jc, i, nd=len(s): (i,) + (0,) * nd))
            out_shape.append(jax.ShapeDtypeStruct((nb,) + tuple(s), F32))
    args = [r['a'] for r in rows] + [rows[k]['a'] for k in prev_ids] + [p['a'] for p in params]
    aliases = {}
    for n_, arr in sorted((out_alias or {}).items()):
        assert arr.shape == out_shape[n_].shape and arr.dtype == out_shape[n_].dtype
        aliases[len(args)] = n_
        args.append(arr)
        in_specs.append(pl.BlockSpec(memory_space=pl.ANY))
    res = pl.pallas_call(
        body, name=name, grid=(ncol, nb), in_specs=in_specs, out_specs=out_specs, out_shape=out_shape,
        scratch_shapes=[pltpu.VMEM(tuple(s), F32) for s in state_shapes], input_output_aliases=aliases,
        compiler_params=_cparams())(*args)
    return list(res[:n_out]), list(res[n_out:])


def _seq_bwd(name, f, R, rows, params, state_shapes, douts, saved, *, ncol=1, period=None, dx_dt=F32, dx_add=None,
             dx_bf=False, dx_alias=None):
    nrows = rows[0]['a'].shape[-2]
    nb = nrows // R
    period = nb if period is None else period
    prev_ids = [k for k, r in enumerate(rows) if r['prev']]
    drow_ids = [k for k, r in enumerate(rows) if r['diff']]
    dpar_ids = [k for k, p in enumerate(params) if p['diff']]
    for k in prev_ids:
        assert rows[k]['diff']
    dx_add, dx_alias = dict(dx_add or {}), dict(dx_alias or {})
    add_ids, alias_ids = sorted(dx_add), sorted(dx_alias)
    n_rows, n_prev, n_par, n_do, n_st = len(rows), len(prev_ids), len(params), len(douts), len(state_shapes)
    n_dx, n_dp, n_add, n_al = len(drow_ids), len(dpar_ids), len(add_ids), len(alias_ids)

    def body(*refs):
        o = 0
        cur = refs[o:o + n_rows]; o += n_rows
        prv = refs[o:o + n_prev]; o += n_prev
        par = refs[o:o + n_par]; o += n_par
        sav = refs[o:o + n_st]; o += n_st
        dou = refs[o:o + n_do]; o += n_do
        adr = refs[o:o + n_add]; o += n_add
        o += n_al
        dxr = refs[o:o + n_dx]; o += n_dx
        dpr = refs[o:o + n_dp]; o += n_dp
        dxb = refs[o:o + (n_dx if dx_bf else 0)]; o += len(dxb)
        dst = refs[o:o + n_st]; o += n_st
        car = refs[o:o + n_prev]
        j = pl.program_id(1)
        i = nb - 1 - j
        first = (i % period) == 0

        @pl.when(j == 0)
        def _():
            for s in tuple(dst) + tuple(car) + tuple(dpr):
                s[...] = jnp.zeros_like(s)

        xs = [r[...].astype(F32) for r in cur]
        xp = [r[...].astype(F32) for r in prv]
        ps = [r[...] for r in par]
        sts = [s[0] for s in sav]

        def g(dxs, dxp, dps, dsts):
            xs_, ps_ = list(xs), list(ps)
            for k, v in zip(drow_ids, dxs):
                xs_[k] = v
            for k, v in zip(dpar_ids, dps):
                ps_[k] = v
            ov, ns = f(first, list(dxp), xs_, ps_, list(dsts))
            return tuple(ov), tuple(ns)

        _, vjp = jax.vjp(g, tuple(xs[k] for k in drow_ids), tuple(xp), tuple(ps[k] for k in dpar_ids), tuple(sts))
        dxs, dxp, dps, dsts = vjp((tuple(r[...].astype(F32) for r in dou), tuple(s[...] for s in dst)))
        dxs = list(dxs)
        for n_, pos in enumerate(add_ids):
            dxs[pos] = dxs[pos] + adr[n_][...].astype(F32)
        tails = {}
        for n_, k in enumerate(prev_ids):
            pos = drow_ids.index(k)
            if rows[k]['prev'] == 'block':
                dxs[pos] = dxs[pos] + car[n_][...]
            else:
                tails[pos] = car[n_][...]
            car[n_][...] = dxp[n_]
        for pos, v in enumerate(dxs):
            outs_ = [dxr[pos]] + ([dxb[pos]] if dx_bf else [])
            if pos in tails:
                v = jnp.concatenate([v[..., :R - HALO, :], v[..., R - HALO:, :] + tails[pos]], axis=-2)
            for r in outs_:
                r[...] = v.astype(r.dtype)
        for r, v in zip(dpr, dps):
            r[...] += v
        for s, v in zip(dst, dsts):
            s[...] = v

    def rev(j):
        return nb - 1 - j

    def dspec(k):
        return _bspec(rows[k], R, rev)

    in_specs = [_bspec(r, R, rev) for r in rows]
    in_specs += [_pspec(rows[k], R, rev) for k in prev_ids]
    in_specs += [pl.BlockSpec(p['bs'], functools.partial(lambda jc, j, idx: idx(jc), idx=p['idx'])) for p in params]
    in_specs += [pl.BlockSpec((1,) + tuple(s), lambda jc, j, nd=len(s): (nb - 1 - j,) + (0,) * nd) for s in state_shapes]
    in_specs += [_bspec(d, R, rev) for d in douts]
    in_specs += [dspec(drow_ids[pos]) for pos in add_ids]
    in_specs += [pl.BlockSpec(memory_space=pl.ANY) for _ in alias_ids]
    out_specs = [dspec(k) for k in drow_ids]
    out_shape = [jax.ShapeDtypeStruct(tuple(rows[k]['a'].shape[:-1]) + (rows[k]['dn'],), dx_dt) for k in drow_ids]
    for k in dpar_ids:
        p = params[k]
        out_specs.append(pl.BlockSpec(p['bs'], functools.partial(lambda jc, j, idx: idx(jc), idx=p['idx'])))
        out_shape.append(jax.ShapeDtypeStruct(p['a'].shape, F32))
    if dx_bf:
        out_specs += [dspec(k) for k in drow_ids]
        out_shape += [jax.ShapeDtypeStruct(tuple(rows[k]['a'].shape[:-1]) + (rows[k]['dn'],), BF) for k in drow_ids]
    scratch = [pltpu.VMEM(tuple(s), F32) for s in state_shapes]
    scratch += [pltpu.VMEM(tuple(d for d in rows[k]['lb'] if d is not None) + (_prev_rows(rows[k], R), rows[k]['w']), F32)
                for k in prev_ids]
    args = ([r['a'] for r in rows] + [rows[k]['a'] for k in prev_ids] + [p['a'] for p in params] + list(saved)
            + [d['a'] for d in douts] + [dx_add[pos] for pos in add_ids] + [dx_alias[pos] for pos in alias_ids])
    n_in = len(args)
    aliases = {n_in - n_al + n_: pos for n_, pos in enumerate(alias_ids)}
    for pos in alias_ids:
        assert dx_alias[pos].shape == out_shape[pos].shape and dx_alias[pos].dtype == out_shape[pos].dtype
    res = pl.pallas_call(
        body, name=name, grid=(ncol, nb), in_specs=in_specs, out_specs=out_specs, out_shape=out_shape,
        scratch_shapes=scratch, input_output_aliases=aliases, compiler_params=_cparams())(*args)
    if dx_bf:
        return list(res[:n_dx]), list(res[n_dx:n_dx + n_dp]), list(res[n_dx + n_dp:])
    return list(res[:n_dx]), list(res[n_dx:])


def _tile(n, cands):
    for c in cands:
        if n % c == 0:
            return c
    return n


def _mm_call(name, grid, a, a_spec, b, b_spec, contract, out_shape, out_spec, acc_shape, add=None, add_spec=None,
             exchange=None, gather=None):
    nk = grid[2]
    ca, cb = contract
    has_add = add is not None
    ex = list(exchange or []) + list(gather or [])
    ex_shapes = [g.shape for g in exchange or []] + [(N_DEV,) + tuple(g.shape) for g in gather or []]
    n_ex = len(ex)
    n_in = 2 + has_add

    def body(*refs):
        a_ref, b_ref = refs[0], refs[1]
        add_ref = refs[2] if has_add else None
        o_ref = refs[n_in + n_ex]
        scr = refs[n_in + 2 * n_ex + 1:]
        step = [pl.program_id(d) for d in range(3)]
        if n_ex:
            g_refs, r_refs, sems = refs[n_in:n_in + n_ex], refs[n_in + n_ex + 1:n_in + 2 * n_ex + 1], scr[-3:]

            @pl.when((step[0] == 0) & (step[1] == 0) & (step[2] == 0))
            def _():
                _exchange_start(g_refs, r_refs, *sems)

        part = _dot(a_ref[...], b_ref[...], ca, cb)

        def finish(r):
            if has_add:
                r = r + add_ref[...].astype(F32)
            o_ref[...] = r.astype(o_ref.dtype)

        if nk == 1:
            finish(part)
        else:
            acc = scr[0]

            @pl.when(step[2] == 0)
            def _():
                acc[...] = part

            @pl.when(step[2] > 0)
            def _():
                acc[...] += part

            @pl.when(step[2] == nk - 1)
            def _():
                finish(acc[...])

        if n_ex:
            @pl.when((step[0] == grid[0] - 1) & (step[1] == grid[1] - 1) & (step[2] == grid[2] - 1))
            def _():
                _exchange_wait(g_refs, r_refs, *sems)

    in_specs, args = [a_spec, b_spec], [a, b]
    if has_add:
        in_specs.append(add_spec)
        args.append(add)
    any_spec = pl.BlockSpec(memory_space=pl.ANY)
    scratch = [] if nk == 1 else [pltpu.VMEM(acc_shape, F32)]
    if n_ex:
        scratch += _exchange_sems(n_ex)
    res = pl.pallas_call(
        body, name=name, grid=grid, in_specs=in_specs + [any_spec] * n_ex, out_specs=[out_spec] + [any_spec] * n_ex,
        out_shape=[out_shape] + [jax.ShapeDtypeStruct(s, g.dtype) for s, g in zip(ex_shapes, ex)], scratch_shapes=scratch,
        compiler_params=pltpu.CompilerParams(
            dimension_semantics=("arbitrary",) * 3 if n_ex else ("parallel", "parallel", "arbitrary"),
            vmem_limit_bytes=VMEM_LIMIT, has_side_effects=bool(n_ex)))(*args, *ex)
    return (res[0], list(res[1:])) if n_ex else res[0]


def _matmul(a, b, mode="nn", add=None, out_dtype=F32, name="matmul", **pushed):
    if mode == "nn":
        (M, K), N = a.shape, b.shape[1]
    elif mode == "nt":
        (M, K), N = a.shape, b.shape[0]
    else:
        (K, M), N = a.shape, b.shape[1]
    if mode == "tn":
        tm = _tile(M, (256, 128, 64, 32, 16, 8))
        tn = N if N <= 5120 else _tile(N, (512, 256, 128))
        tk = _tile(K, (2048 if tn <= 3072 else 1024, 1024, 512, 256, 128))
    else:
        tm = _tile(M, (1024, 512, 256, 128, 64, 32, 16, 8))
        tn = _tile(N, (512, 256, 128))
        tk = K if K <= 5120 else _tile(K, (2048, 1024, 512, 256, 128))
    if mode == "tn":
        a_spec = pl.BlockSpec((tk, tm), lambda i, j, k: (k, i))
    else:
        a_spec = pl.BlockSpec((tm, tk), lambda i, j, k: (i, k))
    if mode == "nt":
        b_spec = pl.BlockSpec((tn, tk), lambda i, j, k: (j, k))
    else:
        b_spec = pl.BlockSpec((tk, tn), lambda i, j, k: (k, j))
    blk = pl.BlockSpec((tm, tn), lambda i, j, k: (i, j))
    return _mm_call(name, (M // tm, N // tn, K // tk), a, a_spec, b, b_spec,
                    {"nn": (1, 0), "nt": (1, 1), "tn": (0, 0)}[mode], jax.ShapeDtypeStruct((M, N), out_dtype), blk,
                    (tm, tn), add, blk, **pushed)


FF_SH = 2 * D_FF // N_DEV


def _ffn_up(h2, wup, **pushed):
    S, D = h2.shape
    tm = _tile(S, (1024, 512, 256, 128))
    return _mm_call("matmul_up", (S // tm, N_DEV, 1), h2, pl.BlockSpec((tm, D), lambda m, j, k: (m, 0)),
                    wup, pl.BlockSpec((None, D, FF_SH), lambda m, j, k: (j, 0, 0)), (1, 0),
                    jax.ShapeDtypeStruct((2, N_DEV // 2, S, FF_SH), F32),
                    pl.BlockSpec((None, None, tm, FF_SH), lambda m, j, k: (j // 4, j % 4, m, 0)), (tm, FF_SH), **pushed)


def _ffn_down(act, wd, x1, **pushed):
    _, S, _ = act.shape
    D = wd.shape[1]
    tm, tn = _tile(S, (1024, 512, 256, 128)), _tile(D, (512, 256, 128))
    blk = pl.BlockSpec((tm, tn), lambda m, n, k: (m, n))
    return _mm_call("matmul_down", (S // tm, D // tn, N_DEV // 2), act,
                    pl.BlockSpec((None, tm, FF_SH), lambda m, n, k: (k, m, 0)), wd,
                    pl.BlockSpec((FF_SH, tn), lambda m, n, k: (k, n)), (1, 0), jax.ShapeDtypeStruct((S, D), F32), blk,
                    (tm, tn), x1, blk, **pushed)


def _ffn_dact(dxb, wd):
    S, D = dxb.shape
    tm = _tile(S, (1024, 512, 256, 128))
    return _mm_call("matmul_dact", (S // tm, N_DEV // 2, 1), dxb, pl.BlockSpec((tm, D), lambda m, j, k: (m, 0)), wd,
                    pl.BlockSpec((FF_SH, D), lambda m, j, k: (j, 0)), (1, 1),
                    jax.ShapeDtypeStruct((N_DEV // 2, S, FF_SH), BF),
                    pl.BlockSpec((None, tm, FF_SH), lambda m, j, k: (j, m, 0)), (tm, FF_SH))


def _ffn_dw_down(act, dxb):
    _, S, _ = act.shape
    D = dxb.shape[1]
    tk, tn = _tile(S, (2048, 1024, 512, 256, 128)), _tile(D, (512, 256, 128))
    return _mm_call("matmul_dw_down", (N_DEV // 2, D // tn, S // tk), act,
                    pl.BlockSpec((None, tk, FF_SH), lambda j, n, k: (j, k, 0)), dxb,
                    pl.BlockSpec((tk, tn), lambda j, n, k: (k, n)), (0, 0), jax.ShapeDtypeStruct((D_FF, D), BF),
                    pl.BlockSpec((FF_SH, tn), lambda j, n, k: (j, n)), (FF_SH, tn))


def _ffn_dw_up(h2, du, exchange=None):
    S, D = h2.shape
    tk = _tile(S, (1024, 512, 256, 128))
    return _mm_call("matmul_dw_up", (N_DEV, 1, S // tk), h2, pl.BlockSpec((tk, D), lambda j, n, k: (k, 0)), du,
                    pl.BlockSpec((None, None, tk, FF_SH), lambda j, n, k: (j // 4, j % 4, k, 0)), (0, 0),
                    jax.ShapeDtypeStruct((N_DEV, D, FF_SH), BF),
                    pl.BlockSpec((None, D, FF_SH), lambda j, n, k: (j, 0, 0)), (D, FF_SH), exchange=exchange)


def _ffn_dh2(du, wup, exchange=None):
    S = du.shape[2]
    D = wup.shape[1]
    tm = _tile(S, (1024, 512, 256, 128))
    return _mm_call("matmul_dh2", (S // tm, 1, N_DEV), du,
                    pl.BlockSpec((None, None, tm, FF_SH), lambda m, n, k: (k // 4, k % 4, m, 0)), wup,
                    pl.BlockSpec((None, D, FF_SH), lambda m, n, k: (k, 0, 0)), (1, 1),
                    jax.ShapeDtypeStruct((S, D), F32), pl.BlockSpec((tm, D), lambda m, n, k: (m, 0)), (tm, D),
                    exchange=exchange)


def f_rmsnorm(first, xp, xs, ps, sts):
    return (_rms(xs[0], ps[0]),), ()


def f_xattn(first, xp, xs, ps, sts):
    (xq,), (kv, qn, kn) = xs, ps
    qs = _split(xq, [XA_HD] * XA_HEADS)
    kvs = _split(kv, [XA_HD] * (2 * XA_HEADS))
    outs = []
    for h in range(XA_HEADS):
        q = _rms(qs[h], qn)
        k = _rms(kvs[h], kn)
        v = kvs[XA_HEADS + h]
        s = mm_nt(q, k) * (XA_HD ** -0.5)
        m = lax.stop_gradient(jnp.max(s, axis=-1, keepdims=True))
        p = jnp.exp(s - m)
        outs.append(mm_nn(p / jnp.sum(p, axis=-1, keepdims=True), v))
    return (jnp.concatenate(outs, axis=1),), ()


def _conv(xp, x, w, b, first, taps):
    xp = jnp.where(first, 0.0, xp)
    y = b + w[taps - 1:taps] * x
    for d in range(1, taps):
        y = y + w[taps - 1 - d:taps - d] * _shift(xp, x, d)
    return y


def _unstack2(x):
    @jax.custom_vjp
    def us(x):
        return x[0], x[1]

    us.defvjp(lambda x: (us(x), None), lambda r, g: (jnp.stack(g),))
    return us(x)


def f_ffn_act(first, xp, xs, ps, sts):
    (up,), (u,), (wg, wv, bg, bv) = xp, xs, ps
    (ugp, uvp), (ug, uv) = _unstack2(up), _unstack2(u)
    gate = _conv(ugp, ug, wg, bg, first, FFN_CONV)
    val = _conv(uvp, uv, wv, bv, first, FFN_CONV)
    return (jax.nn.silu(gate) * val,), ()


def _gla_chunk(q, k, v, la, st):
    c = q.shape[0]
    b = cumsum_rows(la)
    b_last = _row(b, c - 1)
    b_ref = _row(b, c // 2 - 1)
    att = mm_nt(q * jnp.exp(b - b_ref), k * jnp.exp(b_ref - b))
    att = jnp.where(_tril(c), att, 0.0)
    o = mm_nn(att, v) + mm_nt(q * jnp.exp(b), st)
    st_new = st * jnp.exp(b_last) + mm_tn(v, k * jnp.exp(b_last - b))
    return o, st_new


def _split_rows(x, n):
    c = x.shape[0] // n

    @jax.custom_vjp
    def sp(x):
        return tuple(x[i * c:(i + 1) * c] for i in range(n))

    sp.defvjp(lambda x: (sp(x), None), lambda r, g: (jnp.concatenate(list(g), axis=0),))
    return sp(x)


def _gla_scan(q, k, v, la, st):
    n = q.shape[0] // CHUNK
    if n == 1:
        return _gla_chunk(q, k, v, la, st)
    outs = []
    for qc, kc, vc, lc in zip(*(_split_rows(t, n) for t in (q, k, v, la))):
        o, st = _gla_chunk(qc, kc, vc, lc, st)
        outs.append(o)
    return jnp.concatenate(outs, axis=0), st


def _a_cols(ntot):
    used = D_XA + 2 * GLA_HEADS * GLA_DK + D_MIX + GLA_RANK + D_MIX
    return [D_XA, GLA_HEADS * GLA_DK, GLA_HEADS * GLA_DK, D_MIX, GLA_RANK, D_MIX] + ([ntot - used] if ntot > used else [])


def f_gla(first, xp, xs, ps, sts):
    (p,), (wg2, bg, on) = xs, ps
    parts = _split(p, _a_cols(p.shape[1]))
    q, k, v, glr, og = parts[1:6]
    la = jax.nn.log_sigmoid(mm_nn(glr, wg2) + bg) / GLA_GATE_NORM
    qs = _split(q * (GLA_DK ** -0.5), [GLA_DK] * GLA_HEADS)
    ks = _split(k, [GLA_DK] * GLA_HEADS)
    vs = _split(v, [GLA_DV] * GLA_HEADS)
    las = _split(la, [GLA_DK] * GLA_HEADS)
    outs, new = [], []
    for h in range(GLA_HEADS):
        o, s = _gla_scan(qs[h], ks[h], vs[h], las[h], sts[h])
        outs.append(_rms(o, on))
        new.append(s)
    return (jnp.concatenate(outs, axis=1) * jax.nn.silu(og),), tuple(new)


def f_hgrn(first, xp, xs, ps, sts):
    (p,), (lbp, on) = xs, ps
    _, q, fgate, iv, og = _split(p, [D_XA, D_MIX, D_MIX, D_MIX, D_MIX])
    e = jnp.exp(lbp - jnp.max(lbp, axis=0, keepdims=True))
    row = lax.broadcasted_iota(jnp.int32, e.shape, 0)
    lb = jnp.sum(jnp.where(row >= 1, e, 0.0), axis=0, keepdims=True) / jnp.sum(e, axis=0, keepdims=True)
    fg = lb + (1.0 - lb) * jax.nn.sigmoid(fgate)
    qs = _split(jax.nn.silu(q), [HGRN_DK] * HGRN_HEADS)
    ks = _split(1.0 - fg, [HGRN_DK] * HGRN_HEADS)
    vs = _split(iv, [HGRN_DV] * HGRN_HEADS)
    las = _split(jnp.log(fg), [HGRN_DK] * HGRN_HEADS)
    outs, new = [], []
    for h in range(HGRN_HEADS):
        o, s = _gla_scan(qs[h], ks[h], vs[h], las[h], sts[h])
        outs.append(_rms(o, on))
        new.append(s)
    return (jnp.concatenate(outs, axis=1) * jax.nn.sigmoid(og),), tuple(new)


def _c_cols(ntot):
    gn = SSM_GROUPS * SSM_STATE
    used = D_XA + D_MIX + D_MIX + 2 * gn + SSM_HEADS
    return [D_XA, D_MIX, D_MIX + 2 * gn, SSM_HEADS] + ([ntot - used] if ntot > used else [])


def f_ssd(first, xp, xs, ps, sts):
    (pp,), (p,), (cw, cb, dtb, alog, dsk, ng) = xp, xs, ps
    gn = SSM_GROUPS * SSM_STATE
    _, z, xbc, dtr = _split(p, _c_cols(p.shape[1]))[:4]
    xbc_p = _split(pp, _c_cols(p.shape[1]))[2]
    xbc = jax.nn.silu(_conv(xbc_p, xbc, cw, cb, first, SSM_CONV))
    xs_, bm, cm = _split(xbc, [D_MIX, gn, gn])
    dt = jax.nn.softplus(dtr + dtb)
    n = p.shape[0] // CHUNK
    ys, sts = [], tuple(sts)
    for xc, bc, cc, dc in zip(*(_split_rows(t, n) for t in (xs_, bm, cm, dt))):
        y, sts = _ssd_chunk(xc, bc, cc, dc, alog, dsk, sts)
        ys.append(y)
    y = jnp.concatenate(ys, axis=0) * jax.nn.silu(z)
    gw = D_MIX // SSM_GROUPS
    yg = _split(y, [gw] * SSM_GROUPS)
    ngs = _split(ng, [gw] * SSM_GROUPS)
    y = jnp.concatenate([_rms(yg[g], ngs[g]) for g in range(SSM_GROUPS)], axis=1)
    return (y,), sts


def _ssd_chunk(xs_, bm, cm, dt, alog, dsk, sts):
    c = xs_.shape[0]
    hg = SSM_HEADS // SSM_GROUPS
    a = dt * (-jnp.exp(alog))
    acs = cumsum_rows(a)
    acs_t = cumsum_rows_t(a)
    acs_last = _row(acs, c - 1)
    dt_h = _split(dt, [1] * SSM_HEADS)
    acs_h = _split(acs, [1] * SSM_HEADS)
    al_h = _split(acs_last, [1] * SSM_HEADS)
    d_h = _split(dsk, [1] * SSM_HEADS)
    x2s = _split(xs_, [2 * SSM_HD] * (SSM_HEADS // 2))
    bms = _split(bm, [SSM_STATE] * SSM_GROUPS)
    cms = _split(cm, [SSM_STATE] * SSM_GROUPS)
    tril = _tril(c)
    cbs = [mm_nt(cms[g], bms[g]) for g in range(SSM_GROUPS)]
    ys, new = [], []
    for j in range(SSM_HEADS // 2):
        g = (2 * j) // hg
        h0, h1 = 2 * j, 2 * j + 1
        xdt = x2s[j] * _lane_pair(dt_h[h0], dt_h[h1])
        acs2 = _lane_pair(acs_h[h0], acs_h[h1])
        al2 = _lane_pair(al_h[h0], al_h[h1])
        yd = []
        for h in (h0, h1):
            seg = acs_h[h] - _row(acs_t, h)
            lm = jnp.exp(jnp.where(tril, seg, NEG))
            yd.append(mm_nn(cbs[g] * lm, xdt))
        lane = lax.broadcasted_iota(jnp.int32, xdt.shape, 1)
        y_diag = jnp.where(lane < SSM_HD, yd[0], yd[1])
        y_off = mm_nn(cms[g], sts[j]) * jnp.exp(acs2)
        x_end = xdt * jnp.exp(al2 - acs2)
        new.append(sts[j] * jnp.exp(al2) + mm_tn(bms[g], x_end))
        ys.append(y_diag + y_off + _lane_pair(d_h[h0], d_h[h1]) * x2s[j])
    return jnp.concatenate(ys, axis=1), tuple(new)


def f_dil_prep(first, xp, xs, ps, sts):
    (p, pos), (qn, kn, invf, sign) = xs, ps
    nh = len(DIL_GROUPS) * DIL_HEADS
    _, q, k, v = _split(p, [D_XA] + [nh * DIL_HD] * 3)
    ang = pos * invf
    cos, sin = jnp.cos(ang), jnp.sin(ang) * sign

    def rope(t, g):
        hs = _split(t, [DIL_HD] * nh)
        out = []
        for h in hs:
            n = _rms(h, g)
            out.append(n * cos + _roll_half(n) * sin)
        return [jnp.concatenate(out[i:i + DIL_HEADS], axis=1) for i in range(0, nh, DIL_HEADS)]

    return tuple(rope(q, qn) + rope(k, kn) + list(_split(v, [D_DIL] * len(DIL_GROUPS)))), ()


def f_dil_attn(first, xp, xs, ps, sts):
    (kp, vp), (q, k, v) = xp, xs
    Q = DIL_BLOCK
    qs, ks, vs = (_split(t, [DIL_HD] * DIL_HEADS) for t in (q, k, v))
    kps, vps = (_split(t, [DIL_HD] * DIL_HEADS) for t in (kp, vp))
    i = lax.broadcasted_iota(jnp.int32, (Q, 2 * Q), 0)
    j = lax.broadcasted_iota(jnp.int32, (Q, 2 * Q), 1)
    dist = Q + i - j
    mask = (dist >= 0) & (dist <= Q) & (jnp.logical_not(first) | (j >= Q))
    outs, lses = [], []
    for h in range(DIL_HEADS):
        k2 = jnp.concatenate([kps[h], ks[h]], axis=0)
        v2 = jnp.concatenate([vps[h], vs[h]], axis=0)
        s = jnp.where(mask, mm_nt(qs[h], k2) * (DIL_HD ** -0.5), NEG)
        m = lax.stop_gradient(jnp.max(s, axis=-1, keepdims=True))
        p = jnp.exp(s - m)
        l = jnp.sum(p, axis=-1, keepdims=True)
        outs.append(mm_nn(p / l, v2))
        lses.append(jnp.broadcast_to(m + jnp.log(l), (Q, DIL_HD)))
    return (jnp.concatenate(outs, axis=1), jnp.concatenate(lses, axis=1)), ()


def f_dil_merge(first, xp, xs, ps, sts):
    o0, o1, o2, l0, l1, l2 = xs
    m = jnp.maximum(jnp.maximum(l0, l1), l2)
    e0, e1, e2 = jnp.exp(l0 - m), jnp.exp(l1 - m), jnp.exp(l2 - m)
    den = e0 + e1 + e2
    return ((e0 * o0 + e1 * o1 + e2 * o2) / den,), ()


def _loss_head(y, target):
    S, D = y.shape
    R = _tile(S, (512, 256, 128, 64, 32, 16, 8))

    def body(y_ref, t_ref, dy_ref, dyb_ref, l_ref):
        e = y_ref[...] - t_ref[...]
        dy_ref[...] = e * (1.0 / D)
        dyb_ref[...] = (e * (1.0 / D)).astype(BF)

        @pl.when(pl.program_id(0) == 0)
        def _():
            l_ref[...] = jnp.zeros_like(l_ref)

        l_ref[...] += jnp.broadcast_to(0.5 * jnp.sum(jnp.mean(e * e, axis=-1, keepdims=True), axis=0, keepdims=True),
                                       l_ref.shape)

    blk = pl.BlockSpec((R, D), lambda i: (i, 0))
    dy, dyb, l = pl.pallas_call(
        body, name="loss_head", grid=(S // R,), in_specs=[blk, blk],
        out_specs=[blk, blk, pl.BlockSpec((8, 128), lambda i: (0, 0))],
        out_shape=[jax.ShapeDtypeStruct((S, D), F32), jax.ShapeDtypeStruct((S, D), BF),
                   jax.ShapeDtypeStruct((8, 128), F32)],
        compiler_params=pltpu.CompilerParams(dimension_semantics=("arbitrary",)))(y, target)
    return dy, dyb, l[0, 0]


def _adamw(parts, w, m, v, name):
    _, n, width = parts.shape
    tr = _tile(n, [t for t in (512, 256, 128, 64, 32, 16, 8) if t * width <= ADAM_BLOCK])

    def body(p_ref, w_ref, m_ref, v_ref, g_ref, d_ref, nm_ref, nv_ref):
        g = p_ref[0].astype(F32)
        for s in range(1, N_DEV):
            g = g + p_ref[s].astype(F32)
        nm = ADAM_B1 * m_ref[...] + (1.0 - ADAM_B1) * g
        nv = ADAM_B2 * v_ref[...] + (1.0 - ADAM_B2) * (g * g)
        m_hat = nm / (1.0 - ADAM_B1 ** ADAM_STEP)
        v_hat = nv / (1.0 - ADAM_B2 ** ADAM_STEP)
        g_ref[...] = g
        d_ref[...] = -ADAM_LR * (m_hat / (jnp.sqrt(v_hat) + ADAM_EPS) + ADAM_WD * w_ref[...])
        nm_ref[...] = nm
        nv_ref[...] = nv

    blk = pl.BlockSpec((tr, width), lambda i: (i, 0))
    return pl.pallas_call(
        body, name=name, grid=(n // tr,),
        in_specs=[pl.BlockSpec((N_DEV, tr, width), lambda i: (0, i, 0)), blk, blk, blk],
        out_specs=[blk] * 4, out_shape=[jax.ShapeDtypeStruct((n, width), F32)] * 4,
        compiler_params=pltpu.CompilerParams(dimension_semantics=("arbitrary",), vmem_limit_bytes=VMEM_LIMIT))(
            parts, w, m, v)


def _peer(k):
    x, y, c = lax.axis_index("x"), lax.axis_index("y"), lax.axis_index("c")
    px = 1 - x if k & 4 else x
    py = 1 - y if k & 2 else y
    pc = 1 - c if k & 1 else c
    return (px, py, pc), 4 * px + 2 * py + pc


def _my_id():
    return 4 * lax.axis_index("x") + 2 * lax.axis_index("y") + lax.axis_index("c")


def _all_gather(x, name):
    def body(x_ref, out_ref, send, recv, loc):
        me = _my_id()
        mine = pltpu.make_async_copy(x_ref, out_ref.at[me], loc)
        mine.start()
        cps = []
        for k in range(1, N_DEV):
            peer, _ = _peer(k)
            cp = pltpu.make_async_remote_copy(src_ref=x_ref, dst_ref=out_ref.at[me], send_sem=send.at[k - 1],
                                              recv_sem=recv.at[k - 1], device_id=peer,
                                              device_id_type=pl.DeviceIdType.MESH)
            cp.start()
            cps.append(cp)
        for k in range(1, N_DEV):
            peer, pid = _peer(k)
            pltpu.make_async_remote_copy(src_ref=x_ref, dst_ref=out_ref.at[pid], send_sem=send.at[k - 1],
                                         recv_sem=recv.at[k - 1], device_id=peer,
                                         device_id_type=pl.DeviceIdType.MESH).wait_recv()
        for cp in cps:
            cp.wait_send()
        mine.wait()

    return pl.pallas_call(
        body, name=name, out_shape=jax.ShapeDtypeStruct((N_DEV,) + x.shape, x.dtype),
        in_specs=[pl.BlockSpec(memory_space=pl.ANY)], out_specs=pl.BlockSpec(memory_space=pl.ANY),
        scratch_shapes=[pltpu.SemaphoreType.DMA((N_DEV - 1,)), pltpu.SemaphoreType.DMA((N_DEV - 1,)),
                        pltpu.SemaphoreType.DMA],
        compiler_params=pltpu.CompilerParams(has_side_effects=True))(x)


def _exchange_sems(n):
    return [pltpu.SemaphoreType.DMA((n * (N_DEV - 1),)), pltpu.SemaphoreType.DMA((n * (N_DEV - 1),)),
            pltpu.SemaphoreType.DMA((n,))]


def _exchange_copies(g_refs, out_refs, send, recv, loc, with_arrivals):
    me = _my_id()

    def mine(g, o, d):
        return g.at[d] if len(g.shape) == len(o.shape) else g

    local = [pltpu.make_async_copy(mine(g, o, me), o.at[me], loc.at[w]) for w, (g, o) in enumerate(zip(g_refs, out_refs))]
    pushes, arrivals = [], []
    for k in range(1, N_DEV):
        peer, pid = _peer(k)
        for w, (g, o) in enumerate(zip(g_refs, out_refs)):
            s = w * (N_DEV - 1) + k - 1
            ends = [(mine(g, o, pid), o.at[me], pushes)] + ([(mine(g, o, me), o.at[pid], arrivals)] if with_arrivals else [])
            for src, dst, into in ends:
                into.append(pltpu.make_async_remote_copy(src_ref=src, dst_ref=dst, send_sem=send.at[s],
                                                         recv_sem=recv.at[s], device_id=peer,
                                                         device_id_type=pl.DeviceIdType.MESH))
    return local, pushes, arrivals


def _exchange_start(g_refs, out_refs, send, recv, loc):
    local, pushes, _ = _exchange_copies(g_refs, out_refs, send, recv, loc, False)
    for cp in local + pushes:
        cp.start()


def _exchange_wait(g_refs, out_refs, send, recv, loc):
    local, pushes, arrivals = _exchange_copies(g_refs, out_refs, send, recv, loc, True)
    for cp in arrivals:
        cp.wait_recv()
    for cp in pushes:
        cp.wait_send()
    for cp in local:
        cp.wait()


def _exchange_many(gs, name):
    n = len(gs)

    def body(*refs):
        g_refs, out_refs, sems = refs[:n], refs[n:2 * n], refs[2 * n:]
        _exchange_start(g_refs, out_refs, *sems)
        _exchange_wait(g_refs, out_refs, *sems)

    return pl.pallas_call(
        body, name=name, out_shape=[jax.ShapeDtypeStruct(g.shape, g.dtype) for g in gs],
        in_specs=[pl.BlockSpec(memory_space=pl.ANY)] * n, out_specs=[pl.BlockSpec(memory_space=pl.ANY)] * n,
        scratch_shapes=_exchange_sems(n), compiler_params=pltpu.CompilerParams(has_side_effects=True))(*gs)


def _gather_many(xs, name):
    n = len(xs)

    def body(*refs):
        x_refs, out_refs, (send, recv, loc) = refs[:n], refs[n:2 * n], refs[2 * n:]
        x, y, c = lax.axis_index("x"), lax.axis_index("y"), lax.axis_index("c")
        me, sibling = (x, y, c), (x, y, 1 - c)
        chips = [(1 - x, y), (x, 1 - y), (1 - x, 1 - y)]

        def slot(p):
            return 4 * p[0] + 2 * p[1] + p[2]

        def copy(w, k, block, to, src=None):
            dst = out_refs[w].at[slot(block)]
            return pltpu.make_async_remote_copy(src_ref=dst if src is None else src, dst_ref=dst,
                                                send_sem=send.at[w * (N_DEV - 1) + k], recv_sem=recv.at[w * (N_DEV - 1) + k],
                                                device_id=to, device_id_type=pl.DeviceIdType.MESH)

        mine = [pltpu.make_async_copy(x_refs[w], out_refs[w].at[slot(me)], loc.at[w]) for w in range(n)]
        for cp in mine:
            cp.start()
        first = []
        for j, chip in enumerate(chips):
            first += [copy(w, 1 + j, me, (*chip, c), src=x_refs[w]) for w in range(n)]
        first += [copy(w, 0, me, sibling, src=x_refs[w]) for w in range(n)]
        for cp in first:
            cp.start()
        passed = []
        for j, chip in enumerate(chips):
            for w in range(n):
                copy(w, 1 + j, (*chip, c), me).wait_recv()
                cp = copy(w, 4 + j, (*chip, c), sibling)
                cp.start()
                passed.append(cp)
        for w in range(n):
            copy(w, 0, sibling, me).wait_recv()
            for j, chip in enumerate(chips):
                copy(w, 4 + j, (*chip, 1 - c), me).wait_recv()
        for cp in first + passed:
            cp.wait_send()
        for cp in mine:
            cp.wait()

    return pl.pallas_call(
        body, name=name, out_shape=[jax.ShapeDtypeStruct((N_DEV,) + x.shape, x.dtype) for x in xs],
        in_specs=[pl.BlockSpec(memory_space=pl.ANY)] * n, out_specs=[pl.BlockSpec(memory_space=pl.ANY)] * n,
        scratch_shapes=[pltpu.SemaphoreType.DMA((n * (N_DEV - 1),)), pltpu.SemaphoreType.DMA((n * (N_DEV - 1),)),
                        pltpu.SemaphoreType.DMA((n,))],
        compiler_params=pltpu.CompilerParams(has_side_effects=True))(*xs)


def _cat_segs(G, ws, n_mix):
    segs = []
    for g in range(G):
        lo, hi = g * ws, (g + 1) * ws
        if lo < n_mix:
            segs.append((g, 0, min(hi, n_mix) - lo, D_XA + lo))
        if hi > n_mix:
            s = max(lo, n_mix)
            segs.append((g, s - lo, hi - s, s - n_mix))
    return segs


def _cat_cols(src, n_mix, ntot):
    G, R, ws = src.shape
    segs = _cat_segs(G, ws, n_mix)
    tr = _tile(R, (256, 128, 64, 32, 16, 8))

    def body(i_ref, o_ref):
        if ntot > G * ws:
            o_ref[...] = jnp.zeros_like(o_ref)
        for g, s, n, d in segs:
            o_ref[:, d:d + n] = i_ref[g][:, s:s + n]

    return pl.pallas_call(
        body, name="cat_cols", grid=(R // tr,), in_specs=[pl.BlockSpec((G, tr, ws), lambda i: (0, i, 0))],
        out_specs=pl.BlockSpec((tr, ntot), lambda i: (i, 0)), out_shape=jax.ShapeDtypeStruct((R, ntot), src.dtype),
        compiler_params=pltpu.CompilerParams(dimension_semantics=("arbitrary",)))(src)


def _uncat_cols(dw, G, ws, n_mix):
    R, ntot = dw.shape
    segs = _cat_segs(G, ws, n_mix)
    tr = _tile(R, (256, 128, 64, 32, 16, 8))

    def body(i_ref, o_ref):
        v = i_ref[...]
        for g, s, n, d in segs:
            o_ref[g, :, s:s + n] = v[:, d:d + n]

    return pl.pallas_call(
        body, name="uncat_cols", grid=(R // tr,), in_specs=[pl.BlockSpec((tr, ntot), lambda i: (i, 0))],
        out_specs=pl.BlockSpec((G, tr, ws), lambda i: (0, i, 0)), out_shape=jax.ShapeDtypeStruct((G, R, ws), dw.dtype),
        compiler_params=pltpu.CompilerParams(dimension_semantics=("arbitrary",)))(dw)


PACK_W = 1024


def _granule(n):
    return (256 if n >= 256 * PACK_W else 8) * PACK_W


def _pack(arrs, dtype):
    flat = jnp.concatenate([a.reshape(-1).astype(dtype) for a in arrs])
    n = flat.shape[0]
    pad = (-n) % _granule(n)
    if pad:
        flat = jnp.concatenate([flat, jnp.zeros((pad,), dtype)])
    return flat.reshape(-1, PACK_W)


def _unpack(packed, shapes):
    flat = packed.reshape(-1)
    out, o = [], 0
    for s in shapes:
        n = math.prod(s)
        out.append(flat[o:o + n].reshape(s))
        o += n
    return out


def _pack_lead(arrs, dtype):
    flat = jnp.concatenate([a.reshape(N_DEV, -1).astype(dtype) for a in arrs], axis=1)
    n = flat.shape[1]
    pad = (-n) % _granule(n)
    if pad:
        flat = jnp.concatenate([flat, jnp.zeros((N_DEV, pad), dtype)], axis=1)
    return flat.reshape(N_DEV, -1, PACK_W)


def _to_full(stacked, axis):
    t = jnp.moveaxis(stacked, 0, axis)
    s = list(t.shape)
    return t.reshape(s[:axis] + [s[axis] * s[axis + 1]] + s[axis + 2:])


def _to_chunks(full, axis):
    s = list(full.shape)
    t = full.reshape(s[:axis] + [N_DEV, s[axis] // N_DEV] + s[axis + 1:])
    return jnp.moveaxis(t, axis, 0)


def _rows_of(S):
    return _tile(S, (512, 256, 128, 64))


def _norm_fwd(x, g, dt=BF):
    (h,), _ = _seq_fwd("rmsnorm_fwd", f_rmsnorm, _rows_of(x.shape[0]), [_row_spec(x)], [_par_spec(g)], [],
                       [_out_spec(x.shape[1], dt=dt)])
    return h


def _norm_bwd(x, g, dh, res=None):
    if res is None:
        (dx,), (dg,) = _seq_bwd("rmsnorm_bwd", f_rmsnorm, _rows_of(x.shape[0]), [_row_spec(x)], [_par_spec(g)], [],
                                [_row_spec(dh)], [])
        return dx, None, dg
    (dx,), (dg,), (dxb,) = _seq_bwd("rmsnorm_res_bwd", f_rmsnorm, _rows_of(x.shape[0]), [_row_spec(x)], [_par_spec(g)], [],
                                    [_row_spec(dh)], [], dx_add={0: res}, dx_bf=True)
    return dx, dxb, dg


def _mixer_specs(kind, S, p, w):
    if kind == 0:
        return (f_gla, min(S, MIX_ROWS), [_row_spec(p)],
                [_par_spec(w['a_w_gate2']), _par_spec(w['a_b_gate'].reshape(1, -1)), _par_spec(w['a_o_norm'].reshape(1, -1))],
                [(GLA_DV, GLA_DK)] * GLA_HEADS, D_MIX)
    if kind == 2:
        return (f_ssd, min(S, MIX_ROWS), [_row_spec(p, prev='halo')],
                [_par_spec(w['c_conv_w']), _par_spec(w['c_conv_b'].reshape(1, -1)), _par_spec(w['c_dt_bias'].reshape(1, -1)),
                 _par_spec(w['c_a_log'].reshape(1, -1)), _par_spec(w['c_d'].reshape(1, -1)),
                 _par_spec(w['c_norm'].reshape(1, -1))],
                [(SSM_STATE, 2 * SSM_HD)] * (SSM_HEADS // 2), D_MIX)
    return (f_hgrn, min(S, MIX_ROWS), [_row_spec(p)],
            [_par_spec(w['d_lower_bounds']), _par_spec(w['d_o_norm'].reshape(1, -1))],
            [(HGRN_DV, HGRN_DK)] * HGRN_HEADS, D_MIX)


def _perm(t, r):
    if r == 1:
        return t
    S, n = t.shape
    return t.reshape(S // r, r, n).transpose(1, 0, 2).reshape(S, n)


def _unperm(t, r):
    if r == 1:
        return t
    S, n = t.shape
    return t.reshape(r, S // r, n).transpose(1, 0, 2).reshape(S, n)


def _rope_consts():
    half = DIL_HD // 2
    inv = ROPE_THETA ** (-jnp.arange(half, dtype=F32) / half)
    invf = jnp.concatenate([inv, inv]).reshape(1, DIL_HD)
    sign = jnp.concatenate([-jnp.ones((half,), F32), jnp.ones((half,), F32)]).reshape(1, DIL_HD)
    return invf, sign


def _dil_fwd(p, pos, w, ncat):
    S = p.shape[0]
    invf, sign = _rope_consts()
    prep_rows = [_row_spec(p), _row_spec(pos, diff=False)]
    prep_pars = [_par_spec(w['b_q_norm'].reshape(1, -1)), _par_spec(w['b_k_norm'].reshape(1, -1)),
                 _par_spec(invf, diff=False), _par_spec(sign, diff=False)]
    ng = len(DIL_GROUPS)
    qkv, _ = _seq_fwd("dil_prep_fwd", f_dil_prep, _tile(S, (256, 128)), prep_rows, prep_pars, [],
                      [_out_spec(D_DIL) for _ in range(3 * ng)])
    res = dict(perm=[], o=[], lse=[])
    for g, (window, r) in enumerate(DIL_GROUPS):
        qp, kp, vp = _perm(qkv[g], r), _perm(qkv[ng + g], r), _perm(qkv[2 * ng + g], r)
        rows = [_row_spec(qp), _row_spec(kp, prev='block'), _row_spec(vp, prev='block')]
        (o, lse), _ = _seq_fwd("dil_attn_fwd", f_dil_attn, DIL_BLOCK, rows, [], [], [_out_spec(D_DIL), _out_spec(D_DIL)],
                               period=S // r // DIL_BLOCK)
        res['perm'].append((qp, kp, vp))
        res['o'].append(_unperm(o, r))
        res['lse'].append(_unperm(lse, r))
    mrows = [_row_spec(t) for t in res['o'] + res['lse']]
    (cat,), _ = _seq_fwd("dil_merge_fwd", f_dil_merge, _rows_of(S), mrows, [], [], [_out_spec(ncat, w=D_DIL, dt=BF)])
    res['prep'] = (prep_rows, prep_pars)
    return cat, res


def _dil_bwd(dtok, res, p):
    S = p.shape[0]
    mrows = [_row_spec(t) for t in res['o'] + res['lse']]
    dm, _ = _seq_bwd("dil_merge_bwd", f_dil_merge, _rows_of(S), mrows, [], [], [dtok], [])
    dq, dk, dv = [], [], []
    for g, (window, r) in enumerate(DIL_GROUPS):
        qp, kp, vp = res['perm'][g]
        rows = [_row_spec(qp), _row_spec(kp, prev='block'), _row_spec(vp, prev='block')]
        douts = [_row_spec(_perm(dm[g], r)), _row_spec(_perm(dm[3 + g], r))]
        (a, b, c), _ = _seq_bwd("dil_attn_bwd", f_dil_attn, DIL_BLOCK, rows, [], [], douts, [],
                                period=S // r // DIL_BLOCK)
        dq.append(_unperm(a, r)); dk.append(_unperm(b, r)); dv.append(_unperm(c, r))
    prep_rows, prep_pars = res['prep']
    (dp,), (dqn, dkn) = _seq_bwd("dil_prep_bwd", f_dil_prep, _tile(S, (256, 128)), prep_rows, prep_pars, [],
                                 [_row_spec(t) for t in dq + dk + dv], [], dx_dt=BF)
    return dp, dict(b_q_norm=dqn.reshape(-1), b_k_norm=dkn.reshape(-1))


def _ffn_specs(u, cw, cb):
    half = N_DEV // 2
    rows = [_row_spec(u, prev='halo', lb=(2, None), li=lambda jc: (0, jc))]
    pars = [_par_spec(cw, bs=(None, FFN_CONV, FF_SH), idx=lambda jc: (jc, 0, 0)),
            _par_spec(cw, bs=(None, FFN_CONV, FF_SH), idx=lambda jc: (jc + half, 0, 0)),
            _par_spec(cb, bs=(None, 1, FF_SH), idx=lambda jc: (jc, 0, 0)),
            _par_spec(cb, bs=(None, 1, FF_SH), idx=lambda jc: (jc + half, 0, 0))]
    return half, rows, pars


N_MIX = {0: 2 * GLA_HEADS * GLA_DK + 2 * D_MIX + GLA_RANK, 1: 3 * len(DIL_GROUPS) * D_DIL,
         2: 2 * D_MIX + 2 * SSM_GROUPS * SSM_STATE + SSM_HEADS, 3: 2 * HGRN_HEADS * HGRN_DK + 2 * D_MIX}
W_IN = {0: 'a_w_in', 1: 'b_w_in', 2: 'c_w_in', 3: 'd_w_in'}
W_OUT = {0: 'a_w_out', 1: 'b_w_out', 2: 'c_w_out', 3: 'd_w_out'}


def _in_blocks(name, t):
    return t if SHARD_AXIS[name] == 1 else t.reshape(1, N_DEV * t.shape[1], t.shape[2])


LAYER_STACKED = ('ffn_w_up', 'ffn_conv_w', 'ffn_w_down', 'xa_w_kv')


SMALL_OF_KIND = {0: ['a_w_gate2'], 2: ['c_conv_w']}


def _layer_names(i):
    return list(LAYER_STACKED) + [W_IN[i % 4], W_OUT[i % 4]] + SMALL_OF_KIND.get(i % 4, [])


def _device_step(x, mem, pos, sh, rep, target, distributed=True):
    S, D = x.shape
    w = dict(rep)
    posf = pos.reshape(S, 1).astype(F32)
    n_mix, w_in_name, w_out_name = N_MIX, W_IN, W_OUT
    ntot = {k: -(-(n_mix[k] + D_XA) // 256) * 256 for k in n_mix}
    mem_g = w['mem_norm'].reshape(1, -1)
    mem_n = _norm_fwd(mem, mem_g)
    R = _rows_of(S)

    def mine(i):
        return {n: (sh[n][i] if n in LAYER_STACKED else sh[n]) for n in _layer_names(i)}

    if distributed:
        gl = dict(zip(_layer_names(0), _gather_many(list(mine(0).values()), "gather_weights")))
    else:
        gl = {n: (sh[n][:, 0] if n in LAYER_STACKED else sh[n]) for n in _layer_names(0)}

    saved = []
    for i in range(DEPTH):
        kind = i % 4
        L = dict(x0=x)
        for n in SMALL_OF_KIND.get(kind, []):
            w[n] = _to_full(gl[n], 1)
        in_blocks = _in_blocks(w_in_name[kind], gl[w_in_name[kind]])
        w_out = (_to_full(gl[w_out_name[kind]], 1) if SHARD_AXIS[w_out_name[kind]] == 1
                 else gl[w_out_name[kind]].reshape(-1, D))
        nxt, push = {}, [[], [], []]
        if i + 1 < DEPTH:
            if distributed:
                nxt = mine(i + 1)
                push = [[n for n in nxt if n not in ('ffn_w_up', w_in_name[(i + 1) % 4], w_out_name[(i + 1) % 4])],
                        ['ffn_w_up'], [w_in_name[(i + 1) % 4], w_out_name[(i + 1) % 4]]]
            else:
                nxt = {n: (sh[n][:, i + 1] if n in LAYER_STACKED else sh[n]) for n in _layer_names(i + 1)}
        got = dict(nxt) if not distributed else {}

        def hosted(call, names):
            if not names:
                return call()
            res, arrived = call(gather=[nxt[n] for n in names])
            got.update(zip(names, arrived))
            return res

        g1 = w['mix_norm'][i].reshape(1, -1)
        h = _norm_fwd(x, g1)
        wcat = _cat_cols(in_blocks, n_mix[kind], ntot[kind])
        p = hosted(functools.partial(_matmul, h, wcat, name="matmul_in"), push[0])
        ntok = D_DIL if kind == 1 else D_MIX
        if kind == 1:
            cat, L['dil'] = _dil_fwd(p, posf, w, ntok + D_XA)
        else:
            f, Rm, rows, pars, sshapes, _ = _mixer_specs(kind, S, p, w)
            (cat,), L['states'] = _seq_fwd("mixer%d_fwd" % kind, f, Rm, rows, pars, sshapes,
                                           [_out_spec(ntok + D_XA, w=ntok, dt=BF)], save_states=True)
        wkv = gl['xa_w_kv'].reshape(D, 2 * D_XA)
        kv = _matmul(mem_n, wkv, name="matmul_kv")
        xa_rows = [_row_spec(p, w=D_XA, dn=D_XA)]
        xa_pars = [_par_spec(kv), _par_spec(w['xa_q_norm'][i].reshape(1, -1)), _par_spec(w['xa_k_norm'][i].reshape(1, -1))]
        (cat,), _ = _seq_fwd("xattn_fwd", f_xattn, R, xa_rows, xa_pars, [],
                             [_out_spec(ntok + D_XA, w=D_XA, c=lambda jc: ntok // D_XA, dt=BF)], out_alias={0: cat})
        x1 = _matmul(cat, w_out, add=x, name="matmul_out")
        g2 = w['ffn_norm'][i].reshape(1, -1)
        h2 = _norm_fwd(x1, g2)
        wup = gl['ffn_w_up']
        u = hosted(functools.partial(_ffn_up, h2, wup), push[1])
        cw, cb = gl['ffn_conv_w'], w['ffn_conv_b'][i].reshape(N_DEV, 1, FF_SH)
        nt, frows, fpars = _ffn_specs(u, cw, cb)
        (act,), _ = _seq_fwd("ffn_act_fwd", f_ffn_act, R, frows, fpars, [],
                             [_out_spec(FF_SH, dt=BF, ls=(nt,), lb=(None,), li=lambda jc: (jc,))], ncol=nt)
        wd = gl['ffn_w_down'].reshape(D_FF, D)
        x = hosted(functools.partial(_ffn_down, act, wd, x1), push[2])
        L.update(h=h, p=p, wcat=wcat, kv=kv, wkv=wkv, cat=cat, x1=x1, h2=h2, u=u, act=act, wd=wd, wup=wup, cw=cw, g1=g1,
                 g2=g2, in_blocks=in_blocks, w_out=w_out, shapes={n: t.shape for n, t in gl.items()})
        saved.append(L)
        gl = got

    dx, dxb, loss = _loss_head(x, target)

    G = {}
    d_mem_n = None
    acc = {k: [None] * DEPTH for k in ('mix_norm', 'ffn_norm', 'ffn_conv_b', 'xa_q_norm', 'xa_k_norm')}
    parts = [{} for _ in range(DEPTH)]
    pending = {}
    half = N_DEV // 2

    def sent(call, blocks, layer):
        if not blocks:
            return call()
        if not distributed:
            parts[layer].update(blocks)
            return call()
        res, arrived = call(exchange=list(blocks.values()))
        parts[layer].update(zip(blocks, arrived))
        return res

    for i in reversed(range(DEPTH)):
        kind = i % 4
        L = saved[i]
        Gc = {}
        Gc['ffn_w_down'] = _ffn_dw_down(L['act'], dxb).reshape(N_DEV, D_FF // N_DEV, D)
        dact = _ffn_dact(dxb, L['wd'])
        cw, cb = L['cw'], w['ffn_conv_b'][i].reshape(N_DEV, 1, FF_SH)
        nt, frows, fpars = _ffn_specs(L['u'], cw, cb)
        (du,), (dwg, dwv, dbg, dbv) = _seq_bwd(
            "ffn_act_bwd", f_ffn_act, R, frows, fpars, [], [_row_spec(dact, lb=(None,), li=lambda jc: (jc,))], [],
            ncol=nt, dx_dt=BF)
        Gc['ffn_conv_w'] = jnp.concatenate([dwg[:half], dwv[half:]], axis=0)
        acc['ffn_conv_b'][i] = jnp.concatenate([dbg[:half], dbv[half:]], axis=0).reshape(-1)
        Gc['ffn_w_up'] = _ffn_dw_up(L['h2'], du)
        dh2 = sent(functools.partial(_ffn_dh2, du, L['wup']), pending, i + 1)
        dx1, dx1b, dg2 = _norm_bwd(L['x1'], L['g2'], dh2, res=dx)
        acc['ffn_norm'][i] = dg2.reshape(-1)
        G_out = _matmul(L['cat'], dx1b, mode="tn", out_dtype=BF, name="matmul_dw_out")
        dcat = _matmul(dx1b, L['w_out'], mode="nt", name="matmul_dcat")
        ntok = D_DIL if kind == 1 else D_MIX
        dtok = _row_spec(dcat, w=ntok)
        dxa = _row_spec(dcat, w=D_XA, c=lambda jc: ntok // D_XA)
        p = L['p']
        if kind == 1:
            dp, gm = _dil_bwd(dtok, L['dil'], p)
            G.update(gm)
        else:
            f, Rm, rows, pars, sshapes, _ = _mixer_specs(kind, S, p, w)
            (dp,), dps = _seq_bwd("mixer%d_bwd" % kind, f, Rm, rows, pars, sshapes, [dtok], L['states'], dx_dt=BF)
            if kind == 0:
                Gc['a_w_gate2'], G['a_b_gate'], G['a_o_norm'] = _to_chunks(dps[0], 1), dps[1].reshape(-1), dps[2].reshape(-1)
            elif kind == 2:
                Gc['c_conv_w'] = _to_chunks(dps[0], 1)
                for nme, v in zip(('c_conv_b', 'c_dt_bias', 'c_a_log', 'c_d', 'c_norm'), dps[1:]):
                    G[nme] = v.reshape(-1)
            else:
                G['d_lower_bounds'], G['d_o_norm'] = dps[0], dps[1].reshape(-1)
        xa_rows = [_row_spec(p, w=D_XA)]
        xa_pars = [_par_spec(L['kv']), _par_spec(w['xa_q_norm'][i].reshape(1, -1)), _par_spec(w['xa_k_norm'][i].reshape(1, -1))]
        (dp,), (dkv, dqn, dkn) = _seq_bwd("xattn_bwd", f_xattn, R, xa_rows, xa_pars, [], [dxa], [], dx_dt=BF,
                                          dx_alias={0: dp})
        acc['xa_q_norm'][i], acc['xa_k_norm'][i] = dqn.reshape(-1), dkn.reshape(-1)
        Gc['xa_w_kv'] = _matmul(mem_n, dkv, mode="tn", out_dtype=BF, name="matmul_dw_kv").reshape(
            N_DEV, D // N_DEV, 2 * D_XA)
        d_mem_n = _matmul(dkv, L['wkv'], mode="nt", add=d_mem_n, name="matmul_dmem" + ("" if d_mem_n is None else "_acc"))
        dwcat = sent(functools.partial(_matmul, L['h'], dp, mode="tn", out_dtype=BF, name="matmul_dw_in"),
                     {'ffn_w_up': Gc.pop('ffn_w_up')}, i)
        blocks = L['in_blocks']
        Gc[w_in_name[kind]] = _uncat_cols(dwcat, blocks.shape[0], blocks.shape[2], n_mix[kind]).reshape(
            L['shapes'][w_in_name[kind]])
        Gc[w_out_name[kind]] = (_to_chunks(G_out, 1) if SHARD_AXIS[w_out_name[kind]] == 1
                                else G_out.reshape(L['shapes'][w_out_name[kind]]))
        dh = sent(functools.partial(_matmul, dp, L['wcat'], mode="nt", name="matmul_dh"),
                  {n: Gc.pop(n) for n in ('ffn_w_down', 'ffn_conv_w')}, i)
        dx, dxb, dg1 = _norm_bwd(L['x0'], L['g1'], dh, res=dx1)
        acc['mix_norm'][i] = dg1.reshape(-1)
        pending = Gc

    if distributed:
        names = list(pending)
        parts[0].update(zip(names, _exchange_many([pending[n] for n in names], "exchange_grads")))
    else:
        parts[0].update(pending)
    _, _, dmg = _norm_bwd(mem, mem_g, d_mem_n)
    G['mem_norm'] = dmg.reshape(-1)
    for k, v in acc.items():
        G[k] = jnp.stack(v)
    got = {}
    for i in range(DEPTH):
        for n, t in parts[i].items():
            if n not in LAYER_STACKED:
                got[n] = t
    for n in LAYER_STACKED:
        got[n] = jnp.stack([parts[i][n] for i in range(DEPTH)], axis=1)
    return loss, dx, got, G


def kernel(x, mem, positions, mem_norm, mix_norm, xa_w_kv, xa_q_norm, xa_k_norm, ffn_norm, ffn_w_up, ffn_conv_w, ffn_conv_b, ffn_w_down, a_w_in, a_w_gate2, a_b_gate, a_o_norm, a_w_out, b_w_in, b_q_norm, b_k_norm, b_w_out, c_w_in, c_conv_w, c_conv_b, c_dt_bias, c_a_log, c_d, c_norm, c_w_out, d_w_in, d_lower_bounds, d_o_norm, d_w_out, loss_target, m_mem_norm, m_mix_norm, m_xa_w_kv, m_xa_q_norm, m_xa_k_norm, m_ffn_norm, m_ffn_w_up, m_ffn_conv_w, m_ffn_conv_b, m_ffn_w_down, m_a_w_in, m_a_w_gate2, m_a_b_gate, m_a_o_norm, m_a_w_out, m_b_w_in, m_b_q_norm, m_b_k_norm, m_b_w_out, m_c_w_in, m_c_conv_w, m_c_conv_b, m_c_dt_bias, m_c_a_log, m_c_d, m_c_norm, m_c_w_out, m_d_w_in, m_d_lower_bounds, m_d_o_norm, m_d_w_out, v_mem_norm, v_mix_norm, v_xa_w_kv, v_xa_q_norm, v_xa_k_norm, v_ffn_norm, v_ffn_w_up, v_ffn_conv_w, v_ffn_conv_b, v_ffn_w_down, v_a_w_in, v_a_w_gate2, v_a_b_gate, v_a_o_norm, v_a_w_out, v_b_w_in, v_b_q_norm, v_b_k_norm, v_b_w_out, v_c_w_in, v_c_conv_w, v_c_conv_b, v_c_dt_bias, v_c_a_log, v_c_d, v_c_norm, v_c_w_out, v_d_w_in, v_d_lower_bounds, v_d_o_norm, v_d_w_out):
    args = locals()
    w = {n: args[n] for n in WEIGHTS}
    m = {n: args['m_' + n] for n in WEIGHTS}
    v = {n: args['v_' + n] for n in WEIGHTS}

    big = [n for n in SHARDED if w[n].size >= 65536]
    small = [n for n in SHARDED if n not in big]
    sh = {n: (w[n].astype(BF) if n in big else w[n]) for n in SHARDED}
    loss, grad_x, parts, G = _device_step(x[0], mem[0], positions[0], sh, {n: w[n] for n in REPLICATED}, loss_target[0])
    loss = lax.psum(loss, ("x", "y", "c"))
    rep_parts = _all_gather(_pack([G[n] for n in REPLICATED], F32), "gather_replicated_grads")

    out = {}

    def put(names, res, shapes):
        for kind, r in zip(("grad", "delta", "new_m", "new_v"), res):
            for n, t in zip(names, _unpack(r, shapes)):
                out[kind + "_" + n] = t

    for n in big:
        shp = tuple(w[n].shape)
        two_d = (math.prod(shp[:-1]), shp[-1])
        res = _adamw(parts[n].reshape((N_DEV,) + two_d), w[n].reshape(two_d), m[n].reshape(two_d), v[n].reshape(two_d),
                     "adamw")
        for kind, r in zip(("grad", "delta", "new_m", "new_v"), res):
            out[kind + "_" + n] = r.reshape(shp)
    for names, prt, tag in ((small, _pack_lead([parts[n] for n in small], F32), "adamw_small"),
                            (REPLICATED, rep_parts, "adamw_replicated")):
        res = _adamw(prt, _pack([w[n] for n in names], F32), _pack([m[n] for n in names], F32),
                     _pack([v[n] for n in names], F32), tag)
        put(names, res, [tuple(w[n].shape) for n in names])
    return (loss, grad_x[None], *[out["grad_" + n] for n in WEIGHTS], *[out["delta_" + n] for n in WEIGHTS],
            *[out["new_m_" + n] for n in WEIGHTS], *[out["new_v_" + n] for n in WEIGHTS])
```

```python
import functools
import math

import jax
import jax.numpy as jnp
from jax import lax
from jax.experimental import pallas as pl
from jax.experimental.pallas import tpu as pltpu

F32 = jnp.float32
BF = jnp.bfloat16
_MM_DTYPE = BF

N_DEV = 8
EPS = 1e-6
ROPE_THETA = 10000.0
CHUNK = 64
MIX_ROWS = 256
D_MIX = 768
XA_HEADS, XA_HD, D_XA = 4, 64, 256
GLA_HEADS, GLA_DK, GLA_DV, GLA_RANK, GLA_GATE_NORM = 4, 96, 192, 16, 16.0
DIL_GROUPS = ((128, 1), (512, 4), (2048, 16))
DIL_HEADS, DIL_HD, DIL_BLOCK, D_DIL = 4, 128, 128, 512
SSM_HD, SSM_HEADS, SSM_GROUPS, SSM_STATE, SSM_CONV = 64, 12, 2, 128, 4
HGRN_HEADS, HGRN_DK, HGRN_DV = 6, 128, 128
D_FF = 2816
FFN_CONV = 3
DEPTH = 4
ADAM_LR, ADAM_B1, ADAM_B2, ADAM_EPS, ADAM_WD, ADAM_STEP = 0.001, 0.9, 0.999, 1e-08, 0.01, 10
NEG = -1e30
HALO = 8
VMEM_LIMIT = 56 << 20
ADAM_BLOCK = 1 << 18

WEIGHTS = ['mem_norm', 'mix_norm', 'xa_w_kv', 'xa_q_norm', 'xa_k_norm', 'ffn_norm', 'ffn_w_up', 'ffn_conv_w',
           'ffn_conv_b', 'ffn_w_down', 'a_w_in', 'a_w_gate2', 'a_b_gate', 'a_o_norm', 'a_w_out', 'b_w_in', 'b_q_norm',
           'b_k_norm', 'b_w_out', 'c_w_in', 'c_conv_w', 'c_conv_b', 'c_dt_bias', 'c_a_log', 'c_d', 'c_norm', 'c_w_out',
           'd_w_in', 'd_lower_bounds', 'd_o_norm', 'd_w_out']
SHARD_AXIS = {'xa_w_kv': 1, 'ffn_w_up': 2, 'ffn_conv_w': 2, 'ffn_w_down': 1, 'a_w_in': 1, 'a_w_gate2': 1, 'a_w_out': 0,
              'b_w_in': 1, 'b_w_out': 1, 'c_w_in': 0, 'c_conv_w': 1, 'c_w_out': 0, 'd_w_in': 1, 'd_w_out': 0}
SHARDED = [n for n in WEIGHTS if n in SHARD_AXIS]
REPLICATED = [n for n in WEIGHTS if n not in SHARD_AXIS]


def _dot(a, b, ca, cb):
    return lax.dot_general(a.astype(_MM_DTYPE), b.astype(_MM_DTYPE), (((ca,), (cb,)), ((), ())),
                           preferred_element_type=F32)


@jax.custom_vjp
def mm_nn(a, b):
    return _dot(a, b, 1, 0)


mm_nn.defvjp(lambda a, b: (_dot(a, b, 1, 0), (a, b)),
             lambda r, g: (_dot(g, r[1], 1, 1), _dot(r[0], g, 0, 0)))


@jax.custom_vjp
def mm_nt(a, b):
    return _dot(a, b, 1, 1)


mm_nt.defvjp(lambda a, b: (_dot(a, b, 1, 1), (a, b)),
             lambda r, g: (_dot(g, r[1], 1, 0), _dot(g, r[0], 0, 0)))


@jax.custom_vjp
def mm_tn(a, b):
    return _dot(a, b, 0, 0)


mm_tn.defvjp(lambda a, b: (_dot(a, b, 0, 0), (a, b)),
             lambda r, g: (_dot(r[1], g, 1, 1), _dot(r[0], g, 1, 0)))


def _dot_hi(a, b, ca, cb):
    return lax.dot_general(a, b, (((ca,), (cb,)), ((), ())), precision=lax.Precision.HIGHEST,
                           preferred_element_type=F32)


def _tril(c):
    return lax.broadcasted_iota(jnp.int32, (c, c), 0) >= lax.broadcasted_iota(jnp.int32, (c, c), 1)


@jax.custom_vjp
def cumsum_rows(x):
    return _dot_hi(_tril(x.shape[0]).astype(F32), x, 1, 0)


cumsum_rows.defvjp(lambda x: (cumsum_rows(x), None),
                   lambda r, g: (_dot_hi(_tril(g.shape[0]).astype(F32), g, 0, 0),))


@jax.custom_vjp
def cumsum_rows_t(x):
    return _dot_hi(x, _tril(x.shape[0]).astype(F32), 0, 1)


cumsum_rows_t.defvjp(lambda x: (cumsum_rows_t(x), None),
                     lambda r, g: (_dot_hi(_tril(g.shape[1]).astype(F32), g, 0, 1),))


def _split(x, sizes):
    sizes = tuple(int(s) for s in sizes)
    assert sum(sizes) == x.shape[-1], (sizes, x.shape)

    @jax.custom_vjp
    def sp(x):
        out, o = [], 0
        for s in sizes:
            out.append(x[:, o:o + s])
            o += s
        return tuple(out)

    sp.defvjp(lambda x: (sp(x), None), lambda r, g: (jnp.concatenate(list(g), axis=1),))
    return sp(x)


def _row(x, r):
    m = lax.broadcasted_iota(jnp.int32, x.shape, 0) == r
    return jnp.sum(jnp.where(m, x, 0.0), axis=0, keepdims=True)


@jax.custom_vjp
def _roll_half(x):
    return pltpu.roll(x, 64, 1)


_roll_half.defvjp(lambda x: (pltpu.roll(x, 64, 1), None), lambda r, g: (pltpu.roll(g, 64, 1),))


def _shift(xp, x, d):
    if d == 0:
        return x
    n, m = x.shape[0], xp.shape[0]
    assert d <= m == HALO

    @jax.custom_vjp
    def sh(xp, x):
        r = pltpu.roll(x, d, 0)
        row = lax.broadcasted_iota(jnp.int32, xp.shape, 0)
        head = jnp.where(row < d, pltpu.roll(xp, d, 0), r[:m])
        return jnp.concatenate([head, r[m:]], axis=0)

    def bwd(_, g):
        row = lax.broadcasted_iota(jnp.int32, g.shape, 0)
        rowp = lax.broadcasted_iota(jnp.int32, (m,) + g.shape[1:], 0)
        dxp = jnp.where(rowp >= m - d, pltpu.roll(g[:m], m - d, 0), 0.0)
        return dxp, jnp.where(row < n - d, pltpu.roll(g, n - d, 0), 0.0)

    sh.defvjp(lambda xp, x: (sh(xp, x), None), bwd)
    return sh(xp, x)


def _rms(x, g):
    return x * lax.rsqrt(jnp.mean(x * x, axis=-1, keepdims=True) + EPS) * g


def _lane_pair(a, b, width=128):
    shape = a.shape[:-1] + (width,)
    lane = lax.broadcasted_iota(jnp.int32, shape, len(shape) - 1)
    return jnp.where(lane < width // 2, a, b)


def _row_spec(a, w=None, c=None, prev=False, diff=True, dn=None, lb=(), li=None):
    return dict(a=a, w=a.shape[-1] if w is None else w, c=(lambda jc: 0) if c is None else c, prev=prev, diff=diff,
                dn=a.shape[-1] if dn is None else dn, lb=tuple(lb), li=(lambda jc: ()) if li is None else li)


def _par_spec(a, bs=None, idx=None, diff=True):
    nd = a.ndim
    return dict(a=a, bs=tuple(a.shape) if bs is None else tuple(bs),
                idx=(lambda jc: (0,) * nd) if idx is None else idx, diff=diff)


def _out_spec(n, w=None, c=None, dt=F32, ls=(), lb=(), li=None):
    return dict(n=n, w=n if w is None else w, c=(lambda jc: 0) if c is None else c, dt=dt, ls=tuple(ls), lb=tuple(lb),
                li=(lambda jc: ()) if li is None else li)


def _cparams():
    return pltpu.CompilerParams(dimension_semantics=("arbitrary", "arbitrary"), vmem_limit_bytes=VMEM_LIMIT)


def _bspec(s, R, rowfn):
    return pl.BlockSpec(s['lb'] + (R, s['w']),
                        functools.partial(lambda jc, i, s: tuple(s['li'](jc)) + (rowfn(i), s['c'](jc)), s=s))


def _prev_rows(s, R):
    return R if s['prev'] == 'block' else HALO


def _pspec(s, R, blockfn):
    pr = _prev_rows(s, R)
    return pl.BlockSpec(s['lb'] + (pr, s['w']), functools.partial(
        lambda jc, i, s: tuple(s['li'](jc)) + (jnp.maximum(blockfn(i) * (R // pr) - 1, 0), s['c'](jc)), s=s))


def _seq_fwd(name, f, R, rows, params, state_shapes, outs, *, ncol=1, period=None, save_states=False, out_alias=None):
    nrows = rows[0]['a'].shape[-2]
    nb = nrows // R
    assert nb * R == nrows
    period = nb if period is None else period
    prev_ids = [k for k, r in enumerate(rows) if r['prev']]
    n_rows, n_prev, n_par, n_out, n_st = len(rows), len(prev_ids), len(params), len(outs), len(state_shapes)

    def body(*refs):
        o = 0
        cur = refs[o:o + n_rows]; o += n_rows
        prv = refs[o:o + n_prev]; o += n_prev
        par = refs[o:o + n_par]; o += n_par + len(out_alias or {})
        out = refs[o:o + n_out]; o += n_out
        sav = refs[o:o + (n_st if save_states else 0)]; o += len(sav)
        st = refs[o:o + n_st]
        i = pl.program_id(1)
        first = (i % period) == 0

        @pl.when(i == 0)
        def _():
            for s in st:
                s[...] = jnp.zeros_like(s)

        xs = [r[...].astype(F32) for r in cur]
        xp = [r[...].astype(F32) for r in prv]
        ps = [r[...] for r in par]
        sts = [s[...] for s in st]
        for sv, s in zip(sav, sts):
            sv[0] = s
        ov, ns = f(first, xp, xs, ps, sts)
        for r, v in zip(out, ov):
            r[...] = v.astype(r.dtype)
        for s, v in zip(st, ns):
            s[...] = v

    in_specs = [_bspec(r, R, lambda i: i) for r in rows]
    in_specs += [_pspec(rows[k], R, lambda i: i) for k in prev_ids]
    in_specs += [pl.BlockSpec(p['bs'], functools.partial(lambda jc, i, idx: idx(jc), idx=p['idx'])) for p in params]
    out_specs = [_bspec(o_, R, lambda i: i) for o_ in outs]
    out_shape = [jax.ShapeDtypeStruct(o_['ls'] + (nrows, o_['n']), o_['dt']) for o_ in outs]
    if save_states:
        for s in state_shapes:
            out_specs.append(pl.BlockSpec((1,) + tuple(s), lambda jc, i, nd=len(s): (i,) + (0,) * nd))
            out_shape.append(jax.ShapeDtypeStruct((nb,) + tuple(s), F32))
    args = [r['a'] for r in rows] + [rows[k]['a'] for k in prev_ids] + [p['a'] for p in params]
    aliases = {}
    for n_, arr in sorted((out_alias or {}).items()):
        assert arr.shape == out_shape[n_].shape and arr.dtype == out_shape[n_].dtype
        aliases[len(args)] = n_
        args.append(arr)
        in_specs.append(pl.BlockSpec(memory_space=pl.ANY))
    res = pl.pallas_call(
        body, name=name, grid=(ncol, nb), in_specs=in_specs, out_specs=out_specs, out_shape=out_shape,
        scratch_shapes=[pltpu.VMEM(tuple(s), F32) for s in state_shapes], input_output_aliases=aliases,
        compiler_params=_cparams())(*args)
    return list(res[:n_out]), list(res[n_out:])


def _seq_bwd(name, f, R, rows, params, state_shapes, douts, saved, *, ncol=1, period=None, dx_dt=F32, dx_add=None,
             dx_bf=False, dx_alias=None):
    nrows = rows[0]['a'].shape[-2]
    nb = nrows // R
    period = nb if period is None else period
    prev_ids = [k for k, r in enumerate(rows) if r['prev']]
    drow_ids = [k for k, r in enumerate(rows) if r['diff']]
    dpar_ids = [k for k, p in enumerate(params) if p['diff']]
    for k in prev_ids:
        assert rows[k]['diff']
    dx_add, dx_alias = dict(dx_add or {}), dict(dx_alias or {})
    add_ids, alias_ids = sorted(dx_add), sorted(dx_alias)
    n_rows, n_prev, n_par, n_do, n_st = len(rows), len(prev_ids), len(params), len(douts), len(state_shapes)
    n_dx, n_dp, n_add, n_al = len(drow_ids), len(dpar_ids), len(add_ids), len(alias_ids)

    def body(*refs):
        o = 0
        cur = refs[o:o + n_rows]; o += n_rows
        prv = refs[o:o + n_prev]; o += n_prev
        par = refs[o:o + n_par]; o += n_par
        sav = refs[o:o + n_st]; o += n_st
        dou = refs[o:o + n_do]; o += n_do
        adr = refs[o:o + n_add]; o += n_add
        o += n_al
        dxr = refs[o:o + n_dx]; o += n_dx
        dpr = refs[o:o + n_dp]; o += n_dp
        dxb = refs[o:o + (n_dx if dx_bf else 0)]; o += len(dxb)
        dst = refs[o:o + n_st]; o += n_st
        car = refs[o:o + n_prev]
        j = pl.program_id(1)
        i = nb - 1 - j
        first = (i % period) == 0

        @pl.when(j == 0)
        def _():
            for s in tuple(dst) + tuple(car) + tuple(dpr):
                s[...] = jnp.zeros_like(s)

        xs = [r[...].astype(F32) for r in cur]
        xp = [r[...].astype(F32) for r in prv]
        ps = [r[...] for r in par]
        sts = [s[0] for s in sav]

        def g(dxs, dxp, dps, dsts):
            xs_, ps_ = list(xs), list(ps)
            for k, v in zip(drow_ids, dxs):
                xs_[k] = v
            for k, v in zip(dpar_ids, dps):
                ps_[k] = v
            ov, ns = f(first, list(dxp), xs_, ps_, list(dsts))
            return tuple(ov), tuple(ns)

        _, vjp = jax.vjp(g, tuple(xs[k] for k in drow_ids), tuple(xp), tuple(ps[k] for k in dpar_ids), tuple(sts))
        dxs, dxp, dps, dsts = vjp((tuple(r[...].astype(F32) for r in dou), tuple(s[...] for s in dst)))
        dxs = list(dxs)
        for n_, pos in enumerate(add_ids):
            dxs[pos] = dxs[pos] + adr[n_][...].astype(F32)
        tails = {}
        for n_, k in enumerate(prev_ids):
            pos = drow_ids.index(k)
            if rows[k]['prev'] == 'block':
                dxs[pos] = dxs[pos] + car[n_][...]
            else:
                tails[pos] = car[n_][...]
            car[n_][...] = dxp[n_]
        for pos, v in enumerate(dxs):
            outs_ = [dxr[pos]] + ([dxb[pos]] if dx_bf else [])
            if pos in tails:
                v = jnp.concatenate([v[..., :R - HALO, :], v[..., R - HALO:, :] + tails[pos]], axis=-2)
            for r in outs_:
                r[...] = v.astype(r.dtype)
        for r, v in zip(dpr, dps):
            r[...] += v
        for s, v in zip(dst, dsts):
            s[...] = v

    def rev(j):
        return nb - 1 - j

    def dspec(k):
        return _bspec(rows[k], R, rev)

    in_specs = [_bspec(r, R, rev) for r in rows]
    in_specs += [_pspec(rows[k], R, rev) for k in prev_ids]
    in_specs += [pl.BlockSpec(p['bs'], functools.partial(lambda jc, j, idx: idx(jc), idx=p['idx'])) for p in params]
    in_specs += [pl.BlockSpec((1,) + tuple(s), lambda jc, j, nd=len(s): (nb - 1 - j,) + (0,) * nd) for s in state_shapes]
    in_specs += [_bspec(d, R, rev) for d in douts]
    in_specs += [dspec(drow_ids[pos]) for pos in add_ids]
    in_specs += [pl.BlockSpec(memory_space=pl.ANY) for _ in alias_ids]
    out_specs = [dspec(k) for k in drow_ids]
    out_shape = [jax.ShapeDtypeStruct(tuple(rows[k]['a'].shape[:-1]) + (rows[k]['dn'],), dx_dt) for k in drow_ids]
    for k in dpar_ids:
        p = params[k]
        out_specs.append(pl.BlockSpec(p['bs'], functools.partial(lambda jc, j, idx: idx(jc), idx=p['idx'])))
        out_shape.append(jax.ShapeDtypeStruct(p['a'].shape, F32))
    if dx_bf:
        out_specs += [dspec(k) for k in drow_ids]
        out_shape += [jax.ShapeDtypeStruct(tuple(rows[k]['a'].shape[:-1]) + (rows[k]['dn'],), BF) for k in drow_ids]
    scratch = [pltpu.VMEM(tuple(s), F32) for s in state_shapes]
    scratch += [pltpu.VMEM(tuple(d for d in rows[k]['lb'] if d is not None) + (_prev_rows(rows[k], R), rows[k]['w']), F32)
                for k in prev_ids]
    args = ([r['a'] for r in rows] + [rows[k]['a'] for k in prev_ids] + [p['a'] for p in params] + list(saved)
            + [d['a'] for d in douts] + [dx_add[pos] for pos in add_ids] + [dx_alias[pos] for pos in alias_ids])
    n_in = len(args)
    aliases = {n_in - n_al + n_: pos for n_, pos in enumerate(alias_ids)}
    for pos in alias_ids:
        assert dx_alias[pos].shape == out_shape[pos].shape and dx_alias[pos].dtype == out_shape[pos].dtype
    res = pl.pallas_call(
        body, name=name, grid=(ncol, nb), in_specs=in_specs, out_specs=out_specs, out_shape=out_shape,
        scratch_shapes=scratch, input_output_aliases=aliases, compiler_params=_cparams())(*args)
    if dx_bf:
        return list(res[:n_dx]), list(res[n_dx:n_dx + n_dp]), list(res[n_dx + n_dp:])
    return list(res[:n_dx]), list(res[n_dx:])


def _tile(n, cands):
    for c in cands:
        if n % c == 0:
            return c
    return n


def _mm_call(name, grid, a, a_spec, b, b_spec, contract, out_shape, out_spec, acc_shape, add=None, add_spec=None,
             exchange=None, gather=None):
    nk = grid[2]
    ca, cb = contract
    has_add = add is not None
    ex = list(exchange or []) + list(gather or [])
    ex_shapes = [g.shape for g in exchange or []] + [(N_DEV,) + tuple(g.shape) for g in gather or []]
    n_ex = len(ex)
    n_in = 2 + has_add

    def body(*refs):
        a_ref, b_ref = refs[0], refs[1]
        add_ref = refs[2] if has_add else None
        o_ref = refs[n_in + n_ex]
        scr = refs[n_in + 2 * n_ex + 1:]
        step = [pl.program_id(d) for d in range(3)]
        if n_ex:
            g_refs, r_refs, sems = refs[n_in:n_in + n_ex], refs[n_in + n_ex + 1:n_in + 2 * n_ex + 1], scr[-3:]

            @pl.when((step[0] == 0) & (step[1] == 0) & (step[2] == 0))
            def _():
                _exchange_start(g_refs, r_refs, *sems)

        part = _dot(a_ref[...], b_ref[...], ca, cb)

        def finish(r):
            if has_add:
                r = r + add_ref[...].astype(F32)
            o_ref[...] = r.astype(o_ref.dtype)

        if nk == 1:
            finish(part)
        else:
            acc = scr[0]

            @pl.when(step[2] == 0)
            def _():
                acc[...] = part

            @pl.when(step[2] > 0)
            def _():
                acc[...] += part

            @pl.when(step[2] == nk - 1)
            def _():
                finish(acc[...])

        if n_ex:
            @pl.when((step[0] == grid[0] - 1) & (step[1] == grid[1] - 1) & (step[2] == grid[2] - 1))
            def _():
                _exchange_wait(g_refs, r_refs, *sems)

    in_specs, args = [a_spec, b_spec], [a, b]
    if has_add:
        in_specs.append(add_spec)
        args.append(add)
    any_spec = pl.BlockSpec(memory_space=pl.ANY)
    scratch = [] if nk == 1 else [pltpu.VMEM(acc_shape, F32)]
    if n_ex:
        scratch += _exchange_sems(n_ex)
    res = pl.pallas_call(
        body, name=name, grid=grid, in_specs=in_specs + [any_spec] * n_ex, out_specs=[out_spec] + [any_spec] * n_ex,
        out_shape=[out_shape] + [jax.ShapeDtypeStruct(s, g.dtype) for s, g in zip(ex_shapes, ex)], scratch_shapes=scratch,
        compiler_params=pltpu.CompilerParams(
            dimension_semantics=("arbitrary",) * 3 if n_ex else ("parallel", "parallel", "arbitrary"),
            vmem_limit_bytes=VMEM_LIMIT, has_side_effects=bool(n_ex)))(*args, *ex)
    return (res[0], list(res[1:])) if n_ex else res[0]


def _matmul(a, b, mode="nn", add=None, out_dtype=F32, name="matmul", **pushed):
    if mode == "nn":
        (M, K), N = a.shape, b.shape[1]
    elif mode == "nt":
        (M, K), N = a.shape, b.shape[0]
    else:
        (K, M), N = a.shape, b.shape[1]
    if mode == "tn" and 1024 < N <= 5120:
        tm, tn = _tile(M, (256, 128, 64, 32, 16, 8)), N
        tk = _tile(K, (2048 if tn <= 3072 else 1024, 1024, 512, 256, 128))
    else:
        tk = K if K <= 5120 else _tile(K, (2048, 1024, 512, 256, 128))
        tm = _tile(M, ((2048,) if tk <= 1024 and mode != "tn" else ()) + (1024, 512, 256, 128, 64, 32, 16, 8))
        tn = _tile(N, (512, 256, 128))
    if mode == "tn":
        a_spec = pl.BlockSpec((tk, tm), lambda i, j, k: (k, i))
    else:
        a_spec = pl.BlockSpec((tm, tk), lambda i, j, k: (i, k))
    if mode == "nt":
        b_spec = pl.BlockSpec((tn, tk), lambda i, j, k: (j, k))
    else:
        b_spec = pl.BlockSpec((tk, tn), lambda i, j, k: (k, j))
    blk = pl.BlockSpec((tm, tn), lambda i, j, k: (i, j))
    return _mm_call(name, (M // tm, N // tn, K // tk), a, a_spec, b, b_spec,
                    {"nn": (1, 0), "nt": (1, 1), "tn": (0, 0)}[mode], jax.ShapeDtypeStruct((M, N), out_dtype), blk,
                    (tm, tn), add, blk, **pushed)


FF_SH = 2 * D_FF // N_DEV


def _ffn_up(h2, wup, **pushed):
    S, D = h2.shape
    tm = _tile(S, (2048, 1024, 512, 256, 128))
    return _mm_call("matmul_up", (S // tm, N_DEV, 1), h2, pl.BlockSpec((tm, D), lambda m, j, k: (m, 0)),
                    wup, pl.BlockSpec((None, D, FF_SH), lambda m, j, k: (j, 0, 0)), (1, 0),
                    jax.ShapeDtypeStruct((2, N_DEV // 2, S, FF_SH), F32),
                    pl.BlockSpec((None, None, tm, FF_SH), lambda m, j, k: (j // 4, j % 4, m, 0)), (tm, FF_SH), **pushed)


def _ffn_down(act, wd, x1, **pushed):
    _, S, _ = act.shape
    D = wd.shape[1]
    tm, tn = _tile(S, (1024, 512, 256, 128)), _tile(D, (512, 256, 128))
    blk = pl.BlockSpec((tm, tn), lambda m, n, k: (m, n))
    return _mm_call("matmul_down", (S // tm, D // tn, N_DEV // 2), act,
                    pl.BlockSpec((None, tm, FF_SH), lambda m, n, k: (k, m, 0)), wd,
                    pl.BlockSpec((FF_SH, tn), lambda m, n, k: (k, n)), (1, 0), jax.ShapeDtypeStruct((S, D), F32), blk,
                    (tm, tn), x1, blk, **pushed)


def _ffn_dact(dxb, wd):
    S, D = dxb.shape
    tm = _tile(S, (2048, 1024, 512, 256, 128))
    return _mm_call("matmul_dact", (S // tm, N_DEV // 2, 1), dxb, pl.BlockSpec((tm, D), lambda m, j, k: (m, 0)), wd,
                    pl.BlockSpec((FF_SH, D), lambda m, j, k: (j, 0)), (1, 1),
                    jax.ShapeDtypeStruct((N_DEV // 2, S, FF_SH), BF),
                    pl.BlockSpec((None, tm, FF_SH), lambda m, j, k: (j, m, 0)), (tm, FF_SH))


def _ffn_dw_down(act, dxb):
    _, S, _ = act.shape
    D = dxb.shape[1]
    tk, tn = _tile(S, (2048, 1024, 512, 256, 128)), _tile(D, (512, 256, 128))
    return _mm_call("matmul_dw_down", (N_DEV // 2, D // tn, S // tk), act,
                    pl.BlockSpec((None, tk, FF_SH), lambda j, n, k: (j, k, 0)), dxb,
                    pl.BlockSpec((tk, tn), lambda j, n, k: (k, n)), (0, 0), jax.ShapeDtypeStruct((D_FF, D), BF),
                    pl.BlockSpec((FF_SH, tn), lambda j, n, k: (j, n)), (FF_SH, tn))


def _ffn_dw_up(h2, du, exchange=None):
    S, D = h2.shape
    tk = _tile(S, (1024, 512, 256, 128))
    return _mm_call("matmul_dw_up", (N_DEV, 1, S // tk), h2, pl.BlockSpec((tk, D), lambda j, n, k: (k, 0)), du,
                    pl.BlockSpec((None, None, tk, FF_SH), lambda j, n, k: (j // 4, j % 4, k, 0)), (0, 0),
                    jax.ShapeDtypeStruct((N_DEV, D, FF_SH), BF),
                    pl.BlockSpec((None, D, FF_SH), lambda j, n, k: (j, 0, 0)), (D, FF_SH), exchange=exchange)


def _ffn_dh2(du, wup, exchange=None):
    S = du.shape[2]
    D = wup.shape[1]
    tm = _tile(S, (1024, 512, 256, 128))
    return _mm_call("matmul_dh2", (S // tm, 1, N_DEV), du,
                    pl.BlockSpec((None, None, tm, FF_SH), lambda m, n, k: (k // 4, k % 4, m, 0)), wup,
                    pl.BlockSpec((None, D, FF_SH), lambda m, n, k: (k, 0, 0)), (1, 1),
                    jax.ShapeDtypeStruct((S, D), F32), pl.BlockSpec((tm, D), lambda m, n, k: (m, 0)), (tm, D),
                    exchange=exchange)


def f_rmsnorm(first, xp, xs, ps, sts):
    return (_rms(xs[0], ps[0]),), ()


def _same_block(shape, rows_per, cols_per):
    return (lax.broadcasted_iota(jnp.int32, shape, 0) // rows_per) == (lax.broadcasted_iota(jnp.int32, shape, 1) // cols_per)


@jax.custom_vjp
def _head_mean(x):
    n = x.shape[1]
    return _dot_hi(x, jnp.where(_same_block((n, n), XA_HD, XA_HD), 1.0 / XA_HD, 0.0), 1, 0)


_head_mean.defvjp(lambda x: (_head_mean(x), None), lambda r, g: (_head_mean(g),))


def f_xattn(first, xp, xs, ps, sts):
    (xq,), (kv, qn, kn) = xs, ps
    k, v = _split(kv, [D_XA, D_XA])
    m_rows = kv.shape[0]
    q = xq * lax.rsqrt(_head_mean(xq * xq) + EPS) * jnp.concatenate([qn] * XA_HEADS, axis=1)
    k = k * lax.rsqrt(_head_mean(k * k) + EPS) * jnp.concatenate([kn] * XA_HEADS, axis=1)
    kt = k.T
    kbd = jnp.where(_same_block((D_XA, XA_HEADS * m_rows), XA_HD, m_rows), jnp.concatenate([kt] * XA_HEADS, axis=1), 0.0)
    s = mm_nn(q, kbd) * (XA_HD ** -0.5)
    ps_ = []
    for sh in _split(s, [m_rows] * XA_HEADS):
        mx = lax.stop_gradient(jnp.max(sh, axis=-1, keepdims=True))
        p = jnp.exp(sh - mx)
        ps_.append(p / jnp.sum(p, axis=-1, keepdims=True))
    vbd = jnp.where(_same_block((XA_HEADS * m_rows, D_XA), m_rows, XA_HD), jnp.concatenate([v] * XA_HEADS, axis=0), 0.0)
    return (mm_nn(jnp.concatenate(ps_, axis=1), vbd),), ()


def _conv(xp, x, w, b, first, taps):
    xp = jnp.where(first, 0.0, xp)
    y = b + w[taps - 1:taps] * x
    for d in range(1, taps):
        y = y + w[taps - 1 - d:taps - d] * _shift(xp, x, d)
    return y


def _unstack2(x):
    @jax.custom_vjp
    def us(x):
        return x[0], x[1]

    us.defvjp(lambda x: (us(x), None), lambda r, g: (jnp.stack(g),))
    return us(x)


def f_ffn_act(first, xp, xs, ps, sts):
    (up,), (u,), (wg, wv, bg, bv) = xp, xs, ps
    (ugp, uvp), (ug, uv) = _unstack2(up), _unstack2(u)
    gate = _conv(ugp, ug, wg, bg, first, FFN_CONV)
    val = _conv(uvp, uv, wv, bv, first, FFN_CONV)
    return (jax.nn.silu(gate) * val,), ()


def _gla_chunk(q, k, v, la, st):
    c = q.shape[0]
    b = cumsum_rows(la)
    b_last = _row(b, c - 1)
    b_ref = _row(b, c // 2 - 1)
    att = mm_nt(q * jnp.exp(b - b_ref), k * jnp.exp(b_ref - b))
    att = jnp.where(_tril(c), att, 0.0)
    o = mm_nn(att, v) + mm_nt(q * jnp.exp(b), st)
    st_new = st * jnp.exp(b_last) + mm_tn(v, k * jnp.exp(b_last - b))
    return o, st_new


def _split_rows(x, n):
    c = x.shape[0] // n

    @jax.custom_vjp
    def sp(x):
        return tuple(x[i * c:(i + 1) * c] for i in range(n))

    sp.defvjp(lambda x: (sp(x), None), lambda r, g: (jnp.concatenate(list(g), axis=0),))
    return sp(x)


def _gla_scan(q, k, v, la, st):
    n = q.shape[0] // CHUNK
    if n == 1:
        return _gla_chunk(q, k, v, la, st)
    outs = []
    for qc, kc, vc, lc in zip(*(_split_rows(t, n) for t in (q, k, v, la))):
        o, st = _gla_chunk(qc, kc, vc, lc, st)
        outs.append(o)
    return jnp.concatenate(outs, axis=0), st


def _a_cols(ntot):
    used = D_XA + 2 * GLA_HEADS * GLA_DK + D_MIX + GLA_RANK + D_MIX
    return [D_XA, GLA_HEADS * GLA_DK, GLA_HEADS * GLA_DK, D_MIX, GLA_RANK, D_MIX] + ([ntot - used] if ntot > used else [])


def f_gla(first, xp, xs, ps, sts):
    (p,), (wg2, bg, on) = xs, ps
    parts = _split(p, _a_cols(p.shape[1]))
    q, k, v, glr, og = parts[1:6]
    la = jax.nn.log_sigmoid(mm_nn(glr, wg2) + bg) / GLA_GATE_NORM
    qs = _split(q * (GLA_DK ** -0.5), [GLA_DK] * GLA_HEADS)
    ks = _split(k, [GLA_DK] * GLA_HEADS)
    vs = _split(v, [GLA_DV] * GLA_HEADS)
    las = _split(la, [GLA_DK] * GLA_HEADS)
    outs, new = [], []
    for h in range(GLA_HEADS):
        o, s = _gla_scan(qs[h], ks[h], vs[h], las[h], sts[h])
        outs.append(_rms(o, on))
        new.append(s)
    return (jnp.concatenate(outs, axis=1) * jax.nn.silu(og),), tuple(new)


def f_hgrn(first, xp, xs, ps, sts):
    (p,), (lbp, on) = xs, ps
    _, q, fgate, iv, og = _split(p, [D_XA, D_MIX, D_MIX, D_MIX, D_MIX])
    e = jnp.exp(lbp - jnp.max(lbp, axis=0, keepdims=True))
    row = lax.broadcasted_iota(jnp.int32, e.shape, 0)
    lb = jnp.sum(jnp.where(row >= 1, e, 0.0), axis=0, keepdims=True) / jnp.sum(e, axis=0, keepdims=True)
    fg = lb + (1.0 - lb) * jax.nn.sigmoid(fgate)
    qs = _split(jax.nn.silu(q), [HGRN_DK] * HGRN_HEADS)
    ks = _split(1.0 - fg, [HGRN_DK] * HGRN_HEADS)
    vs = _split(iv, [HGRN_DV] * HGRN_HEADS)
    las = _split(jnp.log(fg), [HGRN_DK] * HGRN_HEADS)
    outs, new = [], []
    for h in range(HGRN_HEADS):
        o, s = _gla_scan(qs[h], ks[h], vs[h], las[h], sts[h])
        outs.append(_rms(o, on))
        new.append(s)
    return (jnp.concatenate(outs, axis=1) * jax.nn.sigmoid(og),), tuple(new)


def _c_cols(ntot):
    gn = SSM_GROUPS * SSM_STATE
    used = D_XA + D_MIX + D_MIX + 2 * gn + SSM_HEADS
    return [D_XA, D_MIX, D_MIX + 2 * gn, SSM_HEADS] + ([ntot - used] if ntot > used else [])


def f_ssd(first, xp, xs, ps, sts):
    (pp,), (p,), (cw, cb, dtb, alog, dsk, ng) = xp, xs, ps
    gn = SSM_GROUPS * SSM_STATE
    _, z, xbc, dtr = _split(p, _c_cols(p.shape[1]))[:4]
    xbc_p = _split(pp, _c_cols(p.shape[1]))[2]
    xbc = jax.nn.silu(_conv(xbc_p, xbc, cw, cb, first, SSM_CONV))
    xs_, bm, cm = _split(xbc, [D_MIX, gn, gn])
    dt = jax.nn.softplus(dtr + dtb)
    n = p.shape[0] // CHUNK
    ys, sts = [], tuple(sts)
    for xc, bc, cc, dc in zip(*(_split_rows(t, n) for t in (xs_, bm, cm, dt))):
        y, sts = _ssd_chunk(xc, bc, cc, dc, alog, dsk, sts)
        ys.append(y)
    y = jnp.concatenate(ys, axis=0) * jax.nn.silu(z)
    gw = D_MIX // SSM_GROUPS
    yg = _split(y, [gw] * SSM_GROUPS)
    ngs = _split(ng, [gw] * SSM_GROUPS)
    y = jnp.concatenate([_rms(yg[g], ngs[g]) for g in range(SSM_GROUPS)], axis=1)
    return (y,), sts


def _ssd_chunk(xs_, bm, cm, dt, alog, dsk, sts):
    c = xs_.shape[0]
    hg = SSM_HEADS // SSM_GROUPS
    a = dt * (-jnp.exp(alog))
    acs = cumsum_rows(a)
    acs_t = cumsum_rows_t(a)
    acs_last = _row(acs, c - 1)
    dt_h = _split(dt, [1] * SSM_HEADS)
    acs_h = _split(acs, [1] * SSM_HEADS)
    al_h = _split(acs_last, [1] * SSM_HEADS)
    d_h = _split(dsk, [1] * SSM_HEADS)
    x2s = _split(xs_, [2 * SSM_HD] * (SSM_HEADS // 2))
    bms = _split(bm, [SSM_STATE] * SSM_GROUPS)
    cms = _split(cm, [SSM_STATE] * SSM_GROUPS)
    tril = _tril(c)
    cbs = [mm_nt(cms[g], bms[g]) for g in range(SSM_GROUPS)]
    ys, new = [], []
    for j in range(SSM_HEADS // 2):
        g = (2 * j) // hg
        h0, h1 = 2 * j, 2 * j + 1
        xdt = x2s[j] * _lane_pair(dt_h[h0], dt_h[h1])
        acs2 = _lane_pair(acs_h[h0], acs_h[h1])
        al2 = _lane_pair(al_h[h0], al_h[h1])
        yd = []
        for h in (h0, h1):
            seg = acs_h[h] - _row(acs_t, h)
            lm = jnp.exp(jnp.where(tril, seg, NEG))
            yd.append(mm_nn(cbs[g] * lm, xdt))
        lane = lax.broadcasted_iota(jnp.int32, xdt.shape, 1)
        y_diag = jnp.where(lane < SSM_HD, yd[0], yd[1])
        y_off = mm_nn(cms[g], sts[j]) * jnp.exp(acs2)
        x_end = xdt * jnp.exp(al2 - acs2)
        new.append(sts[j] * jnp.exp(al2) + mm_tn(bms[g], x_end))
        ys.append(y_diag + y_off + _lane_pair(d_h[h0], d_h[h1]) * x2s[j])
    return jnp.concatenate(ys, axis=1), tuple(new)


def f_dil_prep(first, xp, xs, ps, sts):
    (p, pos), (qn, kn, invf, sign) = xs, ps
    nh = len(DIL_GROUPS) * DIL_HEADS
    _, q, k, v = _split(p, [D_XA] + [nh * DIL_HD] * 3)
    ang = pos * invf
    cos, sin = jnp.cos(ang), jnp.sin(ang) * sign

    def rope(t, g):
        hs = _split(t, [DIL_HD] * nh)
        out = []
        for h in hs:
            n = _rms(h, g)
            out.append(n * cos + _roll_half(n) * sin)
        return [jnp.concatenate(out[i:i + DIL_HEADS], axis=1) for i in range(0, nh, DIL_HEADS)]

    return tuple(rope(q, qn) + rope(k, kn) + list(_split(v, [D_DIL] * len(DIL_GROUPS)))), ()


def f_dil_attn(first, xp, xs, ps, sts):
    (kp, vp), (q, k, v) = xp, xs
    Q = DIL_BLOCK
    qs, ks, vs = (_split(t, [DIL_HD] * DIL_HEADS) for t in (q, k, v))
    kps, vps = (_split(t, [DIL_HD] * DIL_HEADS) for t in (kp, vp))
    i = lax.broadcasted_iota(jnp.int32, (Q, 2 * Q), 0)
    j = lax.broadcasted_iota(jnp.int32, (Q, 2 * Q), 1)
    dist = Q + i - j
    mask = (dist >= 0) & (dist <= Q) & (jnp.logical_not(first) | (j >= Q))
    outs, lses = [], []
    for h in range(DIL_HEADS):
        k2 = jnp.concatenate([kps[h], ks[h]], axis=0)
        v2 = jnp.concatenate([vps[h], vs[h]], axis=0)
        s = jnp.where(mask, mm_nt(qs[h], k2) * (DIL_HD ** -0.5), NEG)
        m = lax.stop_gradient(jnp.max(s, axis=-1, keepdims=True))
        p = jnp.exp(s - m)
        l = jnp.sum(p, axis=-1, keepdims=True)
        outs.append(mm_nn(p / l, v2))
        lses.append(jnp.broadcast_to(m + jnp.log(l), (Q, DIL_HD)))
    return (jnp.concatenate(outs, axis=1), jnp.concatenate(lses, axis=1)), ()


def f_dil_merge(first, xp, xs, ps, sts):
    o0, o1, o2, l0, l1, l2 = xs
    m = jnp.maximum(jnp.maximum(l0, l1), l2)
    e0, e1, e2 = jnp.exp(l0 - m), jnp.exp(l1 - m), jnp.exp(l2 - m)
    den = e0 + e1 + e2
    return ((e0 * o0 + e1 * o1 + e2 * o2) / den,), ()


def _loss_head(y, target):
    S, D = y.shape
    R = _tile(S, (512, 256, 128, 64, 32, 16, 8))

    def body(y_ref, t_ref, dy_ref, dyb_ref, l_ref):
        e = y_ref[...] - t_ref[...]
        dy_ref[...] = e * (1.0 / D)
        dyb_ref[...] = (e * (1.0 / D)).astype(BF)

        @pl.when(pl.program_id(0) == 0)
        def _():
            l_ref[...] = jnp.zeros_like(l_ref)

        l_ref[...] += jnp.broadcast_to(0.5 * jnp.sum(jnp.mean(e * e, axis=-1, keepdims=True), axis=0, keepdims=True),
                                       l_ref.shape)

    blk = pl.BlockSpec((R, D), lambda i: (i, 0))
    dy, dyb, l = pl.pallas_call(
        body, name="loss_head", grid=(S // R,), in_specs=[blk, blk],
        out_specs=[blk, blk, pl.BlockSpec((8, 128), lambda i: (0, 0))],
        out_shape=[jax.ShapeDtypeStruct((S, D), F32), jax.ShapeDtypeStruct((S, D), BF),
                   jax.ShapeDtypeStruct((8, 128), F32)],
        compiler_params=pltpu.CompilerParams(dimension_semantics=("arbitrary",)))(y, target)
    return dy, dyb, l[0, 0]


def _adamw(parts, w, m, v, name):
    _, n, width = parts.shape
    tr = _tile(n, [t for t in (512, 256, 128, 64, 32, 16, 8) if t * width <= ADAM_BLOCK])

    def body(p_ref, w_ref, m_ref, v_ref, g_ref, d_ref, nm_ref, nv_ref):
        g = p_ref[0].astype(F32)
        for s in range(1, N_DEV):
            g = g + p_ref[s].astype(F32)
        nm = ADAM_B1 * m_ref[...] + (1.0 - ADAM_B1) * g
        nv = ADAM_B2 * v_ref[...] + (1.0 - ADAM_B2) * (g * g)
        m_hat = nm / (1.0 - ADAM_B1 ** ADAM_STEP)
        v_hat = nv / (1.0 - ADAM_B2 ** ADAM_STEP)
        g_ref[...] = g
        d_ref[...] = -ADAM_LR * (m_hat / (jnp.sqrt(v_hat) + ADAM_EPS) + ADAM_WD * w_ref[...])
        nm_ref[...] = nm
        nv_ref[...] = nv

    blk = pl.BlockSpec((tr, width), lambda i: (i, 0))
    return pl.pallas_call(
        body, name=name, grid=(n // tr,),
        in_specs=[pl.BlockSpec((N_DEV, tr, width), lambda i: (0, i, 0)), blk, blk, blk],
        out_specs=[blk] * 4, out_shape=[jax.ShapeDtypeStruct((n, width), F32)] * 4,
        compiler_params=pltpu.CompilerParams(dimension_semantics=("arbitrary",), vmem_limit_bytes=VMEM_LIMIT))(
            parts, w, m, v)


def _peer(k):
    x, y, c = lax.axis_index("x"), lax.axis_index("y"), lax.axis_index("c")
    px = 1 - x if k & 4 else x
    py = 1 - y if k & 2 else y
    pc = 1 - c if k & 1 else c
    return (px, py, pc), 4 * px + 2 * py + pc


def _my_id():
    return 4 * lax.axis_index("x") + 2 * lax.axis_index("y") + lax.axis_index("c")


def _all_gather(x, name):
    def body(x_ref, out_ref, send, recv, loc):
        me = _my_id()
        mine = pltpu.make_async_copy(x_ref, out_ref.at[me], loc)
        mine.start()
        cps = []
        for k in range(1, N_DEV):
            peer, _ = _peer(k)
            cp = pltpu.make_async_remote_copy(src_ref=x_ref, dst_ref=out_ref.at[me], send_sem=send.at[k - 1],
                                              recv_sem=recv.at[k - 1], device_id=peer,
                                              device_id_type=pl.DeviceIdType.MESH)
            cp.start()
            cps.append(cp)
        for k in range(1, N_DEV):
            peer, pid = _peer(k)
            pltpu.make_async_remote_copy(src_ref=x_ref, dst_ref=out_ref.at[pid], send_sem=send.at[k - 1],
                                         recv_sem=recv.at[k - 1], device_id=peer,
                                         device_id_type=pl.DeviceIdType.MESH).wait_recv()
        for cp in cps:
            cp.wait_send()
        mine.wait()

    return pl.pallas_call(
        body, name=name, out_shape=jax.ShapeDtypeStruct((N_DEV,) + x.shape, x.dtype),
        in_specs=[pl.BlockSpec(memory_space=pl.ANY)], out_specs=pl.BlockSpec(memory_space=pl.ANY),
        scratch_shapes=[pltpu.SemaphoreType.DMA((N_DEV - 1,)), pltpu.SemaphoreType.DMA((N_DEV - 1,)),
                        pltpu.SemaphoreType.DMA],
        compiler_params=pltpu.CompilerParams(has_side_effects=True))(x)


def _exchange_sems(n):
    return [pltpu.SemaphoreType.DMA((n * (N_DEV - 1),)), pltpu.SemaphoreType.DMA((n * (N_DEV - 1),)),
            pltpu.SemaphoreType.DMA((n,))]


def _exchange_copies(g_refs, out_refs, send, recv, loc, with_arrivals):
    me = _my_id()

    def mine(g, o, d):
        return g.at[d] if len(g.shape) == len(o.shape) else g

    local = [pltpu.make_async_copy(mine(g, o, me), o.at[me], loc.at[w]) for w, (g, o) in enumerate(zip(g_refs, out_refs))]
    pushes, arrivals = [], []
    for k in range(1, N_DEV):
        peer, pid = _peer(k)
        for w, (g, o) in enumerate(zip(g_refs, out_refs)):
            s = w * (N_DEV - 1) + k - 1
            ends = [(mine(g, o, pid), o.at[me], pushes)] + ([(mine(g, o, me), o.at[pid], arrivals)] if with_arrivals else [])
            for src, dst, into in ends:
                into.append(pltpu.make_async_remote_copy(src_ref=src, dst_ref=dst, send_sem=send.at[s],
                                                         recv_sem=recv.at[s], device_id=peer,
                                                         device_id_type=pl.DeviceIdType.MESH))
    return local, pushes, arrivals


def _exchange_start(g_refs, out_refs, send, recv, loc):
    local, pushes, _ = _exchange_copies(g_refs, out_refs, send, recv, loc, False)
    for cp in local + pushes:
        cp.start()


def _exchange_wait(g_refs, out_refs, send, recv, loc):
    local, pushes, arrivals = _exchange_copies(g_refs, out_refs, send, recv, loc, True)
    for cp in arrivals:
        cp.wait_recv()
    for cp in pushes:
        cp.wait_send()
    for cp in local:
        cp.wait()


def _exchange_many(gs, name):
    n = len(gs)

    def body(*refs):
        g_refs, out_refs, sems = refs[:n], refs[n:2 * n], refs[2 * n:]
        _exchange_start(g_refs, out_refs, *sems)
        _exchange_wait(g_refs, out_refs, *sems)

    return pl.pallas_call(
        body, name=name, out_shape=[jax.ShapeDtypeStruct(g.shape, g.dtype) for g in gs],
        in_specs=[pl.BlockSpec(memory_space=pl.ANY)] * n, out_specs=[pl.BlockSpec(memory_space=pl.ANY)] * n,
        scratch_shapes=_exchange_sems(n), compiler_params=pltpu.CompilerParams(has_side_effects=True))(*gs)


def _gather_many(xs, name):
    n = len(xs)

    def body(*refs):
        x_refs, out_refs, (send, recv, loc) = refs[:n], refs[n:2 * n], refs[2 * n:]
        x, y, c = lax.axis_index("x"), lax.axis_index("y"), lax.axis_index("c")
        me, sibling = (x, y, c), (x, y, 1 - c)
        chips = [(1 - x, y), (x, 1 - y), (1 - x, 1 - y)]

        def slot(p):
            return 4 * p[0] + 2 * p[1] + p[2]

        def copy(w, k, block, to, src=None):
            dst = out_refs[w].at[slot(block)]
            return pltpu.make_async_remote_copy(src_ref=dst if src is None else src, dst_ref=dst,
                                                send_sem=send.at[w * (N_DEV - 1) + k], recv_sem=recv.at[w * (N_DEV - 1) + k],
                                                device_id=to, device_id_type=pl.DeviceIdType.MESH)

        mine = [pltpu.make_async_copy(x_refs[w], out_refs[w].at[slot(me)], loc.at[w]) for w in range(n)]
        for cp in mine:
            cp.start()
        first = []
        for j, chip in enumerate(chips):
            first += [copy(w, 1 + j, me, (*chip, c), src=x_refs[w]) for w in range(n)]
        first += [copy(w, 0, me, sibling, src=x_refs[w]) for w in range(n)]
        for cp in first:
            cp.start()
        passed = []
        for j, chip in enumerate(chips):
            for w in range(n):
                copy(w, 1 + j, (*chip, c), me).wait_recv()
                cp = copy(w, 4 + j, (*chip, c), sibling)
                cp.start()
                passed.append(cp)
        for w in range(n):
            copy(w, 0, sibling, me).wait_recv()
            for j, chip in enumerate(chips):
                copy(w, 4 + j, (*chip, 1 - c), me).wait_recv()
        for cp in first + passed:
            cp.wait_send()
        for cp in mine:
            cp.wait()

    return pl.pallas_call(
        body, name=name, out_shape=[jax.ShapeDtypeStruct((N_DEV,) + x.shape, x.dtype) for x in xs],
        in_specs=[pl.BlockSpec(memory_space=pl.ANY)] * n, out_specs=[pl.BlockSpec(memory_space=pl.ANY)] * n,
        scratch_shapes=[pltpu.SemaphoreType.DMA((n * (N_DEV - 1),)), pltpu.SemaphoreType.DMA((n * (N_DEV - 1),)),
                        pltpu.SemaphoreType.DMA((n,))],
        compiler_params=pltpu.CompilerParams(has_side_effects=True))(*xs)


def _cat_segs(G, ws, n_mix):
    segs = []
    for g in range(G):
        lo, hi = g * ws, (g + 1) * ws
        if lo < n_mix:
            segs.append((g, 0, min(hi, n_mix) - lo, D_XA + lo))
        if hi > n_mix:
            s = max(lo, n_mix)
            segs.append((g, s - lo, hi - s, s - n_mix))
    return segs


def _cat_cols(src, n_mix, ntot):
    G, R, ws = src.shape
    segs = _cat_segs(G, ws, n_mix)
    tr = _tile(R, (256, 128, 64, 32, 16, 8))

    def body(i_ref, o_ref):
        if ntot > G * ws:
            o_ref[...] = jnp.zeros_like(o_ref)
        for g, s, n, d in segs:
            o_ref[:, d:d + n] = i_ref[g][:, s:s + n]

    return pl.pallas_call(
        body, name="cat_cols", grid=(R // tr,), in_specs=[pl.BlockSpec((G, tr, ws), lambda i: (0, i, 0))],
        out_specs=pl.BlockSpec((tr, ntot), lambda i: (i, 0)), out_shape=jax.ShapeDtypeStruct((R, ntot), src.dtype),
        compiler_params=pltpu.CompilerParams(dimension_semantics=("arbitrary",)))(src)


def _uncat_cols(dw, G, ws, n_mix):
    R, ntot = dw.shape
    segs = _cat_segs(G, ws, n_mix)
    tr = _tile(R, (256, 128, 64, 32, 16, 8))

    def body(i_ref, o_ref):
        v = i_ref[...]
        for g, s, n, d in segs:
            o_ref[g, :, s:s + n] = v[:, d:d + n]

    return pl.pallas_call(
        body, name="uncat_cols", grid=(R // tr,), in_specs=[pl.BlockSpec((tr, ntot), lambda i: (i, 0))],
        out_specs=pl.BlockSpec((G, tr, ws), lambda i: (0, i, 0)), out_shape=jax.ShapeDtypeStruct((G, R, ws), dw.dtype),
        compiler_params=pltpu.CompilerParams(dimension_semantics=("arbitrary",)))(dw)


PACK_W = 1024


def _granule(n):
    return (256 if n >= 256 * PACK_W else 8) * PACK_W


def _pack(arrs, dtype):
    flat = jnp.concatenate([a.reshape(-1).astype(dtype) for a in arrs])
    n = flat.shape[0]
    pad = (-n) % _granule(n)
    if pad:
        flat = jnp.concatenate([flat, jnp.zeros((pad,), dtype)])
    return flat.reshape(-1, PACK_W)


def _unpack(packed, shapes):
    flat = packed.reshape(-1)
    out, o = [], 0
    for s in shapes:
        n = math.prod(s)
        out.append(flat[o:o + n].reshape(s))
        o += n
    return out


def _pack_lead(arrs, dtype):
    flat = jnp.concatenate([a.reshape(N_DEV, -1).astype(dtype) for a in arrs], axis=1)
    n = flat.shape[1]
    pad = (-n) % _granule(n)
    if pad:
        flat = jnp.concatenate([flat, jnp.zeros((N_DEV, pad), dtype)], axis=1)
    return flat.reshape(N_DEV, -1, PACK_W)


def _to_full(stacked, axis):
    t = jnp.moveaxis(stacked, 0, axis)
    s = list(t.shape)
    return t.reshape(s[:axis] + [s[axis] * s[axis + 1]] + s[axis + 2:])


def _to_chunks(full, axis):
    s = list(full.shape)
    t = full.reshape(s[:axis] + [N_DEV, s[axis] // N_DEV] + s[axis + 1:])
    return jnp.moveaxis(t, axis, 0)


def _rows_of(S):
    return _tile(S, (512, 256, 128, 64))


def _norm_fwd(x, g, dt=BF):
    (h,), _ = _seq_fwd("rmsnorm_fwd", f_rmsnorm, _rows_of(x.shape[0]), [_row_spec(x)], [_par_spec(g)], [],
                       [_out_spec(x.shape[1], dt=dt)])
    return h


def _norm_bwd(x, g, dh, res=None):
    if res is None:
        (dx,), (dg,) = _seq_bwd("rmsnorm_bwd", f_rmsnorm, _rows_of(x.shape[0]), [_row_spec(x)], [_par_spec(g)], [],
                                [_row_spec(dh)], [])
        return dx, None, dg
    (dx,), (dg,), (dxb,) = _seq_bwd("rmsnorm_res_bwd", f_rmsnorm, _rows_of(x.shape[0]), [_row_spec(x)], [_par_spec(g)], [],
                                    [_row_spec(dh)], [], dx_add={0: res}, dx_bf=True)
    return dx, dxb, dg


def _mixer_specs(kind, S, p, w):
    if kind == 0:
        return (f_gla, min(S, MIX_ROWS), [_row_spec(p)],
                [_par_spec(w['a_w_gate2']), _par_spec(w['a_b_gate'].reshape(1, -1)), _par_spec(w['a_o_norm'].reshape(1, -1))],
                [(GLA_DV, GLA_DK)] * GLA_HEADS, D_MIX)
    if kind == 2:
        return (f_ssd, min(S, MIX_ROWS), [_row_spec(p, prev='halo')],
                [_par_spec(w['c_conv_w']), _par_spec(w['c_conv_b'].reshape(1, -1)), _par_spec(w['c_dt_bias'].reshape(1, -1)),
                 _par_spec(w['c_a_log'].reshape(1, -1)), _par_spec(w['c_d'].reshape(1, -1)),
                 _par_spec(w['c_norm'].reshape(1, -1))],
                [(SSM_STATE, 2 * SSM_HD)] * (SSM_HEADS // 2), D_MIX)
    return (f_hgrn, CHUNK, [_row_spec(p)],
            [_par_spec(w['d_lower_bounds']), _par_spec(w['d_o_norm'].reshape(1, -1))],
            [(HGRN_DV, HGRN_DK)] * HGRN_HEADS, D_MIX)


def _perm(t, r):
    if r == 1:
        return t
    S, n = t.shape
    return t.reshape(S // r, r, n).transpose(1, 0, 2).reshape(S, n)


def _unperm(t, r):
    if r == 1:
        return t
    S, n = t.shape
    return t.reshape(r, S // r, n).transpose(1, 0, 2).reshape(S, n)


def _rope_consts():
    half = DIL_HD // 2
    inv = ROPE_THETA ** (-jnp.arange(half, dtype=F32) / half)
    invf = jnp.concatenate([inv, inv]).reshape(1, DIL_HD)
    sign = jnp.concatenate([-jnp.ones((half,), F32), jnp.ones((half,), F32)]).reshape(1, DIL_HD)
    return invf, sign


def _dil_fwd(p, pos, w, ncat):
    S = p.shape[0]
    invf, sign = _rope_consts()
    prep_rows = [_row_spec(p), _row_spec(pos, diff=False)]
    prep_pars = [_par_spec(w['b_q_norm'].reshape(1, -1)), _par_spec(w['b_k_norm'].reshape(1, -1)),
                 _par_spec(invf, diff=False), _par_spec(sign, diff=False)]
    ng = len(DIL_GROUPS)
    qkv, _ = _seq_fwd("dil_prep_fwd", f_dil_prep, _tile(S, (256, 128)), prep_rows, prep_pars, [],
                      [_out_spec(D_DIL) for _ in range(3 * ng)])
    res = dict(perm=[], o=[], lse=[])
    for g, (window, r) in enumerate(DIL_GROUPS):
        qp, kp, vp = _perm(qkv[g], r), _perm(qkv[ng + g], r), _perm(qkv[2 * ng + g], r)
        rows = [_row_spec(qp), _row_spec(kp, prev='block'), _row_spec(vp, prev='block')]
        (o, lse), _ = _seq_fwd("dil_attn_fwd", f_dil_attn, DIL_BLOCK, rows, [], [], [_out_spec(D_DIL), _out_spec(D_DIL)],
                               period=S // r // DIL_BLOCK)
        res['perm'].append((qp, kp, vp))
        res['o'].append(_unperm(o, r))
        res['lse'].append(_unperm(lse, r))
    mrows = [_row_spec(t) for t in res['o'] + res['lse']]
    (cat,), _ = _seq_fwd("dil_merge_fwd", f_dil_merge, _rows_of(S), mrows, [], [], [_out_spec(ncat, w=D_DIL, dt=BF)])
    res['prep'] = (prep_rows, prep_pars)
    return cat, res


def _dil_bwd(dtok, res, p):
    S = p.shape[0]
    mrows = [_row_spec(t) for t in res['o'] + res['lse']]
    dm, _ = _seq_bwd("dil_merge_bwd", f_dil_merge, _rows_of(S), mrows, [], [], [dtok], [])
    dq, dk, dv = [], [], []
    for g, (window, r) in enumerate(DIL_GROUPS):
        qp, kp, vp = res['perm'][g]
        rows = [_row_spec(qp), _row_spec(kp, prev='block'), _row_spec(vp, prev='block')]
        douts = [_row_spec(_perm(dm[g], r)), _row_spec(_perm(dm[3 + g], r))]
        (a, b, c), _ = _seq_bwd("dil_attn_bwd", f_dil_attn, DIL_BLOCK, rows, [], [], douts, [],
                                period=S // r // DIL_BLOCK)
        dq.append(_unperm(a, r)); dk.append(_unperm(b, r)); dv.append(_unperm(c, r))
    prep_rows, prep_pars = res['prep']
    (dp,), (dqn, dkn) = _seq_bwd("dil_prep_bwd", f_dil_prep, _tile(S, (256, 128)), prep_rows, prep_pars, [],
                                 [_row_spec(t) for t in dq + dk + dv], [], dx_dt=BF)
    return dp, dict(b_q_norm=dqn.reshape(-1), b_k_norm=dkn.reshape(-1))


def _ffn_specs(u, cw, cb):
    half = N_DEV // 2
    rows = [_row_spec(u, prev='halo', lb=(2, None), li=lambda jc: (0, jc))]
    pars = [_par_spec(cw, bs=(None, FFN_CONV, FF_SH), idx=lambda jc: (jc, 0, 0)),
            _par_spec(cw, bs=(None, FFN_CONV, FF_SH), idx=lambda jc: (jc + half, 0, 0)),
            _par_spec(cb, bs=(None, 1, FF_SH), idx=lambda jc: (jc, 0, 0)),
            _par_spec(cb, bs=(None, 1, FF_SH), idx=lambda jc: (jc + half, 0, 0))]
    return half, rows, pars


N_MIX = {0: 2 * GLA_HEADS * GLA_DK + 2 * D_MIX + GLA_RANK, 1: 3 * len(DIL_GROUPS) * D_DIL,
         2: 2 * D_MIX + 2 * SSM_GROUPS * SSM_STATE + SSM_HEADS, 3: 2 * HGRN_HEADS * HGRN_DK + 2 * D_MIX}
W_IN = {0: 'a_w_in', 1: 'b_w_in', 2: 'c_w_in', 3: 'd_w_in'}
W_OUT = {0: 'a_w_out', 1: 'b_w_out', 2: 'c_w_out', 3: 'd_w_out'}


def _in_blocks(name, t):
    return t if SHARD_AXIS[name] == 1 else t.reshape(1, N_DEV * t.shape[1], t.shape[2])


LAYER_STACKED = ('ffn_w_up', 'ffn_conv_w', 'ffn_w_down', 'xa_w_kv')


SMALL_OF_KIND = {0: ['a_w_gate2'], 2: ['c_conv_w']}


def _layer_names(i):
    return list(LAYER_STACKED) + [W_IN[i % 4], W_OUT[i % 4]] + SMALL_OF_KIND.get(i % 4, [])


def _device_step(x, mem, pos, sh, rep, target, distributed=True):
    S, D = x.shape
    w = dict(rep)
    posf = pos.reshape(S, 1).astype(F32)
    n_mix, w_in_name, w_out_name = N_MIX, W_IN, W_OUT
    ntot = {k: -(-(n_mix[k] + D_XA) // 256) * 256 for k in n_mix}
    mem_g = w['mem_norm'].reshape(1, -1)
    mem_n = _norm_fwd(mem, mem_g)
    R = _rows_of(S)

    def mine(i):
        return {n: (sh[n][i] if n in LAYER_STACKED else sh[n]) for n in _layer_names(i)}

    if distributed:
        gl = dict(zip(_layer_names(0), _gather_many(list(mine(0).values()), "gather_weights")))
    else:
        gl = {n: (sh[n][:, 0] if n in LAYER_STACKED else sh[n]) for n in _layer_names(0)}

    saved = []
    for i in range(DEPTH):
        kind = i % 4
        L = dict(x0=x)
        for n in SMALL_OF_KIND.get(kind, []):
            w[n] = _to_full(gl[n], 1)
        in_blocks = _in_blocks(w_in_name[kind], gl[w_in_name[kind]])
        w_out = (_to_full(gl[w_out_name[kind]], 1) if SHARD_AXIS[w_out_name[kind]] == 1
                 else gl[w_out_name[kind]].reshape(-1, D))
        nxt, push = {}, [[], [], []]
        if i + 1 < DEPTH:
            if distributed:
                nxt = mine(i + 1)
                push = [[n for n in nxt if n not in ('ffn_w_up', w_in_name[(i + 1) % 4], w_out_name[(i + 1) % 4])],
                        ['ffn_w_up'], [w_in_name[(i + 1) % 4], w_out_name[(i + 1) % 4]]]
            else:
                nxt = {n: (sh[n][:, i + 1] if n in LAYER_STACKED else sh[n]) for n in _layer_names(i + 1)}
        got = dict(nxt) if not distributed else {}

        def hosted(call, names):
            if not names:
                return call()
            res, arrived = call(gather=[nxt[n] for n in names])
            got.update(zip(names, arrived))
            return res

        g1 = w['mix_norm'][i].reshape(1, -1)
        h = _norm_fwd(x, g1)
        wcat = _cat_cols(in_blocks, n_mix[kind], ntot[kind])
        p = hosted(functools.partial(_matmul, h, wcat, name="matmul_in"), push[0])
        ntok = D_DIL if kind == 1 else D_MIX
        if kind == 1:
            cat, L['dil'] = _dil_fwd(p, posf, w, ntok + D_XA)
        else:
            f, Rm, rows, pars, sshapes, _ = _mixer_specs(kind, S, p, w)
            (cat,), L['states'] = _seq_fwd("mixer%d_fwd" % kind, f, Rm, rows, pars, sshapes,
                                           [_out_spec(ntok + D_XA, w=ntok, dt=BF)], save_states=True)
        wkv = gl['xa_w_kv'].reshape(D, 2 * D_XA)
        kv = _matmul(mem_n, wkv, name="matmul_kv")
        xa_rows = [_row_spec(p, w=D_XA, dn=D_XA)]
        xa_pars = [_par_spec(kv), _par_spec(w['xa_q_norm'][i].reshape(1, -1)), _par_spec(w['xa_k_norm'][i].reshape(1, -1))]
        (cat,), _ = _seq_fwd("xattn_fwd", f_xattn, R, xa_rows, xa_pars, [],
                             [_out_spec(ntok + D_XA, w=D_XA, c=lambda jc: ntok // D_XA, dt=BF)], out_alias={0: cat})
        x1 = _matmul(cat, w_out, add=x, name="matmul_out")
        g2 = w['ffn_norm'][i].reshape(1, -1)
        h2 = _norm_fwd(x1, g2)
        wup = gl['ffn_w_up']
        u = hosted(functools.partial(_ffn_up, h2, wup), push[1])
        cw, cb = gl['ffn_conv_w'], w['ffn_conv_b'][i].reshape(N_DEV, 1, FF_SH)
        nt, frows, fpars = _ffn_specs(u, cw, cb)
        (act,), _ = _seq_fwd("ffn_act_fwd", f_ffn_act, R, frows, fpars, [],
                             [_out_spec(FF_SH, dt=BF, ls=(nt,), lb=(None,), li=lambda jc: (jc,))], ncol=nt)
        wd = gl['ffn_w_down'].reshape(D_FF, D)
        x = hosted(functools.partial(_ffn_down, act, wd, x1), push[2])
        L.update(h=h, p=p, wcat=wcat, kv=kv, wkv=wkv, cat=cat, x1=x1, h2=h2, u=u, act=act, wd=wd, wup=wup, cw=cw, g1=g1,
                 g2=g2, in_blocks=in_blocks, w_out=w_out, shapes={n: t.shape for n, t in gl.items()})
        saved.append(L)
        gl = got

    dx, dxb, loss = _loss_head(x, target)

    G = {}
    d_mem_n = None
    acc = {k: [None] * DEPTH for k in ('mix_norm', 'ffn_norm', 'ffn_conv_b', 'xa_q_norm', 'xa_k_norm')}
    parts = [{} for _ in range(DEPTH)]
    pending = {}
    half = N_DEV // 2

    def sent(call, blocks, layer):
        if not blocks:
            return call()
        if not distributed:
            parts[layer].update(blocks)
            return call()
        res, arrived = call(exchange=list(blocks.values()))
        parts[layer].update(zip(blocks, arrived))
        return res

    for i in reversed(range(DEPTH)):
        kind = i % 4
        L = saved[i]
        Gc = {}
        Gc['ffn_w_down'] = _ffn_dw_down(L['act'], dxb).reshape(N_DEV, D_FF // N_DEV, D)
        dact = _ffn_dact(dxb, L['wd'])
        cw, cb = L['cw'], w['ffn_conv_b'][i].reshape(N_DEV, 1, FF_SH)
        nt, frows, fpars = _ffn_specs(L['u'], cw, cb)
        (du,), (dwg, dwv, dbg, dbv) = _seq_bwd(
            "ffn_act_bwd", f_ffn_act, R, frows, fpars, [], [_row_spec(dact, lb=(None,), li=lambda jc: (jc,))], [],
            ncol=nt, dx_dt=BF)
        Gc['ffn_conv_w'] = jnp.concatenate([dwg[:half], dwv[half:]], axis=0)
        acc['ffn_conv_b'][i] = jnp.concatenate([dbg[:half], dbv[half:]], axis=0).reshape(-1)
        Gc['ffn_w_up'] = _ffn_dw_up(L['h2'], du)
        dh2 = sent(functools.partial(_ffn_dh2, du, L['wup']), pending, i + 1)
        dx1, dx1b, dg2 = _norm_bwd(L['x1'], L['g2'], dh2, res=dx)
        acc['ffn_norm'][i] = dg2.reshape(-1)
        G_out = _matmul(L['cat'], dx1b, mode="tn", out_dtype=BF, name="matmul_dw_out")
        dcat = _matmul(dx1b, L['w_out'], mode="nt", name="matmul_dcat")
        ntok = D_DIL if kind == 1 else D_MIX
        dtok = _row_spec(dcat, w=ntok)
        dxa = _row_spec(dcat, w=D_XA, c=lambda jc: ntok // D_XA)
        p = L['p']
        if kind == 1:
            dp, gm = _dil_bwd(dtok, L['dil'], p)
            G.update(gm)
        else:
            f, Rm, rows, pars, sshapes, _ = _mixer_specs(kind, S, p, w)
            (dp,), dps = _seq_bwd("mixer%d_bwd" % kind, f, Rm, rows, pars, sshapes, [dtok], L['states'], dx_dt=BF)
            if kind == 0:
                Gc['a_w_gate2'], G['a_b_gate'], G['a_o_norm'] = _to_chunks(dps[0], 1), dps[1].reshape(-1), dps[2].reshape(-1)
            elif kind == 2:
                Gc['c_conv_w'] = _to_chunks(dps[0], 1)
                for nme, v in zip(('c_conv_b', 'c_dt_bias', 'c_a_log', 'c_d', 'c_norm'), dps[1:]):
                    G[nme] = v.reshape(-1)
            else:
                G['d_lower_bounds'], G['d_o_norm'] = dps[0], dps[1].reshape(-1)
        xa_rows = [_row_spec(p, w=D_XA)]
        xa_pars = [_par_spec(L['kv']), _par_spec(w['xa_q_norm'][i].reshape(1, -1)), _par_spec(w['xa_k_norm'][i].reshape(1, -1))]
        (dp,), (dkv, dqn, dkn) = _seq_bwd("xattn_bwd", f_xattn, R, xa_rows, xa_pars, [], [dxa], [], dx_dt=BF,
                                          dx_alias={0: dp})
        acc['xa_q_norm'][i], acc['xa_k_norm'][i] = dqn.reshape(-1), dkn.reshape(-1)
        Gc['xa_w_kv'] = _matmul(mem_n, dkv, mode="tn", out_dtype=BF, name="matmul_dw_kv").reshape(
            N_DEV, D // N_DEV, 2 * D_XA)
        d_mem_n = _matmul(dkv, L['wkv'], mode="nt", add=d_mem_n, name="matmul_dmem" + ("" if d_mem_n is None else "_acc"))
        dwcat = sent(functools.partial(_matmul, L['h'], dp, mode="tn", out_dtype=BF, name="matmul_dw_in"),
                     {'ffn_w_up': Gc.pop('ffn_w_up')}, i)
        blocks = L['in_blocks']
        Gc[w_in_name[kind]] = _uncat_cols(dwcat, blocks.shape[0], blocks.shape[2], n_mix[kind]).reshape(
            L['shapes'][w_in_name[kind]])
        Gc[w_out_name[kind]] = (_to_chunks(G_out, 1) if SHARD_AXIS[w_out_name[kind]] == 1
                                else G_out.reshape(L['shapes'][w_out_name[kind]]))
        dh = sent(functools.partial(_matmul, dp, L['wcat'], mode="nt", name="matmul_dh"),
                  {n: Gc.pop(n) for n in ('ffn_w_down', 'ffn_conv_w')}, i)
        dx, dxb, dg1 = _norm_bwd(L['x0'], L['g1'], dh, res=dx1)
        acc['mix_norm'][i] = dg1.reshape(-1)
        pending = Gc

    if distributed:
        names = list(pending)
        parts[0].update(zip(names, _exchange_many([pending[n] for n in names], "exchange_grads")))
    else:
        parts[0].update(pending)
    _, _, dmg = _norm_bwd(mem, mem_g, d_mem_n)
    G['mem_norm'] = dmg.reshape(-1)
    for k, v in acc.items():
        G[k] = jnp.stack(v)
    got = {}
    for i in range(DEPTH):
        for n, t in parts[i].items():
            if n not in LAYER_STACKED:
                got[n] = t
    for n in LAYER_STACKED:
        got[n] = jnp.stack([parts[i][n] for i in range(DEPTH)], axis=1)
    return loss, dx, got, G


def kernel(x, mem, positions, mem_norm, mix_norm, xa_w_kv, xa_q_norm, xa_k_norm, ffn_norm, ffn_w_up, ffn_conv_w, ffn_conv_b, ffn_w_down, a_w_in, a_w_gate2, a_b_gate, a_o_norm, a_w_out, b_w_in, b_q_norm, b_k_norm, b_w_out, c_w_in, c_conv_w, c_conv_b, c_dt_bias, c_a_log, c_d, c_norm, c_w_out, d_w_in, d_lower_bounds, d_o_norm, d_w_out, loss_target, m_mem_norm, m_mix_norm, m_xa_w_kv, m_xa_q_norm, m_xa_k_norm, m_ffn_norm, m_ffn_w_up, m_ffn_conv_w, m_ffn_conv_b, m_ffn_w_down, m_a_w_in, m_a_w_gate2, m_a_b_gate, m_a_o_norm, m_a_w_out, m_b_w_in, m_b_q_norm, m_b_k_norm, m_b_w_out, m_c_w_in, m_c_conv_w, m_c_conv_b, m_c_dt_bias, m_c_a_log, m_c_d, m_c_norm, m_c_w_out, m_d_w_in, m_d_lower_bounds, m_d_o_norm, m_d_w_out, v_mem_norm, v_mix_norm, v_xa_w_kv, v_xa_q_norm, v_xa_k_norm, v_ffn_norm, v_ffn_w_up, v_ffn_conv_w, v_ffn_conv_b, v_ffn_w_down, v_a_w_in, v_a_w_gate2, v_a_b_gate, v_a_o_norm, v_a_w_out, v_b_w_in, v_b_q_norm, v_b_k_norm, v_b_w_out, v_c_w_in, v_c_conv_w, v_c_conv_b, v_c_dt_bias, v_c_a_log, v_c_d, v_c_norm, v_c_w_out, v_d_w_in, v_d_lower_bounds, v_d_o_norm, v_d_w_out):
    args = locals()
    w = {n: args[n] for n in WEIGHTS}
    m = {n: args['m_' + n] for n in WEIGHTS}
    v = {n: args['v_' + n] for n in WEIGHTS}

    big = [n for n in SHARDED if w[n].size >= 65536]
    small = [n for n in SHARDED if n not in big]
    sh = {n: (w[n].astype(BF) if n in big else w[n]) for n in SHARDED}
    loss, grad_x, parts, G = _device_step(x[0], mem[0], positions[0], sh, {n: w[n] for n in REPLICATED}, loss_target[0])
    loss = lax.psum(loss, ("x", "y", "c"))
    rep_parts = _all_gather(_pack([G[n] for n in REPLICATED], F32), "gather_replicated_grads")

    out = {}

    def put(names, res, shapes):
        for kind, r in zip(("grad", "delta", "new_m", "new_v"), res):
            for n, t in zip(names, _unpack(r, shapes)):
                out[kind + "_" + n] = t

    for n in big:
        shp = tuple(w[n].shape)
        two_d = (math.prod(shp[:-1]), shp[-1])
        res = _adamw(parts[n].reshape((N_DEV,) + two_d), w[n].reshape(two_d), m[n].reshape(two_d), v[n].reshape(two_d),
                     "adamw")
        for kind, r in zip(("grad", "delta", "new_m", "new_v"), res):
            out[kind + "_" + n] = r.reshape(shp)
    for names, prt, tag in ((small, _pack_lead([parts[n] for n in small], F32), "adamw_small"),
                            (REPLICATED, rep_parts, "adamw_replicated")):
        res = _adamw(prt, _pack([w[n] for n in names], F32), _pack([m[n] for n in names], F32),
                     _pack([v[n] for n in names], F32), tag)
        put(names, res, [tuple(w[n].shape) for n in names])
    return (loss, grad_x[None], *[out["grad_" + n] for n in WEIGHTS], *[out["delta_" + n] for n in WEIGHTS],
            *[out["new_m_" + n] for n in WEIGHTS], *[out["new_v_" + n] for n in WEIGHTS])
```

```python
import functools
import math

import jax
import jax.numpy as jnp
from jax import lax
from jax.experimental import pallas as pl
from jax.experimental.pallas import tpu as pltpu

F32 = jnp.float32
BF = jnp.bfloat16
_MM_DTYPE = BF

N_DEV = 8
EPS = 1e-6
ROPE_THETA = 10000.0
CHUNK = 64
MIX_ROWS = 256
D_MIX = 768
XA_HEADS, XA_HD, D_XA = 4, 64, 256
GLA_HEADS, GLA_DK, GLA_DV, GLA_RANK, GLA_GATE_NORM = 4, 96, 192, 16, 16.0
DIL_GROUPS = ((128, 1), (512, 4), (2048, 16))
DIL_HEADS, DIL_HD, DIL_BLOCK, D_DIL = 4, 128, 128, 512
SSM_HD, SSM_HEADS, SSM_GROUPS, SSM_STATE, SSM_CONV = 64, 12, 2, 128, 4
HGRN_HEADS, HGRN_DK, HGRN_DV = 6, 128, 128
D_FF = 2816
FFN_CONV = 3
DEPTH = 4
ADAM_LR, ADAM_B1, ADAM_B2, ADAM_EPS, ADAM_WD, ADAM_STEP = 0.001, 0.9, 0.999, 1e-08, 0.01, 10
NEG = -1e30
HALO = 8
VMEM_LIMIT = 56 << 20
ADAM_BLOCK = 1 << 18

WEIGHTS = ['mem_norm', 'mix_norm', 'xa_w_kv', 'xa_q_norm', 'xa_k_norm', 'ffn_norm', 'ffn_w_up', 'ffn_conv_w',
           'ffn_conv_b', 'ffn_w_down', 'a_w_in', 'a_w_gate2', 'a_b_gate', 'a_o_norm', 'a_w_out', 'b_w_in', 'b_q_norm',
           'b_k_norm', 'b_w_out', 'c_w_in', 'c_conv_w', 'c_conv_b', 'c_dt_bias', 'c_a_log', 'c_d', 'c_norm', 'c_w_out',
           'd_w_in', 'd_lower_bounds', 'd_o_norm', 'd_w_out']
SHARD_AXIS = {'xa_w_kv': 1, 'ffn_w_up': 2, 'ffn_conv_w': 2, 'ffn_w_down': 1, 'a_w_in': 1, 'a_w_gate2': 1, 'a_w_out': 0,
              'b_w_in': 1, 'b_w_out': 1, 'c_w_in': 0, 'c_conv_w': 1, 'c_w_out': 0, 'd_w_in': 1, 'd_w_out': 0}
SHARDED = [n for n in WEIGHTS if n in SHARD_AXIS]
REPLICATED = [n for n in WEIGHTS if n not in SHARD_AXIS]


def _dot(a, b, ca, cb):
    return lax.dot_general(a.astype(_MM_DTYPE), b.astype(_MM_DTYPE), (((ca,), (cb,)), ((), ())),
                           preferred_element_type=F32)


@jax.custom_vjp
def mm_nn(a, b):
    return _dot(a, b, 1, 0)


mm_nn.defvjp(lambda a, b: (_dot(a, b, 1, 0), (a, b)),
             lambda r, g: (_dot(g, r[1], 1, 1), _dot(r[0], g, 0, 0)))


@jax.custom_vjp
def mm_nt(a, b):
    return _dot(a, b, 1, 1)


mm_nt.defvjp(lambda a, b: (_dot(a, b, 1, 1), (a, b)),
             lambda r, g: (_dot(g, r[1], 1, 0), _dot(g, r[0], 0, 0)))


@jax.custom_vjp
def mm_tn(a, b):
    return _dot(a, b, 0, 0)


mm_tn.defvjp(lambda a, b: (_dot(a, b, 0, 0), (a, b)),
             lambda r, g: (_dot(r[1], g, 1, 1), _dot(r[0], g, 1, 0)))


def _dot_hi(a, b, ca, cb):
    return lax.dot_general(a, b, (((ca,), (cb,)), ((), ())), precision=lax.Precision.HIGHEST,
                           preferred_element_type=F32)


def _tril(c):
    return lax.broadcasted_iota(jnp.int32, (c, c), 0) >= lax.broadcasted_iota(jnp.int32, (c, c), 1)


@jax.custom_vjp
def cumsum_rows(x):
    return _dot_hi(_tril(x.shape[0]).astype(F32), x, 1, 0)


cumsum_rows.defvjp(lambda x: (cumsum_rows(x), None),
                   lambda r, g: (_dot_hi(_tril(g.shape[0]).astype(F32), g, 0, 0),))


@jax.custom_vjp
def cumsum_rows_t(x):
    return _dot_hi(x, _tril(x.shape[0]).astype(F32), 0, 1)


cumsum_rows_t.defvjp(lambda x: (cumsum_rows_t(x), None),
                     lambda r, g: (_dot_hi(_tril(g.shape[1]).astype(F32), g, 0, 1),))


def _split(x, sizes):
    sizes = tuple(int(s) for s in sizes)
    assert sum(sizes) == x.shape[-1], (sizes, x.shape)

    @jax.custom_vjp
    def sp(x):
        out, o = [], 0
        for s in sizes:
            out.append(x[:, o:o + s])
            o += s
        return tuple(out)

    sp.defvjp(lambda x: (sp(x), None), lambda r, g: (jnp.concatenate(list(g), axis=1),))
    return sp(x)


def _row(x, r):
    m = lax.broadcasted_iota(jnp.int32, x.shape, 0) == r
    return jnp.sum(jnp.where(m, x, 0.0), axis=0, keepdims=True)


@jax.custom_vjp
def _roll_half(x):
    return pltpu.roll(x, 64, 1)


_roll_half.defvjp(lambda x: (pltpu.roll(x, 64, 1), None), lambda r, g: (pltpu.roll(g, 64, 1),))


def _shift(xp, x, d):
    if d == 0:
        return x
    n, m = x.shape[0], xp.shape[0]
    assert d <= m == HALO

    @jax.custom_vjp
    def sh(xp, x):
        r = pltpu.roll(x, d, 0)
        row = lax.broadcasted_iota(jnp.int32, xp.shape, 0)
        head = jnp.where(row < d, pltpu.roll(xp, d, 0), r[:m])
        return jnp.concatenate([head, r[m:]], axis=0)

    def bwd(_, g):
        row = lax.broadcasted_iota(jnp.int32, g.shape, 0)
        rowp = lax.broadcasted_iota(jnp.int32, (m,) + g.shape[1:], 0)
        dxp = jnp.where(rowp >= m - d, pltpu.roll(g[:m], m - d, 0), 0.0)
        return dxp, jnp.where(row < n - d, pltpu.roll(g, n - d, 0), 0.0)

    sh.defvjp(lambda xp, x: (sh(xp, x), None), bwd)
    return sh(xp, x)


def _rms(x, g):
    return x * lax.rsqrt(jnp.mean(x * x, axis=-1, keepdims=True) + EPS) * g


def _lane_pair(a, b, width=128):
    shape = a.shape[:-1] + (width,)
    lane = lax.broadcasted_iota(jnp.int32, shape, len(shape) - 1)
    return jnp.where(lane < width // 2, a, b)


def _row_spec(a, w=None, c=None, prev=False, diff=True, dn=None, lb=(), li=None):
    return dict(a=a, w=a.shape[-1] if w is None else w, c=(lambda jc: 0) if c is None else c, prev=prev, diff=diff,
                dn=a.shape[-1] if dn is None else dn, lb=tuple(lb), li=(lambda jc: ()) if li is None else li)


def _par_spec(a, bs=None, idx=None, diff=True):
    nd = a.ndim
    return dict(a=a, bs=tuple(a.shape) if bs is None else tuple(bs),
                idx=(lambda jc: (0,) * nd) if idx is None else idx, diff=diff)


def _out_spec(n, w=None, c=None, dt=F32, ls=(), lb=(), li=None):
    return dict(n=n, w=n if w is None else w, c=(lambda jc: 0) if c is None else c, dt=dt, ls=tuple(ls), lb=tuple(lb),
                li=(lambda jc: ()) if li is None else li)


def _cparams():
    return pltpu.CompilerParams(dimension_semantics=("arbitrary", "arbitrary"), vmem_limit_bytes=VMEM_LIMIT)


def _bspec(s, R, rowfn):
    return pl.BlockSpec(s['lb'] + (R, s['w']),
                        functools.partial(lambda jc, i, s: tuple(s['li'](jc)) + (rowfn(i), s['c'](jc)), s=s))


def _prev_rows(s, R):
    return R if s['prev'] == 'block' else HALO


def _pspec(s, R, blockfn):
    pr = _prev_rows(s, R)
    return pl.BlockSpec(s['lb'] + (pr, s['w']), functools.partial(
        lambda jc, i, s: tuple(s['li'](jc)) + (jnp.maximum(blockfn(i) * (R // pr) - 1, 0), s['c'](jc)), s=s))


def _seq_fwd(name, f, R, rows, params, state_shapes, outs, *, ncol=1, period=None, save_states=False, out_alias=None):
    nrows = rows[0]['a'].shape[-2]
    nb = nrows // R
    assert nb * R == nrows
    period = nb if period is None else period
    prev_ids = [k for k, r in enumerate(rows) if r['prev']]
    n_rows, n_prev, n_par, n_out, n_st = len(rows), len(prev_ids), len(params), len(outs), len(state_shapes)

    def body(*refs):
        o = 0
        cur = refs[o:o + n_rows]; o += n_rows
        prv = refs[o:o + n_prev]; o += n_prev
        par = refs[o:o + n_par]; o += n_par + len(out_alias or {})
        out = refs[o:o + n_out]; o += n_out
        sav = refs[o:o + (n_st if save_states else 0)]; o += len(sav)
        st = refs[o:o + n_st]
        i = pl.program_id(1)
        first = (i % period) == 0

        @pl.when(i == 0)
        def _():
            for s in st:
                s[...] = jnp.zeros_like(s)

        xs = [r[...].astype(F32) for r in cur]
        xp = [r[...].astype(F32) for r in prv]
        ps = [r[...] for r in par]
        sts = [s[...] for s in st]
        for sv, s in zip(sav, sts):
            sv[0] = s
        ov, ns = f(first, xp, xs, ps, sts)
        for r, v in zip(out, ov):
            r[...] = v.astype(r.dtype)
        for s, v in zip(st, ns):
            s[...] = v

    in_specs = [_bspec(r, R, lambda i: i) for r in rows]
    in_specs += [_pspec(rows[k], R, lambda i: i) for k in prev_ids]
    in_specs += [pl.BlockSpec(p['bs'], functools.partial(lambda jc, i, idx: idx(jc), idx=p['idx'])) for p in params]
    out_specs = [_bspec(o_, R, lambda i: i) for o_ in outs]
    out_shape = [jax.ShapeDtypeStruct(o_['ls'] + (nrows, o_['n']), o_['dt']) for o_ in outs]
    if save_states:
        for s in state_shapes:
            out_specs.append(pl.BlockSpec((1,) + tuple(s), lambda jc, i, nd=len(s): (i,) + (0,) * nd))
            out_shape.append(jax.ShapeDtypeStruct((nb,) + tuple(s), F32))
    args = [r['a'] for r in rows] + [rows[k]['a'] for k in prev_ids] + [p['a'] for p in params]
    aliases = {}
    for n_, arr in sorted((out_alias or {}).items()):
        assert arr.shape == out_shape[n_].shape and arr.dtype == out_shape[n_].dtype
        aliases[len(args)] = n_
        args.append(arr)
        in_specs.append(pl.BlockSpec(memory_space=pl.ANY))
    res = pl.pallas_call(
        body, name=name, grid=(ncol, nb), in_specs=in_specs, out_specs=out_specs, out_shape=out_shape,
        scratch_shapes=[pltpu.VMEM(tuple(s), F32) for s in state_shapes], input_output_aliases=aliases,
        compiler_params=_cparams())(*args)
    return list(res[:n_out]), list(res[n_out:])


def _seq_bwd(name, f, R, rows, params, state_shapes, douts, saved, *, ncol=1, period=None, dx_dt=F32, dx_add=None,
             dx_bf=False, dx_alias=None):
    nrows = rows[0]['a'].shape[-2]
    nb = nrows // R
    period = nb if period is None else period
    prev_ids = [k for k, r in enumerate(rows) if r['prev']]
    drow_ids = [k for k, r in enumerate(rows) if r['diff']]
    dpar_ids = [k for k, p in enumerate(params) if p['diff']]
    for k in prev_ids:
        assert rows[k]['diff']
    dx_add, dx_alias = dict(dx_add or {}), dict(dx_alias or {})
    add_ids, alias_ids = sorted(dx_add), sorted(dx_alias)
    n_rows, n_prev, n_par, n_do, n_st = len(rows), len(prev_ids), len(params), len(douts), len(state_shapes)
    n_dx, n_dp, n_add, n_al = len(drow_ids), len(dpar_ids), len(add_ids), len(alias_ids)

    def body(*refs):
        o = 0
        cur = refs[o:o + n_rows]; o += n_rows
        prv = refs[o:o + n_prev]; o += n_prev
        par = refs[o:o + n_par]; o += n_par
        sav = refs[o:o + n_st]; o += n_st
        dou = refs[o:o + n_do]; o += n_do
        adr = refs[o:o + n_add]; o += n_add
        o += n_al
        dxr = refs[o:o + n_dx]; o += n_dx
        dpr = refs[o:o + n_dp]; o += n_dp
        dxb = refs[o:o + (n_dx if dx_bf else 0)]; o += len(dxb)
        dst = refs[o:o + n_st]; o += n_st
        car = refs[o:o + n_prev]
        j = pl.program_id(1)
        i = nb - 1 - j
        first = (i % period) == 0

        @pl.when(j == 0)
        def _():
            for s in tuple(dst) + tuple(car) + tuple(dpr):
                s[...] = jnp.zeros_like(s)

        xs = [r[...].astype(F32) for r in cur]
        xp = [r[...].astype(F32) for r in prv]
        ps = [r[...] for r in par]
        sts = [s[0] for s in sav]

        def g(dxs, dxp, dps, dsts):
            xs_, ps_ = list(xs), list(ps)
            for k, v in zip(drow_ids, dxs):
                xs_[k] = v
            for k, v in zip(dpar_ids, dps):
                ps_[k] = v
            ov, ns = f(first, list(dxp), xs_, ps_, list(dsts))
            return tuple(ov), tuple(ns)

        _, vjp = jax.vjp(g, tuple(xs[k] for k in drow_ids), tuple(xp), tuple(ps[k] for k in dpar_ids), tuple(sts))
        dxs, dxp, dps, dsts = vjp((tuple(r[...].astype(F32) for r in dou), tuple(s[...] for s in dst)))
        dxs = list(dxs)
        for n_, pos in enumerate(add_ids):
            dxs[pos] = dxs[pos] + adr[n_][...].astype(F32)
        tails = {}
        for n_, k in enumerate(prev_ids):
            pos = drow_ids.index(k)
            if rows[k]['prev'] == 'block':
                dxs[pos] = dxs[pos] + car[n_][...]
            else:
                tails[pos] = car[n_][...]
            car[n_][...] = dxp[n_]
        for pos, v in enumerate(dxs):
            outs_ = [dxr[pos]] + ([dxb[pos]] if dx_bf else [])
            if pos in tails:
                v = jnp.concatenate([v[..., :R - HALO, :], v[..., R - HALO:, :] + tails[pos]], axis=-2)
            for r in outs_:
                r[...] = v.astype(r.dtype)
        for r, v in zip(dpr, dps):
            r[...] += v
        for s, v in zip(dst, dsts):
            s[...] = v

    def rev(j):
        return nb - 1 - j

    def dspec(k):
        return _bspec(rows[k], R, rev)

    in_specs = [_bspec(r, R, rev) for r in rows]
    in_specs += [_pspec(rows[k], R, rev) for k in prev_ids]
    in_specs += [pl.BlockSpec(p['bs'], functools.partial(lambda jc, j, idx: idx(jc), idx=p['idx'])) for p in params]
    in_specs += [pl.BlockSpec((1,) + tuple(s), lambda jc, j, nd=len(s): (nb - 1 - j,) + (0,) * nd) for s in state_shapes]
    in_specs += [_bspec(d, R, rev) for d in douts]
    in_specs += [dspec(drow_ids[pos]) for pos in add_ids]
    in_specs += [pl.BlockSpec(memory_space=pl.ANY) for _ in alias_ids]
    out_specs = [dspec(k) for k in drow_ids]
    out_shape = [jax.ShapeDtypeStruct(tuple(rows[k]['a'].shape[:-1]) + (rows[k]['dn'],), dx_dt) for k in drow_ids]
    for k in dpar_ids:
        p = params[k]
        out_specs.append(pl.BlockSpec(p['bs'], functools.partial(lambda jc, j, idx: idx(jc), idx=p['idx'])))
        out_shape.append(jax.ShapeDtypeStruct(p['a'].shape, F32))
    if dx_bf:
        out_specs += [dspec(k) for k in drow_ids]
        out_shape += [jax.ShapeDtypeStruct(tuple(rows[k]['a'].shape[:-1]) + (rows[k]['dn'],), BF) for k in drow_ids]
    scratch = [pltpu.VMEM(tuple(s), F32) for s in state_shapes]
    scratch += [pltpu.VMEM(tuple(d for d in rows[k]['lb'] if d is not None) + (_prev_rows(rows[k], R), rows[k]['w']), F32)
                for k in prev_ids]
    args = ([r['a'] for r in rows] + [rows[k]['a'] for k in prev_ids] + [p['a'] for p in params] + list(saved)
            + [d['a'] for d in douts] + [dx_add[pos] for pos in add_ids] + [dx_alias[pos] for pos in alias_ids])
    n_in = len(args)
    aliases = {n_in - n_al + n_: pos for n_, pos in enumerate(alias_ids)}
    for pos in alias_ids:
        assert dx_alias[pos].shape == out_shape[pos].shape and dx_alias[pos].dtype == out_shape[pos].dtype
    res = pl.pallas_call(
        body, name=name, grid=(ncol, nb), in_specs=in_specs, out_specs=out_specs, out_shape=out_shape,
        scratch_shapes=scratch, input_output_aliases=aliases, compiler_params=_cparams())(*args)
    if dx_bf:
        return list(res[:n_dx]), list(res[n_dx:n_dx + n_dp]), list(res[n_dx + n_dp:])
    return list(res[:n_dx]), list(res[n_dx:])


def _tile(n, cands):
    for c in cands:
        if n % c == 0:
            return c
    return n


def _mm_call(name, grid, a, a_spec, b, b_spec, contract, out_shape, out_spec, acc_shape, add=None, add_spec=None,
             exchange=None, gather=None):
    nk = grid[2]
    ca, cb = contract
    has_add = add is not None
    ex = list(exchange or []) + list(gather or [])
    ex_shapes = [g.shape for g in exchange or []] + [(N_DEV,) + tuple(g.shape) for g in gather or []]
    n_ex = len(ex)
    n_in = 2 + has_add

    def body(*refs):
        a_ref, b_ref = refs[0], refs[1]
        add_ref = refs[2] if has_add else None
        o_ref = refs[n_in + n_ex]
        scr = refs[n_in + 2 * n_ex + 1:]
        step = [pl.program_id(d) for d in range(3)]
        if n_ex:
            g_refs, r_refs, sems = refs[n_in:n_in + n_ex], refs[n_in + n_ex + 1:n_in + 2 * n_ex + 1], scr[-3:]

            @pl.when((step[0] == 0) & (step[1] == 0) & (step[2] == 0))
            def _():
                _exchange_start(g_refs, r_refs, *sems)

        part = _dot(a_ref[...], b_ref[...], ca, cb)

        def finish(r):
            if has_add:
                r = r + add_ref[...].astype(F32)
            o_ref[...] = r.astype(o_ref.dtype)

        if nk == 1:
            finish(part)
        else:
            acc = scr[0]

            @pl.when(step[2] == 0)
            def _():
                acc[...] = part

            @pl.when(step[2] > 0)
            def _():
                acc[...] += part

            @pl.when(step[2] == nk - 1)
            def _():
                finish(acc[...])

        if n_ex:
            @pl.when((step[0] == grid[0] - 1) & (step[1] == grid[1] - 1) & (step[2] == grid[2] - 1))
            def _():
                _exchange_wait(g_refs, r_refs, *sems)

    in_specs, args = [a_spec, b_spec], [a, b]
    if has_add:
        in_specs.append(add_spec)
        args.append(add)
    any_spec = pl.BlockSpec(memory_space=pl.ANY)
    scratch = [] if nk == 1 else [pltpu.VMEM(acc_shape, F32)]
    if n_ex:
        scratch += _exchange_sems(n_ex)
    res = pl.pallas_call(
        body, name=name, grid=grid, in_specs=in_specs + [any_spec] * n_ex, out_specs=[out_spec] + [any_spec] * n_ex,
        out_shape=[out_shape] + [jax.ShapeDtypeStruct(s, g.dtype) for s, g in zip(ex_shapes, ex)], scratch_shapes=scratch,
        compiler_params=pltpu.CompilerParams(
            dimension_semantics=("arbitrary",) * 3 if n_ex else ("parallel", "parallel", "arbitrary"),
            vmem_limit_bytes=VMEM_LIMIT, has_side_effects=bool(n_ex)))(*args, *ex)
    return (res[0], list(res[1:])) if n_ex else res[0]


def _matmul(a, b, mode="nn", add=None, out_dtype=F32, name="matmul", **pushed):
    if mode == "nn":
        (M, K), N = a.shape, b.shape[1]
    elif mode == "nt":
        (M, K), N = a.shape, b.shape[0]
    else:
        (K, M), N = a.shape, b.shape[1]
    if mode == "tn" and 1024 < N <= 5120:
        tm, tn = _tile(M, (512, 256, 128, 64, 32, 16, 8)), N
        tk = _tile(K, (2048 if tn <= 3072 else 1024, 1024, 512, 256, 128))
    else:
        tk = K if K <= 5120 else _tile(K, (2048, 1024, 512, 256, 128))
        tm = _tile(M, ((2048,) if tk <= 1024 and mode != "tn" else ()) + (1024, 512, 256, 128, 64, 32, 16, 8))
        tn = _tile(N, (512, 256, 128))
    if mode == "tn":
        a_spec = pl.BlockSpec((tk, tm), lambda i, j, k: (k, i))
    else:
        a_spec = pl.BlockSpec((tm, tk), lambda i, j, k: (i, k))
    if mode == "nt":
        b_spec = pl.BlockSpec((tn, tk), lambda i, j, k: (j, k))
    else:
        b_spec = pl.BlockSpec((tk, tn), lambda i, j, k: (k, j))
    blk = pl.BlockSpec((tm, tn), lambda i, j, k: (i, j))
    return _mm_call(name, (M // tm, N // tn, K // tk), a, a_spec, b, b_spec,
                    {"nn": (1, 0), "nt": (1, 1), "tn": (0, 0)}[mode], jax.ShapeDtypeStruct((M, N), out_dtype), blk,
                    (tm, tn), add, blk, **pushed)


FF_SH = 2 * D_FF // N_DEV


def _ffn_up(h2, wup, **pushed):
    S, D = h2.shape
    tm = _tile(S, (2048, 1024, 512, 256, 128))
    return _mm_call("matmul_up", (S // tm, N_DEV, 1), h2, pl.BlockSpec((tm, D), lambda m, j, k: (m, 0)),
                    wup, pl.BlockSpec((None, D, FF_SH), lambda m, j, k: (j, 0, 0)), (1, 0),
                    jax.ShapeDtypeStruct((2, N_DEV // 2, S, FF_SH), F32),
                    pl.BlockSpec((None, None, tm, FF_SH), lambda m, j, k: (j // 4, j % 4, m, 0)), (tm, FF_SH), **pushed)


def _ffn_down(act, wd, x1, **pushed):
    _, S, _ = act.shape
    D = wd.shape[1]
    tm, tn = _tile(S, (1024, 512, 256, 128)), _tile(D, (1024, 512, 256, 128))
    blk = pl.BlockSpec((tm, tn), lambda m, n, k: (m, n))
    return _mm_call("matmul_down", (S // tm, D // tn, N_DEV // 2), act,
                    pl.BlockSpec((None, tm, FF_SH), lambda m, n, k: (k, m, 0)), wd,
                    pl.BlockSpec((FF_SH, tn), lambda m, n, k: (k, n)), (1, 0), jax.ShapeDtypeStruct((S, D), F32), blk,
                    (tm, tn), x1, blk, **pushed)


def _ffn_dact(dxb, wd):
    S, D = dxb.shape
    tm = _tile(S, (2048, 1024, 512, 256, 128))
    return _mm_call("matmul_dact", (S // tm, N_DEV // 2, 1), dxb, pl.BlockSpec((tm, D), lambda m, j, k: (m, 0)), wd,
                    pl.BlockSpec((FF_SH, D), lambda m, j, k: (j, 0)), (1, 1),
                    jax.ShapeDtypeStruct((N_DEV // 2, S, FF_SH), BF),
                    pl.BlockSpec((None, tm, FF_SH), lambda m, j, k: (j, m, 0)), (tm, FF_SH))


def _ffn_dw_down(act, dxb):
    _, S, _ = act.shape
    D = dxb.shape[1]
    tk, tn = _tile(S, (2048, 1024, 512, 256, 128)), _tile(D, (512, 256, 128))
    return _mm_call("matmul_dw_down", (N_DEV // 2, D // tn, S // tk), act,
                    pl.BlockSpec((None, tk, FF_SH), lambda j, n, k: (j, k, 0)), dxb,
                    pl.BlockSpec((tk, tn), lambda j, n, k: (k, n)), (0, 0), jax.ShapeDtypeStruct((D_FF, D), BF),
                    pl.BlockSpec((FF_SH, tn), lambda j, n, k: (j, n)), (FF_SH, tn))


def _ffn_dw_up(h2, du, exchange=None):
    S, D = h2.shape
    tk = _tile(S, (2048, 1024, 512, 256, 128))
    return _mm_call("matmul_dw_up", (N_DEV, 1, S // tk), h2, pl.BlockSpec((tk, D), lambda j, n, k: (k, 0)), du,
                    pl.BlockSpec((None, None, tk, FF_SH), lambda j, n, k: (j // 4, j % 4, k, 0)), (0, 0),
                    jax.ShapeDtypeStruct((N_DEV, D, FF_SH), BF),
                    pl.BlockSpec((None, D, FF_SH), lambda j, n, k: (j, 0, 0)), (D, FF_SH), exchange=exchange)


def _ffn_dh2(du, wup, exchange=None):
    S = du.shape[2]
    D = wup.shape[1]
    tm = _tile(S, (1024, 512, 256, 128))
    return _mm_call("matmul_dh2", (S // tm, 1, N_DEV), du,
                    pl.BlockSpec((None, None, tm, FF_SH), lambda m, n, k: (k // 4, k % 4, m, 0)), wup,
                    pl.BlockSpec((None, D, FF_SH), lambda m, n, k: (k, 0, 0)), (1, 1),
                    jax.ShapeDtypeStruct((S, D), F32), pl.BlockSpec((tm, D), lambda m, n, k: (m, 0)), (tm, D),
                    exchange=exchange)


def f_rmsnorm(first, xp, xs, ps, sts):
    return (_rms(xs[0], ps[0]),), ()


def _same_block(shape, rows_per, cols_per):
    return (lax.broadcasted_iota(jnp.int32, shape, 0) // rows_per) == (lax.broadcasted_iota(jnp.int32, shape, 1) // cols_per)


@jax.custom_vjp
def _head_mean(x):
    n = x.shape[1]
    return _dot_hi(x, jnp.where(_same_block((n, n), XA_HD, XA_HD), 1.0 / XA_HD, 0.0), 1, 0)


_head_mean.defvjp(lambda x: (_head_mean(x), None), lambda r, g: (_head_mean(g),))


def f_xattn(first, xp, xs, ps, sts):
    (xq,), (kv, qn, kn) = xs, ps
    k, v = _split(kv, [D_XA, D_XA])
    m_rows = kv.shape[0]
    q = xq * lax.rsqrt(_head_mean(xq * xq) + EPS) * jnp.concatenate([qn] * XA_HEADS, axis=1)
    k = k * lax.rsqrt(_head_mean(k * k) + EPS) * jnp.concatenate([kn] * XA_HEADS, axis=1)
    kt = k.T
    kbd = jnp.where(_same_block((D_XA, XA_HEADS * m_rows), XA_HD, m_rows), jnp.concatenate([kt] * XA_HEADS, axis=1), 0.0)
    s = mm_nn(q, kbd) * (XA_HD ** -0.5)
    ps_ = []
    for sh in _split(s, [m_rows] * XA_HEADS):
        mx = lax.stop_gradient(jnp.max(sh, axis=-1, keepdims=True))
        p = jnp.exp(sh - mx)
        ps_.append(p / jnp.sum(p, axis=-1, keepdims=True))
    vbd = jnp.where(_same_block((XA_HEADS * m_rows, D_XA), m_rows, XA_HD), jnp.concatenate([v] * XA_HEADS, axis=0), 0.0)
    return (mm_nn(jnp.concatenate(ps_, axis=1), vbd),), ()


def _conv(xp, x, w, b, first, taps):
    xp = jnp.where(first, 0.0, xp)
    y = b + w[taps - 1:taps] * x
    for d in range(1, taps):
        y = y + w[taps - 1 - d:taps - d] * _shift(xp, x, d)
    return y


def _unstack2(x):
    @jax.custom_vjp
    def us(x):
        return x[0], x[1]

    us.defvjp(lambda x: (us(x), None), lambda r, g: (jnp.stack(g),))
    return us(x)


def f_ffn_act(first, xp, xs, ps, sts):
    (up,), (u,), (wg, wv, bg, bv) = xp, xs, ps
    (ugp, uvp), (ug, uv) = _unstack2(up), _unstack2(u)
    gate = _conv(ugp, ug, wg, bg, first, FFN_CONV)
    val = _conv(uvp, uv, wv, bv, first, FFN_CONV)
    return (jax.nn.silu(gate) * val,), ()


def _gla_chunk(q, k, v, la, sts, dk, dv):
    c, nh = q.shape[0], len(sts)
    b = cumsum_rows(la)
    b_last = _row(b, c - 1)
    b_ref = _row(b, c // 2 - 1)
    qe, ke = _split(q * jnp.exp(b - b_ref), [dk] * nh), _split(k * jnp.exp(b_ref - b), [dk] * nh)
    qi, kl = _split(q * jnp.exp(b), [dk] * nh), _split(k * jnp.exp(b_last - b), [dk] * nh)
    dec, vs = _split(jnp.exp(b_last), [dk] * nh), _split(v, [dv] * nh)
    tril = _tril(c)
    outs, new = [], []
    for h in range(nh):
        att = jnp.where(tril, mm_nt(qe[h], ke[h]), 0.0)
        outs.append(mm_nn(att, vs[h]) + mm_nt(qi[h], sts[h]))
        new.append(sts[h] * dec[h] + mm_tn(vs[h], kl[h]))
    return outs, tuple(new)


def _split_rows(x, n):
    c = x.shape[0] // n

    @jax.custom_vjp
    def sp(x):
        return tuple(x[i * c:(i + 1) * c] for i in range(n))

    sp.defvjp(lambda x: (sp(x), None), lambda r, g: (jnp.concatenate(list(g), axis=0),))
    return sp(x)


def _gla_scan(q, k, v, la, sts, dk, dv):
    n = q.shape[0] // CHUNK
    per_chunk = []
    for qc, kc, vc, lc in zip(*(_split_rows(t, n) for t in (q, k, v, la))):
        o, sts = _gla_chunk(qc, kc, vc, lc, sts, dk, dv)
        per_chunk.append(o)
    return [jnp.concatenate([o[h] for o in per_chunk], axis=0) for h in range(len(sts))], sts


def _a_cols(ntot):
    used = D_XA + 2 * GLA_HEADS * GLA_DK + D_MIX + GLA_RANK + D_MIX
    return [D_XA, GLA_HEADS * GLA_DK, GLA_HEADS * GLA_DK, D_MIX, GLA_RANK, D_MIX] + ([ntot - used] if ntot > used else [])


def f_gla(first, xp, xs, ps, sts):
    (p,), (wg2, bg, on) = xs, ps
    parts = _split(p, _a_cols(p.shape[1]))
    q, k, v, glr, og = parts[1:6]
    la = jax.nn.log_sigmoid(mm_nn(glr, wg2) + bg) / GLA_GATE_NORM
    outs, new = _gla_scan(q * (GLA_DK ** -0.5), k, v, la, tuple(sts), GLA_DK, GLA_DV)
    return (jnp.concatenate([_rms(o, on) for o in outs], axis=1) * jax.nn.silu(og),), new


def f_hgrn(first, xp, xs, ps, sts):
    (p,), (lbp, on) = xs, ps
    _, q, fgate, iv, og = _split(p, [D_XA, D_MIX, D_MIX, D_MIX, D_MIX])
    e = jnp.exp(lbp - jnp.max(lbp, axis=0, keepdims=True))
    row = lax.broadcasted_iota(jnp.int32, e.shape, 0)
    lb = jnp.sum(jnp.where(row >= 1, e, 0.0), axis=0, keepdims=True) / jnp.sum(e, axis=0, keepdims=True)
    fg = lb + (1.0 - lb) * jax.nn.sigmoid(fgate)
    outs, new = _gla_scan(jax.nn.silu(q), 1.0 - fg, iv, jnp.log(fg), tuple(sts), HGRN_DK, HGRN_DV)
    return (jnp.concatenate([_rms(o, on) for o in outs], axis=1) * jax.nn.sigmoid(og),), new


def _c_cols(ntot):
    gn = SSM_GROUPS * SSM_STATE
    used = D_XA + D_MIX + D_MIX + 2 * gn + SSM_HEADS
    return [D_XA, D_MIX, D_MIX + 2 * gn, SSM_HEADS] + ([ntot - used] if ntot > used else [])


def f_ssd(first, xp, xs, ps, sts):
    (pp,), (p,), (cw, cb, dtb, alog, dsk, ng) = xp, xs, ps
    gn = SSM_GROUPS * SSM_STATE
    _, z, xbc, dtr = _split(p, _c_cols(p.shape[1]))[:4]
    xbc_p = _split(pp, _c_cols(p.shape[1]))[2]
    xbc = jax.nn.silu(_conv(xbc_p, xbc, cw, cb, first, SSM_CONV))
    xs_, bm, cm = _split(xbc, [D_MIX, gn, gn])
    dt = jax.nn.softplus(dtr + dtb)
    n = p.shape[0] // CHUNK
    ys, sts = [], tuple(sts)
    for xc, bc, cc, dc in zip(*(_split_rows(t, n) for t in (xs_, bm, cm, dt))):
        y, sts = _ssd_chunk(xc, bc, cc, dc, alog, dsk, sts)
        ys.append(y)
    y = jnp.concatenate(ys, axis=0) * jax.nn.silu(z)
    gw = D_MIX // SSM_GROUPS
    yg = _split(y, [gw] * SSM_GROUPS)
    ngs = _split(ng, [gw] * SSM_GROUPS)
    y = jnp.concatenate([_rms(yg[g], ngs[g]) for g in range(SSM_GROUPS)], axis=1)
    return (y,), sts


def _ssd_chunk(xs_, bm, cm, dt, alog, dsk, sts):
    c = xs_.shape[0]
    hg = SSM_HEADS // SSM_GROUPS
    a = dt * (-jnp.exp(alog))
    acs = cumsum_rows(a)
    acs_t = cumsum_rows_t(a)
    acs_last = _row(acs, c - 1)
    dt_h = _split(dt, [1] * SSM_HEADS)
    acs_h = _split(acs, [1] * SSM_HEADS)
    al_h = _split(acs_last, [1] * SSM_HEADS)
    d_h = _split(dsk, [1] * SSM_HEADS)
    x2s = _split(xs_, [2 * SSM_HD] * (SSM_HEADS // 2))
    bms = _split(bm, [SSM_STATE] * SSM_GROUPS)
    cms = _split(cm, [SSM_STATE] * SSM_GROUPS)
    tril = _tril(c)
    cbs = [mm_nt(cms[g], bms[g]) for g in range(SSM_GROUPS)]
    ys, new = [], []
    for j in range(SSM_HEADS // 2):
        g = (2 * j) // hg
        h0, h1 = 2 * j, 2 * j + 1
        xdt = x2s[j] * _lane_pair(dt_h[h0], dt_h[h1])
        acs2 = _lane_pair(acs_h[h0], acs_h[h1])
        al2 = _lane_pair(al_h[h0], al_h[h1])
        yd = []
        for h in (h0, h1):
            seg = acs_h[h] - _row(acs_t, h)
            lm = jnp.exp(jnp.where(tril, seg, NEG))
            yd.append(mm_nn(cbs[g] * lm, xdt))
        lane = lax.broadcasted_iota(jnp.int32, xdt.shape, 1)
        y_diag = jnp.where(lane < SSM_HD, yd[0], yd[1])
        y_off = mm_nn(cms[g], sts[j]) * jnp.exp(acs2)
        x_end = xdt * jnp.exp(al2 - acs2)
        new.append(sts[j] * jnp.exp(al2) + mm_tn(bms[g], x_end))
        ys.append(y_diag + y_off + _lane_pair(d_h[h0], d_h[h1]) * x2s[j])
    return jnp.concatenate(ys, axis=1), tuple(new)


def f_dil_prep(first, xp, xs, ps, sts):
    (p, pos), (qn, kn, invf, sign) = xs, ps
    nh = len(DIL_GROUPS) * DIL_HEADS
    _, q, k, v = _split(p, [D_XA] + [nh * DIL_HD] * 3)
    ang = pos * invf
    cos, sin = jnp.cos(ang), jnp.sin(ang) * sign

    def rope(t, g):
        hs = _split(t, [DIL_HD] * nh)
        out = []
        for h in hs:
            n = _rms(h, g)
            out.append(n * cos + _roll_half(n) * sin)
        return [jnp.concatenate(out[i:i + DIL_HEADS], axis=1) for i in range(0, nh, DIL_HEADS)]

    return tuple(rope(q, qn) + rope(k, kn) + list(_split(v, [D_DIL] * len(DIL_GROUPS)))), ()


def f_dil_attn(first, xp, xs, ps, sts):
    (kp, vp), (q, k, v) = xp, xs
    Q = DIL_BLOCK
    qs, ks, vs = (_split(t, [DIL_HD] * DIL_HEADS) for t in (q, k, v))
    kps, vps = (_split(t, [DIL_HD] * DIL_HEADS) for t in (kp, vp))
    i = lax.broadcasted_iota(jnp.int32, (Q, 2 * Q), 0)
    j = lax.broadcasted_iota(jnp.int32, (Q, 2 * Q), 1)
    dist = Q + i - j
    mask = (dist >= 0) & (dist <= Q) & (jnp.logical_not(first) | (j >= Q))
    outs, lses = [], []
    for h in range(DIL_HEADS):
        k2 = jnp.concatenate([kps[h], ks[h]], axis=0)
        v2 = jnp.concatenate([vps[h], vs[h]], axis=0)
        s = jnp.where(mask, mm_nt(qs[h], k2) * (DIL_HD ** -0.5), NEG)
        m = lax.stop_gradient(jnp.max(s, axis=-1, keepdims=True))
        p = jnp.exp(s - m)
        l = jnp.sum(p, axis=-1, keepdims=True)
        outs.append(mm_nn(p / l, v2))
        lses.append(jnp.broadcast_to(m + jnp.log(l), (Q, DIL_HD)))
    return (jnp.concatenate(outs, axis=1), jnp.concatenate(lses, axis=1)), ()


def f_dil_merge(first, xp, xs, ps, sts):
    o0, o1, o2, l0, l1, l2 = xs
    m = jnp.maximum(jnp.maximum(l0, l1), l2)
    e0, e1, e2 = jnp.exp(l0 - m), jnp.exp(l1 - m), jnp.exp(l2 - m)
    den = e0 + e1 + e2
    return ((e0 * o0 + e1 * o1 + e2 * o2) / den,), ()


def _loss_head(y, target):
    S, D = y.shape
    R = _tile(S, (512, 256, 128, 64, 32, 16, 8))

    def body(y_ref, t_ref, dy_ref, dyb_ref, l_ref):
        e = y_ref[...] - t_ref[...]
        dy_ref[...] = e * (1.0 / D)
        dyb_ref[...] = (e * (1.0 / D)).astype(BF)

        @pl.when(pl.program_id(0) == 0)
        def _():
            l_ref[...] = jnp.zeros_like(l_ref)

        l_ref[...] += jnp.broadcast_to(0.5 * jnp.sum(jnp.mean(e * e, axis=-1, keepdims=True), axis=0, keepdims=True),
                                       l_ref.shape)

    blk = pl.BlockSpec((R, D), lambda i: (i, 0))
    dy, dyb, l = pl.pallas_call(
        body, name="loss_head", grid=(S // R,), in_specs=[blk, blk],
        out_specs=[blk, blk, pl.BlockSpec((8, 128), lambda i: (0, 0))],
        out_shape=[jax.ShapeDtypeStruct((S, D), F32), jax.ShapeDtypeStruct((S, D), BF),
                   jax.ShapeDtypeStruct((8, 128), F32)],
        compiler_params=pltpu.CompilerParams(dimension_semantics=("arbitrary",)))(y, target)
    return dy, dyb, l[0, 0]


def _adamw(parts, w, m, v, name):
    _, n, width = parts.shape
    tr = _tile(n, [t for t in (512, 256, 128, 64, 32, 16, 8) if t * width <= ADAM_BLOCK])

    def body(p_ref, w_ref, m_ref, v_ref, g_ref, d_ref, nm_ref, nv_ref):
        g = p_ref[0].astype(F32)
        for s in range(1, N_DEV):
            g = g + p_ref[s].astype(F32)
        nm = ADAM_B1 * m_ref[...] + (1.0 - ADAM_B1) * g
        nv = ADAM_B2 * v_ref[...] + (1.0 - ADAM_B2) * (g * g)
        m_hat = nm / (1.0 - ADAM_B1 ** ADAM_STEP)
        v_hat = nv / (1.0 - ADAM_B2 ** ADAM_STEP)
        g_ref[...] = g
        d_ref[...] = -ADAM_LR * (m_hat / (jnp.sqrt(v_hat) + ADAM_EPS) + ADAM_WD * w_ref[...])
        nm_ref[...] = nm
        nv_ref[...] = nv

    blk = pl.BlockSpec((tr, width), lambda i: (i, 0))
    return pl.pallas_call(
        body, name=name, grid=(n // tr,),
        in_specs=[pl.BlockSpec((N_DEV, tr, width), lambda i: (0, i, 0)), blk, blk, blk],
        out_specs=[blk] * 4, out_shape=[jax.ShapeDtypeStruct((n, width), F32)] * 4,
        compiler_params=pltpu.CompilerParams(dimension_semantics=("arbitrary",), vmem_limit_bytes=VMEM_LIMIT))(
            parts, w, m, v)


def _peer(k):
    x, y, c = lax.axis_index("x"), lax.axis_index("y"), lax.axis_index("c")
    px = 1 - x if k & 4 else x
    py = 1 - y if k & 2 else y
    pc = 1 - c if k & 1 else c
    return (px, py, pc), 4 * px + 2 * py + pc


def _my_id():
    return 4 * lax.axis_index("x") + 2 * lax.axis_index("y") + lax.axis_index("c")


def _all_gather(x, name):
    def body(x_ref, out_ref, send, recv, loc):
        me = _my_id()
        mine = pltpu.make_async_copy(x_ref, out_ref.at[me], loc)
        mine.start()
        cps = []
        for k in range(1, N_DEV):
            peer, _ = _peer(k)
            cp = pltpu.make_async_remote_copy(src_ref=x_ref, dst_ref=out_ref.at[me], send_sem=send.at[k - 1],
                                              recv_sem=recv.at[k - 1], device_id=peer,
                                              device_id_type=pl.DeviceIdType.MESH)
            cp.start()
            cps.append(cp)
        for k in range(1, N_DEV):
            peer, pid = _peer(k)
            pltpu.make_async_remote_copy(src_ref=x_ref, dst_ref=out_ref.at[pid], send_sem=send.at[k - 1],
                                         recv_sem=recv.at[k - 1], device_id=peer,
                                         device_id_type=pl.DeviceIdType.MESH).wait_recv()
        for cp in cps:
            cp.wait_send()
        mine.wait()

    return pl.pallas_call(
        body, name=name, out_shape=jax.ShapeDtypeStruct((N_DEV,) + x.shape, x.dtype),
        in_specs=[pl.BlockSpec(memory_space=pl.ANY)], out_specs=pl.BlockSpec(memory_space=pl.ANY),
        scratch_shapes=[pltpu.SemaphoreType.DMA((N_DEV - 1,)), pltpu.SemaphoreType.DMA((N_DEV - 1,)),
                        pltpu.SemaphoreType.DMA],
        compiler_params=pltpu.CompilerParams(has_side_effects=True))(x)


def _exchange_sems(n):
    return [pltpu.SemaphoreType.DMA((n * (N_DEV - 1),)), pltpu.SemaphoreType.DMA((n * (N_DEV - 1),)),
            pltpu.SemaphoreType.DMA((n,))]


def _exchange_copies(g_refs, out_refs, send, recv, loc, with_arrivals):
    me = _my_id()

    def mine(g, o, d):
        return g.at[d] if len(g.shape) == len(o.shape) else g

    local = [pltpu.make_async_copy(mine(g, o, me), o.at[me], loc.at[w]) for w, (g, o) in enumerate(zip(g_refs, out_refs))]
    pushes, arrivals = [], []
    for k in range(1, N_DEV):
        peer, pid = _peer(k)
        for w, (g, o) in enumerate(zip(g_refs, out_refs)):
            s = w * (N_DEV - 1) + k - 1
            ends = [(mine(g, o, pid), o.at[me], pushes)] + ([(mine(g, o, me), o.at[pid], arrivals)] if with_arrivals else [])
            for src, dst, into in ends:
                into.append(pltpu.make_async_remote_copy(src_ref=src, dst_ref=dst, send_sem=send.at[s],
                                                         recv_sem=recv.at[s], device_id=peer,
                                                         device_id_type=pl.DeviceIdType.MESH))
    return local, pushes, arrivals


def _exchange_start(g_refs, out_refs, send, recv, loc):
    local, pushes, _ = _exchange_copies(g_refs, out_refs, send, recv, loc, False)
    for cp in local + pushes:
        cp.start()


def _exchange_wait(g_refs, out_refs, send, recv, loc):
    local, pushes, arrivals = _exchange_copies(g_refs, out_refs, send, recv, loc, True)
    for cp in arrivals:
        cp.wait_recv()
    for cp in pushes:
        cp.wait_send()
    for cp in local:
        cp.wait()


def _exchange_many(gs, name):
    n = len(gs)

    def body(*refs):
        g_refs, out_refs, sems = refs[:n], refs[n:2 * n], refs[2 * n:]
        _exchange_start(g_refs, out_refs, *sems)
        _exchange_wait(g_refs, out_refs, *sems)

    return pl.pallas_call(
        body, name=name, out_shape=[jax.ShapeDtypeStruct(g.shape, g.dtype) for g in gs],
        in_specs=[pl.BlockSpec(memory_space=pl.ANY)] * n, out_specs=[pl.BlockSpec(memory_space=pl.ANY)] * n,
        scratch_shapes=_exchange_sems(n), compiler_params=pltpu.CompilerParams(has_side_effects=True))(*gs)


def _gather_many(xs, name):
    n = len(xs)

    def body(*refs):
        x_refs, out_refs, (send, recv, loc) = refs[:n], refs[n:2 * n], refs[2 * n:]
        x, y, c = lax.axis_index("x"), lax.axis_index("y"), lax.axis_index("c")
        me, sibling = (x, y, c), (x, y, 1 - c)
        chips = [(1 - x, y), (x, 1 - y), (1 - x, 1 - y)]

        def slot(p):
            return 4 * p[0] + 2 * p[1] + p[2]

        def copy(w, k, block, to, src=None):
            dst = out_refs[w].at[slot(block)]
            return pltpu.make_async_remote_copy(src_ref=dst if src is None else src, dst_ref=dst,
                                                send_sem=send.at[w * (N_DEV - 1) + k], recv_sem=recv.at[w * (N_DEV - 1) + k],
                                                device_id=to, device_id_type=pl.DeviceIdType.MESH)

        mine = [pltpu.make_async_copy(x_refs[w], out_refs[w].at[slot(me)], loc.at[w]) for w in range(n)]
        for cp in mine:
            cp.start()
        first = []
        for j, chip in enumerate(chips):
            first += [copy(w, 1 + j, me, (*chip, c), src=x_refs[w]) for w in range(n)]
        first += [copy(w, 0, me, sibling, src=x_refs[w]) for w in range(n)]
        for cp in first:
            cp.start()
        passed = []
        for j, chip in enumerate(chips):
            for w in range(n):
                copy(w, 1 + j, (*chip, c), me).wait_recv()
                cp = copy(w, 4 + j, (*chip, c), sibling)
                cp.start()
                passed.append(cp)
        for w in range(n):
            copy(w, 0, sibling, me).wait_recv()
            for j, chip in enumerate(chips):
                copy(w, 4 + j, (*chip, 1 - c), me).wait_recv()
        for cp in first + passed:
            cp.wait_send()
        for cp in mine:
            cp.wait()

    return pl.pallas_call(
        body, name=name, out_shape=[jax.ShapeDtypeStruct((N_DEV,) + x.shape, x.dtype) for x in xs],
        in_specs=[pl.BlockSpec(memory_space=pl.ANY)] * n, out_specs=[pl.BlockSpec(memory_space=pl.ANY)] * n,
        scratch_shapes=[pltpu.SemaphoreType.DMA((n * (N_DEV - 1),)), pltpu.SemaphoreType.DMA((n * (N_DEV - 1),)),
                        pltpu.SemaphoreType.DMA((n,))],
        compiler_params=pltpu.CompilerParams(has_side_effects=True))(*xs)


def _cat_segs(G, ws, n_mix):
    segs = []
    for g in range(G):
        lo, hi = g * ws, (g + 1) * ws
        if lo < n_mix:
            segs.append((g, 0, min(hi, n_mix) - lo, D_XA + lo))
        if hi > n_mix:
            s = max(lo, n_mix)
            segs.append((g, s - lo, hi - s, s - n_mix))
    return segs


def _cat_cols(src, n_mix, ntot):
    G, R, ws = src.shape
    segs = _cat_segs(G, ws, n_mix)
    tr = _tile(R, (256, 128, 64, 32, 16, 8))

    def body(i_ref, o_ref):
        if ntot > G * ws:
            o_ref[...] = jnp.zeros_like(o_ref)
        for g, s, n, d in segs:
            o_ref[:, d:d + n] = i_ref[g][:, s:s + n]

    return pl.pallas_call(
        body, name="cat_cols", grid=(R // tr,), in_specs=[pl.BlockSpec((G, tr, ws), lambda i: (0, i, 0))],
        out_specs=pl.BlockSpec((tr, ntot), lambda i: (i, 0)), out_shape=jax.ShapeDtypeStruct((R, ntot), src.dtype),
        compiler_params=pltpu.CompilerParams(dimension_semantics=("arbitrary",)))(src)


def _uncat_cols(dw, G, ws, n_mix):
    R, ntot = dw.shape
    segs = _cat_segs(G, ws, n_mix)
    tr = _tile(R, (256, 128, 64, 32, 16, 8))

    def body(i_ref, o_ref):
        v = i_ref[...]
        for g, s, n, d in segs:
            o_ref[g, :, s:s + n] = v[:, d:d + n]

    return pl.pallas_call(
        body, name="uncat_cols", grid=(R // tr,), in_specs=[pl.BlockSpec((tr, ntot), lambda i: (i, 0))],
        out_specs=pl.BlockSpec((G, tr, ws), lambda i: (0, i, 0)), out_shape=jax.ShapeDtypeStruct((G, R, ws), dw.dtype),
        compiler_params=pltpu.CompilerParams(dimension_semantics=("arbitrary",)))(dw)


PACK_W = 1024


def _granule(n):
    return (256 if n >= 256 * PACK_W else 8) * PACK_W


def _pack(arrs, dtype):
    flat = jnp.concatenate([a.reshape(-1).astype(dtype) for a in arrs])
    n = flat.shape[0]
    pad = (-n) % _granule(n)
    if pad:
        flat = jnp.concatenate([flat, jnp.zeros((pad,), dtype)])
    return flat.reshape(-1, PACK_W)


def _unpack(packed, shapes):
    flat = packed.reshape(-1)
    out, o = [], 0
    for s in shapes:
        n = math.prod(s)
        out.append(flat[o:o + n].reshape(s))
        o += n
    return out


def _pack_lead(arrs, dtype):
    flat = jnp.concatenate([a.reshape(N_DEV, -1).astype(dtype) for a in arrs], axis=1)
    n = flat.shape[1]
    pad = (-n) % _granule(n)
    if pad:
        flat = jnp.concatenate([flat, jnp.zeros((N_DEV, pad), dtype)], axis=1)
    return flat.reshape(N_DEV, -1, PACK_W)


def _to_full(stacked, axis):
    t = jnp.moveaxis(stacked, 0, axis)
    s = list(t.shape)
    return t.reshape(s[:axis] + [s[axis] * s[axis + 1]] + s[axis + 2:])


def _to_chunks(full, axis):
    s = list(full.shape)
    t = full.reshape(s[:axis] + [N_DEV, s[axis] // N_DEV] + s[axis + 1:])
    return jnp.moveaxis(t, axis, 0)


def _rows_of(S):
    return _tile(S, (512, 256, 128, 64))


def _norm_fwd(x, g, dt=BF):
    (h,), _ = _seq_fwd("rmsnorm_fwd", f_rmsnorm, _rows_of(x.shape[0]), [_row_spec(x)], [_par_spec(g)], [],
                       [_out_spec(x.shape[1], dt=dt)])
    return h


def _norm_bwd(x, g, dh, res=None):
    if res is None:
        (dx,), (dg,) = _seq_bwd("rmsnorm_bwd", f_rmsnorm, _rows_of(x.shape[0]), [_row_spec(x)], [_par_spec(g)], [],
                                [_row_spec(dh)], [])
        return dx, None, dg
    (dx,), (dg,), (dxb,) = _seq_bwd("rmsnorm_res_bwd", f_rmsnorm, _rows_of(x.shape[0]), [_row_spec(x)], [_par_spec(g)], [],
                                    [_row_spec(dh)], [], dx_add={0: res}, dx_bf=True)
    return dx, dxb, dg


def _mixer_specs(kind, S, p, w):
    if kind == 0:
        return (f_gla, min(S, MIX_ROWS), [_row_spec(p)],
                [_par_spec(w['a_w_gate2']), _par_spec(w['a_b_gate'].reshape(1, -1)), _par_spec(w['a_o_norm'].reshape(1, -1))],
                [(GLA_DV, GLA_DK)] * GLA_HEADS, D_MIX)
    if kind == 2:
        return (f_ssd, min(S, MIX_ROWS), [_row_spec(p, prev='halo')],
                [_par_spec(w['c_conv_w']), _par_spec(w['c_conv_b'].reshape(1, -1)), _par_spec(w['c_dt_bias'].reshape(1, -1)),
                 _par_spec(w['c_a_log'].reshape(1, -1)), _par_spec(w['c_d'].reshape(1, -1)),
                 _par_spec(w['c_norm'].reshape(1, -1))],
                [(SSM_STATE, 2 * SSM_HD)] * (SSM_HEADS // 2), D_MIX)
    return (f_hgrn, CHUNK, [_row_spec(p)],
            [_par_spec(w['d_lower_bounds']), _par_spec(w['d_o_norm'].reshape(1, -1))],
            [(HGRN_DV, HGRN_DK)] * HGRN_HEADS, D_MIX)


def _perm(t, r):
    if r == 1:
        return t
    S, n = t.shape
    return t.reshape(S // r, r, n).transpose(1, 0, 2).reshape(S, n)


def _unperm(t, r):
    if r == 1:
        return t
    S, n = t.shape
    return t.reshape(r, S // r, n).transpose(1, 0, 2).reshape(S, n)


def _rope_consts():
    half = DIL_HD // 2
    inv = ROPE_THETA ** (-jnp.arange(half, dtype=F32) / half)
    invf = jnp.concatenate([inv, inv]).reshape(1, DIL_HD)
    sign = jnp.concatenate([-jnp.ones((half,), F32), jnp.ones((half,), F32)]).reshape(1, DIL_HD)
    return invf, sign


def _dil_fwd(p, pos, w, ncat):
    S = p.shape[0]
    invf, sign = _rope_consts()
    prep_rows = [_row_spec(p), _row_spec(pos, diff=False)]
    prep_pars = [_par_spec(w['b_q_norm'].reshape(1, -1)), _par_spec(w['b_k_norm'].reshape(1, -1)),
                 _par_spec(invf, diff=False), _par_spec(sign, diff=False)]
    ng = len(DIL_GROUPS)
    qkv, _ = _seq_fwd("dil_prep_fwd", f_dil_prep, _tile(S, (256, 128)), prep_rows, prep_pars, [],
                      [_out_spec(D_DIL) for _ in range(3 * ng)])
    res = dict(perm=[], o=[], lse=[])
    for g, (window, r) in enumerate(DIL_GROUPS):
        qp, kp, vp = _perm(qkv[g], r), _perm(qkv[ng + g], r), _perm(qkv[2 * ng + g], r)
        rows = [_row_spec(qp), _row_spec(kp, prev='block'), _row_spec(vp, prev='block')]
        (o, lse), _ = _seq_fwd("dil_attn_fwd", f_dil_attn, DIL_BLOCK, rows, [], [], [_out_spec(D_DIL), _out_spec(D_DIL)],
                               period=S // r // DIL_BLOCK)
        res['perm'].append((qp, kp, vp))
        res['o'].append(_unperm(o, r))
        res['lse'].append(_unperm(lse, r))
    mrows = [_row_spec(t) for t in res['o'] + res['lse']]
    (cat,), _ = _seq_fwd("dil_merge_fwd", f_dil_merge, _rows_of(S), mrows, [], [], [_out_spec(ncat, w=D_DIL, dt=BF)])
    res['prep'] = (prep_rows, prep_pars)
    return cat, res


def _dil_bwd(dtok, res, p):
    S = p.shape[0]
    mrows = [_row_spec(t) for t in res['o'] + res['lse']]
    dm, _ = _seq_bwd("dil_merge_bwd", f_dil_merge, _rows_of(S), mrows, [], [], [dtok], [])
    dq, dk, dv = [], [], []
    for g, (window, r) in enumerate(DIL_GROUPS):
        qp, kp, vp = res['perm'][g]
        rows = [_row_spec(qp), _row_spec(kp, prev='block'), _row_spec(vp, prev='block')]
        douts = [_row_spec(_perm(dm[g], r)), _row_spec(_perm(dm[3 + g], r))]
        (a, b, c), _ = _seq_bwd("dil_attn_bwd", f_dil_attn, DIL_BLOCK, rows, [], [], douts, [],
                                period=S // r // DIL_BLOCK)
        dq.append(_unperm(a, r)); dk.append(_unperm(b, r)); dv.append(_unperm(c, r))
    prep_rows, prep_pars = res['prep']
    (dp,), (dqn, dkn) = _seq_bwd("dil_prep_bwd", f_dil_prep, _tile(S, (256, 128)), prep_rows, prep_pars, [],
                                 [_row_spec(t) for t in dq + dk + dv], [], dx_dt=BF)
    return dp, dict(b_q_norm=dqn.reshape(-1), b_k_norm=dkn.reshape(-1))


def _ffn_specs(u, cw, cb):
    half = N_DEV // 2
    rows = [_row_spec(u, prev='halo', lb=(2, None), li=lambda jc: (0, jc))]
    pars = [_par_spec(cw, bs=(None, FFN_CONV, FF_SH), idx=lambda jc: (jc, 0, 0)),
            _par_spec(cw, bs=(None, FFN_CONV, FF_SH), idx=lambda jc: (jc + half, 0, 0)),
            _par_spec(cb, bs=(None, 1, FF_SH), idx=lambda jc: (jc, 0, 0)),
            _par_spec(cb, bs=(None, 1, FF_SH), idx=lambda jc: (jc + half, 0, 0))]
    return half, rows, pars


N_MIX = {0: 2 * GLA_HEADS * GLA_DK + 2 * D_MIX + GLA_RANK, 1: 3 * len(DIL_GROUPS) * D_DIL,
         2: 2 * D_MIX + 2 * SSM_GROUPS * SSM_STATE + SSM_HEADS, 3: 2 * HGRN_HEADS * HGRN_DK + 2 * D_MIX}
W_IN = {0: 'a_w_in', 1: 'b_w_in', 2: 'c_w_in', 3: 'd_w_in'}
W_OUT = {0: 'a_w_out', 1: 'b_w_out', 2: 'c_w_out', 3: 'd_w_out'}


def _in_blocks(name, t):
    return t if SHARD_AXIS[name] == 1 else t.reshape(1, N_DEV * t.shape[1], t.shape[2])


LAYER_STACKED = ('ffn_w_up', 'ffn_conv_w', 'ffn_w_down', 'xa_w_kv')


SMALL_OF_KIND = {0: ['a_w_gate2'], 2: ['c_conv_w']}


def _layer_names(i):
    return list(LAYER_STACKED) + [W_IN[i % 4], W_OUT[i % 4]] + SMALL_OF_KIND.get(i % 4, [])


def _device_step(x, mem, pos, sh, rep, target, distributed=True):
    S, D = x.shape
    w = dict(rep)
    posf = pos.reshape(S, 1).astype(F32)
    n_mix, w_in_name, w_out_name = N_MIX, W_IN, W_OUT
    ntot = {k: -(-(n_mix[k] + D_XA) // 256) * 256 for k in n_mix}
    mem_g = w['mem_norm'].reshape(1, -1)
    mem_n = _norm_fwd(mem, mem_g)
    R = _rows_of(S)

    def mine(i):
        return {n: (sh[n][i] if n in LAYER_STACKED else sh[n]) for n in _layer_names(i)}

    if distributed:
        gl = dict(zip(_layer_names(0), _gather_many(list(mine(0).values()), "gather_weights")))
    else:
        gl = {n: (sh[n][:, 0] if n in LAYER_STACKED else sh[n]) for n in _layer_names(0)}

    saved = []
    for i in range(DEPTH):
        kind = i % 4
        L = dict(x0=x)
        for n in SMALL_OF_KIND.get(kind, []):
            w[n] = _to_full(gl[n], 1)
        in_blocks = _in_blocks(w_in_name[kind], gl[w_in_name[kind]])
        w_out = (_to_full(gl[w_out_name[kind]], 1) if SHARD_AXIS[w_out_name[kind]] == 1
                 else gl[w_out_name[kind]].reshape(-1, D))
        nxt, push = {}, [[], [], []]
        if i + 1 < DEPTH:
            if distributed:
                nxt = mine(i + 1)
                push = [[n for n in nxt if n not in ('ffn_w_up', w_in_name[(i + 1) % 4], w_out_name[(i + 1) % 4])],
                        ['ffn_w_up'], [w_in_name[(i + 1) % 4], w_out_name[(i + 1) % 4]]]
            else:
                nxt = {n: (sh[n][:, i + 1] if n in LAYER_STACKED else sh[n]) for n in _layer_names(i + 1)}
        got = dict(nxt) if not distributed else {}

        def hosted(call, names):
            if not names:
                return call()
            res, arrived = call(gather=[nxt[n] for n in names])
            got.update(zip(names, arrived))
            return res

        g1 = w['mix_norm'][i].reshape(1, -1)
        h = _norm_fwd(x, g1)
        wcat = _cat_cols(in_blocks, n_mix[kind], ntot[kind])
        p = hosted(functools.partial(_matmul, h, wcat, name="matmul_in"), push[0])
        ntok = D_DIL if kind == 1 else D_MIX
        if kind == 1:
            cat, L['dil'] = _dil_fwd(p, posf, w, ntok + D_XA)
        else:
            f, Rm, rows, pars, sshapes, _ = _mixer_specs(kind, S, p, w)
            (cat,), L['states'] = _seq_fwd("mixer%d_fwd" % kind, f, Rm, rows, pars, sshapes,
                                           [_out_spec(ntok + D_XA, w=ntok, dt=BF)], save_states=True)
        wkv = gl['xa_w_kv'].reshape(D, 2 * D_XA)
        kv = _matmul(mem_n, wkv, name="matmul_kv")
        xa_rows = [_row_spec(p, w=D_XA, dn=D_XA)]
        xa_pars = [_par_spec(kv), _par_spec(w['xa_q_norm'][i].reshape(1, -1)), _par_spec(w['xa_k_norm'][i].reshape(1, -1))]
        (cat,), _ = _seq_fwd("xattn_fwd", f_xattn, R, xa_rows, xa_pars, [],
                             [_out_spec(ntok + D_XA, w=D_XA, c=lambda jc: ntok // D_XA, dt=BF)], out_alias={0: cat})
        x1 = _matmul(cat, w_out, add=x, name="matmul_out")
        g2 = w['ffn_norm'][i].reshape(1, -1)
        h2 = _norm_fwd(x1, g2)
        wup = gl['ffn_w_up']
        u = hosted(functools.partial(_ffn_up, h2, wup), push[1])
        cw, cb = gl['ffn_conv_w'], w['ffn_conv_b'][i].reshape(N_DEV, 1, FF_SH)
        nt, frows, fpars = _ffn_specs(u, cw, cb)
        (act,), _ = _seq_fwd("ffn_act_fwd", f_ffn_act, R, frows, fpars, [],
                             [_out_spec(FF_SH, dt=BF, ls=(nt,), lb=(None,), li=lambda jc: (jc,))], ncol=nt)
        wd = gl['ffn_w_down'].reshape(D_FF, D)
        x = hosted(functools.partial(_ffn_down, act, wd, x1), push[2])
        L.update(h=h, p=p, wcat=wcat, kv=kv, wkv=wkv, cat=cat, x1=x1, h2=h2, u=u, act=act, wd=wd, wup=wup, cw=cw, g1=g1,
                 g2=g2, in_blocks=in_blocks, w_out=w_out, shapes={n: t.shape for n, t in gl.items()})
        saved.append(L)
        gl = got

    dx, dxb, loss = _loss_head(x, target)

    G = {}
    d_mem_n = None
    acc = {k: [None] * DEPTH for k in ('mix_norm', 'ffn_norm', 'ffn_conv_b', 'xa_q_norm', 'xa_k_norm')}
    parts = [{} for _ in range(DEPTH)]
    pending = {}
    half = N_DEV // 2

    def sent(call, blocks, layer):
        if not blocks:
            return call()
        if not distributed:
            parts[layer].update(blocks)
            return call()
        res, arrived = call(exchange=list(blocks.values()))
        parts[layer].update(zip(blocks, arrived))
        return res

    for i in reversed(range(DEPTH)):
        kind = i % 4
        L = saved[i]
        Gc = {}
        Gc['ffn_w_down'] = _ffn_dw_down(L['act'], dxb).reshape(N_DEV, D_FF // N_DEV, D)
        dact = _ffn_dact(dxb, L['wd'])
        cw, cb = L['cw'], w['ffn_conv_b'][i].reshape(N_DEV, 1, FF_SH)
        nt, frows, fpars = _ffn_specs(L['u'], cw, cb)
        (du,), (dwg, dwv, dbg, dbv) = _seq_bwd(
            "ffn_act_bwd", f_ffn_act, R, frows, fpars, [], [_row_spec(dact, lb=(None,), li=lambda jc: (jc,))], [],
            ncol=nt, dx_dt=BF)
        Gc['ffn_conv_w'] = jnp.concatenate([dwg[:half], dwv[half:]], axis=0)
        acc['ffn_conv_b'][i] = jnp.concatenate([dbg[:half], dbv[half:]], axis=0).reshape(-1)
        Gc['ffn_w_up'] = _ffn_dw_up(L['h2'], du)
        dh2 = sent(functools.partial(_ffn_dh2, du, L['wup']), pending, i + 1)
        dx1, dx1b, dg2 = _norm_bwd(L['x1'], L['g2'], dh2, res=dx)
        acc['ffn_norm'][i] = dg2.reshape(-1)
        G_out = _matmul(L['cat'], dx1b, mode="tn", out_dtype=BF, name="matmul_dw_out")
        dcat = _matmul(dx1b, L['w_out'], mode="nt", name="matmul_dcat")
        ntok = D_DIL if kind == 1 else D_MIX
        dtok = _row_spec(dcat, w=ntok)
        dxa = _row_spec(dcat, w=D_XA, c=lambda jc: ntok // D_XA)
        p = L['p']
        if kind == 1:
            dp, gm = _dil_bwd(dtok, L['dil'], p)
            G.update(gm)
        else:
            f, Rm, rows, pars, sshapes, _ = _mixer_specs(kind, S, p, w)
            (dp,), dps = _seq_bwd("mixer%d_bwd" % kind, f, Rm, rows, pars, sshapes, [dtok], L['states'], dx_dt=BF)
            if kind == 0:
                Gc['a_w_gate2'], G['a_b_gate'], G['a_o_norm'] = _to_chunks(dps[0], 1), dps[1].reshape(-1), dps[2].reshape(-1)
            elif kind == 2:
                Gc['c_conv_w'] = _to_chunks(dps[0], 1)
                for nme, v in zip(('c_conv_b', 'c_dt_bias', 'c_a_log', 'c_d', 'c_norm'), dps[1:]):
                    G[nme] = v.reshape(-1)
            else:
                G['d_lower_bounds'], G['d_o_norm'] = dps[0], dps[1].reshape(-1)
        xa_rows = [_row_spec(p, w=D_XA)]
        xa_pars = [_par_spec(L['kv']), _par_spec(w['xa_q_norm'][i].reshape(1, -1)), _par_spec(w['xa_k_norm'][i].reshape(1, -1))]
        (dp,), (dkv, dqn, dkn) = _seq_bwd("xattn_bwd", f_xattn, R, xa_rows, xa_pars, [], [dxa], [], dx_dt=BF,
                                          dx_alias={0: dp})
        acc['xa_q_norm'][i], acc['xa_k_norm'][i] = dqn.reshape(-1), dkn.reshape(-1)
        Gc['xa_w_kv'] = _matmul(mem_n, dkv, mode="tn", out_dtype=BF, name="matmul_dw_kv").reshape(
            N_DEV, D // N_DEV, 2 * D_XA)
        d_mem_n = _matmul(dkv, L['wkv'], mode="nt", add=d_mem_n, name="matmul_dmem" + ("" if d_mem_n is None else "_acc"))
        dwcat = sent(functools.partial(_matmul, L['h'], dp, mode="tn", out_dtype=BF, name="matmul_dw_in"),
                     {'ffn_w_up': Gc.pop('ffn_w_up')}, i)
        blocks = L['in_blocks']
        Gc[w_in_name[kind]] = _uncat_cols(dwcat, blocks.shape[0], blocks.shape[2], n_mix[kind]).reshape(
            L['shapes'][w_in_name[kind]])
        Gc[w_out_name[kind]] = (_to_chunks(G_out, 1) if SHARD_AXIS[w_out_name[kind]] == 1
                                else G_out.reshape(L['shapes'][w_out_name[kind]]))
        dh = sent(functools.partial(_matmul, dp, L['wcat'], mode="nt", name="matmul_dh"),
                  {n: Gc.pop(n) for n in ('ffn_w_down', 'ffn_conv_w')}, i)
        dx, dxb, dg1 = _norm_bwd(L['x0'], L['g1'], dh, res=dx1)
        acc['mix_norm'][i] = dg1.reshape(-1)
        pending = Gc

    if distributed:
        names = list(pending)
        parts[0].update(zip(names, _exchange_many([pending[n] for n in names], "exchange_grads")))
    else:
        parts[0].update(pending)
    _, _, dmg = _norm_bwd(mem, mem_g, d_mem_n)
    G['mem_norm'] = dmg.reshape(-1)
    for k, v in acc.items():
        G[k] = jnp.stack(v)
    got = {}
    for i in range(DEPTH):
        for n, t in parts[i].items():
            if n not in LAYER_STACKED:
                got[n] = t
    for n in LAYER_STACKED:
        got[n] = jnp.stack([parts[i][n] for i in range(DEPTH)], axis=1)
    return loss, dx, got, G


def kernel(x, mem, positions, mem_norm, mix_norm, xa_w_kv, xa_q_norm, xa_k_norm, ffn_norm, ffn_w_up, ffn_conv_w, ffn_conv_b, ffn_w_down, a_w_in, a_w_gate2, a_b_gate, a_o_norm, a_w_out, b_w_in, b_q_norm, b_k_norm, b_w_out, c_w_in, c_conv_w, c_conv_b, c_dt_bias, c_a_log, c_d, c_norm, c_w_out, d_w_in, d_lower_bounds, d_o_norm, d_w_out, loss_target, m_mem_norm, m_mix_norm, m_xa_w_kv, m_xa_q_norm, m_xa_k_norm, m_ffn_norm, m_ffn_w_up, m_ffn_conv_w, m_ffn_conv_b, m_ffn_w_down, m_a_w_in, m_a_w_gate2, m_a_b_gate, m_a_o_norm, m_a_w_out, m_b_w_in, m_b_q_norm, m_b_k_norm, m_b_w_out, m_c_w_in, m_c_conv_w, m_c_conv_b, m_c_dt_bias, m_c_a_log, m_c_d, m_c_norm, m_c_w_out, m_d_w_in, m_d_lower_bounds, m_d_o_norm, m_d_w_out, v_mem_norm, v_mix_norm, v_xa_w_kv, v_xa_q_norm, v_xa_k_norm, v_ffn_norm, v_ffn_w_up, v_ffn_conv_w, v_ffn_conv_b, v_ffn_w_down, v_a_w_in, v_a_w_gate2, v_a_b_gate, v_a_o_norm, v_a_w_out, v_b_w_in, v_b_q_norm, v_b_k_norm, v_b_w_out, v_c_w_in, v_c_conv_w, v_c_conv_b, v_c_dt_bias, v_c_a_log, v_c_d, v_c_norm, v_c_w_out, v_d_w_in, v_d_lower_bounds, v_d_o_norm, v_d_w_out):
    args = locals()
    w = {n: args[n] for n in WEIGHTS}
    m = {n: args['m_' + n] for n in WEIGHTS}
    v = {n: args['v_' + n] for n in WEIGHTS}

    big = [n for n in SHARDED if w[n].size >= 65536]
    small = [n for n in SHARDED if n not in big]
    sh = {n: (w[n].astype(BF) if n in big else w[n]) for n in SHARDED}
    loss, grad_x, parts, G = _device_step(x[0], mem[0], positions[0], sh, {n: w[n] for n in REPLICATED}, loss_target[0])
    loss = lax.psum(loss, ("x", "y", "c"))
    rep_parts = _all_gather(_pack([G[n] for n in REPLICATED], F32), "gather_replicated_grads")

    out = {}

    def put(names, res, shapes):
        for kind, r in zip(("grad", "delta", "new_m", "new_v"), res):
            for n, t in zip(names, _unpack(r, shapes)):
                out[kind + "_" + n] = t

    for n in big:
        shp = tuple(w[n].shape)
        two_d = (math.prod(shp[:-1]), shp[-1])
        res = _adamw(parts[n].reshape((N_DEV,) + two_d), w[n].reshape(two_d), m[n].reshape(two_d), v[n].reshape(two_d),
                     "adamw")
        for kind, r in zip(("grad", "delta", "new_m", "new_v"), res):
            out[kind + "_" + n] = r.reshape(shp)
    for names, prt, tag in ((small, _pack_lead([parts[n] for n in small], F32), "adamw_small"),
                            (REPLICATED, rep_parts, "adamw_replicated")):
        res = _adamw(prt, _pack([w[n] for n in names], F32), _pack([m[n] for n in names], F32),
                     _pack([v[n] for n in names], F32), tag)
        put(names, res, [tuple(w[n].shape) for n in names])
    return (loss, grad_x[None], *[out["grad_" + n] for n in WEIGHTS], *[out["delta_" + n] for n in WEIGHTS],
            *[out["new_m_" + n] for n in WEIGHTS], *[out["new_v_" + n] for n in WEIGHTS])
```

```python
import functools
import math

import jax
import jax.numpy as jnp
from jax import lax
from jax.experimental import pallas as pl
from jax.experimental.pallas import tpu as pltpu

F32 = jnp.float32
BF = jnp.bfloat16
_MM_DTYPE = BF

N_DEV = 8
EPS = 1e-6
ROPE_THETA = 10000.0
CHUNK = 64
MIX_ROWS = 256
D_MIX = 768
XA_HEADS, XA_HD, D_XA = 4, 64, 256
GLA_HEADS, GLA_DK, GLA_DV, GLA_RANK, GLA_GATE_NORM = 4, 96, 192, 16, 16.0
DIL_GROUPS = ((128, 1), (512, 4), (2048, 16))
DIL_HEADS, DIL_HD, DIL_BLOCK, D_DIL = 4, 128, 128, 512
SSM_HD, SSM_HEADS, SSM_GROUPS, SSM_STATE, SSM_CONV = 64, 12, 2, 128, 4
HGRN_HEADS, HGRN_DK, HGRN_DV = 6, 128, 128
D_FF = 2816
FFN_CONV = 3
DEPTH = 4
ADAM_LR, ADAM_B1, ADAM_B2, ADAM_EPS, ADAM_WD, ADAM_STEP = 0.001, 0.9, 0.999, 1e-08, 0.01, 10
NEG = -1e30
HALO = 8
VMEM_LIMIT = 56 << 20
ADAM_BLOCK = 1 << 18

WEIGHTS = ['mem_norm', 'mix_norm', 'xa_w_kv', 'xa_q_norm', 'xa_k_norm', 'ffn_norm', 'ffn_w_up', 'ffn_conv_w',
           'ffn_conv_b', 'ffn_w_down', 'a_w_in', 'a_w_gate2', 'a_b_gate', 'a_o_norm', 'a_w_out', 'b_w_in', 'b_q_norm',
           'b_k_norm', 'b_w_out', 'c_w_in', 'c_conv_w', 'c_conv_b', 'c_dt_bias', 'c_a_log', 'c_d', 'c_norm', 'c_w_out',
           'd_w_in', 'd_lower_bounds', 'd_o_norm', 'd_w_out']
SHARD_AXIS = {'xa_w_kv': 1, 'ffn_w_up': 2, 'ffn_conv_w': 2, 'ffn_w_down': 1, 'a_w_in': 1, 'a_w_gate2': 1, 'a_w_out': 0,
              'b_w_in': 1, 'b_w_out': 1, 'c_w_in': 0, 'c_conv_w': 1, 'c_w_out': 0, 'd_w_in': 1, 'd_w_out': 0}
SHARDED = [n for n in WEIGHTS if n in SHARD_AXIS]
REPLICATED = [n for n in WEIGHTS if n not in SHARD_AXIS]


def _dot(a, b, ca, cb):
    return lax.dot_general(a.astype(_MM_DTYPE), b.astype(_MM_DTYPE), (((ca,), (cb,)), ((), ())),
                           preferred_element_type=F32)


@jax.custom_vjp
def mm_nn(a, b):
    return _dot(a, b, 1, 0)


mm_nn.defvjp(lambda a, b: (_dot(a, b, 1, 0), (a, b)),
             lambda r, g: (_dot(g, r[1], 1, 1), _dot(r[0], g, 0, 0)))


@jax.custom_vjp
def mm_nt(a, b):
    return _dot(a, b, 1, 1)


mm_nt.defvjp(lambda a, b: (_dot(a, b, 1, 1), (a, b)),
             lambda r, g: (_dot(g, r[1], 1, 0), _dot(g, r[0], 0, 0)))


@jax.custom_vjp
def mm_tn(a, b):
    return _dot(a, b, 0, 0)


mm_tn.defvjp(lambda a, b: (_dot(a, b, 0, 0), (a, b)),
             lambda r, g: (_dot(r[1], g, 1, 1), _dot(r[0], g, 1, 0)))


def _dot_hi(a, b, ca, cb):
    return lax.dot_general(a, b, (((ca,), (cb,)), ((), ())), precision=lax.Precision.HIGHEST,
                           preferred_element_type=F32)


def _tril(c):
    return lax.broadcasted_iota(jnp.int32, (c, c), 0) >= lax.broadcasted_iota(jnp.int32, (c, c), 1)


@jax.custom_vjp
def cumsum_rows(x):
    return _dot_hi(_tril(x.shape[0]).astype(F32), x, 1, 0)


cumsum_rows.defvjp(lambda x: (cumsum_rows(x), None),
                   lambda r, g: (_dot_hi(_tril(g.shape[0]).astype(F32), g, 0, 0),))


@jax.custom_vjp
def cumsum_rows_t(x):
    return _dot_hi(x, _tril(x.shape[0]).astype(F32), 0, 1)


cumsum_rows_t.defvjp(lambda x: (cumsum_rows_t(x), None),
                     lambda r, g: (_dot_hi(_tril(g.shape[1]).astype(F32), g, 0, 1),))


def _split(x, sizes):
    sizes = tuple(int(s) for s in sizes)
    assert sum(sizes) == x.shape[-1], (sizes, x.shape)

    @jax.custom_vjp
    def sp(x):
        out, o = [], 0
        for s in sizes:
            out.append(x[:, o:o + s])
            o += s
        return tuple(out)

    sp.defvjp(lambda x: (sp(x), None), lambda r, g: (jnp.concatenate(list(g), axis=1),))
    return sp(x)


def _row(x, r):
    m = lax.broadcasted_iota(jnp.int32, x.shape, 0) == r
    return jnp.sum(jnp.where(m, x, 0.0), axis=0, keepdims=True)


@jax.custom_vjp
def _roll_half(x):
    return pltpu.roll(x, 64, 1)


_roll_half.defvjp(lambda x: (pltpu.roll(x, 64, 1), None), lambda r, g: (pltpu.roll(g, 64, 1),))


def _shift(xp, x, d):
    if d == 0:
        return x
    n, m = x.shape[0], xp.shape[0]
    assert d <= m == HALO

    @jax.custom_vjp
    def sh(xp, x):
        r = pltpu.roll(x, d, 0)
        row = lax.broadcasted_iota(jnp.int32, xp.shape, 0)
        head = jnp.where(row < d, pltpu.roll(xp, d, 0), r[:m])
        return jnp.concatenate([head, r[m:]], axis=0)

    def bwd(_, g):
        row = lax.broadcasted_iota(jnp.int32, g.shape, 0)
        rowp = lax.broadcasted_iota(jnp.int32, (m,) + g.shape[1:], 0)
        dxp = jnp.where(rowp >= m - d, pltpu.roll(g[:m], m - d, 0), 0.0)
        return dxp, jnp.where(row < n - d, pltpu.roll(g, n - d, 0), 0.0)

    sh.defvjp(lambda xp, x: (sh(xp, x), None), bwd)
    return sh(xp, x)


def _rms(x, g):
    return x * lax.rsqrt(jnp.mean(x * x, axis=-1, keepdims=True) + EPS) * g


def _lane_pair(a, b, width=128):
    shape = a.shape[:-1] + (width,)
    lane = lax.broadcasted_iota(jnp.int32, shape, len(shape) - 1)
    return jnp.where(lane < width // 2, a, b)


def _row_spec(a, w=None, c=None, prev=False, diff=True, dn=None, lb=(), li=None):
    return dict(a=a, w=a.shape[-1] if w is None else w, c=(lambda jc: 0) if c is None else c, prev=prev, diff=diff,
                dn=a.shape[-1] if dn is None else dn, lb=tuple(lb), li=(lambda jc: ()) if li is None else li)


def _par_spec(a, bs=None, idx=None, diff=True):
    nd = a.ndim
    return dict(a=a, bs=tuple(a.shape) if bs is None else tuple(bs),
                idx=(lambda jc: (0,) * nd) if idx is None else idx, diff=diff)


def _out_spec(n, w=None, c=None, dt=F32, ls=(), lb=(), li=None):
    return dict(n=n, w=n if w is None else w, c=(lambda jc: 0) if c is None else c, dt=dt, ls=tuple(ls), lb=tuple(lb),
                li=(lambda jc: ()) if li is None else li)


def _cparams():
    return pltpu.CompilerParams(dimension_semantics=("arbitrary", "arbitrary"), vmem_limit_bytes=VMEM_LIMIT)


def _bspec(s, R, rowfn):
    return pl.BlockSpec(s['lb'] + (R, s['w']),
                        functools.partial(lambda jc, i, s: tuple(s['li'](jc)) + (rowfn(i), s['c'](jc)), s=s))


def _prev_rows(s, R):
    return R if s['prev'] == 'block' else HALO


def _pspec(s, R, blockfn):
    pr = _prev_rows(s, R)
    return pl.BlockSpec(s['lb'] + (pr, s['w']), functools.partial(
        lambda jc, i, s: tuple(s['li'](jc)) + (jnp.maximum(blockfn(i) * (R // pr) - 1, 0), s['c'](jc)), s=s))


def _seq_fwd(name, f, R, rows, params, state_shapes, outs, *, ncol=1, period=None, save_states=False, out_alias=None):
    nrows = rows[0]['a'].shape[-2]
    nb = nrows // R
    assert nb * R == nrows
    period = nb if period is None else period
    prev_ids = [k for k, r in enumerate(rows) if r['prev']]
    n_rows, n_prev, n_par, n_out, n_st = len(rows), len(prev_ids), len(params), len(outs), len(state_shapes)

    def body(*refs):
        o = 0
        cur = refs[o:o + n_rows]; o += n_rows
        prv = refs[o:o + n_prev]; o += n_prev
        par = refs[o:o + n_par]; o += n_par + len(out_alias or {})
        out = refs[o:o + n_out]; o += n_out
        sav = refs[o:o + (n_st if save_states else 0)]; o += len(sav)
        st = refs[o:o + n_st]
        i = pl.program_id(1)
        first = (i % period) == 0

        @pl.when(i == 0)
        def _():
            for s in st:
                s[...] = jnp.zeros_like(s)

        xs = [r[...].astype(F32) for r in cur]
        xp = [r[...].astype(F32) for r in prv]
        ps = [r[...] for r in par]
        sts = [s[...] for s in st]
        for sv, s in zip(sav, sts):
            sv[0] = s
        ov, ns = f(first, xp, xs, ps, sts)
        for r, v in zip(out, ov):
            r[...] = v.astype(r.dtype)
        for s, v in zip(st, ns):
            s[...] = v

    in_specs = [_bspec(r, R, lambda i: i) for r in rows]
    in_specs += [_pspec(rows[k], R, lambda i: i) for k in prev_ids]
    in_specs += [pl.BlockSpec(p['bs'], functools.partial(lambda jc, i, idx: idx(jc), idx=p['idx'])) for p in params]
    out_specs = [_bspec(o_, R, lambda i: i) for o_ in outs]
    out_shape = [jax.ShapeDtypeStruct(o_['ls'] + (nrows, o_['n']), o_['dt']) for o_ in outs]
    if save_states:
        for s in state_shapes:
            out_specs.append(pl.BlockSpec((1,) + tuple(s), lambda jc, i, nd=len(s): (i,) + (0,) * nd))
            out_shape.append(jax.ShapeDtypeStruct((nb,) + tuple(s), F32))
    args = [r['a'] for r in rows] + [rows[k]['a'] for k in prev_ids] + [p['a'] for p in params]
    aliases = {}
    for n_, arr in sorted((out_alias or {}).items()):
        assert arr.shape == out_shape[n_].shape and arr.dtype == out_shape[n_].dtype
        aliases[len(args)] = n_
        args.append(arr)
        in_specs.append(pl.BlockSpec(memory_space=pl.ANY))
    res = pl.pallas_call(
        body, name=name, grid=(ncol, nb), in_specs=in_specs, out_specs=out_specs, out_shape=out_shape,
        scratch_shapes=[pltpu.VMEM(tuple(s), F32) for s in state_shapes], input_output_aliases=aliases,
        compiler_params=_cparams())(*args)
    return list(res[:n_out]), list(res[n_out:])


def _seq_bwd(name, f, R, rows, params, state_shapes, douts, saved, *, ncol=1, period=None, dx_dt=F32, dx_add=None,
             dx_bf=False, dx_alias=None, exchange=None):
    ex = list(exchange or [])
    n_ex = len(ex)
    nrows = rows[0]['a'].shape[-2]
    nb = nrows // R
    period = nb if period is None else period
    prev_ids = [k for k, r in enumerate(rows) if r['prev']]
    drow_ids = [k for k, r in enumerate(rows) if r['diff']]
    dpar_ids = [k for k, p in enumerate(params) if p['diff']]
    for k in prev_ids:
        assert rows[k]['diff']
    dx_add, dx_alias = dict(dx_add or {}), dict(dx_alias or {})
    add_ids, alias_ids = sorted(dx_add), sorted(dx_alias)
    n_rows, n_prev, n_par, n_do, n_st = len(rows), len(prev_ids), len(params), len(douts), len(state_shapes)
    n_dx, n_dp, n_add, n_al = len(drow_ids), len(dpar_ids), len(add_ids), len(alias_ids)

    def body(*refs):
        o = 0
        cur = refs[o:o + n_rows]; o += n_rows
        prv = refs[o:o + n_prev]; o += n_prev
        par = refs[o:o + n_par]; o += n_par
        sav = refs[o:o + n_st]; o += n_st
        dou = refs[o:o + n_do]; o += n_do
        adr = refs[o:o + n_add]; o += n_add
        o += n_al
        exg = refs[o:o + n_ex]; o += n_ex
        dxr = refs[o:o + n_dx]; o += n_dx
        dpr = refs[o:o + n_dp]; o += n_dp
        dxb = refs[o:o + (n_dx if dx_bf else 0)]; o += len(dxb)
        exr = refs[o:o + n_ex]; o += n_ex
        dst = refs[o:o + n_st]; o += n_st
        car = refs[o:o + n_prev]; o += n_prev
        sems = refs[o:]
        j = pl.program_id(1)
        i = nb - 1 - j
        first = (i % period) == 0
        if n_ex:
            @pl.when((pl.program_id(0) == 0) & (j == 0))
            def _():
                _exchange_start(exg, exr, *sems)

        @pl.when(j == 0)
        def _():
            for s in tuple(dst) + tuple(car) + tuple(dpr):
                s[...] = jnp.zeros_like(s)

        xs = [r[...].astype(F32) for r in cur]
        xp = [r[...].astype(F32) for r in prv]
        ps = [r[...] for r in par]
        sts = [s[0] for s in sav]

        def g(dxs, dxp, dps, dsts):
            xs_, ps_ = list(xs), list(ps)
            for k, v in zip(drow_ids, dxs):
                xs_[k] = v
            for k, v in zip(dpar_ids, dps):
                ps_[k] = v
            ov, ns = f(first, list(dxp), xs_, ps_, list(dsts))
            return tuple(ov), tuple(ns)

        _, vjp = jax.vjp(g, tuple(xs[k] for k in drow_ids), tuple(xp), tuple(ps[k] for k in dpar_ids), tuple(sts))
        dxs, dxp, dps, dsts = vjp((tuple(r[...].astype(F32) for r in dou), tuple(s[...] for s in dst)))
        dxs = list(dxs)
        for n_, pos in enumerate(add_ids):
            dxs[pos] = dxs[pos] + adr[n_][...].astype(F32)
        tails = {}
        for n_, k in enumerate(prev_ids):
            pos = drow_ids.index(k)
            if rows[k]['prev'] == 'block':
                dxs[pos] = dxs[pos] + car[n_][...]
            else:
                tails[pos] = car[n_][...]
            car[n_][...] = dxp[n_]
        for pos, v in enumerate(dxs):
            outs_ = [dxr[pos]] + ([dxb[pos]] if dx_bf else [])
            if pos in tails:
                v = jnp.concatenate([v[..., :R - HALO, :], v[..., R - HALO:, :] + tails[pos]], axis=-2)
            for r in outs_:
                r[...] = v.astype(r.dtype)
        for r, v in zip(dpr, dps):
            r[...] += v
        for s, v in zip(dst, dsts):
            s[...] = v
        if n_ex:
            @pl.when((pl.program_id(0) == ncol - 1) & (j == nb - 1))
            def _():
                _exchange_wait(exg, exr, *sems)

    def rev(j):
        return nb - 1 - j

    def dspec(k):
        return _bspec(rows[k], R, rev)

    in_specs = [_bspec(r, R, rev) for r in rows]
    in_specs += [_pspec(rows[k], R, rev) for k in prev_ids]
    in_specs += [pl.BlockSpec(p['bs'], functools.partial(lambda jc, j, idx: idx(jc), idx=p['idx'])) for p in params]
    in_specs += [pl.BlockSpec((1,) + tuple(s), lambda jc, j, nd=len(s): (nb - 1 - j,) + (0,) * nd) for s in state_shapes]
    in_specs += [_bspec(d, R, rev) for d in douts]
    in_specs += [dspec(drow_ids[pos]) for pos in add_ids]
    in_specs += [pl.BlockSpec(memory_space=pl.ANY) for _ in alias_ids + ex]
    out_specs = [dspec(k) for k in drow_ids]
    out_shape = [jax.ShapeDtypeStruct(tuple(rows[k]['a'].shape[:-1]) + (rows[k]['dn'],), dx_dt) for k in drow_ids]
    for k in dpar_ids:
        p = params[k]
        out_specs.append(pl.BlockSpec(p['bs'], functools.partial(lambda jc, j, idx: idx(jc), idx=p['idx'])))
        out_shape.append(jax.ShapeDtypeStruct(p['a'].shape, F32))
    if dx_bf:
        out_specs += [dspec(k) for k in drow_ids]
        out_shape += [jax.ShapeDtypeStruct(tuple(rows[k]['a'].shape[:-1]) + (rows[k]['dn'],), BF) for k in drow_ids]
    out_specs += [pl.BlockSpec(memory_space=pl.ANY) for _ in ex]
    out_shape += [jax.ShapeDtypeStruct(g.shape, g.dtype) for g in ex]
    scratch = [pltpu.VMEM(tuple(s), F32) for s in state_shapes]
    scratch += [pltpu.VMEM(tuple(d for d in rows[k]['lb'] if d is not None) + (_prev_rows(rows[k], R), rows[k]['w']), F32)
                for k in prev_ids]
    if n_ex:
        scratch += _exchange_sems(n_ex)
    args = ([r['a'] for r in rows] + [rows[k]['a'] for k in prev_ids] + [p['a'] for p in params] + list(saved)
            + [d['a'] for d in douts] + [dx_add[pos] for pos in add_ids] + [dx_alias[pos] for pos in alias_ids])
    n_in = len(args)
    aliases = {n_in - n_al + n_: pos for n_, pos in enumerate(alias_ids)}
    for pos in alias_ids:
        assert dx_alias[pos].shape == out_shape[pos].shape and dx_alias[pos].dtype == out_shape[pos].dtype
    cp = pltpu.CompilerParams(dimension_semantics=("arbitrary", "arbitrary"), vmem_limit_bytes=VMEM_LIMIT,
                              has_side_effects=bool(n_ex))
    res = pl.pallas_call(
        body, name=name, grid=(ncol, nb), in_specs=in_specs, out_specs=out_specs, out_shape=out_shape,
        scratch_shapes=scratch, input_output_aliases=aliases, compiler_params=cp)(*args, *ex)
    lists = [list(res[:n_dx]), list(res[n_dx:n_dx + n_dp])]
    o = n_dx + n_dp
    if dx_bf:
        lists.append(list(res[o:o + n_dx]))
        o += n_dx
    if n_ex:
        lists.append(list(res[o:o + n_ex]))
    return tuple(lists)


def _tile(n, cands):
    for c in cands:
        if n % c == 0:
            return c
    return n


def _mm_call(name, grid, a, a_spec, b, b_spec, contract, out_shape, out_spec, acc_shape, add=None, add_spec=None,
             exchange=None, gather=None):
    nk = grid[2]
    ca, cb = contract
    has_add = add is not None
    ex = list(exchange or []) + list(gather or [])
    ex_shapes = [g.shape for g in exchange or []] + [(N_DEV,) + tuple(g.shape) for g in gather or []]
    n_ex = len(ex)
    n_in = 2 + has_add

    def body(*refs):
        a_ref, b_ref = refs[0], refs[1]
        add_ref = refs[2] if has_add else None
        o_ref = refs[n_in + n_ex]
        scr = refs[n_in + 2 * n_ex + 1:]
        step = [pl.program_id(d) for d in range(3)]
        if n_ex:
            g_refs, r_refs, sems = refs[n_in:n_in + n_ex], refs[n_in + n_ex + 1:n_in + 2 * n_ex + 1], scr[-3:]

            @pl.when((step[0] == 0) & (step[1] == 0) & (step[2] == 0))
            def _():
                _exchange_start(g_refs, r_refs, *sems)

        part = _dot(a_ref[...], b_ref[...], ca, cb)

        def finish(r):
            if has_add:
                r = r + add_ref[...].astype(F32)
            o_ref[...] = r.astype(o_ref.dtype)

        if nk == 1:
            finish(part)
        else:
            acc = scr[0]

            @pl.when(step[2] == 0)
            def _():
                acc[...] = part

            @pl.when(step[2] > 0)
            def _():
                acc[...] += part

            @pl.when(step[2] == nk - 1)
            def _():
                finish(acc[...])

        if n_ex:
            @pl.when((step[0] == grid[0] - 1) & (step[1] == grid[1] - 1) & (step[2] == grid[2] - 1))
            def _():
                _exchange_wait(g_refs, r_refs, *sems)

    in_specs, args = [a_spec, b_spec], [a, b]
    if has_add:
        in_specs.append(add_spec)
        args.append(add)
    any_spec = pl.BlockSpec(memory_space=pl.ANY)
    scratch = [] if nk == 1 else [pltpu.VMEM(acc_shape, F32)]
    if n_ex:
        scratch += _exchange_sems(n_ex)
    res = pl.pallas_call(
        body, name=name, grid=grid, in_specs=in_specs + [any_spec] * n_ex, out_specs=[out_spec] + [any_spec] * n_ex,
        out_shape=[out_shape] + [jax.ShapeDtypeStruct(s, g.dtype) for s, g in zip(ex_shapes, ex)], scratch_shapes=scratch,
        compiler_params=pltpu.CompilerParams(
            dimension_semantics=("arbitrary",) * 3 if n_ex else ("parallel", "parallel", "arbitrary"),
            vmem_limit_bytes=VMEM_LIMIT, has_side_effects=bool(n_ex)))(*args, *ex)
    return (res[0], list(res[1:])) if n_ex else res[0]


def _matmul(a, b, mode="nn", add=None, out_dtype=F32, name="matmul", **pushed):
    if mode == "nn":
        (M, K), N = a.shape, b.shape[1]
    elif mode == "nt":
        (M, K), N = a.shape, b.shape[0]
    else:
        (K, M), N = a.shape, b.shape[1]
    if mode == "tn" and 1024 < N <= 5120:
        tm, tn = _tile(M, (512, 256, 128, 64, 32, 16, 8)), N
        tk = _tile(K, (2048 if tn <= 3072 else 1024, 1024, 512, 256, 128))
    else:
        tk = K if K <= 5120 else _tile(K, (2048, 1024, 512, 256, 128))
        tm = _tile(M, ((2048,) if tk <= 1024 and mode != "tn" else ()) + (1024, 512, 256, 128, 64, 32, 16, 8))
        tn = _tile(N, (512, 256, 128))
    if mode == "tn":
        a_spec = pl.BlockSpec((tk, tm), lambda i, j, k: (k, i))
    else:
        a_spec = pl.BlockSpec((tm, tk), lambda i, j, k: (i, k))
    if mode == "nt":
        b_spec = pl.BlockSpec((tn, tk), lambda i, j, k: (j, k))
    else:
        b_spec = pl.BlockSpec((tk, tn), lambda i, j, k: (k, j))
    blk = pl.BlockSpec((tm, tn), lambda i, j, k: (i, j))
    return _mm_call(name, (M // tm, N // tn, K // tk), a, a_spec, b, b_spec,
                    {"nn": (1, 0), "nt": (1, 1), "tn": (0, 0)}[mode], jax.ShapeDtypeStruct((M, N), out_dtype), blk,
                    (tm, tn), add, blk, **pushed)


FF_SH = 2 * D_FF // N_DEV


def _ffn_up(h2, wup, **pushed):
    S, D = h2.shape
    tm = _tile(S, (2048, 1024, 512, 256, 128))
    return _mm_call("matmul_up", (S // tm, N_DEV, 1), h2, pl.BlockSpec((tm, D), lambda m, j, k: (m, 0)),
                    wup, pl.BlockSpec((None, D, FF_SH), lambda m, j, k: (j, 0, 0)), (1, 0),
                    jax.ShapeDtypeStruct((2, N_DEV // 2, S, FF_SH), F32),
                    pl.BlockSpec((None, None, tm, FF_SH), lambda m, j, k: (j // 4, j % 4, m, 0)), (tm, FF_SH), **pushed)


def _ffn_down(act, wd, x1, **pushed):
    _, S, _ = act.shape
    D = wd.shape[1]
    tm, tn = _tile(S, (1024, 512, 256, 128)), _tile(D, (1024, 512, 256, 128))
    blk = pl.BlockSpec((tm, tn), lambda m, n, k: (m, n))
    return _mm_call("matmul_down", (S // tm, D // tn, N_DEV // 2), act,
                    pl.BlockSpec((None, tm, FF_SH), lambda m, n, k: (k, m, 0)), wd,
                    pl.BlockSpec((FF_SH, tn), lambda m, n, k: (k, n)), (1, 0), jax.ShapeDtypeStruct((S, D), F32), blk,
                    (tm, tn), x1, blk, **pushed)


def _ffn_dact(dxb, wd):
    S, D = dxb.shape
    tm = _tile(S, (2048, 1024, 512, 256, 128))
    return _mm_call("matmul_dact", (S // tm, N_DEV // 2, 1), dxb, pl.BlockSpec((tm, D), lambda m, j, k: (m, 0)), wd,
                    pl.BlockSpec((FF_SH, D), lambda m, j, k: (j, 0)), (1, 1),
                    jax.ShapeDtypeStruct((N_DEV // 2, S, FF_SH), BF),
                    pl.BlockSpec((None, tm, FF_SH), lambda m, j, k: (j, m, 0)), (tm, FF_SH))


def _ffn_dw_down(act, dxb):
    _, S, _ = act.shape
    D = dxb.shape[1]
    tk, tn = _tile(S, (2048, 1024, 512, 256, 128)), _tile(D, (512, 256, 128))
    return _mm_call("matmul_dw_down", (N_DEV // 2, D // tn, S // tk), act,
                    pl.BlockSpec((None, tk, FF_SH), lambda j, n, k: (j, k, 0)), dxb,
                    pl.BlockSpec((tk, tn), lambda j, n, k: (k, n)), (0, 0), jax.ShapeDtypeStruct((D_FF, D), BF),
                    pl.BlockSpec((FF_SH, tn), lambda j, n, k: (j, n)), (FF_SH, tn))


def _ffn_dw_up(h2, du, exchange=None):
    S, D = h2.shape
    tk = _tile(S, (2048, 1024, 512, 256, 128))
    return _mm_call("matmul_dw_up", (N_DEV, 1, S // tk), h2, pl.BlockSpec((tk, D), lambda j, n, k: (k, 0)), du,
                    pl.BlockSpec((None, None, tk, FF_SH), lambda j, n, k: (j // 4, j % 4, k, 0)), (0, 0),
                    jax.ShapeDtypeStruct((N_DEV, D, FF_SH), BF),
                    pl.BlockSpec((None, D, FF_SH), lambda j, n, k: (j, 0, 0)), (D, FF_SH), exchange=exchange)


def _ffn_dh2(du, wup, exchange=None):
    S = du.shape[2]
    D = wup.shape[1]
    tm = _tile(S, (1024, 512, 256, 128))
    return _mm_call("matmul_dh2", (S // tm, 1, N_DEV), du,
                    pl.BlockSpec((None, None, tm, FF_SH), lambda m, n, k: (k // 4, k % 4, m, 0)), wup,
                    pl.BlockSpec((None, D, FF_SH), lambda m, n, k: (k, 0, 0)), (1, 1),
                    jax.ShapeDtypeStruct((S, D), F32), pl.BlockSpec((tm, D), lambda m, n, k: (m, 0)), (tm, D),
                    exchange=exchange)


def f_rmsnorm(first, xp, xs, ps, sts):
    return (_rms(xs[0], ps[0]),), ()


def _same_block(shape, rows_per, cols_per):
    return (lax.broadcasted_iota(jnp.int32, shape, 0) // rows_per) == (lax.broadcasted_iota(jnp.int32, shape, 1) // cols_per)


@jax.custom_vjp
def _head_mean(x):
    n = x.shape[1]
    return _dot_hi(x, jnp.where(_same_block((n, n), XA_HD, XA_HD), 1.0 / XA_HD, 0.0), 1, 0)


_head_mean.defvjp(lambda x: (_head_mean(x), None), lambda r, g: (_head_mean(g),))


def f_xattn(first, xp, xs, ps, sts):
    (xq,), (kv, qn, kn) = xs, ps
    k, v = _split(kv, [D_XA, D_XA])
    m_rows = kv.shape[0]
    q = xq * lax.rsqrt(_head_mean(xq * xq) + EPS) * jnp.concatenate([qn] * XA_HEADS, axis=1)
    k = k * lax.rsqrt(_head_mean(k * k) + EPS) * jnp.concatenate([kn] * XA_HEADS, axis=1)
    kt = k.T
    kbd = jnp.where(_same_block((D_XA, XA_HEADS * m_rows), XA_HD, m_rows), jnp.concatenate([kt] * XA_HEADS, axis=1), 0.0)
    s = mm_nn(q, kbd) * (XA_HD ** -0.5)
    ps_ = []
    for sh in _split(s, [m_rows] * XA_HEADS):
        mx = lax.stop_gradient(jnp.max(sh, axis=-1, keepdims=True))
        p = jnp.exp(sh - mx)
        ps_.append(p / jnp.sum(p, axis=-1, keepdims=True))
    vbd = jnp.where(_same_block((XA_HEADS * m_rows, D_XA), m_rows, XA_HD), jnp.concatenate([v] * XA_HEADS, axis=0), 0.0)
    return (mm_nn(jnp.concatenate(ps_, axis=1), vbd),), ()


def _conv(xp, x, w, b, first, taps):
    xp = jnp.where(first, 0.0, xp)
    y = b + w[taps - 1:taps] * x
    for d in range(1, taps):
        y = y + w[taps - 1 - d:taps - d] * _shift(xp, x, d)
    return y


def _unstack2(x):
    @jax.custom_vjp
    def us(x):
        return x[0], x[1]

    us.defvjp(lambda x: (us(x), None), lambda r, g: (jnp.stack(g),))
    return us(x)


def f_ffn_act(first, xp, xs, ps, sts):
    (up,), (u,), (wg, wv, bg, bv) = xp, xs, ps
    (ugp, uvp), (ug, uv) = _unstack2(up), _unstack2(u)
    gate = _conv(ugp, ug, wg, bg, first, FFN_CONV)
    val = _conv(uvp, uv, wv, bv, first, FFN_CONV)
    return (jax.nn.silu(gate) * val,), ()


def _gla_chunk(q, k, v, la, sts, dk, dv):
    c, nh = q.shape[0], len(sts)
    b = cumsum_rows(la)
    b_last = _row(b, c - 1)
    b_ref = _row(b, c // 2 - 1)
    qe, ke = _split(q * jnp.exp(b - b_ref), [dk] * nh), _split(k * jnp.exp(b_ref - b), [dk] * nh)
    qi, kl = _split(q * jnp.exp(b), [dk] * nh), _split(k * jnp.exp(b_last - b), [dk] * nh)
    dec, vs = _split(jnp.exp(b_last), [dk] * nh), _split(v, [dv] * nh)
    tril = _tril(c)
    outs, new = [], []
    for h in range(nh):
        att = jnp.where(tril, mm_nt(qe[h], ke[h]), 0.0)
        outs.append(mm_nn(att, vs[h]) + mm_nt(qi[h], sts[h]))
        new.append(sts[h] * dec[h] + mm_tn(vs[h], kl[h]))
    return outs, tuple(new)


def _split_rows(x, n):
    c = x.shape[0] // n

    @jax.custom_vjp
    def sp(x):
        return tuple(x[i * c:(i + 1) * c] for i in range(n))

    sp.defvjp(lambda x: (sp(x), None), lambda r, g: (jnp.concatenate(list(g), axis=0),))
    return sp(x)


def _gla_scan(q, k, v, la, sts, dk, dv):
    n = q.shape[0] // CHUNK
    per_chunk = []
    for qc, kc, vc, lc in zip(*(_split_rows(t, n) for t in (q, k, v, la))):
        o, sts = _gla_chunk(qc, kc, vc, lc, sts, dk, dv)
        per_chunk.append(o)
    return [jnp.concatenate([o[h] for o in per_chunk], axis=0) for h in range(len(sts))], sts


def _a_cols(ntot):
    used = D_XA + 2 * GLA_HEADS * GLA_DK + D_MIX + GLA_RANK + D_MIX
    return [D_XA, GLA_HEADS * GLA_DK, GLA_HEADS * GLA_DK, D_MIX, GLA_RANK, D_MIX] + ([ntot - used] if ntot > used else [])


def f_gla(first, xp, xs, ps, sts):
    (p,), (wg2, bg, on) = xs, ps
    parts = _split(p, _a_cols(p.shape[1]))
    q, k, v, glr, og = parts[1:6]
    la = jax.nn.log_sigmoid(mm_nn(glr, wg2) + bg) / GLA_GATE_NORM
    outs, new = _gla_scan(q * (GLA_DK ** -0.5), k, v, la, tuple(sts), GLA_DK, GLA_DV)
    return (jnp.concatenate([_rms(o, on) for o in outs], axis=1) * jax.nn.silu(og),), new


def f_hgrn(first, xp, xs, ps, sts):
    (p,), (lbp, on) = xs, ps
    _, q, fgate, iv, og = _split(p, [D_XA, D_MIX, D_MIX, D_MIX, D_MIX])
    e = jnp.exp(lbp - jnp.max(lbp, axis=0, keepdims=True))
    row = lax.broadcasted_iota(jnp.int32, e.shape, 0)
    lb = jnp.sum(jnp.where(row >= 1, e, 0.0), axis=0, keepdims=True) / jnp.sum(e, axis=0, keepdims=True)
    fg = lb + (1.0 - lb) * jax.nn.sigmoid(fgate)
    outs, new = _gla_scan(jax.nn.silu(q), 1.0 - fg, iv, jnp.log(fg), tuple(sts), HGRN_DK, HGRN_DV)
    return (jnp.concatenate([_rms(o, on) for o in outs], axis=1) * jax.nn.sigmoid(og),), new


def _c_cols(ntot):
    gn = SSM_GROUPS * SSM_STATE
    used = D_XA + D_MIX + D_MIX + 2 * gn + SSM_HEADS
    return [D_XA, D_MIX, D_MIX + 2 * gn, SSM_HEADS] + ([ntot - used] if ntot > used else [])


def f_ssd(first, xp, xs, ps, sts):
    (pp,), (p,), (cw, cb, dtb, alog, dsk, ng) = xp, xs, ps
    gn = SSM_GROUPS * SSM_STATE
    _, z, xbc, dtr = _split(p, _c_cols(p.shape[1]))[:4]
    xbc_p = _split(pp, _c_cols(p.shape[1]))[2]
    xbc = jax.nn.silu(_conv(xbc_p, xbc, cw, cb, first, SSM_CONV))
    xs_, bm, cm = _split(xbc, [D_MIX, gn, gn])
    dt = jax.nn.softplus(dtr + dtb)
    n = p.shape[0] // CHUNK
    ys, sts = [], tuple(sts)
    for xc, bc, cc, dc in zip(*(_split_rows(t, n) for t in (xs_, bm, cm, dt))):
        y, sts = _ssd_chunk(xc, bc, cc, dc, alog, dsk, sts)
        ys.append(y)
    y = jnp.concatenate(ys, axis=0) * jax.nn.silu(z)
    gw = D_MIX // SSM_GROUPS
    yg = _split(y, [gw] * SSM_GROUPS)
    ngs = _split(ng, [gw] * SSM_GROUPS)
    y = jnp.concatenate([_rms(yg[g], ngs[g]) for g in range(SSM_GROUPS)], axis=1)
    return (y,), sts


def _ssd_chunk(xs_, bm, cm, dt, alog, dsk, sts):
    c = xs_.shape[0]
    hg = SSM_HEADS // SSM_GROUPS
    a = dt * (-jnp.exp(alog))
    acs = cumsum_rows(a)
    acs_t = cumsum_rows_t(a)
    acs_last = _row(acs, c - 1)
    dt_h = _split(dt, [1] * SSM_HEADS)
    acs_h = _split(acs, [1] * SSM_HEADS)
    al_h = _split(acs_last, [1] * SSM_HEADS)
    d_h = _split(dsk, [1] * SSM_HEADS)
    x2s = _split(xs_, [2 * SSM_HD] * (SSM_HEADS // 2))
    bms = _split(bm, [SSM_STATE] * SSM_GROUPS)
    cms = _split(cm, [SSM_STATE] * SSM_GROUPS)
    tril = _tril(c)
    cbs = [mm_nt(cms[g], bms[g]) for g in range(SSM_GROUPS)]
    ys, new = [], []
    for j in range(SSM_HEADS // 2):
        g = (2 * j) // hg
        h0, h1 = 2 * j, 2 * j + 1
        xdt = x2s[j] * _lane_pair(dt_h[h0], dt_h[h1])
        acs2 = _lane_pair(acs_h[h0], acs_h[h1])
        al2 = _lane_pair(al_h[h0], al_h[h1])
        yd = []
        for h in (h0, h1):
            seg = acs_h[h] - _row(acs_t, h)
            lm = jnp.exp(jnp.where(tril, seg, NEG))
            yd.append(mm_nn(cbs[g] * lm, xdt))
        lane = lax.broadcasted_iota(jnp.int32, xdt.shape, 1)
        y_diag = jnp.where(lane < SSM_HD, yd[0], yd[1])
        y_off = mm_nn(cms[g], sts[j]) * jnp.exp(acs2)
        x_end = xdt * jnp.exp(al2 - acs2)
        new.append(sts[j] * jnp.exp(al2) + mm_tn(bms[g], x_end))
        ys.append(y_diag + y_off + _lane_pair(d_h[h0], d_h[h1]) * x2s[j])
    return jnp.concatenate(ys, axis=1), tuple(new)


def f_dil_prep(first, xp, xs, ps, sts):
    (p, pos), (qn, kn, invf, sign) = xs, ps
    nh = len(DIL_GROUPS) * DIL_HEADS
    _, q, k, v = _split(p, [D_XA] + [nh * DIL_HD] * 3)
    ang = pos * invf
    cos, sin = jnp.cos(ang), jnp.sin(ang) * sign

    def rope(t, g):
        hs = _split(t, [DIL_HD] * nh)
        out = []
        for h in hs:
            n = _rms(h, g)
            out.append(n * cos + _roll_half(n) * sin)
        return [jnp.concatenate(out[i:i + DIL_HEADS], axis=1) for i in range(0, nh, DIL_HEADS)]

    return tuple(rope(q, qn) + rope(k, kn) + list(_split(v, [D_DIL] * len(DIL_GROUPS)))), ()


def f_dil_attn(first, xp, xs, ps, sts):
    (kp, vp), (q, k, v) = xp, xs
    Q = DIL_BLOCK
    qs, ks, vs = (_split(t, [DIL_HD] * DIL_HEADS) for t in (q, k, v))
    kps, vps = (_split(t, [DIL_HD] * DIL_HEADS) for t in (kp, vp))
    i = lax.broadcasted_iota(jnp.int32, (Q, 2 * Q), 0)
    j = lax.broadcasted_iota(jnp.int32, (Q, 2 * Q), 1)
    dist = Q + i - j
    mask = (dist >= 0) & (dist <= Q) & (jnp.logical_not(first) | (j >= Q))
    outs, lses = [], []
    for h in range(DIL_HEADS):
        k2 = jnp.concatenate([kps[h], ks[h]], axis=0)
        v2 = jnp.concatenate([vps[h], vs[h]], axis=0)
        s = jnp.where(mask, mm_nt(qs[h], k2) * (DIL_HD ** -0.5), NEG)
        m = lax.stop_gradient(jnp.max(s, axis=-1, keepdims=True))
        p = jnp.exp(s - m)
        l = jnp.sum(p, axis=-1, keepdims=True)
        outs.append(mm_nn(p / l, v2))
        lses.append(jnp.broadcast_to(m + jnp.log(l), (Q, DIL_HD)))
    return (jnp.concatenate(outs, axis=1), jnp.concatenate(lses, axis=1)), ()


def f_dil_merge(first, xp, xs, ps, sts):
    o0, o1, o2, l0, l1, l2 = xs
    m = jnp.maximum(jnp.maximum(l0, l1), l2)
    e0, e1, e2 = jnp.exp(l0 - m), jnp.exp(l1 - m), jnp.exp(l2 - m)
    den = e0 + e1 + e2
    return ((e0 * o0 + e1 * o1 + e2 * o2) / den,), ()


def _loss_head(y, target):
    S, D = y.shape
    R = _tile(S, (512, 256, 128, 64, 32, 16, 8))

    def body(y_ref, t_ref, dy_ref, dyb_ref, l_ref):
        e = y_ref[...] - t_ref[...]
        dy_ref[...] = e * (1.0 / D)
        dyb_ref[...] = (e * (1.0 / D)).astype(BF)

        @pl.when(pl.program_id(0) == 0)
        def _():
            l_ref[...] = jnp.zeros_like(l_ref)

        l_ref[...] += jnp.broadcast_to(0.5 * jnp.sum(jnp.mean(e * e, axis=-1, keepdims=True), axis=0, keepdims=True),
                                       l_ref.shape)

    blk = pl.BlockSpec((R, D), lambda i: (i, 0))
    dy, dyb, l = pl.pallas_call(
        body, name="loss_head", grid=(S // R,), in_specs=[blk, blk],
        out_specs=[blk, blk, pl.BlockSpec((8, 128), lambda i: (0, 0))],
        out_shape=[jax.ShapeDtypeStruct((S, D), F32), jax.ShapeDtypeStruct((S, D), BF),
                   jax.ShapeDtypeStruct((8, 128), F32)],
        compiler_params=pltpu.CompilerParams(dimension_semantics=("arbitrary",)))(y, target)
    return dy, dyb, l[0, 0]


def _adamw(parts, w, m, v, name):
    _, n, width = parts.shape
    tr = _tile(n, [t for t in (512, 256, 128, 64, 32, 16, 8) if t * width <= ADAM_BLOCK])

    def body(p_ref, w_ref, m_ref, v_ref, g_ref, d_ref, nm_ref, nv_ref):
        g = p_ref[0].astype(F32)
        for s in range(1, N_DEV):
            g = g + p_ref[s].astype(F32)
        nm = ADAM_B1 * m_ref[...] + (1.0 - ADAM_B1) * g
        nv = ADAM_B2 * v_ref[...] + (1.0 - ADAM_B2) * (g * g)
        m_hat = nm / (1.0 - ADAM_B1 ** ADAM_STEP)
        v_hat = nv / (1.0 - ADAM_B2 ** ADAM_STEP)
        g_ref[...] = g
        d_ref[...] = -ADAM_LR * (m_hat / (jnp.sqrt(v_hat) + ADAM_EPS) + ADAM_WD * w_ref[...])
        nm_ref[...] = nm
        nv_ref[...] = nv

    blk = pl.BlockSpec((tr, width), lambda i: (i, 0))
    return pl.pallas_call(
        body, name=name, grid=(n // tr,),
        in_specs=[pl.BlockSpec((N_DEV, tr, width), lambda i: (0, i, 0)), blk, blk, blk],
        out_specs=[blk] * 4, out_shape=[jax.ShapeDtypeStruct((n, width), F32)] * 4,
        compiler_params=pltpu.CompilerParams(dimension_semantics=("arbitrary",), vmem_limit_bytes=VMEM_LIMIT))(
            parts, w, m, v)


def _peer(k):
    x, y, c = lax.axis_index("x"), lax.axis_index("y"), lax.axis_index("c")
    px = 1 - x if k & 4 else x
    py = 1 - y if k & 2 else y
    pc = 1 - c if k & 1 else c
    return (px, py, pc), 4 * px + 2 * py + pc


def _my_id():
    return 4 * lax.axis_index("x") + 2 * lax.axis_index("y") + lax.axis_index("c")


def _all_gather(x, name):
    def body(x_ref, out_ref, send, recv, loc):
        me = _my_id()
        mine = pltpu.make_async_copy(x_ref, out_ref.at[me], loc)
        mine.start()
        cps = []
        for k in range(1, N_DEV):
            peer, _ = _peer(k)
            cp = pltpu.make_async_remote_copy(src_ref=x_ref, dst_ref=out_ref.at[me], send_sem=send.at[k - 1],
                                              recv_sem=recv.at[k - 1], device_id=peer,
                                              device_id_type=pl.DeviceIdType.MESH)
            cp.start()
            cps.append(cp)
        for k in range(1, N_DEV):
            peer, pid = _peer(k)
            pltpu.make_async_remote_copy(src_ref=x_ref, dst_ref=out_ref.at[pid], send_sem=send.at[k - 1],
                                         recv_sem=recv.at[k - 1], device_id=peer,
                                         device_id_type=pl.DeviceIdType.MESH).wait_recv()
        for cp in cps:
            cp.wait_send()
        mine.wait()

    return pl.pallas_call(
        body, name=name, out_shape=jax.ShapeDtypeStruct((N_DEV,) + x.shape, x.dtype),
        in_specs=[pl.BlockSpec(memory_space=pl.ANY)], out_specs=pl.BlockSpec(memory_space=pl.ANY),
        scratch_shapes=[pltpu.SemaphoreType.DMA((N_DEV - 1,)), pltpu.SemaphoreType.DMA((N_DEV - 1,)),
                        pltpu.SemaphoreType.DMA],
        compiler_params=pltpu.CompilerParams(has_side_effects=True))(x)


def _exchange_sems(n):
    return [pltpu.SemaphoreType.DMA((n * (N_DEV - 1),)), pltpu.SemaphoreType.DMA((n * (N_DEV - 1),)),
            pltpu.SemaphoreType.DMA((n,))]


def _exchange_copies(g_refs, out_refs, send, recv, loc, with_arrivals):
    me = _my_id()

    def mine(g, o, d):
        return g.at[d] if len(g.shape) == len(o.shape) else g

    local = [pltpu.make_async_copy(mine(g, o, me), o.at[me], loc.at[w]) for w, (g, o) in enumerate(zip(g_refs, out_refs))]
    pushes, arrivals = [], []
    for k in range(1, N_DEV):
        peer, pid = _peer(k)
        for w, (g, o) in enumerate(zip(g_refs, out_refs)):
            s = w * (N_DEV - 1) + k - 1
            ends = [(mine(g, o, pid), o.at[me], pushes)] + ([(mine(g, o, me), o.at[pid], arrivals)] if with_arrivals else [])
            for src, dst, into in ends:
                into.append(pltpu.make_async_remote_copy(src_ref=src, dst_ref=dst, send_sem=send.at[s],
                                                         recv_sem=recv.at[s], device_id=peer,
                                                         device_id_type=pl.DeviceIdType.MESH))
    return local, pushes, arrivals


def _exchange_start(g_refs, out_refs, send, recv, loc):
    local, pushes, _ = _exchange_copies(g_refs, out_refs, send, recv, loc, False)
    for cp in local + pushes:
        cp.start()


def _exchange_wait(g_refs, out_refs, send, recv, loc):
    local, pushes, arrivals = _exchange_copies(g_refs, out_refs, send, recv, loc, True)
    for cp in arrivals:
        cp.wait_recv()
    for cp in pushes:
        cp.wait_send()
    for cp in local:
        cp.wait()


def _exchange_many(gs, name):
    n = len(gs)

    def body(*refs):
        g_refs, out_refs, sems = refs[:n], refs[n:2 * n], refs[2 * n:]
        _exchange_start(g_refs, out_refs, *sems)
        _exchange_wait(g_refs, out_refs, *sems)

    return pl.pallas_call(
        body, name=name, out_shape=[jax.ShapeDtypeStruct(g.shape, g.dtype) for g in gs],
        in_specs=[pl.BlockSpec(memory_space=pl.ANY)] * n, out_specs=[pl.BlockSpec(memory_space=pl.ANY)] * n,
        scratch_shapes=_exchange_sems(n), compiler_params=pltpu.CompilerParams(has_side_effects=True))(*gs)


def _gather_many(xs, name):
    n = len(xs)

    def body(*refs):
        x_refs, out_refs, (send, recv, loc) = refs[:n], refs[n:2 * n], refs[2 * n:]
        x, y, c = lax.axis_index("x"), lax.axis_index("y"), lax.axis_index("c")
        me, sibling = (x, y, c), (x, y, 1 - c)
        chips = [(1 - x, y), (x, 1 - y), (1 - x, 1 - y)]

        def slot(p):
            return 4 * p[0] + 2 * p[1] + p[2]

        def copy(w, k, block, to, src=None):
            dst = out_refs[w].at[slot(block)]
            return pltpu.make_async_remote_copy(src_ref=dst if src is None else src, dst_ref=dst,
                                                send_sem=send.at[w * (N_DEV - 1) + k], recv_sem=recv.at[w * (N_DEV - 1) + k],
                                                device_id=to, device_id_type=pl.DeviceIdType.MESH)

        mine = [pltpu.make_async_copy(x_refs[w], out_refs[w].at[slot(me)], loc.at[w]) for w in range(n)]
        for cp in mine:
            cp.start()
        first = []
        for j, chip in enumerate(chips):
            first += [copy(w, 1 + j, me, (*chip, c), src=x_refs[w]) for w in range(n)]
        first += [copy(w, 0, me, sibling, src=x_refs[w]) for w in range(n)]
        for cp in first:
            cp.start()
        passed = []
        for j, chip in enumerate(chips):
            for w in range(n):
                copy(w, 1 + j, (*chip, c), me).wait_recv()
                cp = copy(w, 4 + j, (*chip, c), sibling)
                cp.start()
                passed.append(cp)
        for w in range(n):
            copy(w, 0, sibling, me).wait_recv()
            for j, chip in enumerate(chips):
                copy(w, 4 + j, (*chip, 1 - c), me).wait_recv()
        for cp in first + passed:
            cp.wait_send()
        for cp in mine:
            cp.wait()

    return pl.pallas_call(
        body, name=name, out_shape=[jax.ShapeDtypeStruct((N_DEV,) + x.shape, x.dtype) for x in xs],
        in_specs=[pl.BlockSpec(memory_space=pl.ANY)] * n, out_specs=[pl.BlockSpec(memory_space=pl.ANY)] * n,
        scratch_shapes=[pltpu.SemaphoreType.DMA((n * (N_DEV - 1),)), pltpu.SemaphoreType.DMA((n * (N_DEV - 1),)),
                        pltpu.SemaphoreType.DMA((n,))],
        compiler_params=pltpu.CompilerParams(has_side_effects=True))(*xs)


def _cat_segs(G, ws, n_mix):
    segs = []
    for g in range(G):
        lo, hi = g * ws, (g + 1) * ws
        if lo < n_mix:
            segs.append((g, 0, min(hi, n_mix) - lo, D_XA + lo))
        if hi > n_mix:
            s = max(lo, n_mix)
            segs.append((g, s - lo, hi - s, s - n_mix))
    return segs


def _cat_cols(src, n_mix, ntot):
    G, R, ws = src.shape
    segs = _cat_segs(G, ws, n_mix)
    tr = _tile(R, (256, 128, 64, 32, 16, 8))

    def body(i_ref, o_ref):
        if ntot > G * ws:
            o_ref[...] = jnp.zeros_like(o_ref)
        for g, s, n, d in segs:
            o_ref[:, d:d + n] = i_ref[g][:, s:s + n]

    return pl.pallas_call(
        body, name="cat_cols", grid=(R // tr,), in_specs=[pl.BlockSpec((G, tr, ws), lambda i: (0, i, 0))],
        out_specs=pl.BlockSpec((tr, ntot), lambda i: (i, 0)), out_shape=jax.ShapeDtypeStruct((R, ntot), src.dtype),
        compiler_params=pltpu.CompilerParams(dimension_semantics=("arbitrary",)))(src)


def _uncat_cols(dw, G, ws, n_mix):
    R, ntot = dw.shape
    segs = _cat_segs(G, ws, n_mix)
    tr = _tile(R, (256, 128, 64, 32, 16, 8))

    def body(i_ref, o_ref):
        v = i_ref[...]
        for g, s, n, d in segs:
            o_ref[g, :, s:s + n] = v[:, d:d + n]

    return pl.pallas_call(
        body, name="uncat_cols", grid=(R // tr,), in_specs=[pl.BlockSpec((tr, ntot), lambda i: (i, 0))],
        out_specs=pl.BlockSpec((G, tr, ws), lambda i: (0, i, 0)), out_shape=jax.ShapeDtypeStruct((G, R, ws), dw.dtype),
        compiler_params=pltpu.CompilerParams(dimension_semantics=("arbitrary",)))(dw)


PACK_W = 1024


def _granule(n):
    return (256 if n >= 256 * PACK_W else 8) * PACK_W


def _pack(arrs, dtype):
    flat = jnp.concatenate([a.reshape(-1).astype(dtype) for a in arrs])
    n = flat.shape[0]
    pad = (-n) % _granule(n)
    if pad:
        flat = jnp.concatenate([flat, jnp.zeros((pad,), dtype)])
    return flat.reshape(-1, PACK_W)


def _unpack(packed, shapes):
    flat = packed.reshape(-1)
    out, o = [], 0
    for s in shapes:
        n = math.prod(s)
        out.append(flat[o:o + n].reshape(s))
        o += n
    return out


def _pack_lead(arrs, dtype):
    flat = jnp.concatenate([a.reshape(N_DEV, -1).astype(dtype) for a in arrs], axis=1)
    n = flat.shape[1]
    pad = (-n) % _granule(n)
    if pad:
        flat = jnp.concatenate([flat, jnp.zeros((N_DEV, pad), dtype)], axis=1)
    return flat.reshape(N_DEV, -1, PACK_W)


def _to_full(stacked, axis):
    t = jnp.moveaxis(stacked, 0, axis)
    s = list(t.shape)
    return t.reshape(s[:axis] + [s[axis] * s[axis + 1]] + s[axis + 2:])


def _to_chunks(full, axis):
    s = list(full.shape)
    t = full.reshape(s[:axis] + [N_DEV, s[axis] // N_DEV] + s[axis + 1:])
    return jnp.moveaxis(t, axis, 0)


def _rows_of(S):
    return _tile(S, (512, 256, 128, 64))


def _norm_fwd(x, g, dt=BF):
    (h,), _ = _seq_fwd("rmsnorm_fwd", f_rmsnorm, _rows_of(x.shape[0]), [_row_spec(x)], [_par_spec(g)], [],
                       [_out_spec(x.shape[1], dt=dt)])
    return h


def _norm_bwd(x, g, dh, res=None):
    if res is None:
        (dx,), (dg,) = _seq_bwd("rmsnorm_bwd", f_rmsnorm, _rows_of(x.shape[0]), [_row_spec(x)], [_par_spec(g)], [],
                                [_row_spec(dh)], [])
        return dx, None, dg
    (dx,), (dg,), (dxb,) = _seq_bwd("rmsnorm_res_bwd", f_rmsnorm, _rows_of(x.shape[0]), [_row_spec(x)], [_par_spec(g)], [],
                                    [_row_spec(dh)], [], dx_add={0: res}, dx_bf=True)
    return dx, dxb, dg


def _mixer_specs(kind, S, p, w):
    if kind == 0:
        return (f_gla, min(S, MIX_ROWS), [_row_spec(p)],
                [_par_spec(w['a_w_gate2']), _par_spec(w['a_b_gate'].reshape(1, -1)), _par_spec(w['a_o_norm'].reshape(1, -1))],
                [(GLA_DV, GLA_DK)] * GLA_HEADS, D_MIX)
    if kind == 2:
        return (f_ssd, min(S, MIX_ROWS), [_row_spec(p, prev='halo')],
                [_par_spec(w['c_conv_w']), _par_spec(w['c_conv_b'].reshape(1, -1)), _par_spec(w['c_dt_bias'].reshape(1, -1)),
                 _par_spec(w['c_a_log'].reshape(1, -1)), _par_spec(w['c_d'].reshape(1, -1)),
                 _par_spec(w['c_norm'].reshape(1, -1))],
                [(SSM_STATE, 2 * SSM_HD)] * (SSM_HEADS // 2), D_MIX)
    return (f_hgrn, CHUNK, [_row_spec(p)],
            [_par_spec(w['d_lower_bounds']), _par_spec(w['d_o_norm'].reshape(1, -1))],
            [(HGRN_DV, HGRN_DK)] * HGRN_HEADS, D_MIX)


def _perm(t, r):
    if r == 1:
        return t
    S, n = t.shape
    return t.reshape(S // r, r, n).transpose(1, 0, 2).reshape(S, n)


def _unperm(t, r):
    if r == 1:
        return t
    S, n = t.shape
    return t.reshape(r, S // r, n).transpose(1, 0, 2).reshape(S, n)


def _rope_consts():
    half = DIL_HD // 2
    inv = ROPE_THETA ** (-jnp.arange(half, dtype=F32) / half)
    invf = jnp.concatenate([inv, inv]).reshape(1, DIL_HD)
    sign = jnp.concatenate([-jnp.ones((half,), F32), jnp.ones((half,), F32)]).reshape(1, DIL_HD)
    return invf, sign


def _dil_fwd(p, pos, w, ncat):
    S = p.shape[0]
    invf, sign = _rope_consts()
    prep_rows = [_row_spec(p), _row_spec(pos, diff=False)]
    prep_pars = [_par_spec(w['b_q_norm'].reshape(1, -1)), _par_spec(w['b_k_norm'].reshape(1, -1)),
                 _par_spec(invf, diff=False), _par_spec(sign, diff=False)]
    ng = len(DIL_GROUPS)
    qkv, _ = _seq_fwd("dil_prep_fwd", f_dil_prep, _tile(S, (256, 128)), prep_rows, prep_pars, [],
                      [_out_spec(D_DIL) for _ in range(3 * ng)])
    res = dict(perm=[], o=[], lse=[])
    for g, (window, r) in enumerate(DIL_GROUPS):
        qp, kp, vp = _perm(qkv[g], r), _perm(qkv[ng + g], r), _perm(qkv[2 * ng + g], r)
        rows = [_row_spec(qp), _row_spec(kp, prev='block'), _row_spec(vp, prev='block')]
        (o, lse), _ = _seq_fwd("dil_attn_fwd", f_dil_attn, DIL_BLOCK, rows, [], [], [_out_spec(D_DIL), _out_spec(D_DIL)],
                               period=S // r // DIL_BLOCK)
        res['perm'].append((qp, kp, vp))
        res['o'].append(_unperm(o, r))
        res['lse'].append(_unperm(lse, r))
    mrows = [_row_spec(t) for t in res['o'] + res['lse']]
    (cat,), _ = _seq_fwd("dil_merge_fwd", f_dil_merge, _rows_of(S), mrows, [], [], [_out_spec(ncat, w=D_DIL, dt=BF)])
    res['prep'] = (prep_rows, prep_pars)
    return cat, res


def _dil_bwd(dtok, res, p, exchange=None, exchange_local=None):
    S = p.shape[0]
    mrows = [_row_spec(t) for t in res['o'] + res['lse']]
    dm, _ = _seq_bwd("dil_merge_bwd", f_dil_merge, _rows_of(S), mrows, [], [], [dtok], [])
    dq, dk, dv = [], [], []
    for g, (window, r) in enumerate(DIL_GROUPS):
        qp, kp, vp = res['perm'][g]
        rows = [_row_spec(qp), _row_spec(kp, prev='block'), _row_spec(vp, prev='block')]
        douts = [_row_spec(_perm(dm[g], r)), _row_spec(_perm(dm[3 + g], r))]
        (a, b, c), _ = _seq_bwd("dil_attn_bwd", f_dil_attn, DIL_BLOCK, rows, [], [], douts, [],
                                period=S // r // DIL_BLOCK)
        dq.append(_unperm(a, r)); dk.append(_unperm(b, r)); dv.append(_unperm(c, r))
    prep_rows, prep_pars = res['prep']
    res = _seq_bwd("dil_prep_bwd", f_dil_prep, _tile(S, (256, 128)), prep_rows, prep_pars, [],
                   [_row_spec(t) for t in dq + dk + dv], [], dx_dt=BF, exchange=exchange)
    (dp,), (dqn, dkn) = res[0], res[1]
    return dp, dict(b_q_norm=dqn.reshape(-1), b_k_norm=dkn.reshape(-1)), (res[2] if exchange else exchange_local)


def _ffn_specs(u, cw, cb):
    half = N_DEV // 2
    rows = [_row_spec(u, prev='halo', lb=(2, None), li=lambda jc: (0, jc))]
    pars = [_par_spec(cw, bs=(None, FFN_CONV, FF_SH), idx=lambda jc: (jc, 0, 0)),
            _par_spec(cw, bs=(None, FFN_CONV, FF_SH), idx=lambda jc: (jc + half, 0, 0)),
            _par_spec(cb, bs=(None, 1, FF_SH), idx=lambda jc: (jc, 0, 0)),
            _par_spec(cb, bs=(None, 1, FF_SH), idx=lambda jc: (jc + half, 0, 0))]
    return half, rows, pars


N_MIX = {0: 2 * GLA_HEADS * GLA_DK + 2 * D_MIX + GLA_RANK, 1: 3 * len(DIL_GROUPS) * D_DIL,
         2: 2 * D_MIX + 2 * SSM_GROUPS * SSM_STATE + SSM_HEADS, 3: 2 * HGRN_HEADS * HGRN_DK + 2 * D_MIX}
W_IN = {0: 'a_w_in', 1: 'b_w_in', 2: 'c_w_in', 3: 'd_w_in'}
W_OUT = {0: 'a_w_out', 1: 'b_w_out', 2: 'c_w_out', 3: 'd_w_out'}


def _in_blocks(name, t):
    return t if SHARD_AXIS[name] == 1 else t.reshape(1, N_DEV * t.shape[1], t.shape[2])


LAYER_STACKED = ('ffn_w_up', 'ffn_conv_w', 'ffn_w_down', 'xa_w_kv')


SMALL_OF_KIND = {0: ['a_w_gate2'], 2: ['c_conv_w']}


def _layer_names(i):
    return list(LAYER_STACKED) + [W_IN[i % 4], W_OUT[i % 4]] + SMALL_OF_KIND.get(i % 4, [])


def _device_step(x, mem, pos, sh, rep, target, distributed=True):
    S, D = x.shape
    w = dict(rep)
    posf = pos.reshape(S, 1).astype(F32)
    n_mix, w_in_name, w_out_name = N_MIX, W_IN, W_OUT
    ntot = {k: -(-(n_mix[k] + D_XA) // 256) * 256 for k in n_mix}
    mem_g = w['mem_norm'].reshape(1, -1)
    mem_n = _norm_fwd(mem, mem_g)
    R = _rows_of(S)

    def mine(i):
        return {n: (sh[n][i] if n in LAYER_STACKED else sh[n]) for n in _layer_names(i)}

    if distributed:
        gl = dict(zip(_layer_names(0), _gather_many(list(mine(0).values()), "gather_weights")))
    else:
        gl = {n: (sh[n][:, 0] if n in LAYER_STACKED else sh[n]) for n in _layer_names(0)}

    saved = []
    for i in range(DEPTH):
        kind = i % 4
        L = dict(x0=x)
        for n in SMALL_OF_KIND.get(kind, []):
            w[n] = _to_full(gl[n], 1)
        in_blocks = _in_blocks(w_in_name[kind], gl[w_in_name[kind]])
        w_out = (_to_full(gl[w_out_name[kind]], 1) if SHARD_AXIS[w_out_name[kind]] == 1
                 else gl[w_out_name[kind]].reshape(-1, D))
        nxt, push = {}, [[], [], []]
        if i + 1 < DEPTH:
            if distributed:
                nxt = mine(i + 1)
                push = [[n for n in nxt if n not in ('ffn_w_up', w_in_name[(i + 1) % 4], w_out_name[(i + 1) % 4])],
                        ['ffn_w_up'], [w_in_name[(i + 1) % 4], w_out_name[(i + 1) % 4]]]
            else:
                nxt = {n: (sh[n][:, i + 1] if n in LAYER_STACKED else sh[n]) for n in _layer_names(i + 1)}
        got = dict(nxt) if not distributed else {}

        def hosted(call, names):
            if not names:
                return call()
            res, arrived = call(gather=[nxt[n] for n in names])
            got.update(zip(names, arrived))
            return res

        g1 = w['mix_norm'][i].reshape(1, -1)
        h = _norm_fwd(x, g1)
        wcat = _cat_cols(in_blocks, n_mix[kind], ntot[kind])
        p = hosted(functools.partial(_matmul, h, wcat, name="matmul_in"), push[0])
        ntok = D_DIL if kind == 1 else D_MIX
        if kind == 1:
            cat, L['dil'] = _dil_fwd(p, posf, w, ntok + D_XA)
        else:
            f, Rm, rows, pars, sshapes, _ = _mixer_specs(kind, S, p, w)
            (cat,), L['states'] = _seq_fwd("mixer%d_fwd" % kind, f, Rm, rows, pars, sshapes,
                                           [_out_spec(ntok + D_XA, w=ntok, dt=BF)], save_states=True)
        wkv = gl['xa_w_kv'].reshape(D, 2 * D_XA)
        kv = _matmul(mem_n, wkv, name="matmul_kv")
        xa_rows = [_row_spec(p, w=D_XA, dn=D_XA)]
        xa_pars = [_par_spec(kv), _par_spec(w['xa_q_norm'][i].reshape(1, -1)), _par_spec(w['xa_k_norm'][i].reshape(1, -1))]
        (cat,), _ = _seq_fwd("xattn_fwd", f_xattn, R, xa_rows, xa_pars, [],
                             [_out_spec(ntok + D_XA, w=D_XA, c=lambda jc: ntok // D_XA, dt=BF)], out_alias={0: cat})
        x1 = _matmul(cat, w_out, add=x, name="matmul_out")
        g2 = w['ffn_norm'][i].reshape(1, -1)
        h2 = _norm_fwd(x1, g2)
        wup = gl['ffn_w_up']
        u = hosted(functools.partial(_ffn_up, h2, wup), push[1])
        cw, cb = gl['ffn_conv_w'], w['ffn_conv_b'][i].reshape(N_DEV, 1, FF_SH)
        nt, frows, fpars = _ffn_specs(u, cw, cb)
        (act,), _ = _seq_fwd("ffn_act_fwd", f_ffn_act, R, frows, fpars, [],
                             [_out_spec(FF_SH, dt=BF, ls=(nt,), lb=(None,), li=lambda jc: (jc,))], ncol=nt)
        wd = gl['ffn_w_down'].reshape(D_FF, D)
        x = hosted(functools.partial(_ffn_down, act, wd, x1), push[2])
        L.update(h=h, p=p, wcat=wcat, kv=kv, wkv=wkv, cat=cat, x1=x1, h2=h2, u=u, act=act, wd=wd, wup=wup, cw=cw, g1=g1,
                 g2=g2, in_blocks=in_blocks, w_out=w_out, shapes={n: t.shape for n, t in gl.items()})
        saved.append(L)
        gl = got

    dx, dxb, loss = _loss_head(x, target)

    G = {}
    d_mem_n = None
    acc = {k: [None] * DEPTH for k in ('mix_norm', 'ffn_norm', 'ffn_conv_b', 'xa_q_norm', 'xa_k_norm')}
    parts = [{} for _ in range(DEPTH)]
    pending = {}
    half = N_DEV // 2

    def sent(call, blocks, layer):
        if not blocks:
            return call()
        if not distributed:
            parts[layer].update(blocks)
            return call()
        res, arrived = call(exchange=list(blocks.values()))
        parts[layer].update(zip(blocks, arrived))
        return res

    for i in reversed(range(DEPTH)):
        kind = i % 4
        L = saved[i]
        Gc = {}
        Gc['ffn_w_down'] = _ffn_dw_down(L['act'], dxb).reshape(N_DEV, D_FF // N_DEV, D)
        dact = _ffn_dact(dxb, L['wd'])
        cw, cb = L['cw'], w['ffn_conv_b'][i].reshape(N_DEV, 1, FF_SH)
        nt, frows, fpars = _ffn_specs(L['u'], cw, cb)
        (du,), (dwg, dwv, dbg, dbv) = _seq_bwd(
            "ffn_act_bwd", f_ffn_act, R, frows, fpars, [], [_row_spec(dact, lb=(None,), li=lambda jc: (jc,))], [],
            ncol=nt, dx_dt=BF)
        Gc['ffn_conv_w'] = jnp.concatenate([dwg[:half], dwv[half:]], axis=0)
        acc['ffn_conv_b'][i] = jnp.concatenate([dbg[:half], dbv[half:]], axis=0).reshape(-1)
        Gc['ffn_w_up'] = _ffn_dw_up(L['h2'], du)
        dh2 = sent(functools.partial(_ffn_dh2, du, L['wup']), pending, i + 1)
        dx1, dx1b, dg2 = _norm_bwd(L['x1'], L['g2'], dh2, res=dx)
        acc['ffn_norm'][i] = dg2.reshape(-1)
        G_out = _matmul(L['cat'], dx1b, mode="tn", out_dtype=BF, name="matmul_dw_out")
        dcat = _matmul(dx1b, L['w_out'], mode="nt", name="matmul_dcat")
        ntok = D_DIL if kind == 1 else D_MIX
        dtok = _row_spec(dcat, w=ntok)
        dxa = _row_spec(dcat, w=D_XA, c=lambda jc: ntok // D_XA)
        p = L['p']
        up = [Gc.pop('ffn_w_up')]
        if kind == 1:
            dp, gm, got_up = _dil_bwd(dtok, L['dil'], p, up if distributed else None, up)
            G.update(gm)
        else:
            f, Rm, rows, pars, sshapes, _ = _mixer_specs(kind, S, p, w)
            res = _seq_bwd("mixer%d_bwd" % kind, f, Rm, rows, pars, sshapes, [dtok], L['states'], dx_dt=BF,
                           exchange=up if distributed else None)
            (dp,), dps, got_up = res[0], res[1], (res[2] if distributed else up)
            if kind == 0:
                Gc['a_w_gate2'], G['a_b_gate'], G['a_o_norm'] = _to_chunks(dps[0], 1), dps[1].reshape(-1), dps[2].reshape(-1)
            elif kind == 2:
                Gc['c_conv_w'] = _to_chunks(dps[0], 1)
                for nme, v in zip(('c_conv_b', 'c_dt_bias', 'c_a_log', 'c_d', 'c_norm'), dps[1:]):
                    G[nme] = v.reshape(-1)
            else:
                G['d_lower_bounds'], G['d_o_norm'] = dps[0], dps[1].reshape(-1)
        xa_rows = [_row_spec(p, w=D_XA)]
        xa_pars = [_par_spec(L['kv']), _par_spec(w['xa_q_norm'][i].reshape(1, -1)), _par_spec(w['xa_k_norm'][i].reshape(1, -1))]
        (dp,), (dkv, dqn, dkn) = _seq_bwd("xattn_bwd", f_xattn, R, xa_rows, xa_pars, [], [dxa], [], dx_dt=BF,
                                          dx_alias={0: dp})
        acc['xa_q_norm'][i], acc['xa_k_norm'][i] = dqn.reshape(-1), dkn.reshape(-1)
        Gc['xa_w_kv'] = _matmul(mem_n, dkv, mode="tn", out_dtype=BF, name="matmul_dw_kv").reshape(
            N_DEV, D // N_DEV, 2 * D_XA)
        d_mem_n = _matmul(dkv, L['wkv'], mode="nt", add=d_mem_n, name="matmul_dmem" + ("" if d_mem_n is None else "_acc"))
        parts[i]['ffn_w_up'] = got_up[0]
        dwcat = _matmul(L['h'], dp, mode="tn", out_dtype=BF, name="matmul_dw_in")
        blocks = L['in_blocks']
        Gc[w_in_name[kind]] = _uncat_cols(dwcat, blocks.shape[0], blocks.shape[2], n_mix[kind]).reshape(
            L['shapes'][w_in_name[kind]])
        Gc[w_out_name[kind]] = (_to_chunks(G_out, 1) if SHARD_AXIS[w_out_name[kind]] == 1
                                else G_out.reshape(L['shapes'][w_out_name[kind]]))
        dh = sent(functools.partial(_matmul, dp, L['wcat'], mode="nt", name="matmul_dh"),
                  {n: Gc.pop(n) for n in ('ffn_w_down', 'ffn_conv_w')}, i)
        dx, dxb, dg1 = _norm_bwd(L['x0'], L['g1'], dh, res=dx1)
        acc['mix_norm'][i] = dg1.reshape(-1)
        pending = Gc

    if distributed:
        names = list(pending)
        parts[0].update(zip(names, _exchange_many([pending[n] for n in names], "exchange_grads")))
    else:
        parts[0].update(pending)
    _, _, dmg = _norm_bwd(mem, mem_g, d_mem_n)
    G['mem_norm'] = dmg.reshape(-1)
    for k, v in acc.items():
        G[k] = jnp.stack(v)
    got = {}
    for i in range(DEPTH):
        for n, t in parts[i].items():
            if n not in LAYER_STACKED:
                got[n] = t
    for n in LAYER_STACKED:
        got[n] = jnp.stack([parts[i][n] for i in range(DEPTH)], axis=1)
    return loss, dx, got, G


def kernel(x, mem, positions, mem_norm, mix_norm, xa_w_kv, xa_q_norm, xa_k_norm, ffn_norm, ffn_w_up, ffn_conv_w, ffn_conv_b, ffn_w_down, a_w_in, a_w_gate2, a_b_gate, a_o_norm, a_w_out, b_w_in, b_q_norm, b_k_norm, b_w_out, c_w_in, c_conv_w, c_conv_b, c_dt_bias, c_a_log, c_d, c_norm, c_w_out, d_w_in, d_lower_bounds, d_o_norm, d_w_out, loss_target, m_mem_norm, m_mix_norm, m_xa_w_kv, m_xa_q_norm, m_xa_k_norm, m_ffn_norm, m_ffn_w_up, m_ffn_conv_w, m_ffn_conv_b, m_ffn_w_down, m_a_w_in, m_a_w_gate2, m_a_b_gate, m_a_o_norm, m_a_w_out, m_b_w_in, m_b_q_norm, m_b_k_norm, m_b_w_out, m_c_w_in, m_c_conv_w, m_c_conv_b, m_c_dt_bias, m_c_a_log, m_c_d, m_c_norm, m_c_w_out, m_d_w_in, m_d_lower_bounds, m_d_o_norm, m_d_w_out, v_mem_norm, v_mix_norm, v_xa_w_kv, v_xa_q_norm, v_xa_k_norm, v_ffn_norm, v_ffn_w_up, v_ffn_conv_w, v_ffn_conv_b, v_ffn_w_down, v_a_w_in, v_a_w_gate2, v_a_b_gate, v_a_o_norm, v_a_w_out, v_b_w_in, v_b_q_norm, v_b_k_norm, v_b_w_out, v_c_w_in, v_c_conv_w, v_c_conv_b, v_c_dt_bias, v_c_a_log, v_c_d, v_c_norm, v_c_w_out, v_d_w_in, v_d_lower_bounds, v_d_o_norm, v_d_w_out):
    args = locals()
    w = {n: args[n] for n in WEIGHTS}
    m = {n: args['m_' + n] for n in WEIGHTS}
    v = {n: args['v_' + n] for n in WEIGHTS}

    big = [n for n in SHARDED if w[n].size >= 65536]
    small = [n for n in SHARDED if n not in big]
    sh = {n: (w[n].astype(BF) if n in big else w[n]) for n in SHARDED}
    loss, grad_x, parts, G = _device_step(x[0], mem[0], positions[0], sh, {n: w[n] for n in REPLICATED}, loss_target[0])
    loss = lax.psum(loss, ("x", "y", "c"))
    rep_parts = _all_gather(_pack([G[n] for n in REPLICATED], F32), "gather_replicated_grads")

    out = {}

    def put(names, res, shapes):
        for kind, r in zip(("grad", "delta", "new_m", "new_v"), res):
            for n, t in zip(names, _unpack(r, shapes)):
                out[kind + "_" + n] = t

    for n in big:
        shp = tuple(w[n].shape)
        two_d = (math.prod(shp[:-1]), shp[-1])
        res = _adamw(parts[n].reshape((N_DEV,) + two_d), w[n].reshape(two_d), m[n].reshape(two_d), v[n].reshape(two_d),
                     "adamw")
        for kind, r in zip(("grad", "delta", "new_m", "new_v"), res):
            out[kind + "_" + n] = r.reshape(shp)
    for names, prt, tag in ((small, _pack_lead([parts[n] for n in small], F32), "adamw_small"),
                            (REPLICATED, rep_parts, "adamw_replicated")):
        res = _adamw(prt, _pack([w[n] for n in names], F32), _pack([m[n] for n in names], F32),
                     _pack([v[n] for n in names], F32), tag)
        put(names, res, [tuple(w[n].shape) for n in names])
    return (loss, grad_x[None], *[out["grad_" + n] for n in WEIGHTS], *[out["delta_" + n] for n in WEIGHTS],
            *[out["new_m_" + n] for n in WEIGHTS], *[out["new_v_" + n] for n in WEIGHTS])
```

```python
import functools
import math

import jax
import jax.numpy as jnp
from jax import lax
from jax.experimental import pallas as pl
from jax.experimental.pallas import tpu as pltpu

F32 = jnp.float32
BF = jnp.bfloat16
_MM_DTYPE = BF

N_DEV = 8
EPS = 1e-6
ROPE_THETA = 10000.0
CHUNK = 64
MIX_ROWS = 256
D_MIX = 768
XA_HEADS, XA_HD, D_XA = 4, 64, 256
GLA_HEADS, GLA_DK, GLA_DV, GLA_RANK, GLA_GATE_NORM = 4, 96, 192, 16, 16.0
DIL_GROUPS = ((128, 1), (512, 4), (2048, 16))
DIL_HEADS, DIL_HD, DIL_BLOCK, D_DIL = 4, 128, 128, 512
SSM_HD, SSM_HEADS, SSM_GROUPS, SSM_STATE, SSM_CONV = 64, 12, 2, 128, 4
HGRN_HEADS, HGRN_DK, HGRN_DV = 6, 128, 128
D_FF = 2816
FFN_CONV = 3
DEPTH = 4
ADAM_LR, ADAM_B1, ADAM_B2, ADAM_EPS, ADAM_WD, ADAM_STEP = 0.001, 0.9, 0.999, 1e-08, 0.01, 10
NEG = -1e30
HALO = 8
VMEM_LIMIT = 56 << 20
ADAM_BLOCK = 1 << 18

WEIGHTS = ['mem_norm', 'mix_norm', 'xa_w_kv', 'xa_q_norm', 'xa_k_norm', 'ffn_norm', 'ffn_w_up', 'ffn_conv_w',
           'ffn_conv_b', 'ffn_w_down', 'a_w_in', 'a_w_gate2', 'a_b_gate', 'a_o_norm', 'a_w_out', 'b_w_in', 'b_q_norm',
           'b_k_norm', 'b_w_out', 'c_w_in', 'c_conv_w', 'c_conv_b', 'c_dt_bias', 'c_a_log', 'c_d', 'c_norm', 'c_w_out',
           'd_w_in', 'd_lower_bounds', 'd_o_norm', 'd_w_out']
SHARD_AXIS = {'xa_w_kv': 1, 'ffn_w_up': 2, 'ffn_conv_w': 2, 'ffn_w_down': 1, 'a_w_in': 1, 'a_w_gate2': 1, 'a_w_out': 0,
              'b_w_in': 1, 'b_w_out': 1, 'c_w_in': 0, 'c_conv_w': 1, 'c_w_out': 0, 'd_w_in': 1, 'd_w_out': 0}
SHARDED = [n for n in WEIGHTS if n in SHARD_AXIS]
REPLICATED = [n for n in WEIGHTS if n not in SHARD_AXIS]


def _dot(a, b, ca, cb):
    return lax.dot_general(a.astype(_MM_DTYPE), b.astype(_MM_DTYPE), (((ca,), (cb,)), ((), ())),
                           preferred_element_type=F32)


@jax.custom_vjp
def mm_nn(a, b):
    return _dot(a, b, 1, 0)


mm_nn.defvjp(lambda a, b: (_dot(a, b, 1, 0), (a, b)),
             lambda r, g: (_dot(g, r[1], 1, 1), _dot(r[0], g, 0, 0)))


@jax.custom_vjp
def mm_nt(a, b):
    return _dot(a, b, 1, 1)


mm_nt.defvjp(lambda a, b: (_dot(a, b, 1, 1), (a, b)),
             lambda r, g: (_dot(g, r[1], 1, 0), _dot(g, r[0], 0, 0)))


@jax.custom_vjp
def mm_tn(a, b):
    return _dot(a, b, 0, 0)


mm_tn.defvjp(lambda a, b: (_dot(a, b, 0, 0), (a, b)),
             lambda r, g: (_dot(r[1], g, 1, 1), _dot(r[0], g, 1, 0)))


def _dot_hi(a, b, ca, cb):
    return lax.dot_general(a, b, (((ca,), (cb,)), ((), ())), precision=lax.Precision.HIGHEST,
                           preferred_element_type=F32)


def _tril(c):
    return lax.broadcasted_iota(jnp.int32, (c, c), 0) >= lax.broadcasted_iota(jnp.int32, (c, c), 1)


@jax.custom_vjp
def cumsum_rows(x):
    return _dot_hi(_tril(x.shape[0]).astype(F32), x, 1, 0)


cumsum_rows.defvjp(lambda x: (cumsum_rows(x), None),
                   lambda r, g: (_dot_hi(_tril(g.shape[0]).astype(F32), g, 0, 0),))


@jax.custom_vjp
def cumsum_rows_t(x):
    return _dot_hi(x, _tril(x.shape[0]).astype(F32), 0, 1)


cumsum_rows_t.defvjp(lambda x: (cumsum_rows_t(x), None),
                     lambda r, g: (_dot_hi(_tril(g.shape[1]).astype(F32), g, 0, 1),))


def _split(x, sizes):
    sizes = tuple(int(s) for s in sizes)
    assert sum(sizes) == x.shape[-1], (sizes, x.shape)

    @jax.custom_vjp
    def sp(x):
        out, o = [], 0
        for s in sizes:
            out.append(x[:, o:o + s])
            o += s
        return tuple(out)

    sp.defvjp(lambda x: (sp(x), None), lambda r, g: (jnp.concatenate(list(g), axis=1),))
    return sp(x)


def _row(x, r):
    m = lax.broadcasted_iota(jnp.int32, x.shape, 0) == r
    return jnp.sum(jnp.where(m, x, 0.0), axis=0, keepdims=True)


@jax.custom_vjp
def _roll_half(x):
    return pltpu.roll(x, 64, 1)


_roll_half.defvjp(lambda x: (pltpu.roll(x, 64, 1), None), lambda r, g: (pltpu.roll(g, 64, 1),))


def _shift(xp, x, d):
    if d == 0:
        return x
    n, m = x.shape[0], xp.shape[0]
    assert d <= m == HALO

    @jax.custom_vjp
    def sh(xp, x):
        r = pltpu.roll(x, d, 0)
        row = lax.broadcasted_iota(jnp.int32, xp.shape, 0)
        head = jnp.where(row < d, pltpu.roll(xp, d, 0), r[:m])
        return jnp.concatenate([head, r[m:]], axis=0)

    def bwd(_, g):
        row = lax.broadcasted_iota(jnp.int32, g.shape, 0)
        rowp = lax.broadcasted_iota(jnp.int32, (m,) + g.shape[1:], 0)
        dxp = jnp.where(rowp >= m - d, pltpu.roll(g[:m], m - d, 0), 0.0)
        return dxp, jnp.where(row < n - d, pltpu.roll(g, n - d, 0), 0.0)

    sh.defvjp(lambda xp, x: (sh(xp, x), None), bwd)
    return sh(xp, x)


def _rms(x, g):
    return x * lax.rsqrt(jnp.mean(x * x, axis=-1, keepdims=True) + EPS) * g


def _lane_pair(a, b, width=128):
    shape = a.shape[:-1] + (width,)
    lane = lax.broadcasted_iota(jnp.int32, shape, len(shape) - 1)
    return jnp.where(lane < width // 2, a, b)


def _row_spec(a, w=None, c=None, prev=False, diff=True, dn=None, lb=(), li=None):
    return dict(a=a, w=a.shape[-1] if w is None else w, c=(lambda jc: 0) if c is None else c, prev=prev, diff=diff,
                dn=a.shape[-1] if dn is None else dn, lb=tuple(lb), li=(lambda jc: ()) if li is None else li)


def _par_spec(a, bs=None, idx=None, diff=True):
    nd = a.ndim
    return dict(a=a, bs=tuple(a.shape) if bs is None else tuple(bs),
                idx=(lambda jc: (0,) * nd) if idx is None else idx, diff=diff)


def _out_spec(n, w=None, c=None, dt=F32, ls=(), lb=(), li=None):
    return dict(n=n, w=n if w is None else w, c=(lambda jc: 0) if c is None else c, dt=dt, ls=tuple(ls), lb=tuple(lb),
                li=(lambda jc: ()) if li is None else li)


def _cparams():
    return pltpu.CompilerParams(dimension_semantics=("arbitrary", "arbitrary"), vmem_limit_bytes=VMEM_LIMIT)


def _bspec(s, R, rowfn):
    return pl.BlockSpec(s['lb'] + (R, s['w']),
                        functools.partial(lambda jc, i, s: tuple(s['li'](jc)) + (rowfn(i), s['c'](jc)), s=s))


def _prev_rows(s, R):
    return R if s['prev'] == 'block' else HALO


def _pspec(s, R, blockfn):
    pr = _prev_rows(s, R)
    return pl.BlockSpec(s['lb'] + (pr, s['w']), functools.partial(
        lambda jc, i, s: tuple(s['li'](jc)) + (jnp.maximum(blockfn(i) * (R // pr) - 1, 0), s['c'](jc)), s=s))


def _seq_fwd(name, f, R, rows, params, state_shapes, outs, *, ncol=1, period=None, save_states=False, out_alias=None,
             gather=None):
    nrows = rows[0]['a'].shape[-2]
    nb = nrows // R
    assert nb * R == nrows
    period = nb if period is None else period
    prev_ids = [k for k, r in enumerate(rows) if r['prev']]
    n_rows, n_prev, n_par, n_out, n_st = len(rows), len(prev_ids), len(params), len(outs), len(state_shapes)
    ex = list(gather or [])
    n_ex = len(ex)

    def body(*refs):
        o = 0
        cur = refs[o:o + n_rows]; o += n_rows
        prv = refs[o:o + n_prev]; o += n_prev
        par = refs[o:o + n_par]; o += n_par + len(out_alias or {})
        exg = refs[o:o + n_ex]; o += n_ex
        out = refs[o:o + n_out]; o += n_out
        sav = refs[o:o + (n_st if save_states else 0)]; o += len(sav)
        exr = refs[o:o + n_ex]; o += n_ex
        st = refs[o:o + n_st]; o += n_st
        sems = refs[o:]
        i = pl.program_id(1)
        first = (i % period) == 0
        if n_ex:
            @pl.when((pl.program_id(0) == 0) & (i == 0))
            def _():
                _exchange_start(exg, exr, *sems)

        @pl.when(i == 0)
        def _():
            for s in st:
                s[...] = jnp.zeros_like(s)

        xs = [r[...].astype(F32) for r in cur]
        xp = [r[...].astype(F32) for r in prv]
        ps = [r[...] for r in par]
        sts = [s[...] for s in st]
        for sv, s in zip(sav, sts):
            sv[0] = s
        ov, ns = f(first, xp, xs, ps, sts)
        for r, v in zip(out, ov):
            r[...] = v.astype(r.dtype)
        for s, v in zip(st, ns):
            s[...] = v
        if n_ex:
            @pl.when((pl.program_id(0) == ncol - 1) & (i == nb - 1))
            def _():
                _exchange_wait(exg, exr, *sems)

    in_specs = [_bspec(r, R, lambda i: i) for r in rows]
    in_specs += [_pspec(rows[k], R, lambda i: i) for k in prev_ids]
    in_specs += [pl.BlockSpec(p['bs'], functools.partial(lambda jc, i, idx: idx(jc), idx=p['idx'])) for p in params]
    out_specs = [_bspec(o_, R, lambda i: i) for o_ in outs]
    out_shape = [jax.ShapeDtypeStruct(o_['ls'] + (nrows, o_['n']), o_['dt']) for o_ in outs]
    if save_states:
        for s in state_shapes:
            out_specs.append(pl.BlockSpec((1,) + tuple(s), lambda jc, i, nd=len(s): (i,) + (0,) * nd))
            out_shape.append(jax.ShapeDtypeStruct((nb,) + tuple(s), F32))
    args = [r['a'] for r in rows] + [rows[k]['a'] for k in prev_ids] + [p['a'] for p in params]
    aliases = {}
    for n_, arr in sorted((out_alias or {}).items()):
        assert arr.shape == out_shape[n_].shape and arr.dtype == out_shape[n_].dtype
        aliases[len(args)] = n_
        args.append(arr)
        in_specs.append(pl.BlockSpec(memory_space=pl.ANY))
    n_sav = len(out_shape) - n_out
    in_specs += [pl.BlockSpec(memory_space=pl.ANY)] * n_ex
    out_specs += [pl.BlockSpec(memory_space=pl.ANY)] * n_ex
    out_shape += [jax.ShapeDtypeStruct((N_DEV,) + tuple(g.shape), g.dtype) for g in ex]
    cp = pltpu.CompilerParams(dimension_semantics=("arbitrary", "arbitrary"), vmem_limit_bytes=VMEM_LIMIT,
                              has_side_effects=bool(n_ex))
    res = pl.pallas_call(
        body, name=name, grid=(ncol, nb), in_specs=in_specs, out_specs=out_specs, out_shape=out_shape,
        scratch_shapes=[pltpu.VMEM(tuple(s), F32) for s in state_shapes] + (_exchange_sems(n_ex) if n_ex else []),
        input_output_aliases=aliases, compiler_params=cp)(*args, *ex)
    if n_ex:
        return list(res[:n_out]), list(res[n_out:n_out + n_sav]), list(res[n_out + n_sav:])
    return list(res[:n_out]), list(res[n_out:])


def _seq_bwd(name, f, R, rows, params, state_shapes, douts, saved, *, ncol=1, period=None, dx_dt=F32, dx_add=None,
             dx_bf=False, dx_alias=None, exchange=None):
    ex = list(exchange or [])
    n_ex = len(ex)
    nrows = rows[0]['a'].shape[-2]
    nb = nrows // R
    period = nb if period is None else period
    prev_ids = [k for k, r in enumerate(rows) if r['prev']]
    drow_ids = [k for k, r in enumerate(rows) if r['diff']]
    dpar_ids = [k for k, p in enumerate(params) if p['diff']]
    for k in prev_ids:
        assert rows[k]['diff']
    dx_add, dx_alias = dict(dx_add or {}), dict(dx_alias or {})
    add_ids, alias_ids = sorted(dx_add), sorted(dx_alias)
    n_rows, n_prev, n_par, n_do, n_st = len(rows), len(prev_ids), len(params), len(douts), len(state_shapes)
    n_dx, n_dp, n_add, n_al = len(drow_ids), len(dpar_ids), len(add_ids), len(alias_ids)

    def body(*refs):
        o = 0
        cur = refs[o:o + n_rows]; o += n_rows
        prv = refs[o:o + n_prev]; o += n_prev
        par = refs[o:o + n_par]; o += n_par
        sav = refs[o:o + n_st]; o += n_st
        dou = refs[o:o + n_do]; o += n_do
        adr = refs[o:o + n_add]; o += n_add
        o += n_al
        exg = refs[o:o + n_ex]; o += n_ex
        dxr = refs[o:o + n_dx]; o += n_dx
        dpr = refs[o:o + n_dp]; o += n_dp
        dxb = refs[o:o + (n_dx if dx_bf else 0)]; o += len(dxb)
        exr = refs[o:o + n_ex]; o += n_ex
        dst = refs[o:o + n_st]; o += n_st
        car = refs[o:o + n_prev]; o += n_prev
        sems = refs[o:]
        j = pl.program_id(1)
        i = nb - 1 - j
        first = (i % period) == 0
        if n_ex:
            @pl.when((pl.program_id(0) == 0) & (j == 0))
            def _():
                _exchange_start(exg, exr, *sems)

        @pl.when(j == 0)
        def _():
            for s in tuple(dst) + tuple(car) + tuple(dpr):
                s[...] = jnp.zeros_like(s)

        xs = [r[...].astype(F32) for r in cur]
        xp = [r[...].astype(F32) for r in prv]
        ps = [r[...] for r in par]
        sts = [s[0] for s in sav]

        def g(dxs, dxp, dps, dsts):
            xs_, ps_ = list(xs), list(ps)
            for k, v in zip(drow_ids, dxs):
                xs_[k] = v
            for k, v in zip(dpar_ids, dps):
                ps_[k] = v
            ov, ns = f(first, list(dxp), xs_, ps_, list(dsts))
            return tuple(ov), tuple(ns)

        _, vjp = jax.vjp(g, tuple(xs[k] for k in drow_ids), tuple(xp), tuple(ps[k] for k in dpar_ids), tuple(sts))
        dxs, dxp, dps, dsts = vjp((tuple(r[...].astype(F32) for r in dou), tuple(s[...] for s in dst)))
        dxs = list(dxs)
        for n_, pos in enumerate(add_ids):
            dxs[pos] = dxs[pos] + adr[n_][...].astype(F32)
        tails = {}
        for n_, k in enumerate(prev_ids):
            pos = drow_ids.index(k)
            if rows[k]['prev'] == 'block':
                dxs[pos] = dxs[pos] + car[n_][...]
            else:
                tails[pos] = car[n_][...]
            car[n_][...] = dxp[n_]
        for pos, v in enumerate(dxs):
            outs_ = [dxr[pos]] + ([dxb[pos]] if dx_bf else [])
            if pos in tails:
                v = jnp.concatenate([v[..., :R - HALO, :], v[..., R - HALO:, :] + tails[pos]], axis=-2)
            for r in outs_:
                r[...] = v.astype(r.dtype)
        for r, v in zip(dpr, dps):
            r[...] += v
        for s, v in zip(dst, dsts):
            s[...] = v
        if n_ex:
            @pl.when((pl.program_id(0) == ncol - 1) & (j == nb - 1))
            def _():
                _exchange_wait(exg, exr, *sems)

    def rev(j):
        return nb - 1 - j

    def dspec(k):
        return _bspec(rows[k], R, rev)

    in_specs = [_bspec(r, R, rev) for r in rows]
    in_specs += [_pspec(rows[k], R, rev) for k in prev_ids]
    in_specs += [pl.BlockSpec(p['bs'], functools.partial(lambda jc, j, idx: idx(jc), idx=p['idx'])) for p in params]
    in_specs += [pl.BlockSpec((1,) + tuple(s), lambda jc, j, nd=len(s): (nb - 1 - j,) + (0,) * nd) for s in state_shapes]
    in_specs += [_bspec(d, R, rev) for d in douts]
    in_specs += [dspec(drow_ids[pos]) for pos in add_ids]
    in_specs += [pl.BlockSpec(memory_space=pl.ANY) for _ in alias_ids + ex]
    out_specs = [dspec(k) for k in drow_ids]
    out_shape = [jax.ShapeDtypeStruct(tuple(rows[k]['a'].shape[:-1]) + (rows[k]['dn'],), dx_dt) for k in drow_ids]
    for k in dpar_ids:
        p = params[k]
        out_specs.append(pl.BlockSpec(p['bs'], functools.partial(lambda jc, j, idx: idx(jc), idx=p['idx'])))
        out_shape.append(jax.ShapeDtypeStruct(p['a'].shape, F32))
    if dx_bf:
        out_specs += [dspec(k) for k in drow_ids]
        out_shape += [jax.ShapeDtypeStruct(tuple(rows[k]['a'].shape[:-1]) + (rows[k]['dn'],), BF) for k in drow_ids]
    out_specs += [pl.BlockSpec(memory_space=pl.ANY) for _ in ex]
    out_shape += [jax.ShapeDtypeStruct(g.shape, g.dtype) for g in ex]
    scratch = [pltpu.VMEM(tuple(s), F32) for s in state_shapes]
    scratch += [pltpu.VMEM(tuple(d for d in rows[k]['lb'] if d is not None) + (_prev_rows(rows[k], R), rows[k]['w']), F32)
                for k in prev_ids]
    if n_ex:
        scratch += _exchange_sems(n_ex)
    args = ([r['a'] for r in rows] + [rows[k]['a'] for k in prev_ids] + [p['a'] for p in params] + list(saved)
            + [d['a'] for d in douts] + [dx_add[pos] for pos in add_ids] + [dx_alias[pos] for pos in alias_ids])
    n_in = len(args)
    aliases = {n_in - n_al + n_: pos for n_, pos in enumerate(alias_ids)}
    for pos in alias_ids:
        assert dx_alias[pos].shape == out_shape[pos].shape and dx_alias[pos].dtype == out_shape[pos].dtype
    cp = pltpu.CompilerParams(dimension_semantics=("arbitrary", "arbitrary"), vmem_limit_bytes=VMEM_LIMIT,
                              has_side_effects=bool(n_ex))
    res = pl.pallas_call(
        body, name=name, grid=(ncol, nb), in_specs=in_specs, out_specs=out_specs, out_shape=out_shape,
        scratch_shapes=scratch, input_output_aliases=aliases, compiler_params=cp)(*args, *ex)
    lists = [list(res[:n_dx]), list(res[n_dx:n_dx + n_dp])]
    o = n_dx + n_dp
    if dx_bf:
        lists.append(list(res[o:o + n_dx]))
        o += n_dx
    if n_ex:
        lists.append(list(res[o:o + n_ex]))
    return tuple(lists)


def _tile(n, cands):
    for c in cands:
        if n % c == 0:
            return c
    return n


def _mm_call(name, grid, a, a_spec, b, b_spec, contract, out_shape, out_spec, acc_shape, add=None, add_spec=None,
             exchange=None, gather=None):
    nk = grid[2]
    ca, cb = contract
    has_add = add is not None
    ex = list(exchange or []) + list(gather or [])
    ex_shapes = [g.shape for g in exchange or []] + [(N_DEV,) + tuple(g.shape) for g in gather or []]
    n_ex = len(ex)
    n_in = 2 + has_add

    def body(*refs):
        a_ref, b_ref = refs[0], refs[1]
        add_ref = refs[2] if has_add else None
        o_ref = refs[n_in + n_ex]
        scr = refs[n_in + 2 * n_ex + 1:]
        step = [pl.program_id(d) for d in range(3)]
        if n_ex:
            g_refs, r_refs, sems = refs[n_in:n_in + n_ex], refs[n_in + n_ex + 1:n_in + 2 * n_ex + 1], scr[-3:]

            @pl.when((step[0] == 0) & (step[1] == 0) & (step[2] == 0))
            def _():
                _exchange_start(g_refs, r_refs, *sems)

        part = _dot(a_ref[...], b_ref[...], ca, cb)

        def finish(r):
            if has_add:
                r = r + add_ref[...].astype(F32)
            o_ref[...] = r.astype(o_ref.dtype)

        if nk == 1:
            finish(part)
        else:
            acc = scr[0]

            @pl.when(step[2] == 0)
            def _():
                acc[...] = part

            @pl.when(step[2] > 0)
            def _():
                acc[...] += part

            @pl.when(step[2] == nk - 1)
            def _():
                finish(acc[...])

        if n_ex:
            @pl.when((step[0] == grid[0] - 1) & (step[1] == grid[1] - 1) & (step[2] == grid[2] - 1))
            def _():
                _exchange_wait(g_refs, r_refs, *sems)

    in_specs, args = [a_spec, b_spec], [a, b]
    if has_add:
        in_specs.append(add_spec)
        args.append(add)
    any_spec = pl.BlockSpec(memory_space=pl.ANY)
    scratch = [] if nk == 1 else [pltpu.VMEM(acc_shape, F32)]
    if n_ex:
        scratch += _exchange_sems(n_ex)
    res = pl.pallas_call(
        body, name=name, grid=grid, in_specs=in_specs + [any_spec] * n_ex, out_specs=[out_spec] + [any_spec] * n_ex,
        out_shape=[out_shape] + [jax.ShapeDtypeStruct(s, g.dtype) for s, g in zip(ex_shapes, ex)], scratch_shapes=scratch,
        compiler_params=pltpu.CompilerParams(
            dimension_semantics=("arbitrary",) * 3 if n_ex else ("parallel", "parallel", "arbitrary"),
            vmem_limit_bytes=VMEM_LIMIT, has_side_effects=bool(n_ex)))(*args, *ex)
    return (res[0], list(res[1:])) if n_ex else res[0]


def _matmul(a, b, mode="nn", add=None, out_dtype=F32, name="matmul", **pushed):
    if mode == "nn":
        (M, K), N = a.shape, b.shape[1]
    elif mode == "nt":
        (M, K), N = a.shape, b.shape[0]
    else:
        (K, M), N = a.shape, b.shape[1]
    if mode == "tn" and 1024 < N <= 5120:
        tm, tn = _tile(M, (512, 256, 128, 64, 32, 16, 8)), N
        tk = _tile(K, (2048 if tn <= 3072 else 1024, 1024, 512, 256, 128))
    else:
        tk = K if K <= 5120 else _tile(K, (2048, 1024, 512, 256, 128))
        tm = _tile(M, ((2048,) if tk <= 1024 and mode != "tn" else ()) + (1024, 512, 256, 128, 64, 32, 16, 8))
        tn = _tile(N, (512, 256, 128))
    if mode == "tn":
        a_spec = pl.BlockSpec((tk, tm), lambda i, j, k: (k, i))
    else:
        a_spec = pl.BlockSpec((tm, tk), lambda i, j, k: (i, k))
    if mode == "nt":
        b_spec = pl.BlockSpec((tn, tk), lambda i, j, k: (j, k))
    else:
        b_spec = pl.BlockSpec((tk, tn), lambda i, j, k: (k, j))
    blk = pl.BlockSpec((tm, tn), lambda i, j, k: (i, j))
    return _mm_call(name, (M // tm, N // tn, K // tk), a, a_spec, b, b_spec,
                    {"nn": (1, 0), "nt": (1, 1), "tn": (0, 0)}[mode], jax.ShapeDtypeStruct((M, N), out_dtype), blk,
                    (tm, tn), add, blk, **pushed)


FF_SH = 2 * D_FF // N_DEV


def _ffn_up(h2, wup, **pushed):
    S, D = h2.shape
    tm = _tile(S, (2048, 1024, 512, 256, 128))
    return _mm_call("matmul_up", (S // tm, N_DEV, 1), h2, pl.BlockSpec((tm, D), lambda m, j, k: (m, 0)),
                    wup, pl.BlockSpec((None, D, FF_SH), lambda m, j, k: (j, 0, 0)), (1, 0),
                    jax.ShapeDtypeStruct((2, N_DEV // 2, S, FF_SH), F32),
                    pl.BlockSpec((None, None, tm, FF_SH), lambda m, j, k: (j // 4, j % 4, m, 0)), (tm, FF_SH), **pushed)


def _ffn_down(act, wd, x1, **pushed):
    _, S, _ = act.shape
    D = wd.shape[1]
    tm, tn = _tile(S, (1024, 512, 256, 128)), _tile(D, (1024, 512, 256, 128))
    blk = pl.BlockSpec((tm, tn), lambda m, n, k: (m, n))
    return _mm_call("matmul_down", (S // tm, D // tn, N_DEV // 2), act,
                    pl.BlockSpec((None, tm, FF_SH), lambda m, n, k: (k, m, 0)), wd,
                    pl.BlockSpec((FF_SH, tn), lambda m, n, k: (k, n)), (1, 0), jax.ShapeDtypeStruct((S, D), F32), blk,
                    (tm, tn), x1, blk, **pushed)


def _ffn_dact(dxb, wd):
    S, D = dxb.shape
    tm = _tile(S, (2048, 1024, 512, 256, 128))
    return _mm_call("matmul_dact", (S // tm, N_DEV // 2, 1), dxb, pl.BlockSpec((tm, D), lambda m, j, k: (m, 0)), wd,
                    pl.BlockSpec((FF_SH, D), lambda m, j, k: (j, 0)), (1, 1),
                    jax.ShapeDtypeStruct((N_DEV // 2, S, FF_SH), BF),
                    pl.BlockSpec((None, tm, FF_SH), lambda m, j, k: (j, m, 0)), (tm, FF_SH))


def _ffn_dw_down(act, dxb):
    _, S, _ = act.shape
    D = dxb.shape[1]
    tk, tn = _tile(S, (2048, 1024, 512, 256, 128)), _tile(D, (512, 256, 128))
    return _mm_call("matmul_dw_down", (N_DEV // 2, D // tn, S // tk), act,
                    pl.BlockSpec((None, tk, FF_SH), lambda j, n, k: (j, k, 0)), dxb,
                    pl.BlockSpec((tk, tn), lambda j, n, k: (k, n)), (0, 0), jax.ShapeDtypeStruct((D_FF, D), BF),
                    pl.BlockSpec((FF_SH, tn), lambda j, n, k: (j, n)), (FF_SH, tn))


def _ffn_dw_up(h2, du, exchange=None):
    S, D = h2.shape
    tk = _tile(S, (2048, 1024, 512, 256, 128))
    return _mm_call("matmul_dw_up", (N_DEV, 1, S // tk), h2, pl.BlockSpec((tk, D), lambda j, n, k: (k, 0)), du,
                    pl.BlockSpec((None, None, tk, FF_SH), lambda j, n, k: (j // 4, j % 4, k, 0)), (0, 0),
                    jax.ShapeDtypeStruct((N_DEV, D, FF_SH), BF),
                    pl.BlockSpec((None, D, FF_SH), lambda j, n, k: (j, 0, 0)), (D, FF_SH), exchange=exchange)


def _ffn_dh2(du, wup, exchange=None):
    S = du.shape[2]
    D = wup.shape[1]
    tm = _tile(S, (1024, 512, 256, 128))
    return _mm_call("matmul_dh2", (S // tm, 1, N_DEV), du,
                    pl.BlockSpec((None, None, tm, FF_SH), lambda m, n, k: (k // 4, k % 4, m, 0)), wup,
                    pl.BlockSpec((None, D, FF_SH), lambda m, n, k: (k, 0, 0)), (1, 1),
                    jax.ShapeDtypeStruct((S, D), F32), pl.BlockSpec((tm, D), lambda m, n, k: (m, 0)), (tm, D),
                    exchange=exchange)


def f_rmsnorm(first, xp, xs, ps, sts):
    return (_rms(xs[0], ps[0]),), ()


def _same_block(shape, rows_per, cols_per):
    return (lax.broadcasted_iota(jnp.int32, shape, 0) // rows_per) == (lax.broadcasted_iota(jnp.int32, shape, 1) // cols_per)


@jax.custom_vjp
def _head_mean(x):
    n = x.shape[1]
    return _dot_hi(x, jnp.where(_same_block((n, n), XA_HD, XA_HD), 1.0 / XA_HD, 0.0), 1, 0)


_head_mean.defvjp(lambda x: (_head_mean(x), None), lambda r, g: (_head_mean(g),))


def f_xattn(first, xp, xs, ps, sts):
    (xq,), (kv, qn, kn) = xs, ps
    k, v = _split(kv, [D_XA, D_XA])
    m_rows = kv.shape[0]
    q = xq * lax.rsqrt(_head_mean(xq * xq) + EPS) * jnp.concatenate([qn] * XA_HEADS, axis=1)
    k = k * lax.rsqrt(_head_mean(k * k) + EPS) * jnp.concatenate([kn] * XA_HEADS, axis=1)
    kt = k.T
    kbd = jnp.where(_same_block((D_XA, XA_HEADS * m_rows), XA_HD, m_rows), jnp.concatenate([kt] * XA_HEADS, axis=1), 0.0)
    s = mm_nn(q, kbd) * (XA_HD ** -0.5)
    ps_ = []
    for sh in _split(s, [m_rows] * XA_HEADS):
        mx = lax.stop_gradient(jnp.max(sh, axis=-1, keepdims=True))
        p = jnp.exp(sh - mx)
        ps_.append(p / jnp.sum(p, axis=-1, keepdims=True))
    vbd = jnp.where(_same_block((XA_HEADS * m_rows, D_XA), m_rows, XA_HD), jnp.concatenate([v] * XA_HEADS, axis=0), 0.0)
    return (mm_nn(jnp.concatenate(ps_, axis=1), vbd),), ()


def _conv(xp, x, w, b, first, taps):
    xp = jnp.where(first, 0.0, xp)
    y = b + w[taps - 1:taps] * x
    for d in range(1, taps):
        y = y + w[taps - 1 - d:taps - d] * _shift(xp, x, d)
    return y


def _unstack2(x):
    @jax.custom_vjp
    def us(x):
        return x[0], x[1]

    us.defvjp(lambda x: (us(x), None), lambda r, g: (jnp.stack(g),))
    return us(x)


def f_ffn_act(first, xp, xs, ps, sts):
    (up,), (u,), (wg, wv, bg, bv) = xp, xs, ps
    (ugp, uvp), (ug, uv) = _unstack2(up), _unstack2(u)
    gate = _conv(ugp, ug, wg, bg, first, FFN_CONV)
    val = _conv(uvp, uv, wv, bv, first, FFN_CONV)
    return (jax.nn.silu(gate) * val,), ()


def _gla_chunk(q, k, v, la, sts, dk, dv):
    c, nh = q.shape[0], len(sts)
    b = cumsum_rows(la)
    b_last = _row(b, c - 1)
    b_ref = _row(b, c // 2 - 1)
    qe, ke = _split(q * jnp.exp(b - b_ref), [dk] * nh), _split(k * jnp.exp(b_ref - b), [dk] * nh)
    qi, kl = _split(q * jnp.exp(b), [dk] * nh), _split(k * jnp.exp(b_last - b), [dk] * nh)
    dec, vs = _split(jnp.exp(b_last), [dk] * nh), _split(v, [dv] * nh)
    tril = _tril(c)
    outs, new = [], []
    for h in range(nh):
        att = jnp.where(tril, mm_nt(qe[h], ke[h]), 0.0)
        outs.append(mm_nn(att, vs[h]) + mm_nt(qi[h], sts[h]))
        new.append(sts[h] * dec[h] + mm_tn(vs[h], kl[h]))
    return outs, tuple(new)


def _split_rows(x, n):
    c = x.shape[0] // n

    @jax.custom_vjp
    def sp(x):
        return tuple(x[i * c:(i + 1) * c] for i in range(n))

    sp.defvjp(lambda x: (sp(x), None), lambda r, g: (jnp.concatenate(list(g), axis=0),))
    return sp(x)


def _gla_scan(q, k, v, la, sts, dk, dv):
    n = q.shape[0] // CHUNK
    per_chunk = []
    for qc, kc, vc, lc in zip(*(_split_rows(t, n) for t in (q, k, v, la))):
        o, sts = _gla_chunk(qc, kc, vc, lc, sts, dk, dv)
        per_chunk.append(o)
    return [jnp.concatenate([o[h] for o in per_chunk], axis=0) for h in range(len(sts))], sts


def _a_cols(ntot):
    used = D_XA + 2 * GLA_HEADS * GLA_DK + D_MIX + GLA_RANK + D_MIX
    return [D_XA, GLA_HEADS * GLA_DK, GLA_HEADS * GLA_DK, D_MIX, GLA_RANK, D_MIX] + ([ntot - used] if ntot > used else [])


def f_gla(first, xp, xs, ps, sts):
    (p,), (wg2, bg, on) = xs, ps
    parts = _split(p, _a_cols(p.shape[1]))
    q, k, v, glr, og = parts[1:6]
    la = jax.nn.log_sigmoid(mm_nn(glr, wg2) + bg) / GLA_GATE_NORM
    outs, new = _gla_scan(q * (GLA_DK ** -0.5), k, v, la, tuple(sts), GLA_DK, GLA_DV)
    return (jnp.concatenate([_rms(o, on) for o in outs], axis=1) * jax.nn.silu(og),), new


def f_hgrn(first, xp, xs, ps, sts):
    (p,), (lbp, on) = xs, ps
    _, q, fgate, iv, og = _split(p, [D_XA, D_MIX, D_MIX, D_MIX, D_MIX])
    e = jnp.exp(lbp - jnp.max(lbp, axis=0, keepdims=True))
    row = lax.broadcasted_iota(jnp.int32, e.shape, 0)
    lb = jnp.sum(jnp.where(row >= 1, e, 0.0), axis=0, keepdims=True) / jnp.sum(e, axis=0, keepdims=True)
    fg = lb + (1.0 - lb) * jax.nn.sigmoid(fgate)
    outs, new = _gla_scan(jax.nn.silu(q), 1.0 - fg, iv, jnp.log(fg), tuple(sts), HGRN_DK, HGRN_DV)
    return (jnp.concatenate([_rms(o, on) for o in outs], axis=1) * jax.nn.sigmoid(og),), new


def _c_cols(ntot):
    gn = SSM_GROUPS * SSM_STATE
    used = D_XA + D_MIX + D_MIX + 2 * gn + SSM_HEADS
    return [D_XA, D_MIX, D_MIX + 2 * gn, SSM_HEADS] + ([ntot - used] if ntot > used else [])


def f_ssd(first, xp, xs, ps, sts):
    (pp,), (p,), (cw, cb, dtb, alog, dsk, ng) = xp, xs, ps
    gn = SSM_GROUPS * SSM_STATE
    _, z, xbc, dtr = _split(p, _c_cols(p.shape[1]))[:4]
    xbc_p = _split(pp, _c_cols(p.shape[1]))[2]
    xbc = jax.nn.silu(_conv(xbc_p, xbc, cw, cb, first, SSM_CONV))
    xs_, bm, cm = _split(xbc, [D_MIX, gn, gn])
    dt = jax.nn.softplus(dtr + dtb)
    n = p.shape[0] // CHUNK
    ys, sts = [], tuple(sts)
    for xc, bc, cc, dc in zip(*(_split_rows(t, n) for t in (xs_, bm, cm, dt))):
        y, sts = _ssd_chunk(xc, bc, cc, dc, alog, dsk, sts)
        ys.append(y)
    y = jnp.concatenate(ys, axis=0) * jax.nn.silu(z)
    gw = D_MIX // SSM_GROUPS
    yg = _split(y, [gw] * SSM_GROUPS)
    ngs = _split(ng, [gw] * SSM_GROUPS)
    y = jnp.concatenate([_rms(yg[g], ngs[g]) for g in range(SSM_GROUPS)], axis=1)
    return (y,), sts


def _ssd_chunk(xs_, bm, cm, dt, alog, dsk, sts):
    c = xs_.shape[0]
    hg = SSM_HEADS // SSM_GROUPS
    a = dt * (-jnp.exp(alog))
    acs = cumsum_rows(a)
    acs_t = cumsum_rows_t(a)
    acs_last = _row(acs, c - 1)
    dt_h = _split(dt, [1] * SSM_HEADS)
    acs_h = _split(acs, [1] * SSM_HEADS)
    al_h = _split(acs_last, [1] * SSM_HEADS)
    d_h = _split(dsk, [1] * SSM_HEADS)
    x2s = _split(xs_, [2 * SSM_HD] * (SSM_HEADS // 2))
    bms = _split(bm, [SSM_STATE] * SSM_GROUPS)
    cms = _split(cm, [SSM_STATE] * SSM_GROUPS)
    tril = _tril(c)
    cbs = [mm_nt(cms[g], bms[g]) for g in range(SSM_GROUPS)]
    ys, new = [], []
    for j in range(SSM_HEADS // 2):
        g = (2 * j) // hg
        h0, h1 = 2 * j, 2 * j + 1
        xdt = x2s[j] * _lane_pair(dt_h[h0], dt_h[h1])
        acs2 = _lane_pair(acs_h[h0], acs_h[h1])
        al2 = _lane_pair(al_h[h0], al_h[h1])
        yd = []
        for h in (h0, h1):
            seg = acs_h[h] - _row(acs_t, h)
            lm = jnp.exp(jnp.where(tril, seg, NEG))
            yd.append(mm_nn(cbs[g] * lm, xdt))
        lane = lax.broadcasted_iota(jnp.int32, xdt.shape, 1)
        y_diag = jnp.where(lane < SSM_HD, yd[0], yd[1])
        y_off = mm_nn(cms[g], sts[j]) * jnp.exp(acs2)
        x_end = xdt * jnp.exp(al2 - acs2)
        new.append(sts[j] * jnp.exp(al2) + mm_tn(bms[g], x_end))
        ys.append(y_diag + y_off + _lane_pair(d_h[h0], d_h[h1]) * x2s[j])
    return jnp.concatenate(ys, axis=1), tuple(new)


def f_dil_prep(first, xp, xs, ps, sts):
    (p, pos), (qn, kn, invf, sign) = xs, ps
    nh = len(DIL_GROUPS) * DIL_HEADS
    _, q, k, v = _split(p, [D_XA] + [nh * DIL_HD] * 3)
    ang = pos * invf
    cos, sin = jnp.cos(ang), jnp.sin(ang) * sign

    def rope(t, g):
        hs = _split(t, [DIL_HD] * nh)
        out = []
        for h in hs:
            n = _rms(h, g)
            out.append(n * cos + _roll_half(n) * sin)
        return [jnp.concatenate(out[i:i + DIL_HEADS], axis=1) for i in range(0, nh, DIL_HEADS)]

    return tuple(rope(q, qn) + rope(k, kn) + list(_split(v, [D_DIL] * len(DIL_GROUPS)))), ()


def f_dil_attn(first, xp, xs, ps, sts):
    (kp, vp), (q, k, v) = xp, xs
    Q = DIL_BLOCK
    qs, ks, vs = (_split(t, [DIL_HD] * DIL_HEADS) for t in (q, k, v))
    kps, vps = (_split(t, [DIL_HD] * DIL_HEADS) for t in (kp, vp))
    i = lax.broadcasted_iota(jnp.int32, (Q, 2 * Q), 0)
    j = lax.broadcasted_iota(jnp.int32, (Q, 2 * Q), 1)
    dist = Q + i - j
    mask = (dist >= 0) & (dist <= Q) & (jnp.logical_not(first) | (j >= Q))
    outs, lses = [], []
    for h in range(DIL_HEADS):
        k2 = jnp.concatenate([kps[h], ks[h]], axis=0)
        v2 = jnp.concatenate([vps[h], vs[h]], axis=0)
        s = jnp.where(mask, mm_nt(qs[h], k2) * (DIL_HD ** -0.5), NEG)
        m = lax.stop_gradient(jnp.max(s, axis=-1, keepdims=True))
        p = jnp.exp(s - m)
        l = jnp.sum(p, axis=-1, keepdims=True)
        outs.append(mm_nn(p / l, v2))
        lses.append(jnp.broadcast_to(m + jnp.log(l), (Q, DIL_HD)))
    return (jnp.concatenate(outs, axis=1), jnp.concatenate(lses, axis=1)), ()


def f_dil_merge(first, xp, xs, ps, sts):
    o0, o1, o2, l0, l1, l2 = xs
    m = jnp.maximum(jnp.maximum(l0, l1), l2)
    e0, e1, e2 = jnp.exp(l0 - m), jnp.exp(l1 - m), jnp.exp(l2 - m)
    den = e0 + e1 + e2
    return ((e0 * o0 + e1 * o1 + e2 * o2) / den,), ()


def _loss_head(y, target):
    S, D = y.shape
    R = _tile(S, (512, 256, 128, 64, 32, 16, 8))

    def body(y_ref, t_ref, dy_ref, dyb_ref, l_ref):
        e = y_ref[...] - t_ref[...]
        dy_ref[...] = e * (1.0 / D)
        dyb_ref[...] = (e * (1.0 / D)).astype(BF)

        @pl.when(pl.program_id(0) == 0)
        def _():
            l_ref[...] = jnp.zeros_like(l_ref)

        l_ref[...] += jnp.broadcast_to(0.5 * jnp.sum(jnp.mean(e * e, axis=-1, keepdims=True), axis=0, keepdims=True),
                                       l_ref.shape)

    blk = pl.BlockSpec((R, D), lambda i: (i, 0))
    dy, dyb, l = pl.pallas_call(
        body, name="loss_head", grid=(S // R,), in_specs=[blk, blk],
        out_specs=[blk, blk, pl.BlockSpec((8, 128), lambda i: (0, 0))],
        out_shape=[jax.ShapeDtypeStruct((S, D), F32), jax.ShapeDtypeStruct((S, D), BF),
                   jax.ShapeDtypeStruct((8, 128), F32)],
        compiler_params=pltpu.CompilerParams(dimension_semantics=("arbitrary",)))(y, target)
    return dy, dyb, l[0, 0]


def _adamw(parts, w, m, v, name):
    _, n, width = parts.shape
    tr = _tile(n, [t for t in (512, 256, 128, 64, 32, 16, 8) if t * width <= ADAM_BLOCK])

    def body(p_ref, w_ref, m_ref, v_ref, g_ref, d_ref, nm_ref, nv_ref):
        g = p_ref[0].astype(F32)
        for s in range(1, N_DEV):
            g = g + p_ref[s].astype(F32)
        nm = ADAM_B1 * m_ref[...] + (1.0 - ADAM_B1) * g
        nv = ADAM_B2 * v_ref[...] + (1.0 - ADAM_B2) * (g * g)
        m_hat = nm / (1.0 - ADAM_B1 ** ADAM_STEP)
        v_hat = nv / (1.0 - ADAM_B2 ** ADAM_STEP)
        g_ref[...] = g
        d_ref[...] = -ADAM_LR * (m_hat / (jnp.sqrt(v_hat) + ADAM_EPS) + ADAM_WD * w_ref[...])
        nm_ref[...] = nm
        nv_ref[...] = nv

    blk = pl.BlockSpec((tr, width), lambda i: (i, 0))
    return pl.pallas_call(
        body, name=name, grid=(n // tr,),
        in_specs=[pl.BlockSpec((N_DEV, tr, width), lambda i: (0, i, 0)), blk, blk, blk],
        out_specs=[blk] * 4, out_shape=[jax.ShapeDtypeStruct((n, width), F32)] * 4,
        compiler_params=pltpu.CompilerParams(dimension_semantics=("arbitrary",), vmem_limit_bytes=VMEM_LIMIT))(
            parts, w, m, v)


def _peer(k):
    x, y, c = lax.axis_index("x"), lax.axis_index("y"), lax.axis_index("c")
    px = 1 - x if k & 4 else x
    py = 1 - y if k & 2 else y
    pc = 1 - c if k & 1 else c
    return (px, py, pc), 4 * px + 2 * py + pc


def _my_id():
    return 4 * lax.axis_index("x") + 2 * lax.axis_index("y") + lax.axis_index("c")


def _all_gather(x, name):
    def body(x_ref, out_ref, send, recv, loc):
        me = _my_id()
        mine = pltpu.make_async_copy(x_ref, out_ref.at[me], loc)
        mine.start()
        cps = []
        for k in range(1, N_DEV):
            peer, _ = _peer(k)
            cp = pltpu.make_async_remote_copy(src_ref=x_ref, dst_ref=out_ref.at[me], send_sem=send.at[k - 1],
                                              recv_sem=recv.at[k - 1], device_id=peer,
                                              device_id_type=pl.DeviceIdType.MESH)
            cp.start()
            cps.append(cp)
        for k in range(1, N_DEV):
            peer, pid = _peer(k)
            pltpu.make_async_remote_copy(src_ref=x_ref, dst_ref=out_ref.at[pid], send_sem=send.at[k - 1],
                                         recv_sem=recv.at[k - 1], device_id=peer,
                                         device_id_type=pl.DeviceIdType.MESH).wait_recv()
        for cp in cps:
            cp.wait_send()
        mine.wait()

    return pl.pallas_call(
        body, name=name, out_shape=jax.ShapeDtypeStruct((N_DEV,) + x.shape, x.dtype),
        in_specs=[pl.BlockSpec(memory_space=pl.ANY)], out_specs=pl.BlockSpec(memory_space=pl.ANY),
        scratch_shapes=[pltpu.SemaphoreType.DMA((N_DEV - 1,)), pltpu.SemaphoreType.DMA((N_DEV - 1,)),
                        pltpu.SemaphoreType.DMA],
        compiler_params=pltpu.CompilerParams(has_side_effects=True))(x)


def _exchange_sems(n):
    return [pltpu.SemaphoreType.DMA((n * (N_DEV - 1),)), pltpu.SemaphoreType.DMA((n * (N_DEV - 1),)),
            pltpu.SemaphoreType.DMA((n,))]


def _exchange_copies(g_refs, out_refs, send, recv, loc, with_arrivals):
    me = _my_id()

    def mine(g, o, d):
        return g.at[d] if len(g.shape) == len(o.shape) else g

    local = [pltpu.make_async_copy(mine(g, o, me), o.at[me], loc.at[w]) for w, (g, o) in enumerate(zip(g_refs, out_refs))]
    pushes, arrivals = [], []
    for k in range(1, N_DEV):
        peer, pid = _peer(k)
        for w, (g, o) in enumerate(zip(g_refs, out_refs)):
            s = w * (N_DEV - 1) + k - 1
            ends = [(mine(g, o, pid), o.at[me], pushes)] + ([(mine(g, o, me), o.at[pid], arrivals)] if with_arrivals else [])
            for src, dst, into in ends:
                into.append(pltpu.make_async_remote_copy(src_ref=src, dst_ref=dst, send_sem=send.at[s],
                                                         recv_sem=recv.at[s], device_id=peer,
                                                         device_id_type=pl.DeviceIdType.MESH))
    return local, pushes, arrivals


def _exchange_start(g_refs, out_refs, send, recv, loc):
    local, pushes, _ = _exchange_copies(g_refs, out_refs, send, recv, loc, False)
    for cp in local + pushes:
        cp.start()


def _exchange_wait(g_refs, out_refs, send, recv, loc):
    local, pushes, arrivals = _exchange_copies(g_refs, out_refs, send, recv, loc, True)
    for cp in arrivals:
        cp.wait_recv()
    for cp in pushes:
        cp.wait_send()
    for cp in local:
        cp.wait()


def _exchange_many(gs, name):
    n = len(gs)

    def body(*refs):
        g_refs, out_refs, sems = refs[:n], refs[n:2 * n], refs[2 * n:]
        _exchange_start(g_refs, out_refs, *sems)
        _exchange_wait(g_refs, out_refs, *sems)

    return pl.pallas_call(
        body, name=name, out_shape=[jax.ShapeDtypeStruct(g.shape, g.dtype) for g in gs],
        in_specs=[pl.BlockSpec(memory_space=pl.ANY)] * n, out_specs=[pl.BlockSpec(memory_space=pl.ANY)] * n,
        scratch_shapes=_exchange_sems(n), compiler_params=pltpu.CompilerParams(has_side_effects=True))(*gs)


def _gather_many(xs, name):
    n = len(xs)

    def body(*refs):
        x_refs, out_refs, (send, recv, loc) = refs[:n], refs[n:2 * n], refs[2 * n:]
        x, y, c = lax.axis_index("x"), lax.axis_index("y"), lax.axis_index("c")
        me, sibling = (x, y, c), (x, y, 1 - c)
        chips = [(1 - x, y), (x, 1 - y), (1 - x, 1 - y)]

        def slot(p):
            return 4 * p[0] + 2 * p[1] + p[2]

        def copy(w, k, block, to, src=None):
            dst = out_refs[w].at[slot(block)]
            return pltpu.make_async_remote_copy(src_ref=dst if src is None else src, dst_ref=dst,
                                                send_sem=send.at[w * (N_DEV - 1) + k], recv_sem=recv.at[w * (N_DEV - 1) + k],
                                                device_id=to, device_id_type=pl.DeviceIdType.MESH)

        mine = [pltpu.make_async_copy(x_refs[w], out_refs[w].at[slot(me)], loc.at[w]) for w in range(n)]
        for cp in mine:
            cp.start()
        first = []
        for j, chip in enumerate(chips):
            first += [copy(w, 1 + j, me, (*chip, c), src=x_refs[w]) for w in range(n)]
        first += [copy(w, 0, me, sibling, src=x_refs[w]) for w in range(n)]
        for cp in first:
            cp.start()
        passed = []
        for j, chip in enumerate(chips):
            for w in range(n):
                copy(w, 1 + j, (*chip, c), me).wait_recv()
                cp = copy(w, 4 + j, (*chip, c), sibling)
                cp.start()
                passed.append(cp)
        for w in range(n):
            copy(w, 0, sibling, me).wait_recv()
            for j, chip in enumerate(chips):
                copy(w, 4 + j, (*chip, 1 - c), me).wait_recv()
        for cp in first + passed:
            cp.wait_send()
        for cp in mine:
            cp.wait()

    return pl.pallas_call(
        body, name=name, out_shape=[jax.ShapeDtypeStruct((N_DEV,) + x.shape, x.dtype) for x in xs],
        in_specs=[pl.BlockSpec(memory_space=pl.ANY)] * n, out_specs=[pl.BlockSpec(memory_space=pl.ANY)] * n,
        scratch_shapes=[pltpu.SemaphoreType.DMA((n * (N_DEV - 1),)), pltpu.SemaphoreType.DMA((n * (N_DEV - 1),)),
                        pltpu.SemaphoreType.DMA((n,))],
        compiler_params=pltpu.CompilerParams(has_side_effects=True))(*xs)


def _cat_segs(G, ws, n_mix):
    segs = []
    for g in range(G):
        lo, hi = g * ws, (g + 1) * ws
        if lo < n_mix:
            segs.append((g, 0, min(hi, n_mix) - lo, D_XA + lo))
        if hi > n_mix:
            s = max(lo, n_mix)
            segs.append((g, s - lo, hi - s, s - n_mix))
    return segs


def _cat_cols(src, n_mix, ntot):
    G, R, ws = src.shape
    segs = _cat_segs(G, ws, n_mix)
    tr = _tile(R, (256, 128, 64, 32, 16, 8))

    def body(i_ref, o_ref):
        if ntot > G * ws:
            o_ref[...] = jnp.zeros_like(o_ref)
        for g, s, n, d in segs:
            o_ref[:, d:d + n] = i_ref[g][:, s:s + n]

    return pl.pallas_call(
        body, name="cat_cols", grid=(R // tr,), in_specs=[pl.BlockSpec((G, tr, ws), lambda i: (0, i, 0))],
        out_specs=pl.BlockSpec((tr, ntot), lambda i: (i, 0)), out_shape=jax.ShapeDtypeStruct((R, ntot), src.dtype),
        compiler_params=pltpu.CompilerParams(dimension_semantics=("arbitrary",)))(src)


def _uncat_cols(dw, G, ws, n_mix):
    R, ntot = dw.shape
    segs = _cat_segs(G, ws, n_mix)
    tr = _tile(R, (256, 128, 64, 32, 16, 8))

    def body(i_ref, o_ref):
        v = i_ref[...]
        for g, s, n, d in segs:
            o_ref[g, :, s:s + n] = v[:, d:d + n]

    return pl.pallas_call(
        body, name="uncat_cols", grid=(R // tr,), in_specs=[pl.BlockSpec((tr, ntot), lambda i: (i, 0))],
        out_specs=pl.BlockSpec((G, tr, ws), lambda i: (0, i, 0)), out_shape=jax.ShapeDtypeStruct((G, R, ws), dw.dtype),
        compiler_params=pltpu.CompilerParams(dimension_semantics=("arbitrary",)))(dw)


PACK_W = 1024


def _granule(n):
    return (256 if n >= 256 * PACK_W else 8) * PACK_W


def _pack(arrs, dtype):
    flat = jnp.concatenate([a.reshape(-1).astype(dtype) for a in arrs])
    n = flat.shape[0]
    pad = (-n) % _granule(n)
    if pad:
        flat = jnp.concatenate([flat, jnp.zeros((pad,), dtype)])
    return flat.reshape(-1, PACK_W)


def _unpack(packed, shapes):
    flat = packed.reshape(-1)
    out, o = [], 0
    for s in shapes:
        n = math.prod(s)
        out.append(flat[o:o + n].reshape(s))
        o += n
    return out


def _pack_lead(arrs, dtype):
    flat = jnp.concatenate([a.reshape(N_DEV, -1).astype(dtype) for a in arrs], axis=1)
    n = flat.shape[1]
    pad = (-n) % _granule(n)
    if pad:
        flat = jnp.concatenate([flat, jnp.zeros((N_DEV, pad), dtype)], axis=1)
    return flat.reshape(N_DEV, -1, PACK_W)


def _to_full(stacked, axis):
    t = jnp.moveaxis(stacked, 0, axis)
    s = list(t.shape)
    return t.reshape(s[:axis] + [s[axis] * s[axis + 1]] + s[axis + 2:])


def _to_chunks(full, axis):
    s = list(full.shape)
    t = full.reshape(s[:axis] + [N_DEV, s[axis] // N_DEV] + s[axis + 1:])
    return jnp.moveaxis(t, axis, 0)


def _rows_of(S):
    return _tile(S, (512, 256, 128, 64))


def _norm_fwd(x, g, dt=BF):
    (h,), _ = _seq_fwd("rmsnorm_fwd", f_rmsnorm, _rows_of(x.shape[0]), [_row_spec(x)], [_par_spec(g)], [],
                       [_out_spec(x.shape[1], dt=dt)])
    return h


def _norm_bwd(x, g, dh, res=None):
    if res is None:
        (dx,), (dg,) = _seq_bwd("rmsnorm_bwd", f_rmsnorm, _rows_of(x.shape[0]), [_row_spec(x)], [_par_spec(g)], [],
                                [_row_spec(dh)], [])
        return dx, None, dg
    (dx,), (dg,), (dxb,) = _seq_bwd("rmsnorm_res_bwd", f_rmsnorm, _rows_of(x.shape[0]), [_row_spec(x)], [_par_spec(g)], [],
                                    [_row_spec(dh)], [], dx_add={0: res}, dx_bf=True)
    return dx, dxb, dg


def _mixer_specs(kind, S, p, w):
    if kind == 0:
        return (f_gla, min(S, MIX_ROWS), [_row_spec(p)],
                [_par_spec(w['a_w_gate2']), _par_spec(w['a_b_gate'].reshape(1, -1)), _par_spec(w['a_o_norm'].reshape(1, -1))],
                [(GLA_DV, GLA_DK)] * GLA_HEADS, D_MIX)
    if kind == 2:
        return (f_ssd, min(S, MIX_ROWS), [_row_spec(p, prev='halo')],
                [_par_spec(w['c_conv_w']), _par_spec(w['c_conv_b'].reshape(1, -1)), _par_spec(w['c_dt_bias'].reshape(1, -1)),
                 _par_spec(w['c_a_log'].reshape(1, -1)), _par_spec(w['c_d'].reshape(1, -1)),
                 _par_spec(w['c_norm'].reshape(1, -1))],
                [(SSM_STATE, 2 * SSM_HD)] * (SSM_HEADS // 2), D_MIX)
    return (f_hgrn, min(S, MIX_ROWS), [_row_spec(p)],
            [_par_spec(w['d_lower_bounds']), _par_spec(w['d_o_norm'].reshape(1, -1))],
            [(HGRN_DV, HGRN_DK)] * HGRN_HEADS, D_MIX)


def _perm(t, r):
    if r == 1:
        return t
    S, n = t.shape
    return t.reshape(S // r, r, n).transpose(1, 0, 2).reshape(S, n)


def _unperm(t, r):
    if r == 1:
        return t
    S, n = t.shape
    return t.reshape(r, S // r, n).transpose(1, 0, 2).reshape(S, n)


def _rope_consts():
    half = DIL_HD // 2
    inv = ROPE_THETA ** (-jnp.arange(half, dtype=F32) / half)
    invf = jnp.concatenate([inv, inv]).reshape(1, DIL_HD)
    sign = jnp.concatenate([-jnp.ones((half,), F32), jnp.ones((half,), F32)]).reshape(1, DIL_HD)
    return invf, sign


def _dil_fwd(p, pos, w, ncat):
    S = p.shape[0]
    invf, sign = _rope_consts()
    prep_rows = [_row_spec(p), _row_spec(pos, diff=False)]
    prep_pars = [_par_spec(w['b_q_norm'].reshape(1, -1)), _par_spec(w['b_k_norm'].reshape(1, -1)),
                 _par_spec(invf, diff=False), _par_spec(sign, diff=False)]
    ng = len(DIL_GROUPS)
    qkv, _ = _seq_fwd("dil_prep_fwd", f_dil_prep, _tile(S, (256, 128)), prep_rows, prep_pars, [],
                      [_out_spec(D_DIL) for _ in range(3 * ng)])
    res = dict(perm=[], o=[], lse=[])
    for g, (window, r) in enumerate(DIL_GROUPS):
        qp, kp, vp = _perm(qkv[g], r), _perm(qkv[ng + g], r), _perm(qkv[2 * ng + g], r)
        rows = [_row_spec(qp), _row_spec(kp, prev='block'), _row_spec(vp, prev='block')]
        (o, lse), _ = _seq_fwd("dil_attn_fwd", f_dil_attn, DIL_BLOCK, rows, [], [], [_out_spec(D_DIL), _out_spec(D_DIL)],
                               period=S // r // DIL_BLOCK)
        res['perm'].append((qp, kp, vp))
        res['o'].append(_unperm(o, r))
        res['lse'].append(_unperm(lse, r))
    mrows = [_row_spec(t) for t in res['o'] + res['lse']]
    (cat,), _ = _seq_fwd("dil_merge_fwd", f_dil_merge, _rows_of(S), mrows, [], [], [_out_spec(ncat, w=D_DIL, dt=BF)])
    res['prep'] = (prep_rows, prep_pars)
    return cat, res


def _dil_bwd(dtok, res, p, exchange=None, exchange_local=None):
    S = p.shape[0]
    mrows = [_row_spec(t) for t in res['o'] + res['lse']]
    dm, _ = _seq_bwd("dil_merge_bwd", f_dil_merge, _rows_of(S), mrows, [], [], [dtok], [])
    dq, dk, dv = [], [], []
    for g, (window, r) in enumerate(DIL_GROUPS):
        qp, kp, vp = res['perm'][g]
        rows = [_row_spec(qp), _row_spec(kp, prev='block'), _row_spec(vp, prev='block')]
        douts = [_row_spec(_perm(dm[g], r)), _row_spec(_perm(dm[3 + g], r))]
        (a, b, c), _ = _seq_bwd("dil_attn_bwd", f_dil_attn, DIL_BLOCK, rows, [], [], douts, [],
                                period=S // r // DIL_BLOCK)
        dq.append(_unperm(a, r)); dk.append(_unperm(b, r)); dv.append(_unperm(c, r))
    prep_rows, prep_pars = res['prep']
    res = _seq_bwd("dil_prep_bwd", f_dil_prep, _tile(S, (256, 128)), prep_rows, prep_pars, [],
                   [_row_spec(t) for t in dq + dk + dv], [], dx_dt=BF, exchange=exchange)
    (dp,), (dqn, dkn) = res[0], res[1]
    return dp, dict(b_q_norm=dqn.reshape(-1), b_k_norm=dkn.reshape(-1)), (res[2] if exchange else exchange_local)


def _ffn_specs(u, cw, cb):
    half = N_DEV // 2
    rows = [_row_spec(u, prev='halo', lb=(2, None), li=lambda jc: (0, jc))]
    pars = [_par_spec(cw, bs=(None, FFN_CONV, FF_SH), idx=lambda jc: (jc, 0, 0)),
            _par_spec(cw, bs=(None, FFN_CONV, FF_SH), idx=lambda jc: (jc + half, 0, 0)),
            _par_spec(cb, bs=(None, 1, FF_SH), idx=lambda jc: (jc, 0, 0)),
            _par_spec(cb, bs=(None, 1, FF_SH), idx=lambda jc: (jc + half, 0, 0))]
    return half, rows, pars


N_MIX = {0: 2 * GLA_HEADS * GLA_DK + 2 * D_MIX + GLA_RANK, 1: 3 * len(DIL_GROUPS) * D_DIL,
         2: 2 * D_MIX + 2 * SSM_GROUPS * SSM_STATE + SSM_HEADS, 3: 2 * HGRN_HEADS * HGRN_DK + 2 * D_MIX}
W_IN = {0: 'a_w_in', 1: 'b_w_in', 2: 'c_w_in', 3: 'd_w_in'}
W_OUT = {0: 'a_w_out', 1: 'b_w_out', 2: 'c_w_out', 3: 'd_w_out'}


def _in_blocks(name, t):
    return t if SHARD_AXIS[name] == 1 else t.reshape(1, N_DEV * t.shape[1], t.shape[2])


LAYER_STACKED = ('ffn_w_up', 'ffn_conv_w', 'ffn_w_down', 'xa_w_kv')


SMALL_OF_KIND = {0: ['a_w_gate2'], 2: ['c_conv_w']}


def _layer_names(i):
    return list(LAYER_STACKED) + [W_IN[i % 4], W_OUT[i % 4]] + SMALL_OF_KIND.get(i % 4, [])


def _device_step(x, mem, pos, sh, rep, target, distributed=True):
    S, D = x.shape
    w = dict(rep)
    posf = pos.reshape(S, 1).astype(F32)
    n_mix, w_in_name, w_out_name = N_MIX, W_IN, W_OUT
    ntot = {k: -(-(n_mix[k] + D_XA) // 256) * 256 for k in n_mix}
    mem_g = w['mem_norm'].reshape(1, -1)
    mem_n = _norm_fwd(mem, mem_g)
    R = _rows_of(S)

    def mine(i):
        return {n: (sh[n][i] if n in LAYER_STACKED else sh[n]) for n in _layer_names(i)}

    if distributed:
        gl = dict(zip(_layer_names(0), _gather_many(list(mine(0).values()), "gather_weights")))
    else:
        gl = {n: (sh[n][:, 0] if n in LAYER_STACKED else sh[n]) for n in _layer_names(0)}

    saved = []
    for i in range(DEPTH):
        kind = i % 4
        L = dict(x0=x)
        for n in SMALL_OF_KIND.get(kind, []):
            w[n] = _to_full(gl[n], 1)
        in_blocks = _in_blocks(w_in_name[kind], gl[w_in_name[kind]])
        w_out = (_to_full(gl[w_out_name[kind]], 1) if SHARD_AXIS[w_out_name[kind]] == 1
                 else gl[w_out_name[kind]].reshape(-1, D))
        nxt, push = {}, [[], [], []]
        if i + 1 < DEPTH:
            if distributed:
                nxt = mine(i + 1)
                push = [[n for n in nxt if n not in ('ffn_w_up', w_in_name[(i + 1) % 4], w_out_name[(i + 1) % 4])],
                        ['ffn_w_up'], [w_in_name[(i + 1) % 4], w_out_name[(i + 1) % 4]]]
            else:
                nxt = {n: (sh[n][:, i + 1] if n in LAYER_STACKED else sh[n]) for n in _layer_names(i + 1)}
        got = dict(nxt) if not distributed else {}

        def hosted(call, names):
            if not names:
                return call()
            res, arrived = call(gather=[nxt[n] for n in names])
            got.update(zip(names, arrived))
            return res

        g1 = w['mix_norm'][i].reshape(1, -1)
        h = _norm_fwd(x, g1)
        wcat = _cat_cols(in_blocks, n_mix[kind], ntot[kind])
        p = hosted(functools.partial(_matmul, h, wcat, name="matmul_in"), push[0])
        ntok = D_DIL if kind == 1 else D_MIX
        if kind == 1:
            cat, L['dil'] = _dil_fwd(p, posf, w, ntok + D_XA)
        else:
            f, Rm, rows, pars, sshapes, _ = _mixer_specs(kind, S, p, w)
            (cat,), L['states'] = _seq_fwd("mixer%d_fwd" % kind, f, Rm, rows, pars, sshapes,
                                           [_out_spec(ntok + D_XA, w=ntok, dt=BF)], save_states=True)
        wkv = gl['xa_w_kv'].reshape(D, 2 * D_XA)
        kv = _matmul(mem_n, wkv, name="matmul_kv")
        xa_rows = [_row_spec(p, w=D_XA, dn=D_XA)]
        xa_pars = [_par_spec(kv), _par_spec(w['xa_q_norm'][i].reshape(1, -1)), _par_spec(w['xa_k_norm'][i].reshape(1, -1))]
        (cat,), _ = _seq_fwd("xattn_fwd", f_xattn, R, xa_rows, xa_pars, [],
                             [_out_spec(ntok + D_XA, w=D_XA, c=lambda jc: ntok // D_XA, dt=BF)], out_alias={0: cat})
        x1 = _matmul(cat, w_out, add=x, name="matmul_out")
        g2 = w['ffn_norm'][i].reshape(1, -1)
        h2 = _norm_fwd(x1, g2)
        wup = gl['ffn_w_up']
        u = hosted(functools.partial(_ffn_up, h2, wup), push[1])
        cw, cb = gl['ffn_conv_w'], w['ffn_conv_b'][i].reshape(N_DEV, 1, FF_SH)
        nt, frows, fpars = _ffn_specs(u, cw, cb)
        res = _seq_fwd("ffn_act_fwd", f_ffn_act, R, frows, fpars, [],
                       [_out_spec(FF_SH, dt=BF, ls=(nt,), lb=(None,), li=lambda jc: (jc,))], ncol=nt,
                       gather=[nxt[n] for n in push[2]] if push[2] else None)
        (act,) = res[0]
        if push[2]:
            got.update(zip(push[2], res[2]))
        wd = gl['ffn_w_down'].reshape(D_FF, D)
        x = _ffn_down(act, wd, x1)
        L.update(h=h, p=p, wcat=wcat, kv=kv, wkv=wkv, cat=cat, x1=x1, h2=h2, u=u, act=act, wd=wd, wup=wup, cw=cw, g1=g1,
                 g2=g2, in_blocks=in_blocks, w_out=w_out, shapes={n: t.shape for n, t in gl.items()})
        saved.append(L)
        gl = got

    dx, dxb, loss = _loss_head(x, target)

    G = {}
    d_mem_n = None
    acc = {k: [None] * DEPTH for k in ('mix_norm', 'ffn_norm', 'ffn_conv_b', 'xa_q_norm', 'xa_k_norm')}
    parts = [{} for _ in range(DEPTH)]
    pending = {}
    half = N_DEV // 2

    def sent(call, blocks, layer):
        if not blocks:
            return call()
        if not distributed:
            parts[layer].update(blocks)
            return call()
        res, arrived = call(exchange=list(blocks.values()))
        parts[layer].update(zip(blocks, arrived))
        return res

    for i in reversed(range(DEPTH)):
        kind = i % 4
        L = saved[i]
        Gc = {}
        Gc['ffn_w_down'] = _ffn_dw_down(L['act'], dxb).reshape(N_DEV, D_FF // N_DEV, D)
        dact = _ffn_dact(dxb, L['wd'])
        cw, cb = L['cw'], w['ffn_conv_b'][i].reshape(N_DEV, 1, FF_SH)
        nt, frows, fpars = _ffn_specs(L['u'], cw, cb)
        (du,), (dwg, dwv, dbg, dbv) = _seq_bwd(
            "ffn_act_bwd", f_ffn_act, R, frows, fpars, [], [_row_spec(dact, lb=(None,), li=lambda jc: (jc,))], [],
            ncol=nt, dx_dt=BF)
        Gc['ffn_conv_w'] = jnp.concatenate([dwg[:half], dwv[half:]], axis=0)
        acc['ffn_conv_b'][i] = jnp.concatenate([dbg[:half], dbv[half:]], axis=0).reshape(-1)
        Gc['ffn_w_up'] = _ffn_dw_up(L['h2'], du)
        dh2 = sent(functools.partial(_ffn_dh2, du, L['wup']), pending, i + 1)
        dx1, dx1b, dg2 = _norm_bwd(L['x1'], L['g2'], dh2, res=dx)
        acc['ffn_norm'][i] = dg2.reshape(-1)
        G_out = _matmul(L['cat'], dx1b, mode="tn", out_dtype=BF, name="matmul_dw_out")
        dcat = _matmul(dx1b, L['w_out'], mode="nt", name="matmul_dcat")
        ntok = D_DIL if kind == 1 else D_MIX
        dtok = _row_spec(dcat, w=ntok)
        dxa = _row_spec(dcat, w=D_XA, c=lambda jc: ntok // D_XA)
        p = L['p']
        up = [Gc.pop('ffn_w_up')]
        if kind == 1:
            dp, gm, got_up = _dil_bwd(dtok, L['dil'], p, up if distributed else None, up)
            G.update(gm)
        else:
            f, Rm, rows, pars, sshapes, _ = _mixer_specs(kind, S, p, w)
            res = _seq_bwd("mixer%d_bwd" % kind, f, Rm, rows, pars, sshapes, [dtok], L['states'], dx_dt=BF,
                           exchange=up if distributed else None)
            (dp,), dps, got_up = res[0], res[1], (res[2] if distributed else up)
            if kind == 0:
                Gc['a_w_gate2'], G['a_b_gate'], G['a_o_norm'] = _to_chunks(dps[0], 1), dps[1].reshape(-1), dps[2].reshape(-1)
            elif kind == 2:
                Gc['c_conv_w'] = _to_chunks(dps[0], 1)
                for nme, v in zip(('c_conv_b', 'c_dt_bias', 'c_a_log', 'c_d', 'c_norm'), dps[1:]):
                    G[nme] = v.reshape(-1)
            else:
                G['d_lower_bounds'], G['d_o_norm'] = dps[0], dps[1].reshape(-1)
        xa_rows = [_row_spec(p, w=D_XA)]
        xa_pars = [_par_spec(L['kv']), _par_spec(w['xa_q_norm'][i].reshape(1, -1)), _par_spec(w['xa_k_norm'][i].reshape(1, -1))]
        (dp,), (dkv, dqn, dkn) = _seq_bwd("xattn_bwd", f_xattn, R, xa_rows, xa_pars, [], [dxa], [], dx_dt=BF,
                                          dx_alias={0: dp})
        acc['xa_q_norm'][i], acc['xa_k_norm'][i] = dqn.reshape(-1), dkn.reshape(-1)
        Gc['xa_w_kv'] = _matmul(mem_n, dkv, mode="tn", out_dtype=BF, name="matmul_dw_kv").reshape(
            N_DEV, D // N_DEV, 2 * D_XA)
        d_mem_n = _matmul(dkv, L['wkv'], mode="nt", add=d_mem_n, name="matmul_dmem" + ("" if d_mem_n is None else "_acc"))
        parts[i]['ffn_w_up'] = got_up[0]
        dwcat = _matmul(L['h'], dp, mode="tn", out_dtype=BF, name="matmul_dw_in")
        blocks = L['in_blocks']
        Gc[w_in_name[kind]] = _uncat_cols(dwcat, blocks.shape[0], blocks.shape[2], n_mix[kind]).reshape(
            L['shapes'][w_in_name[kind]])
        Gc[w_out_name[kind]] = (_to_chunks(G_out, 1) if SHARD_AXIS[w_out_name[kind]] == 1
                                else G_out.reshape(L['shapes'][w_out_name[kind]]))
        dh = sent(functools.partial(_matmul, dp, L['wcat'], mode="nt", name="matmul_dh"),
                  {n: Gc.pop(n) for n in ('ffn_w_down', 'ffn_conv_w')}, i)
        dx, dxb, dg1 = _norm_bwd(L['x0'], L['g1'], dh, res=dx1)
        acc['mix_norm'][i] = dg1.reshape(-1)
        pending = Gc

    if distributed:
        names = list(pending)
        parts[0].update(zip(names, _exchange_many([pending[n] for n in names], "exchange_grads")))
    else:
        parts[0].update(pending)
    _, _, dmg = _norm_bwd(mem, mem_g, d_mem_n)
    G['mem_norm'] = dmg.reshape(-1)
    for k, v in acc.items():
        G[k] = jnp.stack(v)
    got = {}
    for i in range(DEPTH):
        for n, t in parts[i].items():
            if n not in LAYER_STACKED:
                got[n] = t
    for n in LAYER_STACKED:
        got[n] = jnp.stack([parts[i][n] for i in range(DEPTH)], axis=1)
    return loss, dx, got, G


def kernel(x, mem, positions, mem_norm, mix_norm, xa_w_kv, xa_q_norm, xa_k_norm, ffn_norm, ffn_w_up, ffn_conv_w, ffn_conv_b, ffn_w_down, a_w_in, a_w_gate2, a_b_gate, a_o_norm, a_w_out, b_w_in, b_q_norm, b_k_norm, b_w_out, c_w_in, c_conv_w, c_conv_b, c_dt_bias, c_a_log, c_d, c_norm, c_w_out, d_w_in, d_lower_bounds, d_o_norm, d_w_out, loss_target, m_mem_norm, m_mix_norm, m_xa_w_kv, m_xa_q_norm, m_xa_k_norm, m_ffn_norm, m_ffn_w_up, m_ffn_conv_w, m_ffn_conv_b, m_ffn_w_down, m_a_w_in, m_a_w_gate2, m_a_b_gate, m_a_o_norm, m_a_w_out, m_b_w_in, m_b_q_norm, m_b_k_norm, m_b_w_out, m_c_w_in, m_c_conv_w, m_c_conv_b, m_c_dt_bias, m_c_a_log, m_c_d, m_c_norm, m_c_w_out, m_d_w_in, m_d_lower_bounds, m_d_o_norm, m_d_w_out, v_mem_norm, v_mix_norm, v_xa_w_kv, v_xa_q_norm, v_xa_k_norm, v_ffn_norm, v_ffn_w_up, v_ffn_conv_w, v_ffn_conv_b, v_ffn_w_down, v_a_w_in, v_a_w_gate2, v_a_b_gate, v_a_o_norm, v_a_w_out, v_b_w_in, v_b_q_norm, v_b_k_norm, v_b_w_out, v_c_w_in, v_c_conv_w, v_c_conv_b, v_c_dt_bias, v_c_a_log, v_c_d, v_c_norm, v_c_w_out, v_d_w_in, v_d_lower_bounds, v_d_o_norm, v_d_w_out):
    args = locals()
    w = {n: args[n] for n in WEIGHTS}
    m = {n: args['m_' + n] for n in WEIGHTS}
    v = {n: args['v_' + n] for n in WEIGHTS}

    big = [n for n in SHARDED if w[n].size >= 65536]
    small = [n for n in SHARDED if n not in big]
    sh = {n: (w[n].astype(BF) if n in big else w[n]) for n in SHARDED}
    loss, grad_x, parts, G = _device_step(x[0], mem[0], positions[0], sh, {n: w[n] for n in REPLICATED}, loss_target[0])
    loss = lax.psum(loss, ("x", "y", "c"))
    rep_parts = _all_gather(_pack([G[n] for n in REPLICATED], F32), "gather_replicated_grads")

    out = {}

    def put(names, res, shapes):
        for kind, r in zip(("grad", "delta", "new_m", "new_v"), res):
            for n, t in zip(names, _unpack(r, shapes)):
                out[kind + "_" + n] = t

    for n in big:
        shp = tuple(w[n].shape)
        two_d = (math.prod(shp[:-1]), shp[-1])
        res = _adamw(parts[n].reshape((N_DEV,) + two_d), w[n].reshape(two_d), m[n].reshape(two_d), v[n].reshape(two_d),
                     "adamw")
        for kind, r in zip(("grad", "delta", "new_m", "new_v"), res):
            out[kind + "_" + n] = r.reshape(shp)
    for names, prt, tag in ((small, _pack_lead([parts[n] for n in small], F32), "adamw_small"),
                            (REPLICATED, rep_parts, "adamw_replicated")):
        res = _adamw(prt, _pack([w[n] for n in names], F32), _pack([m[n] for n in names], F32),
                     _pack([v[n] for n in names], F32), tag)
        put(names, res, [tuple(w[n].shape) for n in names])
    return (loss, grad_x[None], *[out["grad_" + n] for n in WEIGHTS], *[out["delta_" + n] for n in WEIGHTS],
            *[out["new_m_" + n] for n in WEIGHTS], *[out["new_v_" + n] for n in WEIGHTS])
```

```python
import functools
import math

import jax
import jax.numpy as jnp
from jax import lax
from jax.experimental import pallas as pl
from jax.experimental.pallas import tpu as pltpu

F32 = jnp.float32
BF = jnp.bfloat16
_MM_DTYPE = BF

N_DEV = 8
EPS = 1e-6
ROPE_THETA = 10000.0
CHUNK = 64
MIX_ROWS = 256
DIL_ROWS = 256
D_MIX = 768
XA_HEADS, XA_HD, D_XA = 4, 64, 256
GLA_HEADS, GLA_DK, GLA_DV, GLA_RANK, GLA_GATE_NORM = 4, 96, 192, 16, 16.0
DIL_GROUPS = ((128, 1), (512, 4), (2048, 16))
DIL_HEADS, DIL_HD, DIL_BLOCK, D_DIL = 4, 128, 128, 512
SSM_HD, SSM_HEADS, SSM_GROUPS, SSM_STATE, SSM_CONV = 64, 12, 2, 128, 4
HGRN_HEADS, HGRN_DK, HGRN_DV = 6, 128, 128
D_FF = 2816
FFN_CONV = 3
DEPTH = 4
ADAM_LR, ADAM_B1, ADAM_B2, ADAM_EPS, ADAM_WD, ADAM_STEP = 0.001, 0.9, 0.999, 1e-08, 0.01, 10
NEG = -1e30
HALO = 8
VMEM_LIMIT = 56 << 20
ADAM_BLOCK = 1 << 18

WEIGHTS = ['mem_norm', 'mix_norm', 'xa_w_kv', 'xa_q_norm', 'xa_k_norm', 'ffn_norm', 'ffn_w_up', 'ffn_conv_w',
           'ffn_conv_b', 'ffn_w_down', 'a_w_in', 'a_w_gate2', 'a_b_gate', 'a_o_norm', 'a_w_out', 'b_w_in', 'b_q_norm',
           'b_k_norm', 'b_w_out', 'c_w_in', 'c_conv_w', 'c_conv_b', 'c_dt_bias', 'c_a_log', 'c_d', 'c_norm', 'c_w_out',
           'd_w_in', 'd_lower_bounds', 'd_o_norm', 'd_w_out']
SHARD_AXIS = {'xa_w_kv': 1, 'ffn_w_up': 2, 'ffn_conv_w': 2, 'ffn_w_down': 1, 'a_w_in': 1, 'a_w_gate2': 1, 'a_w_out': 0,
              'b_w_in': 1, 'b_w_out': 1, 'c_w_in': 0, 'c_conv_w': 1, 'c_w_out': 0, 'd_w_in': 1, 'd_w_out': 0}
SHARDED = [n for n in WEIGHTS if n in SHARD_AXIS]
REPLICATED = [n for n in WEIGHTS if n not in SHARD_AXIS]


def _dot(a, b, ca, cb):
    return lax.dot_general(a.astype(_MM_DTYPE), b.astype(_MM_DTYPE), (((ca,), (cb,)), ((), ())),
                           preferred_element_type=F32)


@jax.custom_vjp
def mm_nn(a, b):
    return _dot(a, b, 1, 0)


mm_nn.defvjp(lambda a, b: (_dot(a, b, 1, 0), (a, b)),
             lambda r, g: (_dot(g, r[1], 1, 1), _dot(r[0], g, 0, 0)))


@jax.custom_vjp
def mm_nt(a, b):
    return _dot(a, b, 1, 1)


mm_nt.defvjp(lambda a, b: (_dot(a, b, 1, 1), (a, b)),
             lambda r, g: (_dot(g, r[1], 1, 0), _dot(g, r[0], 0, 0)))


@jax.custom_vjp
def mm_tn(a, b):
    return _dot(a, b, 0, 0)


mm_tn.defvjp(lambda a, b: (_dot(a, b, 0, 0), (a, b)),
             lambda r, g: (_dot(r[1], g, 1, 1), _dot(r[0], g, 1, 0)))


def _dot_hi(a, b, ca, cb):
    return lax.dot_general(a, b, (((ca,), (cb,)), ((), ())), precision=lax.Precision.HIGHEST,
                           preferred_element_type=F32)


def _tril(c):
    return lax.broadcasted_iota(jnp.int32, (c, c), 0) >= lax.broadcasted_iota(jnp.int32, (c, c), 1)


@jax.custom_vjp
def cumsum_rows(x):
    return _dot_hi(_tril(x.shape[0]).astype(F32), x, 1, 0)


cumsum_rows.defvjp(lambda x: (cumsum_rows(x), None),
                   lambda r, g: (_dot_hi(_tril(g.shape[0]).astype(F32), g, 0, 0),))


@jax.custom_vjp
def cumsum_rows_t(x):
    return _dot_hi(x, _tril(x.shape[0]).astype(F32), 0, 1)


cumsum_rows_t.defvjp(lambda x: (cumsum_rows_t(x), None),
                     lambda r, g: (_dot_hi(_tril(g.shape[1]).astype(F32), g, 0, 1),))


def _split(x, sizes):
    sizes = tuple(int(s) for s in sizes)
    assert sum(sizes) == x.shape[-1], (sizes, x.shape)

    @jax.custom_vjp
    def sp(x):
        out, o = [], 0
        for s in sizes:
            out.append(x[:, o:o + s])
            o += s
        return tuple(out)

    sp.defvjp(lambda x: (sp(x), None), lambda r, g: (jnp.concatenate(list(g), axis=1),))
    return sp(x)


def _row(x, r):
    m = lax.broadcasted_iota(jnp.int32, x.shape, 0) == r
    return jnp.sum(jnp.where(m, x, 0.0), axis=0, keepdims=True)


@jax.custom_vjp
def _roll_half(x):
    return pltpu.roll(x, 64, 1)


_roll_half.defvjp(lambda x: (pltpu.roll(x, 64, 1), None), lambda r, g: (pltpu.roll(g, 64, 1),))


def _shift(xp, x, d):
    if d == 0:
        return x
    n, m = x.shape[0], xp.shape[0]
    assert d <= m == HALO

    @jax.custom_vjp
    def sh(xp, x):
        r = pltpu.roll(x, d, 0)
        row = lax.broadcasted_iota(jnp.int32, xp.shape, 0)
        head = jnp.where(row < d, pltpu.roll(xp, d, 0), r[:m])
        return jnp.concatenate([head, r[m:]], axis=0)

    def bwd(_, g):
        row = lax.broadcasted_iota(jnp.int32, g.shape, 0)
        rowp = lax.broadcasted_iota(jnp.int32, (m,) + g.shape[1:], 0)
        dxp = jnp.where(rowp >= m - d, pltpu.roll(g[:m], m - d, 0), 0.0)
        return dxp, jnp.where(row < n - d, pltpu.roll(g, n - d, 0), 0.0)

    sh.defvjp(lambda xp, x: (sh(xp, x), None), bwd)
    return sh(xp, x)


def _rms(x, g):
    return x * lax.rsqrt(jnp.mean(x * x, axis=-1, keepdims=True) + EPS) * g


def _lane_pair(a, b, width=128):
    shape = a.shape[:-1] + (width,)
    lane = lax.broadcasted_iota(jnp.int32, shape, len(shape) - 1)
    return jnp.where(lane < width // 2, a, b)


def _row_spec(a, w=None, c=None, prev=False, diff=True, dn=None, lb=(), li=None):
    return dict(a=a, w=a.shape[-1] if w is None else w, c=(lambda jc: 0) if c is None else c, prev=prev, diff=diff,
                dn=a.shape[-1] if dn is None else dn, lb=tuple(lb), li=(lambda jc: ()) if li is None else li)


def _par_spec(a, bs=None, idx=None, diff=True):
    nd = a.ndim
    return dict(a=a, bs=tuple(a.shape) if bs is None else tuple(bs),
                idx=(lambda jc: (0,) * nd) if idx is None else idx, diff=diff)


def _out_spec(n, w=None, c=None, dt=F32, ls=(), lb=(), li=None):
    return dict(n=n, w=n if w is None else w, c=(lambda jc: 0) if c is None else c, dt=dt, ls=tuple(ls), lb=tuple(lb),
                li=(lambda jc: ()) if li is None else li)


def _cparams():
    return pltpu.CompilerParams(dimension_semantics=("arbitrary", "arbitrary"), vmem_limit_bytes=VMEM_LIMIT)


def _bspec(s, R, rowfn):
    return pl.BlockSpec(s['lb'] + (R, s['w']),
                        functools.partial(lambda jc, i, s: tuple(s['li'](jc)) + (rowfn(i), s['c'](jc)), s=s))


def _prev_rows(s, R):
    return R if s['prev'] == 'block' else HALO


def _pspec(s, R, blockfn):
    pr = _prev_rows(s, R)
    return pl.BlockSpec(s['lb'] + (pr, s['w']), functools.partial(
        lambda jc, i, s: tuple(s['li'](jc)) + (jnp.maximum(blockfn(i) * (R // pr) - 1, 0), s['c'](jc)), s=s))


def _seq_fwd(name, f, R, rows, params, state_shapes, outs, *, ncol=1, period=None, save_states=False, out_alias=None,
             gather=None):
    nrows = rows[0]['a'].shape[-2]
    nb = nrows // R
    assert nb * R == nrows
    period = nb if period is None else period
    prev_ids = [k for k, r in enumerate(rows) if r['prev']]
    n_rows, n_prev, n_par, n_out, n_st = len(rows), len(prev_ids), len(params), len(outs), len(state_shapes)
    ex = list(gather or [])
    n_ex = len(ex)

    def body(*refs):
        o = 0
        cur = refs[o:o + n_rows]; o += n_rows
        prv = refs[o:o + n_prev]; o += n_prev
        par = refs[o:o + n_par]; o += n_par + len(out_alias or {})
        exg = refs[o:o + n_ex]; o += n_ex
        out = refs[o:o + n_out]; o += n_out
        sav = refs[o:o + (n_st if save_states else 0)]; o += len(sav)
        exr = refs[o:o + n_ex]; o += n_ex
        st = refs[o:o + n_st]; o += n_st
        sems = refs[o:]
        i = pl.program_id(1)
        first = (i % period) == 0
        if n_ex:
            @pl.when((pl.program_id(0) == 0) & (i == 0))
            def _():
                _exchange_start(exg, exr, *sems)

        @pl.when(i == 0)
        def _():
            for s in st:
                s[...] = jnp.zeros_like(s)

        xs = [r[...].astype(F32) for r in cur]
        xp = [r[...].astype(F32) for r in prv]
        ps = [r[...] for r in par]
        sts = [s[...] for s in st]
        for sv, s in zip(sav, sts):
            sv[0] = s
        ov, ns = f(first, xp, xs, ps, sts)
        for r, v in zip(out, ov):
            r[...] = v.astype(r.dtype)
        for s, v in zip(st, ns):
            s[...] = v
        if n_ex:
            @pl.when((pl.program_id(0) == ncol - 1) & (i == nb - 1))
            def _():
                _exchange_wait(exg, exr, *sems)

    in_specs = [_bspec(r, R, lambda i: i) for r in rows]
    in_specs += [_pspec(rows[k], R, lambda i: i) for k in prev_ids]
    in_specs += [pl.BlockSpec(p['bs'], functools.partial(lambda jc, i, idx: idx(jc), idx=p['idx'])) for p in params]
    out_specs = [_bspec(o_, R, lambda i: i) for o_ in outs]
    out_shape = [jax.ShapeDtypeStruct(o_['ls'] + (nrows, o_['n']), o_['dt']) for o_ in outs]
    if save_states:
        for s in state_shapes:
            out_specs.append(pl.BlockSpec((1,) + tuple(s), lambda jc, i, nd=len(s): (i,) + (0,) * nd))
            out_shape.append(jax.ShapeDtypeStruct((nb,) + tuple(s), F32))
    args = [r['a'] for r in rows] + [rows[k]['a'] for k in prev_ids] + [p['a'] for p in params]
    aliases = {}
    for n_, arr in sorted((out_alias or {}).items()):
        assert arr.shape == out_shape[n_].shape and arr.dtype == out_shape[n_].dtype
        aliases[len(args)] = n_
        args.append(arr)
        in_specs.append(pl.BlockSpec(memory_space=pl.ANY))
    n_sav = len(out_shape) - n_out
    in_specs += [pl.BlockSpec(memory_space=pl.ANY)] * n_ex
    out_specs += [pl.BlockSpec(memory_space=pl.ANY)] * n_ex
    out_shape += [jax.ShapeDtypeStruct((N_DEV,) + tuple(g.shape), g.dtype) for g in ex]
    cp = pltpu.CompilerParams(dimension_semantics=("arbitrary", "arbitrary"), vmem_limit_bytes=VMEM_LIMIT,
                              has_side_effects=bool(n_ex))
    res = pl.pallas_call(
        body, name=name, grid=(ncol, nb), in_specs=in_specs, out_specs=out_specs, out_shape=out_shape,
        scratch_shapes=[pltpu.VMEM(tuple(s), F32) for s in state_shapes] + (_exchange_sems(n_ex) if n_ex else []),
        input_output_aliases=aliases, compiler_params=cp)(*args, *ex)
    if n_ex:
        return list(res[:n_out]), list(res[n_out:n_out + n_sav]), list(res[n_out + n_sav:])
    return list(res[:n_out]), list(res[n_out:])


def _seq_bwd(name, f, R, rows, params, state_shapes, douts, saved, *, ncol=1, period=None, dx_dt=F32, dx_add=None,
             dx_bf=False, dx_alias=None, exchange=None):
    ex = list(exchange or [])
    n_ex = len(ex)
    nrows = rows[0]['a'].shape[-2]
    nb = nrows // R
    period = nb if period is None else period
    prev_ids = [k for k, r in enumerate(rows) if r['prev']]
    drow_ids = [k for k, r in enumerate(rows) if r['diff']]
    dpar_ids = [k for k, p in enumerate(params) if p['diff']]
    for k in prev_ids:
        assert rows[k]['diff']
    dx_add, dx_alias = dict(dx_add or {}), dict(dx_alias or {})
    add_ids, alias_ids = sorted(dx_add), sorted(dx_alias)
    n_rows, n_prev, n_par, n_do, n_st = len(rows), len(prev_ids), len(params), len(douts), len(state_shapes)
    n_dx, n_dp, n_add, n_al = len(drow_ids), len(dpar_ids), len(add_ids), len(alias_ids)

    def body(*refs):
        o = 0
        cur = refs[o:o + n_rows]; o += n_rows
        prv = refs[o:o + n_prev]; o += n_prev
        par = refs[o:o + n_par]; o += n_par
        sav = refs[o:o + n_st]; o += n_st
        dou = refs[o:o + n_do]; o += n_do
        adr = refs[o:o + n_add]; o += n_add
        o += n_al
        exg = refs[o:o + n_ex]; o += n_ex
        dxr = refs[o:o + n_dx]; o += n_dx
        dpr = refs[o:o + n_dp]; o += n_dp
        dxb = refs[o:o + (n_dx if dx_bf else 0)]; o += len(dxb)
        exr = refs[o:o + n_ex]; o += n_ex
        dst = refs[o:o + n_st]; o += n_st
        car = refs[o:o + n_prev]; o += n_prev
        sems = refs[o:]
        j = pl.program_id(1)
        i = nb - 1 - j
        first = (i % period) == 0
        if n_ex:
            @pl.when((pl.program_id(0) == 0) & (j == 0))
            def _():
                _exchange_start(exg, exr, *sems)

        @pl.when(j == 0)
        def _():
            for s in tuple(dst) + tuple(car) + tuple(dpr):
                s[...] = jnp.zeros_like(s)

        xs = [r[...].astype(F32) for r in cur]
        xp = [r[...].astype(F32) for r in prv]
        ps = [r[...] for r in par]
        sts = [s[0] for s in sav]

        def g(dxs, dxp, dps, dsts):
            xs_, ps_ = list(xs), list(ps)
            for k, v in zip(drow_ids, dxs):
                xs_[k] = v
            for k, v in zip(dpar_ids, dps):
                ps_[k] = v
            ov, ns = f(first, list(dxp), xs_, ps_, list(dsts))
            return tuple(ov), tuple(ns)

        _, vjp = jax.vjp(g, tuple(xs[k] for k in drow_ids), tuple(xp), tuple(ps[k] for k in dpar_ids), tuple(sts))
        dxs, dxp, dps, dsts = vjp((tuple(r[...].astype(F32) for r in dou), tuple(s[...] for s in dst)))
        dxs = list(dxs)
        for n_, pos in enumerate(add_ids):
            dxs[pos] = dxs[pos] + adr[n_][...].astype(F32)
        tails = {}
        for n_, k in enumerate(prev_ids):
            pos = drow_ids.index(k)
            if rows[k]['prev'] == 'block':
                dxs[pos] = dxs[pos] + car[n_][...]
            else:
                tails[pos] = car[n_][...]
            car[n_][...] = dxp[n_]
        for pos, v in enumerate(dxs):
            outs_ = [dxr[pos]] + ([dxb[pos]] if dx_bf else [])
            if pos in tails:
                v = jnp.concatenate([v[..., :R - HALO, :], v[..., R - HALO:, :] + tails[pos]], axis=-2)
            for r in outs_:
                r[...] = v.astype(r.dtype)
        for r, v in zip(dpr, dps):
            r[...] += v
        for s, v in zip(dst, dsts):
            s[...] = v
        if n_ex:
            @pl.when((pl.program_id(0) == ncol - 1) & (j == nb - 1))
            def _():
                _exchange_wait(exg, exr, *sems)

    def rev(j):
        return nb - 1 - j

    def dspec(k):
        return _bspec(rows[k], R, rev)

    in_specs = [_bspec(r, R, rev) for r in rows]
    in_specs += [_pspec(rows[k], R, rev) for k in prev_ids]
    in_specs += [pl.BlockSpec(p['bs'], functools.partial(lambda jc, j, idx: idx(jc), idx=p['idx'])) for p in params]
    in_specs += [pl.BlockSpec((1,) + tuple(s), lambda jc, j, nd=len(s): (nb - 1 - j,) + (0,) * nd) for s in state_shapes]
    in_specs += [_bspec(d, R, rev) for d in douts]
    in_specs += [dspec(drow_ids[pos]) for pos in add_ids]
    in_specs += [pl.BlockSpec(memory_space=pl.ANY) for _ in alias_ids + ex]
    out_specs = [dspec(k) for k in drow_ids]
    out_shape = [jax.ShapeDtypeStruct(tuple(rows[k]['a'].shape[:-1]) + (rows[k]['dn'],), dx_dt) for k in drow_ids]
    for k in dpar_ids:
        p = params[k]
        out_specs.append(pl.BlockSpec(p['bs'], functools.partial(lambda jc, j, idx: idx(jc), idx=p['idx'])))
        out_shape.append(jax.ShapeDtypeStruct(p['a'].shape, F32))
    if dx_bf:
        out_specs += [dspec(k) for k in drow_ids]
        out_shape += [jax.ShapeDtypeStruct(tuple(rows[k]['a'].shape[:-1]) + (rows[k]['dn'],), BF) for k in drow_ids]
    out_specs += [pl.BlockSpec(memory_space=pl.ANY) for _ in ex]
    out_shape += [jax.ShapeDtypeStruct(g.shape, g.dtype) for g in ex]
    scratch = [pltpu.VMEM(tuple(s), F32) for s in state_shapes]
    scratch += [pltpu.VMEM(tuple(d for d in rows[k]['lb'] if d is not None) + (_prev_rows(rows[k], R), rows[k]['w']), F32)
                for k in prev_ids]
    if n_ex:
        scratch += _exchange_sems(n_ex)
    args = ([r['a'] for r in rows] + [rows[k]['a'] for k in prev_ids] + [p['a'] for p in params] + list(saved)
            + [d['a'] for d in douts] + [dx_add[pos] for pos in add_ids] + [dx_alias[pos] for pos in alias_ids])
    n_in = len(args)
    aliases = {n_in - n_al + n_: pos for n_, pos in enumerate(alias_ids)}
    for pos in alias_ids:
        assert dx_alias[pos].shape == out_shape[pos].shape and dx_alias[pos].dtype == out_shape[pos].dtype
    cp = pltpu.CompilerParams(dimension_semantics=("arbitrary", "arbitrary"), vmem_limit_bytes=VMEM_LIMIT,
                              has_side_effects=bool(n_ex))
    res = pl.pallas_call(
        body, name=name, grid=(ncol, nb), in_specs=in_specs, out_specs=out_specs, out_shape=out_shape,
        scratch_shapes=scratch, input_output_aliases=aliases, compiler_params=cp)(*args, *ex)
    lists = [list(res[:n_dx]), list(res[n_dx:n_dx + n_dp])]
    o = n_dx + n_dp
    if dx_bf:
        lists.append(list(res[o:o + n_dx]))
        o += n_dx
    if n_ex:
        lists.append(list(res[o:o + n_ex]))
    return tuple(lists)


def _tile(n, cands):
    for c in cands:
        if n % c == 0:
            return c
    return n


def _mm_call(name, grid, a, a_spec, b, b_spec, contract, out_shape, out_spec, acc_shape, add=None, add_spec=None,
             exchange=None, gather=None):
    nk = grid[2]
    ca, cb = contract
    has_add = add is not None
    ex = list(exchange or []) + list(gather or [])
    ex_shapes = [g.shape for g in exchange or []] + [(N_DEV,) + tuple(g.shape) for g in gather or []]
    n_ex = len(ex)
    n_in = 2 + has_add

    def body(*refs):
        a_ref, b_ref = refs[0], refs[1]
        add_ref = refs[2] if has_add else None
        o_ref = refs[n_in + n_ex]
        scr = refs[n_in + 2 * n_ex + 1:]
        step = [pl.program_id(d) for d in range(3)]
        if n_ex:
            g_refs, r_refs, sems = refs[n_in:n_in + n_ex], refs[n_in + n_ex + 1:n_in + 2 * n_ex + 1], scr[-3:]

            @pl.when((step[0] == 0) & (step[1] == 0) & (step[2] == 0))
            def _():
                _exchange_start(g_refs, r_refs, *sems)

        part = _dot(a_ref[...], b_ref[...], ca, cb)

        def finish(r):
            if has_add:
                r = r + add_ref[...].astype(F32)
            o_ref[...] = r.astype(o_ref.dtype)

        if nk == 1:
            finish(part)
        else:
            acc = scr[0]

            @pl.when(step[2] == 0)
            def _():
                acc[...] = part

            @pl.when(step[2] > 0)
            def _():
                acc[...] += part

            @pl.when(step[2] == nk - 1)
            def _():
                finish(acc[...])

        if n_ex:
            @pl.when((step[0] == grid[0] - 1) & (step[1] == grid[1] - 1) & (step[2] == grid[2] - 1))
            def _():
                _exchange_wait(g_refs, r_refs, *sems)

    in_specs, args = [a_spec, b_spec], [a, b]
    if has_add:
        in_specs.append(add_spec)
        args.append(add)
    any_spec = pl.BlockSpec(memory_space=pl.ANY)
    scratch = [] if nk == 1 else [pltpu.VMEM(acc_shape, F32)]
    if n_ex:
        scratch += _exchange_sems(n_ex)
    res = pl.pallas_call(
        body, name=name, grid=grid, in_specs=in_specs + [any_spec] * n_ex, out_specs=[out_spec] + [any_spec] * n_ex,
        out_shape=[out_shape] + [jax.ShapeDtypeStruct(s, g.dtype) for s, g in zip(ex_shapes, ex)], scratch_shapes=scratch,
        compiler_params=pltpu.CompilerParams(
            dimension_semantics=("arbitrary",) * 3 if n_ex else ("parallel", "parallel", "arbitrary"),
            vmem_limit_bytes=VMEM_LIMIT, has_side_effects=bool(n_ex)))(*args, *ex)
    return (res[0], list(res[1:])) if n_ex else res[0]


def _matmul(a, b, mode="nn", add=None, out_dtype=F32, name="matmul", **pushed):
    if mode == "nn":
        (M, K), N = a.shape, b.shape[1]
    elif mode == "nt":
        (M, K), N = a.shape, b.shape[0]
    else:
        (K, M), N = a.shape, b.shape[1]
    if mode == "tn" and 1024 < N <= 5120:
        tm, tn = _tile(M, (512, 256, 128, 64, 32, 16, 8)), N
        tk = _tile(K, (2048 if tn <= 3072 else 1024, 1024, 512, 256, 128))
    else:
        tk = K if K <= 5120 else _tile(K, (2048, 1024, 512, 256, 128))
        tm = _tile(M, ((2048,) if tk <= 1024 and mode != "tn" else ()) + (1024, 512, 256, 128, 64, 32, 16, 8))
        tn = _tile(N, (512, 256, 128))
    if mode == "tn":
        a_spec = pl.BlockSpec((tk, tm), lambda i, j, k: (k, i))
    else:
        a_spec = pl.BlockSpec((tm, tk), lambda i, j, k: (i, k))
    if mode == "nt":
        b_spec = pl.BlockSpec((tn, tk), lambda i, j, k: (j, k))
    else:
        b_spec = pl.BlockSpec((tk, tn), lambda i, j, k: (k, j))
    blk = pl.BlockSpec((tm, tn), lambda i, j, k: (i, j))
    return _mm_call(name, (M // tm, N // tn, K // tk), a, a_spec, b, b_spec,
                    {"nn": (1, 0), "nt": (1, 1), "tn": (0, 0)}[mode], jax.ShapeDtypeStruct((M, N), out_dtype), blk,
                    (tm, tn), add, blk, **pushed)


FF_SH = 2 * D_FF // N_DEV


def _ffn_up(h2, wup, **pushed):
    S, D = h2.shape
    tm = _tile(S, (2048, 1024, 512, 256, 128))
    return _mm_call("matmul_up", (S // tm, N_DEV, 1), h2, pl.BlockSpec((tm, D), lambda m, j, k: (m, 0)),
                    wup, pl.BlockSpec((None, D, FF_SH), lambda m, j, k: (j, 0, 0)), (1, 0),
                    jax.ShapeDtypeStruct((2, N_DEV // 2, S, FF_SH), F32),
                    pl.BlockSpec((None, None, tm, FF_SH), lambda m, j, k: (j // 4, j % 4, m, 0)), (tm, FF_SH), **pushed)


def _ffn_down(act, wd, x1, **pushed):
    _, S, _ = act.shape
    D = wd.shape[1]
    tm, tn = _tile(S, (1024, 512, 256, 128)), _tile(D, (1024, 512, 256, 128))
    blk = pl.BlockSpec((tm, tn), lambda m, n, k: (m, n))
    return _mm_call("matmul_down", (S // tm, D // tn, N_DEV // 2), act,
                    pl.BlockSpec((None, tm, FF_SH), lambda m, n, k: (k, m, 0)), wd,
                    pl.BlockSpec((FF_SH, tn), lambda m, n, k: (k, n)), (1, 0), jax.ShapeDtypeStruct((S, D), F32), blk,
                    (tm, tn), x1, blk, **pushed)


def _ffn_dact(dxb, wd):
    S, D = dxb.shape
    tm = _tile(S, (2048, 1024, 512, 256, 128))
    return _mm_call("matmul_dact", (S // tm, N_DEV // 2, 1), dxb, pl.BlockSpec((tm, D), lambda m, j, k: (m, 0)), wd,
                    pl.BlockSpec((FF_SH, D), lambda m, j, k: (j, 0)), (1, 1),
                    jax.ShapeDtypeStruct((N_DEV // 2, S, FF_SH), BF),
                    pl.BlockSpec((None, tm, FF_SH), lambda m, j, k: (j, m, 0)), (tm, FF_SH))


def _ffn_dw_down(act, dxb):
    _, S, _ = act.shape
    D = dxb.shape[1]
    tk, tn = _tile(S, (2048, 1024, 512, 256, 128)), _tile(D, (512, 256, 128))
    return _mm_call("matmul_dw_down", (N_DEV // 2, D // tn, S // tk), act,
                    pl.BlockSpec((None, tk, FF_SH), lambda j, n, k: (j, k, 0)), dxb,
                    pl.BlockSpec((tk, tn), lambda j, n, k: (k, n)), (0, 0), jax.ShapeDtypeStruct((D_FF, D), BF),
                    pl.BlockSpec((FF_SH, tn), lambda j, n, k: (j, n)), (FF_SH, tn))


def _ffn_dw_up(h2, du, exchange=None):
    S, D = h2.shape
    tk = _tile(S, (2048, 1024, 512, 256, 128))
    return _mm_call("matmul_dw_up", (N_DEV, 1, S // tk), h2, pl.BlockSpec((tk, D), lambda j, n, k: (k, 0)), du,
                    pl.BlockSpec((None, None, tk, FF_SH), lambda j, n, k: (j // 4, j % 4, k, 0)), (0, 0),
                    jax.ShapeDtypeStruct((N_DEV, D, FF_SH), BF),
                    pl.BlockSpec((None, D, FF_SH), lambda j, n, k: (j, 0, 0)), (D, FF_SH), exchange=exchange)


def _ffn_dh2(du, wup, exchange=None):
    S = du.shape[2]
    D = wup.shape[1]
    tm = _tile(S, (1024, 512, 256, 128))
    return _mm_call("matmul_dh2", (S // tm, 1, N_DEV), du,
                    pl.BlockSpec((None, None, tm, FF_SH), lambda m, n, k: (k // 4, k % 4, m, 0)), wup,
                    pl.BlockSpec((None, D, FF_SH), lambda m, n, k: (k, 0, 0)), (1, 1),
                    jax.ShapeDtypeStruct((S, D), F32), pl.BlockSpec((tm, D), lambda m, n, k: (m, 0)), (tm, D),
                    exchange=exchange)


def f_rmsnorm(first, xp, xs, ps, sts):
    return (_rms(xs[0], ps[0]),), ()


def _same_block(shape, rows_per, cols_per):
    return (lax.broadcasted_iota(jnp.int32, shape, 0) // rows_per) == (lax.broadcasted_iota(jnp.int32, shape, 1) // cols_per)


@jax.custom_vjp
def _head_mean(x):
    n = x.shape[1]
    return _dot_hi(x, jnp.where(_same_block((n, n), XA_HD, XA_HD), 1.0 / XA_HD, 0.0), 1, 0)


_head_mean.defvjp(lambda x: (_head_mean(x), None), lambda r, g: (_head_mean(g),))


def f_xattn(first, xp, xs, ps, sts):
    (xq,), (kv, qn, kn) = xs, ps
    k, v = _split(kv, [D_XA, D_XA])
    m_rows = kv.shape[0]
    q = xq * lax.rsqrt(_head_mean(xq * xq) + EPS) * jnp.concatenate([qn] * XA_HEADS, axis=1)
    k = k * lax.rsqrt(_head_mean(k * k) + EPS) * jnp.concatenate([kn] * XA_HEADS, axis=1)
    kt = k.T
    kbd = jnp.where(_same_block((D_XA, XA_HEADS * m_rows), XA_HD, m_rows), jnp.concatenate([kt] * XA_HEADS, axis=1), 0.0)
    s = mm_nn(q, kbd) * (XA_HD ** -0.5)
    ps_ = []
    for sh in _split(s, [m_rows] * XA_HEADS):
        mx = lax.stop_gradient(jnp.max(sh, axis=-1, keepdims=True))
        p = jnp.exp(sh - mx)
        ps_.append(p / jnp.sum(p, axis=-1, keepdims=True))
    vbd = jnp.where(_same_block((XA_HEADS * m_rows, D_XA), m_rows, XA_HD), jnp.concatenate([v] * XA_HEADS, axis=0), 0.0)
    return (mm_nn(jnp.concatenate(ps_, axis=1), vbd),), ()


def _conv(xp, x, w, b, first, taps):
    xp = jnp.where(first, 0.0, xp)
    y = b + w[taps - 1:taps] * x
    for d in range(1, taps):
        y = y + w[taps - 1 - d:taps - d] * _shift(xp, x, d)
    return y


def _unstack2(x):
    @jax.custom_vjp
    def us(x):
        return x[0], x[1]

    us.defvjp(lambda x: (us(x), None), lambda r, g: (jnp.stack(g),))
    return us(x)


def f_ffn_act(first, xp, xs, ps, sts):
    (up,), (u,), (wg, wv, bg, bv) = xp, xs, ps
    (ugp, uvp), (ug, uv) = _unstack2(up), _unstack2(u)
    gate = _conv(ugp, ug, wg, bg, first, FFN_CONV)
    val = _conv(uvp, uv, wv, bv, first, FFN_CONV)
    return (jax.nn.silu(gate) * val,), ()


def _gla_chunk(q, k, v, la, sts, dk, dv):
    c, nh = q.shape[0], len(sts)
    b = cumsum_rows(la)
    b_last = _row(b, c - 1)
    b_ref = _row(b, c // 2 - 1)
    qe, ke = _split(q * jnp.exp(b - b_ref), [dk] * nh), _split(k * jnp.exp(b_ref - b), [dk] * nh)
    qi, kl = _split(q * jnp.exp(b), [dk] * nh), _split(k * jnp.exp(b_last - b), [dk] * nh)
    dec, vs = _split(jnp.exp(b_last), [dk] * nh), _split(v, [dv] * nh)
    tril = _tril(c)
    outs, new = [], []
    for h in range(nh):
        att = jnp.where(tril, mm_nt(qe[h], ke[h]), 0.0)
        outs.append(mm_nn(att, vs[h]) + mm_nt(qi[h], sts[h]))
        new.append(sts[h] * dec[h] + mm_tn(vs[h], kl[h]))
    return outs, tuple(new)


def _split_rows(x, n):
    c = x.shape[0] // n

    @jax.custom_vjp
    def sp(x):
        return tuple(x[i * c:(i + 1) * c] for i in range(n))

    sp.defvjp(lambda x: (sp(x), None), lambda r, g: (jnp.concatenate(list(g), axis=0),))
    return sp(x)


def _gla_scan(q, k, v, la, sts, dk, dv):
    n = q.shape[0] // CHUNK
    per_chunk = []
    for qc, kc, vc, lc in zip(*(_split_rows(t, n) for t in (q, k, v, la))):
        o, sts = _gla_chunk(qc, kc, vc, lc, sts, dk, dv)
        per_chunk.append(o)
    return [jnp.concatenate([o[h] for o in per_chunk], axis=0) for h in range(len(sts))], sts


def _a_cols(ntot):
    used = D_XA + 2 * GLA_HEADS * GLA_DK + D_MIX + GLA_RANK + D_MIX
    return [D_XA, GLA_HEADS * GLA_DK, GLA_HEADS * GLA_DK, D_MIX, GLA_RANK, D_MIX] + ([ntot - used] if ntot > used else [])


def f_gla(first, xp, xs, ps, sts):
    (p,), (wg2, bg, on) = xs, ps
    parts = _split(p, _a_cols(p.shape[1]))
    q, k, v, glr, og = parts[1:6]
    la = jax.nn.log_sigmoid(mm_nn(glr, wg2) + bg) / GLA_GATE_NORM
    outs, new = _gla_scan(q * (GLA_DK ** -0.5), k, v, la, tuple(sts), GLA_DK, GLA_DV)
    return (jnp.concatenate([_rms(o, on) for o in outs], axis=1) * jax.nn.silu(og),), new


def f_hgrn(first, xp, xs, ps, sts):
    (p,), (lbp, on) = xs, ps
    _, q, fgate, iv, og = _split(p, [D_XA, D_MIX, D_MIX, D_MIX, D_MIX])
    e = jnp.exp(lbp - jnp.max(lbp, axis=0, keepdims=True))
    row = lax.broadcasted_iota(jnp.int32, e.shape, 0)
    lb = jnp.sum(jnp.where(row >= 1, e, 0.0), axis=0, keepdims=True) / jnp.sum(e, axis=0, keepdims=True)
    fg = lb + (1.0 - lb) * jax.nn.sigmoid(fgate)
    outs, new = _gla_scan(jax.nn.silu(q), 1.0 - fg, iv, jnp.log(fg), tuple(sts), HGRN_DK, HGRN_DV)
    return (jnp.concatenate([_rms(o, on) for o in outs], axis=1) * jax.nn.sigmoid(og),), new


def _c_cols(ntot):
    gn = SSM_GROUPS * SSM_STATE
    used = D_XA + D_MIX + D_MIX + 2 * gn + SSM_HEADS
    return [D_XA, D_MIX, D_MIX + 2 * gn, SSM_HEADS] + ([ntot - used] if ntot > used else [])


def f_ssd(first, xp, xs, ps, sts):
    (pp,), (p,), (cw, cb, dtb, alog, dsk, ng) = xp, xs, ps
    gn = SSM_GROUPS * SSM_STATE
    _, z, xbc, dtr = _split(p, _c_cols(p.shape[1]))[:4]
    xbc_p = _split(pp, _c_cols(p.shape[1]))[2]
    xbc = jax.nn.silu(_conv(xbc_p, xbc, cw, cb, first, SSM_CONV))
    xs_, bm, cm = _split(xbc, [D_MIX, gn, gn])
    dt = jax.nn.softplus(dtr + dtb)
    n = p.shape[0] // CHUNK
    ys, sts = [], tuple(sts)
    for xc, bc, cc, dc in zip(*(_split_rows(t, n) for t in (xs_, bm, cm, dt))):
        y, sts = _ssd_chunk(xc, bc, cc, dc, alog, dsk, sts)
        ys.append(y)
    y = jnp.concatenate(ys, axis=0) * jax.nn.silu(z)
    gw = D_MIX // SSM_GROUPS
    yg = _split(y, [gw] * SSM_GROUPS)
    ngs = _split(ng, [gw] * SSM_GROUPS)
    y = jnp.concatenate([_rms(yg[g], ngs[g]) for g in range(SSM_GROUPS)], axis=1)
    return (y,), sts


def _ssd_chunk(xs_, bm, cm, dt, alog, dsk, sts):
    c = xs_.shape[0]
    hg = SSM_HEADS // SSM_GROUPS
    a = dt * (-jnp.exp(alog))
    acs = cumsum_rows(a)
    acs_t = cumsum_rows_t(a)
    acs_last = _row(acs, c - 1)
    dt_h = _split(dt, [1] * SSM_HEADS)
    acs_h = _split(acs, [1] * SSM_HEADS)
    al_h = _split(acs_last, [1] * SSM_HEADS)
    d_h = _split(dsk, [1] * SSM_HEADS)
    x2s = _split(xs_, [2 * SSM_HD] * (SSM_HEADS // 2))
    bms = _split(bm, [SSM_STATE] * SSM_GROUPS)
    cms = _split(cm, [SSM_STATE] * SSM_GROUPS)
    tril = _tril(c)
    cbs = [mm_nt(cms[g], bms[g]) for g in range(SSM_GROUPS)]
    ys, new = [], []
    for j in range(SSM_HEADS // 2):
        g = (2 * j) // hg
        h0, h1 = 2 * j, 2 * j + 1
        xdt = x2s[j] * _lane_pair(dt_h[h0], dt_h[h1])
        acs2 = _lane_pair(acs_h[h0], acs_h[h1])
        al2 = _lane_pair(al_h[h0], al_h[h1])
        yd = []
        for h in (h0, h1):
            seg = acs_h[h] - _row(acs_t, h)
            lm = jnp.exp(jnp.where(tril, seg, NEG))
            yd.append(mm_nn(cbs[g] * lm, xdt))
        lane = lax.broadcasted_iota(jnp.int32, xdt.shape, 1)
        y_diag = jnp.where(lane < SSM_HD, yd[0], yd[1])
        y_off = mm_nn(cms[g], sts[j]) * jnp.exp(acs2)
        x_end = xdt * jnp.exp(al2 - acs2)
        new.append(sts[j] * jnp.exp(al2) + mm_tn(bms[g], x_end))
        ys.append(y_diag + y_off + _lane_pair(d_h[h0], d_h[h1]) * x2s[j])
    return jnp.concatenate(ys, axis=1), tuple(new)


def f_dil_prep(first, xp, xs, ps, sts):
    (p, pos), (qn, kn, invf, sign) = xs, ps
    nh = len(DIL_GROUPS) * DIL_HEADS
    _, q, k, v = _split(p, [D_XA] + [nh * DIL_HD] * 3)
    ang = pos * invf
    cos, sin = jnp.cos(ang), jnp.sin(ang) * sign

    def rope(t, g):
        hs = _split(t, [DIL_HD] * nh)
        out = []
        for h in hs:
            n = _rms(h, g)
            out.append(n * cos + _roll_half(n) * sin)
        return [jnp.concatenate(out[i:i + DIL_HEADS], axis=1) for i in range(0, nh, DIL_HEADS)]

    return tuple(rope(q, qn) + rope(k, kn) + list(_split(v, [D_DIL] * len(DIL_GROUPS)))), ()


def f_dil_attn(first, xp, xs, ps, sts):
    (kp, vp), (q, k, v) = xp, xs
    Q = DIL_BLOCK
    n = q.shape[0] // Q
    qb, kb, vb = (_split_rows(t, n) for t in (q, k, v))
    kprev, vprev = _split_rows(kp, n)[-1], _split_rows(vp, n)[-1]
    i = lax.broadcasted_iota(jnp.int32, (Q, 2 * Q), 0)
    j = lax.broadcasted_iota(jnp.int32, (Q, 2 * Q), 1)
    dist = Q + i - j
    band = (dist >= 0) & (dist <= Q)
    o_rows, lse_rows = [], []
    for b in range(n):
        mask = band & (jnp.logical_not(first) | (j >= Q)) if b == 0 else band
        qs, ks, vs = (_split(t, [DIL_HD] * DIL_HEADS) for t in (qb[b], kb[b], vb[b]))
        kps, vps = (_split(t, [DIL_HD] * DIL_HEADS) for t in (kprev, vprev))
        outs, lses = [], []
        for h in range(DIL_HEADS):
            k2 = jnp.concatenate([kps[h], ks[h]], axis=0)
            v2 = jnp.concatenate([vps[h], vs[h]], axis=0)
            s = jnp.where(mask, mm_nt(qs[h], k2) * (DIL_HD ** -0.5), NEG)
            m = lax.stop_gradient(jnp.max(s, axis=-1, keepdims=True))
            p = jnp.exp(s - m)
            l = jnp.sum(p, axis=-1, keepdims=True)
            outs.append(mm_nn(p / l, v2))
            lses.append(jnp.broadcast_to(m + jnp.log(l), (Q, DIL_HD)))
        o_rows.append(jnp.concatenate(outs, axis=1))
        lse_rows.append(jnp.concatenate(lses, axis=1))
        kprev, vprev = kb[b], vb[b]
    return (jnp.concatenate(o_rows, axis=0), jnp.concatenate(lse_rows, axis=0)), ()


def f_dil_merge(first, xp, xs, ps, sts):
    o0, o1, o2, l0, l1, l2 = xs
    m = jnp.maximum(jnp.maximum(l0, l1), l2)
    e0, e1, e2 = jnp.exp(l0 - m), jnp.exp(l1 - m), jnp.exp(l2 - m)
    den = e0 + e1 + e2
    return ((e0 * o0 + e1 * o1 + e2 * o2) / den,), ()


def _loss_head(y, target):
    S, D = y.shape
    R = _tile(S, (512, 256, 128, 64, 32, 16, 8))

    def body(y_ref, t_ref, dy_ref, dyb_ref, l_ref):
        e = y_ref[...] - t_ref[...]
        dy_ref[...] = e * (1.0 / D)
        dyb_ref[...] = (e * (1.0 / D)).astype(BF)

        @pl.when(pl.program_id(0) == 0)
        def _():
            l_ref[...] = jnp.zeros_like(l_ref)

        l_ref[...] += jnp.broadcast_to(0.5 * jnp.sum(jnp.mean(e * e, axis=-1, keepdims=True), axis=0, keepdims=True),
                                       l_ref.shape)

    blk = pl.BlockSpec((R, D), lambda i: (i, 0))
    dy, dyb, l = pl.pallas_call(
        body, name="loss_head", grid=(S // R,), in_specs=[blk, blk],
        out_specs=[blk, blk, pl.BlockSpec((8, 128), lambda i: (0, 0))],
        out_shape=[jax.ShapeDtypeStruct((S, D), F32), jax.ShapeDtypeStruct((S, D), BF),
                   jax.ShapeDtypeStruct((8, 128), F32)],
        compiler_params=pltpu.CompilerParams(dimension_semantics=("arbitrary",)))(y, target)
    return dy, dyb, l[0, 0]


def _adamw(parts, w, m, v, name):
    _, n, width = parts.shape
    tr = _tile(n, [t for t in (512, 256, 128, 64, 32, 16, 8) if t * width <= ADAM_BLOCK])

    def body(p_ref, w_ref, m_ref, v_ref, g_ref, d_ref, nm_ref, nv_ref):
        g = p_ref[0].astype(F32)
        for s in range(1, N_DEV):
            g = g + p_ref[s].astype(F32)
        nm = ADAM_B1 * m_ref[...] + (1.0 - ADAM_B1) * g
        nv = ADAM_B2 * v_ref[...] + (1.0 - ADAM_B2) * (g * g)
        m_hat = nm / (1.0 - ADAM_B1 ** ADAM_STEP)
        v_hat = nv / (1.0 - ADAM_B2 ** ADAM_STEP)
        g_ref[...] = g
        d_ref[...] = -ADAM_LR * (m_hat / (jnp.sqrt(v_hat) + ADAM_EPS) + ADAM_WD * w_ref[...])
        nm_ref[...] = nm
        nv_ref[...] = nv

    blk = pl.BlockSpec((tr, width), lambda i: (i, 0))
    return pl.pallas_call(
        body, name=name, grid=(n // tr,),
        in_specs=[pl.BlockSpec((N_DEV, tr, width), lambda i: (0, i, 0)), blk, blk, blk],
        out_specs=[blk] * 4, out_shape=[jax.ShapeDtypeStruct((n, width), F32)] * 4,
        compiler_params=pltpu.CompilerParams(dimension_semantics=("arbitrary",), vmem_limit_bytes=VMEM_LIMIT))(
            parts, w, m, v)


def _peer(k):
    x, y, c = lax.axis_index("x"), lax.axis_index("y"), lax.axis_index("c")
    px = 1 - x if k & 4 else x
    py = 1 - y if k & 2 else y
    pc = 1 - c if k & 1 else c
    return (px, py, pc), 4 * px + 2 * py + pc


def _my_id():
    return 4 * lax.axis_index("x") + 2 * lax.axis_index("y") + lax.axis_index("c")


def _all_gather(x, name):
    def body(x_ref, out_ref, send, recv, loc):
        me = _my_id()
        mine = pltpu.make_async_copy(x_ref, out_ref.at[me], loc)
        mine.start()
        cps = []
        for k in range(1, N_DEV):
            peer, _ = _peer(k)
            cp = pltpu.make_async_remote_copy(src_ref=x_ref, dst_ref=out_ref.at[me], send_sem=send.at[k - 1],
                                              recv_sem=recv.at[k - 1], device_id=peer,
                                              device_id_type=pl.DeviceIdType.MESH)
            cp.start()
            cps.append(cp)
        for k in range(1, N_DEV):
            peer, pid = _peer(k)
            pltpu.make_async_remote_copy(src_ref=x_ref, dst_ref=out_ref.at[pid], send_sem=send.at[k - 1],
                                         recv_sem=recv.at[k - 1], device_id=peer,
                                         device_id_type=pl.DeviceIdType.MESH).wait_recv()
        for cp in cps:
            cp.wait_send()
        mine.wait()

    return pl.pallas_call(
        body, name=name, out_shape=jax.ShapeDtypeStruct((N_DEV,) + x.shape, x.dtype),
        in_specs=[pl.BlockSpec(memory_space=pl.ANY)], out_specs=pl.BlockSpec(memory_space=pl.ANY),
        scratch_shapes=[pltpu.SemaphoreType.DMA((N_DEV - 1,)), pltpu.SemaphoreType.DMA((N_DEV - 1,)),
                        pltpu.SemaphoreType.DMA],
        compiler_params=pltpu.CompilerParams(has_side_effects=True))(x)


def _exchange_sems(n):
    return [pltpu.SemaphoreType.DMA((n * (N_DEV - 1),)), pltpu.SemaphoreType.DMA((n * (N_DEV - 1),)),
            pltpu.SemaphoreType.DMA((n,))]


def _exchange_copies(g_refs, out_refs, send, recv, loc, with_arrivals):
    me = _my_id()

    def mine(g, o, d):
        return g.at[d] if len(g.shape) == len(o.shape) else g

    local = [pltpu.make_async_copy(mine(g, o, me), o.at[me], loc.at[w]) for w, (g, o) in enumerate(zip(g_refs, out_refs))]
    pushes, arrivals = [], []
    for k in range(1, N_DEV):
        peer, pid = _peer(k)
        for w, (g, o) in enumerate(zip(g_refs, out_refs)):
            s = w * (N_DEV - 1) + k - 1
            ends = [(mine(g, o, pid), o.at[me], pushes)] + ([(mine(g, o, me), o.at[pid], arrivals)] if with_arrivals else [])
            for src, dst, into in ends:
                into.append(pltpu.make_async_remote_copy(src_ref=src, dst_ref=dst, send_sem=send.at[s],
                                                         recv_sem=recv.at[s], device_id=peer,
                                                         device_id_type=pl.DeviceIdType.MESH))
    return local, pushes, arrivals


def _exchange_start(g_refs, out_refs, send, recv, loc):
    local, pushes, _ = _exchange_copies(g_refs, out_refs, send, recv, loc, False)
    for cp in local + pushes:
        cp.start()


def _exchange_wait(g_refs, out_refs, send, recv, loc):
    local, pushes, arrivals = _exchange_copies(g_refs, out_refs, send, recv, loc, True)
    for cp in arrivals:
        cp.wait_recv()
    for cp in pushes:
        cp.wait_send()
    for cp in local:
        cp.wait()


def _exchange_many(gs, name):
    n = len(gs)

    def body(*refs):
        g_refs, out_refs, sems = refs[:n], refs[n:2 * n], refs[2 * n:]
        _exchange_start(g_refs, out_refs, *sems)
        _exchange_wait(g_refs, out_refs, *sems)

    return pl.pallas_call(
        body, name=name, out_shape=[jax.ShapeDtypeStruct(g.shape, g.dtype) for g in gs],
        in_specs=[pl.BlockSpec(memory_space=pl.ANY)] * n, out_specs=[pl.BlockSpec(memory_space=pl.ANY)] * n,
        scratch_shapes=_exchange_sems(n), compiler_params=pltpu.CompilerParams(has_side_effects=True))(*gs)


def _gather_many(xs, name):
    n = len(xs)

    def body(*refs):
        x_refs, out_refs, (send, recv, loc) = refs[:n], refs[n:2 * n], refs[2 * n:]
        x, y, c = lax.axis_index("x"), lax.axis_index("y"), lax.axis_index("c")
        me, sibling = (x, y, c), (x, y, 1 - c)
        chips = [(1 - x, y), (x, 1 - y), (1 - x, 1 - y)]

        def slot(p):
            return 4 * p[0] + 2 * p[1] + p[2]

        def copy(w, k, block, to, src=None):
            dst = out_refs[w].at[slot(block)]
            return pltpu.make_async_remote_copy(src_ref=dst if src is None else src, dst_ref=dst,
                                                send_sem=send.at[w * (N_DEV - 1) + k], recv_sem=recv.at[w * (N_DEV - 1) + k],
                                                device_id=to, device_id_type=pl.DeviceIdType.MESH)

        mine = [pltpu.make_async_copy(x_refs[w], out_refs[w].at[slot(me)], loc.at[w]) for w in range(n)]
        for cp in mine:
            cp.start()
        first = []
        for j, chip in enumerate(chips):
            first += [copy(w, 1 + j, me, (*chip, c), src=x_refs[w]) for w in range(n)]
        first += [copy(w, 0, me, sibling, src=x_refs[w]) for w in range(n)]
        for cp in first:
            cp.start()
        passed = []
        for j, chip in enumerate(chips):
            for w in range(n):
                copy(w, 1 + j, (*chip, c), me).wait_recv()
                cp = copy(w, 4 + j, (*chip, c), sibling)
                cp.start()
                passed.append(cp)
        for w in range(n):
            copy(w, 0, sibling, me).wait_recv()
            for j, chip in enumerate(chips):
                copy(w, 4 + j, (*chip, 1 - c), me).wait_recv()
        for cp in first + passed:
            cp.wait_send()
        for cp in mine:
            cp.wait()

    return pl.pallas_call(
        body, name=name, out_shape=[jax.ShapeDtypeStruct((N_DEV,) + x.shape, x.dtype) for x in xs],
        in_specs=[pl.BlockSpec(memory_space=pl.ANY)] * n, out_specs=[pl.BlockSpec(memory_space=pl.ANY)] * n,
        scratch_shapes=[pltpu.SemaphoreType.DMA((n * (N_DEV - 1),)), pltpu.SemaphoreType.DMA((n * (N_DEV - 1),)),
                        pltpu.SemaphoreType.DMA((n,))],
        compiler_params=pltpu.CompilerParams(has_side_effects=True))(*xs)


def _cat_segs(G, ws, n_mix):
    segs = []
    for g in range(G):
        lo, hi = g * ws, (g + 1) * ws
        if lo < n_mix:
            segs.append((g, 0, min(hi, n_mix) - lo, D_XA + lo))
        if hi > n_mix:
            s = max(lo, n_mix)
            segs.append((g, s - lo, hi - s, s - n_mix))
    return segs


def _cat_cols(src, n_mix, ntot):
    G, R, ws = src.shape
    segs = _cat_segs(G, ws, n_mix)
    tr = _tile(R, (256, 128, 64, 32, 16, 8))

    def body(i_ref, o_ref):
        if ntot > G * ws:
            o_ref[...] = jnp.zeros_like(o_ref)
        for g, s, n, d in segs:
            o_ref[:, d:d + n] = i_ref[g][:, s:s + n]

    return pl.pallas_call(
        body, name="cat_cols", grid=(R // tr,), in_specs=[pl.BlockSpec((G, tr, ws), lambda i: (0, i, 0))],
        out_specs=pl.BlockSpec((tr, ntot), lambda i: (i, 0)), out_shape=jax.ShapeDtypeStruct((R, ntot), src.dtype),
        compiler_params=pltpu.CompilerParams(dimension_semantics=("arbitrary",)))(src)


def _uncat_cols(dw, G, ws, n_mix):
    R, ntot = dw.shape
    segs = _cat_segs(G, ws, n_mix)
    tr = _tile(R, (256, 128, 64, 32, 16, 8))

    def body(i_ref, o_ref):
        v = i_ref[...]
        for g, s, n, d in segs:
            o_ref[g, :, s:s + n] = v[:, d:d + n]

    return pl.pallas_call(
        body, name="uncat_cols", grid=(R // tr,), in_specs=[pl.BlockSpec((tr, ntot), lambda i: (i, 0))],
        out_specs=pl.BlockSpec((G, tr, ws), lambda i: (0, i, 0)), out_shape=jax.ShapeDtypeStruct((G, R, ws), dw.dtype),
        compiler_params=pltpu.CompilerParams(dimension_semantics=("arbitrary",)))(dw)


PACK_W = 1024


def _granule(n):
    return (256 if n >= 256 * PACK_W else 8) * PACK_W


def _pack(arrs, dtype):
    flat = jnp.concatenate([a.reshape(-1).astype(dtype) for a in arrs])
    n = flat.shape[0]
    pad = (-n) % _granule(n)
    if pad:
        flat = jnp.concatenate([flat, jnp.zeros((pad,), dtype)])
    return flat.reshape(-1, PACK_W)


def _unpack(packed, shapes):
    flat = packed.reshape(-1)
    out, o = [], 0
    for s in shapes:
        n = math.prod(s)
        out.append(flat[o:o + n].reshape(s))
        o += n
    return out


def _pack_lead(arrs, dtype):
    flat = jnp.concatenate([a.reshape(N_DEV, -1).astype(dtype) for a in arrs], axis=1)
    n = flat.shape[1]
    pad = (-n) % _granule(n)
    if pad:
        flat = jnp.concatenate([flat, jnp.zeros((N_DEV, pad), dtype)], axis=1)
    return flat.reshape(N_DEV, -1, PACK_W)


def _to_full(stacked, axis):
    t = jnp.moveaxis(stacked, 0, axis)
    s = list(t.shape)
    return t.reshape(s[:axis] + [s[axis] * s[axis + 1]] + s[axis + 2:])


def _to_chunks(full, axis):
    s = list(full.shape)
    t = full.reshape(s[:axis] + [N_DEV, s[axis] // N_DEV] + s[axis + 1:])
    return jnp.moveaxis(t, axis, 0)


def _rows_of(S):
    return _tile(S, (512, 256, 128, 64))


def _norm_fwd(x, g, dt=BF):
    (h,), _ = _seq_fwd("rmsnorm_fwd", f_rmsnorm, _rows_of(x.shape[0]), [_row_spec(x)], [_par_spec(g)], [],
                       [_out_spec(x.shape[1], dt=dt)])
    return h


def _norm_bwd(x, g, dh, res=None):
    if res is None:
        (dx,), (dg,) = _seq_bwd("rmsnorm_bwd", f_rmsnorm, _rows_of(x.shape[0]), [_row_spec(x)], [_par_spec(g)], [],
                                [_row_spec(dh)], [])
        return dx, None, dg
    (dx,), (dg,), (dxb,) = _seq_bwd("rmsnorm_res_bwd", f_rmsnorm, _rows_of(x.shape[0]), [_row_spec(x)], [_par_spec(g)], [],
                                    [_row_spec(dh)], [], dx_add={0: res}, dx_bf=True)
    return dx, dxb, dg


def _mixer_specs(kind, S, p, w):
    if kind == 0:
        return (f_gla, min(S, MIX_ROWS), [_row_spec(p)],
                [_par_spec(w['a_w_gate2']), _par_spec(w['a_b_gate'].reshape(1, -1)), _par_spec(w['a_o_norm'].reshape(1, -1))],
                [(GLA_DV, GLA_DK)] * GLA_HEADS, D_MIX)
    if kind == 2:
        return (f_ssd, min(S, MIX_ROWS), [_row_spec(p, prev='halo')],
                [_par_spec(w['c_conv_w']), _par_spec(w['c_conv_b'].reshape(1, -1)), _par_spec(w['c_dt_bias'].reshape(1, -1)),
                 _par_spec(w['c_a_log'].reshape(1, -1)), _par_spec(w['c_d'].reshape(1, -1)),
                 _par_spec(w['c_norm'].reshape(1, -1))],
                [(SSM_STATE, 2 * SSM_HD)] * (SSM_HEADS // 2), D_MIX)
    return (f_hgrn, min(S, MIX_ROWS), [_row_spec(p)],
            [_par_spec(w['d_lower_bounds']), _par_spec(w['d_o_norm'].reshape(1, -1))],
            [(HGRN_DV, HGRN_DK)] * HGRN_HEADS, D_MIX)


def _perm(t, r):
    if r == 1:
        return t
    S, n = t.shape
    return t.reshape(S // r, r, n).transpose(1, 0, 2).reshape(S, n)


def _unperm(t, r):
    if r == 1:
        return t
    S, n = t.shape
    return t.reshape(r, S // r, n).transpose(1, 0, 2).reshape(S, n)


def _rope_consts():
    half = DIL_HD // 2
    inv = ROPE_THETA ** (-jnp.arange(half, dtype=F32) / half)
    invf = jnp.concatenate([inv, inv]).reshape(1, DIL_HD)
    sign = jnp.concatenate([-jnp.ones((half,), F32), jnp.ones((half,), F32)]).reshape(1, DIL_HD)
    return invf, sign


def _dil_fwd(p, pos, w, ncat):
    S = p.shape[0]
    invf, sign = _rope_consts()
    prep_rows = [_row_spec(p), _row_spec(pos, diff=False)]
    prep_pars = [_par_spec(w['b_q_norm'].reshape(1, -1)), _par_spec(w['b_k_norm'].reshape(1, -1)),
                 _par_spec(invf, diff=False), _par_spec(sign, diff=False)]
    ng = len(DIL_GROUPS)
    qkv, _ = _seq_fwd("dil_prep_fwd", f_dil_prep, _tile(S, (256, 128)), prep_rows, prep_pars, [],
                      [_out_spec(D_DIL) for _ in range(3 * ng)])
    res = dict(perm=[], o=[], lse=[])
    for g, (window, r) in enumerate(DIL_GROUPS):
        qp, kp, vp = _perm(qkv[g], r), _perm(qkv[ng + g], r), _perm(qkv[2 * ng + g], r)
        rows = [_row_spec(qp), _row_spec(kp, prev='block'), _row_spec(vp, prev='block')]
        Rg = min(DIL_ROWS, S // r)
        (o, lse), _ = _seq_fwd("dil_attn_fwd", f_dil_attn, Rg, rows, [], [], [_out_spec(D_DIL), _out_spec(D_DIL)],
                               period=S // r // Rg)
        res['perm'].append((qp, kp, vp))
        res['o'].append(_unperm(o, r))
        res['lse'].append(_unperm(lse, r))
    mrows = [_row_spec(t) for t in res['o'] + res['lse']]
    (cat,), _ = _seq_fwd("dil_merge_fwd", f_dil_merge, _rows_of(S), mrows, [], [], [_out_spec(ncat, w=D_DIL, dt=BF)])
    res['prep'] = (prep_rows, prep_pars)
    return cat, res


def _dil_bwd(dtok, res, p, exchange=None, exchange_local=None):
    S = p.shape[0]
    mrows = [_row_spec(t) for t in res['o'] + res['lse']]
    dm, _ = _seq_bwd("dil_merge_bwd", f_dil_merge, _rows_of(S), mrows, [], [], [dtok], [])
    dq, dk, dv = [], [], []
    for g, (window, r) in enumerate(DIL_GROUPS):
        qp, kp, vp = res['perm'][g]
        rows = [_row_spec(qp), _row_spec(kp, prev='block'), _row_spec(vp, prev='block')]
        douts = [_row_spec(_perm(dm[g], r)), _row_spec(_perm(dm[3 + g], r))]
        Rg = min(DIL_ROWS, S // r)
        (a, b, c), _ = _seq_bwd("dil_attn_bwd", f_dil_attn, Rg, rows, [], [], douts, [], period=S // r // Rg)
        dq.append(_unperm(a, r)); dk.append(_unperm(b, r)); dv.append(_unperm(c, r))
    prep_rows, prep_pars = res['prep']
    res = _seq_bwd("dil_prep_bwd", f_dil_prep, _tile(S, (256, 128)), prep_rows, prep_pars, [],
                   [_row_spec(t) for t in dq + dk + dv], [], dx_dt=BF, exchange=exchange)
    (dp,), (dqn, dkn) = res[0], res[1]
    return dp, dict(b_q_norm=dqn.reshape(-1), b_k_norm=dkn.reshape(-1)), (res[2] if exchange else exchange_local)


def _ffn_specs(u, cw, cb):
    half = N_DEV // 2
    rows = [_row_spec(u, prev='halo', lb=(2, None), li=lambda jc: (0, jc))]
    pars = [_par_spec(cw, bs=(None, FFN_CONV, FF_SH), idx=lambda jc: (jc, 0, 0)),
            _par_spec(cw, bs=(None, FFN_CONV, FF_SH), idx=lambda jc: (jc + half, 0, 0)),
            _par_spec(cb, bs=(None, 1, FF_SH), idx=lambda jc: (jc, 0, 0)),
            _par_spec(cb, bs=(None, 1, FF_SH), idx=lambda jc: (jc + half, 0, 0))]
    return half, rows, pars


N_MIX = {0: 2 * GLA_HEADS * GLA_DK + 2 * D_MIX + GLA_RANK, 1: 3 * len(DIL_GROUPS) * D_DIL,
         2: 2 * D_MIX + 2 * SSM_GROUPS * SSM_STATE + SSM_HEADS, 3: 2 * HGRN_HEADS * HGRN_DK + 2 * D_MIX}
W_IN = {0: 'a_w_in', 1: 'b_w_in', 2: 'c_w_in', 3: 'd_w_in'}
W_OUT = {0: 'a_w_out', 1: 'b_w_out', 2: 'c_w_out', 3: 'd_w_out'}


def _in_blocks(name, t):
    return t if SHARD_AXIS[name] == 1 else t.reshape(1, N_DEV * t.shape[1], t.shape[2])


LAYER_STACKED = ('ffn_w_up', 'ffn_conv_w', 'ffn_w_down', 'xa_w_kv')


SMALL_OF_KIND = {0: ['a_w_gate2'], 2: ['c_conv_w']}


def _layer_names(i):
    return list(LAYER_STACKED) + [W_IN[i % 4], W_OUT[i % 4]] + SMALL_OF_KIND.get(i % 4, [])


def _device_step(x, mem, pos, sh, rep, target, distributed=True):
    S, D = x.shape
    w = dict(rep)
    posf = pos.reshape(S, 1).astype(F32)
    n_mix, w_in_name, w_out_name = N_MIX, W_IN, W_OUT
    ntot = {k: -(-(n_mix[k] + D_XA) // 256) * 256 for k in n_mix}
    mem_g = w['mem_norm'].reshape(1, -1)
    mem_n = _norm_fwd(mem, mem_g)
    R = _rows_of(S)

    def mine(i):
        return {n: (sh[n][i] if n in LAYER_STACKED else sh[n]) for n in _layer_names(i)}

    if distributed:
        gl = dict(zip(_layer_names(0), _gather_many(list(mine(0).values()), "gather_weights")))
    else:
        gl = {n: (sh[n][:, 0] if n in LAYER_STACKED else sh[n]) for n in _layer_names(0)}

    saved = []
    for i in range(DEPTH):
        kind = i % 4
        L = dict(x0=x)
        for n in SMALL_OF_KIND.get(kind, []):
            w[n] = _to_full(gl[n], 1)
        in_blocks = _in_blocks(w_in_name[kind], gl[w_in_name[kind]])
        w_out = (_to_full(gl[w_out_name[kind]], 1) if SHARD_AXIS[w_out_name[kind]] == 1
                 else gl[w_out_name[kind]].reshape(-1, D))
        nxt, push = {}, [[], [], []]
        if i + 1 < DEPTH:
            if distributed:
                nxt = mine(i + 1)
                push = [[n for n in nxt if n not in ('ffn_w_up', w_in_name[(i + 1) % 4], w_out_name[(i + 1) % 4])],
                        ['ffn_w_up'], [w_in_name[(i + 1) % 4], w_out_name[(i + 1) % 4]]]
            else:
                nxt = {n: (sh[n][:, i + 1] if n in LAYER_STACKED else sh[n]) for n in _layer_names(i + 1)}
        got = dict(nxt) if not distributed else {}

        def hosted(call, names):
            if not names:
                return call()
            res, arrived = call(gather=[nxt[n] for n in names])
            got.update(zip(names, arrived))
            return res

        g1 = w['mix_norm'][i].reshape(1, -1)
        h = _norm_fwd(x, g1)
        wcat = _cat_cols(in_blocks, n_mix[kind], ntot[kind])
        p = hosted(functools.partial(_matmul, h, wcat, name="matmul_in"), push[0])
        ntok = D_DIL if kind == 1 else D_MIX
        if kind == 1:
            cat, L['dil'] = _dil_fwd(p, posf, w, ntok + D_XA)
        else:
            f, Rm, rows, pars, sshapes, _ = _mixer_specs(kind, S, p, w)
            (cat,), L['states'] = _seq_fwd("mixer%d_fwd" % kind, f, Rm, rows, pars, sshapes,
                                           [_out_spec(ntok + D_XA, w=ntok, dt=BF)], save_states=True)
        wkv = gl['xa_w_kv'].reshape(D, 2 * D_XA)
        kv = _matmul(mem_n, wkv, name="matmul_kv")
        xa_rows = [_row_spec(p, w=D_XA, dn=D_XA)]
        xa_pars = [_par_spec(kv), _par_spec(w['xa_q_norm'][i].reshape(1, -1)), _par_spec(w['xa_k_norm'][i].reshape(1, -1))]
        (cat,), _ = _seq_fwd("xattn_fwd", f_xattn, R, xa_rows, xa_pars, [],
                             [_out_spec(ntok + D_XA, w=D_XA, c=lambda jc: ntok // D_XA, dt=BF)], out_alias={0: cat})
        x1 = _matmul(cat, w_out, add=x, name="matmul_out")
        g2 = w['ffn_norm'][i].reshape(1, -1)
        h2 = _norm_fwd(x1, g2)
        wup = gl['ffn_w_up']
        u = hosted(functools.partial(_ffn_up, h2, wup), push[1])
        cw, cb = gl['ffn_conv_w'], w['ffn_conv_b'][i].reshape(N_DEV, 1, FF_SH)
        nt, frows, fpars = _ffn_specs(u, cw, cb)
        res = _seq_fwd("ffn_act_fwd", f_ffn_act, R, frows, fpars, [],
                       [_out_spec(FF_SH, dt=BF, ls=(nt,), lb=(None,), li=lambda jc: (jc,))], ncol=nt,
                       gather=[nxt[n] for n in push[2]] if push[2] else None)
        (act,) = res[0]
        if push[2]:
            got.update(zip(push[2], res[2]))
        wd = gl['ffn_w_down'].reshape(D_FF, D)
        x = _ffn_down(act, wd, x1)
        L.update(h=h, p=p, wcat=wcat, kv=kv, wkv=wkv, cat=cat, x1=x1, h2=h2, u=u, act=act, wd=wd, wup=wup, cw=cw, g1=g1,
                 g2=g2, in_blocks=in_blocks, w_out=w_out, shapes={n: t.shape for n, t in gl.items()})
        saved.append(L)
        gl = got

    dx, dxb, loss = _loss_head(x, target)

    G = {}
    d_mem_n = None
    acc = {k: [None] * DEPTH for k in ('mix_norm', 'ffn_norm', 'ffn_conv_b', 'xa_q_norm', 'xa_k_norm')}
    parts = [{} for _ in range(DEPTH)]
    pending = {}
    half = N_DEV // 2

    def sent(call, blocks, layer):
        if not blocks:
            return call()
        if not distributed:
            parts[layer].update(blocks)
            return call()
        res, arrived = call(exchange=list(blocks.values()))
        parts[layer].update(zip(blocks, arrived))
        return res

    for i in reversed(range(DEPTH)):
        kind = i % 4
        L = saved[i]
        Gc = {}
        Gc['ffn_w_down'] = _ffn_dw_down(L['act'], dxb).reshape(N_DEV, D_FF // N_DEV, D)
        dact = _ffn_dact(dxb, L['wd'])
        cw, cb = L['cw'], w['ffn_conv_b'][i].reshape(N_DEV, 1, FF_SH)
        nt, frows, fpars = _ffn_specs(L['u'], cw, cb)
        (du,), (dwg, dwv, dbg, dbv) = _seq_bwd(
            "ffn_act_bwd", f_ffn_act, R, frows, fpars, [], [_row_spec(dact, lb=(None,), li=lambda jc: (jc,))], [],
            ncol=nt, dx_dt=BF)
        Gc['ffn_conv_w'] = jnp.concatenate([dwg[:half], dwv[half:]], axis=0)
        acc['ffn_conv_b'][i] = jnp.concatenate([dbg[:half], dbv[half:]], axis=0).reshape(-1)
        Gc['ffn_w_up'] = _ffn_dw_up(L['h2'], du)
        dh2 = sent(functools.partial(_ffn_dh2, du, L['wup']), pending, i + 1)
        dx1, dx1b, dg2 = _norm_bwd(L['x1'], L['g2'], dh2, res=dx)
        acc['ffn_norm'][i] = dg2.reshape(-1)
        G_out = _matmul(L['cat'], dx1b, mode="tn", out_dtype=BF, name="matmul_dw_out")
        dcat = _matmul(dx1b, L['w_out'], mode="nt", name="matmul_dcat")
        ntok = D_DIL if kind == 1 else D_MIX
        dtok = _row_spec(dcat, w=ntok)
        dxa = _row_spec(dcat, w=D_XA, c=lambda jc: ntok // D_XA)
        p = L['p']
        up = [Gc.pop('ffn_w_up')]
        if kind == 1:
            dp, gm, got_up = _dil_bwd(dtok, L['dil'], p, up if distributed else None, up)
            G.update(gm)
        else:
            f, Rm, rows, pars, sshapes, _ = _mixer_specs(kind, S, p, w)
            res = _seq_bwd("mixer%d_bwd" % kind, f, Rm, rows, pars, sshapes, [dtok], L['states'], dx_dt=BF,
                           exchange=up if distributed else None)
            (dp,), dps, got_up = res[0], res[1], (res[2] if distributed else up)
            if kind == 0:
                Gc['a_w_gate2'], G['a_b_gate'], G['a_o_norm'] = _to_chunks(dps[0], 1), dps[1].reshape(-1), dps[2].reshape(-1)
            elif kind == 2:
                Gc['c_conv_w'] = _to_chunks(dps[0], 1)
                for nme, v in zip(('c_conv_b', 'c_dt_bias', 'c_a_log', 'c_d', 'c_norm'), dps[1:]):
                    G[nme] = v.reshape(-1)
            else:
                G['d_lower_bounds'], G['d_o_norm'] = dps[0], dps[1].reshape(-1)
        xa_rows = [_row_spec(p, w=D_XA)]
        xa_pars = [_par_spec(L['kv']), _par_spec(w['xa_q_norm'][i].reshape(1, -1)), _par_spec(w['xa_k_norm'][i].reshape(1, -1))]
        (dp,), (dkv, dqn, dkn) = _seq_bwd("xattn_bwd", f_xattn, R, xa_rows, xa_pars, [], [dxa], [], dx_dt=BF,
                                          dx_alias={0: dp})
        acc['xa_q_norm'][i], acc['xa_k_norm'][i] = dqn.reshape(-1), dkn.reshape(-1)
        Gc['xa_w_kv'] = _matmul(mem_n, dkv, mode="tn", out_dtype=BF, name="matmul_dw_kv").reshape(
            N_DEV, D // N_DEV, 2 * D_XA)
        d_mem_n = _matmul(dkv, L['wkv'], mode="nt", add=d_mem_n, name="matmul_dmem" + ("" if d_mem_n is None else "_acc"))
        parts[i]['ffn_w_up'] = got_up[0]
        dwcat = _matmul(L['h'], dp, mode="tn", out_dtype=BF, name="matmul_dw_in")
        blocks = L['in_blocks']
        Gc[w_in_name[kind]] = _uncat_cols(dwcat, blocks.shape[0], blocks.shape[2], n_mix[kind]).reshape(
            L['shapes'][w_in_name[kind]])
        Gc[w_out_name[kind]] = (_to_chunks(G_out, 1) if SHARD_AXIS[w_out_name[kind]] == 1
                                else G_out.reshape(L['shapes'][w_out_name[kind]]))
        dh = sent(functools.partial(_matmul, dp, L['wcat'], mode="nt", name="matmul_dh"),
                  {n: Gc.pop(n) for n in ('ffn_w_down', 'ffn_conv_w')}, i)
        dx, dxb, dg1 = _norm_bwd(L['x0'], L['g1'], dh, res=dx1)
        acc['mix_norm'][i] = dg1.reshape(-1)
        pending = Gc

    if distributed:
        names = list(pending)
        parts[0].update(zip(names, _exchange_many([pending[n] for n in names], "exchange_grads")))
    else:
        parts[0].update(pending)
    _, _, dmg = _norm_bwd(mem, mem_g, d_mem_n)
    G['mem_norm'] = dmg.reshape(-1)
    for k, v in acc.items():
        G[k] = jnp.stack(v)
    got = {}
    for i in range(DEPTH):
        for n, t in parts[i].items():
            if n not in LAYER_STACKED:
                got[n] = t
    for n in LAYER_STACKED:
        got[n] = jnp.stack([parts[i][n] for i in range(DEPTH)], axis=1)
    return loss, dx, got, G


def kernel(x, mem, positions, mem_norm, mix_norm, xa_w_kv, xa_q_norm, xa_k_norm, ffn_norm, ffn_w_up, ffn_conv_w, ffn_conv_b, ffn_w_down, a_w_in, a_w_gate2, a_b_gate, a_o_norm, a_w_out, b_w_in, b_q_norm, b_k_norm, b_w_out, c_w_in, c_conv_w, c_conv_b, c_dt_bias, c_a_log, c_d, c_norm, c_w_out, d_w_in, d_lower_bounds, d_o_norm, d_w_out, loss_target, m_mem_norm, m_mix_norm, m_xa_w_kv, m_xa_q_norm, m_xa_k_norm, m_ffn_norm, m_ffn_w_up, m_ffn_conv_w, m_ffn_conv_b, m_ffn_w_down, m_a_w_in, m_a_w_gate2, m_a_b_gate, m_a_o_norm, m_a_w_out, m_b_w_in, m_b_q_norm, m_b_k_norm, m_b_w_out, m_c_w_in, m_c_conv_w, m_c_conv_b, m_c_dt_bias, m_c_a_log, m_c_d, m_c_norm, m_c_w_out, m_d_w_in, m_d_lower_bounds, m_d_o_norm, m_d_w_out, v_mem_norm, v_mix_norm, v_xa_w_kv, v_xa_q_norm, v_xa_k_norm, v_ffn_norm, v_ffn_w_up, v_ffn_conv_w, v_ffn_conv_b, v_ffn_w_down, v_a_w_in, v_a_w_gate2, v_a_b_gate, v_a_o_norm, v_a_w_out, v_b_w_in, v_b_q_norm, v_b_k_norm, v_b_w_out, v_c_w_in, v_c_conv_w, v_c_conv_b, v_c_dt_bias, v_c_a_log, v_c_d, v_c_norm, v_c_w_out, v_d_w_in, v_d_lower_bounds, v_d_o_norm, v_d_w_out):
    args = locals()
    w = {n: args[n] for n in WEIGHTS}
    m = {n: args['m_' + n] for n in WEIGHTS}
    v = {n: args['v_' + n] for n in WEIGHTS}

    big = [n for n in SHARDED if w[n].size >= 65536]
    small = [n for n in SHARDED if n not in big]
    sh = {n: (w[n].astype(BF) if n in big else w[n]) for n in SHARDED}
    loss, grad_x, parts, G = _device_step(x[0], mem[0], positions[0], sh, {n: w[n] for n in REPLICATED}, loss_target[0])
    loss = lax.psum(loss, ("x", "y", "c"))
    rep_parts = _all_gather(_pack([G[n] for n in REPLICATED], F32), "gather_replicated_grads")

    out = {}

    def put(names, res, shapes):
        for kind, r in zip(("grad", "delta", "new_m", "new_v"), res):
            for n, t in zip(names, _unpack(r, shapes)):
                out[kind + "_" + n] = t

    for n in big:
        shp = tuple(w[n].shape)
        two_d = (math.prod(shp[:-1]), shp[-1])
        res = _adamw(parts[n].reshape((N_DEV,) + two_d), w[n].reshape(two_d), m[n].reshape(two_d), v[n].reshape(two_d),
                     "adamw")
        for kind, r in zip(("grad", "delta", "new_m", "new_v"), res):
            out[kind + "_" + n] = r.reshape(shp)
    for names, prt, tag in ((small, _pack_lead([parts[n] for n in small], F32), "adamw_small"),
                            (REPLICATED, rep_parts, "adamw_replicated")):
        res = _adamw(prt, _pack([w[n] for n in names], F32), _pack([m[n] for n in names], F32),
                     _pack([v[n] for n in names], F32), tag)
        put(names, res, [tuple(w[n].shape) for n in names])
    return (loss, grad_x[None], *[out["grad_" + n] for n in WEIGHTS], *[out["delta_" + n] for n in WEIGHTS],
            *[out["new_m_" + n] for n in WEIGHTS], *[out["new_v_" + n] for n in WEIGHTS])
```

```python
import functools
import math

import jax
import jax.numpy as jnp
from jax import lax
from jax.experimental import pallas as pl
from jax.experimental.pallas import tpu as pltpu

F32 = jnp.float32
BF = jnp.bfloat16
_MM_DTYPE = BF

N_DEV = 8
EPS = 1e-6
ROPE_THETA = 10000.0
CHUNK = 64
MIX_ROWS = 256
DIL_ROWS = 512
D_MIX = 768
XA_HEADS, XA_HD, D_XA = 4, 64, 256
GLA_HEADS, GLA_DK, GLA_DV, GLA_RANK, GLA_GATE_NORM = 4, 96, 192, 16, 16.0
DIL_GROUPS = ((128, 1), (512, 4), (2048, 16))
DIL_HEADS, DIL_HD, DIL_BLOCK, D_DIL = 4, 128, 128, 512
SSM_HD, SSM_HEADS, SSM_GROUPS, SSM_STATE, SSM_CONV = 64, 12, 2, 128, 4
HGRN_HEADS, HGRN_DK, HGRN_DV = 6, 128, 128
D_FF = 2816
FFN_CONV = 3
DEPTH = 4
ADAM_LR, ADAM_B1, ADAM_B2, ADAM_EPS, ADAM_WD, ADAM_STEP = 0.001, 0.9, 0.999, 1e-08, 0.01, 10
NEG = -1e30
HALO = 8
VMEM_LIMIT = 56 << 20
ADAM_BLOCK = 1 << 18

WEIGHTS = ['mem_norm', 'mix_norm', 'xa_w_kv', 'xa_q_norm', 'xa_k_norm', 'ffn_norm', 'ffn_w_up', 'ffn_conv_w',
           'ffn_conv_b', 'ffn_w_down', 'a_w_in', 'a_w_gate2', 'a_b_gate', 'a_o_norm', 'a_w_out', 'b_w_in', 'b_q_norm',
           'b_k_norm', 'b_w_out', 'c_w_in', 'c_conv_w', 'c_conv_b', 'c_dt_bias', 'c_a_log', 'c_d', 'c_norm', 'c_w_out',
           'd_w_in', 'd_lower_bounds', 'd_o_norm', 'd_w_out']
SHARD_AXIS = {'xa_w_kv': 1, 'ffn_w_up': 2, 'ffn_conv_w': 2, 'ffn_w_down': 1, 'a_w_in': 1, 'a_w_gate2': 1, 'a_w_out': 0,
              'b_w_in': 1, 'b_w_out': 1, 'c_w_in': 0, 'c_conv_w': 1, 'c_w_out': 0, 'd_w_in': 1, 'd_w_out': 0}
SHARDED = [n for n in WEIGHTS if n in SHARD_AXIS]
REPLICATED = [n for n in WEIGHTS if n not in SHARD_AXIS]


def _dot(a, b, ca, cb):
    return lax.dot_general(a.astype(_MM_DTYPE), b.astype(_MM_DTYPE), (((ca,), (cb,)), ((), ())),
                           preferred_element_type=F32)


@jax.custom_vjp
def mm_nn(a, b):
    return _dot(a, b, 1, 0)


mm_nn.defvjp(lambda a, b: (_dot(a, b, 1, 0), (a, b)),
             lambda r, g: (_dot(g, r[1], 1, 1), _dot(r[0], g, 0, 0)))


@jax.custom_vjp
def mm_nt(a, b):
    return _dot(a, b, 1, 1)


mm_nt.defvjp(lambda a, b: (_dot(a, b, 1, 1), (a, b)),
             lambda r, g: (_dot(g, r[1], 1, 0), _dot(g, r[0], 0, 0)))


@jax.custom_vjp
def mm_tn(a, b):
    return _dot(a, b, 0, 0)


mm_tn.defvjp(lambda a, b: (_dot(a, b, 0, 0), (a, b)),
             lambda r, g: (_dot(r[1], g, 1, 1), _dot(r[0], g, 1, 0)))


def _dot_hi(a, b, ca, cb):
    return lax.dot_general(a, b, (((ca,), (cb,)), ((), ())), precision=lax.Precision.HIGHEST,
                           preferred_element_type=F32)


def _tril(c):
    return lax.broadcasted_iota(jnp.int32, (c, c), 0) >= lax.broadcasted_iota(jnp.int32, (c, c), 1)


@jax.custom_vjp
def cumsum_rows(x):
    return _dot_hi(_tril(x.shape[0]).astype(F32), x, 1, 0)


cumsum_rows.defvjp(lambda x: (cumsum_rows(x), None),
                   lambda r, g: (_dot_hi(_tril(g.shape[0]).astype(F32), g, 0, 0),))


@jax.custom_vjp
def cumsum_rows_t(x):
    return _dot_hi(x, _tril(x.shape[0]).astype(F32), 0, 1)


cumsum_rows_t.defvjp(lambda x: (cumsum_rows_t(x), None),
                     lambda r, g: (_dot_hi(_tril(g.shape[1]).astype(F32), g, 0, 1),))


def _split(x, sizes):
    sizes = tuple(int(s) for s in sizes)
    assert sum(sizes) == x.shape[-1], (sizes, x.shape)

    @jax.custom_vjp
    def sp(x):
        out, o = [], 0
        for s in sizes:
            out.append(x[:, o:o + s])
            o += s
        return tuple(out)

    sp.defvjp(lambda x: (sp(x), None), lambda r, g: (jnp.concatenate(list(g), axis=1),))
    return sp(x)


def _row(x, r):
    m = lax.broadcasted_iota(jnp.int32, x.shape, 0) == r
    return jnp.sum(jnp.where(m, x, 0.0), axis=0, keepdims=True)


@jax.custom_vjp
def _roll_half(x):
    return pltpu.roll(x, 64, 1)


_roll_half.defvjp(lambda x: (pltpu.roll(x, 64, 1), None), lambda r, g: (pltpu.roll(g, 64, 1),))


def _shift(xp, x, d):
    if d == 0:
        return x
    n, m = x.shape[0], xp.shape[0]
    assert d <= m == HALO

    @jax.custom_vjp
    def sh(xp, x):
        r = pltpu.roll(x, d, 0)
        row = lax.broadcasted_iota(jnp.int32, xp.shape, 0)
        head = jnp.where(row < d, pltpu.roll(xp, d, 0), r[:m])
        return jnp.concatenate([head, r[m:]], axis=0)

    def bwd(_, g):
        row = lax.broadcasted_iota(jnp.int32, g.shape, 0)
        rowp = lax.broadcasted_iota(jnp.int32, (m,) + g.shape[1:], 0)
        dxp = jnp.where(rowp >= m - d, pltpu.roll(g[:m], m - d, 0), 0.0)
        return dxp, jnp.where(row < n - d, pltpu.roll(g, n - d, 0), 0.0)

    sh.defvjp(lambda xp, x: (sh(xp, x), None), bwd)
    return sh(xp, x)


def _rms(x, g):
    return x * lax.rsqrt(jnp.mean(x * x, axis=-1, keepdims=True) + EPS) * g


def _lane_pair(a, b, width=128):
    shape = a.shape[:-1] + (width,)
    lane = lax.broadcasted_iota(jnp.int32, shape, len(shape) - 1)
    return jnp.where(lane < width // 2, a, b)


def _row_spec(a, w=None, c=None, prev=False, diff=True, dn=None, lb=(), li=None):
    return dict(a=a, w=a.shape[-1] if w is None else w, c=(lambda jc: 0) if c is None else c, prev=prev, diff=diff,
                dn=a.shape[-1] if dn is None else dn, lb=tuple(lb), li=(lambda jc: ()) if li is None else li)


def _par_spec(a, bs=None, idx=None, diff=True):
    nd = a.ndim
    return dict(a=a, bs=tuple(a.shape) if bs is None else tuple(bs),
                idx=(lambda jc: (0,) * nd) if idx is None else idx, diff=diff)


def _out_spec(n, w=None, c=None, dt=F32, ls=(), lb=(), li=None):
    return dict(n=n, w=n if w is None else w, c=(lambda jc: 0) if c is None else c, dt=dt, ls=tuple(ls), lb=tuple(lb),
                li=(lambda jc: ()) if li is None else li)


def _cparams():
    return pltpu.CompilerParams(dimension_semantics=("arbitrary", "arbitrary"), vmem_limit_bytes=VMEM_LIMIT)


def _bspec(s, R, rowfn):
    return pl.BlockSpec(s['lb'] + (R, s['w']),
                        functools.partial(lambda jc, i, s: tuple(s['li'](jc)) + (rowfn(i), s['c'](jc)), s=s))


def _prev_rows(s, R):
    return R if s['prev'] == 'block' else HALO


def _pspec(s, R, blockfn):
    pr = _prev_rows(s, R)
    return pl.BlockSpec(s['lb'] + (pr, s['w']), functools.partial(
        lambda jc, i, s: tuple(s['li'](jc)) + (jnp.maximum(blockfn(i) * (R // pr) - 1, 0), s['c'](jc)), s=s))


def _seq_fwd(name, f, R, rows, params, state_shapes, outs, *, ncol=1, period=None, save_states=False, out_alias=None,
             gather=None):
    nrows = rows[0]['a'].shape[-2]
    nb = nrows // R
    assert nb * R == nrows
    period = nb if period is None else period
    prev_ids = [k for k, r in enumerate(rows) if r['prev']]
    n_rows, n_prev, n_par, n_out, n_st = len(rows), len(prev_ids), len(params), len(outs), len(state_shapes)
    ex = list(gather or [])
    n_ex = len(ex)

    def body(*refs):
        o = 0
        cur = refs[o:o + n_rows]; o += n_rows
        prv = refs[o:o + n_prev]; o += n_prev
        par = refs[o:o + n_par]; o += n_par + len(out_alias or {})
        exg = refs[o:o + n_ex]; o += n_ex
        out = refs[o:o + n_out]; o += n_out
        sav = refs[o:o + (n_st if save_states else 0)]; o += len(sav)
        exr = refs[o:o + n_ex]; o += n_ex
        st = refs[o:o + n_st]; o += n_st
        sems = refs[o:]
        i = pl.program_id(1)
        first = (i % period) == 0
        if n_ex:
            @pl.when((pl.program_id(0) == 0) & (i == 0))
            def _():
                _exchange_start(exg, exr, *sems)

        @pl.when(i == 0)
        def _():
            for s in st:
                s[...] = jnp.zeros_like(s)

        xs = [r[...].astype(F32) for r in cur]
        xp = [r[...].astype(F32) for r in prv]
        ps = [r[...] for r in par]
        sts = [s[...] for s in st]
        for sv, s in zip(sav, sts):
            sv[0] = s
        ov, ns = f(first, xp, xs, ps, sts)
        for r, v in zip(out, ov):
            r[...] = v.astype(r.dtype)
        for s, v in zip(st, ns):
            s[...] = v
        if n_ex:
            @pl.when((pl.program_id(0) == ncol - 1) & (i == nb - 1))
            def _():
                _exchange_wait(exg, exr, *sems)

    in_specs = [_bspec(r, R, lambda i: i) for r in rows]
    in_specs += [_pspec(rows[k], R, lambda i: i) for k in prev_ids]
    in_specs += [pl.BlockSpec(p['bs'], functools.partial(lambda jc, i, idx: idx(jc), idx=p['idx'])) for p in params]
    out_specs = [_bspec(o_, R, lambda i: i) for o_ in outs]
    out_shape = [jax.ShapeDtypeStruct(o_['ls'] + (nrows, o_['n']), o_['dt']) for o_ in outs]
    if save_states:
        for s in state_shapes:
            out_specs.append(pl.BlockSpec((1,) + tuple(s), lambda jc, i, nd=len(s): (i,) + (0,) * nd))
            out_shape.append(jax.ShapeDtypeStruct((nb,) + tuple(s), F32))
    args = [r['a'] for r in rows] + [rows[k]['a'] for k in prev_ids] + [p['a'] for p in params]
    aliases = {}
    for n_, arr in sorted((out_alias or {}).items()):
        assert arr.shape == out_shape[n_].shape and arr.dtype == out_shape[n_].dtype
        aliases[len(args)] = n_
        args.append(arr)
        in_specs.append(pl.BlockSpec(memory_space=pl.ANY))
    n_sav = len(out_shape) - n_out
    in_specs += [pl.BlockSpec(memory_space=pl.ANY)] * n_ex
    out_specs += [pl.BlockSpec(memory_space=pl.ANY)] * n_ex
    out_shape += [jax.ShapeDtypeStruct((N_DEV,) + tuple(g.shape), g.dtype) for g in ex]
    cp = pltpu.CompilerParams(dimension_semantics=("arbitrary", "arbitrary"), vmem_limit_bytes=VMEM_LIMIT,
                              has_side_effects=bool(n_ex))
    res = pl.pallas_call(
        body, name=name, grid=(ncol, nb), in_specs=in_specs, out_specs=out_specs, out_shape=out_shape,
        scratch_shapes=[pltpu.VMEM(tuple(s), F32) for s in state_shapes] + (_exchange_sems(n_ex) if n_ex else []),
        input_output_aliases=aliases, compiler_params=cp)(*args, *ex)
    if n_ex:
        return list(res[:n_out]), list(res[n_out:n_out + n_sav]), list(res[n_out + n_sav:])
    return list(res[:n_out]), list(res[n_out:])


def _seq_bwd(name, f, R, rows, params, state_shapes, douts, saved, *, ncol=1, period=None, dx_dt=F32, dx_add=None,
             dx_bf=False, dx_alias=None, exchange=None):
    ex = list(exchange or [])
    n_ex = len(ex)
    nrows = rows[0]['a'].shape[-2]
    nb = nrows // R
    period = nb if period is None else period
    prev_ids = [k for k, r in enumerate(rows) if r['prev']]
    drow_ids = [k for k, r in enumerate(rows) if r['diff']]
    dpar_ids = [k for k, p in enumerate(params) if p['diff']]
    for k in prev_ids:
        assert rows[k]['diff']
    dx_add, dx_alias = dict(dx_add or {}), dict(dx_alias or {})
    add_ids, alias_ids = sorted(dx_add), sorted(dx_alias)
    n_rows, n_prev, n_par, n_do, n_st = len(rows), len(prev_ids), len(params), len(douts), len(state_shapes)
    n_dx, n_dp, n_add, n_al = len(drow_ids), len(dpar_ids), len(add_ids), len(alias_ids)

    def body(*refs):
        o = 0
        cur = refs[o:o + n_rows]; o += n_rows
        prv = refs[o:o + n_prev]; o += n_prev
        par = refs[o:o + n_par]; o += n_par
        sav = refs[o:o + n_st]; o += n_st
        dou = refs[o:o + n_do]; o += n_do
        adr = refs[o:o + n_add]; o += n_add
        o += n_al
        exg = refs[o:o + n_ex]; o += n_ex
        dxr = refs[o:o + n_dx]; o += n_dx
        dpr = refs[o:o + n_dp]; o += n_dp
        dxb = refs[o:o + (n_dx if dx_bf else 0)]; o += len(dxb)
        exr = refs[o:o + n_ex]; o += n_ex
        dst = refs[o:o + n_st]; o += n_st
        car = refs[o:o + n_prev]; o += n_prev
        sems = refs[o:]
        j = pl.program_id(1)
        i = nb - 1 - j
        first = (i % period) == 0
        if n_ex:
            @pl.when((pl.program_id(0) == 0) & (j == 0))
            def _():
                _exchange_start(exg, exr, *sems)

        @pl.when(j == 0)
        def _():
            for s in tuple(dst) + tuple(car) + tuple(dpr):
                s[...] = jnp.zeros_like(s)

        xs = [r[...].astype(F32) for r in cur]
        xp = [r[...].astype(F32) for r in prv]
        ps = [r[...] for r in par]
        sts = [s[0] for s in sav]

        def g(dxs, dxp, dps, dsts):
            xs_, ps_ = list(xs), list(ps)
            for k, v in zip(drow_ids, dxs):
                xs_[k] = v
            for k, v in zip(dpar_ids, dps):
                ps_[k] = v
            ov, ns = f(first, list(dxp), xs_, ps_, list(dsts))
            return tuple(ov), tuple(ns)

        _, vjp = jax.vjp(g, tuple(xs[k] for k in drow_ids), tuple(xp), tuple(ps[k] for k in dpar_ids), tuple(sts))
        dxs, dxp, dps, dsts = vjp((tuple(r[...].astype(F32) for r in dou), tuple(s[...] for s in dst)))
        dxs = list(dxs)
        for n_, pos in enumerate(add_ids):
            dxs[pos] = dxs[pos] + adr[n_][...].astype(F32)
        tails = {}
        for n_, k in enumerate(prev_ids):
            pos = drow_ids.index(k)
            if rows[k]['prev'] == 'block':
                dxs[pos] = dxs[pos] + car[n_][...]
            else:
                tails[pos] = car[n_][...]
            car[n_][...] = dxp[n_]
        for pos, v in enumerate(dxs):
            outs_ = [dxr[pos]] + ([dxb[pos]] if dx_bf else [])
            if pos in tails:
                v = jnp.concatenate([v[..., :R - HALO, :], v[..., R - HALO:, :] + tails[pos]], axis=-2)
            for r in outs_:
                r[...] = v.astype(r.dtype)
        for r, v in zip(dpr, dps):
            r[...] += v
        for s, v in zip(dst, dsts):
            s[...] = v
        if n_ex:
            @pl.when((pl.program_id(0) == ncol - 1) & (j == nb - 1))
            def _():
                _exchange_wait(exg, exr, *sems)

    def rev(j):
        return nb - 1 - j

    def dspec(k):
        return _bspec(rows[k], R, rev)

    in_specs = [_bspec(r, R, rev) for r in rows]
    in_specs += [_pspec(rows[k], R, rev) for k in prev_ids]
    in_specs += [pl.BlockSpec(p['bs'], functools.partial(lambda jc, j, idx: idx(jc), idx=p['idx'])) for p in params]
    in_specs += [pl.BlockSpec((1,) + tuple(s), lambda jc, j, nd=len(s): (nb - 1 - j,) + (0,) * nd) for s in state_shapes]
    in_specs += [_bspec(d, R, rev) for d in douts]
    in_specs += [dspec(drow_ids[pos]) for pos in add_ids]
    in_specs += [pl.BlockSpec(memory_space=pl.ANY) for _ in alias_ids + ex]
    out_specs = [dspec(k) for k in drow_ids]
    out_shape = [jax.ShapeDtypeStruct(tuple(rows[k]['a'].shape[:-1]) + (rows[k]['dn'],), dx_dt) for k in drow_ids]
    for k in dpar_ids:
        p = params[k]
        out_specs.append(pl.BlockSpec(p['bs'], functools.partial(lambda jc, j, idx: idx(jc), idx=p['idx'])))
        out_shape.append(jax.ShapeDtypeStruct(p['a'].shape, F32))
    if dx_bf:
        out_specs += [dspec(k) for k in drow_ids]
        out_shape += [jax.ShapeDtypeStruct(tuple(rows[k]['a'].shape[:-1]) + (rows[k]['dn'],), BF) for k in drow_ids]
    out_specs += [pl.BlockSpec(memory_space=pl.ANY) for _ in ex]
    out_shape += [jax.ShapeDtypeStruct(g.shape, g.dtype) for g in ex]
    scratch = [pltpu.VMEM(tuple(s), F32) for s in state_shapes]
    scratch += [pltpu.VMEM(tuple(d for d in rows[k]['lb'] if d is not None) + (_prev_rows(rows[k], R), rows[k]['w']), F32)
                for k in prev_ids]
    if n_ex:
        scratch += _exchange_sems(n_ex)
    args = ([r['a'] for r in rows] + [rows[k]['a'] for k in prev_ids] + [p['a'] for p in params] + list(saved)
            + [d['a'] for d in douts] + [dx_add[pos] for pos in add_ids] + [dx_alias[pos] for pos in alias_ids])
    n_in = len(args)
    aliases = {n_in - n_al + n_: pos for n_, pos in enumerate(alias_ids)}
    for pos in alias_ids:
        assert dx_alias[pos].shape == out_shape[pos].shape and dx_alias[pos].dtype == out_shape[pos].dtype
    cp = pltpu.CompilerParams(dimension_semantics=("arbitrary", "arbitrary"), vmem_limit_bytes=VMEM_LIMIT,
                              has_side_effects=bool(n_ex))
    res = pl.pallas_call(
        body, name=name, grid=(ncol, nb), in_specs=in_specs, out_specs=out_specs, out_shape=out_shape,
        scratch_shapes=scratch, input_output_aliases=aliases, compiler_params=cp)(*args, *ex)
    lists = [list(res[:n_dx]), list(res[n_dx:n_dx + n_dp])]
    o = n_dx + n_dp
    if dx_bf:
        lists.append(list(res[o:o + n_dx]))
        o += n_dx
    if n_ex:
        lists.append(list(res[o:o + n_ex]))
    return tuple(lists)


def _tile(n, cands):
    for c in cands:
        if n % c == 0:
            return c
    return n


def _mm_call(name, grid, a, a_spec, b, b_spec, contract, out_shape, out_spec, acc_shape, add=None, add_spec=None,
             exchange=None, gather=None):
    nk = grid[2]
    ca, cb = contract
    has_add = add is not None
    ex = list(exchange or []) + list(gather or [])
    ex_shapes = [g.shape for g in exchange or []] + [(N_DEV,) + tuple(g.shape) for g in gather or []]
    n_ex = len(ex)
    n_in = 2 + has_add

    def body(*refs):
        a_ref, b_ref = refs[0], refs[1]
        add_ref = refs[2] if has_add else None
        o_ref = refs[n_in + n_ex]
        scr = refs[n_in + 2 * n_ex + 1:]
        step = [pl.program_id(d) for d in range(3)]
        if n_ex:
            g_refs, r_refs, sems = refs[n_in:n_in + n_ex], refs[n_in + n_ex + 1:n_in + 2 * n_ex + 1], scr[-3:]

            @pl.when((step[0] == 0) & (step[1] == 0) & (step[2] == 0))
            def _():
                _exchange_start(g_refs, r_refs, *sems)

        part = _dot(a_ref[...], b_ref[...], ca, cb)

        def finish(r):
            if has_add:
                r = r + add_ref[...].astype(F32)
            o_ref[...] = r.astype(o_ref.dtype)

        if nk == 1:
            finish(part)
        else:
            acc = scr[0]

            @pl.when(step[2] == 0)
            def _():
                acc[...] = part

            @pl.when(step[2] > 0)
            def _():
                acc[...] += part

            @pl.when(step[2] == nk - 1)
            def _():
                finish(acc[...])

        if n_ex:
            @pl.when((step[0] == grid[0] - 1) & (step[1] == grid[1] - 1) & (step[2] == grid[2] - 1))
            def _():
                _exchange_wait(g_refs, r_refs, *sems)

    in_specs, args = [a_spec, b_spec], [a, b]
    if has_add:
        in_specs.append(add_spec)
        args.append(add)
    any_spec = pl.BlockSpec(memory_space=pl.ANY)
    scratch = [] if nk == 1 else [pltpu.VMEM(acc_shape, F32)]
    if n_ex:
        scratch += _exchange_sems(n_ex)
    res = pl.pallas_call(
        body, name=name, grid=grid, in_specs=in_specs + [any_spec] * n_ex, out_specs=[out_spec] + [any_spec] * n_ex,
        out_shape=[out_shape] + [jax.ShapeDtypeStruct(s, g.dtype) for s, g in zip(ex_shapes, ex)], scratch_shapes=scratch,
        compiler_params=pltpu.CompilerParams(
            dimension_semantics=("arbitrary",) * 3 if n_ex else ("parallel", "parallel", "arbitrary"),
            vmem_limit_bytes=VMEM_LIMIT, has_side_effects=bool(n_ex)))(*args, *ex)
    return (res[0], list(res[1:])) if n_ex else res[0]


def _matmul(a, b, mode="nn", add=None, out_dtype=F32, name="matmul", **pushed):
    if mode == "nn":
        (M, K), N = a.shape, b.shape[1]
    elif mode == "nt":
        (M, K), N = a.shape, b.shape[0]
    else:
        (K, M), N = a.shape, b.shape[1]
    if mode == "tn" and 1024 < N <= 5120:
        tm, tn = _tile(M, (512, 256, 128, 64, 32, 16, 8)), N
        tk = _tile(K, (2048 if tn <= 3072 else 1024, 1024, 512, 256, 128))
    else:
        tk = K if K <= 5120 else _tile(K, (2048, 1024, 512, 256, 128))
        tm = _tile(M, ((2048,) if tk <= 1024 and mode != "tn" else ()) + (1024, 512, 256, 128, 64, 32, 16, 8))
        tn = _tile(N, (512, 256, 128))
    if mode == "tn":
        a_spec = pl.BlockSpec((tk, tm), lambda i, j, k: (k, i))
    else:
        a_spec = pl.BlockSpec((tm, tk), lambda i, j, k: (i, k))
    if mode == "nt":
        b_spec = pl.BlockSpec((tn, tk), lambda i, j, k: (j, k))
    else:
        b_spec = pl.BlockSpec((tk, tn), lambda i, j, k: (k, j))
    blk = pl.BlockSpec((tm, tn), lambda i, j, k: (i, j))
    return _mm_call(name, (M // tm, N // tn, K // tk), a, a_spec, b, b_spec,
                    {"nn": (1, 0), "nt": (1, 1), "tn": (0, 0)}[mode], jax.ShapeDtypeStruct((M, N), out_dtype), blk,
                    (tm, tn), add, blk, **pushed)


FF_SH = 2 * D_FF // N_DEV


def _ffn_up(h2, wup, **pushed):
    S, D = h2.shape
    tm = _tile(S, (2048, 1024, 512, 256, 128))
    return _mm_call("matmul_up", (S // tm, N_DEV, 1), h2, pl.BlockSpec((tm, D), lambda m, j, k: (m, 0)),
                    wup, pl.BlockSpec((None, D, FF_SH), lambda m, j, k: (j, 0, 0)), (1, 0),
                    jax.ShapeDtypeStruct((2, N_DEV // 2, S, FF_SH), F32),
                    pl.BlockSpec((None, None, tm, FF_SH), lambda m, j, k: (j // 4, j % 4, m, 0)), (tm, FF_SH), **pushed)


def _ffn_down(act, wd, x1, **pushed):
    _, S, _ = act.shape
    D = wd.shape[1]
    tm, tn = _tile(S, (1024, 512, 256, 128)), _tile(D, (1024, 512, 256, 128))
    blk = pl.BlockSpec((tm, tn), lambda m, n, k: (m, n))
    return _mm_call("matmul_down", (S // tm, D // tn, N_DEV // 2), act,
                    pl.BlockSpec((None, tm, FF_SH), lambda m, n, k: (k, m, 0)), wd,
                    pl.BlockSpec((FF_SH, tn), lambda m, n, k: (k, n)), (1, 0), jax.ShapeDtypeStruct((S, D), F32), blk,
                    (tm, tn), x1, blk, **pushed)


def _ffn_dact(dxb, wd):
    S, D = dxb.shape
    tm = _tile(S, (2048, 1024, 512, 256, 128))
    return _mm_call("matmul_dact", (S // tm, N_DEV // 2, 1), dxb, pl.BlockSpec((tm, D), lambda m, j, k: (m, 0)), wd,
                    pl.BlockSpec((FF_SH, D), lambda m, j, k: (j, 0)), (1, 1),
                    jax.ShapeDtypeStruct((N_DEV // 2, S, FF_SH), BF),
                    pl.BlockSpec((None, tm, FF_SH), lambda m, j, k: (j, m, 0)), (tm, FF_SH))


def _ffn_dw_down(act, dxb):
    _, S, _ = act.shape
    D = dxb.shape[1]
    tk, tn = _tile(S, (2048, 1024, 512, 256, 128)), _tile(D, (512, 256, 128))
    return _mm_call("matmul_dw_down", (N_DEV // 2, D // tn, S // tk), act,
                    pl.BlockSpec((None, tk, FF_SH), lambda j, n, k: (j, k, 0)), dxb,
                    pl.BlockSpec((tk, tn), lambda j, n, k: (k, n)), (0, 0), jax.ShapeDtypeStruct((D_FF, D), BF),
                    pl.BlockSpec((FF_SH, tn), lambda j, n, k: (j, n)), (FF_SH, tn))


def _ffn_dw_up(h2, du, exchange=None):
    S, D = h2.shape
    tk = _tile(S, (2048, 1024, 512, 256, 128))
    return _mm_call("matmul_dw_up", (N_DEV, 1, S // tk), h2, pl.BlockSpec((tk, D), lambda j, n, k: (k, 0)), du,
                    pl.BlockSpec((None, None, tk, FF_SH), lambda j, n, k: (j // 4, j % 4, k, 0)), (0, 0),
                    jax.ShapeDtypeStruct((N_DEV, D, FF_SH), BF),
                    pl.BlockSpec((None, D, FF_SH), lambda j, n, k: (j, 0, 0)), (D, FF_SH), exchange=exchange)


def _ffn_dh2(du, wup, exchange=None):
    S = du.shape[2]
    D = wup.shape[1]
    tm = _tile(S, (1024, 512, 256, 128))
    return _mm_call("matmul_dh2", (S // tm, 1, N_DEV), du,
                    pl.BlockSpec((None, None, tm, FF_SH), lambda m, n, k: (k // 4, k % 4, m, 0)), wup,
                    pl.BlockSpec((None, D, FF_SH), lambda m, n, k: (k, 0, 0)), (1, 1),
                    jax.ShapeDtypeStruct((S, D), F32), pl.BlockSpec((tm, D), lambda m, n, k: (m, 0)), (tm, D),
                    exchange=exchange)


def f_rmsnorm(first, xp, xs, ps, sts):
    return (_rms(xs[0], ps[0]),), ()


def _same_block(shape, rows_per, cols_per):
    return (lax.broadcasted_iota(jnp.int32, shape, 0) // rows_per) == (lax.broadcasted_iota(jnp.int32, shape, 1) // cols_per)


@jax.custom_vjp
def _head_mean(x):
    n = x.shape[1]
    return _dot_hi(x, jnp.where(_same_block((n, n), XA_HD, XA_HD), 1.0 / XA_HD, 0.0), 1, 0)


_head_mean.defvjp(lambda x: (_head_mean(x), None), lambda r, g: (_head_mean(g),))


def f_xattn(first, xp, xs, ps, sts):
    (xq,), (kv, qn, kn) = xs, ps
    k, v = _split(kv, [D_XA, D_XA])
    m_rows = kv.shape[0]
    q = xq * lax.rsqrt(_head_mean(xq * xq) + EPS) * jnp.concatenate([qn] * XA_HEADS, axis=1)
    k = k * lax.rsqrt(_head_mean(k * k) + EPS) * jnp.concatenate([kn] * XA_HEADS, axis=1)
    kt = k.T
    kbd = jnp.where(_same_block((D_XA, XA_HEADS * m_rows), XA_HD, m_rows), jnp.concatenate([kt] * XA_HEADS, axis=1), 0.0)
    s = mm_nn(q, kbd) * (XA_HD ** -0.5)
    ps_ = []
    for sh in _split(s, [m_rows] * XA_HEADS):
        mx = lax.stop_gradient(jnp.max(sh, axis=-1, keepdims=True))
        p = jnp.exp(sh - mx)
        ps_.append(p / jnp.sum(p, axis=-1, keepdims=True))
    vbd = jnp.where(_same_block((XA_HEADS * m_rows, D_XA), m_rows, XA_HD), jnp.concatenate([v] * XA_HEADS, axis=0), 0.0)
    return (mm_nn(jnp.concatenate(ps_, axis=1), vbd),), ()


def _conv(xp, x, w, b, first, taps):
    xp = jnp.where(first, 0.0, xp)
    y = b + w[taps - 1:taps] * x
    for d in range(1, taps):
        y = y + w[taps - 1 - d:taps - d] * _shift(xp, x, d)
    return y


def _unstack2(x):
    @jax.custom_vjp
    def us(x):
        return x[0], x[1]

    us.defvjp(lambda x: (us(x), None), lambda r, g: (jnp.stack(g),))
    return us(x)


def f_ffn_act(first, xp, xs, ps, sts):
    (up,), (u,), (wg, wv, bg, bv) = xp, xs, ps
    (ugp, uvp), (ug, uv) = _unstack2(up), _unstack2(u)
    gate = _conv(ugp, ug, wg, bg, first, FFN_CONV)
    val = _conv(uvp, uv, wv, bv, first, FFN_CONV)
    return (jax.nn.silu(gate) * val,), ()


def _gla_chunk(q, k, v, la, sts, dk, dv):
    c, nh = q.shape[0], len(sts)
    b = cumsum_rows(la)
    b_last = _row(b, c - 1)
    b_ref = _row(b, c // 2 - 1)
    qe, ke = _split(q * jnp.exp(b - b_ref), [dk] * nh), _split(k * jnp.exp(b_ref - b), [dk] * nh)
    qi, kl = _split(q * jnp.exp(b), [dk] * nh), _split(k * jnp.exp(b_last - b), [dk] * nh)
    dec, vs = _split(jnp.exp(b_last), [dk] * nh), _split(v, [dv] * nh)
    tril = _tril(c)
    outs, new = [], []
    for h in range(nh):
        att = jnp.where(tril, mm_nt(qe[h], ke[h]), 0.0)
        outs.append(mm_nn(att, vs[h]) + mm_nt(qi[h], sts[h]))
        new.append(sts[h] * dec[h] + mm_tn(vs[h], kl[h]))
    return outs, tuple(new)


def _split_rows(x, n):
    c = x.shape[0] // n

    @jax.custom_vjp
    def sp(x):
        return tuple(x[i * c:(i + 1) * c] for i in range(n))

    sp.defvjp(lambda x: (sp(x), None), lambda r, g: (jnp.concatenate(list(g), axis=0),))
    return sp(x)


def _gla_scan(q, k, v, la, sts, dk, dv):
    n = q.shape[0] // CHUNK
    per_chunk = []
    for qc, kc, vc, lc in zip(*(_split_rows(t, n) for t in (q, k, v, la))):
        o, sts = _gla_chunk(qc, kc, vc, lc, sts, dk, dv)
        per_chunk.append(o)
    return [jnp.concatenate([o[h] for o in per_chunk], axis=0) for h in range(len(sts))], sts


def _a_cols(ntot):
    used = D_XA + 2 * GLA_HEADS * GLA_DK + D_MIX + GLA_RANK + D_MIX
    return [D_XA, GLA_HEADS * GLA_DK, GLA_HEADS * GLA_DK, D_MIX, GLA_RANK, D_MIX] + ([ntot - used] if ntot > used else [])


def f_gla(first, xp, xs, ps, sts):
    (p,), (wg2, bg, on) = xs, ps
    parts = _split(p, _a_cols(p.shape[1]))
    q, k, v, glr, og = parts[1:6]
    la = jax.nn.log_sigmoid(mm_nn(glr, wg2) + bg) / GLA_GATE_NORM
    outs, new = _gla_scan(q * (GLA_DK ** -0.5), k, v, la, tuple(sts), GLA_DK, GLA_DV)
    return (jnp.concatenate([_rms(o, on) for o in outs], axis=1) * jax.nn.silu(og),), new


def f_hgrn(first, xp, xs, ps, sts):
    (p,), (lbp, on) = xs, ps
    _, q, fgate, iv, og = _split(p, [D_XA, D_MIX, D_MIX, D_MIX, D_MIX])
    e = jnp.exp(lbp - jnp.max(lbp, axis=0, keepdims=True))
    row = lax.broadcasted_iota(jnp.int32, e.shape, 0)
    lb = jnp.sum(jnp.where(row >= 1, e, 0.0), axis=0, keepdims=True) / jnp.sum(e, axis=0, keepdims=True)
    fg = lb + (1.0 - lb) * jax.nn.sigmoid(fgate)
    outs, new = _gla_scan(jax.nn.silu(q), 1.0 - fg, iv, jnp.log(fg), tuple(sts), HGRN_DK, HGRN_DV)
    return (jnp.concatenate([_rms(o, on) for o in outs], axis=1) * jax.nn.sigmoid(og),), new


def _c_cols(ntot):
    gn = SSM_GROUPS * SSM_STATE
    used = D_XA + D_MIX + D_MIX + 2 * gn + SSM_HEADS
    return [D_XA, D_MIX, D_MIX + 2 * gn, SSM_HEADS] + ([ntot - used] if ntot > used else [])


def f_ssd(first, xp, xs, ps, sts):
    (pp,), (p,), (cw, cb, dtb, alog, dsk, ng) = xp, xs, ps
    gn = SSM_GROUPS * SSM_STATE
    _, z, xbc, dtr = _split(p, _c_cols(p.shape[1]))[:4]
    xbc_p = _split(pp, _c_cols(p.shape[1]))[2]
    xbc = jax.nn.silu(_conv(xbc_p, xbc, cw, cb, first, SSM_CONV))
    xs_, bm, cm = _split(xbc, [D_MIX, gn, gn])
    dt = jax.nn.softplus(dtr + dtb)
    n = p.shape[0] // CHUNK
    ys, sts = [], tuple(sts)
    for xc, bc, cc, dc in zip(*(_split_rows(t, n) for t in (xs_, bm, cm, dt))):
        y, sts = _ssd_chunk(xc, bc, cc, dc, alog, dsk, sts)
        ys.append(y)
    y = jnp.concatenate(ys, axis=0) * jax.nn.silu(z)
    gw = D_MIX // SSM_GROUPS
    yg = _split(y, [gw] * SSM_GROUPS)
    ngs = _split(ng, [gw] * SSM_GROUPS)
    y = jnp.concatenate([_rms(yg[g], ngs[g]) for g in range(SSM_GROUPS)], axis=1)
    return (y,), sts


def _ssd_chunk(xs_, bm, cm, dt, alog, dsk, sts):
    c = xs_.shape[0]
    hg = SSM_HEADS // SSM_GROUPS
    a = dt * (-jnp.exp(alog))
    acs = cumsum_rows(a)
    acs_t = cumsum_rows_t(a)
    acs_last = _row(acs, c - 1)
    dt_h = _split(dt, [1] * SSM_HEADS)
    acs_h = _split(acs, [1] * SSM_HEADS)
    al_h = _split(acs_last, [1] * SSM_HEADS)
    d_h = _split(dsk, [1] * SSM_HEADS)
    x2s = _split(xs_, [2 * SSM_HD] * (SSM_HEADS // 2))
    bms = _split(bm, [SSM_STATE] * SSM_GROUPS)
    cms = _split(cm, [SSM_STATE] * SSM_GROUPS)
    tril = _tril(c)
    cbs = [mm_nt(cms[g], bms[g]) for g in range(SSM_GROUPS)]
    ys, new = [], []
    for j in range(SSM_HEADS // 2):
        g = (2 * j) // hg
        h0, h1 = 2 * j, 2 * j + 1
        xdt = x2s[j] * _lane_pair(dt_h[h0], dt_h[h1])
        acs2 = _lane_pair(acs_h[h0], acs_h[h1])
        al2 = _lane_pair(al_h[h0], al_h[h1])
        yd = []
        for h in (h0, h1):
            seg = acs_h[h] - _row(acs_t, h)
            lm = jnp.exp(jnp.where(tril, seg, NEG))
            yd.append(mm_nn(cbs[g] * lm, xdt))
        lane = lax.broadcasted_iota(jnp.int32, xdt.shape, 1)
        y_diag = jnp.where(lane < SSM_HD, yd[0], yd[1])
        y_off = mm_nn(cms[g], sts[j]) * jnp.exp(acs2)
        x_end = xdt * jnp.exp(al2 - acs2)
        new.append(sts[j] * jnp.exp(al2) + mm_tn(bms[g], x_end))
        ys.append(y_diag + y_off + _lane_pair(d_h[h0], d_h[h1]) * x2s[j])
    return jnp.concatenate(ys, axis=1), tuple(new)


def f_dil_prep(first, xp, xs, ps, sts):
    (p, pos), (qn, kn, invf, sign) = xs, ps
    nh = len(DIL_GROUPS) * DIL_HEADS
    _, q, k, v = _split(p, [D_XA] + [nh * DIL_HD] * 3)
    ang = pos * invf
    cos, sin = jnp.cos(ang), jnp.sin(ang) * sign

    def rope(t, g):
        hs = _split(t, [DIL_HD] * nh)
        out = []
        for h in hs:
            n = _rms(h, g)
            out.append(n * cos + _roll_half(n) * sin)
        return [jnp.concatenate(out[i:i + DIL_HEADS], axis=1) for i in range(0, nh, DIL_HEADS)]

    return tuple(rope(q, qn) + rope(k, kn) + list(_split(v, [D_DIL] * len(DIL_GROUPS)))), ()


def f_dil_attn(first, xp, xs, ps, sts):
    (kp, vp), (q, k, v) = xp, xs
    Q = DIL_BLOCK
    n = q.shape[0] // Q
    qb, kb, vb = (_split_rows(t, n) for t in (q, k, v))
    kprev, vprev = _split_rows(kp, n)[-1], _split_rows(vp, n)[-1]
    i = lax.broadcasted_iota(jnp.int32, (Q, 2 * Q), 0)
    j = lax.broadcasted_iota(jnp.int32, (Q, 2 * Q), 1)
    dist = Q + i - j
    band = (dist >= 0) & (dist <= Q)
    o_rows, lse_rows = [], []
    for b in range(n):
        mask = band & (jnp.logical_not(first) | (j >= Q)) if b == 0 else band
        qs, ks, vs = (_split(t, [DIL_HD] * DIL_HEADS) for t in (qb[b], kb[b], vb[b]))
        kps, vps = (_split(t, [DIL_HD] * DIL_HEADS) for t in (kprev, vprev))
        outs, lses = [], []
        for h in range(DIL_HEADS):
            k2 = jnp.concatenate([kps[h], ks[h]], axis=0)
            v2 = jnp.concatenate([vps[h], vs[h]], axis=0)
            s = jnp.where(mask, mm_nt(qs[h], k2) * (DIL_HD ** -0.5), NEG)
            m = lax.stop_gradient(jnp.max(s, axis=-1, keepdims=True))
            p = jnp.exp(s - m)
            l = jnp.sum(p, axis=-1, keepdims=True)
            outs.append(mm_nn(p / l, v2))
            lses.append(jnp.broadcast_to(m + jnp.log(l), (Q, DIL_HD)))
        o_rows.append(jnp.concatenate(outs, axis=1))
        lse_rows.append(jnp.concatenate(lses, axis=1))
        kprev, vprev = kb[b], vb[b]
    return (jnp.concatenate(o_rows, axis=0), jnp.concatenate(lse_rows, axis=0)), ()


def f_dil_merge(first, xp, xs, ps, sts):
    o0, o1, o2, l0, l1, l2 = xs
    m = jnp.maximum(jnp.maximum(l0, l1), l2)
    e0, e1, e2 = jnp.exp(l0 - m), jnp.exp(l1 - m), jnp.exp(l2 - m)
    den = e0 + e1 + e2
    return ((e0 * o0 + e1 * o1 + e2 * o2) / den,), ()


def _loss_head(y, target):
    S, D = y.shape
    R = _tile(S, (512, 256, 128, 64, 32, 16, 8))

    def body(y_ref, t_ref, dy_ref, dyb_ref, l_ref):
        e = y_ref[...] - t_ref[...]
        dy_ref[...] = e * (1.0 / D)
        dyb_ref[...] = (e * (1.0 / D)).astype(BF)

        @pl.when(pl.program_id(0) == 0)
        def _():
            l_ref[...] = jnp.zeros_like(l_ref)

        l_ref[...] += jnp.broadcast_to(0.5 * jnp.sum(jnp.mean(e * e, axis=-1, keepdims=True), axis=0, keepdims=True),
                                       l_ref.shape)

    blk = pl.BlockSpec((R, D), lambda i: (i, 0))
    dy, dyb, l = pl.pallas_call(
        body, name="loss_head", grid=(S // R,), in_specs=[blk, blk],
        out_specs=[blk, blk, pl.BlockSpec((8, 128), lambda i: (0, 0))],
        out_shape=[jax.ShapeDtypeStruct((S, D), F32), jax.ShapeDtypeStruct((S, D), BF),
                   jax.ShapeDtypeStruct((8, 128), F32)],
        compiler_params=pltpu.CompilerParams(dimension_semantics=("arbitrary",)))(y, target)
    return dy, dyb, l[0, 0]


def _adamw(parts, w, m, v, name):
    _, n, width = parts.shape
    tr = _tile(n, [t for t in (512, 256, 128, 64, 32, 16, 8) if t * width <= ADAM_BLOCK])

    def body(p_ref, w_ref, m_ref, v_ref, g_ref, d_ref, nm_ref, nv_ref):
        g = p_ref[0].astype(F32)
        for s in range(1, N_DEV):
            g = g + p_ref[s].astype(F32)
        nm = ADAM_B1 * m_ref[...] + (1.0 - ADAM_B1) * g
        nv = ADAM_B2 * v_ref[...] + (1.0 - ADAM_B2) * (g * g)
        m_hat = nm / (1.0 - ADAM_B1 ** ADAM_STEP)
        v_hat = nv / (1.0 - ADAM_B2 ** ADAM_STEP)
        g_ref[...] = g
        d_ref[...] = -ADAM_LR * (m_hat / (jnp.sqrt(v_hat) + ADAM_EPS) + ADAM_WD * w_ref[...])
        nm_ref[...] = nm
        nv_ref[...] = nv

    blk = pl.BlockSpec((tr, width), lambda i: (i, 0))
    return pl.pallas_call(
        body, name=name, grid=(n // tr,),
        in_specs=[pl.BlockSpec((N_DEV, tr, width), lambda i: (0, i, 0)), blk, blk, blk],
        out_specs=[blk] * 4, out_shape=[jax.ShapeDtypeStruct((n, width), F32)] * 4,
        compiler_params=pltpu.CompilerParams(dimension_semantics=("arbitrary",), vmem_limit_bytes=VMEM_LIMIT))(
            parts, w, m, v)


def _peer(k):
    x, y, c = lax.axis_index("x"), lax.axis_index("y"), lax.axis_index("c")
    px = 1 - x if k & 4 else x
    py = 1 - y if k & 2 else y
    pc = 1 - c if k & 1 else c
    return (px, py, pc), 4 * px + 2 * py + pc


def _my_id():
    return 4 * lax.axis_index("x") + 2 * lax.axis_index("y") + lax.axis_index("c")


def _all_gather(x, name):
    def body(x_ref, out_ref, send, recv, loc):
        me = _my_id()
        mine = pltpu.make_async_copy(x_ref, out_ref.at[me], loc)
        mine.start()
        cps = []
        for k in range(1, N_DEV):
            peer, _ = _peer(k)
            cp = pltpu.make_async_remote_copy(src_ref=x_ref, dst_ref=out_ref.at[me], send_sem=send.at[k - 1],
                                              recv_sem=recv.at[k - 1], device_id=peer,
                                              device_id_type=pl.DeviceIdType.MESH)
            cp.start()
            cps.append(cp)
        for k in range(1, N_DEV):
            peer, pid = _peer(k)
            pltpu.make_async_remote_copy(src_ref=x_ref, dst_ref=out_ref.at[pid], send_sem=send.at[k - 1],
                                         recv_sem=recv.at[k - 1], device_id=peer,
                                         device_id_type=pl.DeviceIdType.MESH).wait_recv()
        for cp in cps:
            cp.wait_send()
        mine.wait()

    return pl.pallas_call(
        body, name=name, out_shape=jax.ShapeDtypeStruct((N_DEV,) + x.shape, x.dtype),
        in_specs=[pl.BlockSpec(memory_space=pl.ANY)], out_specs=pl.BlockSpec(memory_space=pl.ANY),
        scratch_shapes=[pltpu.SemaphoreType.DMA((N_DEV - 1,)), pltpu.SemaphoreType.DMA((N_DEV - 1,)),
                        pltpu.SemaphoreType.DMA],
        compiler_params=pltpu.CompilerParams(has_side_effects=True))(x)


def _exchange_sems(n):
    return [pltpu.SemaphoreType.DMA((n * (N_DEV - 1),)), pltpu.SemaphoreType.DMA((n * (N_DEV - 1),)),
            pltpu.SemaphoreType.DMA((n,))]


def _exchange_copies(g_refs, out_refs, send, recv, loc, with_arrivals):
    me = _my_id()

    def mine(g, o, d):
        return g.at[d] if len(g.shape) == len(o.shape) else g

    local = [pltpu.make_async_copy(mine(g, o, me), o.at[me], loc.at[w]) for w, (g, o) in enumerate(zip(g_refs, out_refs))]
    pushes, arrivals = [], []
    for k in range(1, N_DEV):
        peer, pid = _peer(k)
        for w, (g, o) in enumerate(zip(g_refs, out_refs)):
            s = w * (N_DEV - 1) + k - 1
            ends = [(mine(g, o, pid), o.at[me], pushes)] + ([(mine(g, o, me), o.at[pid], arrivals)] if with_arrivals else [])
            for src, dst, into in ends:
                into.append(pltpu.make_async_remote_copy(src_ref=src, dst_ref=dst, send_sem=send.at[s],
                                                         recv_sem=recv.at[s], device_id=peer,
                                                         device_id_type=pl.DeviceIdType.MESH))
    return local, pushes, arrivals


def _exchange_start(g_refs, out_refs, send, recv, loc):
    local, pushes, _ = _exchange_copies(g_refs, out_refs, send, recv, loc, False)
    for cp in local + pushes:
        cp.start()


def _exchange_wait(g_refs, out_refs, send, recv, loc):
    local, pushes, arrivals = _exchange_copies(g_refs, out_refs, send, recv, loc, True)
    for cp in arrivals:
        cp.wait_recv()
    for cp in pushes:
        cp.wait_send()
    for cp in local:
        cp.wait()


def _exchange_many(gs, name):
    n = len(gs)

    def body(*refs):
        g_refs, out_refs, sems = refs[:n], refs[n:2 * n], refs[2 * n:]
        _exchange_start(g_refs, out_refs, *sems)
        _exchange_wait(g_refs, out_refs, *sems)

    return pl.pallas_call(
        body, name=name, out_shape=[jax.ShapeDtypeStruct(g.shape, g.dtype) for g in gs],
        in_specs=[pl.BlockSpec(memory_space=pl.ANY)] * n, out_specs=[pl.BlockSpec(memory_space=pl.ANY)] * n,
        scratch_shapes=_exchange_sems(n), compiler_params=pltpu.CompilerParams(has_side_effects=True))(*gs)


def _gather_many(xs, name):
    n = len(xs)

    def body(*refs):
        x_refs, out_refs, (send, recv, loc) = refs[:n], refs[n:2 * n], refs[2 * n:]
        x, y, c = lax.axis_index("x"), lax.axis_index("y"), lax.axis_index("c")
        me, sibling = (x, y, c), (x, y, 1 - c)
        chips = [(1 - x, y), (x, 1 - y), (1 - x, 1 - y)]

        def slot(p):
            return 4 * p[0] + 2 * p[1] + p[2]

        def copy(w, k, block, to, src=None):
            dst = out_refs[w].at[slot(block)]
            return pltpu.make_async_remote_copy(src_ref=dst if src is None else src, dst_ref=dst,
                                                send_sem=send.at[w * (N_DEV - 1) + k], recv_sem=recv.at[w * (N_DEV - 1) + k],
                                                device_id=to, device_id_type=pl.DeviceIdType.MESH)

        mine = [pltpu.make_async_copy(x_refs[w], out_refs[w].at[slot(me)], loc.at[w]) for w in range(n)]
        for cp in mine:
            cp.start()
        first = []
        for j, chip in enumerate(chips):
            first += [copy(w, 1 + j, me, (*chip, c), src=x_refs[w]) for w in range(n)]
        first += [copy(w, 0, me, sibling, src=x_refs[w]) for w in range(n)]
        for cp in first:
            cp.start()
        passed = []
        for j, chip in enumerate(chips):
            for w in range(n):
                copy(w, 1 + j, (*chip, c), me).wait_recv()
                cp = copy(w, 4 + j, (*chip, c), sibling)
                cp.start()
                passed.append(cp)
        for w in range(n):
            copy(w, 0, sibling, me).wait_recv()
            for j, chip in enumerate(chips):
                copy(w, 4 + j, (*chip, 1 - c), me).wait_recv()
        for cp in first + passed:
            cp.wait_send()
        for cp in mine:
            cp.wait()

    return pl.pallas_call(
        body, name=name, out_shape=[jax.ShapeDtypeStruct((N_DEV,) + x.shape, x.dtype) for x in xs],
        in_specs=[pl.BlockSpec(memory_space=pl.ANY)] * n, out_specs=[pl.BlockSpec(memory_space=pl.ANY)] * n,
        scratch_shapes=[pltpu.SemaphoreType.DMA((n * (N_DEV - 1),)), pltpu.SemaphoreType.DMA((n * (N_DEV - 1),)),
                        pltpu.SemaphoreType.DMA((n,))],
        compiler_params=pltpu.CompilerParams(has_side_effects=True))(*xs)


def _cat_segs(G, ws, n_mix):
    segs = []
    for g in range(G):
        lo, hi = g * ws, (g + 1) * ws
        if lo < n_mix:
            segs.append((g, 0, min(hi, n_mix) - lo, D_XA + lo))
        if hi > n_mix:
            s = max(lo, n_mix)
            segs.append((g, s - lo, hi - s, s - n_mix))
    return segs


def _cat_cols(src, n_mix, ntot):
    G, R, ws = src.shape
    segs = _cat_segs(G, ws, n_mix)
    tr = _tile(R, (256, 128, 64, 32, 16, 8))

    def body(i_ref, o_ref):
        if ntot > G * ws:
            o_ref[...] = jnp.zeros_like(o_ref)
        for g, s, n, d in segs:
            o_ref[:, d:d + n] = i_ref[g][:, s:s + n]

    return pl.pallas_call(
        body, name="cat_cols", grid=(R // tr,), in_specs=[pl.BlockSpec((G, tr, ws), lambda i: (0, i, 0))],
        out_specs=pl.BlockSpec((tr, ntot), lambda i: (i, 0)), out_shape=jax.ShapeDtypeStruct((R, ntot), src.dtype),
        compiler_params=pltpu.CompilerParams(dimension_semantics=("arbitrary",)))(src)


def _uncat_cols(dw, G, ws, n_mix):
    R, ntot = dw.shape
    segs = _cat_segs(G, ws, n_mix)
    tr = _tile(R, (256, 128, 64, 32, 16, 8))

    def body(i_ref, o_ref):
        v = i_ref[...]
        for g, s, n, d in segs:
            o_ref[g, :, s:s + n] = v[:, d:d + n]

    return pl.pallas_call(
        body, name="uncat_cols", grid=(R // tr,), in_specs=[pl.BlockSpec((tr, ntot), lambda i: (i, 0))],
        out_specs=pl.BlockSpec((G, tr, ws), lambda i: (0, i, 0)), out_shape=jax.ShapeDtypeStruct((G, R, ws), dw.dtype),
        compiler_params=pltpu.CompilerParams(dimension_semantics=("arbitrary",)))(dw)


PACK_W = 1024


def _granule(n):
    return (256 if n >= 256 * PACK_W else 8) * PACK_W


def _pack(arrs, dtype):
    flat = jnp.concatenate([a.reshape(-1).astype(dtype) for a in arrs])
    n = flat.shape[0]
    pad = (-n) % _granule(n)
    if pad:
        flat = jnp.concatenate([flat, jnp.zeros((pad,), dtype)])
    return flat.reshape(-1, PACK_W)


def _unpack(packed, shapes):
    flat = packed.reshape(-1)
    out, o = [], 0
    for s in shapes:
        n = math.prod(s)
        out.append(flat[o:o + n].reshape(s))
        o += n
    return out


def _pack_lead(arrs, dtype):
    flat = jnp.concatenate([a.reshape(N_DEV, -1).astype(dtype) for a in arrs], axis=1)
    n = flat.shape[1]
    pad = (-n) % _granule(n)
    if pad:
        flat = jnp.concatenate([flat, jnp.zeros((N_DEV, pad), dtype)], axis=1)
    return flat.reshape(N_DEV, -1, PACK_W)


def _to_full(stacked, axis):
    t = jnp.moveaxis(stacked, 0, axis)
    s = list(t.shape)
    return t.reshape(s[:axis] + [s[axis] * s[axis + 1]] + s[axis + 2:])


def _to_chunks(full, axis):
    s = list(full.shape)
    t = full.reshape(s[:axis] + [N_DEV, s[axis] // N_DEV] + s[axis + 1:])
    return jnp.moveaxis(t, axis, 0)


def _rows_of(S):
    return _tile(S, (512, 256, 128, 64))


def _norm_fwd(x, g, dt=BF):
    (h,), _ = _seq_fwd("rmsnorm_fwd", f_rmsnorm, _rows_of(x.shape[0]), [_row_spec(x)], [_par_spec(g)], [],
                       [_out_spec(x.shape[1], dt=dt)])
    return h


def _norm_bwd(x, g, dh, res=None):
    if res is None:
        (dx,), (dg,) = _seq_bwd("rmsnorm_bwd", f_rmsnorm, _rows_of(x.shape[0]), [_row_spec(x)], [_par_spec(g)], [],
                                [_row_spec(dh)], [])
        return dx, None, dg
    (dx,), (dg,), (dxb,) = _seq_bwd("rmsnorm_res_bwd", f_rmsnorm, _rows_of(x.shape[0]), [_row_spec(x)], [_par_spec(g)], [],
                                    [_row_spec(dh)], [], dx_add={0: res}, dx_bf=True)
    return dx, dxb, dg


def _mixer_specs(kind, S, p, w):
    if kind == 0:
        return (f_gla, min(S, MIX_ROWS), [_row_spec(p)],
                [_par_spec(w['a_w_gate2']), _par_spec(w['a_b_gate'].reshape(1, -1)), _par_spec(w['a_o_norm'].reshape(1, -1))],
                [(GLA_DV, GLA_DK)] * GLA_HEADS, D_MIX)
    if kind == 2:
        return (f_ssd, min(S, MIX_ROWS), [_row_spec(p, prev='halo')],
                [_par_spec(w['c_conv_w']), _par_spec(w['c_conv_b'].reshape(1, -1)), _par_spec(w['c_dt_bias'].reshape(1, -1)),
                 _par_spec(w['c_a_log'].reshape(1, -1)), _par_spec(w['c_d'].reshape(1, -1)),
                 _par_spec(w['c_norm'].reshape(1, -1))],
                [(SSM_STATE, 2 * SSM_HD)] * (SSM_HEADS // 2), D_MIX)
    return (f_hgrn, min(S, MIX_ROWS), [_row_spec(p)],
            [_par_spec(w['d_lower_bounds']), _par_spec(w['d_o_norm'].reshape(1, -1))],
            [(HGRN_DV, HGRN_DK)] * HGRN_HEADS, D_MIX)


def _perm(t, r):
    if r == 1:
        return t
    S, n = t.shape
    return t.reshape(S // r, r, n).transpose(1, 0, 2).reshape(S, n)


def _unperm(t, r):
    if r == 1:
        return t
    S, n = t.shape
    return t.reshape(r, S // r, n).transpose(1, 0, 2).reshape(S, n)


def _rope_consts():
    half = DIL_HD // 2
    inv = ROPE_THETA ** (-jnp.arange(half, dtype=F32) / half)
    invf = jnp.concatenate([inv, inv]).reshape(1, DIL_HD)
    sign = jnp.concatenate([-jnp.ones((half,), F32), jnp.ones((half,), F32)]).reshape(1, DIL_HD)
    return invf, sign


def _dil_fwd(p, pos, w, ncat):
    S = p.shape[0]
    invf, sign = _rope_consts()
    prep_rows = [_row_spec(p), _row_spec(pos, diff=False)]
    prep_pars = [_par_spec(w['b_q_norm'].reshape(1, -1)), _par_spec(w['b_k_norm'].reshape(1, -1)),
                 _par_spec(invf, diff=False), _par_spec(sign, diff=False)]
    ng = len(DIL_GROUPS)
    qkv, _ = _seq_fwd("dil_prep_fwd", f_dil_prep, _tile(S, (256, 128)), prep_rows, prep_pars, [],
                      [_out_spec(D_DIL) for _ in range(3 * ng)])
    res = dict(perm=[], o=[], lse=[])
    for g, (window, r) in enumerate(DIL_GROUPS):
        qp, kp, vp = _perm(qkv[g], r), _perm(qkv[ng + g], r), _perm(qkv[2 * ng + g], r)
        rows = [_row_spec(qp), _row_spec(kp, prev='block'), _row_spec(vp, prev='block')]
        Rg = min(DIL_ROWS, S // r)
        (o, lse), _ = _seq_fwd("dil_attn_fwd", f_dil_attn, Rg, rows, [], [], [_out_spec(D_DIL), _out_spec(D_DIL)],
                               period=S // r // Rg)
        res['perm'].append((qp, kp, vp))
        res['o'].append(_unperm(o, r))
        res['lse'].append(_unperm(lse, r))
    mrows = [_row_spec(t) for t in res['o'] + res['lse']]
    (cat,), _ = _seq_fwd("dil_merge_fwd", f_dil_merge, _rows_of(S), mrows, [], [], [_out_spec(ncat, w=D_DIL, dt=BF)])
    res['prep'] = (prep_rows, prep_pars)
    return cat, res


def _dil_bwd(dtok, res, p, exchange=None, exchange_local=None):
    S = p.shape[0]
    mrows = [_row_spec(t) for t in res['o'] + res['lse']]
    dm, _ = _seq_bwd("dil_merge_bwd", f_dil_merge, _rows_of(S), mrows, [], [], [dtok], [])
    dq, dk, dv = [], [], []
    for g, (window, r) in enumerate(DIL_GROUPS):
        qp, kp, vp = res['perm'][g]
        rows = [_row_spec(qp), _row_spec(kp, prev='block'), _row_spec(vp, prev='block')]
        douts = [_row_spec(_perm(dm[g], r)), _row_spec(_perm(dm[3 + g], r))]
        Rg = min(DIL_ROWS, S // r)
        (a, b, c), _ = _seq_bwd("dil_attn_bwd", f_dil_attn, Rg, rows, [], [], douts, [], period=S // r // Rg)
        dq.append(_unperm(a, r)); dk.append(_unperm(b, r)); dv.append(_unperm(c, r))
    prep_rows, prep_pars = res['prep']
    res = _seq_bwd("dil_prep_bwd", f_dil_prep, _tile(S, (256, 128)), prep_rows, prep_pars, [],
                   [_row_spec(t) for t in dq + dk + dv], [], dx_dt=BF, exchange=exchange)
    (dp,), (dqn, dkn) = res[0], res[1]
    return dp, dict(b_q_norm=dqn.reshape(-1), b_k_norm=dkn.reshape(-1)), (res[2] if exchange else exchange_local)


def _ffn_specs(u, cw, cb):
    half = N_DEV // 2
    rows = [_row_spec(u, prev='halo', lb=(2, None), li=lambda jc: (0, jc))]
    pars = [_par_spec(cw, bs=(None, FFN_CONV, FF_SH), idx=lambda jc: (jc, 0, 0)),
            _par_spec(cw, bs=(None, FFN_CONV, FF_SH), idx=lambda jc: (jc + half, 0, 0)),
            _par_spec(cb, bs=(None, 1, FF_SH), idx=lambda jc: (jc, 0, 0)),
            _par_spec(cb, bs=(None, 1, FF_SH), idx=lambda jc: (jc + half, 0, 0))]
    return half, rows, pars


N_MIX = {0: 2 * GLA_HEADS * GLA_DK + 2 * D_MIX + GLA_RANK, 1: 3 * len(DIL_GROUPS) * D_DIL,
         2: 2 * D_MIX + 2 * SSM_GROUPS * SSM_STATE + SSM_HEADS, 3: 2 * HGRN_HEADS * HGRN_DK + 2 * D_MIX}
W_IN = {0: 'a_w_in', 1: 'b_w_in', 2: 'c_w_in', 3: 'd_w_in'}
W_OUT = {0: 'a_w_out', 1: 'b_w_out', 2: 'c_w_out', 3: 'd_w_out'}


def _in_blocks(name, t):
    return t if SHARD_AXIS[name] == 1 else t.reshape(1, N_DEV * t.shape[1], t.shape[2])


LAYER_STACKED = ('ffn_w_up', 'ffn_conv_w', 'ffn_w_down', 'xa_w_kv')


SMALL_OF_KIND = {0: ['a_w_gate2'], 2: ['c_conv_w']}


def _layer_names(i):
    return list(LAYER_STACKED) + [W_IN[i % 4], W_OUT[i % 4]] + SMALL_OF_KIND.get(i % 4, [])


def _device_step(x, mem, pos, sh, rep, target, distributed=True):
    S, D = x.shape
    w = dict(rep)
    posf = pos.reshape(S, 1).astype(F32)
    n_mix, w_in_name, w_out_name = N_MIX, W_IN, W_OUT
    ntot = {k: -(-(n_mix[k] + D_XA) // 256) * 256 for k in n_mix}
    mem_g = w['mem_norm'].reshape(1, -1)
    mem_n = _norm_fwd(mem, mem_g)
    R = _rows_of(S)

    def mine(i):
        return {n: (sh[n][i] if n in LAYER_STACKED else sh[n]) for n in _layer_names(i)}

    if distributed:
        gl = dict(zip(_layer_names(0), _gather_many(list(mine(0).values()), "gather_weights")))
    else:
        gl = {n: (sh[n][:, 0] if n in LAYER_STACKED else sh[n]) for n in _layer_names(0)}

    saved = []
    for i in range(DEPTH):
        kind = i % 4
        L = dict(x0=x)
        for n in SMALL_OF_KIND.get(kind, []):
            w[n] = _to_full(gl[n], 1)
        in_blocks = _in_blocks(w_in_name[kind], gl[w_in_name[kind]])
        w_out = (_to_full(gl[w_out_name[kind]], 1) if SHARD_AXIS[w_out_name[kind]] == 1
                 else gl[w_out_name[kind]].reshape(-1, D))
        nxt, push = {}, [[], [], []]
        if i + 1 < DEPTH:
            if distributed:
                nxt = mine(i + 1)
                push = [[n for n in nxt if n not in ('ffn_w_up', w_in_name[(i + 1) % 4], w_out_name[(i + 1) % 4])],
                        ['ffn_w_up'], [w_in_name[(i + 1) % 4], w_out_name[(i + 1) % 4]]]
            else:
                nxt = {n: (sh[n][:, i + 1] if n in LAYER_STACKED else sh[n]) for n in _layer_names(i + 1)}
        got = dict(nxt) if not distributed else {}

        def hosted(call, names):
            if not names:
                return call()
            res, arrived = call(gather=[nxt[n] for n in names])
            got.update(zip(names, arrived))
            return res

        g1 = w['mix_norm'][i].reshape(1, -1)
        h = _norm_fwd(x, g1)
        wcat = _cat_cols(in_blocks, n_mix[kind], ntot[kind])
        p = hosted(functools.partial(_matmul, h, wcat, name="matmul_in"), push[0])
        ntok = D_DIL if kind == 1 else D_MIX
        if kind == 1:
            cat, L['dil'] = _dil_fwd(p, posf, w, ntok + D_XA)
        else:
            f, Rm, rows, pars, sshapes, _ = _mixer_specs(kind, S, p, w)
            (cat,), L['states'] = _seq_fwd("mixer%d_fwd" % kind, f, Rm, rows, pars, sshapes,
                                           [_out_spec(ntok + D_XA, w=ntok, dt=BF)], save_states=True)
        wkv = gl['xa_w_kv'].reshape(D, 2 * D_XA)
        kv = _matmul(mem_n, wkv, name="matmul_kv")
        xa_rows = [_row_spec(p, w=D_XA, dn=D_XA)]
        xa_pars = [_par_spec(kv), _par_spec(w['xa_q_norm'][i].reshape(1, -1)), _par_spec(w['xa_k_norm'][i].reshape(1, -1))]
        (cat,), _ = _seq_fwd("xattn_fwd", f_xattn, R, xa_rows, xa_pars, [],
                             [_out_spec(ntok + D_XA, w=D_XA, c=lambda jc: ntok // D_XA, dt=BF)], out_alias={0: cat})
        x1 = _matmul(cat, w_out, add=x, name="matmul_out")
        g2 = w['ffn_norm'][i].reshape(1, -1)
        h2 = _norm_fwd(x1, g2)
        wup = gl['ffn_w_up']
        u = hosted(functools.partial(_ffn_up, h2, wup), push[1])
        cw, cb = gl['ffn_conv_w'], w['ffn_conv_b'][i].reshape(N_DEV, 1, FF_SH)
        nt, frows, fpars = _ffn_specs(u, cw, cb)
        res = _seq_fwd("ffn_act_fwd", f_ffn_act, R, frows, fpars, [],
                       [_out_spec(FF_SH, dt=BF, ls=(nt,), lb=(None,), li=lambda jc: (jc,))], ncol=nt,
                       gather=[nxt[n] for n in push[2]] if push[2] else None)
        (act,) = res[0]
        if push[2]:
            got.update(zip(push[2], res[2]))
        wd = gl['ffn_w_down'].reshape(D_FF, D)
        x = _ffn_down(act, wd, x1)
        L.update(h=h, p=p, wcat=wcat, kv=kv, wkv=wkv, cat=cat, x1=x1, h2=h2, u=u, act=act, wd=wd, wup=wup, cw=cw, g1=g1,
                 g2=g2, in_blocks=in_blocks, w_out=w_out, shapes={n: t.shape for n, t in gl.items()})
        saved.append(L)
        gl = got

    dx, dxb, loss = _loss_head(x, target)

    G = {}
    d_mem_n = None
    acc = {k: [None] * DEPTH for k in ('mix_norm', 'ffn_norm', 'ffn_conv_b', 'xa_q_norm', 'xa_k_norm')}
    parts = [{} for _ in range(DEPTH)]
    pending = {}
    half = N_DEV // 2

    def sent(call, blocks, layer):
        if not blocks:
            return call()
        if not distributed:
            parts[layer].update(blocks)
            return call()
        res, arrived = call(exchange=list(blocks.values()))
        parts[layer].update(zip(blocks, arrived))
        return res

    for i in reversed(range(DEPTH)):
        kind = i % 4
        L = saved[i]
        Gc = {}
        Gc['ffn_w_down'] = _ffn_dw_down(L['act'], dxb).reshape(N_DEV, D_FF // N_DEV, D)
        dact = _ffn_dact(dxb, L['wd'])
        cw, cb = L['cw'], w['ffn_conv_b'][i].reshape(N_DEV, 1, FF_SH)
        nt, frows, fpars = _ffn_specs(L['u'], cw, cb)
        (du,), (dwg, dwv, dbg, dbv) = _seq_bwd(
            "ffn_act_bwd", f_ffn_act, R, frows, fpars, [], [_row_spec(dact, lb=(None,), li=lambda jc: (jc,))], [],
            ncol=nt, dx_dt=BF)
        Gc['ffn_conv_w'] = jnp.concatenate([dwg[:half], dwv[half:]], axis=0)
        acc['ffn_conv_b'][i] = jnp.concatenate([dbg[:half], dbv[half:]], axis=0).reshape(-1)
        Gc['ffn_w_up'] = _ffn_dw_up(L['h2'], du)
        dh2 = sent(functools.partial(_ffn_dh2, du, L['wup']), pending, i + 1)
        dx1, dx1b, dg2 = _norm_bwd(L['x1'], L['g2'], dh2, res=dx)
        acc['ffn_norm'][i] = dg2.reshape(-1)
        G_out = _matmul(L['cat'], dx1b, mode="tn", out_dtype=BF, name="matmul_dw_out")
        dcat = _matmul(dx1b, L['w_out'], mode="nt", name="matmul_dcat")
        ntok = D_DIL if kind == 1 else D_MIX
        dtok = _row_spec(dcat, w=ntok)
        dxa = _row_spec(dcat, w=D_XA, c=lambda jc: ntok // D_XA)
        p = L['p']
        up = [Gc.pop('ffn_w_up')]
        if kind == 1:
            dp, gm, got_up = _dil_bwd(dtok, L['dil'], p, up if distributed else None, up)
            G.update(gm)
        else:
            f, Rm, rows, pars, sshapes, _ = _mixer_specs(kind, S, p, w)
            res = _seq_bwd("mixer%d_bwd" % kind, f, Rm, rows, pars, sshapes, [dtok], L['states'], dx_dt=BF,
                           exchange=up if distributed else None)
            (dp,), dps, got_up = res[0], res[1], (res[2] if distributed else up)
            if kind == 0:
                Gc['a_w_gate2'], G['a_b_gate'], G['a_o_norm'] = _to_chunks(dps[0], 1), dps[1].reshape(-1), dps[2].reshape(-1)
            elif kind == 2:
                Gc['c_conv_w'] = _to_chunks(dps[0], 1)
                for nme, v in zip(('c_conv_b', 'c_dt_bias', 'c_a_log', 'c_d', 'c_norm'), dps[1:]):
                    G[nme] = v.reshape(-1)
            else:
                G['d_lower_bounds'], G['d_o_norm'] = dps[0], dps[1].reshape(-1)
        xa_rows = [_row_spec(p, w=D_XA)]
        xa_pars = [_par_spec(L['kv']), _par_spec(w['xa_q_norm'][i].reshape(1, -1)), _par_spec(w['xa_k_norm'][i].reshape(1, -1))]
        (dp,), (dkv, dqn, dkn) = _seq_bwd("xattn_bwd", f_xattn, R, xa_rows, xa_pars, [], [dxa], [], dx_dt=BF,
                                          dx_alias={0: dp})
        acc['xa_q_norm'][i], acc['xa_k_norm'][i] = dqn.reshape(-1), dkn.reshape(-1)
        Gc['xa_w_kv'] = _matmul(mem_n, dkv, mode="tn", out_dtype=BF, name="matmul_dw_kv").reshape(
            N_DEV, D // N_DEV, 2 * D_XA)
        d_mem_n = _matmul(dkv, L['wkv'], mode="nt", add=d_mem_n, name="matmul_dmem" + ("" if d_mem_n is None else "_acc"))
        parts[i]['ffn_w_up'] = got_up[0]
        dwcat = _matmul(L['h'], dp, mode="tn", out_dtype=BF, name="matmul_dw_in")
        blocks = L['in_blocks']
        Gc[w_in_name[kind]] = _uncat_cols(dwcat, blocks.shape[0], blocks.shape[2], n_mix[kind]).reshape(
            L['shapes'][w_in_name[kind]])
        Gc[w_out_name[kind]] = (_to_chunks(G_out, 1) if SHARD_AXIS[w_out_name[kind]] == 1
                                else G_out.reshape(L['shapes'][w_out_name[kind]]))
        dh = sent(functools.partial(_matmul, dp, L['wcat'], mode="nt", name="matmul_dh"),
                  {n: Gc.pop(n) for n in ('ffn_w_down', 'ffn_conv_w')}, i)
        dx, dxb, dg1 = _norm_bwd(L['x0'], L['g1'], dh, res=dx1)
        acc['mix_norm'][i] = dg1.reshape(-1)
        pending = Gc

    if distributed:
        names = list(pending)
        parts[0].update(zip(names, _exchange_many([pending[n] for n in names], "exchange_grads")))
    else:
        parts[0].update(pending)
    _, _, dmg = _norm_bwd(mem, mem_g, d_mem_n)
    G['mem_norm'] = dmg.reshape(-1)
    for k, v in acc.items():
        G[k] = jnp.stack(v)
    got = {}
    for i in range(DEPTH):
        for n, t in parts[i].items():
            if n not in LAYER_STACKED:
                got[n] = t
    for n in LAYER_STACKED:
        got[n] = jnp.stack([parts[i][n] for i in range(DEPTH)], axis=1)
    return loss, dx, got, G


def kernel(x, mem, positions, mem_norm, mix_norm, xa_w_kv, xa_q_norm, xa_k_norm, ffn_norm, ffn_w_up, ffn_conv_w, ffn_conv_b, ffn_w_down, a_w_in, a_w_gate2, a_b_gate, a_o_norm, a_w_out, b_w_in, b_q_norm, b_k_norm, b_w_out, c_w_in, c_conv_w, c_conv_b, c_dt_bias, c_a_log, c_d, c_norm, c_w_out, d_w_in, d_lower_bounds, d_o_norm, d_w_out, loss_target, m_mem_norm, m_mix_norm, m_xa_w_kv, m_xa_q_norm, m_xa_k_norm, m_ffn_norm, m_ffn_w_up, m_ffn_conv_w, m_ffn_conv_b, m_ffn_w_down, m_a_w_in, m_a_w_gate2, m_a_b_gate, m_a_o_norm, m_a_w_out, m_b_w_in, m_b_q_norm, m_b_k_norm, m_b_w_out, m_c_w_in, m_c_conv_w, m_c_conv_b, m_c_dt_bias, m_c_a_log, m_c_d, m_c_norm, m_c_w_out, m_d_w_in, m_d_lower_bounds, m_d_o_norm, m_d_w_out, v_mem_norm, v_mix_norm, v_xa_w_kv, v_xa_q_norm, v_xa_k_norm, v_ffn_norm, v_ffn_w_up, v_ffn_conv_w, v_ffn_conv_b, v_ffn_w_down, v_a_w_in, v_a_w_gate2, v_a_b_gate, v_a_o_norm, v_a_w_out, v_b_w_in, v_b_q_norm, v_b_k_norm, v_b_w_out, v_c_w_in, v_c_conv_w, v_c_conv_b, v_c_dt_bias, v_c_a_log, v_c_d, v_c_norm, v_c_w_out, v_d_w_in, v_d_lower_bounds, v_d_o_norm, v_d_w_out):
    args = locals()
    w = {n: args[n] for n in WEIGHTS}
    m = {n: args['m_' + n] for n in WEIGHTS}
    v = {n: args['v_' + n] for n in WEIGHTS}

    big = [n for n in SHARDED if w[n].size >= 65536]
    small = [n for n in SHARDED if n not in big]
    sh = {n: (w[n].astype(BF) if n in big else w[n]) for n in SHARDED}
    loss, grad_x, parts, G = _device_step(x[0], mem[0], positions[0], sh, {n: w[n] for n in REPLICATED}, loss_target[0])
    loss = lax.psum(loss, ("x", "y", "c"))
    rep_parts = _all_gather(_pack([G[n] for n in REPLICATED], F32), "gather_replicated_grads")

    out = {}

    def put(names, res, shapes):
        for kind, r in zip(("grad", "delta", "new_m", "new_v"), res):
            for n, t in zip(names, _unpack(r, shapes)):
                out[kind + "_" + n] = t

    for n in big:
        shp = tuple(w[n].shape)
        two_d = (math.prod(shp[:-1]), shp[-1])
        res = _adamw(parts[n].reshape((N_DEV,) + two_d), w[n].reshape(two_d), m[n].reshape(two_d), v[n].reshape(two_d),
                     "adamw")
        for kind, r in zip(("grad", "delta", "new_m", "new_v"), res):
            out[kind + "_" + n] = r.reshape(shp)
    for names, prt, tag in ((small, _pack_lead([parts[n] for n in small], F32), "adamw_small"),
                            (REPLICATED, rep_parts, "adamw_replicated")):
        res = _adamw(prt, _pack([w[n] for n in names], F32), _pack([m[n] for n in names], F32),
                     _pack([v[n] for n in names], F32), tag)
        put(names, res, [tuple(w[n].shape) for n in names])
    return (loss, grad_x[None], *[out["grad_" + n] for n in WEIGHTS], *[out["delta_" + n] for n in WEIGHTS],
            *[out["new_m_" + n] for n in WEIGHTS], *[out["new_v_" + n] for n in WEIGHTS])
```

```python
import functools
import math

import jax
import jax.numpy as jnp
from jax import lax
from jax.experimental import pallas as pl
from jax.experimental.pallas import tpu as pltpu

F32 = jnp.float32
BF = jnp.bfloat16
_MM_DTYPE = BF

N_DEV = 8
EPS = 1e-6
ROPE_THETA = 10000.0
CHUNK = 64
MIX_ROWS = 256
DIL_ROWS = 512
D_MIX = 768
XA_HEADS, XA_HD, D_XA = 4, 64, 256
GLA_HEADS, GLA_DK, GLA_DV, GLA_RANK, GLA_GATE_NORM = 4, 96, 192, 16, 16.0
DIL_GROUPS = ((128, 1), (512, 4), (2048, 16))
DIL_HEADS, DIL_HD, DIL_BLOCK, D_DIL = 4, 128, 128, 512
SSM_HD, SSM_HEADS, SSM_GROUPS, SSM_STATE, SSM_CONV = 64, 12, 2, 128, 4
HGRN_HEADS, HGRN_DK, HGRN_DV = 6, 128, 128
D_FF = 2816
FFN_CONV = 3
DEPTH = 4
ADAM_LR, ADAM_B1, ADAM_B2, ADAM_EPS, ADAM_WD, ADAM_STEP = 0.001, 0.9, 0.999, 1e-08, 0.01, 10
NEG = -1e30
HALO = 8
VMEM_LIMIT = 56 << 20
ADAM_BLOCK = 1 << 18

WEIGHTS = ['mem_norm', 'mix_norm', 'xa_w_kv', 'xa_q_norm', 'xa_k_norm', 'ffn_norm', 'ffn_w_up', 'ffn_conv_w',
           'ffn_conv_b', 'ffn_w_down', 'a_w_in', 'a_w_gate2', 'a_b_gate', 'a_o_norm', 'a_w_out', 'b_w_in', 'b_q_norm',
           'b_k_norm', 'b_w_out', 'c_w_in', 'c_conv_w', 'c_conv_b', 'c_dt_bias', 'c_a_log', 'c_d', 'c_norm', 'c_w_out',
           'd_w_in', 'd_lower_bounds', 'd_o_norm', 'd_w_out']
SHARD_AXIS = {'xa_w_kv': 1, 'ffn_w_up': 2, 'ffn_conv_w': 2, 'ffn_w_down': 1, 'a_w_in': 1, 'a_w_gate2': 1, 'a_w_out': 0,
              'b_w_in': 1, 'b_w_out': 1, 'c_w_in': 0, 'c_conv_w': 1, 'c_w_out': 0, 'd_w_in': 1, 'd_w_out': 0}
SHARDED = [n for n in WEIGHTS if n in SHARD_AXIS]
REPLICATED = [n for n in WEIGHTS if n not in SHARD_AXIS]


def _dot(a, b, ca, cb):
    return lax.dot_general(a.astype(_MM_DTYPE), b.astype(_MM_DTYPE), (((ca,), (cb,)), ((), ())),
                           preferred_element_type=F32)


@jax.custom_vjp
def mm_nn(a, b):
    return _dot(a, b, 1, 0)


mm_nn.defvjp(lambda a, b: (_dot(a, b, 1, 0), (a, b)),
             lambda r, g: (_dot(g, r[1], 1, 1), _dot(r[0], g, 0, 0)))


@jax.custom_vjp
def mm_nt(a, b):
    return _dot(a, b, 1, 1)


mm_nt.defvjp(lambda a, b: (_dot(a, b, 1, 1), (a, b)),
             lambda r, g: (_dot(g, r[1], 1, 0), _dot(g, r[0], 0, 0)))


@jax.custom_vjp
def mm_tn(a, b):
    return _dot(a, b, 0, 0)


mm_tn.defvjp(lambda a, b: (_dot(a, b, 0, 0), (a, b)),
             lambda r, g: (_dot(r[1], g, 1, 1), _dot(r[0], g, 1, 0)))


def _dot_hi(a, b, ca, cb):
    return lax.dot_general(a, b, (((ca,), (cb,)), ((), ())), precision=lax.Precision.HIGHEST,
                           preferred_element_type=F32)


def _tril(c):
    return lax.broadcasted_iota(jnp.int32, (c, c), 0) >= lax.broadcasted_iota(jnp.int32, (c, c), 1)


@jax.custom_vjp
def cumsum_rows(x):
    return _dot_hi(_tril(x.shape[0]).astype(F32), x, 1, 0)


cumsum_rows.defvjp(lambda x: (cumsum_rows(x), None),
                   lambda r, g: (_dot_hi(_tril(g.shape[0]).astype(F32), g, 0, 0),))


@jax.custom_vjp
def cumsum_rows_t(x):
    return _dot_hi(x, _tril(x.shape[0]).astype(F32), 0, 1)


cumsum_rows_t.defvjp(lambda x: (cumsum_rows_t(x), None),
                     lambda r, g: (_dot_hi(_tril(g.shape[1]).astype(F32), g, 0, 1),))


def _split(x, sizes):
    sizes = tuple(int(s) for s in sizes)
    assert sum(sizes) == x.shape[-1], (sizes, x.shape)

    @jax.custom_vjp
    def sp(x):
        out, o = [], 0
        for s in sizes:
            out.append(x[:, o:o + s])
            o += s
        return tuple(out)

    sp.defvjp(lambda x: (sp(x), None), lambda r, g: (jnp.concatenate(list(g), axis=1),))
    return sp(x)


def _row(x, r):
    m = lax.broadcasted_iota(jnp.int32, x.shape, 0) == r
    return jnp.sum(jnp.where(m, x, 0.0), axis=0, keepdims=True)


@jax.custom_vjp
def _roll_half(x):
    return pltpu.roll(x, 64, 1)


_roll_half.defvjp(lambda x: (pltpu.roll(x, 64, 1), None), lambda r, g: (pltpu.roll(g, 64, 1),))


def _shift(xp, x, d):
    if d == 0:
        return x
    n, m = x.shape[0], xp.shape[0]
    assert d <= m == HALO

    @jax.custom_vjp
    def sh(xp, x):
        r = pltpu.roll(x, d, 0)
        row = lax.broadcasted_iota(jnp.int32, xp.shape, 0)
        head = jnp.where(row < d, pltpu.roll(xp, d, 0), r[:m])
        return jnp.concatenate([head, r[m:]], axis=0)

    def bwd(_, g):
        row = lax.broadcasted_iota(jnp.int32, g.shape, 0)
        rowp = lax.broadcasted_iota(jnp.int32, (m,) + g.shape[1:], 0)
        dxp = jnp.where(rowp >= m - d, pltpu.roll(g[:m], m - d, 0), 0.0)
        return dxp, jnp.where(row < n - d, pltpu.roll(g, n - d, 0), 0.0)

    sh.defvjp(lambda xp, x: (sh(xp, x), None), bwd)
    return sh(xp, x)


def _rms(x, g):
    return x * lax.rsqrt(jnp.mean(x * x, axis=-1, keepdims=True) + EPS) * g


def _lane_pair(a, b, width=128):
    shape = a.shape[:-1] + (width,)
    lane = lax.broadcasted_iota(jnp.int32, shape, len(shape) - 1)
    return jnp.where(lane < width // 2, a, b)


def _row_spec(a, w=None, c=None, prev=False, diff=True, dn=None, lb=(), li=None):
    return dict(a=a, w=a.shape[-1] if w is None else w, c=(lambda jc: 0) if c is None else c, prev=prev, diff=diff,
                dn=a.shape[-1] if dn is None else dn, lb=tuple(lb), li=(lambda jc: ()) if li is None else li)


def _par_spec(a, bs=None, idx=None, diff=True):
    nd = a.ndim
    return dict(a=a, bs=tuple(a.shape) if bs is None else tuple(bs),
                idx=(lambda jc: (0,) * nd) if idx is None else idx, diff=diff)


def _out_spec(n, w=None, c=None, dt=F32, ls=(), lb=(), li=None):
    return dict(n=n, w=n if w is None else w, c=(lambda jc: 0) if c is None else c, dt=dt, ls=tuple(ls), lb=tuple(lb),
                li=(lambda jc: ()) if li is None else li)


def _cparams():
    return pltpu.CompilerParams(dimension_semantics=("arbitrary", "arbitrary"), vmem_limit_bytes=VMEM_LIMIT)


def _bspec(s, R, rowfn):
    return pl.BlockSpec(s['lb'] + (R, s['w']),
                        functools.partial(lambda jc, i, s: tuple(s['li'](jc)) + (rowfn(i), s['c'](jc)), s=s))


def _prev_rows(s, R):
    return R if s['prev'] == 'block' else HALO


def _pspec(s, R, blockfn):
    pr = _prev_rows(s, R)
    return pl.BlockSpec(s['lb'] + (pr, s['w']), functools.partial(
        lambda jc, i, s: tuple(s['li'](jc)) + (jnp.maximum(blockfn(i) * (R // pr) - 1, 0), s['c'](jc)), s=s))


def _seq_fwd(name, f, R, rows, params, state_shapes, outs, *, ncol=1, period=None, save_states=False, out_alias=None,
             gather=None):
    nrows = rows[0]['a'].shape[-2]
    nb = nrows // R
    assert nb * R == nrows
    period = nb if period is None else period
    prev_ids = [k for k, r in enumerate(rows) if r['prev']]
    n_rows, n_prev, n_par, n_out, n_st = len(rows), len(prev_ids), len(params), len(outs), len(state_shapes)
    ex = list(gather or [])
    n_ex = len(ex)

    def body(*refs):
        o = 0
        cur = refs[o:o + n_rows]; o += n_rows
        prv = refs[o:o + n_prev]; o += n_prev
        par = refs[o:o + n_par]; o += n_par + len(out_alias or {})
        exg = refs[o:o + n_ex]; o += n_ex
        out = refs[o:o + n_out]; o += n_out
        sav = refs[o:o + (n_st if save_states else 0)]; o += len(sav)
        exr = refs[o:o + n_ex]; o += n_ex
        st = refs[o:o + n_st]; o += n_st
        sems = refs[o:]
        i = pl.program_id(1)
        first = (i % period) == 0
        if n_ex:
            @pl.when((pl.program_id(0) == 0) & (i == 0))
            def _():
                _exchange_start(exg, exr, *sems)

        @pl.when(i == 0)
        def _():
            for s in st:
                s[...] = jnp.zeros_like(s)

        xs = [r[...].astype(F32) for r in cur]
        xp = [r[...].astype(F32) for r in prv]
        ps = [r[...] for r in par]
        sts = [s[...] for s in st]
        for sv, s in zip(sav, sts):
            sv[0] = s
        ov, ns = f(first, xp, xs, ps, sts)
        for r, v in zip(out, ov):
            r[...] = v.astype(r.dtype)
        for s, v in zip(st, ns):
            s[...] = v
        if n_ex:
            @pl.when((pl.program_id(0) == ncol - 1) & (i == nb - 1))
            def _():
                _exchange_wait(exg, exr, *sems)

    in_specs = [_bspec(r, R, lambda i: i) for r in rows]
    in_specs += [_pspec(rows[k], R, lambda i: i) for k in prev_ids]
    in_specs += [pl.BlockSpec(p['bs'], functools.partial(lambda jc, i, idx: idx(jc), idx=p['idx'])) for p in params]
    out_specs = [_bspec(o_, R, lambda i: i) for o_ in outs]
    out_shape = [jax.ShapeDtypeStruct(o_['ls'] + (nrows, o_['n']), o_['dt']) for o_ in outs]
    if save_states:
        for s in state_shapes:
            out_specs.append(pl.BlockSpec((1,) + tuple(s), lambda jc, i, nd=len(s): (i,) + (0,) * nd))
            out_shape.append(jax.ShapeDtypeStruct((nb,) + tuple(s), F32))
    args = [r['a'] for r in rows] + [rows[k]['a'] for k in prev_ids] + [p['a'] for p in params]
    aliases = {}
    for n_, arr in sorted((out_alias or {}).items()):
        assert arr.shape == out_shape[n_].shape and arr.dtype == out_shape[n_].dtype
        aliases[len(args)] = n_
        args.append(arr)
        in_specs.append(pl.BlockSpec(memory_space=pl.ANY))
    n_sav = len(out_shape) - n_out
    in_specs += [pl.BlockSpec(memory_space=pl.ANY)] * n_ex
    out_specs += [pl.BlockSpec(memory_space=pl.ANY)] * n_ex
    out_shape += [jax.ShapeDtypeStruct((N_DEV,) + tuple(g.shape), g.dtype) for g in ex]
    cp = pltpu.CompilerParams(dimension_semantics=("arbitrary", "arbitrary"), vmem_limit_bytes=VMEM_LIMIT,
                              has_side_effects=bool(n_ex))
    res = pl.pallas_call(
        body, name=name, grid=(ncol, nb), in_specs=in_specs, out_specs=out_specs, out_shape=out_shape,
        scratch_shapes=[pltpu.VMEM(tuple(s), F32) for s in state_shapes] + (_exchange_sems(n_ex) if n_ex else []),
        input_output_aliases=aliases, compiler_params=cp)(*args, *ex)
    if n_ex:
        return list(res[:n_out]), list(res[n_out:n_out + n_sav]), list(res[n_out + n_sav:])
    return list(res[:n_out]), list(res[n_out:])


def _seq_bwd(name, f, R, rows, params, state_shapes, douts, saved, *, ncol=1, period=None, dx_dt=F32, dx_add=None,
             dx_bf=False, dx_alias=None, exchange=None):
    ex = list(exchange or [])
    n_ex = len(ex)
    nrows = rows[0]['a'].shape[-2]
    nb = nrows // R
    period = nb if period is None else period
    prev_ids = [k for k, r in enumerate(rows) if r['prev']]
    drow_ids = [k for k, r in enumerate(rows) if r['diff']]
    dpar_ids = [k for k, p in enumerate(params) if p['diff']]
    for k in prev_ids:
        assert rows[k]['diff']
    dx_add, dx_alias = dict(dx_add or {}), dict(dx_alias or {})
    add_ids, alias_ids = sorted(dx_add), sorted(dx_alias)
    n_rows, n_prev, n_par, n_do, n_st = len(rows), len(prev_ids), len(params), len(douts), len(state_shapes)
    n_dx, n_dp, n_add, n_al = len(drow_ids), len(dpar_ids), len(add_ids), len(alias_ids)

    def body(*refs):
        o = 0
        cur = refs[o:o + n_rows]; o += n_rows
        prv = refs[o:o + n_prev]; o += n_prev
        par = refs[o:o + n_par]; o += n_par
        sav = refs[o:o + n_st]; o += n_st
        dou = refs[o:o + n_do]; o += n_do
        adr = refs[o:o + n_add]; o += n_add
        o += n_al
        exg = refs[o:o + n_ex]; o += n_ex
        dxr = refs[o:o + n_dx]; o += n_dx
        dpr = refs[o:o + n_dp]; o += n_dp
        dxb = refs[o:o + (n_dx if dx_bf else 0)]; o += len(dxb)
        exr = refs[o:o + n_ex]; o += n_ex
        dst = refs[o:o + n_st]; o += n_st
        car = refs[o:o + n_prev]; o += n_prev
        sems = refs[o:]
        j = pl.program_id(1)
        i = nb - 1 - j
        first = (i % period) == 0
        if n_ex:
            @pl.when((pl.program_id(0) == 0) & (j == 0))
            def _():
                _exchange_start(exg, exr, *sems)

        @pl.when(j == 0)
        def _():
            for s in tuple(dst) + tuple(car) + tuple(dpr):
                s[...] = jnp.zeros_like(s)

        xs = [r[...].astype(F32) for r in cur]
        xp = [r[...].astype(F32) for r in prv]
        ps = [r[...] for r in par]
        sts = [s[0] for s in sav]

        def g(dxs, dxp, dps, dsts):
            xs_, ps_ = list(xs), list(ps)
            for k, v in zip(drow_ids, dxs):
                xs_[k] = v
            for k, v in zip(dpar_ids, dps):
                ps_[k] = v
            ov, ns = f(first, list(dxp), xs_, ps_, list(dsts))
            return tuple(ov), tuple(ns)

        _, vjp = jax.vjp(g, tuple(xs[k] for k in drow_ids), tuple(xp), tuple(ps[k] for k in dpar_ids), tuple(sts))
        dxs, dxp, dps, dsts = vjp((tuple(r[...].astype(F32) for r in dou), tuple(s[...] for s in dst)))
        dxs = list(dxs)
        for n_, pos in enumerate(add_ids):
            dxs[pos] = dxs[pos] + adr[n_][...].astype(F32)
        tails = {}
        for n_, k in enumerate(prev_ids):
            pos = drow_ids.index(k)
            if rows[k]['prev'] == 'block':
                dxs[pos] = dxs[pos] + car[n_][...]
            else:
                tails[pos] = car[n_][...]
            car[n_][...] = dxp[n_]
        for pos, v in enumerate(dxs):
            outs_ = [dxr[pos]] + ([dxb[pos]] if dx_bf else [])
            if pos in tails:
                v = jnp.concatenate([v[..., :R - HALO, :], v[..., R - HALO:, :] + tails[pos]], axis=-2)
            for r in outs_:
                r[...] = v.astype(r.dtype)
        for r, v in zip(dpr, dps):
            r[...] += v
        for s, v in zip(dst, dsts):
            s[...] = v
        if n_ex:
            @pl.when((pl.program_id(0) == ncol - 1) & (j == nb - 1))
            def _():
                _exchange_wait(exg, exr, *sems)

    def rev(j):
        return nb - 1 - j

    def dspec(k):
        return _bspec(rows[k], R, rev)

    in_specs = [_bspec(r, R, rev) for r in rows]
    in_specs += [_pspec(rows[k], R, rev) for k in prev_ids]
    in_specs += [pl.BlockSpec(p['bs'], functools.partial(lambda jc, j, idx: idx(jc), idx=p['idx'])) for p in params]
    in_specs += [pl.BlockSpec((1,) + tuple(s), lambda jc, j, nd=len(s): (nb - 1 - j,) + (0,) * nd) for s in state_shapes]
    in_specs += [_bspec(d, R, rev) for d in douts]
    in_specs += [dspec(drow_ids[pos]) for pos in add_ids]
    in_specs += [pl.BlockSpec(memory_space=pl.ANY) for _ in alias_ids + ex]
    out_specs = [dspec(k) for k in drow_ids]
    out_shape = [jax.ShapeDtypeStruct(tuple(rows[k]['a'].shape[:-1]) + (rows[k]['dn'],), dx_dt) for k in drow_ids]
    for k in dpar_ids:
        p = params[k]
        out_specs.append(pl.BlockSpec(p['bs'], functools.partial(lambda jc, j, idx: idx(jc), idx=p['idx'])))
        out_shape.append(jax.ShapeDtypeStruct(p['a'].shape, F32))
    if dx_bf:
        out_specs += [dspec(k) for k in drow_ids]
        out_shape += [jax.ShapeDtypeStruct(tuple(rows[k]['a'].shape[:-1]) + (rows[k]['dn'],), BF) for k in drow_ids]
    out_specs += [pl.BlockSpec(memory_space=pl.ANY) for _ in ex]
    out_shape += [jax.ShapeDtypeStruct(g.shape, g.dtype) for g in ex]
    scratch = [pltpu.VMEM(tuple(s), F32) for s in state_shapes]
    scratch += [pltpu.VMEM(tuple(d for d in rows[k]['lb'] if d is not None) + (_prev_rows(rows[k], R), rows[k]['w']), F32)
                for k in prev_ids]
    if n_ex:
        scratch += _exchange_sems(n_ex)
    args = ([r['a'] for r in rows] + [rows[k]['a'] for k in prev_ids] + [p['a'] for p in params] + list(saved)
            + [d['a'] for d in douts] + [dx_add[pos] for pos in add_ids] + [dx_alias[pos] for pos in alias_ids])
    n_in = len(args)
    aliases = {n_in - n_al + n_: pos for n_, pos in enumerate(alias_ids)}
    for pos in alias_ids:
        assert dx_alias[pos].shape == out_shape[pos].shape and dx_alias[pos].dtype == out_shape[pos].dtype
    cp = pltpu.CompilerParams(dimension_semantics=("arbitrary", "arbitrary"), vmem_limit_bytes=VMEM_LIMIT,
                              has_side_effects=bool(n_ex))
    res = pl.pallas_call(
        body, name=name, grid=(ncol, nb), in_specs=in_specs, out_specs=out_specs, out_shape=out_shape,
        scratch_shapes=scratch, input_output_aliases=aliases, compiler_params=cp)(*args, *ex)
    lists = [list(res[:n_dx]), list(res[n_dx:n_dx + n_dp])]
    o = n_dx + n_dp
    if dx_bf:
        lists.append(list(res[o:o + n_dx]))
        o += n_dx
    if n_ex:
        lists.append(list(res[o:o + n_ex]))
    return tuple(lists)


def _tile(n, cands):
    for c in cands:
        if n % c == 0:
            return c
    return n


def _mm_call(name, grid, a, a_spec, b, b_spec, contract, out_shape, out_spec, acc_shape, add=None, add_spec=None,
             exchange=None, gather=None):
    nk = grid[2]
    ca, cb = contract
    has_add = add is not None
    ex = list(exchange or []) + list(gather or [])
    ex_shapes = [g.shape for g in exchange or []] + [(N_DEV,) + tuple(g.shape) for g in gather or []]
    n_ex = len(ex)
    n_in = 2 + has_add

    def body(*refs):
        a_ref, b_ref = refs[0], refs[1]
        add_ref = refs[2] if has_add else None
        o_ref = refs[n_in + n_ex]
        scr = refs[n_in + 2 * n_ex + 1:]
        step = [pl.program_id(d) for d in range(3)]
        if n_ex:
            g_refs, r_refs, sems = refs[n_in:n_in + n_ex], refs[n_in + n_ex + 1:n_in + 2 * n_ex + 1], scr[-3:]

            @pl.when((step[0] == 0) & (step[1] == 0) & (step[2] == 0))
            def _():
                _exchange_start(g_refs, r_refs, *sems)

        part = _dot(a_ref[...], b_ref[...], ca, cb)

        def finish(r):
            if has_add:
                r = r + add_ref[...].astype(F32)
            o_ref[...] = r.astype(o_ref.dtype)

        if nk == 1:
            finish(part)
        else:
            acc = scr[0]

            @pl.when(step[2] == 0)
            def _():
                acc[...] = part

            @pl.when(step[2] > 0)
            def _():
                acc[...] += part

            @pl.when(step[2] == nk - 1)
            def _():
                finish(acc[...])

        if n_ex:
            @pl.when((step[0] == grid[0] - 1) & (step[1] == grid[1] - 1) & (step[2] == grid[2] - 1))
            def _():
                _exchange_wait(g_refs, r_refs, *sems)

    in_specs, args = [a_spec, b_spec], [a, b]
    if has_add:
        in_specs.append(add_spec)
        args.append(add)
    any_spec = pl.BlockSpec(memory_space=pl.ANY)
    scratch = [] if nk == 1 else [pltpu.VMEM(acc_shape, F32)]
    if n_ex:
        scratch += _exchange_sems(n_ex)
    res = pl.pallas_call(
        body, name=name, grid=grid, in_specs=in_specs + [any_spec] * n_ex, out_specs=[out_spec] + [any_spec] * n_ex,
        out_shape=[out_shape] + [jax.ShapeDtypeStruct(s, g.dtype) for s, g in zip(ex_shapes, ex)], scratch_shapes=scratch,
        compiler_params=pltpu.CompilerParams(
            dimension_semantics=("arbitrary",) * 3 if n_ex else ("parallel", "parallel", "arbitrary"),
            vmem_limit_bytes=VMEM_LIMIT, has_side_effects=bool(n_ex)))(*args, *ex)
    return (res[0], list(res[1:])) if n_ex else res[0]


def _matmul(a, b, mode="nn", add=None, out_dtype=F32, name="matmul", **pushed):
    if mode == "nn":
        (M, K), N = a.shape, b.shape[1]
    elif mode == "nt":
        (M, K), N = a.shape, b.shape[0]
    else:
        (K, M), N = a.shape, b.shape[1]
    if mode == "tn" and 1024 < N <= 5120:
        tm, tn = _tile(M, (512, 256, 128, 64, 32, 16, 8)), N
        tk = _tile(K, (2048 if tn <= 3072 else 1024, 1024, 512, 256, 128))
    else:
        tk = K if K <= 5120 else _tile(K, (2048, 1024, 512, 256, 128))
        tm = _tile(M, ((2048,) if tk <= 1024 and mode != "tn" else ()) + (1024, 512, 256, 128, 64, 32, 16, 8))
        tn = _tile(N, (512, 256, 128))
    if mode == "tn":
        a_spec = pl.BlockSpec((tk, tm), lambda i, j, k: (k, i))
    else:
        a_spec = pl.BlockSpec((tm, tk), lambda i, j, k: (i, k))
    if mode == "nt":
        b_spec = pl.BlockSpec((tn, tk), lambda i, j, k: (j, k))
    else:
        b_spec = pl.BlockSpec((tk, tn), lambda i, j, k: (k, j))
    blk = pl.BlockSpec((tm, tn), lambda i, j, k: (i, j))
    return _mm_call(name, (M // tm, N // tn, K // tk), a, a_spec, b, b_spec,
                    {"nn": (1, 0), "nt": (1, 1), "tn": (0, 0)}[mode], jax.ShapeDtypeStruct((M, N), out_dtype), blk,
                    (tm, tn), add, blk, **pushed)


FF_SH = 2 * D_FF // N_DEV


def _ffn_up(h2, wup, **pushed):
    S, D = h2.shape
    tm = _tile(S, (2048, 1024, 512, 256, 128))
    return _mm_call("matmul_up", (S // tm, N_DEV, 1), h2, pl.BlockSpec((tm, D), lambda m, j, k: (m, 0)),
                    wup, pl.BlockSpec((None, D, FF_SH), lambda m, j, k: (j, 0, 0)), (1, 0),
                    jax.ShapeDtypeStruct((2, N_DEV // 2, S, FF_SH), F32),
                    pl.BlockSpec((None, None, tm, FF_SH), lambda m, j, k: (j // 4, j % 4, m, 0)), (tm, FF_SH), **pushed)


def _ffn_down(act, wd, x1, **pushed):
    _, S, _ = act.shape
    D = wd.shape[1]
    tm, tn = _tile(S, (1024, 512, 256, 128)), _tile(D, (1024, 512, 256, 128))
    blk = pl.BlockSpec((tm, tn), lambda m, n, k: (m, n))
    return _mm_call("matmul_down", (S // tm, D // tn, N_DEV // 2), act,
                    pl.BlockSpec((None, tm, FF_SH), lambda m, n, k: (k, m, 0)), wd,
                    pl.BlockSpec((FF_SH, tn), lambda m, n, k: (k, n)), (1, 0), jax.ShapeDtypeStruct((S, D), F32), blk,
                    (tm, tn), x1, blk, **pushed)


def _ffn_dact(dxb, wd):
    S, D = dxb.shape
    tm = _tile(S, (2048, 1024, 512, 256, 128))
    return _mm_call("matmul_dact", (S // tm, N_DEV // 2, 1), dxb, pl.BlockSpec((tm, D), lambda m, j, k: (m, 0)), wd,
                    pl.BlockSpec((FF_SH, D), lambda m, j, k: (j, 0)), (1, 1),
                    jax.ShapeDtypeStruct((N_DEV // 2, S, FF_SH), BF),
                    pl.BlockSpec((None, tm, FF_SH), lambda m, j, k: (j, m, 0)), (tm, FF_SH))


def _ffn_dw_down(act, dxb):
    _, S, _ = act.shape
    D = dxb.shape[1]
    tk, tn = _tile(S, (2048, 1024, 512, 256, 128)), _tile(D, (512, 256, 128))
    return _mm_call("matmul_dw_down", (N_DEV // 2, D // tn, S // tk), act,
                    pl.BlockSpec((None, tk, FF_SH), lambda j, n, k: (j, k, 0)), dxb,
                    pl.BlockSpec((tk, tn), lambda j, n, k: (k, n)), (0, 0), jax.ShapeDtypeStruct((D_FF, D), BF),
                    pl.BlockSpec((FF_SH, tn), lambda j, n, k: (j, n)), (FF_SH, tn))


def _ffn_dw_up(h2, du, exchange=None):
    S, D = h2.shape
    tk = _tile(S, (2048, 1024, 512, 256, 128))
    return _mm_call("matmul_dw_up", (N_DEV, 1, S // tk), h2, pl.BlockSpec((tk, D), lambda j, n, k: (k, 0)), du,
                    pl.BlockSpec((None, None, tk, FF_SH), lambda j, n, k: (j // 4, j % 4, k, 0)), (0, 0),
                    jax.ShapeDtypeStruct((N_DEV, D, FF_SH), BF),
                    pl.BlockSpec((None, D, FF_SH), lambda j, n, k: (j, 0, 0)), (D, FF_SH), exchange=exchange)


def _ffn_dh2(du, wup, exchange=None):
    S = du.shape[2]
    D = wup.shape[1]
    tm = _tile(S, (1024, 512, 256, 128))
    return _mm_call("matmul_dh2", (S // tm, 1, N_DEV), du,
                    pl.BlockSpec((None, None, tm, FF_SH), lambda m, n, k: (k // 4, k % 4, m, 0)), wup,
                    pl.BlockSpec((None, D, FF_SH), lambda m, n, k: (k, 0, 0)), (1, 1),
                    jax.ShapeDtypeStruct((S, D), F32), pl.BlockSpec((tm, D), lambda m, n, k: (m, 0)), (tm, D),
                    exchange=exchange)


def f_rmsnorm(first, xp, xs, ps, sts):
    return (_rms(xs[0], ps[0]),), ()


def _same_block(shape, rows_per, cols_per):
    return (lax.broadcasted_iota(jnp.int32, shape, 0) // rows_per) == (lax.broadcasted_iota(jnp.int32, shape, 1) // cols_per)


@jax.custom_vjp
def _head_mean(x):
    n = x.shape[1]
    return _dot_hi(x, jnp.where(_same_block((n, n), XA_HD, XA_HD), 1.0 / XA_HD, 0.0), 1, 0)


_head_mean.defvjp(lambda x: (_head_mean(x), None), lambda r, g: (_head_mean(g),))


def f_xattn(first, xp, xs, ps, sts):
    (xq,), (kv, qn, kn) = xs, ps
    k, v = _split(kv, [D_XA, D_XA])
    m_rows = kv.shape[0]
    q = xq * lax.rsqrt(_head_mean(xq * xq) + EPS) * jnp.concatenate([qn] * XA_HEADS, axis=1)
    k = k * lax.rsqrt(_head_mean(k * k) + EPS) * jnp.concatenate([kn] * XA_HEADS, axis=1)
    kt = k.T
    kbd = jnp.where(_same_block((D_XA, XA_HEADS * m_rows), XA_HD, m_rows), jnp.concatenate([kt] * XA_HEADS, axis=1), 0.0)
    s = mm_nn(q, kbd) * (XA_HD ** -0.5)
    ps_ = []
    for sh in _split(s, [m_rows] * XA_HEADS):
        mx = lax.stop_gradient(jnp.max(sh, axis=-1, keepdims=True))
        p = jnp.exp(sh - mx)
        ps_.append(p / jnp.sum(p, axis=-1, keepdims=True))
    vbd = jnp.where(_same_block((XA_HEADS * m_rows, D_XA), m_rows, XA_HD), jnp.concatenate([v] * XA_HEADS, axis=0), 0.0)
    return (mm_nn(jnp.concatenate(ps_, axis=1), vbd),), ()


def _conv(xp, x, w, b, first, taps):
    xp = jnp.where(first, 0.0, xp)
    y = b + w[taps - 1:taps] * x
    for d in range(1, taps):
        y = y + w[taps - 1 - d:taps - d] * _shift(xp, x, d)
    return y


def _unstack2(x):
    @jax.custom_vjp
    def us(x):
        return x[0], x[1]

    us.defvjp(lambda x: (us(x), None), lambda r, g: (jnp.stack(g),))
    return us(x)


def f_ffn_act(first, xp, xs, ps, sts):
    (up,), (u,), (wg, wv, bg, bv) = xp, xs, ps
    (ugp, uvp), (ug, uv) = _unstack2(up), _unstack2(u)
    gate = _conv(ugp, ug, wg, bg, first, FFN_CONV)
    val = _conv(uvp, uv, wv, bv, first, FFN_CONV)
    return (jax.nn.silu(gate) * val,), ()


def _gla_chunk(q, k, v, la, sts, dk, dv):
    c, nh = q.shape[0], len(sts)
    b = cumsum_rows(la)
    b_last = _row(b, c - 1)
    b_ref = _row(b, c // 2 - 1)
    qe, ke = _split(q * jnp.exp(b - b_ref), [dk] * nh), _split(k * jnp.exp(b_ref - b), [dk] * nh)
    qi, kl = _split(q * jnp.exp(b), [dk] * nh), _split(k * jnp.exp(b_last - b), [dk] * nh)
    dec, vs = _split(jnp.exp(b_last), [dk] * nh), _split(v, [dv] * nh)
    tril = _tril(c)
    outs, new = [], []
    for h in range(nh):
        att = jnp.where(tril, mm_nt(qe[h], ke[h]), 0.0)
        outs.append(mm_nn(att, vs[h]) + mm_nt(qi[h], sts[h]))
        new.append(sts[h] * dec[h] + mm_tn(vs[h], kl[h]))
    return outs, tuple(new)


def _split_rows(x, n):
    c = x.shape[0] // n

    @jax.custom_vjp
    def sp(x):
        return tuple(x[i * c:(i + 1) * c] for i in range(n))

    sp.defvjp(lambda x: (sp(x), None), lambda r, g: (jnp.concatenate(list(g), axis=0),))
    return sp(x)


def _gla_scan(q, k, v, la, sts, dk, dv):
    n = q.shape[0] // CHUNK
    per_chunk = []
    for qc, kc, vc, lc in zip(*(_split_rows(t, n) for t in (q, k, v, la))):
        o, sts = _gla_chunk(qc, kc, vc, lc, sts, dk, dv)
        per_chunk.append(o)
    return [jnp.concatenate([o[h] for o in per_chunk], axis=0) for h in range(len(sts))], sts


def _a_cols(ntot):
    used = D_XA + 2 * GLA_HEADS * GLA_DK + D_MIX + GLA_RANK + D_MIX
    return [D_XA, GLA_HEADS * GLA_DK, GLA_HEADS * GLA_DK, D_MIX, GLA_RANK, D_MIX] + ([ntot - used] if ntot > used else [])


def f_gla(first, xp, xs, ps, sts):
    (p,), (wg2, bg, on) = xs, ps
    parts = _split(p, _a_cols(p.shape[1]))
    q, k, v, glr, og = parts[1:6]
    la = jax.nn.log_sigmoid(mm_nn(glr, wg2) + bg) / GLA_GATE_NORM
    outs, new = _gla_scan(q * (GLA_DK ** -0.5), k, v, la, tuple(sts), GLA_DK, GLA_DV)
    return (jnp.concatenate([_rms(o, on) for o in outs], axis=1) * jax.nn.silu(og),), new


def f_hgrn(first, xp, xs, ps, sts):
    (p,), (lbp, on) = xs, ps
    _, q, fgate, iv, og = _split(p, [D_XA, D_MIX, D_MIX, D_MIX, D_MIX])
    e = jnp.exp(lbp - jnp.max(lbp, axis=0, keepdims=True))
    row = lax.broadcasted_iota(jnp.int32, e.shape, 0)
    lb = jnp.sum(jnp.where(row >= 1, e, 0.0), axis=0, keepdims=True) / jnp.sum(e, axis=0, keepdims=True)
    fg = lb + (1.0 - lb) * jax.nn.sigmoid(fgate)
    outs, new = _gla_scan(jax.nn.silu(q), 1.0 - fg, iv, jnp.log(fg), tuple(sts), HGRN_DK, HGRN_DV)
    return (jnp.concatenate([_rms(o, on) for o in outs], axis=1) * jax.nn.sigmoid(og),), new


def _c_cols(ntot):
    gn = SSM_GROUPS * SSM_STATE
    used = D_XA + D_MIX + D_MIX + 2 * gn + SSM_HEADS
    return [D_XA, D_MIX, D_MIX + 2 * gn, SSM_HEADS] + ([ntot - used] if ntot > used else [])


def f_ssd(first, xp, xs, ps, sts):
    (pp,), (p,), (cw, cb, dtb, alog, dsk, ng) = xp, xs, ps
    gn = SSM_GROUPS * SSM_STATE
    _, z, xbc, dtr = _split(p, _c_cols(p.shape[1]))[:4]
    xbc_p = _split(pp, _c_cols(p.shape[1]))[2]
    xbc = jax.nn.silu(_conv(xbc_p, xbc, cw, cb, first, SSM_CONV))
    xs_, bm, cm = _split(xbc, [D_MIX, gn, gn])
    dt = jax.nn.softplus(dtr + dtb)
    n = p.shape[0] // CHUNK
    ys, sts = [], tuple(sts)
    for xc, bc, cc, dc in zip(*(_split_rows(t, n) for t in (xs_, bm, cm, dt))):
        y, sts = _ssd_chunk(xc, bc, cc, dc, alog, dsk, sts)
        ys.append(y)
    y = jnp.concatenate(ys, axis=0) * jax.nn.silu(z)
    gw = D_MIX // SSM_GROUPS
    yg = _split(y, [gw] * SSM_GROUPS)
    ngs = _split(ng, [gw] * SSM_GROUPS)
    y = jnp.concatenate([_rms(yg[g], ngs[g]) for g in range(SSM_GROUPS)], axis=1)
    return (y,), sts


def _ssd_chunk(xs_, bm, cm, dt, alog, dsk, sts):
    c = xs_.shape[0]
    hg = SSM_HEADS // SSM_GROUPS
    a = dt * (-jnp.exp(alog))
    acs = cumsum_rows(a)
    acs_t = cumsum_rows_t(a)
    acs_last = _row(acs, c - 1)
    dt_h = _split(dt, [1] * SSM_HEADS)
    acs_h = _split(acs, [1] * SSM_HEADS)
    al_h = _split(acs_last, [1] * SSM_HEADS)
    d_h = _split(dsk, [1] * SSM_HEADS)
    x2s = _split(xs_, [2 * SSM_HD] * (SSM_HEADS // 2))
    bms = _split(bm, [SSM_STATE] * SSM_GROUPS)
    cms = _split(cm, [SSM_STATE] * SSM_GROUPS)
    tril = _tril(c)
    cbs = [mm_nt(cms[g], bms[g]) for g in range(SSM_GROUPS)]
    ys, new = [], []
    for j in range(SSM_HEADS // 2):
        g = (2 * j) // hg
        h0, h1 = 2 * j, 2 * j + 1
        xdt = x2s[j] * _lane_pair(dt_h[h0], dt_h[h1])
        acs2 = _lane_pair(acs_h[h0], acs_h[h1])
        al2 = _lane_pair(al_h[h0], al_h[h1])
        yd = []
        for h in (h0, h1):
            seg = acs_h[h] - _row(acs_t, h)
            lm = jnp.exp(jnp.where(tril, seg, NEG))
            yd.append(mm_nn(cbs[g] * lm, xdt))
        lane = lax.broadcasted_iota(jnp.int32, xdt.shape, 1)
        y_diag = jnp.where(lane < SSM_HD, yd[0], yd[1])
        y_off = mm_nn(cms[g], sts[j]) * jnp.exp(acs2)
        x_end = xdt * jnp.exp(al2 - acs2)
        new.append(sts[j] * jnp.exp(al2) + mm_tn(bms[g], x_end))
        ys.append(y_diag + y_off + _lane_pair(d_h[h0], d_h[h1]) * x2s[j])
    return jnp.concatenate(ys, axis=1), tuple(new)


def f_dil_prep(first, xp, xs, ps, sts):
    (p, pos), (qn, kn, invf, sign) = xs, ps
    nh = len(DIL_GROUPS) * DIL_HEADS
    _, q, k, v = _split(p, [D_XA] + [nh * DIL_HD] * 3)
    ang = pos * invf
    cos, sin = jnp.cos(ang), jnp.sin(ang) * sign

    def rope(t, g):
        hs = _split(t, [DIL_HD] * nh)
        out = []
        for h in hs:
            n = _rms(h, g)
            out.append(n * cos + _roll_half(n) * sin)
        return [jnp.concatenate(out[i:i + DIL_HEADS], axis=1) for i in range(0, nh, DIL_HEADS)]

    return tuple(rope(q, qn) + rope(k, kn) + list(_split(v, [D_DIL] * len(DIL_GROUPS)))), ()


def f_dil_attn(first, xp, xs, ps, sts):
    (kp, vp), (q, k, v) = xp, xs
    Q = DIL_BLOCK
    n = q.shape[0] // Q
    qb, kb, vb = (_split_rows(t, n) for t in (q, k, v))
    kprev, vprev = _split_rows(kp, n)[-1], _split_rows(vp, n)[-1]
    i = lax.broadcasted_iota(jnp.int32, (Q, 2 * Q), 0)
    j = lax.broadcasted_iota(jnp.int32, (Q, 2 * Q), 1)
    dist = Q + i - j
    band = (dist >= 0) & (dist <= Q)
    o_rows, lse_rows = [], []
    for b in range(n):
        mask = band & (jnp.logical_not(first) | (j >= Q)) if b == 0 else band
        qs, ks, vs = (_split(t, [DIL_HD] * DIL_HEADS) for t in (qb[b], kb[b], vb[b]))
        kps, vps = (_split(t, [DIL_HD] * DIL_HEADS) for t in (kprev, vprev))
        outs, lses = [], []
        for h in range(DIL_HEADS):
            k2 = jnp.concatenate([kps[h], ks[h]], axis=0)
            v2 = jnp.concatenate([vps[h], vs[h]], axis=0)
            s = jnp.where(mask, mm_nt(qs[h], k2) * (DIL_HD ** -0.5), NEG)
            m = lax.stop_gradient(jnp.max(s, axis=-1, keepdims=True))
            p = jnp.exp(s - m)
            l = jnp.sum(p, axis=-1, keepdims=True)
            outs.append(mm_nn(p / l, v2))
            lses.append(jnp.broadcast_to(m + jnp.log(l), (Q, DIL_HD)))
        o_rows.append(jnp.concatenate(outs, axis=1))
        lse_rows.append(jnp.concatenate(lses, axis=1))
        kprev, vprev = kb[b], vb[b]
    return (jnp.concatenate(o_rows, axis=0), jnp.concatenate(lse_rows, axis=0)), ()


def f_dil_merge(first, xp, xs, ps, sts):
    o0, o1, o2, l0, l1, l2 = xs
    m = jnp.maximum(jnp.maximum(l0, l1), l2)
    e0, e1, e2 = jnp.exp(l0 - m), jnp.exp(l1 - m), jnp.exp(l2 - m)
    den = e0 + e1 + e2
    return ((e0 * o0 + e1 * o1 + e2 * o2) / den,), ()


def _loss_head(y, target):
    S, D = y.shape
    R = _tile(S, (512, 256, 128, 64, 32, 16, 8))

    def body(y_ref, t_ref, dy_ref, dyb_ref, l_ref):
        e = y_ref[...] - t_ref[...]
        dy_ref[...] = e * (1.0 / D)
        dyb_ref[...] = (e * (1.0 / D)).astype(BF)

        @pl.when(pl.program_id(0) == 0)
        def _():
            l_ref[...] = jnp.zeros_like(l_ref)

        l_ref[...] += jnp.broadcast_to(0.5 * jnp.sum(jnp.mean(e * e, axis=-1, keepdims=True), axis=0, keepdims=True),
                                       l_ref.shape)

    blk = pl.BlockSpec((R, D), lambda i: (i, 0))
    dy, dyb, l = pl.pallas_call(
        body, name="loss_head", grid=(S // R,), in_specs=[blk, blk],
        out_specs=[blk, blk, pl.BlockSpec((8, 128), lambda i: (0, 0))],
        out_shape=[jax.ShapeDtypeStruct((S, D), F32), jax.ShapeDtypeStruct((S, D), BF),
                   jax.ShapeDtypeStruct((8, 128), F32)],
        compiler_params=pltpu.CompilerParams(dimension_semantics=("arbitrary",)))(y, target)
    return dy, dyb, l[0, 0]


def _adamw(parts, w, m, v, name):
    _, n, width = parts.shape
    tr = _tile(n, [t for t in (512, 256, 128, 64, 32, 16, 8) if t * width <= ADAM_BLOCK])

    def body(p_ref, w_ref, m_ref, v_ref, g_ref, d_ref, nm_ref, nv_ref):
        g = p_ref[0].astype(F32)
        for s in range(1, N_DEV):
            g = g + p_ref[s].astype(F32)
        nm = ADAM_B1 * m_ref[...] + (1.0 - ADAM_B1) * g
        nv = ADAM_B2 * v_ref[...] + (1.0 - ADAM_B2) * (g * g)
        m_hat = nm / (1.0 - ADAM_B1 ** ADAM_STEP)
        v_hat = nv / (1.0 - ADAM_B2 ** ADAM_STEP)
        g_ref[...] = g
        d_ref[...] = -ADAM_LR * (m_hat / (jnp.sqrt(v_hat) + ADAM_EPS) + ADAM_WD * w_ref[...])
        nm_ref[...] = nm
        nv_ref[...] = nv

    blk = pl.BlockSpec((tr, width), lambda i: (i, 0))
    return pl.pallas_call(
        body, name=name, grid=(n // tr,),
        in_specs=[pl.BlockSpec((N_DEV, tr, width), lambda i: (0, i, 0)), blk, blk, blk],
        out_specs=[blk] * 4, out_shape=[jax.ShapeDtypeStruct((n, width), F32)] * 4,
        compiler_params=pltpu.CompilerParams(dimension_semantics=("arbitrary",), vmem_limit_bytes=VMEM_LIMIT))(
            parts, w, m, v)


def _peer(k):
    x, y, c = lax.axis_index("x"), lax.axis_index("y"), lax.axis_index("c")
    px = 1 - x if k & 4 else x
    py = 1 - y if k & 2 else y
    pc = 1 - c if k & 1 else c
    return (px, py, pc), 4 * px + 2 * py + pc


def _my_id():
    return 4 * lax.axis_index("x") + 2 * lax.axis_index("y") + lax.axis_index("c")


def _all_gather(x, name):
    def body(x_ref, out_ref, send, recv, loc):
        me = _my_id()
        mine = pltpu.make_async_copy(x_ref, out_ref.at[me], loc)
        mine.start()
        cps = []
        for k in range(1, N_DEV):
            peer, _ = _peer(k)
            cp = pltpu.make_async_remote_copy(src_ref=x_ref, dst_ref=out_ref.at[me], send_sem=send.at[k - 1],
                                              recv_sem=recv.at[k - 1], device_id=peer,
                                              device_id_type=pl.DeviceIdType.MESH)
            cp.start()
            cps.append(cp)
        for k in range(1, N_DEV):
            peer, pid = _peer(k)
            pltpu.make_async_remote_copy(src_ref=x_ref, dst_ref=out_ref.at[pid], send_sem=send.at[k - 1],
                                         recv_sem=recv.at[k - 1], device_id=peer,
                                         device_id_type=pl.DeviceIdType.MESH).wait_recv()
        for cp in cps:
            cp.wait_send()
        mine.wait()

    return pl.pallas_call(
        body, name=name, out_shape=jax.ShapeDtypeStruct((N_DEV,) + x.shape, x.dtype),
        in_specs=[pl.BlockSpec(memory_space=pl.ANY)], out_specs=pl.BlockSpec(memory_space=pl.ANY),
        scratch_shapes=[pltpu.SemaphoreType.DMA((N_DEV - 1,)), pltpu.SemaphoreType.DMA((N_DEV - 1,)),
                        pltpu.SemaphoreType.DMA],
        compiler_params=pltpu.CompilerParams(has_side_effects=True))(x)


def _exchange_sems(n):
    return [pltpu.SemaphoreType.DMA((n * (N_DEV - 1),)), pltpu.SemaphoreType.DMA((n * (N_DEV - 1),)),
            pltpu.SemaphoreType.DMA((n,))]


def _exchange_copies(g_refs, out_refs, send, recv, loc, with_arrivals):
    me = _my_id()

    def mine(g, o, d):
        return g.at[d] if len(g.shape) == len(o.shape) else g

    local = [pltpu.make_async_copy(mine(g, o, me), o.at[me], loc.at[w]) for w, (g, o) in enumerate(zip(g_refs, out_refs))]
    pushes, arrivals = [], []
    for k in range(1, N_DEV):
        peer, pid = _peer(k)
        for w, (g, o) in enumerate(zip(g_refs, out_refs)):
            s = w * (N_DEV - 1) + k - 1
            ends = [(mine(g, o, pid), o.at[me], pushes)] + ([(mine(g, o, me), o.at[pid], arrivals)] if with_arrivals else [])
            for src, dst, into in ends:
                into.append(pltpu.make_async_remote_copy(src_ref=src, dst_ref=dst, send_sem=send.at[s],
                                                         recv_sem=recv.at[s], device_id=peer,
                                                         device_id_type=pl.DeviceIdType.MESH))
    return local, pushes, arrivals


def _exchange_start(g_refs, out_refs, send, recv, loc):
    local, pushes, _ = _exchange_copies(g_refs, out_refs, send, recv, loc, False)
    for cp in local + pushes:
        cp.start()


def _exchange_wait(g_refs, out_refs, send, recv, loc):
    local, pushes, arrivals = _exchange_copies(g_refs, out_refs, send, recv, loc, True)
    for cp in arrivals:
        cp.wait_recv()
    for cp in pushes:
        cp.wait_send()
    for cp in local:
        cp.wait()


def _exchange_many(gs, name):
    n = len(gs)

    def body(*refs):
        g_refs, out_refs, sems = refs[:n], refs[n:2 * n], refs[2 * n:]
        _exchange_start(g_refs, out_refs, *sems)
        _exchange_wait(g_refs, out_refs, *sems)

    return pl.pallas_call(
        body, name=name, out_shape=[jax.ShapeDtypeStruct(g.shape, g.dtype) for g in gs],
        in_specs=[pl.BlockSpec(memory_space=pl.ANY)] * n, out_specs=[pl.BlockSpec(memory_space=pl.ANY)] * n,
        scratch_shapes=_exchange_sems(n), compiler_params=pltpu.CompilerParams(has_side_effects=True))(*gs)


def _gather_many(xs, name):
    n = len(xs)

    def body(*refs):
        x_refs, out_refs, (send, recv, loc) = refs[:n], refs[n:2 * n], refs[2 * n:]
        x, y, c = lax.axis_index("x"), lax.axis_index("y"), lax.axis_index("c")
        me, sibling = (x, y, c), (x, y, 1 - c)
        chips = [(1 - x, y), (x, 1 - y), (1 - x, 1 - y)]

        def slot(p):
            return 4 * p[0] + 2 * p[1] + p[2]

        def copy(w, k, block, to, src=None):
            dst = out_refs[w].at[slot(block)]
            return pltpu.make_async_remote_copy(src_ref=dst if src is None else src, dst_ref=dst,
                                                send_sem=send.at[w * (N_DEV - 1) + k], recv_sem=recv.at[w * (N_DEV - 1) + k],
                                                device_id=to, device_id_type=pl.DeviceIdType.MESH)

        mine = [pltpu.make_async_copy(x_refs[w], out_refs[w].at[slot(me)], loc.at[w]) for w in range(n)]
        for cp in mine:
            cp.start()
        first = []
        for j, chip in enumerate(chips):
            first += [copy(w, 1 + j, me, (*chip, c), src=x_refs[w]) for w in range(n)]
        first += [copy(w, 0, me, sibling, src=x_refs[w]) for w in range(n)]
        for cp in first:
            cp.start()
        passed = []
        for j, chip in enumerate(chips):
            for w in range(n):
                copy(w, 1 + j, (*chip, c), me).wait_recv()
                cp = copy(w, 4 + j, (*chip, c), sibling)
                cp.start()
                passed.append(cp)
        for w in range(n):
            copy(w, 0, sibling, me).wait_recv()
            for j, chip in enumerate(chips):
                copy(w, 4 + j, (*chip, 1 - c), me).wait_recv()
        for cp in first + passed:
            cp.wait_send()
        for cp in mine:
            cp.wait()

    return pl.pallas_call(
        body, name=name, out_shape=[jax.ShapeDtypeStruct((N_DEV,) + x.shape, x.dtype) for x in xs],
        in_specs=[pl.BlockSpec(memory_space=pl.ANY)] * n, out_specs=[pl.BlockSpec(memory_space=pl.ANY)] * n,
        scratch_shapes=[pltpu.SemaphoreType.DMA((n * (N_DEV - 1),)), pltpu.SemaphoreType.DMA((n * (N_DEV - 1),)),
                        pltpu.SemaphoreType.DMA((n,))],
        compiler_params=pltpu.CompilerParams(has_side_effects=True))(*xs)


def _cat_segs(G, ws, n_mix):
    segs = []
    for g in range(G):
        lo, hi = g * ws, (g + 1) * ws
        if lo < n_mix:
            segs.append((g, 0, min(hi, n_mix) - lo, D_XA + lo))
        if hi > n_mix:
            s = max(lo, n_mix)
            segs.append((g, s - lo, hi - s, s - n_mix))
    return segs


def _cat_cols(src, n_mix, ntot):
    G, R, ws = src.shape
    segs = _cat_segs(G, ws, n_mix)
    tr = _tile(R, (256, 128, 64, 32, 16, 8))

    def body(i_ref, o_ref):
        if ntot > G * ws:
            o_ref[...] = jnp.zeros_like(o_ref)
        for g, s, n, d in segs:
            o_ref[:, d:d + n] = i_ref[g][:, s:s + n]

    return pl.pallas_call(
        body, name="cat_cols", grid=(R // tr,), in_specs=[pl.BlockSpec((G, tr, ws), lambda i: (0, i, 0))],
        out_specs=pl.BlockSpec((tr, ntot), lambda i: (i, 0)), out_shape=jax.ShapeDtypeStruct((R, ntot), src.dtype),
        compiler_params=pltpu.CompilerParams(dimension_semantics=("arbitrary",)))(src)


def _uncat_cols(dw, G, ws, n_mix):
    R, ntot = dw.shape
    segs = _cat_segs(G, ws, n_mix)
    tr = _tile(R, (256, 128, 64, 32, 16, 8))

    def body(i_ref, o_ref):
        v = i_ref[...]
        for g, s, n, d in segs:
            o_ref[g, :, s:s + n] = v[:, d:d + n]

    return pl.pallas_call(
        body, name="uncat_cols", grid=(R // tr,), in_specs=[pl.BlockSpec((tr, ntot), lambda i: (i, 0))],
        out_specs=pl.BlockSpec((G, tr, ws), lambda i: (0, i, 0)), out_shape=jax.ShapeDtypeStruct((G, R, ws), dw.dtype),
        compiler_params=pltpu.CompilerParams(dimension_semantics=("arbitrary",)))(dw)


PACK_W = 1024


def _granule(n):
    return (256 if n >= 256 * PACK_W else 8) * PACK_W


def _pack(arrs, dtype):
    flat = jnp.concatenate([a.reshape(-1).astype(dtype) for a in arrs])
    n = flat.shape[0]
    pad = (-n) % _granule(n)
    if pad:
        flat = jnp.concatenate([flat, jnp.zeros((pad,), dtype)])
    return flat.reshape(-1, PACK_W)


def _unpack(packed, shapes):
    flat = packed.reshape(-1)
    out, o = [], 0
    for s in shapes:
        n = math.prod(s)
        out.append(flat[o:o + n].reshape(s))
        o += n
    return out


def _pack_lead(arrs, dtype):
    flat = jnp.concatenate([a.reshape(N_DEV, -1).astype(dtype) for a in arrs], axis=1)
    n = flat.shape[1]
    pad = (-n) % _granule(n)
    if pad:
        flat = jnp.concatenate([flat, jnp.zeros((N_DEV, pad), dtype)], axis=1)
    return flat.reshape(N_DEV, -1, PACK_W)


def _to_full(stacked, axis):
    t = jnp.moveaxis(stacked, 0, axis)
    s = list(t.shape)
    return t.reshape(s[:axis] + [s[axis] * s[axis + 1]] + s[axis + 2:])


def _to_chunks(full, axis):
    s = list(full.shape)
    t = full.reshape(s[:axis] + [N_DEV, s[axis] // N_DEV] + s[axis + 1:])
    return jnp.moveaxis(t, axis, 0)


def _rows_of(S):
    return _tile(S, (512, 256, 128, 64))


def _norm_fwd(x, g, dt=BF):
    (h,), _ = _seq_fwd("rmsnorm_fwd", f_rmsnorm, _rows_of(x.shape[0]), [_row_spec(x)], [_par_spec(g)], [],
                       [_out_spec(x.shape[1], dt=dt)])
    return h


def _norm_bwd(x, g, dh, res=None):
    if res is None:
        (dx,), (dg,) = _seq_bwd("rmsnorm_bwd", f_rmsnorm, _rows_of(x.shape[0]), [_row_spec(x)], [_par_spec(g)], [],
                                [_row_spec(dh)], [])
        return dx, None, dg
    (dx,), (dg,), (dxb,) = _seq_bwd("rmsnorm_res_bwd", f_rmsnorm, _rows_of(x.shape[0]), [_row_spec(x)], [_par_spec(g)], [],
                                    [_row_spec(dh)], [], dx_add={0: res}, dx_bf=True)
    return dx, dxb, dg


def _mixer_specs(kind, S, p, w):
    if kind == 0:
        return (f_gla, min(S, MIX_ROWS), [_row_spec(p)],
                [_par_spec(w['a_w_gate2']), _par_spec(w['a_b_gate'].reshape(1, -1)), _par_spec(w['a_o_norm'].reshape(1, -1))],
                [(GLA_DV, GLA_DK)] * GLA_HEADS, D_MIX)
    if kind == 2:
        return (f_ssd, min(S, MIX_ROWS), [_row_spec(p, prev='halo')],
                [_par_spec(w['c_conv_w']), _par_spec(w['c_conv_b'].reshape(1, -1)), _par_spec(w['c_dt_bias'].reshape(1, -1)),
                 _par_spec(w['c_a_log'].reshape(1, -1)), _par_spec(w['c_d'].reshape(1, -1)),
                 _par_spec(w['c_norm'].reshape(1, -1))],
                [(SSM_STATE, 2 * SSM_HD)] * (SSM_HEADS // 2), D_MIX)
    return (f_hgrn, min(S, MIX_ROWS), [_row_spec(p)],
            [_par_spec(w['d_lower_bounds']), _par_spec(w['d_o_norm'].reshape(1, -1))],
            [(HGRN_DV, HGRN_DK)] * HGRN_HEADS, D_MIX)


def _perm(t, r):
    if r == 1:
        return t
    S, n = t.shape
    return t.reshape(S // r, r, n).transpose(1, 0, 2).reshape(S, n)


def _unperm(t, r):
    if r == 1:
        return t
    S, n = t.shape
    return t.reshape(r, S // r, n).transpose(1, 0, 2).reshape(S, n)


def _rope_consts():
    half = DIL_HD // 2
    inv = ROPE_THETA ** (-jnp.arange(half, dtype=F32) / half)
    invf = jnp.concatenate([inv, inv]).reshape(1, DIL_HD)
    sign = jnp.concatenate([-jnp.ones((half,), F32), jnp.ones((half,), F32)]).reshape(1, DIL_HD)
    return invf, sign


def _dil_fwd(p, pos, w, ncat):
    S = p.shape[0]
    invf, sign = _rope_consts()
    prep_rows = [_row_spec(p), _row_spec(pos, diff=False)]
    prep_pars = [_par_spec(w['b_q_norm'].reshape(1, -1)), _par_spec(w['b_k_norm'].reshape(1, -1)),
                 _par_spec(invf, diff=False), _par_spec(sign, diff=False)]
    ng = len(DIL_GROUPS)
    qkv, _ = _seq_fwd("dil_prep_fwd", f_dil_prep, _tile(S, (256, 128)), prep_rows, prep_pars, [],
                      [_out_spec(D_DIL) for _ in range(3 * ng)])
    res = dict(perm=[], o=[], lse=[])
    for g, (window, r) in enumerate(DIL_GROUPS):
        qp, kp, vp = _perm(qkv[g], r), _perm(qkv[ng + g], r), _perm(qkv[2 * ng + g], r)
        rows = [_row_spec(qp), _row_spec(kp, prev='block'), _row_spec(vp, prev='block')]
        Rg = min(DIL_ROWS, S // r)
        (o, lse), _ = _seq_fwd("dil_attn_fwd", f_dil_attn, Rg, rows, [], [], [_out_spec(D_DIL), _out_spec(D_DIL)],
                               period=S // r // Rg)
        res['perm'].append((qp, kp, vp))
        res['o'].append(_unperm(o, r))
        res['lse'].append(_unperm(lse, r))
    mrows = [_row_spec(t) for t in res['o'] + res['lse']]
    (cat,), _ = _seq_fwd("dil_merge_fwd", f_dil_merge, _rows_of(S), mrows, [], [], [_out_spec(ncat, w=D_DIL, dt=BF)])
    res['prep'] = (prep_rows, prep_pars)
    return cat, res


def _dil_bwd(dtok, res, p, exchange=None, exchange_local=None):
    S = p.shape[0]
    mrows = [_row_spec(t) for t in res['o'] + res['lse']]
    dm, _ = _seq_bwd("dil_merge_bwd", f_dil_merge, _rows_of(S), mrows, [], [], [dtok], [])
    dq, dk, dv = [], [], []
    for g, (window, r) in enumerate(DIL_GROUPS):
        qp, kp, vp = res['perm'][g]
        rows = [_row_spec(qp), _row_spec(kp, prev='block'), _row_spec(vp, prev='block')]
        douts = [_row_spec(_perm(dm[g], r)), _row_spec(_perm(dm[3 + g], r))]
        Rg = min(DIL_ROWS, S // r)
        (a, b, c), _ = _seq_bwd("dil_attn_bwd", f_dil_attn, Rg, rows, [], [], douts, [], period=S // r // Rg)
        dq.append(_unperm(a, r)); dk.append(_unperm(b, r)); dv.append(_unperm(c, r))
    prep_rows, prep_pars = res['prep']
    res = _seq_bwd("dil_prep_bwd", f_dil_prep, _tile(S, (256, 128)), prep_rows, prep_pars, [],
                   [_row_spec(t) for t in dq + dk + dv], [], dx_dt=BF, exchange=exchange)
    (dp,), (dqn, dkn) = res[0], res[1]
    return dp, dict(b_q_norm=dqn.reshape(-1), b_k_norm=dkn.reshape(-1)), (res[2] if exchange else exchange_local)


def _ffn_specs(u, cw, cb):
    half = N_DEV // 2
    rows = [_row_spec(u, prev='halo', lb=(2, None), li=lambda jc: (0, jc))]
    pars = [_par_spec(cw, bs=(None, FFN_CONV, FF_SH), idx=lambda jc: (jc, 0, 0)),
            _par_spec(cw, bs=(None, FFN_CONV, FF_SH), idx=lambda jc: (jc + half, 0, 0)),
            _par_spec(cb, bs=(None, 1, FF_SH), idx=lambda jc: (jc, 0, 0)),
            _par_spec(cb, bs=(None, 1, FF_SH), idx=lambda jc: (jc + half, 0, 0))]
    return half, rows, pars


N_MIX = {0: 2 * GLA_HEADS * GLA_DK + 2 * D_MIX + GLA_RANK, 1: 3 * len(DIL_GROUPS) * D_DIL,
         2: 2 * D_MIX + 2 * SSM_GROUPS * SSM_STATE + SSM_HEADS, 3: 2 * HGRN_HEADS * HGRN_DK + 2 * D_MIX}
W_IN = {0: 'a_w_in', 1: 'b_w_in', 2: 'c_w_in', 3: 'd_w_in'}
W_OUT = {0: 'a_w_out', 1: 'b_w_out', 2: 'c_w_out', 3: 'd_w_out'}


def _in_blocks(name, t):
    return t if SHARD_AXIS[name] == 1 else t.reshape(1, N_DEV * t.shape[1], t.shape[2])


LAYER_STACKED = ('ffn_w_up', 'ffn_conv_w', 'ffn_w_down', 'xa_w_kv')


SMALL_OF_KIND = {0: ['a_w_gate2'], 2: ['c_conv_w']}


def _layer_names(i):
    return list(LAYER_STACKED) + [W_IN[i % 4], W_OUT[i % 4]] + SMALL_OF_KIND.get(i % 4, [])


def _device_step(x, mem, pos, sh, rep, target, distributed=True):
    S, D = x.shape
    w = dict(rep)
    posf = pos.reshape(S, 1).astype(F32)
    n_mix, w_in_name, w_out_name = N_MIX, W_IN, W_OUT
    ntot = {k: -(-(n_mix[k] + D_XA) // 256) * 256 for k in n_mix}
    mem_g = w['mem_norm'].reshape(1, -1)
    mem_n = _norm_fwd(mem, mem_g)
    R = _rows_of(S)

    def mine(i):
        return {n: (sh[n][i] if n in LAYER_STACKED else sh[n]) for n in _layer_names(i)}

    if distributed:
        gl = dict(zip(_layer_names(0), _gather_many(list(mine(0).values()), "gather_weights")))
    else:
        gl = {n: (sh[n][:, 0] if n in LAYER_STACKED else sh[n]) for n in _layer_names(0)}

    saved = []
    for i in range(DEPTH):
        kind = i % 4
        L = dict(x0=x)
        for n in SMALL_OF_KIND.get(kind, []):
            w[n] = _to_full(gl[n], 1)
        in_blocks = _in_blocks(w_in_name[kind], gl[w_in_name[kind]])
        w_out = (_to_full(gl[w_out_name[kind]], 1) if SHARD_AXIS[w_out_name[kind]] == 1
                 else gl[w_out_name[kind]].reshape(-1, D))
        nxt, push = {}, [[], [], []]
        if i + 1 < DEPTH:
            if distributed:
                nxt = mine(i + 1)
                push = [[n for n in nxt if n not in ('ffn_w_up', w_in_name[(i + 1) % 4], w_out_name[(i + 1) % 4])],
                        ['ffn_w_up'], [w_in_name[(i + 1) % 4], w_out_name[(i + 1) % 4]]]
            else:
                nxt = {n: (sh[n][:, i + 1] if n in LAYER_STACKED else sh[n]) for n in _layer_names(i + 1)}
        got = dict(nxt) if not distributed else {}

        def hosted(call, names):
            if not names:
                return call()
            res, arrived = call(gather=[nxt[n] for n in names])
            got.update(zip(names, arrived))
            return res

        g1 = w['mix_norm'][i].reshape(1, -1)
        h = _norm_fwd(x, g1)
        wcat = _cat_cols(in_blocks, n_mix[kind], ntot[kind])
        p = hosted(functools.partial(_matmul, h, wcat, name="matmul_in"), push[0])
        ntok = D_DIL if kind == 1 else D_MIX
        if kind == 1:
            cat, L['dil'] = _dil_fwd(p, posf, w, ntok + D_XA)
        else:
            f, Rm, rows, pars, sshapes, _ = _mixer_specs(kind, S, p, w)
            (cat,), L['states'] = _seq_fwd("mixer%d_fwd" % kind, f, Rm, rows, pars, sshapes,
                                           [_out_spec(ntok + D_XA, w=ntok, dt=BF)], save_states=True)
        wkv = gl['xa_w_kv'].reshape(D, 2 * D_XA)
        kv = _matmul(mem_n, wkv, name="matmul_kv")
        xa_rows = [_row_spec(p, w=D_XA, dn=D_XA)]
        xa_pars = [_par_spec(kv), _par_spec(w['xa_q_norm'][i].reshape(1, -1)), _par_spec(w['xa_k_norm'][i].reshape(1, -1))]
        (cat,), _ = _seq_fwd("xattn_fwd", f_xattn, R, xa_rows, xa_pars, [],
                             [_out_spec(ntok + D_XA, w=D_XA, c=lambda jc: ntok // D_XA, dt=BF)], out_alias={0: cat})
        x1 = _matmul(cat, w_out, add=x, name="matmul_out")
        g2 = w['ffn_norm'][i].reshape(1, -1)
        h2 = _norm_fwd(x1, g2)
        wup = gl['ffn_w_up']
        u = hosted(functools.partial(_ffn_up, h2, wup), push[1])
        cw, cb = gl['ffn_conv_w'], w['ffn_conv_b'][i].reshape(N_DEV, 1, FF_SH)
        nt, frows, fpars = _ffn_specs(u, cw, cb)
        res = _seq_fwd("ffn_act_fwd", f_ffn_act, R, frows, fpars, [],
                       [_out_spec(FF_SH, dt=BF, ls=(nt,), lb=(None,), li=lambda jc: (jc,))], ncol=nt,
                       gather=[nxt[n] for n in push[2]] if push[2] else None)
        (act,) = res[0]
        if push[2]:
            got.update(zip(push[2], res[2]))
        wd = gl['ffn_w_down'].reshape(D_FF, D)
        x = _ffn_down(act, wd, x1)
        L.update(h=h, p=p, wcat=wcat, kv=kv, wkv=wkv, cat=cat, x1=x1, h2=h2, u=u, act=act, wd=wd, wup=wup, cw=cw, g1=g1,
                 g2=g2, in_blocks=in_blocks, w_out=w_out, shapes={n: t.shape for n, t in gl.items()})
        saved.append(L)
        gl = got

    dx, dxb, loss = _loss_head(x, target)

    G = {}
    d_mem_n = None
    acc = {k: [None] * DEPTH for k in ('mix_norm', 'ffn_norm', 'ffn_conv_b', 'xa_q_norm', 'xa_k_norm')}
    parts = [{} for _ in range(DEPTH)]
    pending = {}
    half = N_DEV // 2

    def sent(call, blocks, layer):
        if not blocks:
            return call()
        if not distributed:
            parts[layer].update(blocks)
            return call()
        res, arrived = call(exchange=list(blocks.values()))
        parts[layer].update(zip(blocks, arrived))
        return res

    for i in reversed(range(DEPTH)):
        kind = i % 4
        L = saved[i]
        Gc = {}
        Gc['ffn_w_down'] = _ffn_dw_down(L['act'], dxb).reshape(N_DEV, D_FF // N_DEV, D)
        dact = _ffn_dact(dxb, L['wd'])
        cw, cb = L['cw'], w['ffn_conv_b'][i].reshape(N_DEV, 1, FF_SH)
        nt, frows, fpars = _ffn_specs(L['u'], cw, cb)
        (du,), (dwg, dwv, dbg, dbv) = _seq_bwd(
            "ffn_act_bwd", f_ffn_act, R, frows, fpars, [], [_row_spec(dact, lb=(None,), li=lambda jc: (jc,))], [],
            ncol=nt, dx_dt=BF)
        Gc['ffn_conv_w'] = jnp.concatenate([dwg[:half], dwv[half:]], axis=0)
        acc['ffn_conv_b'][i] = jnp.concatenate([dbg[:half], dbv[half:]], axis=0).reshape(-1)
        Gc['ffn_w_up'] = _ffn_dw_up(L['h2'], du)
        dh2 = sent(functools.partial(_ffn_dh2, du, L['wup']), pending, i + 1)
        dx1, dx1b, dg2 = _norm_bwd(L['x1'], L['g2'], dh2, res=dx)
        acc['ffn_norm'][i] = dg2.reshape(-1)
        G_out = _matmul(L['cat'], dx1b, mode="tn", out_dtype=BF, name="matmul_dw_out")
        dcat = _matmul(dx1b, L['w_out'], mode="nt", name="matmul_dcat")
        ntok = D_DIL if kind == 1 else D_MIX
        dtok = _row_spec(dcat, w=ntok)
        dxa = _row_spec(dcat, w=D_XA, c=lambda jc: ntok // D_XA)
        p = L['p']
        up_names = ['ffn_w_up', w_out_name[kind]]
        Gc[w_out_name[kind]] = (_to_chunks(G_out, 1) if SHARD_AXIS[w_out_name[kind]] == 1
                                else G_out.reshape(L['shapes'][w_out_name[kind]]))
        up = [Gc.pop(n) for n in up_names]
        if kind == 1:
            dp, gm, got_up = _dil_bwd(dtok, L['dil'], p, up if distributed else None, up)
            G.update(gm)
        else:
            f, Rm, rows, pars, sshapes, _ = _mixer_specs(kind, S, p, w)
            res = _seq_bwd("mixer%d_bwd" % kind, f, Rm, rows, pars, sshapes, [dtok], L['states'], dx_dt=BF,
                           exchange=up if distributed else None)
            (dp,), dps, got_up = res[0], res[1], (res[2] if distributed else up)
            if kind == 0:
                Gc['a_w_gate2'], G['a_b_gate'], G['a_o_norm'] = _to_chunks(dps[0], 1), dps[1].reshape(-1), dps[2].reshape(-1)
            elif kind == 2:
                Gc['c_conv_w'] = _to_chunks(dps[0], 1)
                for nme, v in zip(('c_conv_b', 'c_dt_bias', 'c_a_log', 'c_d', 'c_norm'), dps[1:]):
                    G[nme] = v.reshape(-1)
            else:
                G['d_lower_bounds'], G['d_o_norm'] = dps[0], dps[1].reshape(-1)
        xa_rows = [_row_spec(p, w=D_XA)]
        xa_pars = [_par_spec(L['kv']), _par_spec(w['xa_q_norm'][i].reshape(1, -1)), _par_spec(w['xa_k_norm'][i].reshape(1, -1))]
        (dp,), (dkv, dqn, dkn) = _seq_bwd("xattn_bwd", f_xattn, R, xa_rows, xa_pars, [], [dxa], [], dx_dt=BF,
                                          dx_alias={0: dp})
        acc['xa_q_norm'][i], acc['xa_k_norm'][i] = dqn.reshape(-1), dkn.reshape(-1)
        Gc['xa_w_kv'] = _matmul(mem_n, dkv, mode="tn", out_dtype=BF, name="matmul_dw_kv").reshape(
            N_DEV, D // N_DEV, 2 * D_XA)
        d_mem_n = _matmul(dkv, L['wkv'], mode="nt", add=d_mem_n, name="matmul_dmem" + ("" if d_mem_n is None else "_acc"))
        parts[i].update(zip(up_names, got_up))
        dwcat = _matmul(L['h'], dp, mode="tn", out_dtype=BF, name="matmul_dw_in")
        blocks = L['in_blocks']
        Gc[w_in_name[kind]] = _uncat_cols(dwcat, blocks.shape[0], blocks.shape[2], n_mix[kind]).reshape(
            L['shapes'][w_in_name[kind]])
        dh = sent(functools.partial(_matmul, dp, L['wcat'], mode="nt", name="matmul_dh"),
                  {n: Gc.pop(n) for n in ('ffn_w_down', 'ffn_conv_w')}, i)
        dx, dxb, dg1 = _norm_bwd(L['x0'], L['g1'], dh, res=dx1)
        acc['mix_norm'][i] = dg1.reshape(-1)
        pending = Gc

    if distributed:
        names = list(pending)
        parts[0].update(zip(names, _exchange_many([pending[n] for n in names], "exchange_grads")))
    else:
        parts[0].update(pending)
    _, _, dmg = _norm_bwd(mem, mem_g, d_mem_n)
    G['mem_norm'] = dmg.reshape(-1)
    for k, v in acc.items():
        G[k] = jnp.stack(v)
    got = {}
    for i in range(DEPTH):
        for n, t in parts[i].items():
            if n not in LAYER_STACKED:
                got[n] = t
    for n in LAYER_STACKED:
        got[n] = jnp.stack([parts[i][n] for i in range(DEPTH)], axis=1)
    return loss, dx, got, G


def kernel(x, mem, positions, mem_norm, mix_norm, xa_w_kv, xa_q_norm, xa_k_norm, ffn_norm, ffn_w_up, ffn_conv_w, ffn_conv_b, ffn_w_down, a_w_in, a_w_gate2, a_b_gate, a_o_norm, a_w_out, b_w_in, b_q_norm, b_k_norm, b_w_out, c_w_in, c_conv_w, c_conv_b, c_dt_bias, c_a_log, c_d, c_norm, c_w_out, d_w_in, d_lower_bounds, d_o_norm, d_w_out, loss_target, m_mem_norm, m_mix_norm, m_xa_w_kv, m_xa_q_norm, m_xa_k_norm, m_ffn_norm, m_ffn_w_up, m_ffn_conv_w, m_ffn_conv_b, m_ffn_w_down, m_a_w_in, m_a_w_gate2, m_a_b_gate, m_a_o_norm, m_a_w_out, m_b_w_in, m_b_q_norm, m_b_k_norm, m_b_w_out, m_c_w_in, m_c_conv_w, m_c_conv_b, m_c_dt_bias, m_c_a_log, m_c_d, m_c_norm, m_c_w_out, m_d_w_in, m_d_lower_bounds, m_d_o_norm, m_d_w_out, v_mem_norm, v_mix_norm, v_xa_w_kv, v_xa_q_norm, v_xa_k_norm, v_ffn_norm, v_ffn_w_up, v_ffn_conv_w, v_ffn_conv_b, v_ffn_w_down, v_a_w_in, v_a_w_gate2, v_a_b_gate, v_a_o_norm, v_a_w_out, v_b_w_in, v_b_q_norm, v_b_k_norm, v_b_w_out, v_c_w_in, v_c_conv_w, v_c_conv_b, v_c_dt_bias, v_c_a_log, v_c_d, v_c_norm, v_c_w_out, v_d_w_in, v_d_lower_bounds, v_d_o_norm, v_d_w_out):
    args = locals()
    w = {n: args[n] for n in WEIGHTS}
    m = {n: args['m_' + n] for n in WEIGHTS}
    v = {n: args['v_' + n] for n in WEIGHTS}

    big = [n for n in SHARDED if w[n].size >= 65536]
    small = [n for n in SHARDED if n not in big]
    sh = {n: (w[n].astype(BF) if n in big else w[n]) for n in SHARDED}
    loss, grad_x, parts, G = _device_step(x[0], mem[0], positions[0], sh, {n: w[n] for n in REPLICATED}, loss_target[0])
    loss = lax.psum(loss, ("x", "y", "c"))
    rep_parts = _all_gather(_pack([G[n] for n in REPLICATED], F32), "gather_replicated_grads")

    out = {}

    def put(names, res, shapes):
        for kind, r in zip(("grad", "delta", "new_m", "new_v"), res):
            for n, t in zip(names, _unpack(r, shapes)):
                out[kind + "_" + n] = t

    for n in big:
        shp = tuple(w[n].shape)
        two_d = (math.prod(shp[:-1]), shp[-1])
        res = _adamw(parts[n].reshape((N_DEV,) + two_d), w[n].reshape(two_d), m[n].reshape(two_d), v[n].reshape(two_d),
                     "adamw")
        for kind, r in zip(("grad", "delta", "new_m", "new_v"), res):
            out[kind + "_" + n] = r.reshape(shp)
    for names, prt, tag in ((small, _pack_lead([parts[n] for n in small], F32), "adamw_small"),
                            (REPLICATED, rep_parts, "adamw_replicated")):
        res = _adamw(prt, _pack([w[n] for n in names], F32), _pack([m[n] for n in names], F32),
                     _pack([v[n] for n in names], F32), tag)
        put(names, res, [tuple(w[n].shape) for n in names])
    return (loss, grad_x[None], *[out["grad_" + n] for n in WEIGHTS], *[out["delta_" + n] for n in WEIGHTS],
            *[out["new_m_" + n] for n in WEIGHTS], *[out["new_v_" + n] for n in WEIGHTS])
```

```python
import functools
import math

import jax
import jax.numpy as jnp
from jax import lax
from jax.experimental import pallas as pl
from jax.experimental.pallas import tpu as pltpu

F32 = jnp.float32
BF = jnp.bfloat16
_MM_DTYPE = BF

N_DEV = 8
EPS = 1e-6
ROPE_THETA = 10000.0
CHUNK = 64
MIX_ROWS = 256
DIL_ROWS = 512
FFN_ROWS = 256
D_MIX = 768
XA_HEADS, XA_HD, D_XA = 4, 64, 256
GLA_HEADS, GLA_DK, GLA_DV, GLA_RANK, GLA_GATE_NORM = 4, 96, 192, 16, 16.0
DIL_GROUPS = ((128, 1), (512, 4), (2048, 16))
DIL_HEADS, DIL_HD, DIL_BLOCK, D_DIL = 4, 128, 128, 512
SSM_HD, SSM_HEADS, SSM_GROUPS, SSM_STATE, SSM_CONV = 64, 12, 2, 128, 4
HGRN_HEADS, HGRN_DK, HGRN_DV = 6, 128, 128
D_FF = 2816
FFN_CONV = 3
DEPTH = 4
ADAM_LR, ADAM_B1, ADAM_B2, ADAM_EPS, ADAM_WD, ADAM_STEP = 0.001, 0.9, 0.999, 1e-08, 0.01, 10
NEG = -1e30
HALO = 8
VMEM_LIMIT = 56 << 20
ADAM_BLOCK = 1 << 18

WEIGHTS = ['mem_norm', 'mix_norm', 'xa_w_kv', 'xa_q_norm', 'xa_k_norm', 'ffn_norm', 'ffn_w_up', 'ffn_conv_w',
           'ffn_conv_b', 'ffn_w_down', 'a_w_in', 'a_w_gate2', 'a_b_gate', 'a_o_norm', 'a_w_out', 'b_w_in', 'b_q_norm',
           'b_k_norm', 'b_w_out', 'c_w_in', 'c_conv_w', 'c_conv_b', 'c_dt_bias', 'c_a_log', 'c_d', 'c_norm', 'c_w_out',
           'd_w_in', 'd_lower_bounds', 'd_o_norm', 'd_w_out']
SHARD_AXIS = {'xa_w_kv': 1, 'ffn_w_up': 2, 'ffn_conv_w': 2, 'ffn_w_down': 1, 'a_w_in': 1, 'a_w_gate2': 1, 'a_w_out': 0,
              'b_w_in': 1, 'b_w_out': 1, 'c_w_in': 0, 'c_conv_w': 1, 'c_w_out': 0, 'd_w_in': 1, 'd_w_out': 0}
SHARDED = [n for n in WEIGHTS if n in SHARD_AXIS]
REPLICATED = [n for n in WEIGHTS if n not in SHARD_AXIS]


def _dot(a, b, ca, cb):
    return lax.dot_general(a.astype(_MM_DTYPE), b.astype(_MM_DTYPE), (((ca,), (cb,)), ((), ())),
                           preferred_element_type=F32)


@jax.custom_vjp
def mm_nn(a, b):
    return _dot(a, b, 1, 0)


mm_nn.defvjp(lambda a, b: (_dot(a, b, 1, 0), (a, b)),
             lambda r, g: (_dot(g, r[1], 1, 1), _dot(r[0], g, 0, 0)))


@jax.custom_vjp
def mm_nt(a, b):
    return _dot(a, b, 1, 1)


mm_nt.defvjp(lambda a, b: (_dot(a, b, 1, 1), (a, b)),
             lambda r, g: (_dot(g, r[1], 1, 0), _dot(g, r[0], 0, 0)))


@jax.custom_vjp
def mm_tn(a, b):
    return _dot(a, b, 0, 0)


mm_tn.defvjp(lambda a, b: (_dot(a, b, 0, 0), (a, b)),
             lambda r, g: (_dot(r[1], g, 1, 1), _dot(r[0], g, 1, 0)))


def _dot_hi(a, b, ca, cb):
    return lax.dot_general(a, b, (((ca,), (cb,)), ((), ())), precision=lax.Precision.HIGHEST,
                           preferred_element_type=F32)


def _tril(c):
    return lax.broadcasted_iota(jnp.int32, (c, c), 0) >= lax.broadcasted_iota(jnp.int32, (c, c), 1)


@jax.custom_vjp
def cumsum_rows(x):
    return _dot_hi(_tril(x.shape[0]).astype(F32), x, 1, 0)


cumsum_rows.defvjp(lambda x: (cumsum_rows(x), None),
                   lambda r, g: (_dot_hi(_tril(g.shape[0]).astype(F32), g, 0, 0),))


@jax.custom_vjp
def cumsum_rows_t(x):
    return _dot_hi(x, _tril(x.shape[0]).astype(F32), 0, 1)


cumsum_rows_t.defvjp(lambda x: (cumsum_rows_t(x), None),
                     lambda r, g: (_dot_hi(_tril(g.shape[1]).astype(F32), g, 0, 1),))


def _split(x, sizes):
    sizes = tuple(int(s) for s in sizes)
    assert sum(sizes) == x.shape[-1], (sizes, x.shape)

    @jax.custom_vjp
    def sp(x):
        out, o = [], 0
        for s in sizes:
            out.append(x[:, o:o + s])
            o += s
        return tuple(out)

    sp.defvjp(lambda x: (sp(x), None), lambda r, g: (jnp.concatenate(list(g), axis=1),))
    return sp(x)


def _row(x, r):
    m = lax.broadcasted_iota(jnp.int32, x.shape, 0) == r
    return jnp.sum(jnp.where(m, x, 0.0), axis=0, keepdims=True)


@jax.custom_vjp
def _roll_half(x):
    return pltpu.roll(x, 64, 1)


_roll_half.defvjp(lambda x: (pltpu.roll(x, 64, 1), None), lambda r, g: (pltpu.roll(g, 64, 1),))


def _shift(xp, x, d):
    if d == 0:
        return x
    n, m = x.shape[0], xp.shape[0]
    assert d <= m == HALO

    @jax.custom_vjp
    def sh(xp, x):
        r = pltpu.roll(x, d, 0)
        row = lax.broadcasted_iota(jnp.int32, xp.shape, 0)
        head = jnp.where(row < d, pltpu.roll(xp, d, 0), r[:m])
        return jnp.concatenate([head, r[m:]], axis=0)

    def bwd(_, g):
        row = lax.broadcasted_iota(jnp.int32, g.shape, 0)
        rowp = lax.broadcasted_iota(jnp.int32, (m,) + g.shape[1:], 0)
        dxp = jnp.where(rowp >= m - d, pltpu.roll(g[:m], m - d, 0), 0.0)
        return dxp, jnp.where(row < n - d, pltpu.roll(g, n - d, 0), 0.0)

    sh.defvjp(lambda xp, x: (sh(xp, x), None), bwd)
    return sh(xp, x)


def _rms(x, g):
    return x * lax.rsqrt(jnp.mean(x * x, axis=-1, keepdims=True) + EPS) * g


def _lane_pair(a, b, width=128):
    shape = a.shape[:-1] + (width,)
    lane = lax.broadcasted_iota(jnp.int32, shape, len(shape) - 1)
    return jnp.where(lane < width // 2, a, b)


def _row_spec(a, w=None, c=None, prev=False, diff=True, dn=None, lb=(), li=None):
    return dict(a=a, w=a.shape[-1] if w is None else w, c=(lambda jc: 0) if c is None else c, prev=prev, diff=diff,
                dn=a.shape[-1] if dn is None else dn, lb=tuple(lb), li=(lambda jc: ()) if li is None else li)


def _par_spec(a, bs=None, idx=None, diff=True):
    nd = a.ndim
    return dict(a=a, bs=tuple(a.shape) if bs is None else tuple(bs),
                idx=(lambda jc: (0,) * nd) if idx is None else idx, diff=diff)


def _out_spec(n, w=None, c=None, dt=F32, ls=(), lb=(), li=None):
    return dict(n=n, w=n if w is None else w, c=(lambda jc: 0) if c is None else c, dt=dt, ls=tuple(ls), lb=tuple(lb),
                li=(lambda jc: ()) if li is None else li)


def _cparams():
    return pltpu.CompilerParams(dimension_semantics=("arbitrary", "arbitrary"), vmem_limit_bytes=VMEM_LIMIT)


def _bspec(s, R, rowfn):
    return pl.BlockSpec(s['lb'] + (R, s['w']),
                        functools.partial(lambda jc, i, s: tuple(s['li'](jc)) + (rowfn(i), s['c'](jc)), s=s))


def _prev_rows(s, R):
    return R if s['prev'] == 'block' else HALO


def _pspec(s, R, blockfn):
    pr = _prev_rows(s, R)
    return pl.BlockSpec(s['lb'] + (pr, s['w']), functools.partial(
        lambda jc, i, s: tuple(s['li'](jc)) + (jnp.maximum(blockfn(i) * (R // pr) - 1, 0), s['c'](jc)), s=s))


def _seq_fwd(name, f, R, rows, params, state_shapes, outs, *, ncol=1, period=None, save_states=False, out_alias=None,
             gather=None):
    nrows = rows[0]['a'].shape[-2]
    nb = nrows // R
    assert nb * R == nrows
    period = nb if period is None else period
    prev_ids = [k for k, r in enumerate(rows) if r['prev']]
    n_rows, n_prev, n_par, n_out, n_st = len(rows), len(prev_ids), len(params), len(outs), len(state_shapes)
    ex = list(gather or [])
    n_ex = len(ex)

    def body(*refs):
        o = 0
        cur = refs[o:o + n_rows]; o += n_rows
        prv = refs[o:o + n_prev]; o += n_prev
        par = refs[o:o + n_par]; o += n_par + len(out_alias or {})
        exg = refs[o:o + n_ex]; o += n_ex
        out = refs[o:o + n_out]; o += n_out
        sav = refs[o:o + (n_st if save_states else 0)]; o += len(sav)
        exr = refs[o:o + n_ex]; o += n_ex
        st = refs[o:o + n_st]; o += n_st
        sems = refs[o:]
        i = pl.program_id(1)
        first = (i % period) == 0
        if n_ex:
            @pl.when((pl.program_id(0) == 0) & (i == 0))
            def _():
                _exchange_start(exg, exr, *sems)

        @pl.when(i == 0)
        def _():
            for s in st:
                s[...] = jnp.zeros_like(s)

        xs = [r[...].astype(F32) for r in cur]
        xp = [r[...].astype(F32) for r in prv]
        ps = [r[...] for r in par]
        sts = [s[...] for s in st]
        for sv, s in zip(sav, sts):
            sv[0] = s
        ov, ns = f(first, xp, xs, ps, sts)
        for r, v in zip(out, ov):
            r[...] = v.astype(r.dtype)
        for s, v in zip(st, ns):
            s[...] = v
        if n_ex:
            @pl.when((pl.program_id(0) == ncol - 1) & (i == nb - 1))
            def _():
                _exchange_wait(exg, exr, *sems)

    in_specs = [_bspec(r, R, lambda i: i) for r in rows]
    in_specs += [_pspec(rows[k], R, lambda i: i) for k in prev_ids]
    in_specs += [pl.BlockSpec(p['bs'], functools.partial(lambda jc, i, idx: idx(jc), idx=p['idx'])) for p in params]
    out_specs = [_bspec(o_, R, lambda i: i) for o_ in outs]
    out_shape = [jax.ShapeDtypeStruct(o_['ls'] + (nrows, o_['n']), o_['dt']) for o_ in outs]
    if save_states:
        for s in state_shapes:
            out_specs.append(pl.BlockSpec((1,) + tuple(s), lambda jc, i, nd=len(s): (i,) + (0,) * nd))
            out_shape.append(jax.ShapeDtypeStruct((nb,) + tuple(s), F32))
    args = [r['a'] for r in rows] + [rows[k]['a'] for k in prev_ids] + [p['a'] for p in params]
    aliases = {}
    for n_, arr in sorted((out_alias or {}).items()):
        assert arr.shape == out_shape[n_].shape and arr.dtype == out_shape[n_].dtype
        aliases[len(args)] = n_
        args.append(arr)
        in_specs.append(pl.BlockSpec(memory_space=pl.ANY))
    n_sav = len(out_shape) - n_out
    in_specs += [pl.BlockSpec(memory_space=pl.ANY)] * n_ex
    out_specs += [pl.BlockSpec(memory_space=pl.ANY)] * n_ex
    out_shape += [jax.ShapeDtypeStruct((N_DEV,) + tuple(g.shape), g.dtype) for g in ex]
    cp = pltpu.CompilerParams(dimension_semantics=("arbitrary", "arbitrary"), vmem_limit_bytes=VMEM_LIMIT,
                              has_side_effects=bool(n_ex))
    res = pl.pallas_call(
        body, name=name, grid=(ncol, nb), in_specs=in_specs, out_specs=out_specs, out_shape=out_shape,
        scratch_shapes=[pltpu.VMEM(tuple(s), F32) for s in state_shapes] + (_exchange_sems(n_ex) if n_ex else []),
        input_output_aliases=aliases, compiler_params=cp)(*args, *ex)
    if n_ex:
        return list(res[:n_out]), list(res[n_out:n_out + n_sav]), list(res[n_out + n_sav:])
    return list(res[:n_out]), list(res[n_out:])


def _seq_bwd(name, f, R, rows, params, state_shapes, douts, saved, *, ncol=1, period=None, dx_dt=F32, dx_add=None,
             dx_bf=False, dx_alias=None, exchange=None):
    ex = list(exchange or [])
    n_ex = len(ex)
    nrows = rows[0]['a'].shape[-2]
    nb = nrows // R
    period = nb if period is None else period
    prev_ids = [k for k, r in enumerate(rows) if r['prev']]
    drow_ids = [k for k, r in enumerate(rows) if r['diff']]
    dpar_ids = [k for k, p in enumerate(params) if p['diff']]
    for k in prev_ids:
        assert rows[k]['diff']
    dx_add, dx_alias = dict(dx_add or {}), dict(dx_alias or {})
    add_ids, alias_ids = sorted(dx_add), sorted(dx_alias)
    n_rows, n_prev, n_par, n_do, n_st = len(rows), len(prev_ids), len(params), len(douts), len(state_shapes)
    n_dx, n_dp, n_add, n_al = len(drow_ids), len(dpar_ids), len(add_ids), len(alias_ids)

    def body(*refs):
        o = 0
        cur = refs[o:o + n_rows]; o += n_rows
        prv = refs[o:o + n_prev]; o += n_prev
        par = refs[o:o + n_par]; o += n_par
        sav = refs[o:o + n_st]; o += n_st
        dou = refs[o:o + n_do]; o += n_do
        adr = refs[o:o + n_add]; o += n_add
        o += n_al
        exg = refs[o:o + n_ex]; o += n_ex
        dxr = refs[o:o + n_dx]; o += n_dx
        dpr = refs[o:o + n_dp]; o += n_dp
        dxb = refs[o:o + (n_dx if dx_bf else 0)]; o += len(dxb)
        exr = refs[o:o + n_ex]; o += n_ex
        dst = refs[o:o + n_st]; o += n_st
        car = refs[o:o + n_prev]; o += n_prev
        sems = refs[o:]
        j = pl.program_id(1)
        i = nb - 1 - j
        first = (i % period) == 0
        if n_ex:
            @pl.when((pl.program_id(0) == 0) & (j == 0))
            def _():
                _exchange_start(exg, exr, *sems)

        @pl.when(j == 0)
        def _():
            for s in tuple(dst) + tuple(car) + tuple(dpr):
                s[...] = jnp.zeros_like(s)

        xs = [r[...].astype(F32) for r in cur]
        xp = [r[...].astype(F32) for r in prv]
        ps = [r[...] for r in par]
        sts = [s[0] for s in sav]

        def g(dxs, dxp, dps, dsts):
            xs_, ps_ = list(xs), list(ps)
            for k, v in zip(drow_ids, dxs):
                xs_[k] = v
            for k, v in zip(dpar_ids, dps):
                ps_[k] = v
            ov, ns = f(first, list(dxp), xs_, ps_, list(dsts))
            return tuple(ov), tuple(ns)

        _, vjp = jax.vjp(g, tuple(xs[k] for k in drow_ids), tuple(xp), tuple(ps[k] for k in dpar_ids), tuple(sts))
        dxs, dxp, dps, dsts = vjp((tuple(r[...].astype(F32) for r in dou), tuple(s[...] for s in dst)))
        dxs = list(dxs)
        for n_, pos in enumerate(add_ids):
            dxs[pos] = dxs[pos] + adr[n_][...].astype(F32)
        tails = {}
        for n_, k in enumerate(prev_ids):
            pos = drow_ids.index(k)
            if rows[k]['prev'] == 'block':
                dxs[pos] = dxs[pos] + car[n_][...]
            else:
                tails[pos] = car[n_][...]
            car[n_][...] = dxp[n_]
        for pos, v in enumerate(dxs):
            outs_ = [dxr[pos]] + ([dxb[pos]] if dx_bf else [])
            if pos in tails:
                v = jnp.concatenate([v[..., :R - HALO, :], v[..., R - HALO:, :] + tails[pos]], axis=-2)
            for r in outs_:
                r[...] = v.astype(r.dtype)
        for r, v in zip(dpr, dps):
            r[...] += v
        for s, v in zip(dst, dsts):
            s[...] = v
        if n_ex:
            @pl.when((pl.program_id(0) == ncol - 1) & (j == nb - 1))
            def _():
                _exchange_wait(exg, exr, *sems)

    def rev(j):
        return nb - 1 - j

    def dspec(k):
        return _bspec(rows[k], R, rev)

    in_specs = [_bspec(r, R, rev) for r in rows]
    in_specs += [_pspec(rows[k], R, rev) for k in prev_ids]
    in_specs += [pl.BlockSpec(p['bs'], functools.partial(lambda jc, j, idx: idx(jc), idx=p['idx'])) for p in params]
    in_specs += [pl.BlockSpec((1,) + tuple(s), lambda jc, j, nd=len(s): (nb - 1 - j,) + (0,) * nd) for s in state_shapes]
    in_specs += [_bspec(d, R, rev) for d in douts]
    in_specs += [dspec(drow_ids[pos]) for pos in add_ids]
    in_specs += [pl.BlockSpec(memory_space=pl.ANY) for _ in alias_ids + ex]
    out_specs = [dspec(k) for k in drow_ids]
    out_shape = [jax.ShapeDtypeStruct(tuple(rows[k]['a'].shape[:-1]) + (rows[k]['dn'],), dx_dt) for k in drow_ids]
    for k in dpar_ids:
        p = params[k]
        out_specs.append(pl.BlockSpec(p['bs'], functools.partial(lambda jc, j, idx: idx(jc), idx=p['idx'])))
        out_shape.append(jax.ShapeDtypeStruct(p['a'].shape, F32))
    if dx_bf:
        out_specs += [dspec(k) for k in drow_ids]
        out_shape += [jax.ShapeDtypeStruct(tuple(rows[k]['a'].shape[:-1]) + (rows[k]['dn'],), BF) for k in drow_ids]
    out_specs += [pl.BlockSpec(memory_space=pl.ANY) for _ in ex]
    out_shape += [jax.ShapeDtypeStruct(g.shape, g.dtype) for g in ex]
    scratch = [pltpu.VMEM(tuple(s), F32) for s in state_shapes]
    scratch += [pltpu.VMEM(tuple(d for d in rows[k]['lb'] if d is not None) + (_prev_rows(rows[k], R), rows[k]['w']), F32)
                for k in prev_ids]
    if n_ex:
        scratch += _exchange_sems(n_ex)
    args = ([r['a'] for r in rows] + [rows[k]['a'] for k in prev_ids] + [p['a'] for p in params] + list(saved)
            + [d['a'] for d in douts] + [dx_add[pos] for pos in add_ids] + [dx_alias[pos] for pos in alias_ids])
    n_in = len(args)
    aliases = {n_in - n_al + n_: pos for n_, pos in enumerate(alias_ids)}
    for pos in alias_ids:
        assert dx_alias[pos].shape == out_shape[pos].shape and dx_alias[pos].dtype == out_shape[pos].dtype
    cp = pltpu.CompilerParams(dimension_semantics=("arbitrary", "arbitrary"), vmem_limit_bytes=VMEM_LIMIT,
                              has_side_effects=bool(n_ex))
    res = pl.pallas_call(
        body, name=name, grid=(ncol, nb), in_specs=in_specs, out_specs=out_specs, out_shape=out_shape,
        scratch_shapes=scratch, input_output_aliases=aliases, compiler_params=cp)(*args, *ex)
    lists = [list(res[:n_dx]), list(res[n_dx:n_dx + n_dp])]
    o = n_dx + n_dp
    if dx_bf:
        lists.append(list(res[o:o + n_dx]))
        o += n_dx
    if n_ex:
        lists.append(list(res[o:o + n_ex]))
    return tuple(lists)


def _tile(n, cands):
    for c in cands:
        if n % c == 0:
            return c
    return n


def _mm_call(name, grid, a, a_spec, b, b_spec, contract, out_shape, out_spec, acc_shape, add=None, add_spec=None,
             exchange=None, gather=None):
    nk = grid[2]
    ca, cb = contract
    has_add = add is not None
    ex = list(exchange or []) + list(gather or [])
    ex_shapes = [g.shape for g in exchange or []] + [(N_DEV,) + tuple(g.shape) for g in gather or []]
    n_ex = len(ex)
    n_in = 2 + has_add

    def body(*refs):
        a_ref, b_ref = refs[0], refs[1]
        add_ref = refs[2] if has_add else None
        o_ref = refs[n_in + n_ex]
        scr = refs[n_in + 2 * n_ex + 1:]
        step = [pl.program_id(d) for d in range(3)]
        if n_ex:
            g_refs, r_refs, sems = refs[n_in:n_in + n_ex], refs[n_in + n_ex + 1:n_in + 2 * n_ex + 1], scr[-3:]

            @pl.when((step[0] == 0) & (step[1] == 0) & (step[2] == 0))
            def _():
                _exchange_start(g_refs, r_refs, *sems)

        part = _dot(a_ref[...], b_ref[...], ca, cb)

        def finish(r):
            if has_add:
                r = r + add_ref[...].astype(F32)
            o_ref[...] = r.astype(o_ref.dtype)

        if nk == 1:
            finish(part)
        else:
            acc = scr[0]

            @pl.when(step[2] == 0)
            def _():
                acc[...] = part

            @pl.when(step[2] > 0)
            def _():
                acc[...] += part

            @pl.when(step[2] == nk - 1)
            def _():
                finish(acc[...])

        if n_ex:
            @pl.when((step[0] == grid[0] - 1) & (step[1] == grid[1] - 1) & (step[2] == grid[2] - 1))
            def _():
                _exchange_wait(g_refs, r_refs, *sems)

    in_specs, args = [a_spec, b_spec], [a, b]
    if has_add:
        in_specs.append(add_spec)
        args.append(add)
    any_spec = pl.BlockSpec(memory_space=pl.ANY)
    scratch = [] if nk == 1 else [pltpu.VMEM(acc_shape, F32)]
    if n_ex:
        scratch += _exchange_sems(n_ex)
    res = pl.pallas_call(
        body, name=name, grid=grid, in_specs=in_specs + [any_spec] * n_ex, out_specs=[out_spec] + [any_spec] * n_ex,
        out_shape=[out_shape] + [jax.ShapeDtypeStruct(s, g.dtype) for s, g in zip(ex_shapes, ex)], scratch_shapes=scratch,
        compiler_params=pltpu.CompilerParams(
            dimension_semantics=("arbitrary",) * 3 if n_ex else ("parallel", "parallel", "arbitrary"),
            vmem_limit_bytes=VMEM_LIMIT, has_side_effects=bool(n_ex)))(*args, *ex)
    return (res[0], list(res[1:])) if n_ex else res[0]


def _matmul(a, b, mode="nn", add=None, out_dtype=F32, name="matmul", **pushed):
    if mode == "nn":
        (M, K), N = a.shape, b.shape[1]
    elif mode == "nt":
        (M, K), N = a.shape, b.shape[0]
    else:
        (K, M), N = a.shape, b.shape[1]
    if mode == "tn" and 1024 < N <= 5120:
        tm, tn = _tile(M, (512, 256, 128, 64, 32, 16, 8)), N
        tk = _tile(K, (2048 if tn <= 3072 else 1024, 1024, 512, 256, 128))
    else:
        tk = K if K <= 5120 else _tile(K, (2048, 1024, 512, 256, 128))
        tm = _tile(M, ((2048,) if tk <= 1024 and mode != "tn" else ()) + (1024, 512, 256, 128, 64, 32, 16, 8))
        tn = _tile(N, (512, 256, 128))
    if mode == "tn":
        a_spec = pl.BlockSpec((tk, tm), lambda i, j, k: (k, i))
    else:
        a_spec = pl.BlockSpec((tm, tk), lambda i, j, k: (i, k))
    if mode == "nt":
        b_spec = pl.BlockSpec((tn, tk), lambda i, j, k: (j, k))
    else:
        b_spec = pl.BlockSpec((tk, tn), lambda i, j, k: (k, j))
    blk = pl.BlockSpec((tm, tn), lambda i, j, k: (i, j))
    return _mm_call(name, (M // tm, N // tn, K // tk), a, a_spec, b, b_spec,
                    {"nn": (1, 0), "nt": (1, 1), "tn": (0, 0)}[mode], jax.ShapeDtypeStruct((M, N), out_dtype), blk,
                    (tm, tn), add, blk, **pushed)


FF_SH = 2 * D_FF // N_DEV


def _ffn_up(h2, wup, **pushed):
    S, D = h2.shape
    tm = _tile(S, (2048, 1024, 512, 256, 128))
    return _mm_call("matmul_up", (S // tm, N_DEV, 1), h2, pl.BlockSpec((tm, D), lambda m, j, k: (m, 0)),
                    wup, pl.BlockSpec((None, D, FF_SH), lambda m, j, k: (j, 0, 0)), (1, 0),
                    jax.ShapeDtypeStruct((2, N_DEV // 2, S, FF_SH), F32),
                    pl.BlockSpec((None, None, tm, FF_SH), lambda m, j, k: (j // 4, j % 4, m, 0)), (tm, FF_SH), **pushed)


def _ffn_down(act, wd, x1, **pushed):
    _, S, _ = act.shape
    D = wd.shape[1]
    tm, tn = _tile(S, (1024, 512, 256, 128)), _tile(D, (1024, 512, 256, 128))
    blk = pl.BlockSpec((tm, tn), lambda m, n, k: (m, n))
    return _mm_call("matmul_down", (S // tm, D // tn, N_DEV // 2), act,
                    pl.BlockSpec((None, tm, FF_SH), lambda m, n, k: (k, m, 0)), wd,
                    pl.BlockSpec((FF_SH, tn), lambda m, n, k: (k, n)), (1, 0), jax.ShapeDtypeStruct((S, D), F32), blk,
                    (tm, tn), x1, blk, **pushed)


def _ffn_dact(dxb, wd):
    S, D = dxb.shape
    tm = _tile(S, (2048, 1024, 512, 256, 128))
    return _mm_call("matmul_dact", (S // tm, N_DEV // 2, 1), dxb, pl.BlockSpec((tm, D), lambda m, j, k: (m, 0)), wd,
                    pl.BlockSpec((FF_SH, D), lambda m, j, k: (j, 0)), (1, 1),
                    jax.ShapeDtypeStruct((N_DEV // 2, S, FF_SH), BF),
                    pl.BlockSpec((None, tm, FF_SH), lambda m, j, k: (j, m, 0)), (tm, FF_SH))


def _ffn_dw_down(act, dxb):
    _, S, _ = act.shape
    D = dxb.shape[1]
    tk, tn = _tile(S, (2048, 1024, 512, 256, 128)), _tile(D, (512, 256, 128))
    return _mm_call("matmul_dw_down", (N_DEV // 2, D // tn, S // tk), act,
                    pl.BlockSpec((None, tk, FF_SH), lambda j, n, k: (j, k, 0)), dxb,
                    pl.BlockSpec((tk, tn), lambda j, n, k: (k, n)), (0, 0), jax.ShapeDtypeStruct((D_FF, D), BF),
                    pl.BlockSpec((FF_SH, tn), lambda j, n, k: (j, n)), (FF_SH, tn))


def _ffn_dw_up(h2, du, exchange=None):
    S, D = h2.shape
    tk = _tile(S, (2048, 1024, 512, 256, 128))
    return _mm_call("matmul_dw_up", (N_DEV, 1, S // tk), h2, pl.BlockSpec((tk, D), lambda j, n, k: (k, 0)), du,
                    pl.BlockSpec((None, None, tk, FF_SH), lambda j, n, k: (j // 4, j % 4, k, 0)), (0, 0),
                    jax.ShapeDtypeStruct((N_DEV, D, FF_SH), BF),
                    pl.BlockSpec((None, D, FF_SH), lambda j, n, k: (j, 0, 0)), (D, FF_SH), exchange=exchange)


def _ffn_dh2(du, wup, exchange=None):
    S = du.shape[2]
    D = wup.shape[1]
    tm = _tile(S, (1024, 512, 256, 128))
    return _mm_call("matmul_dh2", (S // tm, 1, N_DEV), du,
                    pl.BlockSpec((None, None, tm, FF_SH), lambda m, n, k: (k // 4, k % 4, m, 0)), wup,
                    pl.BlockSpec((None, D, FF_SH), lambda m, n, k: (k, 0, 0)), (1, 1),
                    jax.ShapeDtypeStruct((S, D), F32), pl.BlockSpec((tm, D), lambda m, n, k: (m, 0)), (tm, D),
                    exchange=exchange)


def f_rmsnorm(first, xp, xs, ps, sts):
    return (_rms(xs[0], ps[0]),), ()


def _same_block(shape, rows_per, cols_per):
    return (lax.broadcasted_iota(jnp.int32, shape, 0) // rows_per) == (lax.broadcasted_iota(jnp.int32, shape, 1) // cols_per)


@jax.custom_vjp
def _head_mean(x):
    n = x.shape[1]
    return _dot_hi(x, jnp.where(_same_block((n, n), XA_HD, XA_HD), 1.0 / XA_HD, 0.0), 1, 0)


_head_mean.defvjp(lambda x: (_head_mean(x), None), lambda r, g: (_head_mean(g),))


def f_xattn(first, xp, xs, ps, sts):
    (xq,), (kv, qn, kn) = xs, ps
    k, v = _split(kv, [D_XA, D_XA])
    m_rows = kv.shape[0]
    q = xq * lax.rsqrt(_head_mean(xq * xq) + EPS) * jnp.concatenate([qn] * XA_HEADS, axis=1)
    k = k * lax.rsqrt(_head_mean(k * k) + EPS) * jnp.concatenate([kn] * XA_HEADS, axis=1)
    kt = k.T
    kbd = jnp.where(_same_block((D_XA, XA_HEADS * m_rows), XA_HD, m_rows), jnp.concatenate([kt] * XA_HEADS, axis=1), 0.0)
    s = mm_nn(q, kbd) * (XA_HD ** -0.5)
    ps_ = []
    for sh in _split(s, [m_rows] * XA_HEADS):
        mx = lax.stop_gradient(jnp.max(sh, axis=-1, keepdims=True))
        p = jnp.exp(sh - mx)
        ps_.append(p / jnp.sum(p, axis=-1, keepdims=True))
    vbd = jnp.where(_same_block((XA_HEADS * m_rows, D_XA), m_rows, XA_HD), jnp.concatenate([v] * XA_HEADS, axis=0), 0.0)
    return (mm_nn(jnp.concatenate(ps_, axis=1), vbd),), ()


def _conv(xp, x, w, b, first, taps):
    xp = jnp.where(first, 0.0, xp)
    y = b + w[taps - 1:taps] * x
    for d in range(1, taps):
        y = y + w[taps - 1 - d:taps - d] * _shift(xp, x, d)
    return y


def _unstack2(x):
    @jax.custom_vjp
    def us(x):
        return x[0], x[1]

    us.defvjp(lambda x: (us(x), None), lambda r, g: (jnp.stack(g),))
    return us(x)


def f_ffn_act(first, xp, xs, ps, sts):
    (up,), (u,), (wg, wv, bg, bv) = xp, xs, ps
    (ugp, uvp), (ug, uv) = _unstack2(up), _unstack2(u)
    gate = _conv(ugp, ug, wg, bg, first, FFN_CONV)
    val = _conv(uvp, uv, wv, bv, first, FFN_CONV)
    return (jax.nn.silu(gate) * val,), ()


def _gla_chunk(q, k, v, la, sts, dk, dv):
    c, nh = q.shape[0], len(sts)
    b = cumsum_rows(la)
    b_last = _row(b, c - 1)
    b_ref = _row(b, c // 2 - 1)
    qe, ke = _split(q * jnp.exp(b - b_ref), [dk] * nh), _split(k * jnp.exp(b_ref - b), [dk] * nh)
    qi, kl = _split(q * jnp.exp(b), [dk] * nh), _split(k * jnp.exp(b_last - b), [dk] * nh)
    dec, vs = _split(jnp.exp(b_last), [dk] * nh), _split(v, [dv] * nh)
    tril = _tril(c)
    outs, new = [], []
    for h in range(nh):
        att = jnp.where(tril, mm_nt(qe[h], ke[h]), 0.0)
        outs.append(mm_nn(att, vs[h]) + mm_nt(qi[h], sts[h]))
        new.append(sts[h] * dec[h] + mm_tn(vs[h], kl[h]))
    return outs, tuple(new)


def _split_rows(x, n):
    c = x.shape[0] // n

    @jax.custom_vjp
    def sp(x):
        return tuple(x[i * c:(i + 1) * c] for i in range(n))

    sp.defvjp(lambda x: (sp(x), None), lambda r, g: (jnp.concatenate(list(g), axis=0),))
    return sp(x)


def _gla_scan(q, k, v, la, sts, dk, dv):
    n = q.shape[0] // CHUNK
    per_chunk = []
    for qc, kc, vc, lc in zip(*(_split_rows(t, n) for t in (q, k, v, la))):
        o, sts = _gla_chunk(qc, kc, vc, lc, sts, dk, dv)
        per_chunk.append(o)
    return [jnp.concatenate([o[h] for o in per_chunk], axis=0) for h in range(len(sts))], sts


def _a_cols(ntot):
    used = D_XA + 2 * GLA_HEADS * GLA_DK + D_MIX + GLA_RANK + D_MIX
    return [D_XA, GLA_HEADS * GLA_DK, GLA_HEADS * GLA_DK, D_MIX, GLA_RANK, D_MIX] + ([ntot - used] if ntot > used else [])


def f_gla(first, xp, xs, ps, sts):
    (p,), (wg2, bg, on) = xs, ps
    parts = _split(p, _a_cols(p.shape[1]))
    q, k, v, glr, og = parts[1:6]
    la = jax.nn.log_sigmoid(mm_nn(glr, wg2) + bg) / GLA_GATE_NORM
    outs, new = _gla_scan(q * (GLA_DK ** -0.5), k, v, la, tuple(sts), GLA_DK, GLA_DV)
    return (jnp.concatenate([_rms(o, on) for o in outs], axis=1) * jax.nn.silu(og),), new


def f_hgrn(first, xp, xs, ps, sts):
    (p,), (lbp, on) = xs, ps
    _, q, fgate, iv, og = _split(p, [D_XA, D_MIX, D_MIX, D_MIX, D_MIX])
    e = jnp.exp(lbp - jnp.max(lbp, axis=0, keepdims=True))
    row = lax.broadcasted_iota(jnp.int32, e.shape, 0)
    lb = jnp.sum(jnp.where(row >= 1, e, 0.0), axis=0, keepdims=True) / jnp.sum(e, axis=0, keepdims=True)
    fg = lb + (1.0 - lb) * jax.nn.sigmoid(fgate)
    outs, new = _gla_scan(jax.nn.silu(q), 1.0 - fg, iv, jnp.log(fg), tuple(sts), HGRN_DK, HGRN_DV)
    return (jnp.concatenate([_rms(o, on) for o in outs], axis=1) * jax.nn.sigmoid(og),), new


def _c_cols(ntot):
    gn = SSM_GROUPS * SSM_STATE
    used = D_XA + D_MIX + D_MIX + 2 * gn + SSM_HEADS
    return [D_XA, D_MIX, D_MIX + 2 * gn, SSM_HEADS] + ([ntot - used] if ntot > used else [])


def f_ssd(first, xp, xs, ps, sts):
    (pp,), (p,), (cw, cb, dtb, alog, dsk, ng) = xp, xs, ps
    gn = SSM_GROUPS * SSM_STATE
    _, z, xbc, dtr = _split(p, _c_cols(p.shape[1]))[:4]
    xbc_p = _split(pp, _c_cols(p.shape[1]))[2]
    xbc = jax.nn.silu(_conv(xbc_p, xbc, cw, cb, first, SSM_CONV))
    xs_, bm, cm = _split(xbc, [D_MIX, gn, gn])
    dt = jax.nn.softplus(dtr + dtb)
    n = p.shape[0] // CHUNK
    ys, sts = [], tuple(sts)
    for xc, bc, cc, dc in zip(*(_split_rows(t, n) for t in (xs_, bm, cm, dt))):
        y, sts = _ssd_chunk(xc, bc, cc, dc, alog, dsk, sts)
        ys.append(y)
    y = jnp.concatenate(ys, axis=0) * jax.nn.silu(z)
    gw = D_MIX // SSM_GROUPS
    yg = _split(y, [gw] * SSM_GROUPS)
    ngs = _split(ng, [gw] * SSM_GROUPS)
    y = jnp.concatenate([_rms(yg[g], ngs[g]) for g in range(SSM_GROUPS)], axis=1)
    return (y,), sts


def _ssd_chunk(xs_, bm, cm, dt, alog, dsk, sts):
    c = xs_.shape[0]
    hg = SSM_HEADS // SSM_GROUPS
    a = dt * (-jnp.exp(alog))
    acs = cumsum_rows(a)
    acs_t = cumsum_rows_t(a)
    acs_last = _row(acs, c - 1)
    dt_h = _split(dt, [1] * SSM_HEADS)
    acs_h = _split(acs, [1] * SSM_HEADS)
    al_h = _split(acs_last, [1] * SSM_HEADS)
    d_h = _split(dsk, [1] * SSM_HEADS)
    x2s = _split(xs_, [2 * SSM_HD] * (SSM_HEADS // 2))
    bms = _split(bm, [SSM_STATE] * SSM_GROUPS)
    cms = _split(cm, [SSM_STATE] * SSM_GROUPS)
    tril = _tril(c)
    cbs = [mm_nt(cms[g], bms[g]) for g in range(SSM_GROUPS)]
    ys, new = [], []
    for j in range(SSM_HEADS // 2):
        g = (2 * j) // hg
        h0, h1 = 2 * j, 2 * j + 1
        xdt = x2s[j] * _lane_pair(dt_h[h0], dt_h[h1])
        acs2 = _lane_pair(acs_h[h0], acs_h[h1])
        al2 = _lane_pair(al_h[h0], al_h[h1])
        yd = []
        for h in (h0, h1):
            seg = acs_h[h] - _row(acs_t, h)
            lm = jnp.exp(jnp.where(tril, seg, NEG))
            yd.append(mm_nn(cbs[g] * lm, xdt))
        lane = lax.broadcasted_iota(jnp.int32, xdt.shape, 1)
        y_diag = jnp.where(lane < SSM_HD, yd[0], yd[1])
        y_off = mm_nn(cms[g], sts[j]) * jnp.exp(acs2)
        x_end = xdt * jnp.exp(al2 - acs2)
        new.append(sts[j] * jnp.exp(al2) + mm_tn(bms[g], x_end))
        ys.append(y_diag + y_off + _lane_pair(d_h[h0], d_h[h1]) * x2s[j])
    return jnp.concatenate(ys, axis=1), tuple(new)


def f_dil_prep(first, xp, xs, ps, sts):
    (p, pos), (qn, kn, invf, sign) = xs, ps
    nh = len(DIL_GROUPS) * DIL_HEADS
    _, q, k, v = _split(p, [D_XA] + [nh * DIL_HD] * 3)
    ang = pos * invf
    cos, sin = jnp.cos(ang), jnp.sin(ang) * sign

    def rope(t, g):
        hs = _split(t, [DIL_HD] * nh)
        out = []
        for h in hs:
            n = _rms(h, g)
            out.append(n * cos + _roll_half(n) * sin)
        return [jnp.concatenate(out[i:i + DIL_HEADS], axis=1) for i in range(0, nh, DIL_HEADS)]

    return tuple(rope(q, qn) + rope(k, kn) + list(_split(v, [D_DIL] * len(DIL_GROUPS)))), ()


def f_dil_attn(first, xp, xs, ps, sts):
    (kp, vp), (q, k, v) = xp, xs
    Q = DIL_BLOCK
    n = q.shape[0] // Q
    qb, kb, vb = (_split_rows(t, n) for t in (q, k, v))
    kprev, vprev = _split_rows(kp, n)[-1], _split_rows(vp, n)[-1]
    i = lax.broadcasted_iota(jnp.int32, (Q, 2 * Q), 0)
    j = lax.broadcasted_iota(jnp.int32, (Q, 2 * Q), 1)
    dist = Q + i - j
    band = (dist >= 0) & (dist <= Q)
    o_rows, lse_rows = [], []
    for b in range(n):
        mask = band & (jnp.logical_not(first) | (j >= Q)) if b == 0 else band
        qs, ks, vs = (_split(t, [DIL_HD] * DIL_HEADS) for t in (qb[b], kb[b], vb[b]))
        kps, vps = (_split(t, [DIL_HD] * DIL_HEADS) for t in (kprev, vprev))
        outs, lses = [], []
        for h in range(DIL_HEADS):
            k2 = jnp.concatenate([kps[h], ks[h]], axis=0)
            v2 = jnp.concatenate([vps[h], vs[h]], axis=0)
            s = jnp.where(mask, mm_nt(qs[h], k2) * (DIL_HD ** -0.5), NEG)
            m = lax.stop_gradient(jnp.max(s, axis=-1, keepdims=True))
            p = jnp.exp(s - m)
            l = jnp.sum(p, axis=-1, keepdims=True)
            outs.append(mm_nn(p / l, v2))
            lses.append(jnp.broadcast_to(m + jnp.log(l), (Q, DIL_HD)))
        o_rows.append(jnp.concatenate(outs, axis=1))
        lse_rows.append(jnp.concatenate(lses, axis=1))
        kprev, vprev = kb[b], vb[b]
    return (jnp.concatenate(o_rows, axis=0), jnp.concatenate(lse_rows, axis=0)), ()


def f_dil_merge(first, xp, xs, ps, sts):
    o0, o1, o2, l0, l1, l2 = xs
    m = jnp.maximum(jnp.maximum(l0, l1), l2)
    e0, e1, e2 = jnp.exp(l0 - m), jnp.exp(l1 - m), jnp.exp(l2 - m)
    den = e0 + e1 + e2
    return ((e0 * o0 + e1 * o1 + e2 * o2) / den,), ()


def _loss_head(y, target):
    S, D = y.shape
    R = _tile(S, (512, 256, 128, 64, 32, 16, 8))

    def body(y_ref, t_ref, dy_ref, dyb_ref, l_ref):
        e = y_ref[...] - t_ref[...]
        dy_ref[...] = e * (1.0 / D)
        dyb_ref[...] = (e * (1.0 / D)).astype(BF)

        @pl.when(pl.program_id(0) == 0)
        def _():
            l_ref[...] = jnp.zeros_like(l_ref)

        l_ref[...] += jnp.broadcast_to(0.5 * jnp.sum(jnp.mean(e * e, axis=-1, keepdims=True), axis=0, keepdims=True),
                                       l_ref.shape)

    blk = pl.BlockSpec((R, D), lambda i: (i, 0))
    dy, dyb, l = pl.pallas_call(
        body, name="loss_head", grid=(S // R,), in_specs=[blk, blk],
        out_specs=[blk, blk, pl.BlockSpec((8, 128), lambda i: (0, 0))],
        out_shape=[jax.ShapeDtypeStruct((S, D), F32), jax.ShapeDtypeStruct((S, D), BF),
                   jax.ShapeDtypeStruct((8, 128), F32)],
        compiler_params=pltpu.CompilerParams(dimension_semantics=("arbitrary",)))(y, target)
    return dy, dyb, l[0, 0]


def _adamw(parts, w, m, v, name):
    _, n, width = parts.shape
    tr = _tile(n, [t for t in (512, 256, 128, 64, 32, 16, 8) if t * width <= ADAM_BLOCK])

    def body(p_ref, w_ref, m_ref, v_ref, g_ref, d_ref, nm_ref, nv_ref):
        g = p_ref[0].astype(F32)
        for s in range(1, N_DEV):
            g = g + p_ref[s].astype(F32)
        nm = ADAM_B1 * m_ref[...] + (1.0 - ADAM_B1) * g
        nv = ADAM_B2 * v_ref[...] + (1.0 - ADAM_B2) * (g * g)
        m_hat = nm / (1.0 - ADAM_B1 ** ADAM_STEP)
        v_hat = nv / (1.0 - ADAM_B2 ** ADAM_STEP)
        g_ref[...] = g
        d_ref[...] = -ADAM_LR * (m_hat / (jnp.sqrt(v_hat) + ADAM_EPS) + ADAM_WD * w_ref[...])
        nm_ref[...] = nm
        nv_ref[...] = nv

    blk = pl.BlockSpec((tr, width), lambda i: (i, 0))
    return pl.pallas_call(
        body, name=name, grid=(n // tr,),
        in_specs=[pl.BlockSpec((N_DEV, tr, width), lambda i: (0, i, 0)), blk, blk, blk],
        out_specs=[blk] * 4, out_shape=[jax.ShapeDtypeStruct((n, width), F32)] * 4,
        compiler_params=pltpu.CompilerParams(dimension_semantics=("arbitrary",), vmem_limit_bytes=VMEM_LIMIT))(
            parts, w, m, v)


def _peer(k):
    x, y, c = lax.axis_index("x"), lax.axis_index("y"), lax.axis_index("c")
    px = 1 - x if k & 4 else x
    py = 1 - y if k & 2 else y
    pc = 1 - c if k & 1 else c
    return (px, py, pc), 4 * px + 2 * py + pc


def _my_id():
    return 4 * lax.axis_index("x") + 2 * lax.axis_index("y") + lax.axis_index("c")


def _all_gather(x, name):
    def body(x_ref, out_ref, send, recv, loc):
        me = _my_id()
        mine = pltpu.make_async_copy(x_ref, out_ref.at[me], loc)
        mine.start()
        cps = []
        for k in range(1, N_DEV):
            peer, _ = _peer(k)
            cp = pltpu.make_async_remote_copy(src_ref=x_ref, dst_ref=out_ref.at[me], send_sem=send.at[k - 1],
                                              recv_sem=recv.at[k - 1], device_id=peer,
                                              device_id_type=pl.DeviceIdType.MESH)
            cp.start()
            cps.append(cp)
        for k in range(1, N_DEV):
            peer, pid = _peer(k)
            pltpu.make_async_remote_copy(src_ref=x_ref, dst_ref=out_ref.at[pid], send_sem=send.at[k - 1],
                                         recv_sem=recv.at[k - 1], device_id=peer,
                                         device_id_type=pl.DeviceIdType.MESH).wait_recv()
        for cp in cps:
            cp.wait_send()
        mine.wait()

    return pl.pallas_call(
        body, name=name, out_shape=jax.ShapeDtypeStruct((N_DEV,) + x.shape, x.dtype),
        in_specs=[pl.BlockSpec(memory_space=pl.ANY)], out_specs=pl.BlockSpec(memory_space=pl.ANY),
        scratch_shapes=[pltpu.SemaphoreType.DMA((N_DEV - 1,)), pltpu.SemaphoreType.DMA((N_DEV - 1,)),
                        pltpu.SemaphoreType.DMA],
        compiler_params=pltpu.CompilerParams(has_side_effects=True))(x)


def _exchange_sems(n):
    return [pltpu.SemaphoreType.DMA((n * (N_DEV - 1),)), pltpu.SemaphoreType.DMA((n * (N_DEV - 1),)),
            pltpu.SemaphoreType.DMA((n,))]


def _exchange_copies(g_refs, out_refs, send, recv, loc, with_arrivals):
    me = _my_id()

    def mine(g, o, d):
        return g.at[d] if len(g.shape) == len(o.shape) else g

    local = [pltpu.make_async_copy(mine(g, o, me), o.at[me], loc.at[w]) for w, (g, o) in enumerate(zip(g_refs, out_refs))]
    pushes, arrivals = [], []
    for k in range(1, N_DEV):
        peer, pid = _peer(k)
        for w, (g, o) in enumerate(zip(g_refs, out_refs)):
            s = w * (N_DEV - 1) + k - 1
            ends = [(mine(g, o, pid), o.at[me], pushes)] + ([(mine(g, o, me), o.at[pid], arrivals)] if with_arrivals else [])
            for src, dst, into in ends:
                into.append(pltpu.make_async_remote_copy(src_ref=src, dst_ref=dst, send_sem=send.at[s],
                                                         recv_sem=recv.at[s], device_id=peer,
                                                         device_id_type=pl.DeviceIdType.MESH))
    return local, pushes, arrivals


def _exchange_start(g_refs, out_refs, send, recv, loc):
    local, pushes, _ = _exchange_copies(g_refs, out_refs, send, recv, loc, False)
    for cp in local + pushes:
        cp.start()


def _exchange_wait(g_refs, out_refs, send, recv, loc):
    local, pushes, arrivals = _exchange_copies(g_refs, out_refs, send, recv, loc, True)
    for cp in arrivals:
        cp.wait_recv()
    for cp in pushes:
        cp.wait_send()
    for cp in local:
        cp.wait()


def _exchange_many(gs, name):
    n = len(gs)

    def body(*refs):
        g_refs, out_refs, sems = refs[:n], refs[n:2 * n], refs[2 * n:]
        _exchange_start(g_refs, out_refs, *sems)
        _exchange_wait(g_refs, out_refs, *sems)

    return pl.pallas_call(
        body, name=name, out_shape=[jax.ShapeDtypeStruct(g.shape, g.dtype) for g in gs],
        in_specs=[pl.BlockSpec(memory_space=pl.ANY)] * n, out_specs=[pl.BlockSpec(memory_space=pl.ANY)] * n,
        scratch_shapes=_exchange_sems(n), compiler_params=pltpu.CompilerParams(has_side_effects=True))(*gs)


def _gather_many(xs, name):
    n = len(xs)

    def body(*refs):
        x_refs, out_refs, (send, recv, loc) = refs[:n], refs[n:2 * n], refs[2 * n:]
        x, y, c = lax.axis_index("x"), lax.axis_index("y"), lax.axis_index("c")
        me, sibling = (x, y, c), (x, y, 1 - c)
        chips = [(1 - x, y), (x, 1 - y), (1 - x, 1 - y)]

        def slot(p):
            return 4 * p[0] + 2 * p[1] + p[2]

        def copy(w, k, block, to, src=None):
            dst = out_refs[w].at[slot(block)]
            return pltpu.make_async_remote_copy(src_ref=dst if src is None else src, dst_ref=dst,
                                                send_sem=send.at[w * (N_DEV - 1) + k], recv_sem=recv.at[w * (N_DEV - 1) + k],
                                                device_id=to, device_id_type=pl.DeviceIdType.MESH)

        mine = [pltpu.make_async_copy(x_refs[w], out_refs[w].at[slot(me)], loc.at[w]) for w in range(n)]
        for cp in mine:
            cp.start()
        first = []
        for j, chip in enumerate(chips):
            first += [copy(w, 1 + j, me, (*chip, c), src=x_refs[w]) for w in range(n)]
        first += [copy(w, 0, me, sibling, src=x_refs[w]) for w in range(n)]
        for cp in first:
            cp.start()
        passed = []
        for j, chip in enumerate(chips):
            for w in range(n):
                copy(w, 1 + j, (*chip, c), me).wait_recv()
                cp = copy(w, 4 + j, (*chip, c), sibling)
                cp.start()
                passed.append(cp)
        for w in range(n):
            copy(w, 0, sibling, me).wait_recv()
            for j, chip in enumerate(chips):
                copy(w, 4 + j, (*chip, 1 - c), me).wait_recv()
        for cp in first + passed:
            cp.wait_send()
        for cp in mine:
            cp.wait()

    return pl.pallas_call(
        body, name=name, out_shape=[jax.ShapeDtypeStruct((N_DEV,) + x.shape, x.dtype) for x in xs],
        in_specs=[pl.BlockSpec(memory_space=pl.ANY)] * n, out_specs=[pl.BlockSpec(memory_space=pl.ANY)] * n,
        scratch_shapes=[pltpu.SemaphoreType.DMA((n * (N_DEV - 1),)), pltpu.SemaphoreType.DMA((n * (N_DEV - 1),)),
                        pltpu.SemaphoreType.DMA((n,))],
        compiler_params=pltpu.CompilerParams(has_side_effects=True))(*xs)


def _cat_segs(G, ws, n_mix):
    segs = []
    for g in range(G):
        lo, hi = g * ws, (g + 1) * ws
        if lo < n_mix:
            segs.append((g, 0, min(hi, n_mix) - lo, D_XA + lo))
        if hi > n_mix:
            s = max(lo, n_mix)
            segs.append((g, s - lo, hi - s, s - n_mix))
    return segs


def _cat_cols(src, n_mix, ntot):
    G, R, ws = src.shape
    segs = _cat_segs(G, ws, n_mix)
    tr = _tile(R, (256, 128, 64, 32, 16, 8))

    def body(i_ref, o_ref):
        if ntot > G * ws:
            o_ref[...] = jnp.zeros_like(o_ref)
        for g, s, n, d in segs:
            o_ref[:, d:d + n] = i_ref[g][:, s:s + n]

    return pl.pallas_call(
        body, name="cat_cols", grid=(R // tr,), in_specs=[pl.BlockSpec((G, tr, ws), lambda i: (0, i, 0))],
        out_specs=pl.BlockSpec((tr, ntot), lambda i: (i, 0)), out_shape=jax.ShapeDtypeStruct((R, ntot), src.dtype),
        compiler_params=pltpu.CompilerParams(dimension_semantics=("arbitrary",)))(src)


def _uncat_cols(dw, G, ws, n_mix):
    R, ntot = dw.shape
    segs = _cat_segs(G, ws, n_mix)
    tr = _tile(R, (256, 128, 64, 32, 16, 8))

    def body(i_ref, o_ref):
        v = i_ref[...]
        for g, s, n, d in segs:
            o_ref[g, :, s:s + n] = v[:, d:d + n]

    return pl.pallas_call(
        body, name="uncat_cols", grid=(R // tr,), in_specs=[pl.BlockSpec((tr, ntot), lambda i: (i, 0))],
        out_specs=pl.BlockSpec((G, tr, ws), lambda i: (0, i, 0)), out_shape=jax.ShapeDtypeStruct((G, R, ws), dw.dtype),
        compiler_params=pltpu.CompilerParams(dimension_semantics=("arbitrary",)))(dw)


PACK_W = 1024


def _granule(n):
    return (256 if n >= 256 * PACK_W else 8) * PACK_W


def _pack(arrs, dtype):
    flat = jnp.concatenate([a.reshape(-1).astype(dtype) for a in arrs])
    n = flat.shape[0]
    pad = (-n) % _granule(n)
    if pad:
        flat = jnp.concatenate([flat, jnp.zeros((pad,), dtype)])
    return flat.reshape(-1, PACK_W)


def _unpack(packed, shapes):
    flat = packed.reshape(-1)
    out, o = [], 0
    for s in shapes:
        n = math.prod(s)
        out.append(flat[o:o + n].reshape(s))
        o += n
    return out


def _pack_lead(arrs, dtype):
    flat = jnp.concatenate([a.reshape(N_DEV, -1).astype(dtype) for a in arrs], axis=1)
    n = flat.shape[1]
    pad = (-n) % _granule(n)
    if pad:
        flat = jnp.concatenate([flat, jnp.zeros((N_DEV, pad), dtype)], axis=1)
    return flat.reshape(N_DEV, -1, PACK_W)


def _to_full(stacked, axis):
    t = jnp.moveaxis(stacked, 0, axis)
    s = list(t.shape)
    return t.reshape(s[:axis] + [s[axis] * s[axis + 1]] + s[axis + 2:])


def _to_chunks(full, axis):
    s = list(full.shape)
    t = full.reshape(s[:axis] + [N_DEV, s[axis] // N_DEV] + s[axis + 1:])
    return jnp.moveaxis(t, axis, 0)


def _rows_of(S):
    return _tile(S, (512, 256, 128, 64))


def _norm_fwd(x, g, dt=BF):
    (h,), _ = _seq_fwd("rmsnorm_fwd", f_rmsnorm, _rows_of(x.shape[0]), [_row_spec(x)], [_par_spec(g)], [],
                       [_out_spec(x.shape[1], dt=dt)])
    return h


def _norm_bwd(x, g, dh, res=None):
    if res is None:
        (dx,), (dg,) = _seq_bwd("rmsnorm_bwd", f_rmsnorm, _rows_of(x.shape[0]), [_row_spec(x)], [_par_spec(g)], [],
                                [_row_spec(dh)], [])
        return dx, None, dg
    (dx,), (dg,), (dxb,) = _seq_bwd("rmsnorm_res_bwd", f_rmsnorm, _rows_of(x.shape[0]), [_row_spec(x)], [_par_spec(g)], [],
                                    [_row_spec(dh)], [], dx_add={0: res}, dx_bf=True)
    return dx, dxb, dg


def _mixer_specs(kind, S, p, w):
    if kind == 0:
        return (f_gla, min(S, MIX_ROWS), [_row_spec(p)],
                [_par_spec(w['a_w_gate2']), _par_spec(w['a_b_gate'].reshape(1, -1)), _par_spec(w['a_o_norm'].reshape(1, -1))],
                [(GLA_DV, GLA_DK)] * GLA_HEADS, D_MIX)
    if kind == 2:
        return (f_ssd, min(S, MIX_ROWS), [_row_spec(p, prev='halo')],
                [_par_spec(w['c_conv_w']), _par_spec(w['c_conv_b'].reshape(1, -1)), _par_spec(w['c_dt_bias'].reshape(1, -1)),
                 _par_spec(w['c_a_log'].reshape(1, -1)), _par_spec(w['c_d'].reshape(1, -1)),
                 _par_spec(w['c_norm'].reshape(1, -1))],
                [(SSM_STATE, 2 * SSM_HD)] * (SSM_HEADS // 2), D_MIX)
    return (f_hgrn, min(S, MIX_ROWS), [_row_spec(p)],
            [_par_spec(w['d_lower_bounds']), _par_spec(w['d_o_norm'].reshape(1, -1))],
            [(HGRN_DV, HGRN_DK)] * HGRN_HEADS, D_MIX)


def _perm(t, r):
    if r == 1:
        return t
    S, n = t.shape
    return t.reshape(S // r, r, n).transpose(1, 0, 2).reshape(S, n)


def _unperm(t, r):
    if r == 1:
        return t
    S, n = t.shape
    return t.reshape(r, S // r, n).transpose(1, 0, 2).reshape(S, n)


def _rope_consts():
    half = DIL_HD // 2
    inv = ROPE_THETA ** (-jnp.arange(half, dtype=F32) / half)
    invf = jnp.concatenate([inv, inv]).reshape(1, DIL_HD)
    sign = jnp.concatenate([-jnp.ones((half,), F32), jnp.ones((half,), F32)]).reshape(1, DIL_HD)
    return invf, sign


def _dil_fwd(p, pos, w, ncat):
    S = p.shape[0]
    invf, sign = _rope_consts()
    prep_rows = [_row_spec(p), _row_spec(pos, diff=False)]
    prep_pars = [_par_spec(w['b_q_norm'].reshape(1, -1)), _par_spec(w['b_k_norm'].reshape(1, -1)),
                 _par_spec(invf, diff=False), _par_spec(sign, diff=False)]
    ng = len(DIL_GROUPS)
    qkv, _ = _seq_fwd("dil_prep_fwd", f_dil_prep, _tile(S, (256, 128)), prep_rows, prep_pars, [],
                      [_out_spec(D_DIL) for _ in range(3 * ng)])
    res = dict(perm=[], o=[], lse=[])
    for g, (window, r) in enumerate(DIL_GROUPS):
        qp, kp, vp = _perm(qkv[g], r), _perm(qkv[ng + g], r), _perm(qkv[2 * ng + g], r)
        rows = [_row_spec(qp), _row_spec(kp, prev='block'), _row_spec(vp, prev='block')]
        Rg = min(DIL_ROWS, S // r)
        (o, lse), _ = _seq_fwd("dil_attn_fwd", f_dil_attn, Rg, rows, [], [], [_out_spec(D_DIL), _out_spec(D_DIL)],
                               period=S // r // Rg)
        res['perm'].append((qp, kp, vp))
        res['o'].append(_unperm(o, r))
        res['lse'].append(_unperm(lse, r))
    mrows = [_row_spec(t) for t in res['o'] + res['lse']]
    (cat,), _ = _seq_fwd("dil_merge_fwd", f_dil_merge, _rows_of(S), mrows, [], [], [_out_spec(ncat, w=D_DIL, dt=BF)])
    res['prep'] = (prep_rows, prep_pars)
    return cat, res


def _dil_bwd(dtok, res, p, exchange=None, exchange_local=None):
    S = p.shape[0]
    mrows = [_row_spec(t) for t in res['o'] + res['lse']]
    dm, _ = _seq_bwd("dil_merge_bwd", f_dil_merge, _rows_of(S), mrows, [], [], [dtok], [])
    dq, dk, dv = [], [], []
    for g, (window, r) in enumerate(DIL_GROUPS):
        qp, kp, vp = res['perm'][g]
        rows = [_row_spec(qp), _row_spec(kp, prev='block'), _row_spec(vp, prev='block')]
        douts = [_row_spec(_perm(dm[g], r)), _row_spec(_perm(dm[3 + g], r))]
        Rg = min(DIL_ROWS, S // r)
        (a, b, c), _ = _seq_bwd("dil_attn_bwd", f_dil_attn, Rg, rows, [], [], douts, [], period=S // r // Rg)
        dq.append(_unperm(a, r)); dk.append(_unperm(b, r)); dv.append(_unperm(c, r))
    prep_rows, prep_pars = res['prep']
    res = _seq_bwd("dil_prep_bwd", f_dil_prep, _tile(S, (256, 128)), prep_rows, prep_pars, [],
                   [_row_spec(t) for t in dq + dk + dv], [], dx_dt=BF, exchange=exchange)
    (dp,), (dqn, dkn) = res[0], res[1]
    return dp, dict(b_q_norm=dqn.reshape(-1), b_k_norm=dkn.reshape(-1)), (res[2] if exchange else exchange_local)


def _ffn_specs(u, cw, cb):
    half = N_DEV // 2
    rows = [_row_spec(u, prev='halo', lb=(2, None), li=lambda jc: (0, jc))]
    pars = [_par_spec(cw, bs=(None, FFN_CONV, FF_SH), idx=lambda jc: (jc, 0, 0)),
            _par_spec(cw, bs=(None, FFN_CONV, FF_SH), idx=lambda jc: (jc + half, 0, 0)),
            _par_spec(cb, bs=(None, 1, FF_SH), idx=lambda jc: (jc, 0, 0)),
            _par_spec(cb, bs=(None, 1, FF_SH), idx=lambda jc: (jc + half, 0, 0))]
    return half, rows, pars


N_MIX = {0: 2 * GLA_HEADS * GLA_DK + 2 * D_MIX + GLA_RANK, 1: 3 * len(DIL_GROUPS) * D_DIL,
         2: 2 * D_MIX + 2 * SSM_GROUPS * SSM_STATE + SSM_HEADS, 3: 2 * HGRN_HEADS * HGRN_DK + 2 * D_MIX}
W_IN = {0: 'a_w_in', 1: 'b_w_in', 2: 'c_w_in', 3: 'd_w_in'}
W_OUT = {0: 'a_w_out', 1: 'b_w_out', 2: 'c_w_out', 3: 'd_w_out'}


def _in_blocks(name, t):
    return t if SHARD_AXIS[name] == 1 else t.reshape(1, N_DEV * t.shape[1], t.shape[2])


LAYER_STACKED = ('ffn_w_up', 'ffn_conv_w', 'ffn_w_down', 'xa_w_kv')


SMALL_OF_KIND = {0: ['a_w_gate2'], 2: ['c_conv_w']}


def _layer_names(i):
    return list(LAYER_STACKED) + [W_IN[i % 4], W_OUT[i % 4]] + SMALL_OF_KIND.get(i % 4, [])


def _device_step(x, mem, pos, sh, rep, target, distributed=True):
    S, D = x.shape
    w = dict(rep)
    posf = pos.reshape(S, 1).astype(F32)
    n_mix, w_in_name, w_out_name = N_MIX, W_IN, W_OUT
    ntot = {k: -(-(n_mix[k] + D_XA) // 256) * 256 for k in n_mix}
    mem_g = w['mem_norm'].reshape(1, -1)
    mem_n = _norm_fwd(mem, mem_g)
    R = _rows_of(S)

    def mine(i):
        return {n: (sh[n][i] if n in LAYER_STACKED else sh[n]) for n in _layer_names(i)}

    if distributed:
        gl = dict(zip(_layer_names(0), _gather_many(list(mine(0).values()), "gather_weights")))
    else:
        gl = {n: (sh[n][:, 0] if n in LAYER_STACKED else sh[n]) for n in _layer_names(0)}

    saved = []
    for i in range(DEPTH):
        kind = i % 4
        L = dict(x0=x)
        for n in SMALL_OF_KIND.get(kind, []):
            w[n] = _to_full(gl[n], 1)
        in_blocks = _in_blocks(w_in_name[kind], gl[w_in_name[kind]])
        w_out = (_to_full(gl[w_out_name[kind]], 1) if SHARD_AXIS[w_out_name[kind]] == 1
                 else gl[w_out_name[kind]].reshape(-1, D))
        nxt, push = {}, [[], [], []]
        if i + 1 < DEPTH:
            if distributed:
                nxt = mine(i + 1)
                push = [[n for n in nxt if n not in ('ffn_w_up', w_in_name[(i + 1) % 4], w_out_name[(i + 1) % 4])],
                        ['ffn_w_up'], [w_in_name[(i + 1) % 4], w_out_name[(i + 1) % 4]]]
            else:
                nxt = {n: (sh[n][:, i + 1] if n in LAYER_STACKED else sh[n]) for n in _layer_names(i + 1)}
        got = dict(nxt) if not distributed else {}

        def hosted(call, names):
            if not names:
                return call()
            res, arrived = call(gather=[nxt[n] for n in names])
            got.update(zip(names, arrived))
            return res

        g1 = w['mix_norm'][i].reshape(1, -1)
        h = _norm_fwd(x, g1)
        wcat = _cat_cols(in_blocks, n_mix[kind], ntot[kind])
        p = hosted(functools.partial(_matmul, h, wcat, name="matmul_in"), push[0])
        ntok = D_DIL if kind == 1 else D_MIX
        if kind == 1:
            cat, L['dil'] = _dil_fwd(p, posf, w, ntok + D_XA)
        else:
            f, Rm, rows, pars, sshapes, _ = _mixer_specs(kind, S, p, w)
            (cat,), L['states'] = _seq_fwd("mixer%d_fwd" % kind, f, Rm, rows, pars, sshapes,
                                           [_out_spec(ntok + D_XA, w=ntok, dt=BF)], save_states=True)
        wkv = gl['xa_w_kv'].reshape(D, 2 * D_XA)
        kv = _matmul(mem_n, wkv, name="matmul_kv")
        xa_rows = [_row_spec(p, w=D_XA, dn=D_XA)]
        xa_pars = [_par_spec(kv), _par_spec(w['xa_q_norm'][i].reshape(1, -1)), _par_spec(w['xa_k_norm'][i].reshape(1, -1))]
        (cat,), _ = _seq_fwd("xattn_fwd", f_xattn, R, xa_rows, xa_pars, [],
                             [_out_spec(ntok + D_XA, w=D_XA, c=lambda jc: ntok // D_XA, dt=BF)], out_alias={0: cat})
        x1 = _matmul(cat, w_out, add=x, name="matmul_out")
        g2 = w['ffn_norm'][i].reshape(1, -1)
        h2 = _norm_fwd(x1, g2)
        wup = gl['ffn_w_up']
        u = hosted(functools.partial(_ffn_up, h2, wup), push[1])
        cw, cb = gl['ffn_conv_w'], w['ffn_conv_b'][i].reshape(N_DEV, 1, FF_SH)
        nt, frows, fpars = _ffn_specs(u, cw, cb)
        res = _seq_fwd("ffn_act_fwd", f_ffn_act, min(R, FFN_ROWS), frows, fpars, [],
                       [_out_spec(FF_SH, dt=BF, ls=(nt,), lb=(None,), li=lambda jc: (jc,))], ncol=nt,
                       gather=[nxt[n] for n in push[2]] if push[2] else None)
        (act,) = res[0]
        if push[2]:
            got.update(zip(push[2], res[2]))
        wd = gl['ffn_w_down'].reshape(D_FF, D)
        x = _ffn_down(act, wd, x1)
        L.update(h=h, p=p, wcat=wcat, kv=kv, wkv=wkv, cat=cat, x1=x1, h2=h2, u=u, act=act, wd=wd, wup=wup, cw=cw, g1=g1,
                 g2=g2, in_blocks=in_blocks, w_out=w_out, shapes={n: t.shape for n, t in gl.items()})
        saved.append(L)
        gl = got

    dx, dxb, loss = _loss_head(x, target)

    G = {}
    d_mem_n = None
    acc = {k: [None] * DEPTH for k in ('mix_norm', 'ffn_norm', 'ffn_conv_b', 'xa_q_norm', 'xa_k_norm')}
    parts = [{} for _ in range(DEPTH)]
    pending = {}
    half = N_DEV // 2

    def sent(call, blocks, layer):
        if not blocks:
            return call()
        if not distributed:
            parts[layer].update(blocks)
            return call()
        res, arrived = call(exchange=list(blocks.values()))
        parts[layer].update(zip(blocks, arrived))
        return res

    for i in reversed(range(DEPTH)):
        kind = i % 4
        L = saved[i]
        Gc = {}
        Gc['ffn_w_down'] = _ffn_dw_down(L['act'], dxb).reshape(N_DEV, D_FF // N_DEV, D)
        dact = _ffn_dact(dxb, L['wd'])
        cw, cb = L['cw'], w['ffn_conv_b'][i].reshape(N_DEV, 1, FF_SH)
        nt, frows, fpars = _ffn_specs(L['u'], cw, cb)
        (du,), (dwg, dwv, dbg, dbv) = _seq_bwd(
            "ffn_act_bwd", f_ffn_act, min(R, FFN_ROWS), frows, fpars, [], [_row_spec(dact, lb=(None,), li=lambda jc: (jc,))], [],
            ncol=nt, dx_dt=BF)
        Gc['ffn_conv_w'] = jnp.concatenate([dwg[:half], dwv[half:]], axis=0)
        acc['ffn_conv_b'][i] = jnp.concatenate([dbg[:half], dbv[half:]], axis=0).reshape(-1)
        Gc['ffn_w_up'] = _ffn_dw_up(L['h2'], du)
        dh2 = sent(functools.partial(_ffn_dh2, du, L['wup']), pending, i + 1)
        dx1, dx1b, dg2 = _norm_bwd(L['x1'], L['g2'], dh2, res=dx)
        acc['ffn_norm'][i] = dg2.reshape(-1)
        G_out = _matmul(L['cat'], dx1b, mode="tn", out_dtype=BF, name="matmul_dw_out")
        dcat = _matmul(dx1b, L['w_out'], mode="nt", name="matmul_dcat")
        ntok = D_DIL if kind == 1 else D_MIX
        dtok = _row_spec(dcat, w=ntok)
        dxa = _row_spec(dcat, w=D_XA, c=lambda jc: ntok // D_XA)
        p = L['p']
        up = [Gc.pop('ffn_w_up')]
        if kind == 1:
            dp, gm, got_up = _dil_bwd(dtok, L['dil'], p, up if distributed else None, up)
            G.update(gm)
        else:
            f, Rm, rows, pars, sshapes, _ = _mixer_specs(kind, S, p, w)
            res = _seq_bwd("mixer%d_bwd" % kind, f, Rm, rows, pars, sshapes, [dtok], L['states'], dx_dt=BF,
                           exchange=up if distributed else None)
            (dp,), dps, got_up = res[0], res[1], (res[2] if distributed else up)
            if kind == 0:
                Gc['a_w_gate2'], G['a_b_gate'], G['a_o_norm'] = _to_chunks(dps[0], 1), dps[1].reshape(-1), dps[2].reshape(-1)
            elif kind == 2:
                Gc['c_conv_w'] = _to_chunks(dps[0], 1)
                for nme, v in zip(('c_conv_b', 'c_dt_bias', 'c_a_log', 'c_d', 'c_norm'), dps[1:]):
                    G[nme] = v.reshape(-1)
            else:
                G['d_lower_bounds'], G['d_o_norm'] = dps[0], dps[1].reshape(-1)
        xa_rows = [_row_spec(p, w=D_XA)]
        xa_pars = [_par_spec(L['kv']), _par_spec(w['xa_q_norm'][i].reshape(1, -1)), _par_spec(w['xa_k_norm'][i].reshape(1, -1))]
        (dp,), (dkv, dqn, dkn) = _seq_bwd("xattn_bwd", f_xattn, R, xa_rows, xa_pars, [], [dxa], [], dx_dt=BF,
                                          dx_alias={0: dp})
        acc['xa_q_norm'][i], acc['xa_k_norm'][i] = dqn.reshape(-1), dkn.reshape(-1)
        Gc['xa_w_kv'] = _matmul(mem_n, dkv, mode="tn", out_dtype=BF, name="matmul_dw_kv").reshape(
            N_DEV, D // N_DEV, 2 * D_XA)
        d_mem_n = _matmul(dkv, L['wkv'], mode="nt", add=d_mem_n, name="matmul_dmem" + ("" if d_mem_n is None else "_acc"))
        parts[i]['ffn_w_up'] = got_up[0]
        dwcat = _matmul(L['h'], dp, mode="tn", out_dtype=BF, name="matmul_dw_in")
        blocks = L['in_blocks']
        Gc[w_in_name[kind]] = _uncat_cols(dwcat, blocks.shape[0], blocks.shape[2], n_mix[kind]).reshape(
            L['shapes'][w_in_name[kind]])
        Gc[w_out_name[kind]] = (_to_chunks(G_out, 1) if SHARD_AXIS[w_out_name[kind]] == 1
                                else G_out.reshape(L['shapes'][w_out_name[kind]]))
        dh = sent(functools.partial(_matmul, dp, L['wcat'], mode="nt", name="matmul_dh"),
                  {n: Gc.pop(n) for n in ('ffn_w_down', 'ffn_conv_w')}, i)
        dx, dxb, dg1 = _norm_bwd(L['x0'], L['g1'], dh, res=dx1)
        acc['mix_norm'][i] = dg1.reshape(-1)
        pending = Gc

    if distributed:
        names = list(pending)
        parts[0].update(zip(names, _exchange_many([pending[n] for n in names], "exchange_grads")))
    else:
        parts[0].update(pending)
    _, _, dmg = _norm_bwd(mem, mem_g, d_mem_n)
    G['mem_norm'] = dmg.reshape(-1)
    for k, v in acc.items():
        G[k] = jnp.stack(v)
    got = {}
    for i in range(DEPTH):
        for n, t in parts[i].items():
            if n not in LAYER_STACKED:
                got[n] = t
    for n in LAYER_STACKED:
        got[n] = jnp.stack([parts[i][n] for i in range(DEPTH)], axis=1)
    return loss, dx, got, G


def kernel(x, mem, positions, mem_norm, mix_norm, xa_w_kv, xa_q_norm, xa_k_norm, ffn_norm, ffn_w_up, ffn_conv_w, ffn_conv_b, ffn_w_down, a_w_in, a_w_gate2, a_b_gate, a_o_norm, a_w_out, b_w_in, b_q_norm, b_k_norm, b_w_out, c_w_in, c_conv_w, c_conv_b, c_dt_bias, c_a_log, c_d, c_norm, c_w_out, d_w_in, d_lower_bounds, d_o_norm, d_w_out, loss_target, m_mem_norm, m_mix_norm, m_xa_w_kv, m_xa_q_norm, m_xa_k_norm, m_ffn_norm, m_ffn_w_up, m_ffn_conv_w, m_ffn_conv_b, m_ffn_w_down, m_a_w_in, m_a_w_gate2, m_a_b_gate, m_a_o_norm, m_a_w_out, m_b_w_in, m_b_q_norm, m_b_k_norm, m_b_w_out, m_c_w_in, m_c_conv_w, m_c_conv_b, m_c_dt_bias, m_c_a_log, m_c_d, m_c_norm, m_c_w_out, m_d_w_in, m_d_lower_bounds, m_d_o_norm, m_d_w_out, v_mem_norm, v_mix_norm, v_xa_w_kv, v_xa_q_norm, v_xa_k_norm, v_ffn_norm, v_ffn_w_up, v_ffn_conv_w, v_ffn_conv_b, v_ffn_w_down, v_a_w_in, v_a_w_gate2, v_a_b_gate, v_a_o_norm, v_a_w_out, v_b_w_in, v_b_q_norm, v_b_k_norm, v_b_w_out, v_c_w_in, v_c_conv_w, v_c_conv_b, v_c_dt_bias, v_c_a_log, v_c_d, v_c_norm, v_c_w_out, v_d_w_in, v_d_lower_bounds, v_d_o_norm, v_d_w_out):
    args = locals()
    w = {n: args[n] for n in WEIGHTS}
    m = {n: args['m_' + n] for n in WEIGHTS}
    v = {n: args['v_' + n] for n in WEIGHTS}

    big = [n for n in SHARDED if w[n].size >= 65536]
    small = [n for n in SHARDED if n not in big]
    sh = {n: (w[n].astype(BF) if n in big else w[n]) for n in SHARDED}
    loss, grad_x, parts, G = _device_step(x[0], mem[0], positions[0], sh, {n: w[n] for n in REPLICATED}, loss_target[0])
    loss = lax.psum(loss, ("x", "y", "c"))
    rep_parts = _all_gather(_pack([G[n] for n in REPLICATED], F32), "gather_replicated_grads")

    out = {}

    def put(names, res, shapes):
        for kind, r in zip(("grad", "delta", "new_m", "new_v"), res):
            for n, t in zip(names, _unpack(r, shapes)):
                out[kind + "_" + n] = t

    for n in big:
        shp = tuple(w[n].shape)
        two_d = (math.prod(shp[:-1]), shp[-1])
        res = _adamw(parts[n].reshape((N_DEV,) + two_d), w[n].reshape(two_d), m[n].reshape(two_d), v[n].reshape(two_d),
                     "adamw")
        for kind, r in zip(("grad", "delta", "new_m", "new_v"), res):
            out[kind + "_" + n] = r.reshape(shp)
    for names, prt, tag in ((small, _pack_lead([parts[n] for n in small], F32), "adamw_small"),
                            (REPLICATED, rep_parts, "adamw_replicated")):
        res = _adamw(prt, _pack([w[n] for n in names], F32), _pack([m[n] for n in names], F32),
                     _pack([v[n] for n in names], F32), tag)
        put(names, res, [tuple(w[n].shape) for n in names])
    return (loss, grad_x[None], *[out["grad_" + n] for n in WEIGHTS], *[out["delta_" + n] for n in WEIGHTS],
            *[out["new_m_" + n] for n in WEIGHTS], *[out["new_v_" + n] for n in WEIGHTS])
```
